```python
import math
import jax, jax.numpy as jnp
from jax import lax
import numpy as np


D_MODEL = 1024
BATCH = 8
SEQ = 4096
DEPTH = 1

MIX_WIDTH = D_MODEL
NSA_WIDTH = MIX_WIDTH // 2
NSA_HEAD_DIM = 64
NSA_HEADS = NSA_WIDTH // NSA_HEAD_DIM
NSA_KV_HEADS = 2
NSA_GROUP = NSA_HEADS // NSA_KV_HEADS
NSA_KV_WIDTH = NSA_KV_HEADS * NSA_HEAD_DIM
CMP_STRIDE = 16
CMP_BLOCK = 2 * CMP_STRIDE
CMP_HIDDEN = 4 * NSA_HEAD_DIM
SEL_BLOCK = 64
N_SELECT = 16
WINDOW = 512
Q_BLOCK = 128
FORCED_SCORE = 1.0e4
MLSTM_WIDTH = MIX_WIDTH - NSA_WIDTH
MLSTM_HEADS = 4
MLSTM_V_DIM = MLSTM_WIDTH // MLSTM_HEADS
MLSTM_QK_DIM = MLSTM_V_DIM // 2
MLSTM_QK_WIDTH = MLSTM_HEADS * MLSTM_QK_DIM
MLSTM_CHUNK = 64
CONV_WIDTH = 4
REL_BUCKETS = 32
REL_MAX_DISTANCE = 128
MEM_TOKENS = 256
XATTN_HEADS = 4
XATTN_HEAD_DIM = D_MODEL // XATTN_HEADS
D_FF = -(-8 * D_MODEL // (3 * 256)) * 256
NORM_EPS = 1e-6
NEG_INF = -1.0e30

IN_SIZES = (NSA_WIDTH,) + (NSA_KV_WIDTH,) * 6 + (NSA_HEADS * 3, MLSTM_QK_WIDTH, MLSTM_QK_WIDTH,
            MLSTM_WIDTH, MLSTM_HEADS, MLSTM_HEADS, MLSTM_WIDTH)
IN_WIDTH = sum(IN_SIZES)
IN_OFFSETS = tuple(int(o) for o in np.cumsum(IN_SIZES)[:-1])

kernel_name = 'hymba_nsa_mlstm_hybrid_layer'


def rms_norm(x, gain):
    xf = x.astype(jnp.float32)
    y = xf * lax.rsqrt(jnp.mean(xf * xf, axis=-1, keepdims=True) + NORM_EPS)
    return (y * gain.astype(jnp.float32)).astype(x.dtype)


def rel_bucket(dist):
    n = jnp.maximum(dist, 0)
    max_exact = REL_BUCKETS // 2
    nf = jnp.maximum(n, 1).astype(jnp.float32)
    large = max_exact + (jnp.log(nf / max_exact) / math.log(REL_MAX_DISTANCE / max_exact)
                         * (REL_BUCKETS - max_exact)).astype(jnp.int32)
    large = jnp.minimum(large, REL_BUCKETS - 1)
    return jnp.where(n < max_exact, n, large)


def masked_softmax(s, mask):
    s = jnp.where(mask, s.astype(jnp.float32), NEG_INF)
    return jnp.where(mask, jax.nn.softmax(s, axis=-1), 0.0)


def compress_blocks(kv, pos, w1, w2):
    B, H, S, DH = kv.shape
    c = kv.reshape(B, H, S // CMP_STRIDE, CMP_STRIDE, DH)
    blocks = jnp.concatenate([c[:, :, :-1], c[:, :, 1:]], axis=3) + pos
    hid = jax.nn.silu(blocks.reshape(B, H, -1, CMP_BLOCK * DH) @ w1)
    return hid @ w2


def native_sparse_attention(q, k_cmp, v_cmp, k_slc, v_slc, k_win, v_win, gates, rel_bias):
    B, HKV, G, S, DH = q.shape
    n_cmp = k_cmp.shape[2]
    n_sel = S // SEL_BLOCK
    n_top = min(N_SELECT, n_sel)
    n_kwin = Q_BLOCK + WINDOW
    cmp_start = jnp.arange(n_cmp) * CMP_STRIDE
    cmp_end = cmp_start + (CMP_BLOCK - 1)
    sel_start = jnp.arange(n_sel) * SEL_BLOCK
    overlap = ((cmp_start[:, None] <= sel_start[None, :] + SEL_BLOCK - 1)
               & (cmp_end[:, None] >= sel_start[None, :])).astype(jnp.float32)
    k_blk = k_slc.reshape(B, HKV, n_sel, SEL_BLOCK, DH)
    v_blk = v_slc.reshape(B, HKV, n_sel, SEL_BLOCK, DH)
    k_pad = jnp.pad(k_win, ((0, 0), (0, 0), (WINDOW, 0), (0, 0)))
    v_pad = jnp.pad(v_win, ((0, 0), (0, 0), (WINDOW, 0), (0, 0)))
    gather = jax.vmap(jax.vmap(lambda blocks, idx: blocks[idx]))
    head_off = (jnp.arange(NSA_HEADS) * REL_BUCKETS).reshape(HKV, G)
    bias_flat = rel_bias.reshape(-1)
    blk_ids = jnp.arange(n_sel)

    def head_bias(dist):
        return rel_bias[:, rel_bucket(dist)].reshape(HKV, G, *dist.shape)

    def query_block(c):
        t0 = c * Q_BLOCK
        tq = t0 + jnp.arange(Q_BLOCK)
        qc = lax.dynamic_slice_in_dim(q, t0, Q_BLOCK, axis=3)
        gc = lax.dynamic_slice_in_dim(gates, t0, Q_BLOCK, axis=3)
        d_c = tq[:, None] - cmp_end[None, :]
        s_c = jnp.einsum('bhgtd,bhcd->bhgtc', qc, k_cmp).astype(jnp.float32) + head_bias(d_c)
        p_c = masked_softmax(s_c, d_c >= 0)
        o_c = jnp.einsum('bhgtc,bhcd->bhgtd', p_c.astype(v_cmp.dtype), v_cmp)
        imp = jnp.einsum('bhgtc,cj->bhtj', p_c, overlap)
        cur = tq // SEL_BLOCK
        forced = ((blk_ids[None, :] == 0) | (blk_ids[None, :] == cur[:, None])
                  | (blk_ids[None, :] == cur[:, None] - 1))
        score = jnp.where(forced, FORCED_SCORE, jnp.where(blk_ids[None, :] <= cur[:, None], imp, -1.0))
        _, idx = lax.top_k(score, n_top)
        k_s = gather(k_blk, idx).reshape(B, HKV, Q_BLOCK, n_top * SEL_BLOCK, DH)
        v_s = gather(v_blk, idx).reshape(B, HKV, Q_BLOCK, n_top * SEL_BLOCK, DH)
        kpos = (idx[..., None] * SEL_BLOCK + jnp.arange(SEL_BLOCK)).reshape(B, HKV, Q_BLOCK, n_top * SEL_BLOCK)
        d_s = tq[:, None] - kpos
        b_s = bias_flat[head_off[None, :, :, None, None] + rel_bucket(d_s)[:, :, None]]
        s_s = jnp.einsum('bhgtd,bhtkd->bhgtk', qc, k_s).astype(jnp.float32) + b_s
        p_s = masked_softmax(s_s, (d_s >= 0)[:, :, None])
        o_s = jnp.einsum('bhgtk,bhtkd->bhgtd', p_s.astype(v_s.dtype), v_s)
        k_w = lax.dynamic_slice_in_dim(k_pad, t0, n_kwin, axis=2)
        v_w = lax.dynamic_slice_in_dim(v_pad, t0, n_kwin, axis=2)
        kpos_w = t0 - WINDOW + jnp.arange(n_kwin)
        d_w = tq[:, None] - kpos_w[None, :]
        m_w = (d_w >= 0) & (d_w < WINDOW) & (kpos_w[None, :] >= 0)
        s_w = jnp.einsum('bhgtd,bhkd->bhgtk', qc, k_w).astype(jnp.float32) + head_bias(d_w)
        p_w = masked_softmax(s_w, m_w)
        o_w = jnp.einsum('bhgtk,bhkd->bhgtd', p_w.astype(v_w.dtype), v_w)
        return gc[..., 0:1] * o_c + gc[..., 1:2] * o_s + gc[..., 2:3] * o_w

    return lax.map(query_block, jnp.arange(S // Q_BLOCK))


def causal_depthwise_conv(u, w, b):
    y = lax.conv_general_dilated(u, w[:, None, :].astype(u.dtype), window_strides=(1,),
                                 padding=((CONV_WIDTH - 1, 0),), dimension_numbers=('NWC', 'WIO', 'NWC'),
                                 feature_group_count=u.shape[-1])
    return y + b


def mlstm_chunkwise(q, k, v, i_pre, log_f):
    B, S, H, DK = q.shape
    DV = v.shape[-1]
    L = MLSTM_CHUNK
    nc = S // L
    f32 = jnp.float32

    def chunks(t):
        return t.astype(f32).reshape(B, nc, L, H, -1).transpose(1, 0, 3, 2, 4)

    qs, ks, vs = chunks(q), chunks(k) * (DK ** -0.5), chunks(v)
    ig = chunks(i_pre[..., None])[..., 0]
    lf = chunks(log_f[..., None])[..., 0]
    causal = jnp.tril(jnp.ones((L, L), dtype=bool))

    def step(carry, inp):
        C, n, m = carry
        qc, kc, vc, ic, fc = inp
        b = jnp.cumsum(fc, axis=-1)
        D = jnp.where(causal, b[..., :, None] - b[..., None, :] + ic[..., None, :], NEG_INF)
        m_inter = b + m[..., None]
        m_t = jnp.maximum(m_inter, jnp.max(D, axis=-1))
        w = jnp.exp(D - m_t[..., None])
        a = jnp.exp(m_inter - m_t)
        s = jnp.einsum('bhtd,bhsd->bhts', qc, kc) * w
        num = jnp.einsum('bhts,bhsv->bhtv', s, vc) + a[..., None] * jnp.einsum('bhtd,bhdv->bhtv', qc, C)
        den = jnp.sum(s, axis=-1) + a * jnp.einsum('bhtd,bhd->bht', qc, n)
        h = num / jnp.maximum(jnp.abs(den), jnp.exp(-m_t))[..., None]
        bL = b[..., -1]
        ws = bL[..., None] - b + ic
        m_next = jnp.maximum(bL + m, jnp.max(ws, axis=-1))
        decay = jnp.exp(bL + m - m_next)
        kw = kc * jnp.exp(ws - m_next[..., None])[..., None]
        C = decay[..., None, None] * C + jnp.einsum('bhsd,bhsv->bhdv', kw, vc)
        n = decay[..., None] * n + jnp.sum(kw, axis=2)
        return (C, n, m_next), h

    init = (jnp.zeros((B, H, DK, DV), f32), jnp.zeros((B, H, DK), f32), jnp.zeros((B, H), f32))
    _, hs = lax.scan(step, init, (qs, ks, vs, ig, lf))
    return hs.transpose(1, 0, 3, 2, 4).reshape(B, S, H, DV)


def hybrid_mixer(h, rel_bias, w_in, cmp_pos_k, cmp_pos_v, cmp_w1_k, cmp_w2_k, cmp_w1_v, cmp_w2_v,
                 conv_w, conv_b, mlstm_gate_bias, mlstm_norm, w_out):
    B, S, _ = h.shape
    (nq, kc, vc, ksl, vsl, kwn, vwn, gt, mq, mk, mv, mi, mf, mo) = jnp.split(h @ w_in, IN_OFFSETS, axis=-1)

    def kv_heads(t):
        return t.reshape(B, S, NSA_KV_HEADS, NSA_HEAD_DIM).transpose(0, 2, 1, 3)

    q = nq.reshape(B, S, NSA_KV_HEADS, NSA_GROUP, NSA_HEAD_DIM).transpose(0, 2, 3, 1, 4) * (NSA_HEAD_DIM ** -0.5)
    k_cmp = compress_blocks(kv_heads(kc), cmp_pos_k, cmp_w1_k, cmp_w2_k)
    v_cmp = compress_blocks(kv_heads(vc), cmp_pos_v, cmp_w1_v, cmp_w2_v)
    gates = jax.nn.sigmoid(gt.reshape(B, S, NSA_KV_HEADS, NSA_GROUP, 3).transpose(0, 2, 3, 1, 4))
    o = native_sparse_attention(q, k_cmp, v_cmp, kv_heads(ksl), kv_heads(vsl), kv_heads(kwn), kv_heads(vwn),
                                gates, rel_bias)
    y_nsa = o.transpose(1, 0, 4, 2, 3, 5).reshape(B, S, NSA_WIDTH)

    qk = jax.nn.silu(causal_depthwise_conv(jnp.concatenate([mq, mk], axis=-1), conv_w, conv_b))
    mq, mk = jnp.split(qk, 2, axis=-1)
    hm = mlstm_chunkwise(mq.reshape(B, S, MLSTM_HEADS, MLSTM_QK_DIM), mk.reshape(B, S, MLSTM_HEADS, MLSTM_QK_DIM),
                         mv.reshape(B, S, MLSTM_HEADS, MLSTM_V_DIM),
                         mi.astype(jnp.float32) + mlstm_gate_bias[0].astype(jnp.float32),
                         jax.nn.log_sigmoid(mf.astype(jnp.float32) + mlstm_gate_bias[1].astype(jnp.float32)))
    hm = hm * lax.rsqrt(jnp.mean(hm * hm, axis=-1, keepdims=True) + NORM_EPS)
    hm = (hm.reshape(B, S, MLSTM_WIDTH) * mlstm_norm.astype(jnp.float32)).astype(h.dtype)
    y_ml = jax.nn.sigmoid(mo) * hm

    return jnp.concatenate([y_nsa, y_ml], axis=-1) @ w_out


def memory_cross_attention(h, mem_n, w_xq, w_xkv, w_xo):
    B, S, _ = h.shape
    M = mem_n.shape[1]
    q = (h @ w_xq).reshape(B, S, XATTN_HEADS, XATTN_HEAD_DIM) * (XATTN_HEAD_DIM ** -0.5)
    k, v = jnp.split(mem_n @ w_xkv, 2, axis=-1)
    k = k.reshape(B, M, XATTN_HEADS, XATTN_HEAD_DIM)
    v = v.reshape(B, M, XATTN_HEADS, XATTN_HEAD_DIM)
    p = jax.nn.softmax(jnp.einsum('bshd,bmhd->bhsm', q, k).astype(jnp.float32), axis=-1)
    o = jnp.einsum('bhsm,bmhd->bshd', p.astype(v.dtype), v).reshape(B, S, D_MODEL)
    return o @ w_xo


def swiglu_ffn(h, w_gate_up, w_down):
    g, u = jnp.split(h @ w_gate_up, 2, axis=-1)
    return (jax.nn.silu(g) * u) @ w_down


def setup_inputs(seed: int = 0) -> dict:
    key = jax.random.key(seed)
    keys = iter(jax.random.split(key, 40))

    def nrm(shape, scale):
        return scale * jax.random.normal(next(keys), shape, jnp.float32)

    def gain(width=D_MODEL):
        return 1.0 + nrm((DEPTH, width), 0.02)

    inputs = {}
    inputs['x'] = nrm((BATCH, SEQ, D_MODEL), 1.0)
    inputs['mem'] = nrm((BATCH, MEM_TOKENS, D_MODEL), 1.0)
    inputs['rel_bias'] = nrm((NSA_HEADS, REL_BUCKETS), 0.5)
    inputs['mix_norm_pre'] = gain()
    inputs['w_in'] = nrm((DEPTH, D_MODEL, IN_WIDTH), D_MODEL ** -0.5)
    inputs['cmp_pos_k'] = nrm((DEPTH, CMP_BLOCK, NSA_HEAD_DIM), 0.02)
    inputs['cmp_pos_v'] = nrm((DEPTH, CMP_BLOCK, NSA_HEAD_DIM), 0.02)
    inputs['cmp_w1_k'] = nrm((DEPTH, CMP_BLOCK * NSA_HEAD_DIM, CMP_HIDDEN), (CMP_BLOCK * NSA_HEAD_DIM) ** -0.5)
    inputs['cmp_w2_k'] = nrm((DEPTH, CMP_HIDDEN, NSA_HEAD_DIM), CMP_HIDDEN ** -0.5)
    inputs['cmp_w1_v'] = nrm((DEPTH, CMP_BLOCK * NSA_HEAD_DIM, CMP_HIDDEN), (CMP_BLOCK * NSA_HEAD_DIM) ** -0.5)
    inputs['cmp_w2_v'] = nrm((DEPTH, CMP_HIDDEN, NSA_HEAD_DIM), CMP_HIDDEN ** -0.5)
    inputs['conv_w'] = nrm((DEPTH, CONV_WIDTH, 2 * MLSTM_QK_WIDTH), CONV_WIDTH ** -0.5)
    inputs['conv_b'] = nrm((DEPTH, 2 * MLSTM_QK_WIDTH), 0.01)
    forget_bias = jnp.broadcast_to(jnp.linspace(3.0, 6.0, MLSTM_HEADS), (DEPTH, MLSTM_HEADS))
    inputs['mlstm_gate_bias'] = jnp.stack([nrm((DEPTH, MLSTM_HEADS), 0.1),
                                           forget_bias + nrm((DEPTH, MLSTM_HEADS), 0.1)], axis=1)
    inputs['mlstm_norm'] = gain(MLSTM_WIDTH)
    inputs['w_out'] = nrm((DEPTH, MIX_WIDTH, D_MODEL), MIX_WIDTH ** -0.5)
    inputs['mix_norm_post'] = gain()
    inputs['xattn_norm_pre'] = gain()
    inputs['mem_norm'] = gain()
    inputs['w_xq'] = nrm((DEPTH, D_MODEL, D_MODEL), D_MODEL ** -0.5)
    inputs['w_xkv'] = nrm((DEPTH, D_MODEL, 2 * D_MODEL), D_MODEL ** -0.5)
    inputs['w_xo'] = nrm((DEPTH, D_MODEL, D_MODEL), D_MODEL ** -0.5)
    inputs['xattn_norm_post'] = gain()
    inputs['ffn_norm_pre'] = gain()
    inputs['w_gate_up'] = nrm((DEPTH, D_MODEL, 2 * D_FF), D_MODEL ** -0.5)
    inputs['w_down'] = nrm((DEPTH, D_FF, D_MODEL), D_FF ** -0.5)
    inputs['ffn_norm_post'] = gain()
    return inputs


def reference(x, mem, rel_bias, mix_norm_pre, w_in, cmp_pos_k, cmp_pos_v, cmp_w1_k, cmp_w2_k, cmp_w1_v,
              cmp_w2_v, conv_w, conv_b, mlstm_gate_bias, mlstm_norm, w_out, mix_norm_post, xattn_norm_pre,
              mem_norm, w_xq, w_xkv, w_xo, xattn_norm_post, ffn_norm_pre, w_gate_up, w_down, ffn_norm_post):
    for l in range(DEPTH):
        h = rms_norm(x, mix_norm_pre[l])
        y = hybrid_mixer(h, rel_bias, w_in[l], cmp_pos_k[l], cmp_pos_v[l], cmp_w1_k[l], cmp_w2_k[l],
                         cmp_w1_v[l], cmp_w2_v[l], conv_w[l], conv_b[l], mlstm_gate_bias[l], mlstm_norm[l], w_out[l])
        x = x + rms_norm(y, mix_norm_post[l])
        h = rms_norm(x, xattn_norm_pre[l])
        y = memory_cross_attention(h, rms_norm(mem, mem_norm[l]), w_xq[l], w_xkv[l], w_xo[l])
        x = x + rms_norm(y, xattn_norm_post[l])
        h = rms_norm(x, ffn_norm_pre[l])
        y = swiglu_ffn(h, w_gate_up[l], w_down[l])
        x = x + rms_norm(y, ffn_norm_post[l])
    return x
```

```python
import functools
import math

import numpy as np
import jax
import jax.numpy as jnp
from jax import lax
from jax.experimental import pallas as pl
from jax.experimental.pallas import tpu as pltpu

D_MODEL = 1024
NSA_WIDTH = 512
NSA_HEAD_DIM = 64
NSA_HEADS = 8
NSA_KV_HEADS = 2
NSA_GROUP = 4
NSA_KV_WIDTH = 128
CMP_STRIDE = 16
CMP_BLOCK = 32
CMP_HIDDEN = 256
SEL_BLOCK = 64
N_SELECT = 16
WINDOW = 512
Q_BLOCK = 128
FORCED_SCORE = 1.0e4
MLSTM_WIDTH = 512
MLSTM_HEADS = 4
MLSTM_V_DIM = 128
MLSTM_QK_DIM = 64
MLSTM_QK_WIDTH = 256
MLSTM_CHUNK = 64
CONV_WIDTH = 4
REL_BUCKETS = 32
REL_MAX_DISTANCE = 128
XATTN_HEADS = 4
XATTN_HEAD_DIM = 256
D_FF = 2816
NORM_EPS = 1e-6
NEG_INF = -1.0e30

IN_SIZES = (NSA_WIDTH,) + (NSA_KV_WIDTH,) * 6 + (NSA_HEADS * 3, MLSTM_QK_WIDTH, MLSTM_QK_WIDTH,
                                                 MLSTM_WIDTH, MLSTM_HEADS, MLSTM_HEADS, MLSTM_WIDTH)
IN_OFFSETS = tuple(int(o) for o in np.cumsum((0,) + IN_SIZES)[:-1])

LANES = 128
SUBLANES = 8
VMEM_LIMIT_BYTES = 56 * 1024 * 1024

N_GATE_ROWS = 32
F_TILE = 256


def _rms(x, gain):
    return x * lax.rsqrt(jnp.mean(x * x, axis=-1, keepdims=True) + NORM_EPS) * gain


def _nt(a, b):
    return lax.dot_general(a, b, (((1,), (1,)), ((), ())), preferred_element_type=jnp.float32)


def _mm(a, b):
    return jnp.dot(a, b, preferred_element_type=jnp.float32)


def _params(n_axes):
    return pltpu.CompilerParams(dimension_semantics=("arbitrary",) * n_axes,
                                vmem_limit_bytes=VMEM_LIMIT_BYTES)


def _full(shape):
    nd = len(shape)
    return pl.BlockSpec(shape, lambda *_: (0,) * nd)


TM_IN = 512
_TOK_GROUPS = (("kc", 128, jnp.float32), ("vc", 128, jnp.float32), ("ksl", 128, jnp.bfloat16),
               ("kwn", 128, jnp.bfloat16), ("mqk", 512, jnp.bfloat16), ("mv", 512, jnp.bfloat16),
               ("mo", 512, jnp.bfloat16), ("small", 128, jnp.float32))
_FEAT_GROUPS = (("qT", 512, jnp.bfloat16), ("vslT", 128, jnp.bfloat16), ("vwnT", 128, jnp.bfloat16),
                ("smallT", N_GATE_ROWS, jnp.float32))


def _in_proj_kernel(x_ref, g_ref, wtok_ref, wfeat_ref, *out_refs):
    h = _rms(x_ref[...], g_ref[...]).astype(jnp.bfloat16)
    n_tok = len(_TOK_GROUPS)
    off = 0
    for (name, width, dt), o_ref in zip(_TOK_GROUPS, out_refs[:n_tok]):
        o_ref[...] = _mm(h, wtok_ref[:, off:off + width]).astype(dt)
        off += width
    off = 0
    for (name, rows, dt), o_ref in zip(_FEAT_GROUPS, out_refs[n_tok:]):
        r = _nt(wfeat_ref[off:off + rows, :], h)
        if name == "qT":
            r = r * (NSA_HEAD_DIM ** -0.5)
        for j in range(TM_IN // LANES):
            o_ref[j] = r[:, j * LANES:(j + 1) * LANES].astype(dt)
        off += rows


def _in_proj(x2d, gain, w_tok, w_feat):
    T = x2d.shape[0]
    n_tok_cols = w_tok.shape[1]
    n_feat_rows = w_feat.shape[0]
    out_shape, out_specs = [], []
    for name, width, dt in _TOK_GROUPS:
        out_shape.append(jax.ShapeDtypeStruct((T, width), dt))
        out_specs.append(pl.BlockSpec((TM_IN, width), lambda i: (i, 0)))
    for name, rows, dt in _FEAT_GROUPS:
        out_shape.append(jax.ShapeDtypeStruct((T // LANES, rows, LANES), dt))
        out_specs.append(pl.BlockSpec((TM_IN // LANES, rows, LANES), lambda i: (i, 0, 0)))
    return pl.pallas_call(
        _in_proj_kernel,
        grid=(T // TM_IN,),
        in_specs=[pl.BlockSpec((TM_IN, D_MODEL), lambda i: (i, 0)),
                  _full((1, D_MODEL)),
                  _full((D_MODEL, n_tok_cols)),
                  _full((n_feat_rows, D_MODEL))],
        out_specs=out_specs,
        out_shape=out_shape,
        compiler_params=_params(1),
        name="in_proj",
    )(x2d, gain, w_tok, w_feat)


def _in_proj_weights(w_in):
    (nq, kc, vc, ksl, vsl, kwn, vwn, gt, mq, mk, mv, mi, mf, mo) = [
        w_in[:, o:o + s] for o, s in zip(IN_OFFSETS, IN_SIZES)]
    gt_r = gt.reshape(D_MODEL, NSA_KV_HEADS, NSA_GROUP, 3).transpose(0, 3, 1, 2).reshape(D_MODEL, 24)
    small = jnp.concatenate([gt_r, mi, mf], axis=1)
    small_pad = jnp.pad(small, ((0, 0), (0, LANES - N_GATE_ROWS)))
    w_tok = jnp.concatenate([kc, vc, ksl, kwn, mq, mk, mv, mo, small_pad], axis=1)
    w_feat = jnp.concatenate([nq, vsl, vwn, small], axis=1).T
    return w_tok.astype(jnp.bfloat16), w_feat.astype(jnp.bfloat16)


N_CHUNK_COLS = CMP_STRIDE * NSA_KV_WIDTH
N_HID2 = NSA_KV_HEADS * CMP_HIDDEN


def _compress_one(c, pos_ref, w1_ref, n_chunks):
    lo = _mm((c + pos_ref[0:1, :]).astype(jnp.bfloat16), w1_ref[0])
    hi = _mm((c + pos_ref[1:2, :]).astype(jnp.bfloat16), w1_ref[1])
    pre = lo + pltpu.roll(hi, n_chunks - 1, 0)
    return (pre * jax.nn.sigmoid(pre)).astype(jnp.bfloat16)


def _compress_kernel(kc_ref, vc_ref, posk_ref, posv_ref, w1k_ref, w1v_ref, w2k_ref, w2vT_ref,
                     kcmp_ref, vcmpT_ref):
    n_chunks = kc_ref.shape[0]
    hid_k = _compress_one(kc_ref[...], posk_ref, w1k_ref, n_chunks)
    kcmp = _mm(hid_k, w2k_ref[...])
    row = lax.broadcasted_iota(jnp.int32, kcmp.shape, 0)
    kcmp_ref[...] = jnp.where(row < n_chunks - 1, kcmp, 0.0).astype(kcmp_ref.dtype)
    hid_v = _compress_one(vc_ref[...], posv_ref, w1v_ref, n_chunks)
    vcmpT = _nt(w2vT_ref[...], hid_v)
    col = lax.broadcasted_iota(jnp.int32, vcmpT.shape, 1)
    vcmpT_ref[...] = jnp.where(col < n_chunks - 1, vcmpT, 0.0).astype(vcmpT_ref.dtype)


def _compress_weights(pos, w1, w2):
    eye = jnp.eye(NSA_KV_HEADS, dtype=w1.dtype)
    w1r = w1.reshape(2, CMP_STRIDE, NSA_HEAD_DIM, CMP_HIDDEN)
    w1e = jnp.einsum('atdj,hg->athdgj', w1r, eye).reshape(2, N_CHUNK_COLS, N_HID2)
    pos_e = jnp.broadcast_to(pos.reshape(2, CMP_STRIDE, 1, NSA_HEAD_DIM),
                             (2, CMP_STRIDE, NSA_KV_HEADS, NSA_HEAD_DIM)).reshape(2, N_CHUNK_COLS)
    w2e = jnp.einsum('jd,hg->hjgd', w2, eye).reshape(N_HID2, NSA_KV_WIDTH)
    return pos_e, w1e.astype(jnp.bfloat16), w2e.astype(jnp.bfloat16)


def _compress(kc, vc, cmp_pos_k, cmp_pos_v, cmp_w1_k, cmp_w2_k, cmp_w1_v, cmp_w2_v, B, S):
    n_chunks = S // CMP_STRIDE
    posk, w1k, w2k = _compress_weights(cmp_pos_k, cmp_w1_k, cmp_w2_k)
    posv, w1v, w2v = _compress_weights(cmp_pos_v, cmp_w1_v, cmp_w2_v)
    kc3 = kc.reshape(B, n_chunks, N_CHUNK_COLS)
    vc3 = vc.reshape(B, n_chunks, N_CHUNK_COLS)
    chunk_spec = pl.BlockSpec((None, n_chunks, N_CHUNK_COLS), lambda b: (b, 0, 0))
    return pl.pallas_call(
        _compress_kernel,
        grid=(B,),
        in_specs=[chunk_spec, chunk_spec,
                  _full((2, N_CHUNK_COLS)), _full((2, N_CHUNK_COLS)),
                  _full((2, N_CHUNK_COLS, N_HID2)), _full((2, N_CHUNK_COLS, N_HID2)),
                  _full((N_HID2, NSA_KV_WIDTH)), _full((NSA_KV_WIDTH, N_HID2))],
        out_specs=[pl.BlockSpec((None, n_chunks, NSA_KV_WIDTH), lambda b: (b, 0, 0)),
                   pl.BlockSpec((None, NSA_KV_WIDTH, n_chunks), lambda b: (b, 0, 0))],
        out_shape=[jax.ShapeDtypeStruct((B, n_chunks, NSA_KV_WIDTH), jnp.bfloat16),
                   jax.ShapeDtypeStruct((B, NSA_KV_WIDTH, n_chunks), jnp.bfloat16)],
        compiler_params=_params(1),
        name="compress",
    )(kc3, vc3, posk, posv, w1k, w1v, w2k, w2v.T)


CMP_TAB_ROWS = 512
CMP_TAB_ZERO = 248


def _bucket_np(dist):
    n = np.maximum(dist, 0)
    max_exact = REL_BUCKETS // 2
    nf = np.maximum(n, 1).astype(np.float64)
    large = max_exact + (np.log(nf / max_exact) / math.log(REL_MAX_DISTANCE / max_exact)
                         * (REL_BUCKETS - max_exact)).astype(np.int64)
    large = np.minimum(large, REL_BUCKETS - 1)
    return np.where(n < max_exact, n, large).astype(np.int32)


def _bias_index_tables():
    m = np.arange(Q_BLOCK)[:, None]
    r = np.arange(Q_BLOCK)[None, :]
    diag = _bucket_np(r - m)
    off = _bucket_np(Q_BLOCK + r - m)
    jp = np.arange(CMP_TAB_ROWS)[:, None] - CMP_TAB_ZERO
    d_c = r - CMP_STRIDE * jp - (CMP_BLOCK - 1)
    cmp_idx = np.where(d_c >= 0, _bucket_np(d_c), -1).astype(np.int32)
    return diag, off, cmp_idx


def _bias_tables_kernel(rb_ref, diag_idx_ref, off_idx_ref, cmp_idx_ref,
                        diag_ref, off_ref, const_ref, cmp_ref):
    def lookup(idx, head, init):
        acc = jnp.full(idx.shape, init, jnp.float32)
        for k in range(REL_BUCKETS):
            acc = jnp.where(idx == k, rb_ref[head, k], acc)
        return acc

    for h in range(NSA_KV_HEADS):
        for g in range(NSA_GROUP):
            head = h * NSA_GROUP + g
            sl = slice(g * Q_BLOCK, (g + 1) * Q_BLOCK)
            diag_ref[h, :, sl] = lookup(diag_idx_ref[...], head, 0.0)
            off_ref[h, :, sl] = lookup(off_idx_ref[...], head, 0.0)
            cmp_ref[h, :, sl] = lookup(cmp_idx_ref[...], head, NEG_INF)
            const_ref[h, :, sl] = jnp.full((SUBLANES, Q_BLOCK), rb_ref[head, REL_BUCKETS - 1], jnp.float32)


def _bias_tables(rel_bias):
    diag_idx, off_idx, cmp_idx = _bias_index_tables()
    W = NSA_GROUP * Q_BLOCK
    return pl.pallas_call(
        _bias_tables_kernel,
        in_specs=[pl.BlockSpec(memory_space=pltpu.SMEM),
                  _full(diag_idx.shape), _full(off_idx.shape), _full(cmp_idx.shape)],
        out_specs=[_full((NSA_KV_HEADS, Q_BLOCK, W)), _full((NSA_KV_HEADS, Q_BLOCK, W)),
                   _full((NSA_KV_HEADS, SUBLANES, W)), _full((NSA_KV_HEADS, CMP_TAB_ROWS, W))],
        out_shape=[jax.ShapeDtypeStruct((NSA_KV_HEADS, Q_BLOCK, W), jnp.float32),
                   jax.ShapeDtypeStruct((NSA_KV_HEADS, Q_BLOCK, W), jnp.float32),
                   jax.ShapeDtypeStruct((NSA_KV_HEADS, SUBLANES, W), jnp.float32),
                   jax.ShapeDtypeStruct((NSA_KV_HEADS, CMP_TAB_ROWS, W), jnp.float32)],
        grid=(1,),
        compiler_params=_params(1),
        name="bias_tables",
    )(rel_bias, jnp.asarray(diag_idx), jnp.asarray(off_idx), jnp.asarray(cmp_idx))


GQ = NSA_GROUP * Q_BLOCK
TINY = 1e-30


def _overlap_np(n_cmp_rows, n_sel):
    cmp_start = np.arange(n_cmp_rows) * CMP_STRIDE
    cmp_end = cmp_start + CMP_BLOCK - 1
    sel_start = np.arange(n_sel) * SEL_BLOCK
    ov = ((cmp_start[None, :] <= sel_start[:, None] + SEL_BLOCK - 1)
          & (cmp_end[None, :] >= sel_start[:, None])).astype(np.float32)
    ov[:, n_cmp_rows - 1] = 0.0
    return ov


def _tile4(a):
    return jnp.concatenate([a] * NSA_GROUP, axis=1)


def _softmax_step(s, maskf, v_lhs, carry):
    m_i, l_i, acc = carry
    msk = maskf > 0.5
    s = jnp.where(msk, s, NEG_INF)
    m_new = jnp.maximum(m_i, jnp.max(s, axis=0, keepdims=True))
    alpha = jnp.exp(m_i - m_new)
    p = jnp.where(msk, jnp.exp(s - m_new), 0.0)
    l_new = alpha * l_i + jnp.sum(p, axis=0, keepdims=True)
    acc_new = alpha * acc + _mm(v_lhs, p.astype(jnp.bfloat16))
    return m_new, l_new, acc_new


def _nsa_kernel(q_ref, kcmp_ref, vcmpT_ref, ksl_ref, vslT_ref, kwn_ref, vwnT_ref, gate_ref,
                diag_ref, off_ref, const_ref, cmptab_ref, ovl_ref, out_ref, score_ref, selm_ref):
    h = pl.program_id(1)
    c = pl.program_id(2)
    n_cmp = kcmp_ref.shape[0]
    n_sel = ovl_ref.shape[0]
    n_top = min(N_SELECT, n_sel)
    f32 = jnp.float32

    q = q_ref[...]
    qcat = jnp.concatenate([q[g * NSA_HEAD_DIM:(g + 1) * NSA_HEAD_DIM, :] for g in range(NSA_GROUP)],
                           axis=1)
    zq = jnp.zeros_like(qcat)
    first = jnp.concatenate([qcat, zq], axis=0)
    second = jnp.concatenate([zq, qcat], axis=0)
    hsel = (jnp.zeros((NSA_KV_WIDTH, GQ), jnp.int32) + h) == 0
    qpad = jnp.where(hsel, first, second)

    tab = cmptab_ref[pl.ds(pl.multiple_of(CMP_TAB_ZERO - (Q_BLOCK // CMP_STRIDE) * c, SUBLANES), n_cmp), :]
    s = _mm(kcmp_ref[...], qpad) + tab
    valid = tab > 0.5 * NEG_INF
    m = jnp.max(s, axis=0, keepdims=True)
    p = jnp.where(valid, jnp.exp(s - m), 0.0)
    l = jnp.sum(p, axis=0, keepdims=True)
    pn = p * (1.0 / jnp.maximum(l, TINY))
    o_c = _mm(vcmpT_ref[...], pn.astype(jnp.bfloat16))
    psum = pn[:, 0:Q_BLOCK]
    for g in range(1, NSA_GROUP):
        psum = psum + pn[:, g * Q_BLOCK:(g + 1) * Q_BLOCK]
    imp = jnp.dot(ovl_ref[...], psum, precision=lax.Precision.HIGHEST,
                  preferred_element_type=f32)

    j_io = lax.broadcasted_iota(jnp.int32, (n_sel, Q_BLOCK), 0)
    r_io = lax.broadcasted_iota(jnp.int32, (n_sel, Q_BLOCK), 1)
    cur = (Q_BLOCK // SEL_BLOCK) * c + (r_io >= SEL_BLOCK).astype(jnp.int32)
    forced = (j_io == 0) | (j_io == cur) | (j_io == cur - 1)
    score = jnp.where(forced, FORCED_SCORE, jnp.where(j_io <= cur, imp, -1.0))
    score_ref[...] = score
    n_grp = n_sel // SUBLANES
    grp = [score[SUBLANES * v:SUBLANES * (v + 1), :] for v in range(n_grp)]
    cnt = [jnp.zeros((SUBLANES, Q_BLOCK), jnp.int32) for _ in range(n_grp)]
    sub_io = lax.broadcasted_iota(jnp.int32, (SUBLANES, Q_BLOCK), 0)
    for jp in range(n_sel):
        row = score_ref[jp:jp + 1, :]
        for v in range(n_grp):
            if SUBLANES * v > jp:
                inc = (row >= grp[v]).astype(jnp.int32)
            elif SUBLANES * (v + 1) - 1 < jp:
                inc = (row > grp[v]).astype(jnp.int32)
            else:
                tie = (sub_io > jp - SUBLANES * v).astype(jnp.int32)
                inc = jnp.where(row > grp[v], 1, jnp.where(row == grp[v], tie, 0))
            cnt[v] = cnt[v] + inc
    for v in range(n_grp):
        selm_ref[SUBLANES * v:SUBLANES * (v + 1), :] = (cnt[v] < n_top).astype(f32)

    m_io = lax.broadcasted_iota(jnp.int32, (Q_BLOCK, Q_BLOCK), 0)
    q_io = lax.broadcasted_iota(jnp.int32, (Q_BLOCK, Q_BLOCK), 1)
    causal_f = _tile4((q_io >= m_io).astype(f32))
    anti_f = _tile4((q_io < m_io).astype(f32))
    ones_f = jnp.ones((Q_BLOCK, GQ), f32)
    cbias = const_ref[0:1, :]
    init = (jnp.full((1, GQ), NEG_INF, f32), jnp.zeros((1, GQ), f32),
            jnp.zeros((NSA_HEAD_DIM, GQ), f32))

    def key_tile(k_ref, kt):
        return k_ref[pl.ds(pl.multiple_of(kt * Q_BLOCK, Q_BLOCK), Q_BLOCK), :]

    def sel_mask(kt):
        lo = selm_ref[pl.ds(2 * kt, 1), :]
        hi = selm_ref[pl.ds(2 * kt + 1, 1), :]
        return _tile4(jnp.where(m_io < SEL_BLOCK, lo, hi))

    def sel_tile(kt, bias, extra_f, carry):
        s_t = _mm(key_tile(ksl_ref, kt), qpad) + bias
        return _softmax_step(s_t, sel_mask(kt) * extra_f, vslT_ref[kt], carry)

    carry = lax.fori_loop(0, jnp.maximum(c - 1, 0),
                          lambda kt, cr: sel_tile(kt, cbias, ones_f, cr), init)
    has_prev = (c >= 1).astype(f32)
    carry = sel_tile(jnp.maximum(c - 1, 0), off_ref[...], ones_f * has_prev, carry)
    _, l_s, acc_s = sel_tile(c, diag_ref[...], causal_f, carry)
    o_s = acc_s * (1.0 / jnp.maximum(l_s, TINY))

    carry = init
    n_back = WINDOW // Q_BLOCK
    for back in range(n_back, -1, -1):
        kt = jnp.maximum(c - back, 0)
        present = (c >= back).astype(f32)
        if back == 0:
            bias, mf = diag_ref[...], causal_f
        elif back == 1:
            bias, mf = off_ref[...], ones_f * present
        elif back == n_back:
            bias, mf = cbias, anti_f * present
        else:
            bias, mf = cbias, ones_f * present
        s_t = _mm(key_tile(kwn_ref, kt), qpad) + bias
        carry = _softmax_step(s_t, mf, vwnT_ref[kt], carry)
    _, l_w, acc_w = carry
    o_w = acc_w * (1.0 / jnp.maximum(l_w, TINY))

    ys = []
    for g in range(NSA_GROUP):
        sl = slice(g * Q_BLOCK, (g + 1) * Q_BLOCK)
        gates = [jax.nn.sigmoid(gate_ref[pl.ds(kind * NSA_HEADS + h * NSA_GROUP + g, 1), :])
                 for kind in range(3)]
        ys.append(gates[0] * o_c[:, sl] + gates[1] * o_s[:, sl] + gates[2] * o_w[:, sl])
    yT = jnp.concatenate(ys, axis=0)
    for half in range(2):
        out_ref[:, half * LANES:(half + 1) * LANES] = (
            yT[half * LANES:(half + 1) * LANES, :].T.astype(out_ref.dtype))


def _nsa(qT, kcmp, vcmpT, ksl, vslT, kwn, vwnT, smallT, tables, B, S):
    nq = S // Q_BLOCK
    n_cmp = S // CMP_STRIDE
    n_sel = S // SEL_BLOCK
    diag, off, const, cmptab = tables
    ovl = jnp.asarray(_overlap_np(n_cmp, n_sel))
    ksl3 = ksl.reshape(B, S, NSA_KV_WIDTH)
    kwn3 = kwn.reshape(B, S, NSA_KV_WIDTH)
    vslT4 = vslT.reshape(B, nq, NSA_KV_WIDTH, Q_BLOCK)
    vwnT4 = vwnT.reshape(B, nq, NSA_KV_WIDTH, Q_BLOCK)
    k_spec = pl.BlockSpec((None, S, NSA_KV_WIDTH), lambda b, h, c: (b, 0, 0))
    vT_spec = pl.BlockSpec((None, nq, NSA_HEAD_DIM, Q_BLOCK), lambda b, h, c: (b, 0, h, 0))
    tab_spec = lambda rows: pl.BlockSpec((None, rows, GQ), lambda b, h, c: (h, 0, 0))
    half_w = NSA_GROUP * NSA_HEAD_DIM
    return pl.pallas_call(
        _nsa_kernel,
        grid=(B, NSA_KV_HEADS, nq),
        in_specs=[pl.BlockSpec((None, half_w, Q_BLOCK), lambda b, h, c: (b * nq + c, h, 0)),
                  pl.BlockSpec((None, n_cmp, NSA_KV_WIDTH), lambda b, h, c: (b, 0, 0)),
                  pl.BlockSpec((None, NSA_HEAD_DIM, n_cmp), lambda b, h, c: (b, h, 0)),
                  k_spec, vT_spec, k_spec, vT_spec,
                  pl.BlockSpec((None, N_GATE_ROWS, Q_BLOCK), lambda b, h, c: (b * nq + c, 0, 0)),
                  tab_spec(Q_BLOCK), tab_spec(Q_BLOCK), tab_spec(SUBLANES), tab_spec(CMP_TAB_ROWS),
                  pl.BlockSpec((n_sel, n_cmp), lambda b, h, c: (0, 0))],
        out_specs=pl.BlockSpec((Q_BLOCK, half_w), lambda b, h, c: (b * nq + c, h)),
        out_shape=jax.ShapeDtypeStruct((B * S, NSA_WIDTH), jnp.bfloat16),
        scratch_shapes=[pltpu.VMEM((n_sel, Q_BLOCK), jnp.float32),
                        pltpu.VMEM((n_sel, Q_BLOCK), jnp.float32)],
        compiler_params=_params(3),
        name="nsa",
    )(qT, kcmp, vcmpT, ksl3, vslT4, kwn3, vwnT4, smallT, diag, off, const, cmptab, ovl)


CONV_TILE = 256
I_COL = 24
F_COL = 28


def _log_sigmoid(x):
    return jnp.minimum(x, 0.0) - jnp.log(1.0 + jnp.exp(-jnp.abs(x)))


def _mlstm_kernel(mqk_ref, mv_ref, mo_ref, small_ref, convw_ref, convb_ref, gbias_ref, norm_ref,
                  out_ref, qk_ref, c_ref, n_ref, m_ref):
    S = mqk_ref.shape[0]
    L = MLSTM_CHUNK
    f32 = jnp.float32
    bf16 = jnp.bfloat16
    kscale_row = jnp.where(lax.broadcasted_iota(jnp.int32, (1, 2 * MLSTM_QK_WIDTH), 1) < MLSTM_QK_WIDTH,
                           1.0, MLSTM_QK_DIM ** -0.5)

    def conv_body(i, _):
        t0 = pl.multiple_of(i * CONV_TILE, CONV_TILE)
        cur = mqk_ref[pl.ds(t0, CONV_TILE), :].astype(f32)
        prev_start = pl.multiple_of(jnp.maximum(t0 - 16, 0), 16)
        prev = mqk_ref[pl.ds(prev_start, 16), :].astype(f32)[8:16, :] * (i > 0).astype(f32)
        ext = jnp.concatenate([prev, cur], axis=0)
        y = convb_ref[...]
        for j in range(CONV_WIDTH):
            lo = SUBLANES - (CONV_WIDTH - 1) + j
            y = y + convw_ref[j:j + 1, :] * ext[lo:lo + CONV_TILE, :]
        y = y * jax.nn.sigmoid(y) * kscale_row
        qk_ref[pl.ds(t0, CONV_TILE), :] = y.astype(bf16)
        return 0

    lax.fori_loop(0, S // CONV_TILE, conv_body, 0)

    c_ref[...] = jnp.zeros_like(c_ref)
    n_ref[...] = jnp.zeros_like(n_ref)
    m_ref[...] = jnp.zeros_like(m_ref)

    t_io = lax.broadcasted_iota(jnp.int32, (L, L), 0)
    s_io = lax.broadcasted_iota(jnp.int32, (L, L), 1)
    causal = s_io <= t_io
    tri = causal.astype(f32)

    def chunk_body(k, _):
        t0 = pl.multiple_of(k * L, L)
        a = small_ref[pl.ds(t0, L), :] + gbias_ref[...]
        lf = _log_sigmoid(a)
        bcum = jnp.dot(tri, lf, precision=lax.Precision.HIGHEST, preferred_element_type=f32)
        aT = a.T
        bcumT = bcum.T
        qk = qk_ref[pl.ds(t0, L), :]
        for hh in range(MLSTM_HEADS):
            i_c = a[:, I_COL + hh:I_COL + hh + 1]
            b_c = bcum[:, F_COL + hh:F_COL + hh + 1]
            i_r = aT[I_COL + hh:I_COL + hh + 1, :]
            b_r = bcumT[F_COL + hh:F_COL + hh + 1, :]
            m_prev = m_ref[hh:hh + 1, 0:1]
            q = qk[:, hh * MLSTM_QK_DIM:(hh + 1) * MLSTM_QK_DIM]
            kk = qk[:, MLSTM_QK_WIDTH + hh * MLSTM_QK_DIM:MLSTM_QK_WIDTH + (hh + 1) * MLSTM_QK_DIM]
            v = mv_ref[pl.ds(t0, L), hh * MLSTM_V_DIM:(hh + 1) * MLSTM_V_DIM]
            c_prev = c_ref[hh]
            n_prev = n_ref[hh:hh + 1, :]

            dmat = jnp.where(causal, b_c - b_r + i_r, NEG_INF)
            m_inter = b_c + m_prev
            m_t = jnp.maximum(m_inter, jnp.max(dmat, axis=1, keepdims=True))
            w = jnp.exp(dmat - m_t)
            a_t = jnp.exp(m_inter - m_t)
            s = _nt(q, kk) * w
            num = _mm(s.astype(bf16), v) + a_t * _mm(q, c_prev.astype(bf16))
            den = (jnp.sum(s, axis=1, keepdims=True)
                   + a_t * jnp.sum(q.astype(f32) * n_prev, axis=1, keepdims=True))
            hval = num / jnp.maximum(jnp.abs(den), jnp.exp(-m_t))

            b_last = b_c[L - 1:L, :]
            ws = b_last - b_c + i_c
            m_next = jnp.maximum(b_last + m_prev, jnp.max(ws, axis=0, keepdims=True))
            decay = jnp.exp(b_last + m_prev - m_next)
            kw = kk.astype(f32) * jnp.exp(ws - m_next)
            c_ref[hh] = decay * c_prev + _mm(kw.T.astype(bf16), v)
            n_ref[hh:hh + 1, :] = decay * n_prev + jnp.sum(kw, axis=0, keepdims=True)
            m_ref[hh:hh + 1, :] = jnp.broadcast_to(m_next, (1, LANES))

            sl = slice(hh * MLSTM_V_DIM, (hh + 1) * MLSTM_V_DIM)
            hn = hval * lax.rsqrt(jnp.mean(hval * hval, axis=1, keepdims=True) + NORM_EPS) * norm_ref[:, sl]
            gate = jax.nn.sigmoid(mo_ref[pl.ds(t0, L), sl].astype(f32))
            out_ref[pl.ds(t0, L), sl] = (gate * hn).astype(out_ref.dtype)
        return 0

    lax.fori_loop(0, S // L, chunk_body, 0)


def _mlstm(mqk, mv, mo, small, conv_w, conv_b, gate_bias, mlstm_norm, B, S):
    gbias_row = jnp.zeros((1, LANES), jnp.float32)
    gbias_row = gbias_row.at[0, I_COL:I_COL + MLSTM_HEADS].set(gate_bias[0].astype(jnp.float32))
    gbias_row = gbias_row.at[0, F_COL:F_COL + MLSTM_HEADS].set(gate_bias[1].astype(jnp.float32))
    seq = lambda w: pl.BlockSpec((S, w), lambda b: (b, 0))
    return pl.pallas_call(
        _mlstm_kernel,
        grid=(B,),
        in_specs=[seq(2 * MLSTM_QK_WIDTH), seq(MLSTM_WIDTH), seq(MLSTM_WIDTH), seq(LANES),
                  _full((CONV_WIDTH, 2 * MLSTM_QK_WIDTH)), _full((1, 2 * MLSTM_QK_WIDTH)),
                  _full((1, LANES)), _full((1, MLSTM_WIDTH))],
        out_specs=seq(MLSTM_WIDTH),
        out_shape=jax.ShapeDtypeStruct((B * S, MLSTM_WIDTH), jnp.bfloat16),
        scratch_shapes=[pltpu.VMEM((S, 2 * MLSTM_QK_WIDTH), jnp.bfloat16),
                        pltpu.VMEM((MLSTM_HEADS, MLSTM_QK_DIM, MLSTM_V_DIM), jnp.float32),
                        pltpu.VMEM((SUBLANES, MLSTM_QK_DIM), jnp.float32),
                        pltpu.VMEM((SUBLANES, LANES), jnp.float32)],
        compiler_params=_params(1),
        name="mlstm",
    )(mqk, mv, mo, small, conv_w, conv_b.reshape(1, -1), gbias_row, mlstm_norm.reshape(1, -1))


def _mem_kv_kernel(mem_ref, g_ref, w_ref, k_ref, v_ref):
    mn = _rms(mem_ref[...], g_ref[...]).astype(jnp.bfloat16)
    k_ref[...] = _mm(mn, w_ref[:, :D_MODEL]).astype(k_ref.dtype)
    v_ref[...] = _mm(mn, w_ref[:, D_MODEL:]).astype(v_ref.dtype)


def _mem_kv(mem, gain, w_xkv):
    B, M, _ = mem.shape
    spec = pl.BlockSpec((None, M, D_MODEL), lambda b: (b, 0, 0))
    return pl.pallas_call(
        _mem_kv_kernel,
        grid=(B,),
        in_specs=[spec, _full((1, D_MODEL)), _full((D_MODEL, 2 * D_MODEL))],
        out_specs=[spec, spec],
        out_shape=[jax.ShapeDtypeStruct((B, M, D_MODEL), jnp.bfloat16)] * 2,
        compiler_params=_params(1),
        name="mem_kv",
    )(mem, gain, w_xkv.astype(jnp.bfloat16))


TM_X = 512


def _mix_xattn_kernel(ynsa_ref, yml_ref, x_ref, wout_ref, gpost_ref, gpre_ref, wq_ref, k_ref, v_ref,
                      wo_ref, gpost2_ref, out_ref):
    bf16 = jnp.bfloat16
    y = _mm(ynsa_ref[...], wout_ref[:NSA_WIDTH, :]) + _mm(yml_ref[...], wout_ref[NSA_WIDTH:, :])
    x1 = x_ref[...] + _rms(y, gpost_ref[...])
    h2 = _rms(x1, gpre_ref[...]).astype(bf16)
    q = (_mm(h2, wq_ref[...]) * (XATTN_HEAD_DIM ** -0.5)).astype(bf16)
    outs = []
    for hh in range(XATTN_HEADS):
        sl = slice(hh * XATTN_HEAD_DIM, (hh + 1) * XATTN_HEAD_DIM)
        s = _nt(q[:, sl], k_ref[:, sl])
        p = jnp.exp(s - jnp.max(s, axis=1, keepdims=True))
        l = jnp.sum(p, axis=1, keepdims=True)
        outs.append((_mm(p.astype(bf16), v_ref[:, sl]) * (1.0 / l)).astype(bf16))
    o = jnp.concatenate(outs, axis=1)
    y2 = _mm(o, wo_ref[...])
    out_ref[...] = x1 + _rms(y2, gpost2_ref[...])


def _mix_xattn(ynsa, yml, x2d, w_out, g_post, g_pre, w_xq, kx, vx, w_xo, g_post2, B, S):
    nt = S // TM_X
    M = kx.shape[1]
    tok = lambda w: pl.BlockSpec((TM_X, w), lambda b, i: (b * nt + i, 0))
    mem_spec = pl.BlockSpec((None, M, D_MODEL), lambda b, i: (b, 0, 0))
    sq = _full((D_MODEL, D_MODEL))
    row = _full((1, D_MODEL))
    bf = lambda w: w.astype(jnp.bfloat16)
    return pl.pallas_call(
        _mix_xattn_kernel,
        grid=(B, nt),
        in_specs=[tok(NSA_WIDTH), tok(MLSTM_WIDTH), tok(D_MODEL), sq, row, row, sq, mem_spec, mem_spec,
                  sq, row],
        out_specs=tok(D_MODEL),
        out_shape=jax.ShapeDtypeStruct((B * S, D_MODEL), jnp.float32),
        compiler_params=_params(2),
        name="mix_xattn",
    )(ynsa, yml, x2d, bf(w_out), g_post, g_pre, bf(w_xq), kx, vx, bf(w_xo), g_post2)


TM_F = 512


def _ffn_kernel(x_ref, gpre_ref, wg_ref, wu_ref, wd_ref, gpost_ref, out_ref, acc_ref):
    bf16 = jnp.bfloat16
    x = x_ref[...]
    h = _rms(x, gpre_ref[...]).astype(bf16)
    acc_ref[...] = jnp.zeros_like(acc_ref)

    def body(j, _):
        g = _mm(h, wg_ref[j])
        u = _mm(h, wu_ref[j])
        act = (g * jax.nn.sigmoid(g) * u).astype(bf16)
        acc_ref[...] += _mm(act, wd_ref[j])
        return 0

    lax.fori_loop(0, wg_ref.shape[0], body, 0)
    out_ref[...] = x + _rms(acc_ref[...], gpost_ref[...])


def _ffn(x2d, g_pre, w_gate_up, w_down, g_post):
    T = x2d.shape[0]
    nf = D_FF // F_TILE
    wg = w_gate_up[:, :D_FF].reshape(D_MODEL, nf, F_TILE).transpose(1, 0, 2).astype(jnp.bfloat16)
    wu = w_gate_up[:, D_FF:].reshape(D_MODEL, nf, F_TILE).transpose(1, 0, 2).astype(jnp.bfloat16)
    wd = w_down.reshape(nf, F_TILE, D_MODEL).astype(jnp.bfloat16)
    tok = pl.BlockSpec((TM_F, D_MODEL), lambda i: (i, 0))
    row = _full((1, D_MODEL))
    return pl.pallas_call(
        _ffn_kernel,
        grid=(T // TM_F,),
        in_specs=[tok, row, _full((nf, D_MODEL, F_TILE)), _full((nf, D_MODEL, F_TILE)),
                  _full((nf, F_TILE, D_MODEL)), row],
        out_specs=tok,
        out_shape=jax.ShapeDtypeStruct((T, D_MODEL), jnp.float32),
        scratch_shapes=[pltpu.VMEM((TM_F, D_MODEL), jnp.float32)],
        compiler_params=_params(1),
        name="ffn",
    )(x2d, g_pre, wg, wu, wd, g_post)


def _layer(x, mem, rel_bias, mix_norm_pre, w_in, cmp_pos_k, cmp_pos_v, cmp_w1_k, cmp_w2_k, cmp_w1_v,
           cmp_w2_v, conv_w, conv_b, mlstm_gate_bias, mlstm_norm, w_out, mix_norm_post, xattn_norm_pre,
           mem_norm, w_xq, w_xkv, w_xo, xattn_norm_post, ffn_norm_pre, w_gate_up, w_down, ffn_norm_post):
    B, S, _ = x.shape
    row = lambda g: g.reshape(1, -1).astype(jnp.float32)
    x2d = x.reshape(B * S, D_MODEL)
    w_tok, w_feat = _in_proj_weights(w_in)
    (kc, vc, ksl, kwn, mqk, mv, mo, small, qT, vslT, vwnT, smallT) = _in_proj(
        x2d, row(mix_norm_pre), w_tok, w_feat)
    kcmp, vcmpT = _compress(kc, vc, cmp_pos_k, cmp_pos_v, cmp_w1_k, cmp_w2_k, cmp_w1_v, cmp_w2_v, B, S)
    tables = _bias_tables(rel_bias.astype(jnp.float32))
    ynsa = _nsa(qT, kcmp, vcmpT, ksl, vslT, kwn, vwnT, smallT, tables, B, S)
    yml = _mlstm(mqk, mv, mo, small, conv_w, conv_b, mlstm_gate_bias, mlstm_norm, B, S)
    kx, vx = _mem_kv(mem, row(mem_norm), w_xkv)
    x2 = _mix_xattn(ynsa, yml, x2d, w_out, row(mix_norm_post), row(xattn_norm_pre), w_xq, kx, vx, w_xo,
                    row(xattn_norm_post), B, S)
    x3 = _ffn(x2, row(ffn_norm_pre), w_gate_up, w_down, row(ffn_norm_post))
    return x3.reshape(B, S, D_MODEL)


def kernel(x, mem, rel_bias, mix_norm_pre, w_in, cmp_pos_k, cmp_pos_v, cmp_w1_k, cmp_w2_k, cmp_w1_v, cmp_w2_v,
           conv_w, conv_b, mlstm_gate_bias, mlstm_norm, w_out, mix_norm_post, xattn_norm_pre, mem_norm, w_xq,
           w_xkv, w_xo, xattn_norm_post, ffn_norm_pre, w_gate_up, w_down, ffn_norm_post):
    depth = w_in.shape[0]
    for l in range(depth):
        x = _layer(x, mem, rel_bias, mix_norm_pre[l], w_in[l], cmp_pos_k[l], cmp_pos_v[l], cmp_w1_k[l],
                   cmp_w2_k[l], cmp_w1_v[l], cmp_w2_v[l], conv_w[l], conv_b[l], mlstm_gate_bias[l],
                   mlstm_norm[l], w_out[l], mix_norm_post[l], xattn_norm_pre[l], mem_norm[l], w_xq[l],
                   w_xkv[l], w_xo[l], xattn_norm_post[l], ffn_norm_pre[l], w_gate_up[l], w_down[l],
                   ffn_norm_post[l])
    return x
```

```python
import functools
import math

import numpy as np
import jax
import jax.numpy as jnp
from jax import lax
from jax.experimental import pallas as pl
from jax.experimental.pallas import tpu as pltpu

D_MODEL = 1024
NSA_WIDTH = 512
NSA_HEAD_DIM = 64
NSA_HEADS = 8
NSA_KV_HEADS = 2
NSA_GROUP = 4
NSA_KV_WIDTH = 128
CMP_STRIDE = 16
CMP_BLOCK = 32
CMP_HIDDEN = 256
SEL_BLOCK = 64
N_SELECT = 16
WINDOW = 512
Q_BLOCK = 128
FORCED_SCORE = 1.0e4
MLSTM_WIDTH = 512
MLSTM_HEADS = 4
MLSTM_V_DIM = 128
MLSTM_QK_DIM = 64
MLSTM_QK_WIDTH = 256
MLSTM_CHUNK = 64
CONV_WIDTH = 4
REL_BUCKETS = 32
REL_MAX_DISTANCE = 128
XATTN_HEADS = 4
XATTN_HEAD_DIM = 256
D_FF = 2816
NORM_EPS = 1e-6
NEG_INF = -1.0e30

IN_SIZES = (NSA_WIDTH,) + (NSA_KV_WIDTH,) * 6 + (NSA_HEADS * 3, MLSTM_QK_WIDTH, MLSTM_QK_WIDTH,
                                                 MLSTM_WIDTH, MLSTM_HEADS, MLSTM_HEADS, MLSTM_WIDTH)
IN_OFFSETS = tuple(int(o) for o in np.cumsum((0,) + IN_SIZES)[:-1])

LANES = 128
SUBLANES = 8
VMEM_LIMIT_BYTES = 56 * 1024 * 1024

N_GATE_ROWS = 32
F_TILE = 256


def _rms(x, gain):
    return x * lax.rsqrt(jnp.mean(x * x, axis=-1, keepdims=True) + NORM_EPS) * gain


def _nt(a, b):
    return lax.dot_general(a, b, (((1,), (1,)), ((), ())), preferred_element_type=jnp.float32)


def _mm(a, b):
    return jnp.dot(a, b, preferred_element_type=jnp.float32)


def _params(n_axes):
    return pltpu.CompilerParams(dimension_semantics=("arbitrary",) * n_axes,
                                vmem_limit_bytes=VMEM_LIMIT_BYTES)


def _full(shape):
    nd = len(shape)
    return pl.BlockSpec(shape, lambda *_: (0,) * nd)


TM_IN = 512
_TOK_GROUPS = (("kc", 128, jnp.float32), ("vc", 128, jnp.float32), ("ksl", 128, jnp.bfloat16),
               ("kwn", 128, jnp.bfloat16), ("mqk", 512, jnp.bfloat16), ("mv", 512, jnp.bfloat16),
               ("mo", 512, jnp.bfloat16), ("small", 128, jnp.float32))
_FEAT_GROUPS = (("qT", 512, jnp.bfloat16), ("vslT", 128, jnp.bfloat16), ("vwnT", 128, jnp.bfloat16),
                ("smallT", N_GATE_ROWS, jnp.float32))


def _in_proj_kernel(x_ref, g_ref, wtok_ref, wfeat_ref, *out_refs):
    h = _rms(x_ref[...], g_ref[...]).astype(jnp.bfloat16)
    n_tok = len(_TOK_GROUPS)
    off = 0
    for (name, width, dt), o_ref in zip(_TOK_GROUPS, out_refs[:n_tok]):
        o_ref[...] = _mm(h, wtok_ref[:, off:off + width]).astype(dt)
        off += width
    off = 0
    for (name, rows, dt), o_ref in zip(_FEAT_GROUPS, out_refs[n_tok:]):
        r = _nt(wfeat_ref[off:off + rows, :], h)
        if name == "qT":
            r = r * (NSA_HEAD_DIM ** -0.5)
        for j in range(TM_IN // LANES):
            o_ref[j] = r[:, j * LANES:(j + 1) * LANES].astype(dt)
        off += rows


def _in_proj(x2d, gain, w_tok, w_feat):
    T = x2d.shape[0]
    n_tok_cols = w_tok.shape[1]
    n_feat_rows = w_feat.shape[0]
    out_shape, out_specs = [], []
    for name, width, dt in _TOK_GROUPS:
        out_shape.append(jax.ShapeDtypeStruct((T, width), dt))
        out_specs.append(pl.BlockSpec((TM_IN, width), lambda i: (i, 0)))
    for name, rows, dt in _FEAT_GROUPS:
        out_shape.append(jax.ShapeDtypeStruct((T // LANES, rows, LANES), dt))
        out_specs.append(pl.BlockSpec((TM_IN // LANES, rows, LANES), lambda i: (i, 0, 0)))
    return pl.pallas_call(
        _in_proj_kernel,
        grid=(T // TM_IN,),
        in_specs=[pl.BlockSpec((TM_IN, D_MODEL), lambda i: (i, 0)),
                  _full((1, D_MODEL)),
                  _full((D_MODEL, n_tok_cols)),
                  _full((n_feat_rows, D_MODEL))],
        out_specs=out_specs,
        out_shape=out_shape,
        compiler_params=_params(1),
        name="in_proj",
    )(x2d, gain, w_tok, w_feat)


def _in_proj_weights(w_in):
    (nq, kc, vc, ksl, vsl, kwn, vwn, gt, mq, mk, mv, mi, mf, mo) = [
        w_in[:, o:o + s] for o, s in zip(IN_OFFSETS, IN_SIZES)]
    gt_r = gt.reshape(D_MODEL, NSA_KV_HEADS, NSA_GROUP, 3).transpose(0, 3, 1, 2).reshape(D_MODEL, 24)
    small = jnp.concatenate([gt_r, mi, mf], axis=1)
    small_pad = jnp.pad(small, ((0, 0), (0, LANES - N_GATE_ROWS)))
    w_tok = jnp.concatenate([kc, vc, ksl, kwn, mq, mk, mv, mo, small_pad], axis=1)
    w_feat = jnp.concatenate([nq, vsl, vwn, small], axis=1).T
    return w_tok.astype(jnp.bfloat16), w_feat.astype(jnp.bfloat16)


N_CHUNK_COLS = CMP_STRIDE * NSA_KV_WIDTH
N_HID2 = NSA_KV_HEADS * CMP_HIDDEN


def _compress_one(c, pos_ref, w1_ref, n_chunks):
    lo = _mm((c + pos_ref[0:1, :]).astype(jnp.bfloat16), w1_ref[0])
    hi = _mm((c + pos_ref[1:2, :]).astype(jnp.bfloat16), w1_ref[1])
    pre = lo + pltpu.roll(hi, n_chunks - 1, 0)
    return (pre * jax.nn.sigmoid(pre)).astype(jnp.bfloat16)


def _compress_kernel(kc_ref, vc_ref, posk_ref, posv_ref, w1k_ref, w1v_ref, w2k_ref, w2vT_ref,
                     kcmp_ref, vcmpT_ref):
    n_chunks = kc_ref.shape[0]
    hid_k = _compress_one(kc_ref[...], posk_ref, w1k_ref, n_chunks)
    kcmp = _mm(hid_k, w2k_ref[...])
    row = lax.broadcasted_iota(jnp.int32, kcmp.shape, 0)
    kcmp_ref[...] = jnp.where(row < n_chunks - 1, kcmp, 0.0).astype(kcmp_ref.dtype)
    hid_v = _compress_one(vc_ref[...], posv_ref, w1v_ref, n_chunks)
    vcmpT = _nt(w2vT_ref[...], hid_v)
    col = lax.broadcasted_iota(jnp.int32, vcmpT.shape, 1)
    vcmpT_ref[...] = jnp.where(col < n_chunks - 1, vcmpT, 0.0).astype(vcmpT_ref.dtype)


def _compress_weights(pos, w1, w2):
    eye = jnp.eye(NSA_KV_HEADS, dtype=w1.dtype)
    w1r = w1.reshape(2, CMP_STRIDE, NSA_HEAD_DIM, CMP_HIDDEN)
    w1e = jnp.einsum('atdj,hg->athdgj', w1r, eye).reshape(2, N_CHUNK_COLS, N_HID2)
    pos_e = jnp.broadcast_to(pos.reshape(2, CMP_STRIDE, 1, NSA_HEAD_DIM),
                             (2, CMP_STRIDE, NSA_KV_HEADS, NSA_HEAD_DIM)).reshape(2, N_CHUNK_COLS)
    w2e = jnp.einsum('jd,hg->hjgd', w2, eye).reshape(N_HID2, NSA_KV_WIDTH)
    return pos_e, w1e.astype(jnp.bfloat16), w2e.astype(jnp.bfloat16)


def _compress(kc, vc, cmp_pos_k, cmp_pos_v, cmp_w1_k, cmp_w2_k, cmp_w1_v, cmp_w2_v, B, S):
    n_chunks = S // CMP_STRIDE
    posk, w1k, w2k = _compress_weights(cmp_pos_k, cmp_w1_k, cmp_w2_k)
    posv, w1v, w2v = _compress_weights(cmp_pos_v, cmp_w1_v, cmp_w2_v)
    kc3 = kc.reshape(B, n_chunks, N_CHUNK_COLS)
    vc3 = vc.reshape(B, n_chunks, N_CHUNK_COLS)
    chunk_spec = pl.BlockSpec((None, n_chunks, N_CHUNK_COLS), lambda b: (b, 0, 0))
    return pl.pallas_call(
        _compress_kernel,
        grid=(B,),
        in_specs=[chunk_spec, chunk_spec,
                  _full((2, N_CHUNK_COLS)), _full((2, N_CHUNK_COLS)),
                  _full((2, N_CHUNK_COLS, N_HID2)), _full((2, N_CHUNK_COLS, N_HID2)),
                  _full((N_HID2, NSA_KV_WIDTH)), _full((NSA_KV_WIDTH, N_HID2))],
        out_specs=[pl.BlockSpec((None, n_chunks, NSA_KV_WIDTH), lambda b: (b, 0, 0)),
                   pl.BlockSpec((None, NSA_KV_WIDTH, n_chunks), lambda b: (b, 0, 0))],
        out_shape=[jax.ShapeDtypeStruct((B, n_chunks, NSA_KV_WIDTH), jnp.bfloat16),
                   jax.ShapeDtypeStruct((B, NSA_KV_WIDTH, n_chunks), jnp.bfloat16)],
        compiler_params=_params(1),
        name="compress",
    )(kc3, vc3, posk, posv, w1k, w1v, w2k, w2v.T)


GQ = NSA_GROUP * Q_BLOCK
TINY = 1e-30
CMP_TAB_ROWS = 512
CMP_TAB_ZERO = 248
CMP_TAB_LOOKUP = (232, 256)
KEY_SUPER = 512
SEL_TAB_ZERO = 640
SEL_TAB_ROWS = SEL_TAB_ZERO + KEY_SUPER
N_WIN_TILES = WINDOW // Q_BLOCK + 1


def _bucket_np(dist):
    n = np.maximum(dist, 0)
    max_exact = REL_BUCKETS // 2
    nf = np.maximum(n, 1).astype(np.float64)
    large = max_exact + (np.log(nf / max_exact) / math.log(REL_MAX_DISTANCE / max_exact)
                         * (REL_BUCKETS - max_exact)).astype(np.int64)
    large = np.minimum(large, REL_BUCKETS - 1)
    return np.where(n < max_exact, n, large).astype(np.int32)


def _bias_index_tables():
    m = np.arange(Q_BLOCK)[:, None]
    r = np.arange(Q_BLOCK)[None, :]
    diag = np.where(r - m >= 0, _bucket_np(r - m), -1).astype(np.int32)
    off = _bucket_np(Q_BLOCK + r - m)
    jp = np.arange(*CMP_TAB_LOOKUP)[:, None] - CMP_TAB_ZERO
    d_c = r - CMP_STRIDE * jp - (CMP_BLOCK - 1)
    cmp_idx = np.where(d_c >= 0, _bucket_np(d_c), -1).astype(np.int32)
    return diag, off, cmp_idx


def _bias_tables_kernel(rb_ref, diag_idx_ref, off_idx_ref, cmp_idx_ref, sel_ref, win_ref, cmp_ref):
    f32 = jnp.float32

    def lookup(idx, head):
        acc = jnp.full(idx.shape, NEG_INF, f32)
        for k in range(REL_BUCKETS):
            acc = jnp.where(idx == k, rb_ref[head, k], acc)
        return acc

    m_io = lax.broadcasted_iota(jnp.int32, (Q_BLOCK, Q_BLOCK), 0)
    r_io = lax.broadcasted_iota(jnp.int32, (Q_BLOCK, Q_BLOCK), 1)
    neg_tile = jnp.full((Q_BLOCK, Q_BLOCK), NEG_INF, f32)
    lo, hi = CMP_TAB_LOOKUP
    for h in range(NSA_KV_HEADS):
        for g in range(NSA_GROUP):
            head = h * NSA_GROUP + g
            sl = slice(g * Q_BLOCK, (g + 1) * Q_BLOCK)
            far = rb_ref[head, REL_BUCKETS - 1]
            far_tile = jnp.full((Q_BLOCK, Q_BLOCK), far, f32)
            diag_v = lookup(diag_idx_ref[...], head)
            off_v = lookup(off_idx_ref[...], head)
            n_far = (SEL_TAB_ZERO - Q_BLOCK) // Q_BLOCK
            for t in range(SEL_TAB_ROWS // Q_BLOCK):
                rows = slice(t * Q_BLOCK, (t + 1) * Q_BLOCK)
                tile = far_tile if t < n_far else off_v if t == n_far else diag_v if t == n_far + 1 else neg_tile
                sel_ref[h, rows, sl] = tile
            win_ref[h, 0, :, sl] = diag_v
            win_ref[h, 1, :, sl] = off_v
            for back in range(2, N_WIN_TILES - 1):
                win_ref[h, back, :, sl] = far_tile
            win_ref[h, N_WIN_TILES - 1, :, sl] = jnp.where(r_io < m_io, far, NEG_INF)
            win_ref[h, N_WIN_TILES, :, sl] = neg_tile
            cmp_ref[h, 0:lo, sl] = jnp.full((lo, Q_BLOCK), far, f32)
            cmp_ref[h, lo:hi, sl] = lookup(cmp_idx_ref[...], head)
            cmp_ref[h, hi:CMP_TAB_ROWS, sl] = jnp.full((CMP_TAB_ROWS - hi, Q_BLOCK), NEG_INF, f32)


def _bias_tables(rel_bias):
    diag_idx, off_idx, cmp_idx = _bias_index_tables()
    shapes = [(NSA_KV_HEADS, SEL_TAB_ROWS, GQ), (NSA_KV_HEADS, N_WIN_TILES + 1, Q_BLOCK, GQ),
              (NSA_KV_HEADS, CMP_TAB_ROWS, GQ)]
    return pl.pallas_call(
        _bias_tables_kernel,
        in_specs=[pl.BlockSpec(memory_space=pltpu.SMEM),
                  _full(diag_idx.shape), _full(off_idx.shape), _full(cmp_idx.shape)],
        out_specs=[_full(s) for s in shapes],
        out_shape=[jax.ShapeDtypeStruct(s, jnp.float32) for s in shapes],
        grid=(1,),
        compiler_params=_params(1),
        name="bias_tables",
    )(rel_bias, jnp.asarray(diag_idx), jnp.asarray(off_idx), jnp.asarray(cmp_idx))


def _overlap_np(n_cmp_rows, n_sel):
    cmp_start = np.arange(n_cmp_rows) * CMP_STRIDE
    cmp_end = cmp_start + CMP_BLOCK - 1
    sel_start = np.arange(n_sel) * SEL_BLOCK
    ov = ((cmp_start[None, :] <= sel_start[:, None] + SEL_BLOCK - 1)
          & (cmp_end[None, :] >= sel_start[:, None])).astype(np.float32)
    ov[:, n_cmp_rows - 1] = 0.0
    return ov


def _tile4(a):
    return jnp.concatenate([a] * NSA_GROUP, axis=1)


def _select_blocks(score, score_ref, n_top):
    n_sel = score.shape[0]
    score_ref[...] = score
    n_grp = n_sel // SUBLANES
    grp = [score[SUBLANES * v:SUBLANES * (v + 1), :] for v in range(n_grp)]
    cnt = [jnp.zeros((SUBLANES, Q_BLOCK), jnp.int32) for _ in range(n_grp)]
    sub_io = lax.broadcasted_iota(jnp.int32, (SUBLANES, Q_BLOCK), 0)
    for jp in range(n_sel):
        row = score_ref[jp:jp + 1, :]
        for v in range(n_grp):
            if SUBLANES * v > jp:
                inc = (row >= grp[v]).astype(jnp.int32)
            elif SUBLANES * (v + 1) - 1 < jp:
                inc = (row > grp[v]).astype(jnp.int32)
            else:
                tie = (sub_io > jp - SUBLANES * v).astype(jnp.int32)
                inc = jnp.where(row > grp[v], 1, jnp.where(row == grp[v], tie, 0))
            cnt[v] = cnt[v] + inc
    return [jnp.where(cnt[v] < n_top, 0.0, NEG_INF) for v in range(n_grp)]


def _nsa_kernel(q_ref, kcmp_ref, vcmpT_ref, ksl_ref, vslT_ref, kwn_ref, vwnT_ref, gate_ref,
                seltab_ref, wintab_ref, cmptab_ref, ovl_ref, out_ref, score_ref, selb_ref):
    c = pl.program_id(1)
    n_cmp = kcmp_ref.shape[0]
    n_sel = ovl_ref.shape[0]
    n_top = min(N_SELECT, n_sel)
    f32 = jnp.float32
    bf16 = jnp.bfloat16
    DH = NSA_HEAD_DIM
    heads = range(NSA_KV_HEADS)

    q = q_ref[...]
    zq = jnp.zeros((DH, GQ), bf16)
    qpad = []
    for h in heads:
        qcat = jnp.concatenate([q[(h * NSA_GROUP + g) * DH:(h * NSA_GROUP + g + 1) * DH, :]
                                for g in range(NSA_GROUP)], axis=1)
        qpad.append(jnp.concatenate([qcat, zq] if h == 0 else [zq, qcat], axis=0))

    cmp_off = pl.multiple_of(CMP_TAB_ZERO - (Q_BLOCK // CMP_STRIDE) * c, SUBLANES)
    kcmp = kcmp_ref[...]
    j_io = lax.broadcasted_iota(jnp.int32, (n_sel, Q_BLOCK), 0)
    r_io = lax.broadcasted_iota(jnp.int32, (n_sel, Q_BLOCK), 1)
    cur = (Q_BLOCK // SEL_BLOCK) * c + (r_io >= SEL_BLOCK).astype(jnp.int32)
    forced = (j_io == 0) | (j_io == cur) | (j_io == cur - 1)
    visible = j_io <= cur
    o_c = []
    for h in heads:
        tab = cmptab_ref[h, pl.ds(cmp_off, n_cmp), :]
        s = _mm(kcmp, qpad[h]) + tab
        m = jnp.max(s, axis=0, keepdims=True)
        p = jnp.where(tab > 0.5 * NEG_INF, jnp.exp(s - m), 0.0)
        l = jnp.sum(p, axis=0, keepdims=True)
        pn = p * (1.0 / jnp.maximum(l, TINY))
        o_c.append(_mm(vcmpT_ref[h * DH:(h + 1) * DH, :], pn.astype(bf16)))
        psum = pn[:, 0:Q_BLOCK]
        for g in range(1, NSA_GROUP):
            psum = psum + pn[:, g * Q_BLOCK:(g + 1) * Q_BLOCK]
        imp = jnp.dot(ovl_ref[...], psum, precision=lax.Precision.HIGHEST,
                      preferred_element_type=f32)
        score = jnp.where(forced, FORCED_SCORE, jnp.where(visible, imp, -1.0))
        rows = _select_blocks(score, score_ref.at[h], n_top)
        for v, blk in enumerate(rows):
            selb_ref[h, SUBLANES * v:SUBLANES * (v + 1), :] = blk

    blocks_per_step = KEY_SUPER // SEL_BLOCK
    tiles_per_step = KEY_SUPER // Q_BLOCK

    def sel_step(j, carry):
        kblk = ksl_ref[pl.ds(pl.multiple_of(j * KEY_SUPER, KEY_SUPER), KEY_SUPER), :]
        tab_off = pl.multiple_of(jnp.maximum(j * KEY_SUPER - c * Q_BLOCK + SEL_TAB_ZERO, 0), Q_BLOCK)
        new = []
        for h in heads:
            m_i, l_i, acc = carry[h]
            selb = jnp.concatenate(
                [jnp.broadcast_to(selb_ref[h, pl.ds(j * blocks_per_step + i, 1), :], (SEL_BLOCK, Q_BLOCK))
                 for i in range(blocks_per_step)], axis=0)
            s = _mm(kblk, qpad[h]) + seltab_ref[h, pl.ds(tab_off, KEY_SUPER), :] + _tile4(selb)
            m_new = jnp.maximum(m_i, jnp.max(s, axis=0, keepdims=True))
            alpha = jnp.exp(m_i - m_new)
            p = jnp.exp(s - m_new)
            l_new = alpha * l_i + jnp.sum(p, axis=0, keepdims=True)
            vT = jnp.concatenate([vslT_ref[j * tiles_per_step + i, h * DH:(h + 1) * DH, :]
                                  for i in range(tiles_per_step)], axis=1)
            new.append((m_new, l_new, alpha * acc + _mm(vT, p.astype(bf16))))
        return tuple(new)

    init = tuple((jnp.full((1, GQ), NEG_INF, f32), jnp.zeros((1, GQ), f32), jnp.zeros((DH, GQ), f32))
                 for _ in heads)
    n_steps = lax.shift_right_logical(c, 2) + 1
    sel_out = lax.fori_loop(0, n_steps, sel_step, init)
    o_s = [acc * (1.0 / jnp.maximum(l_i, TINY)) for (_, l_i, acc) in sel_out]

    backs = list(range(N_WIN_TILES))
    kts = [jnp.maximum(c - back, 0) for back in backs]
    slots = [jnp.where(c >= back, back, N_WIN_TILES) for back in backs]
    kwin = jnp.concatenate([kwn_ref[pl.ds(pl.multiple_of(kt * Q_BLOCK, Q_BLOCK), Q_BLOCK), :] for kt in kts],
                           axis=0)
    o_w = []
    for h in heads:
        tab = jnp.concatenate([wintab_ref[h, slot] for slot in slots], axis=0)
        s = _mm(kwin, qpad[h]) + tab
        m = jnp.max(s, axis=0, keepdims=True)
        p = jnp.exp(s - m)
        l = jnp.sum(p, axis=0, keepdims=True)
        vT = jnp.concatenate([vwnT_ref[kt, h * DH:(h + 1) * DH, :] for kt in kts], axis=1)
        o_w.append(_mm(vT, p.astype(bf16)) * (1.0 / l))

    for h in heads:
        ys = []
        for g in range(NSA_GROUP):
            sl = slice(g * Q_BLOCK, (g + 1) * Q_BLOCK)
            row0 = h * NSA_GROUP + g
            gates = [jax.nn.sigmoid(gate_ref[kind * NSA_HEADS + row0:kind * NSA_HEADS + row0 + 1, :])
                     for kind in range(3)]
            ys.append(gates[0] * o_c[h][:, sl] + gates[1] * o_s[h][:, sl] + gates[2] * o_w[h][:, sl])
        yT = jnp.concatenate(ys, axis=0)
        for half in range(2):
            col = (2 * h + half) * LANES
            out_ref[:, col:col + LANES] = yT[half * LANES:(half + 1) * LANES, :].T.astype(out_ref.dtype)


def _nsa(qT, kcmp, vcmpT, ksl, vslT, kwn, vwnT, smallT, tables, B, S):
    assert S % KEY_SUPER == 0
    nq = S // Q_BLOCK
    n_cmp = S // CMP_STRIDE
    n_sel = S // SEL_BLOCK
    seltab, wintab, cmptab = tables
    ovl = jnp.asarray(_overlap_np(n_cmp, n_sel))
    ksl3 = ksl.reshape(B, S, NSA_KV_WIDTH)
    kwn3 = kwn.reshape(B, S, NSA_KV_WIDTH)
    vslT4 = vslT.reshape(B, nq, NSA_KV_WIDTH, Q_BLOCK)
    vwnT4 = vwnT.reshape(B, nq, NSA_KV_WIDTH, Q_BLOCK)
    k_spec = pl.BlockSpec((None, S, NSA_KV_WIDTH), lambda b, c: (b, 0, 0))
    vT_spec = pl.BlockSpec((None, nq, NSA_KV_WIDTH, Q_BLOCK), lambda b, c: (b, 0, 0, 0))
    const = lambda a: pl.BlockSpec(a.shape, lambda b, c: (0,) * a.ndim)
    return pl.pallas_call(
        _nsa_kernel,
        grid=(B, nq),
        in_specs=[pl.BlockSpec((None, NSA_WIDTH, Q_BLOCK), lambda b, c: (b * nq + c, 0, 0)),
                  pl.BlockSpec((None, n_cmp, NSA_KV_WIDTH), lambda b, c: (b, 0, 0)),
                  pl.BlockSpec((None, NSA_KV_WIDTH, n_cmp), lambda b, c: (b, 0, 0)),
                  k_spec, vT_spec, k_spec, vT_spec,
                  pl.BlockSpec((None, N_GATE_ROWS, Q_BLOCK), lambda b, c: (b * nq + c, 0, 0)),
                  const(seltab), const(wintab), const(cmptab), const(ovl)],
        out_specs=pl.BlockSpec((Q_BLOCK, NSA_WIDTH), lambda b, c: (b * nq + c, 0)),
        out_shape=jax.ShapeDtypeStruct((B * S, NSA_WIDTH), jnp.bfloat16),
        scratch_shapes=[pltpu.VMEM((NSA_KV_HEADS, n_sel, Q_BLOCK), jnp.float32),
                        pltpu.VMEM((NSA_KV_HEADS, n_sel, Q_BLOCK), jnp.float32)],
        compiler_params=_params(2),
        name="nsa",
    )(qT, kcmp, vcmpT, ksl3, vslT4, kwn3, vwnT4, smallT, seltab, wintab, cmptab, ovl)


CONV_TILE = 256
I_COL = 24
F_COL = 28


def _log_sigmoid(x):
    return jnp.minimum(x, 0.0) - jnp.log(1.0 + jnp.exp(-jnp.abs(x)))


def _mlstm_kernel(mqk_ref, mv_ref, mo_ref, small_ref, convw_ref, convb_ref, gbias_ref, norm_ref,
                  out_ref, qk_ref, c_ref, n_ref, m_ref):
    S = mqk_ref.shape[0]
    L = MLSTM_CHUNK
    f32 = jnp.float32
    bf16 = jnp.bfloat16
    kscale_row = jnp.where(lax.broadcasted_iota(jnp.int32, (1, 2 * MLSTM_QK_WIDTH), 1) < MLSTM_QK_WIDTH,
                           1.0, MLSTM_QK_DIM ** -0.5)

    def conv_body(i, _):
        t0 = pl.multiple_of(i * CONV_TILE, CONV_TILE)
        cur = mqk_ref[pl.ds(t0, CONV_TILE), :].astype(f32)
        prev_start = pl.multiple_of(jnp.maximum(t0 - 16, 0), 16)
        prev = mqk_ref[pl.ds(prev_start, 16), :].astype(f32)[8:16, :] * (i > 0).astype(f32)
        ext = jnp.concatenate([prev, cur], axis=0)
        y = convb_ref[...]
        for j in range(CONV_WIDTH):
            lo = SUBLANES - (CONV_WIDTH - 1) + j
            y = y + convw_ref[j:j + 1, :] * ext[lo:lo + CONV_TILE, :]
        y = y * jax.nn.sigmoid(y) * kscale_row
        qk_ref[pl.ds(t0, CONV_TILE), :] = y.astype(bf16)
        return 0

    lax.fori_loop(0, S // CONV_TILE, conv_body, 0)

    c_ref[...] = jnp.zeros_like(c_ref)
    n_ref[...] = jnp.zeros_like(n_ref)
    m_ref[...] = jnp.zeros_like(m_ref)

    t_io = lax.broadcasted_iota(jnp.int32, (L, L), 0)
    s_io = lax.broadcasted_iota(jnp.int32, (L, L), 1)
    causal = s_io <= t_io
    tri = causal.astype(f32)

    def chunk_body(k, _):
        t0 = pl.multiple_of(k * L, L)
        a = small_ref[pl.ds(t0, L), :] + gbias_ref[...]
        lf = _log_sigmoid(a)
        bcum = jnp.dot(tri, lf, precision=lax.Precision.HIGHEST, preferred_element_type=f32)
        aT = a.T
        bcumT = bcum.T
        qk = qk_ref[pl.ds(t0, L), :]
        for hh in range(MLSTM_HEADS):
            i_c = a[:, I_COL + hh:I_COL + hh + 1]
            b_c = bcum[:, F_COL + hh:F_COL + hh + 1]
            i_r = aT[I_COL + hh:I_COL + hh + 1, :]
            b_r = bcumT[F_COL + hh:F_COL + hh + 1, :]
            m_prev = m_ref[hh:hh + 1, 0:1]
            q = qk[:, hh * MLSTM_QK_DIM:(hh + 1) * MLSTM_QK_DIM]
            kk = qk[:, MLSTM_QK_WIDTH + hh * MLSTM_QK_DIM:MLSTM_QK_WIDTH + (hh + 1) * MLSTM_QK_DIM]
            v = mv_ref[pl.ds(t0, L), hh * MLSTM_V_DIM:(hh + 1) * MLSTM_V_DIM]
            c_prev = c_ref[hh]
            n_prev = n_ref[hh:hh + 1, :]

            dmat = jnp.where(causal, b_c - b_r + i_r, NEG_INF)
            m_inter = b_c + m_prev
            m_t = jnp.maximum(m_inter, jnp.max(dmat, axis=1, keepdims=True))
            w = jnp.exp(dmat - m_t)
            a_t = jnp.exp(m_inter - m_t)
            s = _nt(q, kk) * w
            num = _mm(s.astype(bf16), v) + a_t * _mm(q, c_prev.astype(bf16))
            den = (jnp.sum(s, axis=1, keepdims=True)
                   + a_t * jnp.sum(q.astype(f32) * n_prev, axis=1, keepdims=True))
            hval = num / jnp.maximum(jnp.abs(den), jnp.exp(-m_t))

            b_last = b_c[L - 1:L, :]
            ws = b_last - b_c + i_c
            m_next = jnp.maximum(b_last + m_prev, jnp.max(ws, axis=0, keepdims=True))
            decay = jnp.exp(b_last + m_prev - m_next)
            kw = kk.astype(f32) * jnp.exp(ws - m_next)
            c_ref[hh] = decay * c_prev + _mm(kw.T.astype(bf16), v)
            n_ref[hh:hh + 1, :] = decay * n_prev + jnp.sum(kw, axis=0, keepdims=True)
            m_ref[hh:hh + 1, :] = jnp.broadcast_to(m_next, (1, LANES))

            sl = slice(hh * MLSTM_V_DIM, (hh + 1) * MLSTM_V_DIM)
            hn = hval * lax.rsqrt(jnp.mean(hval * hval, axis=1, keepdims=True) + NORM_EPS) * norm_ref[:, sl]
            gate = jax.nn.sigmoid(mo_ref[pl.ds(t0, L), sl].astype(f32))
            out_ref[pl.ds(t0, L), sl] = (gate * hn).astype(out_ref.dtype)
        return 0

    lax.fori_loop(0, S // L, chunk_body, 0)


def _mlstm(mqk, mv, mo, small, conv_w, conv_b, gate_bias, mlstm_norm, B, S):
    gbias_row = jnp.zeros((1, LANES), jnp.float32)
    gbias_row = gbias_row.at[0, I_COL:I_COL + MLSTM_HEADS].set(gate_bias[0].astype(jnp.float32))
    gbias_row = gbias_row.at[0, F_COL:F_COL + MLSTM_HEADS].set(gate_bias[1].astype(jnp.float32))
    seq = lambda w: pl.BlockSpec((S, w), lambda b: (b, 0))
    return pl.pallas_call(
        _mlstm_kernel,
        grid=(B,),
        in_specs=[seq(2 * MLSTM_QK_WIDTH), seq(MLSTM_WIDTH), seq(MLSTM_WIDTH), seq(LANES),
                  _full((CONV_WIDTH, 2 * MLSTM_QK_WIDTH)), _full((1, 2 * MLSTM_QK_WIDTH)),
                  _full((1, LANES)), _full((1, MLSTM_WIDTH))],
        out_specs=seq(MLSTM_WIDTH),
        out_shape=jax.ShapeDtypeStruct((B * S, MLSTM_WIDTH), jnp.bfloat16),
        scratch_shapes=[pltpu.VMEM((S, 2 * MLSTM_QK_WIDTH), jnp.bfloat16),
                        pltpu.VMEM((MLSTM_HEADS, MLSTM_QK_DIM, MLSTM_V_DIM), jnp.float32),
                        pltpu.VMEM((SUBLANES, MLSTM_QK_DIM), jnp.float32),
                        pltpu.VMEM((SUBLANES, LANES), jnp.float32)],
        compiler_params=_params(1),
        name="mlstm",
    )(mqk, mv, mo, small, conv_w, conv_b.reshape(1, -1), gbias_row, mlstm_norm.reshape(1, -1))


def _mem_kv_kernel(mem_ref, g_ref, w_ref, k_ref, v_ref):
    mn = _rms(mem_ref[...], g_ref[...]).astype(jnp.bfloat16)
    k_ref[...] = _mm(mn, w_ref[:, :D_MODEL]).astype(k_ref.dtype)
    v_ref[...] = _mm(mn, w_ref[:, D_MODEL:]).astype(v_ref.dtype)


def _mem_kv(mem, gain, w_xkv):
    B, M, _ = mem.shape
    spec = pl.BlockSpec((None, M, D_MODEL), lambda b: (b, 0, 0))
    return pl.pallas_call(
        _mem_kv_kernel,
        grid=(B,),
        in_specs=[spec, _full((1, D_MODEL)), _full((D_MODEL, 2 * D_MODEL))],
        out_specs=[spec, spec],
        out_shape=[jax.ShapeDtypeStruct((B, M, D_MODEL), jnp.bfloat16)] * 2,
        compiler_params=_params(1),
        name="mem_kv",
    )(mem, gain, w_xkv.astype(jnp.bfloat16))


TM_X = 512


def _mix_xattn_kernel(ynsa_ref, yml_ref, x_ref, wout_ref, gpost_ref, gpre_ref, wq_ref, k_ref, v_ref,
                      wo_ref, gpost2_ref, out_ref):
    bf16 = jnp.bfloat16
    y = _mm(ynsa_ref[...], wout_ref[:NSA_WIDTH, :]) + _mm(yml_ref[...], wout_ref[NSA_WIDTH:, :])
    x1 = x_ref[...] + _rms(y, gpost_ref[...])
    h2 = _rms(x1, gpre_ref[...]).astype(bf16)
    q = (_mm(h2, wq_ref[...]) * (XATTN_HEAD_DIM ** -0.5)).astype(bf16)
    outs = []
    for hh in range(XATTN_HEADS):
        sl = slice(hh * XATTN_HEAD_DIM, (hh + 1) * XATTN_HEAD_DIM)
        s = _nt(q[:, sl], k_ref[:, sl])
        p = jnp.exp(s - jnp.max(s, axis=1, keepdims=True))
        l = jnp.sum(p, axis=1, keepdims=True)
        outs.append((_mm(p.astype(bf16), v_ref[:, sl]) * (1.0 / l)).astype(bf16))
    o = jnp.concatenate(outs, axis=1)
    y2 = _mm(o, wo_ref[...])
    out_ref[...] = x1 + _rms(y2, gpost2_ref[...])


def _mix_xattn(ynsa, yml, x2d, w_out, g_post, g_pre, w_xq, kx, vx, w_xo, g_post2, B, S):
    nt = S // TM_X
    M = kx.shape[1]
    tok = lambda w: pl.BlockSpec((TM_X, w), lambda b, i: (b * nt + i, 0))
    mem_spec = pl.BlockSpec((None, M, D_MODEL), lambda b, i: (b, 0, 0))
    sq = _full((D_MODEL, D_MODEL))
    row = _full((1, D_MODEL))
    bf = lambda w: w.astype(jnp.bfloat16)
    return pl.pallas_call(
        _mix_xattn_kernel,
        grid=(B, nt),
        in_specs=[tok(NSA_WIDTH), tok(MLSTM_WIDTH), tok(D_MODEL), sq, row, row, sq, mem_spec, mem_spec,
                  sq, row],
        out_specs=tok(D_MODEL),
        out_shape=jax.ShapeDtypeStruct((B * S, D_MODEL), jnp.float32),
        compiler_params=_params(2),
        name="mix_xattn",
    )(ynsa, yml, x2d, bf(w_out), g_post, g_pre, bf(w_xq), kx, vx, bf(w_xo), g_post2)


TM_F = 512


def _ffn_kernel(x_ref, gpre_ref, wg_ref, wu_ref, wd_ref, gpost_ref, out_ref, acc_ref):
    bf16 = jnp.bfloat16
    x = x_ref[...]
    h = _rms(x, gpre_ref[...]).astype(bf16)
    acc_ref[...] = jnp.zeros_like(acc_ref)

    def body(j, _):
        g = _mm(h, wg_ref[j])
        u = _mm(h, wu_ref[j])
        act = (g * jax.nn.sigmoid(g) * u).astype(bf16)
        acc_ref[...] += _mm(act, wd_ref[j])
        return 0

    lax.fori_loop(0, wg_ref.shape[0], body, 0)
    out_ref[...] = x + _rms(acc_ref[...], gpost_ref[...])


def _ffn(x2d, g_pre, w_gate_up, w_down, g_post):
    T = x2d.shape[0]
    nf = D_FF // F_TILE
    wg = w_gate_up[:, :D_FF].reshape(D_MODEL, nf, F_TILE).transpose(1, 0, 2).astype(jnp.bfloat16)
    wu = w_gate_up[:, D_FF:].reshape(D_MODEL, nf, F_TILE).transpose(1, 0, 2).astype(jnp.bfloat16)
    wd = w_down.reshape(nf, F_TILE, D_MODEL).astype(jnp.bfloat16)
    tok = pl.BlockSpec((TM_F, D_MODEL), lambda i: (i, 0))
    row = _full((1, D_MODEL))
    return pl.pallas_call(
        _ffn_kernel,
        grid=(T // TM_F,),
        in_specs=[tok, row, _full((nf, D_MODEL, F_TILE)), _full((nf, D_MODEL, F_TILE)),
                  _full((nf, F_TILE, D_MODEL)), row],
        out_specs=tok,
        out_shape=jax.ShapeDtypeStruct((T, D_MODEL), jnp.float32),
        scratch_shapes=[pltpu.VMEM((TM_F, D_MODEL), jnp.float32)],
        compiler_params=_params(1),
        name="ffn",
    )(x2d, g_pre, wg, wu, wd, g_post)


def _layer(x, mem, rel_bias, mix_norm_pre, w_in, cmp_pos_k, cmp_pos_v, cmp_w1_k, cmp_w2_k, cmp_w1_v,
           cmp_w2_v, conv_w, conv_b, mlstm_gate_bias, mlstm_norm, w_out, mix_norm_post, xattn_norm_pre,
           mem_norm, w_xq, w_xkv, w_xo, xattn_norm_post, ffn_norm_pre, w_gate_up, w_down, ffn_norm_post):
    B, S, _ = x.shape
    row = lambda g: g.reshape(1, -1).astype(jnp.float32)
    x2d = x.reshape(B * S, D_MODEL)
    w_tok, w_feat = _in_proj_weights(w_in)
    (kc, vc, ksl, kwn, mqk, mv, mo, small, qT, vslT, vwnT, smallT) = _in_proj(
        x2d, row(mix_norm_pre), w_tok, w_feat)
    kcmp, vcmpT = _compress(kc, vc, cmp_pos_k, cmp_pos_v, cmp_w1_k, cmp_w2_k, cmp_w1_v, cmp_w2_v, B, S)
    tables = _bias_tables(rel_bias.astype(jnp.float32))
    ynsa = _nsa(qT, kcmp, vcmpT, ksl, vslT, kwn, vwnT, smallT, tables, B, S)
    yml = _mlstm(mqk, mv, mo, small, conv_w, conv_b, mlstm_gate_bias, mlstm_norm, B, S)
    kx, vx = _mem_kv(mem, row(mem_norm), w_xkv)
    x2 = _mix_xattn(ynsa, yml, x2d, w_out, row(mix_norm_post), row(xattn_norm_pre), w_xq, kx, vx, w_xo,
                    row(xattn_norm_post), B, S)
    x3 = _ffn(x2, row(ffn_norm_pre), w_gate_up, w_down, row(ffn_norm_post))
    return x3.reshape(B, S, D_MODEL)


def kernel(x, mem, rel_bias, mix_norm_pre, w_in, cmp_pos_k, cmp_pos_v, cmp_w1_k, cmp_w2_k, cmp_w1_v, cmp_w2_v,
           conv_w, conv_b, mlstm_gate_bias, mlstm_norm, w_out, mix_norm_post, xattn_norm_pre, mem_norm, w_xq,
           w_xkv, w_xo, xattn_norm_post, ffn_norm_pre, w_gate_up, w_down, ffn_norm_post):
    depth = w_in.shape[0]
    for l in range(depth):
        x = _layer(x, mem, rel_bias, mix_norm_pre[l], w_in[l], cmp_pos_k[l], cmp_pos_v[l], cmp_w1_k[l],
                   cmp_w2_k[l], cmp_w1_v[l], cmp_w2_v[l], conv_w[l], conv_b[l], mlstm_gate_bias[l],
                   mlstm_norm[l], w_out[l], mix_norm_post[l], xattn_norm_pre[l], mem_norm[l], w_xq[l],
                   w_xkv[l], w_xo[l], xattn_norm_post[l], ffn_norm_pre[l], w_gate_up[l], w_down[l],
                   ffn_norm_post[l])
    return x
```

```python
import functools
import math

import numpy as np
import jax
import jax.numpy as jnp
from jax import lax
from jax.experimental import pallas as pl
from jax.experimental.pallas import tpu as pltpu

D_MODEL = 1024
NSA_WIDTH = 512
NSA_HEAD_DIM = 64
NSA_HEADS = 8
NSA_KV_HEADS = 2
NSA_GROUP = 4
NSA_KV_WIDTH = 128
CMP_STRIDE = 16
CMP_BLOCK = 32
CMP_HIDDEN = 256
SEL_BLOCK = 64
N_SELECT = 16
WINDOW = 512
Q_BLOCK = 128
FORCED_SCORE = 1.0e4
MLSTM_WIDTH = 512
MLSTM_HEADS = 4
MLSTM_V_DIM = 128
MLSTM_QK_DIM = 64
MLSTM_QK_WIDTH = 256
MLSTM_CHUNK = 64
CONV_WIDTH = 4
REL_BUCKETS = 32
REL_MAX_DISTANCE = 128
XATTN_HEADS = 4
XATTN_HEAD_DIM = 256
D_FF = 2816
NORM_EPS = 1e-6
NEG_INF = -1.0e30
LOG2E = math.log2(math.e)

IN_SIZES = (NSA_WIDTH,) + (NSA_KV_WIDTH,) * 6 + (NSA_HEADS * 3, MLSTM_QK_WIDTH, MLSTM_QK_WIDTH,
                                                 MLSTM_WIDTH, MLSTM_HEADS, MLSTM_HEADS, MLSTM_WIDTH)
IN_OFFSETS = tuple(int(o) for o in np.cumsum((0,) + IN_SIZES)[:-1])

LANES = 128
SUBLANES = 8
VMEM_LIMIT_BYTES = 56 * 1024 * 1024

N_GATE_ROWS = 32
F_TILE = 256


def _rms(x, gain):
    return x * lax.rsqrt(jnp.mean(x * x, axis=-1, keepdims=True) + NORM_EPS) * gain


def _nt(a, b):
    return lax.dot_general(a, b, (((1,), (1,)), ((), ())), preferred_element_type=jnp.float32)


def _mm(a, b):
    return jnp.dot(a, b, preferred_element_type=jnp.float32)


def _params(n_axes, flags=None):
    return pltpu.CompilerParams(dimension_semantics=("arbitrary",) * n_axes,
                                vmem_limit_bytes=VMEM_LIMIT_BYTES, flags=flags)


def _full(shape):
    nd = len(shape)
    return pl.BlockSpec(shape, lambda *_: (0,) * nd)


TM_IN = 512
_TOK_GROUPS = (("kc", 128, jnp.float32), ("vc", 128, jnp.float32), ("ksl", 128, jnp.bfloat16),
               ("kwn", 128, jnp.bfloat16), ("mqk", 512, jnp.bfloat16), ("mv", 512, jnp.bfloat16),
               ("mo", 512, jnp.bfloat16), ("small", 128, jnp.float32))
_FEAT_GROUPS = (("qT", 512, jnp.bfloat16), ("vslT", 128, jnp.bfloat16), ("vwnT", 128, jnp.bfloat16),
                ("smallT", N_GATE_ROWS, jnp.float32))


def _in_proj_kernel(x_ref, g_ref, wtok_ref, wfeat_ref, *out_refs):
    h = _rms(x_ref[...], g_ref[...]).astype(jnp.bfloat16)
    n_tok = len(_TOK_GROUPS)
    off = 0
    for (name, width, dt), o_ref in zip(_TOK_GROUPS, out_refs[:n_tok]):
        o_ref[...] = _mm(h, wtok_ref[:, off:off + width]).astype(dt)
        off += width
    off = 0
    for (name, rows, dt), o_ref in zip(_FEAT_GROUPS, out_refs[n_tok:]):
        r = _nt(wfeat_ref[off:off + rows, :], h)
        if name == "qT":
            r = r * (NSA_HEAD_DIM ** -0.5 * LOG2E)
        for j in range(TM_IN // LANES):
            o_ref[j] = r[:, j * LANES:(j + 1) * LANES].astype(dt)
        off += rows


def _in_proj(x2d, gain, w_tok, w_feat):
    T = x2d.shape[0]
    n_tok_cols = w_tok.shape[1]
    n_feat_rows = w_feat.shape[0]
    out_shape, out_specs = [], []
    for name, width, dt in _TOK_GROUPS:
        out_shape.append(jax.ShapeDtypeStruct((T, width), dt))
        out_specs.append(pl.BlockSpec((TM_IN, width), lambda i: (i, 0)))
    for name, rows, dt in _FEAT_GROUPS:
        out_shape.append(jax.ShapeDtypeStruct((T // LANES, rows, LANES), dt))
        out_specs.append(pl.BlockSpec((TM_IN // LANES, rows, LANES), lambda i: (i, 0, 0)))
    return pl.pallas_call(
        _in_proj_kernel,
        grid=(T // TM_IN,),
        in_specs=[pl.BlockSpec((TM_IN, D_MODEL), lambda i: (i, 0)),
                  _full((1, D_MODEL)),
                  _full((D_MODEL, n_tok_cols)),
                  _full((n_feat_rows, D_MODEL))],
        out_specs=out_specs,
        out_shape=out_shape,
        compiler_params=_params(1),
        name="in_proj",
    )(x2d, gain, w_tok, w_feat)


def _in_proj_weights(w_in):
    (nq, kc, vc, ksl, vsl, kwn, vwn, gt, mq, mk, mv, mi, mf, mo) = [
        w_in[:, o:o + s] for o, s in zip(IN_OFFSETS, IN_SIZES)]
    gt_r = gt.reshape(D_MODEL, NSA_KV_HEADS, NSA_GROUP, 3).transpose(0, 3, 1, 2).reshape(D_MODEL, 24)
    small = jnp.concatenate([gt_r, mi, mf], axis=1)
    small_pad = jnp.pad(small, ((0, 0), (0, LANES - N_GATE_ROWS)))
    w_tok = jnp.concatenate([kc, vc, ksl, kwn, mq, mk, mv, mo, small_pad], axis=1)
    w_feat = jnp.concatenate([nq, vsl, vwn, small], axis=1).T
    return w_tok.astype(jnp.bfloat16), w_feat.astype(jnp.bfloat16)


N_CHUNK_COLS = CMP_STRIDE * NSA_KV_WIDTH
N_HID2 = NSA_KV_HEADS * CMP_HIDDEN


def _compress_one(c, pos_ref, w1_ref, n_chunks):
    lo = _mm((c + pos_ref[0:1, :]).astype(jnp.bfloat16), w1_ref[0])
    hi = _mm((c + pos_ref[1:2, :]).astype(jnp.bfloat16), w1_ref[1])
    pre = lo + pltpu.roll(hi, n_chunks - 1, 0)
    return (pre * jax.nn.sigmoid(pre)).astype(jnp.bfloat16)


def _compress_kernel(kc_ref, vc_ref, posk_ref, posv_ref, w1k_ref, w1v_ref, w2k_ref, w2vT_ref,
                     kcmp_ref, vcmpT_ref):
    n_chunks = kc_ref.shape[0]
    hid_k = _compress_one(kc_ref[...], posk_ref, w1k_ref, n_chunks)
    kcmp = _mm(hid_k, w2k_ref[...])
    row = lax.broadcasted_iota(jnp.int32, kcmp.shape, 0)
    kcmp_ref[...] = jnp.where(row < n_chunks - 1, kcmp, 0.0).astype(kcmp_ref.dtype)
    hid_v = _compress_one(vc_ref[...], posv_ref, w1v_ref, n_chunks)
    vcmpT = _nt(w2vT_ref[...], hid_v)
    col = lax.broadcasted_iota(jnp.int32, vcmpT.shape, 1)
    vcmpT_ref[...] = jnp.where(col < n_chunks - 1, vcmpT, 0.0).astype(vcmpT_ref.dtype)


def _compress_weights(pos, w1, w2):
    eye = jnp.eye(NSA_KV_HEADS, dtype=w1.dtype)
    w1r = w1.reshape(2, CMP_STRIDE, NSA_HEAD_DIM, CMP_HIDDEN)
    w1e = jnp.einsum('atdj,hg->athdgj', w1r, eye).reshape(2, N_CHUNK_COLS, N_HID2)
    pos_e = jnp.broadcast_to(pos.reshape(2, CMP_STRIDE, 1, NSA_HEAD_DIM),
                             (2, CMP_STRIDE, NSA_KV_HEADS, NSA_HEAD_DIM)).reshape(2, N_CHUNK_COLS)
    w2e = jnp.einsum('jd,hg->hjgd', w2, eye).reshape(N_HID2, NSA_KV_WIDTH)
    return pos_e, w1e.astype(jnp.bfloat16), w2e.astype(jnp.bfloat16)


def _compress(kc, vc, cmp_pos_k, cmp_pos_v, cmp_w1_k, cmp_w2_k, cmp_w1_v, cmp_w2_v, B, S):
    n_chunks = S // CMP_STRIDE
    posk, w1k, w2k = _compress_weights(cmp_pos_k, cmp_w1_k, cmp_w2_k)
    posv, w1v, w2v = _compress_weights(cmp_pos_v, cmp_w1_v, cmp_w2_v)
    kc3 = kc.reshape(B, n_chunks, N_CHUNK_COLS)
    vc3 = vc.reshape(B, n_chunks, N_CHUNK_COLS)
    chunk_spec = pl.BlockSpec((None, n_chunks, N_CHUNK_COLS), lambda b: (b, 0, 0))
    return pl.pallas_call(
        _compress_kernel,
        grid=(B,),
        in_specs=[chunk_spec, chunk_spec,
                  _full((2, N_CHUNK_COLS)), _full((2, N_CHUNK_COLS)),
                  _full((2, N_CHUNK_COLS, N_HID2)), _full((2, N_CHUNK_COLS, N_HID2)),
                  _full((N_HID2, NSA_KV_WIDTH)), _full((NSA_KV_WIDTH, N_HID2))],
        out_specs=[pl.BlockSpec((None, n_chunks, NSA_KV_WIDTH), lambda b: (b, 0, 0)),
                   pl.BlockSpec((None, NSA_KV_WIDTH, n_chunks), lambda b: (b, 0, 0))],
        out_shape=[jax.ShapeDtypeStruct((B, n_chunks, NSA_KV_WIDTH), jnp.bfloat16),
                   jax.ShapeDtypeStruct((B, NSA_KV_WIDTH, n_chunks), jnp.bfloat16)],
        compiler_params=_params(1),
        name="compress",
    )(kc3, vc3, posk, posv, w1k, w1v, w2k, w2v.T)


GQ = NSA_GROUP * Q_BLOCK
TINY = 1e-30
CMP_TAB_ROWS = 512
CMP_TAB_ZERO = 248
CMP_TAB_LOOKUP = (232, 256)
SEL_STEP_SHIFT = 2
KEY_SUPER = Q_BLOCK << SEL_STEP_SHIFT
SEL_TAB_ZERO = KEY_SUPER + Q_BLOCK
SEL_TAB_ROWS = SEL_TAB_ZERO + KEY_SUPER
N_WIN_TILES = WINDOW // Q_BLOCK + 1
V_ROWS = NSA_HEAD_DIM + 16


def _bucket_np(dist):
    n = np.maximum(dist, 0)
    max_exact = REL_BUCKETS // 2
    nf = np.maximum(n, 1).astype(np.float64)
    large = max_exact + (np.log(nf / max_exact) / math.log(REL_MAX_DISTANCE / max_exact)
                         * (REL_BUCKETS - max_exact)).astype(np.int64)
    large = np.minimum(large, REL_BUCKETS - 1)
    return np.where(n < max_exact, n, large).astype(np.int32)


def _bias_index_tables():
    m = np.arange(Q_BLOCK)[:, None]
    r = np.arange(Q_BLOCK)[None, :]
    diag = np.where(r - m >= 0, _bucket_np(r - m), -1).astype(np.int32)
    off = _bucket_np(Q_BLOCK + r - m)
    jp = np.arange(*CMP_TAB_LOOKUP)[:, None] - CMP_TAB_ZERO
    d_c = r - CMP_STRIDE * jp - (CMP_BLOCK - 1)
    cmp_idx = np.where(d_c >= 0, _bucket_np(d_c), -1).astype(np.int32)
    return diag, off, cmp_idx


def _bias_tables_kernel(rb_ref, diag_idx_ref, off_idx_ref, cmp_idx_ref, sel_ref, win_ref, cmp_ref):
    f32 = jnp.float32

    def lookup(idx, head):
        far = rb_ref[head, REL_BUCKETS - 1]
        acc = jnp.full(idx.shape, NEG_INF, f32)
        for k in range(REL_BUCKETS):
            acc = jnp.where(idx == k, (rb_ref[head, k] - far) * LOG2E, acc)
        return acc

    m_io = lax.broadcasted_iota(jnp.int32, (Q_BLOCK, Q_BLOCK), 0)
    r_io = lax.broadcasted_iota(jnp.int32, (Q_BLOCK, Q_BLOCK), 1)
    neg_tile = jnp.full((Q_BLOCK, Q_BLOCK), NEG_INF, f32)
    lo, hi = CMP_TAB_LOOKUP
    for h in range(NSA_KV_HEADS):
        for g in range(NSA_GROUP):
            head = h * NSA_GROUP + g
            sl = slice(g * Q_BLOCK, (g + 1) * Q_BLOCK)
            far = 0.0
            far_tile = jnp.full((Q_BLOCK, Q_BLOCK), far, f32)
            diag_v = lookup(diag_idx_ref[...], head)
            off_v = lookup(off_idx_ref[...], head)
            n_far = (SEL_TAB_ZERO - Q_BLOCK) // Q_BLOCK
            for t in range(SEL_TAB_ROWS // Q_BLOCK):
                rows = slice(t * Q_BLOCK, (t + 1) * Q_BLOCK)
                tile = far_tile if t < n_far else off_v if t == n_far else diag_v if t == n_far + 1 else neg_tile
                sel_ref[h, rows, sl] = tile
            win_ref[h, 0, :, sl] = diag_v
            win_ref[h, 1, :, sl] = off_v
            for back in range(2, N_WIN_TILES - 1):
                win_ref[h, back, :, sl] = far_tile
            win_ref[h, N_WIN_TILES - 1, :, sl] = jnp.where(r_io < m_io, far, NEG_INF)
            win_ref[h, N_WIN_TILES, :, sl] = neg_tile
            cmp_ref[h, 0:lo, sl] = jnp.full((lo, Q_BLOCK), far, f32)
            cmp_ref[h, lo:hi, sl] = lookup(cmp_idx_ref[...], head)
            cmp_ref[h, hi:CMP_TAB_ROWS, sl] = jnp.full((CMP_TAB_ROWS - hi, Q_BLOCK), NEG_INF, f32)


def _bias_tables(rel_bias):
    diag_idx, off_idx, cmp_idx = _bias_index_tables()
    shapes = [(NSA_KV_HEADS, SEL_TAB_ROWS, GQ), (NSA_KV_HEADS, N_WIN_TILES + 1, Q_BLOCK, GQ),
              (NSA_KV_HEADS, CMP_TAB_ROWS, GQ)]
    return pl.pallas_call(
        _bias_tables_kernel,
        in_specs=[pl.BlockSpec(memory_space=pltpu.SMEM),
                  _full(diag_idx.shape), _full(off_idx.shape), _full(cmp_idx.shape)],
        out_specs=[_full(s) for s in shapes],
        out_shape=[jax.ShapeDtypeStruct(s, jnp.float32) for s in shapes],
        grid=(1,),
        compiler_params=_params(1),
        name="bias_tables",
    )(rel_bias, jnp.asarray(diag_idx), jnp.asarray(off_idx), jnp.asarray(cmp_idx))


def _overlap_np(n_cmp_rows, n_sel):
    cmp_start = np.arange(n_cmp_rows) * CMP_STRIDE
    cmp_end = cmp_start + CMP_BLOCK - 1
    sel_start = np.arange(n_sel) * SEL_BLOCK
    ov = ((cmp_start[None, :] <= sel_start[:, None] + SEL_BLOCK - 1)
          & (cmp_end[None, :] >= sel_start[:, None])).astype(np.float32)
    ov[:, n_cmp_rows - 1] = 0.0
    return ov


def _tile4(a):
    return jnp.concatenate([a] * NSA_GROUP, axis=1)


def _select_blocks(score, score_ref, n_top):
    n_sel = score.shape[0]
    score_ref[...] = score
    n_grp = n_sel // SUBLANES
    grp = [score[SUBLANES * v:SUBLANES * (v + 1), :] for v in range(n_grp)]
    cnt = [jnp.zeros((SUBLANES, Q_BLOCK), jnp.int32) for _ in range(n_grp)]
    sub_io = lax.broadcasted_iota(jnp.int32, (SUBLANES, Q_BLOCK), 0)
    for jp in range(n_sel):
        row = score_ref[jp:jp + 1, :]
        for v in range(n_grp):
            if SUBLANES * v > jp:
                inc = (row >= grp[v]).astype(jnp.int32)
            elif SUBLANES * (v + 1) - 1 < jp:
                inc = (row > grp[v]).astype(jnp.int32)
            else:
                tie = (sub_io > jp - SUBLANES * v).astype(jnp.int32)
                inc = jnp.where(row > grp[v], 1, jnp.where(row == grp[v], tie, 0))
            cnt[v] = cnt[v] + inc
    return [jnp.where(cnt[v] < n_top, 0.0, NEG_INF) for v in range(n_grp)]


def _nsa_kernel(q_ref, kcmp_ref, vcmpT_ref, ksl_ref, vslT_ref, kwn_ref, vwnT_ref, gate_ref,
                seltab_ref, wintab_ref, cmptab_ref, ovl_ref, out_ref, score_ref, selb_ref, sbuf_ref):
    c = pl.program_id(1)
    n_cmp = kcmp_ref.shape[0]
    n_sel = ovl_ref.shape[0]
    n_top = min(N_SELECT, n_sel)
    f32 = jnp.float32
    bf16 = jnp.bfloat16
    DH = NSA_HEAD_DIM
    heads = range(NSA_KV_HEADS)

    q = q_ref[...]
    zq = jnp.zeros((DH, GQ), bf16)
    qpad = []
    for h in heads:
        qcat = jnp.concatenate([q[(h * NSA_GROUP + g) * DH:(h * NSA_GROUP + g + 1) * DH, :]
                                for g in range(NSA_GROUP)], axis=1)
        qpad.append(jnp.concatenate([qcat, zq] if h == 0 else [zq, qcat], axis=0))

    cmp_off = pl.multiple_of(CMP_TAB_ZERO - (Q_BLOCK // CMP_STRIDE) * c, SUBLANES)
    kcmp = kcmp_ref[...]
    j_io = lax.broadcasted_iota(jnp.int32, (n_sel, Q_BLOCK), 0)
    r_io = lax.broadcasted_iota(jnp.int32, (n_sel, Q_BLOCK), 1)
    cur = (Q_BLOCK // SEL_BLOCK) * c + (r_io >= SEL_BLOCK).astype(jnp.int32)
    forced = (j_io == 0) | (j_io == cur) | (j_io == cur - 1)
    visible = j_io <= cur
    o_c = []
    for h in heads:
        tab = cmptab_ref[h, pl.ds(cmp_off, n_cmp), :]
        s = _mm(kcmp, qpad[h]) + tab
        m = jnp.max(s, axis=0, keepdims=True)
        p = jnp.where(tab > 0.5 * NEG_INF, jnp.exp2(s - m), 0.0)
        l = jnp.sum(p, axis=0, keepdims=True)
        pn = p * (1.0 / jnp.maximum(l, TINY))
        o_c.append(_mm(vcmpT_ref[h * DH:(h + 1) * DH, :], pn.astype(bf16)))
        psum = pn[:, 0:Q_BLOCK]
        for g in range(1, NSA_GROUP):
            psum = psum + pn[:, g * Q_BLOCK:(g + 1) * Q_BLOCK]
        imp = jnp.dot(ovl_ref[...], psum, precision=lax.Precision.HIGHEST,
                      preferred_element_type=f32)
        score = jnp.where(forced, FORCED_SCORE, jnp.where(visible, imp, -1.0))
        rows = _select_blocks(score, score_ref.at[h], n_top)
        for v, blk in enumerate(rows):
            selb_ref[h, SUBLANES * v:SUBLANES * (v + 1), :] = blk

    def ones_rows(n_keys):
        return jnp.ones((V_ROWS - DH, n_keys), bf16)

    backs = list(range(N_WIN_TILES))
    kts = [jnp.maximum(c - back, 0) for back in backs]
    slots = [jnp.where(c >= back, back, N_WIN_TILES) for back in backs]
    kwin = jnp.concatenate([kwn_ref[pl.ds(pl.multiple_of(kt * Q_BLOCK, Q_BLOCK), Q_BLOCK), :] for kt in kts],
                           axis=0)
    o_w = []
    for h in heads:
        tab = jnp.concatenate([wintab_ref[h, slot] for slot in slots], axis=0)
        s = _mm(kwin, qpad[h]) + tab
        m = jnp.max(s, axis=0, keepdims=True)
        p = jnp.exp2(s - m)
        vT = jnp.concatenate([vwnT_ref[kt, h * DH:(h + 1) * DH, :] for kt in kts], axis=1)
        vT = jnp.concatenate([vT, ones_rows(N_WIN_TILES * Q_BLOCK)], axis=0)
        acc = _mm(vT, p.astype(bf16))
        o_w.append(acc[0:DH, :] * (1.0 / acc[DH:DH + 1, :]))

    blocks_per_step = KEY_SUPER // SEL_BLOCK
    tiles_per_step = KEY_SUPER // Q_BLOCK

    def probs(j, m_new):
        return [jnp.exp2(sbuf_ref[j & 1, h] - m_new[h]).astype(bf16) for h in heads]

    def score_matmuls(j):
        kblk = ksl_ref[pl.ds(pl.multiple_of(j * KEY_SUPER, KEY_SUPER), KEY_SUPER), :]
        return [_mm(kblk, qpad[h]) for h in heads]

    def score_finish(j, qk, m_prev):
        tab_off = pl.multiple_of(jnp.maximum(j * KEY_SUPER - c * Q_BLOCK + SEL_TAB_ZERO, 0), Q_BLOCK)
        m_new = []
        for h in heads:
            selb = jnp.concatenate(
                [jnp.broadcast_to(selb_ref[h, pl.ds(j * blocks_per_step + i, 1), :], (SEL_BLOCK, Q_BLOCK))
                 for i in range(blocks_per_step)], axis=0)
            s = qk[h] + _tile4(selb) + seltab_ref[h, pl.ds(tab_off, KEY_SUPER), :]
            sbuf_ref[j & 1, h] = s
            m_new.append(jnp.maximum(m_prev[h], jnp.max(s, axis=0, keepdims=True)))
        return tuple(m_new)

    def accumulate(j, p, m_old, m_new, acc):
        out = []
        for h in heads:
            vT = jnp.concatenate([vslT_ref[j * tiles_per_step + i, h * DH:(h + 1) * DH, :]
                                  for i in range(tiles_per_step)], axis=1)
            vT = jnp.concatenate([vT, ones_rows(KEY_SUPER)], axis=0)
            out.append(jnp.exp2(m_old[h] - m_new[h]) * acc[h] + _mm(vT, p[h]))
        return tuple(out)

    def sel_body(j, carry):
        m_old, m_cur, acc = carry
        p = probs(j, m_cur)
        qk = score_matmuls(j + 1)
        acc = accumulate(j, p, m_old, m_cur, acc)
        return m_cur, score_finish(j + 1, qk, m_cur), acc

    n_steps = lax.shift_right_logical(c, SEL_STEP_SHIFT) + 1
    m_init = tuple(jnp.full((1, GQ), NEG_INF, f32) for _ in heads)
    acc_init = tuple(jnp.zeros((V_ROWS, GQ), f32) for _ in heads)
    m_first = score_finish(0, score_matmuls(0), m_init)
    m_old, m_cur, acc = lax.fori_loop(0, n_steps - 1, sel_body, (m_init, m_first, acc_init))
    acc = accumulate(n_steps - 1, probs(n_steps - 1, m_cur), m_old, m_cur, acc)
    o_s = [a[0:DH, :] * (1.0 / jnp.maximum(a[DH:DH + 1, :], TINY)) for a in acc]

    for h in heads:
        ys = []
        for g in range(NSA_GROUP):
            sl = slice(g * Q_BLOCK, (g + 1) * Q_BLOCK)
            row0 = h * NSA_GROUP + g
            gates = [jax.nn.sigmoid(gate_ref[kind * NSA_HEADS + row0:kind * NSA_HEADS + row0 + 1, :])
                     for kind in range(3)]
            ys.append(gates[0] * o_c[h][:, sl] + gates[1] * o_s[h][:, sl] + gates[2] * o_w[h][:, sl])
        yT = jnp.concatenate(ys, axis=0)
        for half in range(2):
            col = (2 * h + half) * LANES
            out_ref[:, col:col + LANES] = yT[half * LANES:(half + 1) * LANES, :].T.astype(out_ref.dtype)


def _nsa(qT, kcmp, vcmpT, ksl, vslT, kwn, vwnT, smallT, tables, B, S):
    assert S % KEY_SUPER == 0
    nq = S // Q_BLOCK
    n_cmp = S // CMP_STRIDE
    n_sel = S // SEL_BLOCK
    seltab, wintab, cmptab = tables
    ovl = jnp.asarray(_overlap_np(n_cmp, n_sel))
    ksl3 = ksl.reshape(B, S, NSA_KV_WIDTH)
    kwn3 = kwn.reshape(B, S, NSA_KV_WIDTH)
    vslT4 = vslT.reshape(B, nq, NSA_KV_WIDTH, Q_BLOCK)
    vwnT4 = vwnT.reshape(B, nq, NSA_KV_WIDTH, Q_BLOCK)
    k_spec = pl.BlockSpec((None, S, NSA_KV_WIDTH), lambda b, c: (b, 0, 0))
    vT_spec = pl.BlockSpec((None, nq, NSA_KV_WIDTH, Q_BLOCK), lambda b, c: (b, 0, 0, 0))
    const = lambda a: pl.BlockSpec(a.shape, lambda b, c: (0,) * a.ndim)
    return pl.pallas_call(
        _nsa_kernel,
        grid=(B, nq),
        in_specs=[pl.BlockSpec((None, NSA_WIDTH, Q_BLOCK), lambda b, c: (b * nq + c, 0, 0)),
                  pl.BlockSpec((None, n_cmp, NSA_KV_WIDTH), lambda b, c: (b, 0, 0)),
                  pl.BlockSpec((None, NSA_KV_WIDTH, n_cmp), lambda b, c: (b, 0, 0)),
                  k_spec, vT_spec, k_spec, vT_spec,
                  pl.BlockSpec((None, N_GATE_ROWS, Q_BLOCK), lambda b, c: (b * nq + c, 0, 0)),
                  const(seltab), const(wintab), const(cmptab), const(ovl)],
        out_specs=pl.BlockSpec((Q_BLOCK, NSA_WIDTH), lambda b, c: (b * nq + c, 0)),
        out_shape=jax.ShapeDtypeStruct((B * S, NSA_WIDTH), jnp.bfloat16),
        scratch_shapes=[pltpu.VMEM((NSA_KV_HEADS, n_sel, Q_BLOCK), jnp.float32),
                        pltpu.VMEM((NSA_KV_HEADS, n_sel, Q_BLOCK), jnp.float32),
                        pltpu.VMEM((2, NSA_KV_HEADS, KEY_SUPER, GQ), jnp.float32)],
        compiler_params=_params(2),
        name="nsa",
    )(qT, kcmp, vcmpT, ksl3, vslT4, kwn3, vwnT4, smallT, seltab, wintab, cmptab, ovl)


CONV_TILE = 256
I_COL = 24
F_COL = 28


def _log_sigmoid(x):
    return jnp.minimum(x, 0.0) - jnp.log(1.0 + jnp.exp(-jnp.abs(x)))


def _mlstm_kernel(mqk_ref, mv_ref, mo_ref, small_ref, convw_ref, convb_ref, gbias_ref, norm_ref,
                  out_ref, qk_ref):
    S = mqk_ref.shape[0]
    L = MLSTM_CHUNK
    f32 = jnp.float32
    bf16 = jnp.bfloat16
    kscale_row = jnp.where(lax.broadcasted_iota(jnp.int32, (1, 2 * MLSTM_QK_WIDTH), 1) < MLSTM_QK_WIDTH,
                           1.0, MLSTM_QK_DIM ** -0.5)

    def conv_body(i, _):
        t0 = pl.multiple_of(i * CONV_TILE, CONV_TILE)
        cur = mqk_ref[pl.ds(t0, CONV_TILE), :].astype(f32)
        prev_start = pl.multiple_of(jnp.maximum(t0 - 16, 0), 16)
        prev = mqk_ref[pl.ds(prev_start, 16), :].astype(f32)[8:16, :] * jnp.where(i > 0, 1.0, 0.0)
        ext = jnp.concatenate([prev, cur], axis=0)
        y = convb_ref[...]
        for j in range(CONV_WIDTH):
            lo = SUBLANES - (CONV_WIDTH - 1) + j
            y = y + convw_ref[j:j + 1, :] * ext[lo:lo + CONV_TILE, :]
        y = y * jax.nn.sigmoid(y) * kscale_row
        qk_ref[pl.ds(t0, CONV_TILE), :] = y.astype(bf16)
        return 0

    lax.fori_loop(0, S // CONV_TILE, conv_body, 0)

    t_io = lax.broadcasted_iota(jnp.int32, (L, L), 0)
    s_io = lax.broadcasted_iota(jnp.int32, (L, L), 1)
    causal = s_io <= t_io
    tri = causal.astype(f32)

    def chunk_body(k, state):
        new_state = []
        t0 = pl.multiple_of(k * L, L)
        a = small_ref[pl.ds(t0, L), :] + gbias_ref[...]
        lf = _log_sigmoid(a)
        bcum = jnp.dot(tri, lf, precision=lax.Precision.HIGHEST, preferred_element_type=f32)
        aT = a.T
        bcumT = bcum.T
        qk = qk_ref[pl.ds(t0, L), :]
        for hh in range(MLSTM_HEADS):
            i_c = a[:, I_COL + hh:I_COL + hh + 1]
            b_c = bcum[:, F_COL + hh:F_COL + hh + 1]
            i_r = aT[I_COL + hh:I_COL + hh + 1, :]
            b_r = bcumT[F_COL + hh:F_COL + hh + 1, :]
            c_prev, n_prev, m_prev = state[hh]
            q = qk[:, hh * MLSTM_QK_DIM:(hh + 1) * MLSTM_QK_DIM]
            kk = qk[:, MLSTM_QK_WIDTH + hh * MLSTM_QK_DIM:MLSTM_QK_WIDTH + (hh + 1) * MLSTM_QK_DIM]
            v = mv_ref[pl.ds(t0, L), hh * MLSTM_V_DIM:(hh + 1) * MLSTM_V_DIM]

            dmat = jnp.where(causal, b_c - b_r + i_r, NEG_INF)
            m_inter = b_c + m_prev
            m_t = jnp.maximum(m_inter, jnp.max(dmat, axis=1, keepdims=True))
            w = jnp.exp(dmat - m_t)
            a_t = jnp.exp(m_inter - m_t)
            s = _nt(q, kk) * w
            num = _mm(s.astype(bf16), v) + a_t * _mm(q, c_prev.astype(bf16))
            den = (jnp.sum(s, axis=1, keepdims=True)
                   + a_t * jnp.sum(q.astype(f32) * n_prev, axis=1, keepdims=True))
            hval = num / jnp.maximum(jnp.abs(den), jnp.exp(-m_t))

            b_last = b_c[L - 1:L, :]
            ws = b_last - b_c + i_c
            m_next = jnp.maximum(b_last + m_prev, jnp.max(ws, axis=0, keepdims=True))
            decay = jnp.exp(b_last + m_prev - m_next)
            kw = kk.astype(f32) * jnp.exp(ws - m_next)
            new_state.append((decay * c_prev + _mm(kw.T.astype(bf16), v),
                              decay * n_prev + jnp.sum(kw, axis=0, keepdims=True),
                              m_next))

            sl = slice(hh * MLSTM_V_DIM, (hh + 1) * MLSTM_V_DIM)
            hn = hval * lax.rsqrt(jnp.mean(hval * hval, axis=1, keepdims=True) + NORM_EPS) * norm_ref[:, sl]
            gate = jax.nn.sigmoid(mo_ref[pl.ds(t0, L), sl].astype(f32))
            out_ref[pl.ds(t0, L), sl] = (gate * hn).astype(out_ref.dtype)
        return tuple(new_state)

    init = tuple((jnp.zeros((MLSTM_QK_DIM, MLSTM_V_DIM), f32), jnp.zeros((1, MLSTM_QK_DIM), f32),
                  jnp.zeros((1, 1), f32)) for _ in range(MLSTM_HEADS))
    lax.fori_loop(0, S // L, chunk_body, init)


def _mlstm(mqk, mv, mo, small, conv_w, conv_b, gate_bias, mlstm_norm, B, S):
    gbias_row = jnp.zeros((1, LANES), jnp.float32)
    gbias_row = gbias_row.at[0, I_COL:I_COL + MLSTM_HEADS].set(gate_bias[0].astype(jnp.float32))
    gbias_row = gbias_row.at[0, F_COL:F_COL + MLSTM_HEADS].set(gate_bias[1].astype(jnp.float32))
    seq = lambda w: pl.BlockSpec((S, w), lambda b: (b, 0))
    return pl.pallas_call(
        _mlstm_kernel,
        grid=(B,),
        in_specs=[seq(2 * MLSTM_QK_WIDTH), seq(MLSTM_WIDTH), seq(MLSTM_WIDTH), seq(LANES),
                  _full((CONV_WIDTH, 2 * MLSTM_QK_WIDTH)), _full((1, 2 * MLSTM_QK_WIDTH)),
                  _full((1, LANES)), _full((1, MLSTM_WIDTH))],
        out_specs=seq(MLSTM_WIDTH),
        out_shape=jax.ShapeDtypeStruct((B * S, MLSTM_WIDTH), jnp.bfloat16),
        scratch_shapes=[pltpu.VMEM((S, 2 * MLSTM_QK_WIDTH), jnp.bfloat16)],
        compiler_params=_params(1),
        name="mlstm",
    )(mqk, mv, mo, small, conv_w, conv_b.reshape(1, -1), gbias_row, mlstm_norm.reshape(1, -1))


def _mem_kv_kernel(mem_ref, g_ref, w_ref, k_ref, v_ref):
    mn = _rms(mem_ref[...], g_ref[...]).astype(jnp.bfloat16)
    k_ref[...] = _mm(mn, w_ref[:, :D_MODEL]).astype(k_ref.dtype)
    v_ref[...] = _mm(mn, w_ref[:, D_MODEL:]).astype(v_ref.dtype)


def _mem_kv(mem, gain, w_xkv):
    B, M, _ = mem.shape
    spec = pl.BlockSpec((None, M, D_MODEL), lambda b: (b, 0, 0))
    return pl.pallas_call(
        _mem_kv_kernel,
        grid=(B,),
        in_specs=[spec, _full((1, D_MODEL)), _full((D_MODEL, 2 * D_MODEL))],
        out_specs=[spec, spec],
        out_shape=[jax.ShapeDtypeStruct((B, M, D_MODEL), jnp.bfloat16)] * 2,
        compiler_params=_params(1),
        name="mem_kv",
    )(mem, gain, w_xkv.astype(jnp.bfloat16))


TM_X = 512


def _mix_xattn_kernel(ynsa_ref, yml_ref, x_ref, wout_ref, gpost_ref, gpre_ref, wq_ref, k_ref, v_ref,
                      wo_ref, gpost2_ref, out_ref):
    bf16 = jnp.bfloat16
    y = _mm(ynsa_ref[...], wout_ref[:NSA_WIDTH, :]) + _mm(yml_ref[...], wout_ref[NSA_WIDTH:, :])
    x1 = x_ref[...] + _rms(y, gpost_ref[...])
    h2 = _rms(x1, gpre_ref[...]).astype(bf16)
    q = (_mm(h2, wq_ref[...]) * (XATTN_HEAD_DIM ** -0.5)).astype(bf16)
    outs = []
    for hh in range(XATTN_HEADS):
        sl = slice(hh * XATTN_HEAD_DIM, (hh + 1) * XATTN_HEAD_DIM)
        s = _nt(q[:, sl], k_ref[:, sl])
        p = jnp.exp(s - jnp.max(s, axis=1, keepdims=True))
        l = jnp.sum(p, axis=1, keepdims=True)
        outs.append((_mm(p.astype(bf16), v_ref[:, sl]) * (1.0 / l)).astype(bf16))
    o = jnp.concatenate(outs, axis=1)
    y2 = _mm(o, wo_ref[...])
    out_ref[...] = x1 + _rms(y2, gpost2_ref[...])


def _mix_xattn(ynsa, yml, x2d, w_out, g_post, g_pre, w_xq, kx, vx, w_xo, g_post2, B, S):
    nt = S // TM_X
    M = kx.shape[1]
    tok = lambda w: pl.BlockSpec((TM_X, w), lambda b, i: (b * nt + i, 0))
    mem_spec = pl.BlockSpec((None, M, D_MODEL), lambda b, i: (b, 0, 0))
    sq = _full((D_MODEL, D_MODEL))
    row = _full((1, D_MODEL))
    bf = lambda w: w.astype(jnp.bfloat16)
    return pl.pallas_call(
        _mix_xattn_kernel,
        grid=(B, nt),
        in_specs=[tok(NSA_WIDTH), tok(MLSTM_WIDTH), tok(D_MODEL), sq, row, row, sq, mem_spec, mem_spec,
                  sq, row],
        out_specs=tok(D_MODEL),
        out_shape=jax.ShapeDtypeStruct((B * S, D_MODEL), jnp.float32),
        compiler_params=_params(2),
        name="mix_xattn",
    )(ynsa, yml, x2d, bf(w_out), g_post, g_pre, bf(w_xq), kx, vx, bf(w_xo), g_post2)


TM_F = 512


def _ffn_kernel(x_ref, gpre_ref, wg_ref, wu_ref, wd_ref, gpost_ref, out_ref, acc_ref):
    bf16 = jnp.bfloat16
    x = x_ref[...]
    h = _rms(x, gpre_ref[...]).astype(bf16)
    acc_ref[...] = jnp.zeros_like(acc_ref)

    def body(j, _):
        g = _mm(h, wg_ref[j])
        u = _mm(h, wu_ref[j])
        act = (g * jax.nn.sigmoid(g) * u).astype(bf16)
        acc_ref[...] += _mm(act, wd_ref[j])
        return 0

    lax.fori_loop(0, wg_ref.shape[0], body, 0)
    out_ref[...] = x + _rms(acc_ref[...], gpost_ref[...])


def _ffn(x2d, g_pre, w_gate_up, w_down, g_post):
    T = x2d.shape[0]
    nf = D_FF // F_TILE
    wg = w_gate_up[:, :D_FF].reshape(D_MODEL, nf, F_TILE).transpose(1, 0, 2).astype(jnp.bfloat16)
    wu = w_gate_up[:, D_FF:].reshape(D_MODEL, nf, F_TILE).transpose(1, 0, 2).astype(jnp.bfloat16)
    wd = w_down.reshape(nf, F_TILE, D_MODEL).astype(jnp.bfloat16)
    tok = pl.BlockSpec((TM_F, D_MODEL), lambda i: (i, 0))
    row = _full((1, D_MODEL))
    return pl.pallas_call(
        _ffn_kernel,
        grid=(T // TM_F,),
        in_specs=[tok, row, _full((nf, D_MODEL, F_TILE)), _full((nf, D_MODEL, F_TILE)),
                  _full((nf, F_TILE, D_MODEL)), row],
        out_specs=tok,
        out_shape=jax.ShapeDtypeStruct((T, D_MODEL), jnp.float32),
        scratch_shapes=[pltpu.VMEM((TM_F, D_MODEL), jnp.float32)],
        compiler_params=_params(1),
        name="ffn",
    )(x2d, g_pre, wg, wu, wd, g_post)


def _layer(x, mem, rel_bias, mix_norm_pre, w_in, cmp_pos_k, cmp_pos_v, cmp_w1_k, cmp_w2_k, cmp_w1_v,
           cmp_w2_v, conv_w, conv_b, mlstm_gate_bias, mlstm_norm, w_out, mix_norm_post, xattn_norm_pre,
           mem_norm, w_xq, w_xkv, w_xo, xattn_norm_post, ffn_norm_pre, w_gate_up, w_down, ffn_norm_post):
    B, S, _ = x.shape
    row = lambda g: g.reshape(1, -1).astype(jnp.float32)
    x2d = x.reshape(B * S, D_MODEL)
    w_tok, w_feat = _in_proj_weights(w_in)
    (kc, vc, ksl, kwn, mqk, mv, mo, small, qT, vslT, vwnT, smallT) = _in_proj(
        x2d, row(mix_norm_pre), w_tok, w_feat)
    kcmp, vcmpT = _compress(kc, vc, cmp_pos_k, cmp_pos_v, cmp_w1_k, cmp_w2_k, cmp_w1_v, cmp_w2_v, B, S)
    tables = _bias_tables(rel_bias.astype(jnp.float32))
    ynsa = _nsa(qT, kcmp, vcmpT, ksl, vslT, kwn, vwnT, smallT, tables, B, S)
    yml = _mlstm(mqk, mv, mo, small, conv_w, conv_b, mlstm_gate_bias, mlstm_norm, B, S)
    kx, vx = _mem_kv(mem, row(mem_norm), w_xkv)
    x2 = _mix_xattn(ynsa, yml, x2d, w_out, row(mix_norm_post), row(xattn_norm_pre), w_xq, kx, vx, w_xo,
                    row(xattn_norm_post), B, S)
    x3 = _ffn(x2, row(ffn_norm_pre), w_gate_up, w_down, row(ffn_norm_post))
    return x3.reshape(B, S, D_MODEL)


def kernel(x, mem, rel_bias, mix_norm_pre, w_in, cmp_pos_k, cmp_pos_v, cmp_w1_k, cmp_w2_k, cmp_w1_v, cmp_w2_v,
           conv_w, conv_b, mlstm_gate_bias, mlstm_norm, w_out, mix_norm_post, xattn_norm_pre, mem_norm, w_xq,
           w_xkv, w_xo, xattn_norm_post, ffn_norm_pre, w_gate_up, w_down, ffn_norm_post):
    depth = w_in.shape[0]
    for l in range(depth):
        x = _layer(x, mem, rel_bias, mix_norm_pre[l], w_in[l], cmp_pos_k[l], cmp_pos_v[l], cmp_w1_k[l],
                   cmp_w2_k[l], cmp_w1_v[l], cmp_w2_v[l], conv_w[l], conv_b[l], mlstm_gate_bias[l],
                   mlstm_norm[l], w_out[l], mix_norm_post[l], xattn_norm_pre[l], mem_norm[l], w_xq[l],
                   w_xkv[l], w_xo[l], xattn_norm_post[l], ffn_norm_pre[l], w_gate_up[l], w_down[l],
                   ffn_norm_post[l])
    return x
```

```python
import functools
import math

import numpy as np
import jax
import jax.numpy as jnp
from jax import lax
from jax.experimental import pallas as pl
from jax.experimental.pallas import tpu as pltpu

D_MODEL = 1024
NSA_WIDTH = 512
NSA_HEAD_DIM = 64
NSA_HEADS = 8
NSA_KV_HEADS = 2
NSA_GROUP = 4
NSA_KV_WIDTH = 128
CMP_STRIDE = 16
CMP_BLOCK = 32
CMP_HIDDEN = 256
SEL_BLOCK = 64
N_SELECT = 16
WINDOW = 512
Q_BLOCK = 128
FORCED_SCORE = 1.0e4
MLSTM_WIDTH = 512
MLSTM_HEADS = 4
MLSTM_V_DIM = 128
MLSTM_QK_DIM = 64
MLSTM_QK_WIDTH = 256
MLSTM_CHUNK = 64
CONV_WIDTH = 4
REL_BUCKETS = 32
REL_MAX_DISTANCE = 128
XATTN_HEADS = 4
XATTN_HEAD_DIM = 256
D_FF = 2816
NORM_EPS = 1e-6
NEG_INF = -1.0e30
LOG2E = math.log2(math.e)

IN_SIZES = (NSA_WIDTH,) + (NSA_KV_WIDTH,) * 6 + (NSA_HEADS * 3, MLSTM_QK_WIDTH, MLSTM_QK_WIDTH,
                                                 MLSTM_WIDTH, MLSTM_HEADS, MLSTM_HEADS, MLSTM_WIDTH)
IN_OFFSETS = tuple(int(o) for o in np.cumsum((0,) + IN_SIZES)[:-1])

LANES = 128
SUBLANES = 8
VMEM_LIMIT_BYTES = 56 * 1024 * 1024

N_GATE_ROWS = 32
F_TILE = 256


def _rms(x, gain):
    return x * lax.rsqrt(jnp.mean(x * x, axis=-1, keepdims=True) + NORM_EPS) * gain


def _nt(a, b):
    return lax.dot_general(a, b, (((1,), (1,)), ((), ())), preferred_element_type=jnp.float32)


def _mm(a, b):
    return jnp.dot(a, b, preferred_element_type=jnp.float32)


def _params(n_axes, flags=None):
    return pltpu.CompilerParams(dimension_semantics=("arbitrary",) * n_axes,
                                vmem_limit_bytes=VMEM_LIMIT_BYTES, flags=flags)


def _full(shape):
    nd = len(shape)
    return pl.BlockSpec(shape, lambda *_: (0,) * nd)


TM_IN = 512
_TOK_GROUPS = (("kc", 128, jnp.float32), ("vc", 128, jnp.float32), ("ksl", 128, jnp.bfloat16),
               ("kwn", 128, jnp.bfloat16), ("mqk", 512, jnp.bfloat16))
_FEAT_GROUPS = (("qT", 512, jnp.bfloat16), ("vslT", 128, jnp.bfloat16), ("vwnT", 128, jnp.bfloat16),
                ("mvT", 512, jnp.bfloat16), ("moT", 512, jnp.bfloat16), ("smallT", N_GATE_ROWS, jnp.float32))


def _in_proj_kernel(x_ref, g_ref, wtok_ref, wfeat_ref, *out_refs):
    h = _rms(x_ref[...], g_ref[...]).astype(jnp.bfloat16)
    n_tok = len(_TOK_GROUPS)
    off = 0
    for (name, width, dt), o_ref in zip(_TOK_GROUPS, out_refs[:n_tok]):
        o_ref[...] = _mm(h, wtok_ref[:, off:off + width]).astype(dt)
        off += width
    off = 0
    for (name, rows, dt), o_ref in zip(_FEAT_GROUPS, out_refs[n_tok:]):
        r = _nt(wfeat_ref[off:off + rows, :], h)
        if name == "qT":
            r = r * (NSA_HEAD_DIM ** -0.5 * LOG2E)
        for j in range(TM_IN // LANES):
            o_ref[j] = r[:, j * LANES:(j + 1) * LANES].astype(dt)
        off += rows


def _in_proj(x2d, gain, w_tok, w_feat):
    T = x2d.shape[0]
    n_tok_cols = w_tok.shape[1]
    n_feat_rows = w_feat.shape[0]
    out_shape, out_specs = [], []
    for name, width, dt in _TOK_GROUPS:
        out_shape.append(jax.ShapeDtypeStruct((T, width), dt))
        out_specs.append(pl.BlockSpec((TM_IN, width), lambda i: (i, 0)))
    for name, rows, dt in _FEAT_GROUPS:
        out_shape.append(jax.ShapeDtypeStruct((T // LANES, rows, LANES), dt))
        out_specs.append(pl.BlockSpec((TM_IN // LANES, rows, LANES), lambda i: (i, 0, 0)))
    return pl.pallas_call(
        _in_proj_kernel,
        grid=(T // TM_IN,),
        in_specs=[pl.BlockSpec((TM_IN, D_MODEL), lambda i: (i, 0)),
                  _full((1, D_MODEL)),
                  _full((D_MODEL, n_tok_cols)),
                  _full((n_feat_rows, D_MODEL))],
        out_specs=out_specs,
        out_shape=out_shape,
        compiler_params=_params(1),
        name="in_proj",
    )(x2d, gain, w_tok, w_feat)


def _in_proj_weights(w_in):
    (nq, kc, vc, ksl, vsl, kwn, vwn, gt, mq, mk, mv, mi, mf, mo) = [
        w_in[:, o:o + s] for o, s in zip(IN_OFFSETS, IN_SIZES)]
    gt_r = gt.reshape(D_MODEL, NSA_KV_HEADS, NSA_GROUP, 3).transpose(0, 3, 1, 2).reshape(D_MODEL, 24)
    small = jnp.concatenate([gt_r, mi, mf], axis=1)
    w_tok = jnp.concatenate([kc, vc, ksl, kwn, mq, mk], axis=1)
    w_feat = jnp.concatenate([nq, vsl, vwn, mv, mo, small], axis=1).T
    return w_tok.astype(jnp.bfloat16), w_feat.astype(jnp.bfloat16)


N_CHUNK_COLS = CMP_STRIDE * NSA_KV_WIDTH
N_HID2 = NSA_KV_HEADS * CMP_HIDDEN


def _compress_one(c, pos_ref, w1_ref, n_chunks):
    lo = _mm((c + pos_ref[0:1, :]).astype(jnp.bfloat16), w1_ref[0])
    hi = _mm((c + pos_ref[1:2, :]).astype(jnp.bfloat16), w1_ref[1])
    pre = lo + pltpu.roll(hi, n_chunks - 1, 0)
    return (pre * jax.nn.sigmoid(pre)).astype(jnp.bfloat16)


def _compress_kernel(kc_ref, vc_ref, posk_ref, posv_ref, w1k_ref, w1v_ref, w2k_ref, w2vT_ref,
                     kcmp_ref, vcmpT_ref):
    n_chunks = kc_ref.shape[0]
    hid_k = _compress_one(kc_ref[...], posk_ref, w1k_ref, n_chunks)
    kcmp = _mm(hid_k, w2k_ref[...])
    row = lax.broadcasted_iota(jnp.int32, kcmp.shape, 0)
    kcmp_ref[...] = jnp.where(row < n_chunks - 1, kcmp, 0.0).astype(kcmp_ref.dtype)
    hid_v = _compress_one(vc_ref[...], posv_ref, w1v_ref, n_chunks)
    vcmpT = _nt(w2vT_ref[...], hid_v)
    col = lax.broadcasted_iota(jnp.int32, vcmpT.shape, 1)
    vcmpT_ref[...] = jnp.where(col < n_chunks - 1, vcmpT, 0.0).astype(vcmpT_ref.dtype)


def _compress_weights(pos, w1, w2):
    eye = jnp.eye(NSA_KV_HEADS, dtype=w1.dtype)
    w1r = w1.reshape(2, CMP_STRIDE, NSA_HEAD_DIM, CMP_HIDDEN)
    w1e = jnp.einsum('atdj,hg->athdgj', w1r, eye).reshape(2, N_CHUNK_COLS, N_HID2)
    pos_e = jnp.broadcast_to(pos.reshape(2, CMP_STRIDE, 1, NSA_HEAD_DIM),
                             (2, CMP_STRIDE, NSA_KV_HEADS, NSA_HEAD_DIM)).reshape(2, N_CHUNK_COLS)
    w2e = jnp.einsum('jd,hg->hjgd', w2, eye).reshape(N_HID2, NSA_KV_WIDTH)
    return pos_e, w1e.astype(jnp.bfloat16), w2e.astype(jnp.bfloat16)


def _compress(kc, vc, cmp_pos_k, cmp_pos_v, cmp_w1_k, cmp_w2_k, cmp_w1_v, cmp_w2_v, B, S):
    n_chunks = S // CMP_STRIDE
    posk, w1k, w2k = _compress_weights(cmp_pos_k, cmp_w1_k, cmp_w2_k)
    posv, w1v, w2v = _compress_weights(cmp_pos_v, cmp_w1_v, cmp_w2_v)
    kc3 = kc.reshape(B, n_chunks, N_CHUNK_COLS)
    vc3 = vc.reshape(B, n_chunks, N_CHUNK_COLS)
    chunk_spec = pl.BlockSpec((None, n_chunks, N_CHUNK_COLS), lambda b: (b, 0, 0))
    return pl.pallas_call(
        _compress_kernel,
        grid=(B,),
        in_specs=[chunk_spec, chunk_spec,
                  _full((2, N_CHUNK_COLS)), _full((2, N_CHUNK_COLS)),
                  _full((2, N_CHUNK_COLS, N_HID2)), _full((2, N_CHUNK_COLS, N_HID2)),
                  _full((N_HID2, NSA_KV_WIDTH)), _full((NSA_KV_WIDTH, N_HID2))],
        out_specs=[pl.BlockSpec((None, n_chunks, NSA_KV_WIDTH), lambda b: (b, 0, 0)),
                   pl.BlockSpec((None, NSA_KV_WIDTH, n_chunks), lambda b: (b, 0, 0))],
        out_shape=[jax.ShapeDtypeStruct((B, n_chunks, NSA_KV_WIDTH), jnp.bfloat16),
                   jax.ShapeDtypeStruct((B, NSA_KV_WIDTH, n_chunks), jnp.bfloat16)],
        compiler_params=_params(1),
        name="compress",
    )(kc3, vc3, posk, posv, w1k, w1v, w2k, w2v.T)


GQ = NSA_GROUP * Q_BLOCK
TINY = 1e-30
CMP_TAB_ROWS = 512
CMP_TAB_ZERO = 248
CMP_TAB_LOOKUP = (232, 256)
SEL_STEP_SHIFT = 2
KEY_SUPER = Q_BLOCK << SEL_STEP_SHIFT
SEL_TAB_ZERO = KEY_SUPER + Q_BLOCK
SEL_TAB_ROWS = SEL_TAB_ZERO + KEY_SUPER
N_WIN_TILES = WINDOW // Q_BLOCK + 1
V_ROWS = NSA_HEAD_DIM + 16


def _bucket_np(dist):
    n = np.maximum(dist, 0)
    max_exact = REL_BUCKETS // 2
    nf = np.maximum(n, 1).astype(np.float64)
    large = max_exact + (np.log(nf / max_exact) / math.log(REL_MAX_DISTANCE / max_exact)
                         * (REL_BUCKETS - max_exact)).astype(np.int64)
    large = np.minimum(large, REL_BUCKETS - 1)
    return np.where(n < max_exact, n, large).astype(np.int32)


def _bias_index_tables():
    m = np.arange(Q_BLOCK)[:, None]
    r = np.arange(Q_BLOCK)[None, :]
    diag = np.where(r - m >= 0, _bucket_np(r - m), -1).astype(np.int32)
    off = _bucket_np(Q_BLOCK + r - m)
    jp = np.arange(*CMP_TAB_LOOKUP)[:, None] - CMP_TAB_ZERO
    d_c = r - CMP_STRIDE * jp - (CMP_BLOCK - 1)
    cmp_idx = np.where(d_c >= 0, _bucket_np(d_c), -1).astype(np.int32)
    return diag, off, cmp_idx


def _bias_tables_kernel(rb_ref, diag_idx_ref, off_idx_ref, cmp_idx_ref, sel_ref, win_ref, cmp_ref):
    f32 = jnp.float32

    def lookup(idx, head):
        far = rb_ref[head, REL_BUCKETS - 1]
        acc = jnp.full(idx.shape, NEG_INF, f32)
        for k in range(REL_BUCKETS):
            acc = jnp.where(idx == k, (rb_ref[head, k] - far) * LOG2E, acc)
        return acc

    m_io = lax.broadcasted_iota(jnp.int32, (Q_BLOCK, Q_BLOCK), 0)
    r_io = lax.broadcasted_iota(jnp.int32, (Q_BLOCK, Q_BLOCK), 1)
    neg_tile = jnp.full((Q_BLOCK, Q_BLOCK), NEG_INF, f32)
    lo, hi = CMP_TAB_LOOKUP
    for h in range(NSA_KV_HEADS):
        for g in range(NSA_GROUP):
            head = h * NSA_GROUP + g
            sl = slice(g * Q_BLOCK, (g + 1) * Q_BLOCK)
            far = 0.0
            far_tile = jnp.full((Q_BLOCK, Q_BLOCK), far, f32)
            diag_v = lookup(diag_idx_ref[...], head)
            off_v = lookup(off_idx_ref[...], head)
            n_far = (SEL_TAB_ZERO - Q_BLOCK) // Q_BLOCK
            for t in range(SEL_TAB_ROWS // Q_BLOCK):
                rows = slice(t * Q_BLOCK, (t + 1) * Q_BLOCK)
                tile = far_tile if t < n_far else off_v if t == n_far else diag_v if t == n_far + 1 else neg_tile
                sel_ref[h, rows, sl] = tile
            win_ref[h, 0, :, sl] = diag_v
            win_ref[h, 1, :, sl] = off_v
            for back in range(2, N_WIN_TILES - 1):
                win_ref[h, back, :, sl] = far_tile
            win_ref[h, N_WIN_TILES - 1, :, sl] = jnp.where(r_io < m_io, far, NEG_INF)
            win_ref[h, N_WIN_TILES, :, sl] = neg_tile
            cmp_ref[h, 0:lo, sl] = jnp.full((lo, Q_BLOCK), far, f32)
            cmp_ref[h, lo:hi, sl] = lookup(cmp_idx_ref[...], head)
            cmp_ref[h, hi:CMP_TAB_ROWS, sl] = jnp.full((CMP_TAB_ROWS - hi, Q_BLOCK), NEG_INF, f32)


def _bias_tables(rel_bias):
    diag_idx, off_idx, cmp_idx = _bias_index_tables()
    shapes = [(NSA_KV_HEADS, SEL_TAB_ROWS, GQ), (NSA_KV_HEADS, N_WIN_TILES + 1, Q_BLOCK, GQ),
              (NSA_KV_HEADS, CMP_TAB_ROWS, GQ)]
    return pl.pallas_call(
        _bias_tables_kernel,
        in_specs=[pl.BlockSpec(memory_space=pltpu.SMEM),
                  _full(diag_idx.shape), _full(off_idx.shape), _full(cmp_idx.shape)],
        out_specs=[_full(s) for s in shapes],
        out_shape=[jax.ShapeDtypeStruct(s, jnp.float32) for s in shapes],
        grid=(1,),
        compiler_params=_params(1),
        name="bias_tables",
    )(rel_bias, jnp.asarray(diag_idx), jnp.asarray(off_idx), jnp.asarray(cmp_idx))


def _overlap_np(n_cmp_rows, n_sel):
    cmp_start = np.arange(n_cmp_rows) * CMP_STRIDE
    cmp_end = cmp_start + CMP_BLOCK - 1
    sel_start = np.arange(n_sel) * SEL_BLOCK
    ov = ((cmp_start[None, :] <= sel_start[:, None] + SEL_BLOCK - 1)
          & (cmp_end[None, :] >= sel_start[:, None])).astype(np.float32)
    ov[:, n_cmp_rows - 1] = 0.0
    return ov


def _tile4(a):
    return jnp.concatenate([a] * NSA_GROUP, axis=1)


def _select_blocks(score, score_ref, n_top):
    n_sel = score.shape[0]
    score_ref[...] = score
    n_grp = n_sel // SUBLANES
    grp = [score[SUBLANES * v:SUBLANES * (v + 1), :] for v in range(n_grp)]
    cnt = [jnp.zeros((SUBLANES, Q_BLOCK), jnp.int32) for _ in range(n_grp)]
    sub_io = lax.broadcasted_iota(jnp.int32, (SUBLANES, Q_BLOCK), 0)
    for jp in range(n_sel):
        row = score_ref[jp:jp + 1, :]
        for v in range(n_grp):
            if SUBLANES * v > jp:
                inc = (row >= grp[v]).astype(jnp.int32)
            elif SUBLANES * (v + 1) - 1 < jp:
                inc = (row > grp[v]).astype(jnp.int32)
            else:
                tie = (sub_io > jp - SUBLANES * v).astype(jnp.int32)
                inc = jnp.where(row > grp[v], 1, jnp.where(row == grp[v], tie, 0))
            cnt[v] = cnt[v] + inc
    return [jnp.where(cnt[v] < n_top, 0.0, NEG_INF) for v in range(n_grp)]


def _nsa_kernel(q_ref, kcmp_ref, vcmpT_ref, ksl_ref, vslT_ref, kwn_ref, vwnT_ref, gate_ref,
                seltab_ref, wintab_ref, cmptab_ref, ovl_ref, out_ref, score_ref, selb_ref, sbuf_ref):
    c = pl.program_id(1)
    n_cmp = kcmp_ref.shape[0]
    n_sel = ovl_ref.shape[0]
    n_top = min(N_SELECT, n_sel)
    f32 = jnp.float32
    bf16 = jnp.bfloat16
    DH = NSA_HEAD_DIM
    heads = range(NSA_KV_HEADS)

    q = q_ref[...]
    zq = jnp.zeros((DH, GQ), bf16)
    qpad = []
    for h in heads:
        qcat = jnp.concatenate([q[(h * NSA_GROUP + g) * DH:(h * NSA_GROUP + g + 1) * DH, :]
                                for g in range(NSA_GROUP)], axis=1)
        qpad.append(jnp.concatenate([qcat, zq] if h == 0 else [zq, qcat], axis=0))

    cmp_off = pl.multiple_of(CMP_TAB_ZERO - (Q_BLOCK // CMP_STRIDE) * c, SUBLANES)
    kcmp = kcmp_ref[...]
    j_io = lax.broadcasted_iota(jnp.int32, (n_sel, Q_BLOCK), 0)
    r_io = lax.broadcasted_iota(jnp.int32, (n_sel, Q_BLOCK), 1)
    cur = (Q_BLOCK // SEL_BLOCK) * c + (r_io >= SEL_BLOCK).astype(jnp.int32)
    forced = (j_io == 0) | (j_io == cur) | (j_io == cur - 1)
    visible = j_io <= cur
    o_c = []
    for h in heads:
        tab = cmptab_ref[h, pl.ds(cmp_off, n_cmp), :]
        s = _mm(kcmp, qpad[h]) + tab
        m = jnp.max(s, axis=0, keepdims=True)
        p = jnp.where(tab > 0.5 * NEG_INF, jnp.exp2(s - m), 0.0)
        l = jnp.sum(p, axis=0, keepdims=True)
        pn = p * (1.0 / jnp.maximum(l, TINY))
        o_c.append(_mm(vcmpT_ref[h * DH:(h + 1) * DH, :], pn.astype(bf16)))
        psum = pn[:, 0:Q_BLOCK]
        for g in range(1, NSA_GROUP):
            psum = psum + pn[:, g * Q_BLOCK:(g + 1) * Q_BLOCK]
        imp = jnp.dot(ovl_ref[...], psum, precision=lax.Precision.HIGHEST,
                      preferred_element_type=f32)
        score = jnp.where(forced, FORCED_SCORE, jnp.where(visible, imp, -1.0))
        rows = _select_blocks(score, score_ref.at[h], n_top)
        for v, blk in enumerate(rows):
            selb_ref[h, SUBLANES * v:SUBLANES * (v + 1), :] = blk

    def ones_rows(n_keys):
        return jnp.ones((V_ROWS - DH, n_keys), bf16)

    backs = list(range(N_WIN_TILES))
    kts = [jnp.maximum(c - back, 0) for back in backs]
    slots = [jnp.where(c >= back, back, N_WIN_TILES) for back in backs]
    kwin = jnp.concatenate([kwn_ref[pl.ds(pl.multiple_of(kt * Q_BLOCK, Q_BLOCK), Q_BLOCK), :] for kt in kts],
                           axis=0)
    o_w = []
    for h in heads:
        tab = jnp.concatenate([wintab_ref[h, slot] for slot in slots], axis=0)
        s = _mm(kwin, qpad[h]) + tab
        m = jnp.max(s, axis=0, keepdims=True)
        p = jnp.exp2(s - m)
        vT = jnp.concatenate([vwnT_ref[kt, h * DH:(h + 1) * DH, :] for kt in kts], axis=1)
        vT = jnp.concatenate([vT, ones_rows(N_WIN_TILES * Q_BLOCK)], axis=0)
        acc = _mm(vT, p.astype(bf16))
        o_w.append(acc[0:DH, :] * (1.0 / acc[DH:DH + 1, :]))

    blocks_per_step = KEY_SUPER // SEL_BLOCK
    tiles_per_step = KEY_SUPER // Q_BLOCK

    def probs(j, m_new):
        return [jnp.exp2(sbuf_ref[j & 1, h] - m_new[h]).astype(bf16) for h in heads]

    def score_matmuls(j):
        kblk = ksl_ref[pl.ds(pl.multiple_of(j * KEY_SUPER, KEY_SUPER), KEY_SUPER), :]
        return [_mm(kblk, qpad[h]) for h in heads]

    def score_finish(j, qk, m_prev):
        tab_off = pl.multiple_of(jnp.maximum(j * KEY_SUPER - c * Q_BLOCK + SEL_TAB_ZERO, 0), Q_BLOCK)
        m_new = []
        for h in heads:
            selb = jnp.concatenate(
                [jnp.broadcast_to(selb_ref[h, pl.ds(j * blocks_per_step + i, 1), :], (SEL_BLOCK, Q_BLOCK))
                 for i in range(blocks_per_step)], axis=0)
            s = qk[h] + _tile4(selb) + seltab_ref[h, pl.ds(tab_off, KEY_SUPER), :]
            sbuf_ref[j & 1, h] = s
            m_new.append(jnp.maximum(m_prev[h], jnp.max(s, axis=0, keepdims=True)))
        return tuple(m_new)

    def accumulate(j, p, m_old, m_new, acc):
        out = []
        for h in heads:
            vT = jnp.concatenate([vslT_ref[j * tiles_per_step + i, h * DH:(h + 1) * DH, :]
                                  for i in range(tiles_per_step)], axis=1)
            vT = jnp.concatenate([vT, ones_rows(KEY_SUPER)], axis=0)
            out.append(jnp.exp2(m_old[h] - m_new[h]) * acc[h] + _mm(vT, p[h]))
        return tuple(out)

    def sel_body(j, carry):
        m_old, m_cur, acc = carry
        p = probs(j, m_cur)
        qk = score_matmuls(j + 1)
        acc = accumulate(j, p, m_old, m_cur, acc)
        return m_cur, score_finish(j + 1, qk, m_cur), acc

    n_steps = lax.shift_right_logical(c, SEL_STEP_SHIFT) + 1
    m_init = tuple(jnp.full((1, GQ), NEG_INF, f32) for _ in heads)
    acc_init = tuple(jnp.zeros((V_ROWS, GQ), f32) for _ in heads)
    m_first = score_finish(0, score_matmuls(0), m_init)
    m_old, m_cur, acc = lax.fori_loop(0, n_steps - 1, sel_body, (m_init, m_first, acc_init))
    acc = accumulate(n_steps - 1, probs(n_steps - 1, m_cur), m_old, m_cur, acc)
    o_s = [a[0:DH, :] * (1.0 / jnp.maximum(a[DH:DH + 1, :], TINY)) for a in acc]

    for h in heads:
        ys = []
        for g in range(NSA_GROUP):
            sl = slice(g * Q_BLOCK, (g + 1) * Q_BLOCK)
            row0 = h * NSA_GROUP + g
            gates = [jax.nn.sigmoid(gate_ref[kind * NSA_HEADS + row0:kind * NSA_HEADS + row0 + 1, :])
                     for kind in range(3)]
            ys.append(gates[0] * o_c[h][:, sl] + gates[1] * o_s[h][:, sl] + gates[2] * o_w[h][:, sl])
        yT = jnp.concatenate(ys, axis=0)
        for half in range(2):
            col = (2 * h + half) * LANES
            out_ref[:, col:col + LANES] = yT[half * LANES:(half + 1) * LANES, :].T.astype(out_ref.dtype)


def _nsa(qT, kcmp, vcmpT, ksl, vslT, kwn, vwnT, smallT, tables, B, S):
    assert S % KEY_SUPER == 0
    nq = S // Q_BLOCK
    n_cmp = S // CMP_STRIDE
    n_sel = S // SEL_BLOCK
    seltab, wintab, cmptab = tables
    ovl = jnp.asarray(_overlap_np(n_cmp, n_sel))
    ksl3 = ksl.reshape(B, S, NSA_KV_WIDTH)
    kwn3 = kwn.reshape(B, S, NSA_KV_WIDTH)
    vslT4 = vslT.reshape(B, nq, NSA_KV_WIDTH, Q_BLOCK)
    vwnT4 = vwnT.reshape(B, nq, NSA_KV_WIDTH, Q_BLOCK)
    k_spec = pl.BlockSpec((None, S, NSA_KV_WIDTH), lambda b, c: (b, 0, 0))
    vT_spec = pl.BlockSpec((None, nq, NSA_KV_WIDTH, Q_BLOCK), lambda b, c: (b, 0, 0, 0))
    const = lambda a: pl.BlockSpec(a.shape, lambda b, c: (0,) * a.ndim)
    return pl.pallas_call(
        _nsa_kernel,
        grid=(B, nq),
        in_specs=[pl.BlockSpec((None, NSA_WIDTH, Q_BLOCK), lambda b, c: (b * nq + c, 0, 0)),
                  pl.BlockSpec((None, n_cmp, NSA_KV_WIDTH), lambda b, c: (b, 0, 0)),
                  pl.BlockSpec((None, NSA_KV_WIDTH, n_cmp), lambda b, c: (b, 0, 0)),
                  k_spec, vT_spec, k_spec, vT_spec,
                  pl.BlockSpec((None, N_GATE_ROWS, Q_BLOCK), lambda b, c: (b * nq + c, 0, 0)),
                  const(seltab), const(wintab), const(cmptab), const(ovl)],
        out_specs=pl.BlockSpec((Q_BLOCK, NSA_WIDTH), lambda b, c: (b * nq + c, 0)),
        out_shape=jax.ShapeDtypeStruct((B * S, NSA_WIDTH), jnp.bfloat16),
        scratch_shapes=[pltpu.VMEM((NSA_KV_HEADS, n_sel, Q_BLOCK), jnp.float32),
                        pltpu.VMEM((NSA_KV_HEADS, n_sel, Q_BLOCK), jnp.float32),
                        pltpu.VMEM((2, NSA_KV_HEADS, KEY_SUPER, GQ), jnp.float32)],
        compiler_params=_params(2),
        name="nsa",
    )(qT, kcmp, vcmpT, ksl3, vslT4, kwn3, vwnT4, smallT, seltab, wintab, cmptab, ovl)


CONV_TILE = 256
I_ROW = 24
C_ROWS = MLSTM_V_DIM + 16


def _log_sigmoid(x):
    return jnp.minimum(x, 0.0) - jnp.log(1.0 + jnp.exp(-jnp.abs(x)))


def _mlstm_kernel(mqk_ref, mvT_ref, moT_ref, gates_ref, gb_ref, tri_ref, last_ref, convw_ref, convb_ref,
                  norm_ref, out_ref, qk_ref, rows_ref):
    S = mqk_ref.shape[0]
    nt = S // LANES
    L = MLSTM_CHUNK
    f32 = jnp.float32
    bf16 = jnp.bfloat16
    kscale_row = jnp.where(lax.broadcasted_iota(jnp.int32, (1, 2 * MLSTM_QK_WIDTH), 1) < MLSTM_QK_WIDTH,
                           1.0, MLSTM_QK_DIM ** -0.5)

    def conv_body(i, _):
        t0 = pl.multiple_of(i * CONV_TILE, CONV_TILE)
        cur = mqk_ref[pl.ds(t0, CONV_TILE), :].astype(f32)
        prev_start = pl.multiple_of(jnp.maximum(t0 - 16, 0), 16)
        prev = mqk_ref[pl.ds(prev_start, 16), :].astype(f32)[8:16, :] * jnp.where(i > 0, 1.0, 0.0)
        ext = jnp.concatenate([prev, cur], axis=0)
        y = convb_ref[...]
        for j in range(CONV_WIDTH):
            lo = SUBLANES - (CONV_WIDTH - 1) + j
            y = y + convw_ref[j:j + 1, :] * ext[lo:lo + CONV_TILE, :]
        y = y * jax.nn.sigmoid(y) * kscale_row
        qk_ref[pl.ds(t0, CONV_TILE), :] = y.astype(bf16)
        return 0

    lax.fori_loop(0, S // CONV_TILE, conv_body, 0)

    H = MLSTM_HEADS
    G8 = 2 * H
    n_rows = nt * G8
    a3 = gates_ref[:, I_ROW:I_ROW + G8, :] + gb_ref[...][None]
    is_f = lax.broadcasted_iota(jnp.int32, a3.shape, 1) >= H
    x = jnp.where(is_f, _log_sigmoid(a3), a3).reshape(n_rows, LANES)
    bcum = jnp.dot(x, tri_ref[...], precision=lax.Precision.HIGHEST, preferred_element_type=f32)
    b_rows = pltpu.roll(bcum, n_rows - H, 0)
    g_rows = x - b_rows
    pos = lax.broadcasted_iota(jnp.int32, (n_rows, LANES), 1) & (L - 1)

    def chunk_cummax(a):
        shift = 1
        while shift < L:
            a = jnp.where(pos >= shift, jnp.maximum(a, pltpu.roll(a, shift, 1)), a)
            shift *= 2
        return a

    def chunk_last(a):
        return jnp.dot(a, last_ref[...], precision=lax.Precision.HIGHEST, preferred_element_type=f32)

    bl_rows = chunk_last(b_rows)
    rows_ref[0] = g_rows
    rows_ref[1] = chunk_cummax(g_rows)
    rows_ref[2] = b_rows
    rows_ref[3] = bl_rows
    rows_ref[4] = chunk_last(chunk_cummax(bl_rows + g_rows))

    s_io = lax.broadcasted_iota(jnp.int32, (LANES, LANES), 0)
    t_io = lax.broadcasted_iota(jnp.int32, (LANES, LANES), 1)
    causal = (s_io <= t_io) & ((s_io >= L) == (t_io >= L))
    lane = lax.broadcasted_iota(jnp.int32, (1, LANES), 1)
    in_chunk = [lane < L, lane >= L]
    QD, VD = MLSTM_QK_DIM, MLSTM_V_DIM
    ones_aug = jnp.ones((C_ROWS - VD, LANES), bf16)
    head_lanes = [(lax.broadcasted_iota(jnp.int32, (LANES, LANES), 1) >= QD) == bool(par) for par in range(2)]

    def swap_halves(row):
        return pltpu.roll(row, L, 1)

    def tile_body(i, state):
        t0 = pl.multiple_of(i * LANES, LANES)
        r0 = pl.multiple_of(i * G8, G8)
        qk = qk_ref[pl.ds(t0, LANES), :]
        qk32 = qk.astype(f32)
        pairsT = [qk32[:, j * LANES:(j + 1) * LANES].T for j in range(4)]
        g8, cm8, b8, bl8, wm8 = [rows_ref[j, pl.ds(r0, G8), :] for j in range(5)]
        new_state = []
        for hh in range(H):
            pair, par = hh // 2, hh % 2
            g_r, cm_r, b_r, bl_r, wm_r = [a[hh:hh + 1, :] for a in (g8, cm8, b8, bl8, wm8)]
            caug, m_in = state[hh]
            qT = pairsT[pair][par * QD:(par + 1) * QD, :]
            kT = pairsT[2 + pair][par * QD:(par + 1) * QD, :]
            kpair = qk[:, (2 + pair) * LANES:(3 + pair) * LANES]
            qmask = jnp.where(head_lanes[par], qk[:, pair * LANES:(pair + 1) * LANES], jnp.zeros((), bf16))
            vaug = jnp.concatenate([mvT_ref[i, hh * VD:(hh + 1) * VD, :], ones_aug], axis=0)

            m_mid = swap_halves(jnp.maximum(bl_r + m_in, wm_r))
            m_prev = jnp.where(in_chunk[0], m_in, m_mid)
            m_next = jnp.maximum(bl_r + m_prev, wm_r)
            m_intra = b_r + cm_r
            m_inter = b_r + m_prev
            m_t = jnp.maximum(m_inter, m_intra)
            e_intra = jnp.exp(m_intra - m_t)
            e_inter = jnp.exp(m_inter - m_t)
            decay = jnp.exp(bl_r + m_prev - m_next)
            inject = jnp.exp(wm_r - m_next)

            g_mat = jnp.broadcast_to(g_r, (LANES, LANES)).T
            w = jnp.exp(jnp.where(causal, g_mat - cm_r, NEG_INF))
            st = _nt(kpair, qmask) * (w * e_intra)
            y = _mm(vaug, st.astype(bf16))
            kw = kT * jnp.exp(bl_r + g_r - wm_r)
            cs = caug
            for p in range(LANES // L):
                qs = jnp.where(in_chunk[p], qT * e_inter, 0.0).astype(bf16)
                y = y + _mm(cs.astype(bf16), qs)
                u = _nt(vaug, jnp.where(in_chunk[p], kw, 0.0).astype(bf16))
                dec = decay if p == 0 else swap_halves(decay)
                inj = inject if p == 0 else swap_halves(inject)
                cs = dec[:, 0:QD] * cs + inj[:, 0:QD] * u
            m_out = jnp.where(in_chunk[1], m_next, swap_halves(m_next))
            new_state.append((cs, m_out))

            den = y[VD:VD + 1, :]
            hT = y[0:VD, :] * (1.0 / jnp.maximum(jnp.abs(den), jnp.exp(-m_t)))
            sl = slice(hh * VD, (hh + 1) * VD)
            hn = hT * lax.rsqrt(jnp.mean(hT * hT, axis=0, keepdims=True) + NORM_EPS) * norm_ref[sl, :]
            yT = jax.nn.sigmoid(moT_ref[i, sl, :].astype(f32)) * hn
            out_ref[pl.ds(t0, LANES), sl] = yT.T.astype(out_ref.dtype)
        return tuple(new_state)

    init = tuple((jnp.zeros((C_ROWS, QD), f32), jnp.zeros((1, LANES), f32)) for _ in range(H))
    lax.fori_loop(0, nt, tile_body, init)


def _mlstm(mqk, mvT, moT, smallT, conv_w, conv_b, gate_bias, mlstm_norm, B, S):
    nt = S // LANES
    gb = jnp.broadcast_to(gate_bias.astype(jnp.float32).reshape(2 * MLSTM_HEADS, 1), (2 * MLSTM_HEADS, LANES))
    norm_cols = jnp.broadcast_to(mlstm_norm.astype(jnp.float32).reshape(MLSTM_WIDTH, 1), (MLSTM_WIDTH, LANES))
    lane = np.arange(LANES)
    same_chunk = lane[:, None] // MLSTM_CHUNK == lane[None, :] // MLSTM_CHUNK
    tri = (same_chunk & (lane[:, None] <= lane[None, :])).astype(np.float32)
    last = (same_chunk & (lane[:, None] % MLSTM_CHUNK == MLSTM_CHUNK - 1)).astype(np.float32)
    seq = lambda w: pl.BlockSpec((S, w), lambda b: (b, 0))
    tiles = lambda rows: pl.BlockSpec((nt, rows, LANES), lambda b: (b, 0, 0))
    return pl.pallas_call(
        _mlstm_kernel,
        grid=(B,),
        in_specs=[seq(2 * MLSTM_QK_WIDTH), tiles(MLSTM_WIDTH), tiles(MLSTM_WIDTH), tiles(N_GATE_ROWS),
                  _full((2 * MLSTM_HEADS, LANES)), _full((LANES, LANES)), _full((LANES, LANES)),
                  _full((CONV_WIDTH, 2 * MLSTM_QK_WIDTH)), _full((1, 2 * MLSTM_QK_WIDTH)),
                  _full((MLSTM_WIDTH, LANES))],
        out_specs=seq(MLSTM_WIDTH),
        out_shape=jax.ShapeDtypeStruct((B * S, MLSTM_WIDTH), jnp.bfloat16),
        scratch_shapes=[pltpu.VMEM((S, 2 * MLSTM_QK_WIDTH), jnp.bfloat16),
                        pltpu.VMEM((5, nt * 2 * MLSTM_HEADS, LANES), jnp.float32)],
        compiler_params=_params(1),
        name="mlstm",
    )(mqk, mvT, moT, smallT, gb, jnp.asarray(tri), jnp.asarray(last), conv_w, conv_b.reshape(1, -1), norm_cols)


def _mem_kv_kernel(mem_ref, g_ref, w_ref, k_ref, v_ref):
    mn = _rms(mem_ref[...], g_ref[...]).astype(jnp.bfloat16)
    k_ref[...] = _mm(mn, w_ref[:, :D_MODEL]).astype(k_ref.dtype)
    v_ref[...] = _mm(mn, w_ref[:, D_MODEL:]).astype(v_ref.dtype)


def _mem_kv(mem, gain, w_xkv):
    B, M, _ = mem.shape
    spec = pl.BlockSpec((None, M, D_MODEL), lambda b: (b, 0, 0))
    return pl.pallas_call(
        _mem_kv_kernel,
        grid=(B,),
        in_specs=[spec, _full((1, D_MODEL)), _full((D_MODEL, 2 * D_MODEL))],
        out_specs=[spec, spec],
        out_shape=[jax.ShapeDtypeStruct((B, M, D_MODEL), jnp.bfloat16)] * 2,
        compiler_params=_params(1),
        name="mem_kv",
    )(mem, gain, w_xkv.astype(jnp.bfloat16))


TM_X = 512


def _mix_xattn_kernel(ynsa_ref, yml_ref, x_ref, wout_ref, gpost_ref, gpre_ref, wq_ref, k_ref, v_ref,
                      wo_ref, gpost2_ref, out_ref):
    bf16 = jnp.bfloat16
    y = _mm(ynsa_ref[...], wout_ref[:NSA_WIDTH, :]) + _mm(yml_ref[...], wout_ref[NSA_WIDTH:, :])
    x1 = x_ref[...] + _rms(y, gpost_ref[...])
    h2 = _rms(x1, gpre_ref[...]).astype(bf16)
    q = (_mm(h2, wq_ref[...]) * (XATTN_HEAD_DIM ** -0.5)).astype(bf16)
    outs = []
    for hh in range(XATTN_HEADS):
        sl = slice(hh * XATTN_HEAD_DIM, (hh + 1) * XATTN_HEAD_DIM)
        s = _nt(q[:, sl], k_ref[:, sl])
        p = jnp.exp(s - jnp.max(s, axis=1, keepdims=True))
        l = jnp.sum(p, axis=1, keepdims=True)
        outs.append((_mm(p.astype(bf16), v_ref[:, sl]) * (1.0 / l)).astype(bf16))
    o = jnp.concatenate(outs, axis=1)
    y2 = _mm(o, wo_ref[...])
    out_ref[...] = x1 + _rms(y2, gpost2_ref[...])


def _mix_xattn(ynsa, yml, x2d, w_out, g_post, g_pre, w_xq, kx, vx, w_xo, g_post2, B, S):
    nt = S // TM_X
    M = kx.shape[1]
    tok = lambda w: pl.BlockSpec((TM_X, w), lambda b, i: (b * nt + i, 0))
    mem_spec = pl.BlockSpec((None, M, D_MODEL), lambda b, i: (b, 0, 0))
    sq = _full((D_MODEL, D_MODEL))
    row = _full((1, D_MODEL))
    bf = lambda w: w.astype(jnp.bfloat16)
    return pl.pallas_call(
        _mix_xattn_kernel,
        grid=(B, nt),
        in_specs=[tok(NSA_WIDTH), tok(MLSTM_WIDTH), tok(D_MODEL), sq, row, row, sq, mem_spec, mem_spec,
                  sq, row],
        out_specs=tok(D_MODEL),
        out_shape=jax.ShapeDtypeStruct((B * S, D_MODEL), jnp.float32),
        compiler_params=_params(2),
        name="mix_xattn",
    )(ynsa, yml, x2d, bf(w_out), g_post, g_pre, bf(w_xq), kx, vx, bf(w_xo), g_post2)


TM_F = 512


def _ffn_kernel(x_ref, gpre_ref, wg_ref, wu_ref, wd_ref, gpost_ref, out_ref, acc_ref):
    bf16 = jnp.bfloat16
    x = x_ref[...]
    h = _rms(x, gpre_ref[...]).astype(bf16)
    acc_ref[...] = jnp.zeros_like(acc_ref)

    def body(j, _):
        g = _mm(h, wg_ref[j])
        u = _mm(h, wu_ref[j])
        act = (g * jax.nn.sigmoid(g) * u).astype(bf16)
        acc_ref[...] += _mm(act, wd_ref[j])
        return 0

    lax.fori_loop(0, wg_ref.shape[0], body, 0)
    out_ref[...] = x + _rms(acc_ref[...], gpost_ref[...])


def _ffn(x2d, g_pre, w_gate_up, w_down, g_post):
    T = x2d.shape[0]
    nf = D_FF // F_TILE
    wg = w_gate_up[:, :D_FF].reshape(D_MODEL, nf, F_TILE).transpose(1, 0, 2).astype(jnp.bfloat16)
    wu = w_gate_up[:, D_FF:].reshape(D_MODEL, nf, F_TILE).transpose(1, 0, 2).astype(jnp.bfloat16)
    wd = w_down.reshape(nf, F_TILE, D_MODEL).astype(jnp.bfloat16)
    tok = pl.BlockSpec((TM_F, D_MODEL), lambda i: (i, 0))
    row = _full((1, D_MODEL))
    return pl.pallas_call(
        _ffn_kernel,
        grid=(T // TM_F,),
        in_specs=[tok, row, _full((nf, D_MODEL, F_TILE)), _full((nf, D_MODEL, F_TILE)),
                  _full((nf, F_TILE, D_MODEL)), row],
        out_specs=tok,
        out_shape=jax.ShapeDtypeStruct((T, D_MODEL), jnp.float32),
        scratch_shapes=[pltpu.VMEM((TM_F, D_MODEL), jnp.float32)],
        compiler_params=_params(1),
        name="ffn",
    )(x2d, g_pre, wg, wu, wd, g_post)


def _layer(x, mem, rel_bias, mix_norm_pre, w_in, cmp_pos_k, cmp_pos_v, cmp_w1_k, cmp_w2_k, cmp_w1_v,
           cmp_w2_v, conv_w, conv_b, mlstm_gate_bias, mlstm_norm, w_out, mix_norm_post, xattn_norm_pre,
           mem_norm, w_xq, w_xkv, w_xo, xattn_norm_post, ffn_norm_pre, w_gate_up, w_down, ffn_norm_post):
    B, S, _ = x.shape
    row = lambda g: g.reshape(1, -1).astype(jnp.float32)
    x2d = x.reshape(B * S, D_MODEL)
    w_tok, w_feat = _in_proj_weights(w_in)
    (kc, vc, ksl, kwn, mqk, qT, vslT, vwnT, mvT, moT, smallT) = _in_proj(
        x2d, row(mix_norm_pre), w_tok, w_feat)
    kcmp, vcmpT = _compress(kc, vc, cmp_pos_k, cmp_pos_v, cmp_w1_k, cmp_w2_k, cmp_w1_v, cmp_w2_v, B, S)
    tables = _bias_tables(rel_bias.astype(jnp.float32))
    ynsa = _nsa(qT, kcmp, vcmpT, ksl, vslT, kwn, vwnT, smallT, tables, B, S)
    yml = _mlstm(mqk, mvT, moT, smallT, conv_w, conv_b, mlstm_gate_bias, mlstm_norm, B, S)
    kx, vx = _mem_kv(mem, row(mem_norm), w_xkv)
    x2 = _mix_xattn(ynsa, yml, x2d, w_out, row(mix_norm_post), row(xattn_norm_pre), w_xq, kx, vx, w_xo,
                    row(xattn_norm_post), B, S)
    x3 = _ffn(x2, row(ffn_norm_pre), w_gate_up, w_down, row(ffn_norm_post))
    return x3.reshape(B, S, D_MODEL)


def kernel(x, mem, rel_bias, mix_norm_pre, w_in, cmp_pos_k, cmp_pos_v, cmp_w1_k, cmp_w2_k, cmp_w1_v, cmp_w2_v,
           conv_w, conv_b, mlstm_gate_bias, mlstm_norm, w_out, mix_norm_post, xattn_norm_pre, mem_norm, w_xq,
           w_xkv, w_xo, xattn_norm_post, ffn_norm_pre, w_gate_up, w_down, ffn_norm_post):
    depth = w_in.shape[0]
    for l in range(depth):
        x = _layer(x, mem, rel_bias, mix_norm_pre[l], w_in[l], cmp_pos_k[l], cmp_pos_v[l], cmp_w1_k[l],
                   cmp_w2_k[l], cmp_w1_v[l], cmp_w2_v[l], conv_w[l], conv_b[l], mlstm_gate_bias[l],
                   mlstm_norm[l], w_out[l], mix_norm_post[l], xattn_norm_pre[l], mem_norm[l], w_xq[l],
                   w_xkv[l], w_xo[l], xattn_norm_post[l], ffn_norm_pre[l], w_gate_up[l], w_down[l],
                   ffn_norm_post[l])
    return x
```

```python
import functools
import math

import numpy as np
import jax
import jax.numpy as jnp
from jax import lax
from jax.experimental import pallas as pl
from jax.experimental.pallas import tpu as pltpu

D_MODEL = 1024
NSA_WIDTH = 512
NSA_HEAD_DIM = 64
NSA_HEADS = 8
NSA_KV_HEADS = 2
NSA_GROUP = 4
NSA_KV_WIDTH = 128
CMP_STRIDE = 16
CMP_BLOCK = 32
CMP_HIDDEN = 256
SEL_BLOCK = 64
N_SELECT = 16
WINDOW = 512
Q_BLOCK = 128
FORCED_SCORE = 1.0e4
MLSTM_WIDTH = 512
MLSTM_HEADS = 4
MLSTM_V_DIM = 128
MLSTM_QK_DIM = 64
MLSTM_QK_WIDTH = 256
MLSTM_CHUNK = 64
CONV_WIDTH = 4
REL_BUCKETS = 32
REL_MAX_DISTANCE = 128
XATTN_HEADS = 4
XATTN_HEAD_DIM = 256
D_FF = 2816
NORM_EPS = 1e-6
NEG_INF = -1.0e30
LOG2E = math.log2(math.e)

IN_SIZES = (NSA_WIDTH,) + (NSA_KV_WIDTH,) * 6 + (NSA_HEADS * 3, MLSTM_QK_WIDTH, MLSTM_QK_WIDTH,
                                                 MLSTM_WIDTH, MLSTM_HEADS, MLSTM_HEADS, MLSTM_WIDTH)
IN_OFFSETS = tuple(int(o) for o in np.cumsum((0,) + IN_SIZES)[:-1])

LANES = 128
SUBLANES = 8
VMEM_LIMIT_BYTES = 56 * 1024 * 1024

N_GATE_ROWS = 32
F_TILE = 256


def _rms(x, gain):
    return x * lax.rsqrt(jnp.mean(x * x, axis=-1, keepdims=True) + NORM_EPS) * gain


def _nt(a, b):
    return lax.dot_general(a, b, (((1,), (1,)), ((), ())), preferred_element_type=jnp.float32)


def _mm(a, b):
    return jnp.dot(a, b, preferred_element_type=jnp.float32)


def _params(n_axes, flags=None):
    return pltpu.CompilerParams(dimension_semantics=("arbitrary",) * n_axes,
                                vmem_limit_bytes=VMEM_LIMIT_BYTES, flags=flags)


def _full(shape):
    nd = len(shape)
    return pl.BlockSpec(shape, lambda *_: (0,) * nd)


TM_IN = 512
_TOK_GROUPS = (("kc", 128, jnp.float32), ("vc", 128, jnp.float32), ("ksl", 128, jnp.bfloat16),
               ("kwn", 128, jnp.bfloat16), ("mqk", 512, jnp.bfloat16))
_FEAT_GROUPS = (("qT", 512, jnp.bfloat16), ("vslT", 128, jnp.bfloat16), ("vwnT", 128, jnp.bfloat16),
                ("mvT", 512, jnp.bfloat16), ("moT", 512, jnp.bfloat16), ("smallT", N_GATE_ROWS, jnp.float32))


def _in_proj_kernel(x_ref, g_ref, wtok_ref, wfeat_ref, *out_refs):
    h = _rms(x_ref[...], g_ref[...]).astype(jnp.bfloat16)
    n_tok = len(_TOK_GROUPS)
    off = 0
    for (name, width, dt), o_ref in zip(_TOK_GROUPS, out_refs[:n_tok]):
        o_ref[...] = _mm(h, wtok_ref[:, off:off + width]).astype(dt)
        off += width
    off = 0
    for (name, rows, dt), o_ref in zip(_FEAT_GROUPS, out_refs[n_tok:]):
        r = _nt(wfeat_ref[off:off + rows, :], h)
        if name == "qT":
            r = r * (NSA_HEAD_DIM ** -0.5 * LOG2E)
        for j in range(TM_IN // LANES):
            o_ref[j] = r[:, j * LANES:(j + 1) * LANES].astype(dt)
        off += rows


def _in_proj(x2d, gain, w_tok, w_feat):
    T = x2d.shape[0]
    n_tok_cols = w_tok.shape[1]
    n_feat_rows = w_feat.shape[0]
    out_shape, out_specs = [], []
    for name, width, dt in _TOK_GROUPS:
        out_shape.append(jax.ShapeDtypeStruct((T, width), dt))
        out_specs.append(pl.BlockSpec((TM_IN, width), lambda i: (i, 0)))
    for name, rows, dt in _FEAT_GROUPS:
        out_shape.append(jax.ShapeDtypeStruct((T // LANES, rows, LANES), dt))
        out_specs.append(pl.BlockSpec((TM_IN // LANES, rows, LANES), lambda i: (i, 0, 0)))
    return pl.pallas_call(
        _in_proj_kernel,
        grid=(T // TM_IN,),
        in_specs=[pl.BlockSpec((TM_IN, D_MODEL), lambda i: (i, 0)),
                  _full((1, D_MODEL)),
                  _full((D_MODEL, n_tok_cols)),
                  _full((n_feat_rows, D_MODEL))],
        out_specs=out_specs,
        out_shape=out_shape,
        compiler_params=_params(1),
        name="in_proj",
    )(x2d, gain, w_tok, w_feat)


def _in_proj_weights(w_in):
    (nq, kc, vc, ksl, vsl, kwn, vwn, gt, mq, mk, mv, mi, mf, mo) = [
        w_in[:, o:o + s] for o, s in zip(IN_OFFSETS, IN_SIZES)]
    gt_r = gt.reshape(D_MODEL, NSA_KV_HEADS, NSA_GROUP, 3).transpose(0, 3, 1, 2).reshape(D_MODEL, 24)
    small = jnp.concatenate([gt_r, mi, mf], axis=1)
    w_tok = jnp.concatenate([kc, vc, ksl, kwn, mq, mk], axis=1)
    w_feat = jnp.concatenate([nq, vsl, vwn, mv, mo, small], axis=1).T
    return w_tok.astype(jnp.bfloat16), w_feat.astype(jnp.bfloat16)


N_CHUNK_COLS = CMP_STRIDE * NSA_KV_WIDTH
N_HID2 = NSA_KV_HEADS * CMP_HIDDEN


def _compress_one(c, pos_ref, w1_ref, n_chunks):
    lo = _mm((c + pos_ref[0:1, :]).astype(jnp.bfloat16), w1_ref[0])
    hi = _mm((c + pos_ref[1:2, :]).astype(jnp.bfloat16), w1_ref[1])
    pre = lo + pltpu.roll(hi, n_chunks - 1, 0)
    return (pre * jax.nn.sigmoid(pre)).astype(jnp.bfloat16)


def _chunk_rows(ref, n_chunks):
    return jnp.concatenate([ref[pl.ds(t, n_chunks, stride=CMP_STRIDE), :] for t in range(CMP_STRIDE)], axis=1)


def _compress_kernel(kc_ref, vc_ref, posk_ref, posv_ref, w1k_ref, w1v_ref, w2k_ref, w2vT_ref,
                     kcmp_ref, vcmpT_ref):
    n_chunks = kc_ref.shape[0] // CMP_STRIDE
    hid_k = _compress_one(_chunk_rows(kc_ref, n_chunks), posk_ref, w1k_ref, n_chunks)
    kcmp = _mm(hid_k, w2k_ref[...])
    row = lax.broadcasted_iota(jnp.int32, kcmp.shape, 0)
    kcmp_ref[...] = jnp.where(row < n_chunks - 1, kcmp, 0.0).astype(kcmp_ref.dtype)
    hid_v = _compress_one(_chunk_rows(vc_ref, n_chunks), posv_ref, w1v_ref, n_chunks)
    vcmpT = _nt(w2vT_ref[...], hid_v)
    col = lax.broadcasted_iota(jnp.int32, vcmpT.shape, 1)
    vcmpT_ref[...] = jnp.where(col < n_chunks - 1, vcmpT, 0.0).astype(vcmpT_ref.dtype)


def _compress_weights(pos, w1, w2):
    eye = jnp.eye(NSA_KV_HEADS, dtype=w1.dtype)
    w1r = w1.reshape(2, CMP_STRIDE, NSA_HEAD_DIM, CMP_HIDDEN)
    w1e = jnp.einsum('atdj,hg->athdgj', w1r, eye).reshape(2, N_CHUNK_COLS, N_HID2)
    pos_e = jnp.broadcast_to(pos.reshape(2, CMP_STRIDE, 1, NSA_HEAD_DIM),
                             (2, CMP_STRIDE, NSA_KV_HEADS, NSA_HEAD_DIM)).reshape(2, N_CHUNK_COLS)
    w2e = jnp.einsum('jd,hg->hjgd', w2, eye).reshape(N_HID2, NSA_KV_WIDTH)
    return pos_e, w1e.astype(jnp.bfloat16), w2e.astype(jnp.bfloat16)


def _compress(kc, vc, cmp_pos_k, cmp_pos_v, cmp_w1_k, cmp_w2_k, cmp_w1_v, cmp_w2_v, B, S):
    n_chunks = S // CMP_STRIDE
    posk, w1k, w2k = _compress_weights(cmp_pos_k, cmp_w1_k, cmp_w2_k)
    posv, w1v, w2v = _compress_weights(cmp_pos_v, cmp_w1_v, cmp_w2_v)
    chunk_spec = pl.BlockSpec((S, NSA_KV_WIDTH), lambda b: (b, 0))
    return pl.pallas_call(
        _compress_kernel,
        grid=(B,),
        in_specs=[chunk_spec, chunk_spec,
                  _full((2, N_CHUNK_COLS)), _full((2, N_CHUNK_COLS)),
                  _full((2, N_CHUNK_COLS, N_HID2)), _full((2, N_CHUNK_COLS, N_HID2)),
                  _full((N_HID2, NSA_KV_WIDTH)), _full((NSA_KV_WIDTH, N_HID2))],
        out_specs=[pl.BlockSpec((None, n_chunks, NSA_KV_WIDTH), lambda b: (b, 0, 0)),
                   pl.BlockSpec((None, NSA_KV_WIDTH, n_chunks), lambda b: (b, 0, 0))],
        out_shape=[jax.ShapeDtypeStruct((B, n_chunks, NSA_KV_WIDTH), jnp.bfloat16),
                   jax.ShapeDtypeStruct((B, NSA_KV_WIDTH, n_chunks), jnp.bfloat16)],
        compiler_params=_params(1),
        name="compress",
    )(kc, vc, posk, posv, w1k, w1v, w2k, w2v.T)


GQ = NSA_GROUP * Q_BLOCK
TINY = 1e-30
CMP_TAB_ROWS = 512
CMP_TAB_ZERO = 248
CMP_TAB_LOOKUP = (232, 256)
SEL_STEP_SHIFT = 2
KEY_SUPER = Q_BLOCK << SEL_STEP_SHIFT
SEL_TAB_ZERO = KEY_SUPER + Q_BLOCK
SEL_TAB_ROWS = SEL_TAB_ZERO + KEY_SUPER
N_WIN_TILES = WINDOW // Q_BLOCK + 1
V_ROWS = NSA_HEAD_DIM + 16


def _bucket_np(dist):
    n = np.maximum(dist, 0)
    max_exact = REL_BUCKETS // 2
    nf = np.maximum(n, 1).astype(np.float64)
    large = max_exact + (np.log(nf / max_exact) / math.log(REL_MAX_DISTANCE / max_exact)
                         * (REL_BUCKETS - max_exact)).astype(np.int64)
    large = np.minimum(large, REL_BUCKETS - 1)
    return np.where(n < max_exact, n, large).astype(np.int32)


def _bias_index_tables():
    m = np.arange(Q_BLOCK)[:, None]
    r = np.arange(Q_BLOCK)[None, :]
    diag = np.where(r - m >= 0, _bucket_np(r - m), -1).astype(np.int32)
    off = _bucket_np(Q_BLOCK + r - m)
    jp = np.arange(*CMP_TAB_LOOKUP)[:, None] - CMP_TAB_ZERO
    d_c = r - CMP_STRIDE * jp - (CMP_BLOCK - 1)
    cmp_idx = np.where(d_c >= 0, _bucket_np(d_c), -1).astype(np.int32)
    return diag, off, cmp_idx


def _bias_tables_kernel(rb_ref, diag_idx_ref, off_idx_ref, cmp_idx_ref, sel_ref, win_ref, cmp_ref):
    f32 = jnp.float32

    def lookup(idx, head):
        far = rb_ref[head, REL_BUCKETS - 1]
        acc = jnp.full(idx.shape, NEG_INF, f32)
        for k in range(REL_BUCKETS):
            acc = jnp.where(idx == k, (rb_ref[head, k] - far) * LOG2E, acc)
        return acc

    m_io = lax.broadcasted_iota(jnp.int32, (Q_BLOCK, Q_BLOCK), 0)
    r_io = lax.broadcasted_iota(jnp.int32, (Q_BLOCK, Q_BLOCK), 1)
    neg_tile = jnp.full((Q_BLOCK, Q_BLOCK), NEG_INF, f32)
    lo, hi = CMP_TAB_LOOKUP
    for h in range(NSA_KV_HEADS):
        for g in range(NSA_GROUP):
            head = h * NSA_GROUP + g
            sl = slice(g * Q_BLOCK, (g + 1) * Q_BLOCK)
            far = 0.0
            far_tile = jnp.full((Q_BLOCK, Q_BLOCK), far, f32)
            diag_v = lookup(diag_idx_ref[...], head)
            off_v = lookup(off_idx_ref[...], head)
            n_far = (SEL_TAB_ZERO - Q_BLOCK) // Q_BLOCK
            for t in range(SEL_TAB_ROWS // Q_BLOCK):
                rows = slice(t * Q_BLOCK, (t + 1) * Q_BLOCK)
                tile = far_tile if t < n_far else off_v if t == n_far else diag_v if t == n_far + 1 else neg_tile
                sel_ref[h, rows, sl] = tile
            win_ref[h, 0, :, sl] = diag_v
            win_ref[h, 1, :, sl] = off_v
            for back in range(2, N_WIN_TILES - 1):
                win_ref[h, back, :, sl] = far_tile
            win_ref[h, N_WIN_TILES - 1, :, sl] = jnp.where(r_io < m_io, far, NEG_INF)
            win_ref[h, N_WIN_TILES, :, sl] = neg_tile
            cmp_ref[h, 0:lo, sl] = jnp.full((lo, Q_BLOCK), far, f32)
            cmp_ref[h, lo:hi, sl] = lookup(cmp_idx_ref[...], head)
            cmp_ref[h, hi:CMP_TAB_ROWS, sl] = jnp.full((CMP_TAB_ROWS - hi, Q_BLOCK), NEG_INF, f32)


def _bias_tables(rel_bias):
    diag_idx, off_idx, cmp_idx = _bias_index_tables()
    shapes = [(NSA_KV_HEADS, SEL_TAB_ROWS, GQ), (NSA_KV_HEADS, N_WIN_TILES + 1, Q_BLOCK, GQ),
              (NSA_KV_HEADS, CMP_TAB_ROWS, GQ)]
    return pl.pallas_call(
        _bias_tables_kernel,
        in_specs=[pl.BlockSpec(memory_space=pltpu.SMEM),
                  _full(diag_idx.shape), _full(off_idx.shape), _full(cmp_idx.shape)],
        out_specs=[_full(s) for s in shapes],
        out_shape=[jax.ShapeDtypeStruct(s, jnp.float32) for s in shapes],
        grid=(1,),
        compiler_params=_params(1),
        name="bias_tables",
    )(rel_bias, jnp.asarray(diag_idx), jnp.asarray(off_idx), jnp.asarray(cmp_idx))


def _overlap_np(n_cmp_rows, n_sel):
    cmp_start = np.arange(n_cmp_rows) * CMP_STRIDE
    cmp_end = cmp_start + CMP_BLOCK - 1
    sel_start = np.arange(n_sel) * SEL_BLOCK
    ov = ((cmp_start[None, :] <= sel_start[:, None] + SEL_BLOCK - 1)
          & (cmp_end[None, :] >= sel_start[:, None])).astype(np.float32)
    ov[:, n_cmp_rows - 1] = 0.0
    return ov


def _tile4(a):
    return jnp.concatenate([a] * NSA_GROUP, axis=1)


def _select_blocks(score, score_ref, n_top):
    n_sel = score.shape[0]
    score_ref[...] = score
    n_grp = n_sel // SUBLANES
    grp = [score[SUBLANES * v:SUBLANES * (v + 1), :] for v in range(n_grp)]
    cnt = [jnp.zeros((SUBLANES, Q_BLOCK), jnp.int32) for _ in range(n_grp)]
    sub_io = lax.broadcasted_iota(jnp.int32, (SUBLANES, Q_BLOCK), 0)
    for jp in range(n_sel):
        row = score_ref[jp:jp + 1, :]
        for v in range(n_grp):
            if SUBLANES * v > jp:
                inc = (row >= grp[v]).astype(jnp.int32)
            elif SUBLANES * (v + 1) - 1 < jp:
                inc = (row > grp[v]).astype(jnp.int32)
            else:
                tie = (sub_io > jp - SUBLANES * v).astype(jnp.int32)
                inc = jnp.where(row > grp[v], 1, jnp.where(row == grp[v], tie, 0))
            cnt[v] = cnt[v] + inc
    return [jnp.where(cnt[v] < n_top, 0.0, NEG_INF) for v in range(n_grp)]


def _nsa_kernel(q_ref, kcmp_ref, vcmpT_ref, ksl_ref, vslT_ref, kwn_ref, vwnT_ref, gate_ref,
                seltab_ref, wintab_ref, cmptab_ref, ovl_ref, out_ref, score_ref, selb_ref, sbuf_ref):
    c = pl.program_id(1)
    n_cmp = kcmp_ref.shape[0]
    n_sel = ovl_ref.shape[0]
    n_top = min(N_SELECT, n_sel)
    f32 = jnp.float32
    bf16 = jnp.bfloat16
    DH = NSA_HEAD_DIM
    heads = range(NSA_KV_HEADS)

    q = q_ref[...]
    zq = jnp.zeros((DH, GQ), bf16)
    qpad = []
    for h in heads:
        qcat = jnp.concatenate([q[(h * NSA_GROUP + g) * DH:(h * NSA_GROUP + g + 1) * DH, :]
                                for g in range(NSA_GROUP)], axis=1)
        qpad.append(jnp.concatenate([qcat, zq] if h == 0 else [zq, qcat], axis=0))

    cmp_off = pl.multiple_of(CMP_TAB_ZERO - (Q_BLOCK // CMP_STRIDE) * c, SUBLANES)
    kcmp = kcmp_ref[...]
    j_io = lax.broadcasted_iota(jnp.int32, (n_sel, Q_BLOCK), 0)
    r_io = lax.broadcasted_iota(jnp.int32, (n_sel, Q_BLOCK), 1)
    cur = (Q_BLOCK // SEL_BLOCK) * c + (r_io >= SEL_BLOCK).astype(jnp.int32)
    forced = (j_io == 0) | (j_io == cur) | (j_io == cur - 1)
    visible = j_io <= cur
    o_c = []
    for h in heads:
        tab = cmptab_ref[h, pl.ds(cmp_off, n_cmp), :]
        s = _mm(kcmp, qpad[h]) + tab
        m = jnp.max(s, axis=0, keepdims=True)
        p = jnp.where(tab > 0.5 * NEG_INF, jnp.exp2(s - m), 0.0)
        l = jnp.sum(p, axis=0, keepdims=True)
        pn = p * (1.0 / jnp.maximum(l, TINY))
        o_c.append(_mm(vcmpT_ref[h * DH:(h + 1) * DH, :], pn.astype(bf16)))
        psum = pn[:, 0:Q_BLOCK]
        for g in range(1, NSA_GROUP):
            psum = psum + pn[:, g * Q_BLOCK:(g + 1) * Q_BLOCK]
        imp = jnp.dot(ovl_ref[...], psum, precision=lax.Precision.HIGHEST,
                      preferred_element_type=f32)
        score = jnp.where(forced, FORCED_SCORE, jnp.where(visible, imp, -1.0))
        rows = _select_blocks(score, score_ref.at[h], n_top)
        for v, blk in enumerate(rows):
            selb_ref[h, SUBLANES * v:SUBLANES * (v + 1), :] = blk

    def ones_rows(n_keys):
        return jnp.ones((V_ROWS - DH, n_keys), bf16)

    backs = list(range(N_WIN_TILES))
    kts = [jnp.maximum(c - back, 0) for back in backs]
    slots = [jnp.where(c >= back, back, N_WIN_TILES) for back in backs]
    kwin = jnp.concatenate([kwn_ref[pl.ds(pl.multiple_of(kt * Q_BLOCK, Q_BLOCK), Q_BLOCK), :] for kt in kts],
                           axis=0)
    o_w = []
    for h in heads:
        tab = jnp.concatenate([wintab_ref[h, slot] for slot in slots], axis=0)
        s = _mm(kwin, qpad[h]) + tab
        m = jnp.max(s, axis=0, keepdims=True)
        p = jnp.exp2(s - m)
        vT = jnp.concatenate([vwnT_ref[kt, h * DH:(h + 1) * DH, :] for kt in kts], axis=1)
        vT = jnp.concatenate([vT, ones_rows(N_WIN_TILES * Q_BLOCK)], axis=0)
        acc = _mm(vT, p.astype(bf16))
        o_w.append(acc[0:DH, :] * (1.0 / acc[DH:DH + 1, :]))

    blocks_per_step = KEY_SUPER // SEL_BLOCK
    tiles_per_step = KEY_SUPER // Q_BLOCK

    def probs(j, m_new):
        return [jnp.exp2(sbuf_ref[j & 1, h] - m_new[h]).astype(bf16) for h in heads]

    def score_matmuls(j):
        kblk = ksl_ref[pl.ds(pl.multiple_of(j * KEY_SUPER, KEY_SUPER), KEY_SUPER), :]
        return [_mm(kblk, qpad[h]) for h in heads]

    def score_finish(j, qk, m_prev):
        tab_off = pl.multiple_of(jnp.maximum(j * KEY_SUPER - c * Q_BLOCK + SEL_TAB_ZERO, 0), Q_BLOCK)
        m_new = []
        for h in heads:
            selb = jnp.concatenate(
                [jnp.broadcast_to(selb_ref[h, pl.ds(j * blocks_per_step + i, 1), :], (SEL_BLOCK, Q_BLOCK))
                 for i in range(blocks_per_step)], axis=0)
            s = qk[h] + _tile4(selb) + seltab_ref[h, pl.ds(tab_off, KEY_SUPER), :]
            sbuf_ref[j & 1, h] = s
            m_new.append(jnp.maximum(m_prev[h], jnp.max(s, axis=0, keepdims=True)))
        return tuple(m_new)

    def accumulate(j, p, m_old, m_new, acc):
        out = []
        for h in heads:
            vT = jnp.concatenate([vslT_ref[j * tiles_per_step + i, h * DH:(h + 1) * DH, :]
                                  for i in range(tiles_per_step)], axis=1)
            vT = jnp.concatenate([vT, ones_rows(KEY_SUPER)], axis=0)
            out.append(jnp.exp2(m_old[h] - m_new[h]) * acc[h] + _mm(vT, p[h]))
        return tuple(out)

    def sel_body(j, carry):
        m_old, m_cur, acc = carry
        p = probs(j, m_cur)
        qk = score_matmuls(j + 1)
        acc = accumulate(j, p, m_old, m_cur, acc)
        return m_cur, score_finish(j + 1, qk, m_cur), acc

    n_steps = lax.shift_right_logical(c, SEL_STEP_SHIFT) + 1
    m_init = tuple(jnp.full((1, GQ), NEG_INF, f32) for _ in heads)
    acc_init = tuple(jnp.zeros((V_ROWS, GQ), f32) for _ in heads)
    m_first = score_finish(0, score_matmuls(0), m_init)
    m_old, m_cur, acc = lax.fori_loop(0, n_steps - 1, sel_body, (m_init, m_first, acc_init))
    acc = accumulate(n_steps - 1, probs(n_steps - 1, m_cur), m_old, m_cur, acc)
    o_s = [a[0:DH, :] * (1.0 / jnp.maximum(a[DH:DH + 1, :], TINY)) for a in acc]

    for h in heads:
        ys = []
        for g in range(NSA_GROUP):
            sl = slice(g * Q_BLOCK, (g + 1) * Q_BLOCK)
            row0 = h * NSA_GROUP + g
            gates = [jax.nn.sigmoid(gate_ref[kind * NSA_HEADS + row0:kind * NSA_HEADS + row0 + 1, :])
                     for kind in range(3)]
            ys.append(gates[0] * o_c[h][:, sl] + gates[1] * o_s[h][:, sl] + gates[2] * o_w[h][:, sl])
        yT = jnp.concatenate(ys, axis=0)
        for half in range(2):
            col = (2 * h + half) * LANES
            out_ref[:, col:col + LANES] = yT[half * LANES:(half + 1) * LANES, :].T.astype(out_ref.dtype)


def _nsa(qT, kcmp, vcmpT, ksl, vslT, kwn, vwnT, smallT, tables, B, S):
    assert S % KEY_SUPER == 0
    nq = S // Q_BLOCK
    n_cmp = S // CMP_STRIDE
    n_sel = S // SEL_BLOCK
    seltab, wintab, cmptab = tables
    ovl = jnp.asarray(_overlap_np(n_cmp, n_sel))
    ksl3 = ksl.reshape(B, S, NSA_KV_WIDTH)
    kwn3 = kwn.reshape(B, S, NSA_KV_WIDTH)
    vslT4 = vslT.reshape(B, nq, NSA_KV_WIDTH, Q_BLOCK)
    vwnT4 = vwnT.reshape(B, nq, NSA_KV_WIDTH, Q_BLOCK)
    k_spec = pl.BlockSpec((None, S, NSA_KV_WIDTH), lambda b, c: (b, 0, 0))
    vT_spec = pl.BlockSpec((None, nq, NSA_KV_WIDTH, Q_BLOCK), lambda b, c: (b, 0, 0, 0))
    const = lambda a: pl.BlockSpec(a.shape, lambda b, c: (0,) * a.ndim)
    return pl.pallas_call(
        _nsa_kernel,
        grid=(B, nq),
        in_specs=[pl.BlockSpec((None, NSA_WIDTH, Q_BLOCK), lambda b, c: (b * nq + c, 0, 0)),
                  pl.BlockSpec((None, n_cmp, NSA_KV_WIDTH), lambda b, c: (b, 0, 0)),
                  pl.BlockSpec((None, NSA_KV_WIDTH, n_cmp), lambda b, c: (b, 0, 0)),
                  k_spec, vT_spec, k_spec, vT_spec,
                  pl.BlockSpec((None, N_GATE_ROWS, Q_BLOCK), lambda b, c: (b * nq + c, 0, 0)),
                  const(seltab), const(wintab), const(cmptab), const(ovl)],
        out_specs=pl.BlockSpec((Q_BLOCK, NSA_WIDTH), lambda b, c: (b * nq + c, 0)),
        out_shape=jax.ShapeDtypeStruct((B * S, NSA_WIDTH), jnp.bfloat16),
        scratch_shapes=[pltpu.VMEM((NSA_KV_HEADS, n_sel, Q_BLOCK), jnp.float32),
                        pltpu.VMEM((NSA_KV_HEADS, n_sel, Q_BLOCK), jnp.float32),
                        pltpu.VMEM((2, NSA_KV_HEADS, KEY_SUPER, GQ), jnp.float32)],
        compiler_params=_params(2),
        name="nsa",
    )(qT, kcmp, vcmpT, ksl3, vslT4, kwn3, vwnT4, smallT, seltab, wintab, cmptab, ovl)


CONV_TILE = 256
I_ROW = 24
C_ROWS = MLSTM_V_DIM + 16


def _log_sigmoid(x):
    return jnp.minimum(x, 0.0) - jnp.log(1.0 + jnp.exp(-jnp.abs(x)))


def _mlstm_kernel(mqk_ref, mvT_ref, moT_ref, gates_ref, gb_ref, tri_ref, last_ref, convw_ref, convb_ref,
                  norm_ref, out_ref, qk_ref, rows_ref):
    S = mqk_ref.shape[0]
    nt = S // LANES
    L = MLSTM_CHUNK
    f32 = jnp.float32
    bf16 = jnp.bfloat16
    kscale_row = jnp.where(lax.broadcasted_iota(jnp.int32, (1, 2 * MLSTM_QK_WIDTH), 1) < MLSTM_QK_WIDTH,
                           1.0, MLSTM_QK_DIM ** -0.5)

    def conv_body(i, _):
        t0 = pl.multiple_of(i * CONV_TILE, CONV_TILE)
        cur = mqk_ref[pl.ds(t0, CONV_TILE), :].astype(f32)
        prev_start = pl.multiple_of(jnp.maximum(t0 - 16, 0), 16)
        prev = mqk_ref[pl.ds(prev_start, 16), :].astype(f32)[8:16, :] * jnp.where(i > 0, 1.0, 0.0)
        ext = jnp.concatenate([prev, cur], axis=0)
        y = convb_ref[...]
        for j in range(CONV_WIDTH):
            lo = SUBLANES - (CONV_WIDTH - 1) + j
            y = y + convw_ref[j:j + 1, :] * ext[lo:lo + CONV_TILE, :]
        y = y * jax.nn.sigmoid(y) * kscale_row
        qk_ref[pl.ds(t0, CONV_TILE), :] = y.astype(bf16)
        return 0

    lax.fori_loop(0, S // CONV_TILE, conv_body, 0)

    H = MLSTM_HEADS
    G8 = 2 * H
    n_rows = nt * G8
    a3 = gates_ref[:, I_ROW:I_ROW + G8, :] + gb_ref[...][None]
    is_f = lax.broadcasted_iota(jnp.int32, a3.shape, 1) >= H
    x = jnp.where(is_f, _log_sigmoid(a3), a3).reshape(n_rows, LANES)
    bcum = jnp.dot(x, tri_ref[...], precision=lax.Precision.HIGHEST, preferred_element_type=f32)
    b_rows = pltpu.roll(bcum, n_rows - H, 0)
    g_rows = x - b_rows
    pos = lax.broadcasted_iota(jnp.int32, (n_rows, LANES), 1) & (L - 1)

    def chunk_cummax(a):
        shift = 1
        while shift < L:
            a = jnp.where(pos >= shift, jnp.maximum(a, pltpu.roll(a, shift, 1)), a)
            shift *= 2
        return a

    def chunk_last(a):
        return jnp.dot(a, last_ref[...], precision=lax.Precision.HIGHEST, preferred_element_type=f32)

    bl_rows = chunk_last(b_rows)
    rows_ref[0] = g_rows
    rows_ref[1] = chunk_cummax(g_rows)
    rows_ref[2] = b_rows
    rows_ref[3] = bl_rows
    rows_ref[4] = chunk_last(chunk_cummax(bl_rows + g_rows))

    s_io = lax.broadcasted_iota(jnp.int32, (LANES, LANES), 0)
    t_io = lax.broadcasted_iota(jnp.int32, (LANES, LANES), 1)
    causal = (s_io <= t_io) & ((s_io >= L) == (t_io >= L))
    lane = lax.broadcasted_iota(jnp.int32, (1, LANES), 1)
    in_chunk = [lane < L, lane >= L]
    QD, VD = MLSTM_QK_DIM, MLSTM_V_DIM
    ones_aug = jnp.ones((C_ROWS - VD, LANES), bf16)
    head_lanes = [(lax.broadcasted_iota(jnp.int32, (LANES, LANES), 1) >= QD) == bool(par) for par in range(2)]

    def swap_halves(row):
        return pltpu.roll(row, L, 1)

    def tile_body(i, state):
        t0 = pl.multiple_of(i * LANES, LANES)
        r0 = pl.multiple_of(i * G8, G8)
        qk = qk_ref[pl.ds(t0, LANES), :]
        qk32 = qk.astype(f32)
        pairsT = [qk32[:, j * LANES:(j + 1) * LANES].T for j in range(4)]
        g8, cm8, b8, bl8, wm8 = [rows_ref[j, pl.ds(r0, G8), :] for j in range(5)]
        new_state = []
        for hh in range(H):
            pair, par = hh // 2, hh % 2
            g_r, cm_r, b_r, bl_r, wm_r = [a[hh:hh + 1, :] for a in (g8, cm8, b8, bl8, wm8)]
            caug, m_in = state[hh]
            qT = pairsT[pair][par * QD:(par + 1) * QD, :]
            kT = pairsT[2 + pair][par * QD:(par + 1) * QD, :]
            kpair = qk[:, (2 + pair) * LANES:(3 + pair) * LANES]
            qmask = jnp.where(head_lanes[par], qk[:, pair * LANES:(pair + 1) * LANES], jnp.zeros((), bf16))
            vaug = jnp.concatenate([mvT_ref[i, hh * VD:(hh + 1) * VD, :], ones_aug], axis=0)

            m_mid = swap_halves(jnp.maximum(bl_r + m_in, wm_r))
            m_prev = jnp.where(in_chunk[0], m_in, m_mid)
            m_next = jnp.maximum(bl_r + m_prev, wm_r)
            m_intra = b_r + cm_r
            m_inter = b_r + m_prev
            m_t = jnp.maximum(m_inter, m_intra)
            e_intra = jnp.exp(m_intra - m_t)
            e_inter = jnp.exp(m_inter - m_t)
            decay = jnp.exp(bl_r + m_prev - m_next)
            inject = jnp.exp(wm_r - m_next)

            g_mat = jnp.broadcast_to(g_r, (LANES, LANES)).T
            w = jnp.exp(jnp.where(causal, g_mat - cm_r, NEG_INF))
            st = _nt(kpair, qmask) * (w * e_intra)
            y = _mm(vaug, st.astype(bf16))
            kw = kT * jnp.exp(bl_r + g_r - wm_r)
            cs = caug
            for p in range(LANES // L):
                qs = jnp.where(in_chunk[p], qT * e_inter, 0.0).astype(bf16)
                y = y + _mm(cs.astype(bf16), qs)
                u = _nt(vaug, jnp.where(in_chunk[p], kw, 0.0).astype(bf16))
                dec = decay if p == 0 else swap_halves(decay)
                inj = inject if p == 0 else swap_halves(inject)
                cs = dec[:, 0:QD] * cs + inj[:, 0:QD] * u
            m_out = jnp.where(in_chunk[1], m_next, swap_halves(m_next))
            new_state.append((cs, m_out))

            den = y[VD:VD + 1, :]
            hT = y[0:VD, :] * (1.0 / jnp.maximum(jnp.abs(den), jnp.exp(-m_t)))
            sl = slice(hh * VD, (hh + 1) * VD)
            hn = hT * lax.rsqrt(jnp.mean(hT * hT, axis=0, keepdims=True) + NORM_EPS) * norm_ref[sl, :]
            yT = jax.nn.sigmoid(moT_ref[i, sl, :].astype(f32)) * hn
            out_ref[pl.ds(t0, LANES), sl] = yT.T.astype(out_ref.dtype)
        return tuple(new_state)

    init = tuple((jnp.zeros((C_ROWS, QD), f32), jnp.zeros((1, LANES), f32)) for _ in range(H))
    lax.fori_loop(0, nt, tile_body, init)


def _mlstm(mqk, mvT, moT, smallT, conv_w, conv_b, gate_bias, mlstm_norm, B, S):
    nt = S // LANES
    gb = jnp.broadcast_to(gate_bias.astype(jnp.float32).reshape(2 * MLSTM_HEADS, 1), (2 * MLSTM_HEADS, LANES))
    norm_cols = jnp.broadcast_to(mlstm_norm.astype(jnp.float32).reshape(MLSTM_WIDTH, 1), (MLSTM_WIDTH, LANES))
    lane = np.arange(LANES)
    same_chunk = lane[:, None] // MLSTM_CHUNK == lane[None, :] // MLSTM_CHUNK
    tri = (same_chunk & (lane[:, None] <= lane[None, :])).astype(np.float32)
    last = (same_chunk & (lane[:, None] % MLSTM_CHUNK == MLSTM_CHUNK - 1)).astype(np.float32)
    seq = lambda w: pl.BlockSpec((S, w), lambda b: (b, 0))
    tiles = lambda rows: pl.BlockSpec((nt, rows, LANES), lambda b: (b, 0, 0))
    return pl.pallas_call(
        _mlstm_kernel,
        grid=(B,),
        in_specs=[seq(2 * MLSTM_QK_WIDTH), tiles(MLSTM_WIDTH), tiles(MLSTM_WIDTH), tiles(N_GATE_ROWS),
                  _full((2 * MLSTM_HEADS, LANES)), _full((LANES, LANES)), _full((LANES, LANES)),
                  _full((CONV_WIDTH, 2 * MLSTM_QK_WIDTH)), _full((1, 2 * MLSTM_QK_WIDTH)),
                  _full((MLSTM_WIDTH, LANES))],
        out_specs=seq(MLSTM_WIDTH),
        out_shape=jax.ShapeDtypeStruct((B * S, MLSTM_WIDTH), jnp.bfloat16),
        scratch_shapes=[pltpu.VMEM((S, 2 * MLSTM_QK_WIDTH), jnp.bfloat16),
                        pltpu.VMEM((5, nt * 2 * MLSTM_HEADS, LANES), jnp.float32)],
        compiler_params=_params(1),
        name="mlstm",
    )(mqk, mvT, moT, smallT, gb, jnp.asarray(tri), jnp.asarray(last), conv_w, conv_b.reshape(1, -1), norm_cols)


def _mem_kv_kernel(mem_ref, g_ref, w_ref, k_ref, v_ref):
    mn = _rms(mem_ref[...], g_ref[...]).astype(jnp.bfloat16)
    k_ref[...] = _mm(mn, w_ref[:, :D_MODEL]).astype(k_ref.dtype)
    v_ref[...] = _mm(mn, w_ref[:, D_MODEL:]).astype(v_ref.dtype)


def _mem_kv(mem, gain, w_xkv):
    B, M, _ = mem.shape
    spec = pl.BlockSpec((None, M, D_MODEL), lambda b: (b, 0, 0))
    return pl.pallas_call(
        _mem_kv_kernel,
        grid=(B,),
        in_specs=[spec, _full((1, D_MODEL)), _full((D_MODEL, 2 * D_MODEL))],
        out_specs=[spec, spec],
        out_shape=[jax.ShapeDtypeStruct((B, M, D_MODEL), jnp.bfloat16)] * 2,
        compiler_params=_params(1),
        name="mem_kv",
    )(mem, gain, w_xkv.astype(jnp.bfloat16))


TM_X = 512


def _mix_xattn_kernel(ynsa_ref, yml_ref, x_ref, wout_ref, gpost_ref, gpre_ref, wq_ref, k_ref, v_ref,
                      wo_ref, gpost2_ref, out_ref):
    bf16 = jnp.bfloat16
    y = _mm(ynsa_ref[...], wout_ref[:NSA_WIDTH, :]) + _mm(yml_ref[...], wout_ref[NSA_WIDTH:, :])
    x1 = x_ref[...] + _rms(y, gpost_ref[...])
    h2 = _rms(x1, gpre_ref[...]).astype(bf16)
    q = (_mm(h2, wq_ref[...]) * (XATTN_HEAD_DIM ** -0.5)).astype(bf16)
    outs = []
    for hh in range(XATTN_HEADS):
        sl = slice(hh * XATTN_HEAD_DIM, (hh + 1) * XATTN_HEAD_DIM)
        s = _nt(q[:, sl], k_ref[:, sl])
        p = jnp.exp(s - jnp.max(s, axis=1, keepdims=True))
        l = jnp.sum(p, axis=1, keepdims=True)
        outs.append((_mm(p.astype(bf16), v_ref[:, sl]) * (1.0 / l)).astype(bf16))
    o = jnp.concatenate(outs, axis=1)
    y2 = _mm(o, wo_ref[...])
    out_ref[...] = x1 + _rms(y2, gpost2_ref[...])


def _mix_xattn(ynsa, yml, x2d, w_out, g_post, g_pre, w_xq, kx, vx, w_xo, g_post2, B, S):
    nt = S // TM_X
    M = kx.shape[1]
    tok = lambda w: pl.BlockSpec((TM_X, w), lambda b, i: (b * nt + i, 0))
    mem_spec = pl.BlockSpec((None, M, D_MODEL), lambda b, i: (b, 0, 0))
    sq = _full((D_MODEL, D_MODEL))
    row = _full((1, D_MODEL))
    bf = lambda w: w.astype(jnp.bfloat16)
    return pl.pallas_call(
        _mix_xattn_kernel,
        grid=(B, nt),
        in_specs=[tok(NSA_WIDTH), tok(MLSTM_WIDTH), tok(D_MODEL), sq, row, row, sq, mem_spec, mem_spec,
                  sq, row],
        out_specs=tok(D_MODEL),
        out_shape=jax.ShapeDtypeStruct((B * S, D_MODEL), jnp.float32),
        compiler_params=_params(2),
        name="mix_xattn",
    )(ynsa, yml, x2d, bf(w_out), g_post, g_pre, bf(w_xq), kx, vx, bf(w_xo), g_post2)


TM_F = 512


def _ffn_kernel(x_ref, gpre_ref, wgu_ref, wd_ref, gpost_ref, out_ref, acc_ref):
    bf16 = jnp.bfloat16
    x = x_ref[...]
    h = _rms(x, gpre_ref[...]).astype(bf16)
    acc_ref[...] = jnp.zeros_like(acc_ref)
    for j in range(D_FF // F_TILE):
        cols = slice(j * F_TILE, (j + 1) * F_TILE)
        g = _mm(h, wgu_ref[:, cols])
        u = _mm(h, wgu_ref[:, D_FF + j * F_TILE:D_FF + (j + 1) * F_TILE])
        act = (g * jax.nn.sigmoid(g) * u).astype(bf16)
        acc_ref[...] += _mm(act, wd_ref[cols, :])
    out_ref[...] = x + _rms(acc_ref[...], gpost_ref[...])


def _ffn(x2d, g_pre, w_gate_up, w_down, g_post):
    T = x2d.shape[0]
    tok = pl.BlockSpec((TM_F, D_MODEL), lambda i: (i, 0))
    row = _full((1, D_MODEL))
    return pl.pallas_call(
        _ffn_kernel,
        grid=(T // TM_F,),
        in_specs=[tok, row, _full((D_MODEL, 2 * D_FF)), _full((D_FF, D_MODEL)), row],
        out_specs=tok,
        out_shape=jax.ShapeDtypeStruct((T, D_MODEL), jnp.float32),
        scratch_shapes=[pltpu.VMEM((TM_F, D_MODEL), jnp.float32)],
        compiler_params=_params(1),
        name="ffn",
    )(x2d, g_pre, w_gate_up.astype(jnp.bfloat16), w_down.astype(jnp.bfloat16), g_post)


def _layer(x, mem, rel_bias, mix_norm_pre, w_in, cmp_pos_k, cmp_pos_v, cmp_w1_k, cmp_w2_k, cmp_w1_v,
           cmp_w2_v, conv_w, conv_b, mlstm_gate_bias, mlstm_norm, w_out, mix_norm_post, xattn_norm_pre,
           mem_norm, w_xq, w_xkv, w_xo, xattn_norm_post, ffn_norm_pre, w_gate_up, w_down, ffn_norm_post):
    B, S, _ = x.shape
    row = lambda g: g.reshape(1, -1).astype(jnp.float32)
    x2d = x.reshape(B * S, D_MODEL)
    w_tok, w_feat = _in_proj_weights(w_in)
    (kc, vc, ksl, kwn, mqk, qT, vslT, vwnT, mvT, moT, smallT) = _in_proj(
        x2d, row(mix_norm_pre), w_tok, w_feat)
    kcmp, vcmpT = _compress(kc, vc, cmp_pos_k, cmp_pos_v, cmp_w1_k, cmp_w2_k, cmp_w1_v, cmp_w2_v, B, S)
    tables = _bias_tables(rel_bias.astype(jnp.float32))
    ynsa = _nsa(qT, kcmp, vcmpT, ksl, vslT, kwn, vwnT, smallT, tables, B, S)
    yml = _mlstm(mqk, mvT, moT, smallT, conv_w, conv_b, mlstm_gate_bias, mlstm_norm, B, S)
    kx, vx = _mem_kv(mem, row(mem_norm), w_xkv)
    x2 = _mix_xattn(ynsa, yml, x2d, w_out, row(mix_norm_post), row(xattn_norm_pre), w_xq, kx, vx, w_xo,
                    row(xattn_norm_post), B, S)
    x3 = _ffn(x2, row(ffn_norm_pre), w_gate_up, w_down, row(ffn_norm_post))
    return x3.reshape(B, S, D_MODEL)


def kernel(x, mem, rel_bias, mix_norm_pre, w_in, cmp_pos_k, cmp_pos_v, cmp_w1_k, cmp_w2_k, cmp_w1_v, cmp_w2_v,
           conv_w, conv_b, mlstm_gate_bias, mlstm_norm, w_out, mix_norm_post, xattn_norm_pre, mem_norm, w_xq,
           w_xkv, w_xo, xattn_norm_post, ffn_norm_pre, w_gate_up, w_down, ffn_norm_post):
    depth = w_in.shape[0]
    for l in range(depth):
        x = _layer(x, mem, rel_bias, mix_norm_pre[l], w_in[l], cmp_pos_k[l], cmp_pos_v[l], cmp_w1_k[l],
                   cmp_w2_k[l], cmp_w1_v[l], cmp_w2_v[l], conv_w[l], conv_b[l], mlstm_gate_bias[l],
                   mlstm_norm[l], w_out[l], mix_norm_post[l], xattn_norm_pre[l], mem_norm[l], w_xq[l],
                   w_xkv[l], w_xo[l], xattn_norm_post[l], ffn_norm_pre[l], w_gate_up[l], w_down[l],
                   ffn_norm_post[l])
    return x
```

```python
import functools
import math

import numpy as np
import jax
import jax.numpy as jnp
from jax import lax
from jax.experimental import pallas as pl
from jax.experimental.pallas import tpu as pltpu

D_MODEL = 1024
NSA_WIDTH = 512
NSA_HEAD_DIM = 64
NSA_HEADS = 8
NSA_KV_HEADS = 2
NSA_GROUP = 4
NSA_KV_WIDTH = 128
CMP_STRIDE = 16
CMP_BLOCK = 32
CMP_HIDDEN = 256
SEL_BLOCK = 64
N_SELECT = 16
WINDOW = 512
Q_BLOCK = 128
FORCED_SCORE = 1.0e4
MLSTM_WIDTH = 512
MLSTM_HEADS = 4
MLSTM_V_DIM = 128
MLSTM_QK_DIM = 64
MLSTM_QK_WIDTH = 256
MLSTM_CHUNK = 64
CONV_WIDTH = 4
REL_BUCKETS = 32
REL_MAX_DISTANCE = 128
XATTN_HEADS = 4
XATTN_HEAD_DIM = 256
D_FF = 2816
NORM_EPS = 1e-6
NEG_INF = -1.0e30
LOG2E = math.log2(math.e)

IN_SIZES = (NSA_WIDTH,) + (NSA_KV_WIDTH,) * 6 + (NSA_HEADS * 3, MLSTM_QK_WIDTH, MLSTM_QK_WIDTH,
                                                 MLSTM_WIDTH, MLSTM_HEADS, MLSTM_HEADS, MLSTM_WIDTH)
IN_OFFSETS = tuple(int(o) for o in np.cumsum((0,) + IN_SIZES)[:-1])

LANES = 128
SUBLANES = 8
VMEM_LIMIT_BYTES = 56 * 1024 * 1024

N_GATE_ROWS = 32
F_TILE = 256


def _rms(x, gain):
    return x * lax.rsqrt(jnp.mean(x * x, axis=-1, keepdims=True) + NORM_EPS) * gain


def _nt(a, b):
    return lax.dot_general(a, b, (((1,), (1,)), ((), ())), preferred_element_type=jnp.float32)


def _mm(a, b):
    return jnp.dot(a, b, preferred_element_type=jnp.float32)


def _params(n_axes, flags=None):
    return pltpu.CompilerParams(dimension_semantics=("arbitrary",) * n_axes,
                                vmem_limit_bytes=VMEM_LIMIT_BYTES, flags=flags)


def _full(shape):
    nd = len(shape)
    return pl.BlockSpec(shape, lambda *_: (0,) * nd)


TM_IN = 512
_TOK_GROUPS = (("kc", 128, jnp.float32), ("vc", 128, jnp.float32), ("ksl", 128, jnp.bfloat16),
               ("kwn", 128, jnp.bfloat16), ("mqk", 512, jnp.bfloat16))
_FEAT_GROUPS = (("qT", 512, jnp.bfloat16), ("vslT", 128, jnp.bfloat16), ("vwnT", 128, jnp.bfloat16),
                ("mvT", 512, jnp.bfloat16), ("moT", 512, jnp.bfloat16), ("smallT", N_GATE_ROWS, jnp.float32))


def _in_proj_kernel(x_ref, g_ref, wtok_ref, wfeat_ref, *out_refs):
    h = _rms(x_ref[...], g_ref[...]).astype(jnp.bfloat16)
    n_tok = len(_TOK_GROUPS)
    off = 0
    for (name, width, dt), o_ref in zip(_TOK_GROUPS, out_refs[:n_tok]):
        o_ref[...] = _mm(h, wtok_ref[:, off:off + width]).astype(dt)
        off += width
    off = 0
    for (name, rows, dt), o_ref in zip(_FEAT_GROUPS, out_refs[n_tok:]):
        r = _nt(wfeat_ref[off:off + rows, :], h)
        if name == "qT":
            r = r * (NSA_HEAD_DIM ** -0.5 * LOG2E)
        for j in range(TM_IN // LANES):
            o_ref[j] = r[:, j * LANES:(j + 1) * LANES].astype(dt)
        off += rows


def _in_proj(x2d, gain, w_tok, w_feat):
    T = x2d.shape[0]
    n_tok_cols = w_tok.shape[1]
    n_feat_rows = w_feat.shape[0]
    out_shape, out_specs = [], []
    for name, width, dt in _TOK_GROUPS:
        out_shape.append(jax.ShapeDtypeStruct((T, width), dt))
        out_specs.append(pl.BlockSpec((TM_IN, width), lambda i: (i, 0)))
    for name, rows, dt in _FEAT_GROUPS:
        out_shape.append(jax.ShapeDtypeStruct((T // LANES, rows, LANES), dt))
        out_specs.append(pl.BlockSpec((TM_IN // LANES, rows, LANES), lambda i: (i, 0, 0)))
    return pl.pallas_call(
        _in_proj_kernel,
        grid=(T // TM_IN,),
        in_specs=[pl.BlockSpec((TM_IN, D_MODEL), lambda i: (i, 0)),
                  _full((1, D_MODEL)),
                  _full((D_MODEL, n_tok_cols)),
                  _full((n_feat_rows, D_MODEL))],
        out_specs=out_specs,
        out_shape=out_shape,
        compiler_params=_params(1),
        name="in_proj",
    )(x2d, gain, w_tok, w_feat)


def _in_proj_weights(w_in):
    (nq, kc, vc, ksl, vsl, kwn, vwn, gt, mq, mk, mv, mi, mf, mo) = [
        w_in[:, o:o + s] for o, s in zip(IN_OFFSETS, IN_SIZES)]
    gt_r = gt.reshape(D_MODEL, NSA_KV_HEADS, NSA_GROUP, 3).transpose(0, 3, 1, 2).reshape(D_MODEL, 24)
    small = jnp.concatenate([gt_r, mi, mf], axis=1)
    w_tok = jnp.concatenate([kc, vc, ksl, kwn, mq, mk], axis=1)
    w_feat = jnp.concatenate([nq, vsl, vwn, mv, mo, small], axis=1).T
    return w_tok.astype(jnp.bfloat16), w_feat.astype(jnp.bfloat16)


N_CHUNK_COLS = CMP_STRIDE * NSA_KV_WIDTH
N_HID2 = NSA_KV_HEADS * CMP_HIDDEN


def _compress_one(c, pos_ref, w1_ref, n_chunks):
    lo = _mm((c + pos_ref[0:1, :]).astype(jnp.bfloat16), w1_ref[0])
    hi = _mm((c + pos_ref[1:2, :]).astype(jnp.bfloat16), w1_ref[1])
    pre = lo + pltpu.roll(hi, n_chunks - 1, 0)
    return (pre * jax.nn.sigmoid(pre)).astype(jnp.bfloat16)


def _chunk_rows(ref, n_chunks):
    return jnp.concatenate([ref[pl.ds(t, n_chunks, stride=CMP_STRIDE), :] for t in range(CMP_STRIDE)], axis=1)


def _compress_kernel(kc_ref, vc_ref, posk_ref, posv_ref, w1k_ref, w1v_ref, w2k_ref, w2vT_ref,
                     kcmp_ref, vcmpT_ref):
    n_chunks = kc_ref.shape[0] // CMP_STRIDE
    hid_k = _compress_one(_chunk_rows(kc_ref, n_chunks), posk_ref, w1k_ref, n_chunks)
    kcmp = _mm(hid_k, w2k_ref[...])
    row = lax.broadcasted_iota(jnp.int32, kcmp.shape, 0)
    kcmp_ref[...] = jnp.where(row < n_chunks - 1, kcmp, 0.0).astype(kcmp_ref.dtype)
    hid_v = _compress_one(_chunk_rows(vc_ref, n_chunks), posv_ref, w1v_ref, n_chunks)
    vcmpT = _nt(w2vT_ref[...], hid_v)
    col = lax.broadcasted_iota(jnp.int32, vcmpT.shape, 1)
    vcmpT_ref[...] = jnp.where(col < n_chunks - 1, vcmpT, 0.0).astype(vcmpT_ref.dtype)


def _compress_weights(pos, w1, w2):
    eye = jnp.eye(NSA_KV_HEADS, dtype=w1.dtype)
    w1r = w1.reshape(2, CMP_STRIDE, NSA_HEAD_DIM, CMP_HIDDEN)
    w1e = jnp.einsum('atdj,hg->athdgj', w1r, eye).reshape(2, N_CHUNK_COLS, N_HID2)
    pos_e = jnp.broadcast_to(pos.reshape(2, CMP_STRIDE, 1, NSA_HEAD_DIM),
                             (2, CMP_STRIDE, NSA_KV_HEADS, NSA_HEAD_DIM)).reshape(2, N_CHUNK_COLS)
    w2e = jnp.einsum('jd,hg->hjgd', w2, eye).reshape(N_HID2, NSA_KV_WIDTH)
    return pos_e, w1e.astype(jnp.bfloat16), w2e.astype(jnp.bfloat16)


def _compress(kc, vc, cmp_pos_k, cmp_pos_v, cmp_w1_k, cmp_w2_k, cmp_w1_v, cmp_w2_v, B, S):
    n_chunks = S // CMP_STRIDE
    posk, w1k, w2k = _compress_weights(cmp_pos_k, cmp_w1_k, cmp_w2_k)
    posv, w1v, w2v = _compress_weights(cmp_pos_v, cmp_w1_v, cmp_w2_v)
    chunk_spec = pl.BlockSpec((S, NSA_KV_WIDTH), lambda b: (b, 0))
    return pl.pallas_call(
        _compress_kernel,
        grid=(B,),
        in_specs=[chunk_spec, chunk_spec,
                  _full((2, N_CHUNK_COLS)), _full((2, N_CHUNK_COLS)),
                  _full((2, N_CHUNK_COLS, N_HID2)), _full((2, N_CHUNK_COLS, N_HID2)),
                  _full((N_HID2, NSA_KV_WIDTH)), _full((NSA_KV_WIDTH, N_HID2))],
        out_specs=[pl.BlockSpec((None, n_chunks, NSA_KV_WIDTH), lambda b: (b, 0, 0)),
                   pl.BlockSpec((None, NSA_KV_WIDTH, n_chunks), lambda b: (b, 0, 0))],
        out_shape=[jax.ShapeDtypeStruct((B, n_chunks, NSA_KV_WIDTH), jnp.bfloat16),
                   jax.ShapeDtypeStruct((B, NSA_KV_WIDTH, n_chunks), jnp.bfloat16)],
        compiler_params=_params(1),
        name="compress",
    )(kc, vc, posk, posv, w1k, w1v, w2k, w2v.T)


GQ = NSA_GROUP * Q_BLOCK
TINY = 1e-30
CMP_TAB_ROWS = 512
CMP_TAB_ZERO = 248
CMP_TAB_LOOKUP = (232, 256)
SEL_STEP_SHIFT = 2
KEY_SUPER = Q_BLOCK << SEL_STEP_SHIFT
SEL_TAB_ZERO = KEY_SUPER + Q_BLOCK
SEL_TAB_ROWS = SEL_TAB_ZERO + KEY_SUPER
N_WIN_TILES = WINDOW // Q_BLOCK + 1
SEL_SUB_CHUNKS = 2
V_ROWS = NSA_HEAD_DIM + 16


def _bucket_np(dist):
    n = np.maximum(dist, 0)
    max_exact = REL_BUCKETS // 2
    nf = np.maximum(n, 1).astype(np.float64)
    large = max_exact + (np.log(nf / max_exact) / math.log(REL_MAX_DISTANCE / max_exact)
                         * (REL_BUCKETS - max_exact)).astype(np.int64)
    large = np.minimum(large, REL_BUCKETS - 1)
    return np.where(n < max_exact, n, large).astype(np.int32)


def _bias_index_tables():
    m = np.arange(Q_BLOCK)[:, None]
    r = np.arange(Q_BLOCK)[None, :]
    diag = np.where(r - m >= 0, _bucket_np(r - m), -1).astype(np.int32)
    off = _bucket_np(Q_BLOCK + r - m)
    jp = np.arange(*CMP_TAB_LOOKUP)[:, None] - CMP_TAB_ZERO
    d_c = r - CMP_STRIDE * jp - (CMP_BLOCK - 1)
    cmp_idx = np.where(d_c >= 0, _bucket_np(d_c), -1).astype(np.int32)
    return diag, off, cmp_idx


def _bias_tables_kernel(rb_ref, diag_idx_ref, off_idx_ref, cmp_idx_ref, sel_ref, win_ref, cmp_ref):
    f32 = jnp.float32

    def lookup(idx, head):
        far = rb_ref[head, REL_BUCKETS - 1]
        acc = jnp.full(idx.shape, NEG_INF, f32)
        for k in range(REL_BUCKETS):
            acc = jnp.where(idx == k, (rb_ref[head, k] - far) * LOG2E, acc)
        return acc

    m_io = lax.broadcasted_iota(jnp.int32, (Q_BLOCK, Q_BLOCK), 0)
    r_io = lax.broadcasted_iota(jnp.int32, (Q_BLOCK, Q_BLOCK), 1)
    neg_tile = jnp.full((Q_BLOCK, Q_BLOCK), NEG_INF, f32)
    lo, hi = CMP_TAB_LOOKUP
    for h in range(NSA_KV_HEADS):
        for g in range(NSA_GROUP):
            head = h * NSA_GROUP + g
            sl = slice(g * Q_BLOCK, (g + 1) * Q_BLOCK)
            far = 0.0
            far_tile = jnp.full((Q_BLOCK, Q_BLOCK), far, f32)
            diag_v = lookup(diag_idx_ref[...], head)
            off_v = lookup(off_idx_ref[...], head)
            n_far = (SEL_TAB_ZERO - Q_BLOCK) // Q_BLOCK
            for t in range(SEL_TAB_ROWS // Q_BLOCK):
                rows = slice(t * Q_BLOCK, (t + 1) * Q_BLOCK)
                tile = far_tile if t < n_far else off_v if t == n_far else diag_v if t == n_far + 1 else neg_tile
                sel_ref[h, rows, sl] = tile
            win_ref[h, 0, :, sl] = diag_v
            win_ref[h, 1, :, sl] = off_v
            for back in range(2, N_WIN_TILES - 1):
                win_ref[h, back, :, sl] = far_tile
            win_ref[h, N_WIN_TILES - 1, :, sl] = jnp.where(r_io < m_io, far, NEG_INF)
            win_ref[h, N_WIN_TILES, :, sl] = neg_tile
            cmp_ref[h, 0:lo, sl] = jnp.full((lo, Q_BLOCK), far, f32)
            cmp_ref[h, lo:hi, sl] = lookup(cmp_idx_ref[...], head)
            cmp_ref[h, hi:CMP_TAB_ROWS, sl] = jnp.full((CMP_TAB_ROWS - hi, Q_BLOCK), NEG_INF, f32)


def _bias_tables(rel_bias):
    diag_idx, off_idx, cmp_idx = _bias_index_tables()
    shapes = [(NSA_KV_HEADS, SEL_TAB_ROWS, GQ), (NSA_KV_HEADS, N_WIN_TILES + 1, Q_BLOCK, GQ),
              (NSA_KV_HEADS, CMP_TAB_ROWS, GQ)]
    return pl.pallas_call(
        _bias_tables_kernel,
        in_specs=[pl.BlockSpec(memory_space=pltpu.SMEM),
                  _full(diag_idx.shape), _full(off_idx.shape), _full(cmp_idx.shape)],
        out_specs=[_full(s) for s in shapes],
        out_shape=[jax.ShapeDtypeStruct(s, jnp.float32) for s in shapes],
        grid=(1,),
        compiler_params=_params(1),
        name="bias_tables",
    )(rel_bias, jnp.asarray(diag_idx), jnp.asarray(off_idx), jnp.asarray(cmp_idx))


def _overlap_np(n_cmp_rows, n_sel):
    cmp_start = np.arange(n_cmp_rows) * CMP_STRIDE
    cmp_end = cmp_start + CMP_BLOCK - 1
    sel_start = np.arange(n_sel) * SEL_BLOCK
    ov = ((cmp_start[None, :] <= sel_start[:, None] + SEL_BLOCK - 1)
          & (cmp_end[None, :] >= sel_start[:, None])).astype(np.float32)
    ov[:, n_cmp_rows - 1] = 0.0
    return ov


def _tile4(a):
    return jnp.concatenate([a] * NSA_GROUP, axis=1)


def _select_blocks(score, score_ref, n_top, hooks=()):
    n_sel = score.shape[0]
    score_ref[...] = score
    n_grp = n_sel // SUBLANES
    grp = [score[SUBLANES * v:SUBLANES * (v + 1), :] for v in range(n_grp)]
    cnt = [jnp.zeros((SUBLANES, Q_BLOCK), jnp.int32) for _ in range(n_grp)]
    sub_io = lax.broadcasted_iota(jnp.int32, (SUBLANES, Q_BLOCK), 0)
    hook_at = {(i * n_sel) // len(hooks): hk for i, hk in enumerate(hooks)} if hooks else {}
    for jp in range(n_sel):
        if jp in hook_at:
            hook_at[jp]()
        row = score_ref[jp:jp + 1, :]
        for v in range(n_grp):
            if SUBLANES * v > jp:
                inc = (row >= grp[v]).astype(jnp.int32)
            elif SUBLANES * (v + 1) - 1 < jp:
                inc = (row > grp[v]).astype(jnp.int32)
            else:
                tie = (sub_io > jp - SUBLANES * v).astype(jnp.int32)
                inc = jnp.where(row > grp[v], 1, jnp.where(row == grp[v], tie, 0))
            cnt[v] = cnt[v] + inc
    return [jnp.where(cnt[v] < n_top, 0.0, NEG_INF) for v in range(n_grp)]


def _nsa_kernel(q_ref, kcmp_ref, vcmpT_ref, ksl_ref, vslT_ref, kwn_ref, vwnT_ref, gate_ref,
                seltab_ref, wintab_ref, cmptab_ref, ovl_ref, out_ref, score_ref, selb_ref, sbuf_ref, swin_ref):
    c = pl.program_id(1)
    n_cmp = kcmp_ref.shape[0]
    n_sel = ovl_ref.shape[0]
    n_top = min(N_SELECT, n_sel)
    f32 = jnp.float32
    bf16 = jnp.bfloat16
    DH = NSA_HEAD_DIM
    heads = range(NSA_KV_HEADS)

    q = q_ref[...]
    zq = jnp.zeros((DH, GQ), bf16)
    qpad = []
    for h in heads:
        qcat = jnp.concatenate([q[(h * NSA_GROUP + g) * DH:(h * NSA_GROUP + g + 1) * DH, :]
                                for g in range(NSA_GROUP)], axis=1)
        qpad.append(jnp.concatenate([qcat, zq] if h == 0 else [zq, qcat], axis=0))

    backs = list(range(N_WIN_TILES))
    kts = [jnp.maximum(c - back, 0) for back in backs]
    slots = [jnp.where(c >= back, back, N_WIN_TILES) for back in backs]
    m_win = [jnp.full((1, GQ), NEG_INF, f32) for _ in heads]

    def win_score(back, h):
        key0 = pl.multiple_of(kts[back] * Q_BLOCK, Q_BLOCK)
        s = _mm(kwn_ref[pl.ds(key0, Q_BLOCK), :], qpad[h]) + wintab_ref[h, slots[back]]
        swin_ref[h, back * Q_BLOCK:(back + 1) * Q_BLOCK, :] = s
        m_win[h] = jnp.maximum(m_win[h], jnp.max(s, axis=0, keepdims=True))

    cmp_off = pl.multiple_of(CMP_TAB_ZERO - (Q_BLOCK // CMP_STRIDE) * c, SUBLANES)
    kcmp = kcmp_ref[...]
    j_io = lax.broadcasted_iota(jnp.int32, (n_sel, Q_BLOCK), 0)
    r_io = lax.broadcasted_iota(jnp.int32, (n_sel, Q_BLOCK), 1)
    cur = (Q_BLOCK // SEL_BLOCK) * c + (r_io >= SEL_BLOCK).astype(jnp.int32)
    forced = (j_io == 0) | (j_io == cur) | (j_io == cur - 1)
    visible = j_io <= cur
    o_c = []
    for h in heads:
        tab = cmptab_ref[h, pl.ds(cmp_off, n_cmp), :]
        s = _mm(kcmp, qpad[h]) + tab
        m = jnp.max(s, axis=0, keepdims=True)
        p = jnp.where(tab > 0.5 * NEG_INF, jnp.exp2(s - m), 0.0)
        l = jnp.sum(p, axis=0, keepdims=True)
        pn = p * (1.0 / jnp.maximum(l, TINY))
        o_c.append(_mm(vcmpT_ref[h * DH:(h + 1) * DH, :], pn.astype(bf16)))
        psum = pn[:, 0:Q_BLOCK]
        for g in range(1, NSA_GROUP):
            psum = psum + pn[:, g * Q_BLOCK:(g + 1) * Q_BLOCK]
        imp = jnp.dot(ovl_ref[...], psum, precision=lax.Precision.HIGHEST,
                      preferred_element_type=f32)
        score = jnp.where(forced, FORCED_SCORE, jnp.where(visible, imp, -1.0))
        rows = _select_blocks(score, score_ref.at[h], n_top,
                              hooks=[functools.partial(win_score, back, h) for back in backs])
        for v, blk in enumerate(rows):
            selb_ref[h, SUBLANES * v:SUBLANES * (v + 1), :] = blk

    def ones_rows(n_keys):
        return jnp.ones((V_ROWS - DH, n_keys), bf16)

    acc_w = [jnp.zeros((V_ROWS, GQ), f32) for _ in heads]

    def win_value(back):
        for h in heads:
            p = jnp.exp2(swin_ref[h, back * Q_BLOCK:(back + 1) * Q_BLOCK, :] - m_win[h]).astype(bf16)
            vT = jnp.concatenate([vwnT_ref[kts[back], h * DH:(h + 1) * DH, :], ones_rows(Q_BLOCK)], axis=0)
            acc_w[h] = acc_w[h] + _mm(vT, p)

    blocks_per_step = KEY_SUPER // SEL_BLOCK
    tiles_per_step = KEY_SUPER // Q_BLOCK

    sub = KEY_SUPER // SEL_SUB_CHUNKS
    blocks_per_sub = sub // SEL_BLOCK
    tiles_per_sub = sub // Q_BLOCK

    def score_chunk(j, ci, h):
        key0 = pl.multiple_of(j * KEY_SUPER + ci * sub, sub)
        tab_off = pl.multiple_of(jnp.maximum(j * KEY_SUPER - c * Q_BLOCK + SEL_TAB_ZERO, 0) + ci * sub, Q_BLOCK)
        blk0 = j * blocks_per_step + ci * blocks_per_sub
        selb = jnp.concatenate(
            [jnp.broadcast_to(selb_ref[h, pl.ds(blk0 + i, 1), :], (SEL_BLOCK, Q_BLOCK))
             for i in range(blocks_per_sub)], axis=0)
        s = _mm(ksl_ref[pl.ds(key0, sub), :], qpad[h]) + _tile4(selb) + seltab_ref[h, pl.ds(tab_off, sub), :]
        sbuf_ref[h, ci * sub:(ci + 1) * sub, :] = s
        return jnp.max(s, axis=0, keepdims=True)

    def value_chunk(j, ci, h, m_h):
        p = jnp.exp2(sbuf_ref[h, ci * sub:(ci + 1) * sub, :] - m_h).astype(bf16)
        vT = jnp.concatenate([vslT_ref[j * tiles_per_step + ci * tiles_per_sub + i, h * DH:(h + 1) * DH, :]
                              for i in range(tiles_per_sub)], axis=1)
        return _mm(jnp.concatenate([vT, ones_rows(sub)], axis=0), p)

    def sel_body(j, carry):
        m_old, m_cur, acc = carry
        acc = [jnp.exp2(m_old[h] - m_cur[h]) * acc[h] for h in heads]
        m_run = list(m_cur)
        for ci in range(SEL_SUB_CHUNKS):
            for h in heads:
                acc[h] = acc[h] + value_chunk(j, ci, h, m_cur[h])
            for h in heads:
                m_run[h] = jnp.maximum(m_run[h], score_chunk(j + 1, ci, h))
        return m_cur, tuple(m_run), tuple(acc)

    n_steps = lax.shift_right_logical(c, SEL_STEP_SHIFT) + 1
    m_init = tuple(jnp.full((1, GQ), NEG_INF, f32) for _ in heads)
    acc_init = tuple(jnp.zeros((V_ROWS, GQ), f32) for _ in heads)
    m_first = list(m_init)
    win_order = list(backs)
    for ci in range(SEL_SUB_CHUNKS):
        for _ in range(-(-N_WIN_TILES // SEL_SUB_CHUNKS)):
            if win_order:
                win_value(win_order.pop(0))
        for h in heads:
            m_first[h] = jnp.maximum(m_first[h], score_chunk(0, ci, h))
    while win_order:
        win_value(win_order.pop(0))
    o_w = [a[0:DH, :] * (1.0 / a[DH:DH + 1, :]) for a in acc_w]
    m_old, m_cur, acc = lax.fori_loop(0, n_steps - 1, sel_body, (m_init, tuple(m_first), acc_init))
    acc = [jnp.exp2(m_old[h] - m_cur[h]) * acc[h] for h in heads]
    for ci in range(SEL_SUB_CHUNKS):
        for h in heads:
            acc[h] = acc[h] + value_chunk(n_steps - 1, ci, h, m_cur[h])
    o_s = [a[0:DH, :] * (1.0 / jnp.maximum(a[DH:DH + 1, :], TINY)) for a in acc]

    for h in heads:
        ys = []
        for g in range(NSA_GROUP):
            sl = slice(g * Q_BLOCK, (g + 1) * Q_BLOCK)
            row0 = h * NSA_GROUP + g
            gates = [jax.nn.sigmoid(gate_ref[kind * NSA_HEADS + row0:kind * NSA_HEADS + row0 + 1, :])
                     for kind in range(3)]
            ys.append(gates[0] * o_c[h][:, sl] + gates[1] * o_s[h][:, sl] + gates[2] * o_w[h][:, sl])
        yT = jnp.concatenate(ys, axis=0)
        for half in range(2):
            col = (2 * h + half) * LANES
            out_ref[:, col:col + LANES] = yT[half * LANES:(half + 1) * LANES, :].T.astype(out_ref.dtype)


def _nsa(qT, kcmp, vcmpT, ksl, vslT, kwn, vwnT, smallT, tables, B, S):
    assert S % KEY_SUPER == 0
    nq = S // Q_BLOCK
    n_cmp = S // CMP_STRIDE
    n_sel = S // SEL_BLOCK
    seltab, wintab, cmptab = tables
    ovl = jnp.asarray(_overlap_np(n_cmp, n_sel))
    ksl3 = ksl.reshape(B, S, NSA_KV_WIDTH)
    kwn3 = kwn.reshape(B, S, NSA_KV_WIDTH)
    vslT4 = vslT.reshape(B, nq, NSA_KV_WIDTH, Q_BLOCK)
    vwnT4 = vwnT.reshape(B, nq, NSA_KV_WIDTH, Q_BLOCK)
    k_spec = pl.BlockSpec((None, S, NSA_KV_WIDTH), lambda b, c: (b, 0, 0))
    vT_spec = pl.BlockSpec((None, nq, NSA_KV_WIDTH, Q_BLOCK), lambda b, c: (b, 0, 0, 0))
    const = lambda a: pl.BlockSpec(a.shape, lambda b, c: (0,) * a.ndim)
    return pl.pallas_call(
        _nsa_kernel,
        grid=(B, nq),
        in_specs=[pl.BlockSpec((None, NSA_WIDTH, Q_BLOCK), lambda b, c: (b * nq + c, 0, 0)),
                  pl.BlockSpec((None, n_cmp, NSA_KV_WIDTH), lambda b, c: (b, 0, 0)),
                  pl.BlockSpec((None, NSA_KV_WIDTH, n_cmp), lambda b, c: (b, 0, 0)),
                  k_spec, vT_spec, k_spec, vT_spec,
                  pl.BlockSpec((None, N_GATE_ROWS, Q_BLOCK), lambda b, c: (b * nq + c, 0, 0)),
                  const(seltab), const(wintab), const(cmptab), const(ovl)],
        out_specs=pl.BlockSpec((Q_BLOCK, NSA_WIDTH), lambda b, c: (b * nq + c, 0)),
        out_shape=jax.ShapeDtypeStruct((B * S, NSA_WIDTH), jnp.bfloat16),
        scratch_shapes=[pltpu.VMEM((NSA_KV_HEADS, n_sel, Q_BLOCK), jnp.float32),
                        pltpu.VMEM((NSA_KV_HEADS, n_sel, Q_BLOCK), jnp.float32),
                        pltpu.VMEM((NSA_KV_HEADS, KEY_SUPER, GQ), jnp.float32),
                        pltpu.VMEM((NSA_KV_HEADS, N_WIN_TILES * Q_BLOCK, GQ), jnp.float32)],
        compiler_params=_params(2),
        name="nsa",
    )(qT, kcmp, vcmpT, ksl3, vslT4, kwn3, vwnT4, smallT, seltab, wintab, cmptab, ovl)


CONV_TILE = 256
I_ROW = 24
C_ROWS = MLSTM_V_DIM + 16


def _log_sigmoid(x):
    return jnp.minimum(x, 0.0) - jnp.log(1.0 + jnp.exp(-jnp.abs(x)))


def _mlstm_kernel(mqk_ref, mvT_ref, moT_ref, gates_ref, gb_ref, tri_ref, last_ref, convw_ref, convb_ref,
                  norm_ref, out_ref, qk_ref, rows_ref):
    S = mqk_ref.shape[0]
    nt = S // LANES
    L = MLSTM_CHUNK
    f32 = jnp.float32
    bf16 = jnp.bfloat16
    kscale_row = jnp.where(lax.broadcasted_iota(jnp.int32, (1, 2 * MLSTM_QK_WIDTH), 1) < MLSTM_QK_WIDTH,
                           1.0, MLSTM_QK_DIM ** -0.5)

    def conv_body(i, _):
        t0 = pl.multiple_of(i * CONV_TILE, CONV_TILE)
        cur = mqk_ref[pl.ds(t0, CONV_TILE), :].astype(f32)
        prev_start = pl.multiple_of(jnp.maximum(t0 - 16, 0), 16)
        prev = mqk_ref[pl.ds(prev_start, 16), :].astype(f32)[8:16, :] * jnp.where(i > 0, 1.0, 0.0)
        ext = jnp.concatenate([prev, cur], axis=0)
        y = convb_ref[...]
        for j in range(CONV_WIDTH):
            lo = SUBLANES - (CONV_WIDTH - 1) + j
            y = y + convw_ref[j:j + 1, :] * ext[lo:lo + CONV_TILE, :]
        y = y * jax.nn.sigmoid(y) * kscale_row
        qk_ref[pl.ds(t0, CONV_TILE), :] = y.astype(bf16)
        return 0

    lax.fori_loop(0, S // CONV_TILE, conv_body, 0)

    H = MLSTM_HEADS
    G8 = 2 * H
    n_rows = nt * G8
    a3 = gates_ref[:, I_ROW:I_ROW + G8, :] + gb_ref[...][None]
    is_f = lax.broadcasted_iota(jnp.int32, a3.shape, 1) >= H
    x = jnp.where(is_f, _log_sigmoid(a3), a3).reshape(n_rows, LANES)
    bcum = jnp.dot(x, tri_ref[...], precision=lax.Precision.HIGHEST, preferred_element_type=f32)
    b_rows = pltpu.roll(bcum, n_rows - H, 0)
    g_rows = x - b_rows
    pos = lax.broadcasted_iota(jnp.int32, (n_rows, LANES), 1) & (L - 1)

    def chunk_cummax(a):
        shift = 1
        while shift < L:
            a = jnp.where(pos >= shift, jnp.maximum(a, pltpu.roll(a, shift, 1)), a)
            shift *= 2
        return a

    def chunk_last(a):
        return jnp.dot(a, last_ref[...], precision=lax.Precision.HIGHEST, preferred_element_type=f32)

    bl_rows = chunk_last(b_rows)
    rows_ref[0] = g_rows
    rows_ref[1] = chunk_cummax(g_rows)
    rows_ref[2] = b_rows
    rows_ref[3] = bl_rows
    rows_ref[4] = chunk_last(chunk_cummax(bl_rows + g_rows))

    s_io = lax.broadcasted_iota(jnp.int32, (LANES, LANES), 0)
    t_io = lax.broadcasted_iota(jnp.int32, (LANES, LANES), 1)
    causal = (s_io <= t_io) & ((s_io >= L) == (t_io >= L))
    lane = lax.broadcasted_iota(jnp.int32, (1, LANES), 1)
    in_chunk = [lane < L, lane >= L]
    QD, VD = MLSTM_QK_DIM, MLSTM_V_DIM
    ones_aug = jnp.ones((C_ROWS - VD, LANES), bf16)
    head_lanes = [(lax.broadcasted_iota(jnp.int32, (LANES, LANES), 1) >= QD) == bool(par) for par in range(2)]

    def swap_halves(row):
        return pltpu.roll(row, L, 1)

    def tile_body(i, state):
        t0 = pl.multiple_of(i * LANES, LANES)
        r0 = pl.multiple_of(i * G8, G8)
        qk = qk_ref[pl.ds(t0, LANES), :]
        qk32 = qk.astype(f32)
        pairsT = [qk32[:, j * LANES:(j + 1) * LANES].T for j in range(4)]
        g8, cm8, b8, bl8, wm8 = [rows_ref[j, pl.ds(r0, G8), :] for j in range(5)]
        new_state = []
        for hh in range(H):
            pair, par = hh // 2, hh % 2
            g_r, cm_r, b_r, bl_r, wm_r = [a[hh:hh + 1, :] for a in (g8, cm8, b8, bl8, wm8)]
            caug, m_in = state[hh]
            qT = pairsT[pair][par * QD:(par + 1) * QD, :]
            kT = pairsT[2 + pair][par * QD:(par + 1) * QD, :]
            kpair = qk[:, (2 + pair) * LANES:(3 + pair) * LANES]
            qmask = jnp.where(head_lanes[par], qk[:, pair * LANES:(pair + 1) * LANES], jnp.zeros((), bf16))
            vaug = jnp.concatenate([mvT_ref[i, hh * VD:(hh + 1) * VD, :], ones_aug], axis=0)

            m_mid = swap_halves(jnp.maximum(bl_r + m_in, wm_r))
            m_prev = jnp.where(in_chunk[0], m_in, m_mid)
            m_next = jnp.maximum(bl_r + m_prev, wm_r)
            m_intra = b_r + cm_r
            m_inter = b_r + m_prev
            m_t = jnp.maximum(m_inter, m_intra)
            e_intra = jnp.exp(m_intra - m_t)
            e_inter = jnp.exp(m_inter - m_t)
            decay = jnp.exp(bl_r + m_prev - m_next)
            inject = jnp.exp(wm_r - m_next)

            g_mat = jnp.broadcast_to(g_r, (LANES, LANES)).T
            w = jnp.exp(jnp.where(causal, g_mat - cm_r, NEG_INF))
            st = _nt(kpair, qmask) * (w * e_intra)
            y = _mm(vaug, st.astype(bf16))
            kw = kT * jnp.exp(bl_r + g_r - wm_r)
            cs = caug
            for p in range(LANES // L):
                qs = jnp.where(in_chunk[p], qT * e_inter, 0.0).astype(bf16)
                y = y + _mm(cs.astype(bf16), qs)
                u = _nt(vaug, jnp.where(in_chunk[p], kw, 0.0).astype(bf16))
                dec = decay if p == 0 else swap_halves(decay)
                inj = inject if p == 0 else swap_halves(inject)
                cs = dec[:, 0:QD] * cs + inj[:, 0:QD] * u
            m_out = jnp.where(in_chunk[1], m_next, swap_halves(m_next))
            new_state.append((cs, m_out))

            den = y[VD:VD + 1, :]
            hT = y[0:VD, :] * (1.0 / jnp.maximum(jnp.abs(den), jnp.exp(-m_t)))
            sl = slice(hh * VD, (hh + 1) * VD)
            hn = hT * lax.rsqrt(jnp.mean(hT * hT, axis=0, keepdims=True) + NORM_EPS) * norm_ref[sl, :]
            yT = jax.nn.sigmoid(moT_ref[i, sl, :].astype(f32)) * hn
            out_ref[pl.ds(t0, LANES), sl] = yT.T.astype(out_ref.dtype)
        return tuple(new_state)

    init = tuple((jnp.zeros((C_ROWS, QD), f32), jnp.zeros((1, LANES), f32)) for _ in range(H))
    lax.fori_loop(0, nt, tile_body, init)


def _mlstm(mqk, mvT, moT, smallT, conv_w, conv_b, gate_bias, mlstm_norm, B, S):
    nt = S // LANES
    gb = jnp.broadcast_to(gate_bias.astype(jnp.float32).reshape(2 * MLSTM_HEADS, 1), (2 * MLSTM_HEADS, LANES))
    norm_cols = jnp.broadcast_to(mlstm_norm.astype(jnp.float32).reshape(MLSTM_WIDTH, 1), (MLSTM_WIDTH, LANES))
    lane = np.arange(LANES)
    same_chunk = lane[:, None] // MLSTM_CHUNK == lane[None, :] // MLSTM_CHUNK
    tri = (same_chunk & (lane[:, None] <= lane[None, :])).astype(np.float32)
    last = (same_chunk & (lane[:, None] % MLSTM_CHUNK == MLSTM_CHUNK - 1)).astype(np.float32)
    seq = lambda w: pl.BlockSpec((S, w), lambda b: (b, 0))
    tiles = lambda rows: pl.BlockSpec((nt, rows, LANES), lambda b: (b, 0, 0))
    return pl.pallas_call(
        _mlstm_kernel,
        grid=(B,),
        in_specs=[seq(2 * MLSTM_QK_WIDTH), tiles(MLSTM_WIDTH), tiles(MLSTM_WIDTH), tiles(N_GATE_ROWS),
                  _full((2 * MLSTM_HEADS, LANES)), _full((LANES, LANES)), _full((LANES, LANES)),
                  _full((CONV_WIDTH, 2 * MLSTM_QK_WIDTH)), _full((1, 2 * MLSTM_QK_WIDTH)),
                  _full((MLSTM_WIDTH, LANES))],
        out_specs=seq(MLSTM_WIDTH),
        out_shape=jax.ShapeDtypeStruct((B * S, MLSTM_WIDTH), jnp.bfloat16),
        scratch_shapes=[pltpu.VMEM((S, 2 * MLSTM_QK_WIDTH), jnp.bfloat16),
                        pltpu.VMEM((5, nt * 2 * MLSTM_HEADS, LANES), jnp.float32)],
        compiler_params=_params(1),
        name="mlstm",
    )(mqk, mvT, moT, smallT, gb, jnp.asarray(tri), jnp.asarray(last), conv_w, conv_b.reshape(1, -1), norm_cols)


def _mem_kv_kernel(mem_ref, g_ref, w_ref, k_ref, v_ref):
    mn = _rms(mem_ref[...], g_ref[...]).astype(jnp.bfloat16)
    k_ref[...] = _mm(mn, w_ref[:, :D_MODEL]).astype(k_ref.dtype)
    v_ref[...] = _mm(mn, w_ref[:, D_MODEL:]).astype(v_ref.dtype)


def _mem_kv(mem, gain, w_xkv):
    B, M, _ = mem.shape
    spec = pl.BlockSpec((None, M, D_MODEL), lambda b: (b, 0, 0))
    return pl.pallas_call(
        _mem_kv_kernel,
        grid=(B,),
        in_specs=[spec, _full((1, D_MODEL)), _full((D_MODEL, 2 * D_MODEL))],
        out_specs=[spec, spec],
        out_shape=[jax.ShapeDtypeStruct((B, M, D_MODEL), jnp.bfloat16)] * 2,
        compiler_params=_params(1),
        name="mem_kv",
    )(mem, gain, w_xkv.astype(jnp.bfloat16))


TM_X = 512


def _mix_xattn_kernel(ynsa_ref, yml_ref, x_ref, wout_ref, gpost_ref, gpre_ref, wq_ref, k_ref, v_ref,
                      wo_ref, gpost2_ref, out_ref):
    bf16 = jnp.bfloat16
    y = _mm(ynsa_ref[...], wout_ref[:NSA_WIDTH, :]) + _mm(yml_ref[...], wout_ref[NSA_WIDTH:, :])
    x1 = x_ref[...] + _rms(y, gpost_ref[...])
    h2 = _rms(x1, gpre_ref[...]).astype(bf16)
    q = (_mm(h2, wq_ref[...]) * (XATTN_HEAD_DIM ** -0.5)).astype(bf16)
    outs = []
    for hh in range(XATTN_HEADS):
        sl = slice(hh * XATTN_HEAD_DIM, (hh + 1) * XATTN_HEAD_DIM)
        s = _nt(q[:, sl], k_ref[:, sl])
        p = jnp.exp(s - jnp.max(s, axis=1, keepdims=True))
        l = jnp.sum(p, axis=1, keepdims=True)
        outs.append((_mm(p.astype(bf16), v_ref[:, sl]) * (1.0 / l)).astype(bf16))
    o = jnp.concatenate(outs, axis=1)
    y2 = _mm(o, wo_ref[...])
    out_ref[...] = x1 + _rms(y2, gpost2_ref[...])


def _mix_xattn(ynsa, yml, x2d, w_out, g_post, g_pre, w_xq, kx, vx, w_xo, g_post2, B, S):
    nt = S // TM_X
    M = kx.shape[1]
    tok = lambda w: pl.BlockSpec((TM_X, w), lambda b, i: (b * nt + i, 0))
    mem_spec = pl.BlockSpec((None, M, D_MODEL), lambda b, i: (b, 0, 0))
    sq = _full((D_MODEL, D_MODEL))
    row = _full((1, D_MODEL))
    bf = lambda w: w.astype(jnp.bfloat16)
    return pl.pallas_call(
        _mix_xattn_kernel,
        grid=(B, nt),
        in_specs=[tok(NSA_WIDTH), tok(MLSTM_WIDTH), tok(D_MODEL), sq, row, row, sq, mem_spec, mem_spec,
                  sq, row],
        out_specs=tok(D_MODEL),
        out_shape=jax.ShapeDtypeStruct((B * S, D_MODEL), jnp.float32),
        compiler_params=_params(2),
        name="mix_xattn",
    )(ynsa, yml, x2d, bf(w_out), g_post, g_pre, bf(w_xq), kx, vx, bf(w_xo), g_post2)


TM_F = 512


def _ffn_kernel(x_ref, gpre_ref, wgu_ref, wd_ref, gpost_ref, out_ref, acc_ref):
    bf16 = jnp.bfloat16
    x = x_ref[...]
    h = _rms(x, gpre_ref[...]).astype(bf16)
    acc_ref[...] = jnp.zeros_like(acc_ref)
    for j in range(D_FF // F_TILE):
        cols = slice(j * F_TILE, (j + 1) * F_TILE)
        g = _mm(h, wgu_ref[:, cols])
        u = _mm(h, wgu_ref[:, D_FF + j * F_TILE:D_FF + (j + 1) * F_TILE])
        act = (g * jax.nn.sigmoid(g) * u).astype(bf16)
        acc_ref[...] += _mm(act, wd_ref[cols, :])
    out_ref[...] = x + _rms(acc_ref[...], gpost_ref[...])


def _ffn(x2d, g_pre, w_gate_up, w_down, g_post):
    T = x2d.shape[0]
    tok = pl.BlockSpec((TM_F, D_MODEL), lambda i: (i, 0))
    row = _full((1, D_MODEL))
    return pl.pallas_call(
        _ffn_kernel,
        grid=(T // TM_F,),
        in_specs=[tok, row, _full((D_MODEL, 2 * D_FF)), _full((D_FF, D_MODEL)), row],
        out_specs=tok,
        out_shape=jax.ShapeDtypeStruct((T, D_MODEL), jnp.float32),
        scratch_shapes=[pltpu.VMEM((TM_F, D_MODEL), jnp.float32)],
        compiler_params=_params(1),
        name="ffn",
    )(x2d, g_pre, w_gate_up.astype(jnp.bfloat16), w_down.astype(jnp.bfloat16), g_post)


def _layer(x, mem, rel_bias, mix_norm_pre, w_in, cmp_pos_k, cmp_pos_v, cmp_w1_k, cmp_w2_k, cmp_w1_v,
           cmp_w2_v, conv_w, conv_b, mlstm_gate_bias, mlstm_norm, w_out, mix_norm_post, xattn_norm_pre,
           mem_norm, w_xq, w_xkv, w_xo, xattn_norm_post, ffn_norm_pre, w_gate_up, w_down, ffn_norm_post):
    B, S, _ = x.shape
    row = lambda g: g.reshape(1, -1).astype(jnp.float32)
    x2d = x.reshape(B * S, D_MODEL)
    w_tok, w_feat = _in_proj_weights(w_in)
    (kc, vc, ksl, kwn, mqk, qT, vslT, vwnT, mvT, moT, smallT) = _in_proj(
        x2d, row(mix_norm_pre), w_tok, w_feat)
    kcmp, vcmpT = _compress(kc, vc, cmp_pos_k, cmp_pos_v, cmp_w1_k, cmp_w2_k, cmp_w1_v, cmp_w2_v, B, S)
    tables = _bias_tables(rel_bias.astype(jnp.float32))
    ynsa = _nsa(qT, kcmp, vcmpT, ksl, vslT, kwn, vwnT, smallT, tables, B, S)
    yml = _mlstm(mqk, mvT, moT, smallT, conv_w, conv_b, mlstm_gate_bias, mlstm_norm, B, S)
    kx, vx = _mem_kv(mem, row(mem_norm), w_xkv)
    x2 = _mix_xattn(ynsa, yml, x2d, w_out, row(mix_norm_post), row(xattn_norm_pre), w_xq, kx, vx, w_xo,
                    row(xattn_norm_post), B, S)
    x3 = _ffn(x2, row(ffn_norm_pre), w_gate_up, w_down, row(ffn_norm_post))
    return x3.reshape(B, S, D_MODEL)


def kernel(x, mem, rel_bias, mix_norm_pre, w_in, cmp_pos_k, cmp_pos_v, cmp_w1_k, cmp_w2_k, cmp_w1_v, cmp_w2_v,
           conv_w, conv_b, mlstm_gate_bias, mlstm_norm, w_out, mix_norm_post, xattn_norm_pre, mem_norm, w_xq,
           w_xkv, w_xo, xattn_norm_post, ffn_norm_pre, w_gate_up, w_down, ffn_norm_post):
    depth = w_in.shape[0]
    for l in range(depth):
        x = _layer(x, mem, rel_bias, mix_norm_pre[l], w_in[l], cmp_pos_k[l], cmp_pos_v[l], cmp_w1_k[l],
                   cmp_w2_k[l], cmp_w1_v[l], cmp_w2_v[l], conv_w[l], conv_b[l], mlstm_gate_bias[l],
                   mlstm_norm[l], w_out[l], mix_norm_post[l], xattn_norm_pre[l], mem_norm[l], w_xq[l],
                   w_xkv[l], w_xo[l], xattn_norm_post[l], ffn_norm_pre[l], w_gate_up[l], w_down[l],
                   ffn_norm_post[l])
    return x
```

```python
import functools
import math

import numpy as np
import jax
import jax.numpy as jnp
from jax import lax
from jax.experimental import pallas as pl
from jax.experimental.pallas import tpu as pltpu

D_MODEL = 1024
NSA_WIDTH = 512
NSA_HEAD_DIM = 64
NSA_HEADS = 8
NSA_KV_HEADS = 2
NSA_GROUP = 4
NSA_KV_WIDTH = 128
CMP_STRIDE = 16
CMP_BLOCK = 32
CMP_HIDDEN = 256
SEL_BLOCK = 64
N_SELECT = 16
WINDOW = 512
Q_BLOCK = 128
FORCED_SCORE = 1.0e4
MLSTM_WIDTH = 512
MLSTM_HEADS = 4
MLSTM_V_DIM = 128
MLSTM_QK_DIM = 64
MLSTM_QK_WIDTH = 256
MLSTM_CHUNK = 64
CONV_WIDTH = 4
REL_BUCKETS = 32
REL_MAX_DISTANCE = 128
XATTN_HEADS = 4
XATTN_HEAD_DIM = 256
D_FF = 2816
NORM_EPS = 1e-6
NEG_INF = -1.0e30
LOG2E = math.log2(math.e)

IN_SIZES = (NSA_WIDTH,) + (NSA_KV_WIDTH,) * 6 + (NSA_HEADS * 3, MLSTM_QK_WIDTH, MLSTM_QK_WIDTH,
                                                 MLSTM_WIDTH, MLSTM_HEADS, MLSTM_HEADS, MLSTM_WIDTH)
IN_OFFSETS = tuple(int(o) for o in np.cumsum((0,) + IN_SIZES)[:-1])

LANES = 128
SUBLANES = 8
VMEM_LIMIT_BYTES = 56 * 1024 * 1024

N_GATE_ROWS = 32
F_TILE = 256


def _rms(x, gain):
    return x * lax.rsqrt(jnp.mean(x * x, axis=-1, keepdims=True) + NORM_EPS) * gain


def _nt(a, b):
    return lax.dot_general(a, b, (((1,), (1,)), ((), ())), preferred_element_type=jnp.float32)


def _mm(a, b):
    return jnp.dot(a, b, preferred_element_type=jnp.float32)


def _params(n_axes, flags=None):
    return pltpu.CompilerParams(dimension_semantics=("arbitrary",) * n_axes,
                                vmem_limit_bytes=VMEM_LIMIT_BYTES, flags=flags)


def _full(shape):
    nd = len(shape)
    return pl.BlockSpec(shape, lambda *_: (0,) * nd)


TM_IN = 512
_TOK_GROUPS = (("kc", 128, jnp.float32), ("vc", 128, jnp.float32), ("ksl", 128, jnp.bfloat16),
               ("kwn", 128, jnp.bfloat16), ("mqk", 512, jnp.bfloat16))
_FEAT_GROUPS = (("qT", 512, jnp.bfloat16), ("vslT", 128, jnp.bfloat16), ("vwnT", 128, jnp.bfloat16),
                ("mvT", 512, jnp.bfloat16), ("moT", 512, jnp.bfloat16), ("smallT", N_GATE_ROWS, jnp.float32))


def _in_proj_kernel(x_ref, g_ref, wtok_ref, wfeat_ref, *out_refs):
    h = _rms(x_ref[...], g_ref[...]).astype(jnp.bfloat16)
    n_tok = len(_TOK_GROUPS)
    off = 0
    for (name, width, dt), o_ref in zip(_TOK_GROUPS, out_refs[:n_tok]):
        o_ref[...] = _mm(h, wtok_ref[:, off:off + width]).astype(dt)
        off += width
    off = 0
    for (name, rows, dt), o_ref in zip(_FEAT_GROUPS, out_refs[n_tok:]):
        r = _nt(wfeat_ref[off:off + rows, :], h)
        if name == "qT":
            r = r * (NSA_HEAD_DIM ** -0.5 * LOG2E)
        for j in range(TM_IN // LANES):
            o_ref[j] = r[:, j * LANES:(j + 1) * LANES].astype(dt)
        off += rows


def _in_proj(x2d, gain, w_tok, w_feat):
    T = x2d.shape[0]
    n_tok_cols = w_tok.shape[1]
    n_feat_rows = w_feat.shape[0]
    out_shape, out_specs = [], []
    for name, width, dt in _TOK_GROUPS:
        out_shape.append(jax.ShapeDtypeStruct((T, width), dt))
        out_specs.append(pl.BlockSpec((TM_IN, width), lambda i: (i, 0)))
    for name, rows, dt in _FEAT_GROUPS:
        out_shape.append(jax.ShapeDtypeStruct((T // LANES, rows, LANES), dt))
        out_specs.append(pl.BlockSpec((TM_IN // LANES, rows, LANES), lambda i: (i, 0, 0)))
    return pl.pallas_call(
        _in_proj_kernel,
        grid=(T // TM_IN,),
        in_specs=[pl.BlockSpec((TM_IN, D_MODEL), lambda i: (i, 0)),
                  _full((1, D_MODEL)),
                  _full((D_MODEL, n_tok_cols)),
                  _full((n_feat_rows, D_MODEL))],
        out_specs=out_specs,
        out_shape=out_shape,
        compiler_params=_params(1),
        name="in_proj",
    )(x2d, gain, w_tok, w_feat)


def _in_proj_weights(w_in):
    (nq, kc, vc, ksl, vsl, kwn, vwn, gt, mq, mk, mv, mi, mf, mo) = [
        w_in[:, o:o + s] for o, s in zip(IN_OFFSETS, IN_SIZES)]
    gt_r = gt.reshape(D_MODEL, NSA_KV_HEADS, NSA_GROUP, 3).transpose(0, 3, 1, 2).reshape(D_MODEL, 24)
    small = jnp.concatenate([gt_r, mi, mf], axis=1)
    w_tok = jnp.concatenate([kc, vc, ksl, kwn, mq, mk], axis=1)
    w_feat = jnp.concatenate([nq, vsl, vwn, mv, mo, small], axis=1).T
    return w_tok.astype(jnp.bfloat16), w_feat.astype(jnp.bfloat16)


N_CHUNK_COLS = CMP_STRIDE * NSA_KV_WIDTH
N_HID2 = NSA_KV_HEADS * CMP_HIDDEN


def _compress_one(c, pos_ref, w1_ref, n_chunks):
    lo = _mm((c + pos_ref[0:1, :]).astype(jnp.bfloat16), w1_ref[0])
    hi = _mm((c + pos_ref[1:2, :]).astype(jnp.bfloat16), w1_ref[1])
    pre = lo + pltpu.roll(hi, n_chunks - 1, 0)
    return (pre * jax.nn.sigmoid(pre)).astype(jnp.bfloat16)


def _chunk_rows(ref, n_chunks):
    return jnp.concatenate([ref[pl.ds(t, n_chunks, stride=CMP_STRIDE), :] for t in range(CMP_STRIDE)], axis=1)


def _compress_kernel(kc_ref, vc_ref, posk_ref, posv_ref, w1k_ref, w1v_ref, w2k_ref, w2vT_ref,
                     kcmp_ref, vcmpT_ref):
    n_chunks = kc_ref.shape[0] // CMP_STRIDE
    hid_k = _compress_one(_chunk_rows(kc_ref, n_chunks), posk_ref, w1k_ref, n_chunks)
    kcmp = _mm(hid_k, w2k_ref[...])
    row = lax.broadcasted_iota(jnp.int32, kcmp.shape, 0)
    kcmp_ref[...] = jnp.where(row < n_chunks - 1, kcmp, 0.0).astype(kcmp_ref.dtype)
    hid_v = _compress_one(_chunk_rows(vc_ref, n_chunks), posv_ref, w1v_ref, n_chunks)
    vcmpT = _nt(w2vT_ref[...], hid_v)
    col = lax.broadcasted_iota(jnp.int32, vcmpT.shape, 1)
    vcmpT_ref[...] = jnp.where(col < n_chunks - 1, vcmpT, 0.0).astype(vcmpT_ref.dtype)


def _compress_weights(pos, w1, w2):
    eye = jnp.eye(NSA_KV_HEADS, dtype=w1.dtype)
    w1r = w1.reshape(2, CMP_STRIDE, NSA_HEAD_DIM, CMP_HIDDEN)
    w1e = jnp.einsum('atdj,hg->athdgj', w1r, eye).reshape(2, N_CHUNK_COLS, N_HID2)
    pos_e = jnp.broadcast_to(pos.reshape(2, CMP_STRIDE, 1, NSA_HEAD_DIM),
                             (2, CMP_STRIDE, NSA_KV_HEADS, NSA_HEAD_DIM)).reshape(2, N_CHUNK_COLS)
    w2e = jnp.einsum('jd,hg->hjgd', w2, eye).reshape(N_HID2, NSA_KV_WIDTH)
    return pos_e, w1e.astype(jnp.bfloat16), w2e.astype(jnp.bfloat16)


def _compress(kc, vc, cmp_pos_k, cmp_pos_v, cmp_w1_k, cmp_w2_k, cmp_w1_v, cmp_w2_v, B, S):
    n_chunks = S // CMP_STRIDE
    posk, w1k, w2k = _compress_weights(cmp_pos_k, cmp_w1_k, cmp_w2_k)
    posv, w1v, w2v = _compress_weights(cmp_pos_v, cmp_w1_v, cmp_w2_v)
    chunk_spec = pl.BlockSpec((S, NSA_KV_WIDTH), lambda b: (b, 0))
    return pl.pallas_call(
        _compress_kernel,
        grid=(B,),
        in_specs=[chunk_spec, chunk_spec,
                  _full((2, N_CHUNK_COLS)), _full((2, N_CHUNK_COLS)),
                  _full((2, N_CHUNK_COLS, N_HID2)), _full((2, N_CHUNK_COLS, N_HID2)),
                  _full((N_HID2, NSA_KV_WIDTH)), _full((NSA_KV_WIDTH, N_HID2))],
        out_specs=[pl.BlockSpec((None, n_chunks, NSA_KV_WIDTH), lambda b: (b, 0, 0)),
                   pl.BlockSpec((None, NSA_KV_WIDTH, n_chunks), lambda b: (b, 0, 0))],
        out_shape=[jax.ShapeDtypeStruct((B, n_chunks, NSA_KV_WIDTH), jnp.bfloat16),
                   jax.ShapeDtypeStruct((B, NSA_KV_WIDTH, n_chunks), jnp.bfloat16)],
        compiler_params=_params(1),
        name="compress",
    )(kc, vc, posk, posv, w1k, w1v, w2k, w2v.T)


GQ = NSA_GROUP * Q_BLOCK
TINY = 1e-30
CMP_TAB_ROWS = 512
CMP_TAB_ZERO = 248
CMP_TAB_LOOKUP = (232, 256)
SEL_STEP_SHIFT = 2
KEY_SUPER = Q_BLOCK << SEL_STEP_SHIFT
SEL_TAB_ZERO = KEY_SUPER + Q_BLOCK
SEL_TAB_ROWS = SEL_TAB_ZERO + KEY_SUPER
N_WIN_TILES = WINDOW // Q_BLOCK + 1
SEL_SUB_CHUNKS = 2
V_ROWS = NSA_HEAD_DIM + 16


def _bucket_np(dist):
    n = np.maximum(dist, 0)
    max_exact = REL_BUCKETS // 2
    nf = np.maximum(n, 1).astype(np.float64)
    large = max_exact + (np.log(nf / max_exact) / math.log(REL_MAX_DISTANCE / max_exact)
                         * (REL_BUCKETS - max_exact)).astype(np.int64)
    large = np.minimum(large, REL_BUCKETS - 1)
    return np.where(n < max_exact, n, large).astype(np.int32)


def _bias_index_tables():
    m = np.arange(Q_BLOCK)[:, None]
    r = np.arange(Q_BLOCK)[None, :]
    diag = np.where(r - m >= 0, _bucket_np(r - m), -1).astype(np.int32)
    off = _bucket_np(Q_BLOCK + r - m)
    jp = np.arange(*CMP_TAB_LOOKUP)[:, None] - CMP_TAB_ZERO
    d_c = r - CMP_STRIDE * jp - (CMP_BLOCK - 1)
    cmp_idx = np.where(d_c >= 0, _bucket_np(d_c), -1).astype(np.int32)
    return diag, off, cmp_idx


def _bias_tables_kernel(rb_ref, diag_idx_ref, off_idx_ref, cmp_idx_ref, sel_ref, win_ref, cmp_ref):
    f32 = jnp.float32

    def lookup(idx, head):
        far = rb_ref[head, REL_BUCKETS - 1]
        acc = jnp.full(idx.shape, NEG_INF, f32)
        for k in range(REL_BUCKETS):
            acc = jnp.where(idx == k, (rb_ref[head, k] - far) * LOG2E, acc)
        return acc

    m_io = lax.broadcasted_iota(jnp.int32, (Q_BLOCK, Q_BLOCK), 0)
    r_io = lax.broadcasted_iota(jnp.int32, (Q_BLOCK, Q_BLOCK), 1)
    neg_tile = jnp.full((Q_BLOCK, Q_BLOCK), NEG_INF, f32)
    lo, hi = CMP_TAB_LOOKUP
    for h in range(NSA_KV_HEADS):
        for g in range(NSA_GROUP):
            head = h * NSA_GROUP + g
            sl = slice(g * Q_BLOCK, (g + 1) * Q_BLOCK)
            far = 0.0
            far_tile = jnp.full((Q_BLOCK, Q_BLOCK), far, f32)
            diag_v = lookup(diag_idx_ref[...], head)
            off_v = lookup(off_idx_ref[...], head)
            n_far = (SEL_TAB_ZERO - Q_BLOCK) // Q_BLOCK
            for t in range(SEL_TAB_ROWS // Q_BLOCK):
                rows = slice(t * Q_BLOCK, (t + 1) * Q_BLOCK)
                tile = far_tile if t < n_far else off_v if t == n_far else diag_v if t == n_far + 1 else neg_tile
                sel_ref[h, rows, sl] = tile
            win_ref[h, 0, :, sl] = diag_v
            win_ref[h, 1, :, sl] = off_v
            for back in range(2, N_WIN_TILES - 1):
                win_ref[h, back, :, sl] = far_tile
            win_ref[h, N_WIN_TILES - 1, :, sl] = jnp.where(r_io < m_io, far, NEG_INF)
            win_ref[h, N_WIN_TILES, :, sl] = neg_tile
            cmp_ref[h, 0:lo, sl] = jnp.full((lo, Q_BLOCK), far, f32)
            cmp_ref[h, lo:hi, sl] = lookup(cmp_idx_ref[...], head)
            cmp_ref[h, hi:CMP_TAB_ROWS, sl] = jnp.full((CMP_TAB_ROWS - hi, Q_BLOCK), NEG_INF, f32)


def _bias_tables(rel_bias):
    diag_idx, off_idx, cmp_idx = _bias_index_tables()
    shapes = [(NSA_KV_HEADS, SEL_TAB_ROWS, GQ), (NSA_KV_HEADS, N_WIN_TILES + 1, Q_BLOCK, GQ),
              (NSA_KV_HEADS, CMP_TAB_ROWS, GQ)]
    return pl.pallas_call(
        _bias_tables_kernel,
        in_specs=[pl.BlockSpec(memory_space=pltpu.SMEM),
                  _full(diag_idx.shape), _full(off_idx.shape), _full(cmp_idx.shape)],
        out_specs=[_full(s) for s in shapes],
        out_shape=[jax.ShapeDtypeStruct(s, jnp.float32) for s in shapes],
        grid=(1,),
        compiler_params=_params(1),
        name="bias_tables",
    )(rel_bias, jnp.asarray(diag_idx), jnp.asarray(off_idx), jnp.asarray(cmp_idx))


def _overlap_np(n_cmp_rows, n_sel):
    cmp_start = np.arange(n_cmp_rows) * CMP_STRIDE
    cmp_end = cmp_start + CMP_BLOCK - 1
    sel_start = np.arange(n_sel) * SEL_BLOCK
    ov = ((cmp_start[None, :] <= sel_start[:, None] + SEL_BLOCK - 1)
          & (cmp_end[None, :] >= sel_start[:, None])).astype(np.float32)
    ov[:, n_cmp_rows - 1] = 0.0
    return ov


def _tile4(a):
    return jnp.concatenate([a] * NSA_GROUP, axis=1)


def _select_blocks(score, score_ref, n_top, hooks=()):
    n_sel = score.shape[0]
    score_ref[...] = score
    n_grp = n_sel // SUBLANES
    grp = [score[SUBLANES * v:SUBLANES * (v + 1), :] for v in range(n_grp)]
    cnt = [jnp.zeros((SUBLANES, Q_BLOCK), jnp.int32) for _ in range(n_grp)]
    sub_io = lax.broadcasted_iota(jnp.int32, (SUBLANES, Q_BLOCK), 0)
    hook_at = {(i * n_sel) // len(hooks): hk for i, hk in enumerate(hooks)} if hooks else {}
    for jp in range(n_sel):
        if jp in hook_at:
            hook_at[jp]()
        row = score_ref[jp:jp + 1, :]
        for v in range(n_grp):
            if SUBLANES * v > jp:
                inc = (row >= grp[v]).astype(jnp.int32)
            elif SUBLANES * (v + 1) - 1 < jp:
                inc = (row > grp[v]).astype(jnp.int32)
            else:
                tie = (sub_io > jp - SUBLANES * v).astype(jnp.int32)
                inc = jnp.where(row > grp[v], 1, jnp.where(row == grp[v], tie, 0))
            cnt[v] = cnt[v] + inc
    return [jnp.where(cnt[v] < n_top, 0.0, NEG_INF) for v in range(n_grp)]


def _nsa_kernel(q_ref, kcmp_ref, vcmpT_ref, ksl_ref, vslT_ref, kwn_ref, vwnT_ref, gate_ref,
                seltab_ref, wintab_ref, cmptab_ref, ovl_ref, blkind_ref, out_ref, score_ref, selb_ref, sbuf_ref,
                swin_ref):
    c = pl.program_id(1)
    n_cmp = kcmp_ref.shape[0]
    n_sel = ovl_ref.shape[0]
    n_top = min(N_SELECT, n_sel)
    f32 = jnp.float32
    bf16 = jnp.bfloat16
    DH = NSA_HEAD_DIM
    heads = range(NSA_KV_HEADS)

    q = q_ref[...]
    zq = jnp.zeros((DH, GQ), bf16)
    qcat, qpad = [], []
    for h in heads:
        qcat.append(jnp.concatenate([q[(h * NSA_GROUP + g) * DH:(h * NSA_GROUP + g + 1) * DH, :]
                                     for g in range(NSA_GROUP)], axis=1))
        qpad.append(jnp.concatenate([qcat[h], zq] if h == 0 else [zq, qcat[h]], axis=0))

    backs = list(range(N_WIN_TILES))
    kts = [jnp.maximum(c - back, 0) for back in backs]
    slots = [jnp.where(c >= back, back, N_WIN_TILES) for back in backs]
    m_win = [jnp.full((1, GQ), NEG_INF, f32) for _ in heads]

    def win_score(back, h):
        key0 = pl.multiple_of(kts[back] * Q_BLOCK, Q_BLOCK)
        s = _mm(kwn_ref[pl.ds(key0, Q_BLOCK), :], qpad[h]) + wintab_ref[h, slots[back]]
        swin_ref[h, back * Q_BLOCK:(back + 1) * Q_BLOCK, :] = s
        m_win[h] = jnp.maximum(m_win[h], jnp.max(s, axis=0, keepdims=True))

    cmp_off = pl.multiple_of(CMP_TAB_ZERO - (Q_BLOCK // CMP_STRIDE) * c, SUBLANES)
    kcmp = kcmp_ref[...]
    j_io = lax.broadcasted_iota(jnp.int32, (n_sel, Q_BLOCK), 0)
    r_io = lax.broadcasted_iota(jnp.int32, (n_sel, Q_BLOCK), 1)
    cur = (Q_BLOCK // SEL_BLOCK) * c + (r_io >= SEL_BLOCK).astype(jnp.int32)
    forced = (j_io == 0) | (j_io == cur) | (j_io == cur - 1)
    visible = j_io <= cur
    o_c = []
    for h in heads:
        tab = cmptab_ref[h, pl.ds(cmp_off, n_cmp), :]
        s = _mm(kcmp, qpad[h]) + tab
        m = jnp.max(s, axis=0, keepdims=True)
        p = jnp.where(tab > 0.5 * NEG_INF, jnp.exp2(s - m), 0.0)
        l = jnp.sum(p, axis=0, keepdims=True)
        pn = p * (1.0 / jnp.maximum(l, TINY))
        o_c.append(_mm(vcmpT_ref[h * DH:(h + 1) * DH, :], pn.astype(bf16)))
        psum = pn[:, 0:Q_BLOCK]
        for g in range(1, NSA_GROUP):
            psum = psum + pn[:, g * Q_BLOCK:(g + 1) * Q_BLOCK]
        imp = jnp.dot(ovl_ref[...], psum, precision=lax.Precision.HIGHEST,
                      preferred_element_type=f32)
        score = jnp.where(forced, FORCED_SCORE, jnp.where(visible, imp, -1.0))
        rows = _select_blocks(score, score_ref.at[h], n_top,
                              hooks=[functools.partial(win_score, back, h) for back in backs])
        for v, blk in enumerate(rows):
            selb_ref[h, SUBLANES * v:SUBLANES * (v + 1), :] = blk

    def ones_rows(n_keys):
        return jnp.ones((V_ROWS - DH, n_keys), bf16)

    acc_w = [jnp.zeros((V_ROWS, GQ), f32) for _ in heads]

    def win_value(back):
        for h in heads:
            p = jnp.exp2(swin_ref[h, back * Q_BLOCK:(back + 1) * Q_BLOCK, :] - m_win[h]).astype(bf16)
            vT = jnp.concatenate([vwnT_ref[kts[back], h * DH:(h + 1) * DH, :], ones_rows(Q_BLOCK)], axis=0)
            acc_w[h] = acc_w[h] + _mm(vT, p)

    blocks_per_step = KEY_SUPER // SEL_BLOCK
    tiles_per_step = KEY_SUPER // Q_BLOCK

    sub = KEY_SUPER // SEL_SUB_CHUNKS
    blocks_per_sub = sub // SEL_BLOCK
    tiles_per_sub = sub // Q_BLOCK

    own_lanes = [(lax.broadcasted_iota(jnp.int32, (sub, NSA_KV_WIDTH), 1) >= DH) == bool(h) for h in heads]

    def q_with_mask_rows(j, h):
        blk0 = pl.multiple_of(j * blocks_per_step, blocks_per_step)
        rows = jnp.concatenate([_tile4(selb_ref[h, pl.ds(blk0, blocks_per_step), :]),
                                jnp.zeros((DH - blocks_per_step, GQ), f32)], axis=0).astype(bf16)
        return jnp.concatenate([qcat[h], rows] if h == 0 else [rows, qcat[h]], axis=0)

    def score_chunk(j, ci, h, q_aug):
        key0 = pl.multiple_of(j * KEY_SUPER + ci * sub, sub)
        tab_off = pl.multiple_of(jnp.maximum(j * KEY_SUPER - c * Q_BLOCK + SEL_TAB_ZERO, 0) + ci * sub, Q_BLOCK)
        k_aug = jnp.where(own_lanes[h], ksl_ref[pl.ds(key0, sub), :], blkind_ref[h, ci * sub:(ci + 1) * sub, :])
        s = _mm(k_aug, q_aug) + seltab_ref[h, pl.ds(tab_off, sub), :]
        sbuf_ref[h, ci * sub:(ci + 1) * sub, :] = s
        return jnp.max(s, axis=0, keepdims=True)

    def value_chunk(j, ci, h, m_h):
        p = jnp.exp2(sbuf_ref[h, ci * sub:(ci + 1) * sub, :] - m_h).astype(bf16)
        vT = jnp.concatenate([vslT_ref[j * tiles_per_step + ci * tiles_per_sub + i, h * DH:(h + 1) * DH, :]
                              for i in range(tiles_per_sub)], axis=1)
        return _mm(jnp.concatenate([vT, ones_rows(sub)], axis=0), p)

    def sel_body(j, carry):
        m_old, m_cur, acc = carry
        acc = [jnp.exp2(m_old[h] - m_cur[h]) * acc[h] for h in heads]
        m_run = list(m_cur)
        q_aug = [q_with_mask_rows(j + 1, h) for h in heads]
        for ci in range(SEL_SUB_CHUNKS):
            for h in heads:
                acc[h] = acc[h] + value_chunk(j, ci, h, m_cur[h])
            for h in heads:
                m_run[h] = jnp.maximum(m_run[h], score_chunk(j + 1, ci, h, q_aug[h]))
        return m_cur, tuple(m_run), tuple(acc)

    n_steps = lax.shift_right_logical(c, SEL_STEP_SHIFT) + 1
    m_init = tuple(jnp.full((1, GQ), NEG_INF, f32) for _ in heads)
    acc_init = tuple(jnp.zeros((V_ROWS, GQ), f32) for _ in heads)
    m_first = list(m_init)
    win_order = list(backs)
    q_aug0 = [q_with_mask_rows(0, h) for h in heads]
    for ci in range(SEL_SUB_CHUNKS):
        for _ in range(-(-N_WIN_TILES // SEL_SUB_CHUNKS)):
            if win_order:
                win_value(win_order.pop(0))
        for h in heads:
            m_first[h] = jnp.maximum(m_first[h], score_chunk(0, ci, h, q_aug0[h]))
    while win_order:
        win_value(win_order.pop(0))
    o_w = [a[0:DH, :] * (1.0 / a[DH:DH + 1, :]) for a in acc_w]
    m_old, m_cur, acc = lax.fori_loop(0, n_steps - 1, sel_body, (m_init, tuple(m_first), acc_init))
    acc = [jnp.exp2(m_old[h] - m_cur[h]) * acc[h] for h in heads]
    for ci in range(SEL_SUB_CHUNKS):
        for h in heads:
            acc[h] = acc[h] + value_chunk(n_steps - 1, ci, h, m_cur[h])
    o_s = [a[0:DH, :] * (1.0 / jnp.maximum(a[DH:DH + 1, :], TINY)) for a in acc]

    for h in heads:
        ys = []
        for g in range(NSA_GROUP):
            sl = slice(g * Q_BLOCK, (g + 1) * Q_BLOCK)
            row0 = h * NSA_GROUP + g
            gates = [jax.nn.sigmoid(gate_ref[kind * NSA_HEADS + row0:kind * NSA_HEADS + row0 + 1, :])
                     for kind in range(3)]
            ys.append(gates[0] * o_c[h][:, sl] + gates[1] * o_s[h][:, sl] + gates[2] * o_w[h][:, sl])
        yT = jnp.concatenate(ys, axis=0)
        for half in range(2):
            col = (2 * h + half) * LANES
            out_ref[:, col:col + LANES] = yT[half * LANES:(half + 1) * LANES, :].T.astype(out_ref.dtype)


def _nsa(qT, kcmp, vcmpT, ksl, vslT, kwn, vwnT, smallT, tables, B, S):
    assert S % KEY_SUPER == 0
    nq = S // Q_BLOCK
    n_cmp = S // CMP_STRIDE
    n_sel = S // SEL_BLOCK
    seltab, wintab, cmptab = tables
    ovl = jnp.asarray(_overlap_np(n_cmp, n_sel))
    key_blk = np.arange(KEY_SUPER)[:, None] // SEL_BLOCK
    lane = np.arange(NSA_KV_WIDTH)[None, :]
    blkind = jnp.asarray(np.stack([lane - NSA_HEAD_DIM == key_blk, lane == key_blk]), jnp.bfloat16)
    ksl3 = ksl.reshape(B, S, NSA_KV_WIDTH)
    kwn3 = kwn.reshape(B, S, NSA_KV_WIDTH)
    vslT4 = vslT.reshape(B, nq, NSA_KV_WIDTH, Q_BLOCK)
    vwnT4 = vwnT.reshape(B, nq, NSA_KV_WIDTH, Q_BLOCK)
    k_spec = pl.BlockSpec((None, S, NSA_KV_WIDTH), lambda b, c: (b, 0, 0))
    vT_spec = pl.BlockSpec((None, nq, NSA_KV_WIDTH, Q_BLOCK), lambda b, c: (b, 0, 0, 0))
    const = lambda a: pl.BlockSpec(a.shape, lambda b, c: (0,) * a.ndim)
    return pl.pallas_call(
        _nsa_kernel,
        grid=(B, nq),
        in_specs=[pl.BlockSpec((None, NSA_WIDTH, Q_BLOCK), lambda b, c: (b * nq + c, 0, 0)),
                  pl.BlockSpec((None, n_cmp, NSA_KV_WIDTH), lambda b, c: (b, 0, 0)),
                  pl.BlockSpec((None, NSA_KV_WIDTH, n_cmp), lambda b, c: (b, 0, 0)),
                  k_spec, vT_spec, k_spec, vT_spec,
                  pl.BlockSpec((None, N_GATE_ROWS, Q_BLOCK), lambda b, c: (b * nq + c, 0, 0)),
                  const(seltab), const(wintab), const(cmptab), const(ovl), const(blkind)],
        out_specs=pl.BlockSpec((Q_BLOCK, NSA_WIDTH), lambda b, c: (b * nq + c, 0)),
        out_shape=jax.ShapeDtypeStruct((B * S, NSA_WIDTH), jnp.bfloat16),
        scratch_shapes=[pltpu.VMEM((NSA_KV_HEADS, n_sel, Q_BLOCK), jnp.float32),
                        pltpu.VMEM((NSA_KV_HEADS, n_sel, Q_BLOCK), jnp.float32),
                        pltpu.VMEM((NSA_KV_HEADS, KEY_SUPER, GQ), jnp.float32),
                        pltpu.VMEM((NSA_KV_HEADS, N_WIN_TILES * Q_BLOCK, GQ), jnp.float32)],
        compiler_params=_params(2),
        name="nsa",
    )(qT, kcmp, vcmpT, ksl3, vslT4, kwn3, vwnT4, smallT, seltab, wintab, cmptab, ovl, blkind)


CONV_TILE = 256
I_ROW = 24
C_ROWS = MLSTM_V_DIM + 16


def _log_sigmoid(x):
    return jnp.minimum(x, 0.0) - jnp.log(1.0 + jnp.exp(-jnp.abs(x)))


def _mlstm_kernel(mqk_ref, mvT_ref, moT_ref, gates_ref, gb_ref, tri_ref, last_ref, convw_ref, convb_ref,
                  norm_ref, out_ref, qk_ref, rows_ref):
    S = mqk_ref.shape[0]
    nt = S // LANES
    L = MLSTM_CHUNK
    f32 = jnp.float32
    bf16 = jnp.bfloat16
    kscale_row = jnp.where(lax.broadcasted_iota(jnp.int32, (1, 2 * MLSTM_QK_WIDTH), 1) < MLSTM_QK_WIDTH,
                           1.0, MLSTM_QK_DIM ** -0.5)

    def conv_body(i, _):
        t0 = pl.multiple_of(i * CONV_TILE, CONV_TILE)
        cur = mqk_ref[pl.ds(t0, CONV_TILE), :].astype(f32)
        prev_start = pl.multiple_of(jnp.maximum(t0 - 16, 0), 16)
        prev = mqk_ref[pl.ds(prev_start, 16), :].astype(f32)[8:16, :] * jnp.where(i > 0, 1.0, 0.0)
        ext = jnp.concatenate([prev, cur], axis=0)
        y = convb_ref[...]
        for j in range(CONV_WIDTH):
            lo = SUBLANES - (CONV_WIDTH - 1) + j
            y = y + convw_ref[j:j + 1, :] * ext[lo:lo + CONV_TILE, :]
        y = y * jax.nn.sigmoid(y) * kscale_row
        qk_ref[pl.ds(t0, CONV_TILE), :] = y.astype(bf16)
        return 0

    lax.fori_loop(0, S // CONV_TILE, conv_body, 0)

    H = MLSTM_HEADS
    G8 = 2 * H
    n_rows = nt * G8
    a3 = gates_ref[:, I_ROW:I_ROW + G8, :] + gb_ref[...][None]
    is_f = lax.broadcasted_iota(jnp.int32, a3.shape, 1) >= H
    x = jnp.where(is_f, _log_sigmoid(a3), a3).reshape(n_rows, LANES)
    bcum = jnp.dot(x, tri_ref[...], precision=lax.Precision.HIGHEST, preferred_element_type=f32)
    b_rows = pltpu.roll(bcum, n_rows - H, 0)
    g_rows = x - b_rows
    pos = lax.broadcasted_iota(jnp.int32, (n_rows, LANES), 1) & (L - 1)

    def chunk_cummax(a):
        shift = 1
        while shift < L:
            a = jnp.where(pos >= shift, jnp.maximum(a, pltpu.roll(a, shift, 1)), a)
            shift *= 2
        return a

    def chunk_last(a):
        return jnp.dot(a, last_ref[...], precision=lax.Precision.HIGHEST, preferred_element_type=f32)

    bl_rows = chunk_last(b_rows)
    rows_ref[0] = g_rows
    rows_ref[1] = chunk_cummax(g_rows)
    rows_ref[2] = b_rows
    rows_ref[3] = bl_rows
    rows_ref[4] = chunk_last(chunk_cummax(bl_rows + g_rows))

    s_io = lax.broadcasted_iota(jnp.int32, (LANES, LANES), 0)
    t_io = lax.broadcasted_iota(jnp.int32, (LANES, LANES), 1)
    causal = (s_io <= t_io) & ((s_io >= L) == (t_io >= L))
    lane = lax.broadcasted_iota(jnp.int32, (1, LANES), 1)
    in_chunk = [lane < L, lane >= L]
    QD, VD = MLSTM_QK_DIM, MLSTM_V_DIM
    ones_aug = jnp.ones((C_ROWS - VD, LANES), bf16)
    head_lanes = [(lax.broadcasted_iota(jnp.int32, (LANES, LANES), 1) >= QD) == bool(par) for par in range(2)]

    def swap_halves(row):
        return pltpu.roll(row, L, 1)

    def tile_body(i, state):
        t0 = pl.multiple_of(i * LANES, LANES)
        r0 = pl.multiple_of(i * G8, G8)
        qk = qk_ref[pl.ds(t0, LANES), :]
        qk32 = qk.astype(f32)
        pairsT = [qk32[:, j * LANES:(j + 1) * LANES].T for j in range(4)]
        g8, cm8, b8, bl8, wm8 = [rows_ref[j, pl.ds(r0, G8), :] for j in range(5)]
        new_state = []
        for hh in range(H):
            pair, par = hh // 2, hh % 2
            g_r, cm_r, b_r, bl_r, wm_r = [a[hh:hh + 1, :] for a in (g8, cm8, b8, bl8, wm8)]
            caug, m_in = state[hh]
            qT = pairsT[pair][par * QD:(par + 1) * QD, :]
            kT = pairsT[2 + pair][par * QD:(par + 1) * QD, :]
            kpair = qk[:, (2 + pair) * LANES:(3 + pair) * LANES]
            qmask = jnp.where(head_lanes[par], qk[:, pair * LANES:(pair + 1) * LANES], jnp.zeros((), bf16))
            vaug = jnp.concatenate([mvT_ref[i, hh * VD:(hh + 1) * VD, :], ones_aug], axis=0)

            m_mid = swap_halves(jnp.maximum(bl_r + m_in, wm_r))
            m_prev = jnp.where(in_chunk[0], m_in, m_mid)
            m_next = jnp.maximum(bl_r + m_prev, wm_r)
            m_intra = b_r + cm_r
            m_inter = b_r + m_prev
            m_t = jnp.maximum(m_inter, m_intra)
            e_intra = jnp.exp(m_intra - m_t)
            e_inter = jnp.exp(m_inter - m_t)
            decay = jnp.exp(bl_r + m_prev - m_next)
            inject = jnp.exp(wm_r - m_next)

            g_mat = jnp.broadcast_to(g_r, (LANES, LANES)).T
            w = jnp.exp(jnp.where(causal, g_mat - cm_r, NEG_INF))
            st = _nt(kpair, qmask) * (w * e_intra)
            y = _mm(vaug, st.astype(bf16))
            kw = kT * jnp.exp(bl_r + g_r - wm_r)
            cs = caug
            for p in range(LANES // L):
                qs = jnp.where(in_chunk[p], qT * e_inter, 0.0).astype(bf16)
                y = y + _mm(cs.astype(bf16), qs)
                u = _nt(vaug, jnp.where(in_chunk[p], kw, 0.0).astype(bf16))
                dec = decay if p == 0 else swap_halves(decay)
                inj = inject if p == 0 else swap_halves(inject)
                cs = dec[:, 0:QD] * cs + inj[:, 0:QD] * u
            m_out = jnp.where(in_chunk[1], m_next, swap_halves(m_next))
            new_state.append((cs, m_out))

            den = y[VD:VD + 1, :]
            hT = y[0:VD, :] * (1.0 / jnp.maximum(jnp.abs(den), jnp.exp(-m_t)))
            sl = slice(hh * VD, (hh + 1) * VD)
            hn = hT * lax.rsqrt(jnp.mean(hT * hT, axis=0, keepdims=True) + NORM_EPS) * norm_ref[sl, :]
            yT = jax.nn.sigmoid(moT_ref[i, sl, :].astype(f32)) * hn
            out_ref[pl.ds(t0, LANES), sl] = yT.T.astype(out_ref.dtype)
        return tuple(new_state)

    init = tuple((jnp.zeros((C_ROWS, QD), f32), jnp.zeros((1, LANES), f32)) for _ in range(H))
    lax.fori_loop(0, nt, tile_body, init, unroll=4)


def _mlstm(mqk, mvT, moT, smallT, conv_w, conv_b, gate_bias, mlstm_norm, B, S):
    nt = S // LANES
    gb = jnp.broadcast_to(gate_bias.astype(jnp.float32).reshape(2 * MLSTM_HEADS, 1), (2 * MLSTM_HEADS, LANES))
    norm_cols = jnp.broadcast_to(mlstm_norm.astype(jnp.float32).reshape(MLSTM_WIDTH, 1), (MLSTM_WIDTH, LANES))
    lane = np.arange(LANES)
    same_chunk = lane[:, None] // MLSTM_CHUNK == lane[None, :] // MLSTM_CHUNK
    tri = (same_chunk & (lane[:, None] <= lane[None, :])).astype(np.float32)
    last = (same_chunk & (lane[:, None] % MLSTM_CHUNK == MLSTM_CHUNK - 1)).astype(np.float32)
    seq = lambda w: pl.BlockSpec((S, w), lambda b: (b, 0))
    tiles = lambda rows: pl.BlockSpec((nt, rows, LANES), lambda b: (b, 0, 0))
    return pl.pallas_call(
        _mlstm_kernel,
        grid=(B,),
        in_specs=[seq(2 * MLSTM_QK_WIDTH), tiles(MLSTM_WIDTH), tiles(MLSTM_WIDTH), tiles(N_GATE_ROWS),
                  _full((2 * MLSTM_HEADS, LANES)), _full((LANES, LANES)), _full((LANES, LANES)),
                  _full((CONV_WIDTH, 2 * MLSTM_QK_WIDTH)), _full((1, 2 * MLSTM_QK_WIDTH)),
                  _full((MLSTM_WIDTH, LANES))],
        out_specs=seq(MLSTM_WIDTH),
        out_shape=jax.ShapeDtypeStruct((B * S, MLSTM_WIDTH), jnp.bfloat16),
        scratch_shapes=[pltpu.VMEM((S, 2 * MLSTM_QK_WIDTH), jnp.bfloat16),
                        pltpu.VMEM((5, nt * 2 * MLSTM_HEADS, LANES), jnp.float32)],
        compiler_params=_params(1),
        name="mlstm",
    )(mqk, mvT, moT, smallT, gb, jnp.asarray(tri), jnp.asarray(last), conv_w, conv_b.reshape(1, -1), norm_cols)


def _mem_kv_kernel(mem_ref, g_ref, w_ref, k_ref, v_ref):
    mn = _rms(mem_ref[...], g_ref[...]).astype(jnp.bfloat16)
    k_ref[...] = _mm(mn, w_ref[:, :D_MODEL]).astype(k_ref.dtype)
    v_ref[...] = _mm(mn, w_ref[:, D_MODEL:]).astype(v_ref.dtype)


def _mem_kv(mem, gain, w_xkv):
    B, M, _ = mem.shape
    spec = pl.BlockSpec((None, M, D_MODEL), lambda b: (b, 0, 0))
    return pl.pallas_call(
        _mem_kv_kernel,
        grid=(B,),
        in_specs=[spec, _full((1, D_MODEL)), _full((D_MODEL, 2 * D_MODEL))],
        out_specs=[spec, spec],
        out_shape=[jax.ShapeDtypeStruct((B, M, D_MODEL), jnp.bfloat16)] * 2,
        compiler_params=_params(1),
        name="mem_kv",
    )(mem, gain, w_xkv.astype(jnp.bfloat16))


TM_X = 512


def _mix_xattn_kernel(ynsa_ref, yml_ref, x_ref, wout_ref, gpost_ref, gpre_ref, wq_ref, k_ref, v_ref,
                      wo_ref, gpost2_ref, out_ref):
    bf16 = jnp.bfloat16
    y = _mm(ynsa_ref[...], wout_ref[:NSA_WIDTH, :]) + _mm(yml_ref[...], wout_ref[NSA_WIDTH:, :])
    x1 = x_ref[...] + _rms(y, gpost_ref[...])
    h2 = _rms(x1, gpre_ref[...]).astype(bf16)
    q = (_mm(h2, wq_ref[...]) * (XATTN_HEAD_DIM ** -0.5)).astype(bf16)
    outs = []
    for hh in range(XATTN_HEADS):
        sl = slice(hh * XATTN_HEAD_DIM, (hh + 1) * XATTN_HEAD_DIM)
        s = _nt(q[:, sl], k_ref[:, sl])
        p = jnp.exp(s - jnp.max(s, axis=1, keepdims=True))
        l = jnp.sum(p, axis=1, keepdims=True)
        outs.append((_mm(p.astype(bf16), v_ref[:, sl]) * (1.0 / l)).astype(bf16))
    o = jnp.concatenate(outs, axis=1)
    y2 = _mm(o, wo_ref[...])
    out_ref[...] = x1 + _rms(y2, gpost2_ref[...])


def _mix_xattn(ynsa, yml, x2d, w_out, g_post, g_pre, w_xq, kx, vx, w_xo, g_post2, B, S):
    nt = S // TM_X
    M = kx.shape[1]
    tok = lambda w: pl.BlockSpec((TM_X, w), lambda b, i: (b * nt + i, 0))
    mem_spec = pl.BlockSpec((None, M, D_MODEL), lambda b, i: (b, 0, 0))
    sq = _full((D_MODEL, D_MODEL))
    row = _full((1, D_MODEL))
    bf = lambda w: w.astype(jnp.bfloat16)
    return pl.pallas_call(
        _mix_xattn_kernel,
        grid=(B, nt),
        in_specs=[tok(NSA_WIDTH), tok(MLSTM_WIDTH), tok(D_MODEL), sq, row, row, sq, mem_spec, mem_spec,
                  sq, row],
        out_specs=tok(D_MODEL),
        out_shape=jax.ShapeDtypeStruct((B * S, D_MODEL), jnp.float32),
        compiler_params=_params(2),
        name="mix_xattn",
    )(ynsa, yml, x2d, bf(w_out), g_post, g_pre, bf(w_xq), kx, vx, bf(w_xo), g_post2)


TM_F = 512


def _ffn_kernel(x_ref, gpre_ref, wgu_ref, wd_ref, gpost_ref, out_ref, acc_ref):
    bf16 = jnp.bfloat16
    x = x_ref[...]
    h = _rms(x, gpre_ref[...]).astype(bf16)
    acc_ref[...] = jnp.zeros_like(acc_ref)
    for j in range(D_FF // F_TILE):
        cols = slice(j * F_TILE, (j + 1) * F_TILE)
        g = _mm(h, wgu_ref[:, cols])
        u = _mm(h, wgu_ref[:, D_FF + j * F_TILE:D_FF + (j + 1) * F_TILE])
        act = (g * jax.nn.sigmoid(g) * u).astype(bf16)
        acc_ref[...] += _mm(act, wd_ref[cols, :])
    out_ref[...] = x + _rms(acc_ref[...], gpost_ref[...])


def _ffn(x2d, g_pre, w_gate_up, w_down, g_post):
    T = x2d.shape[0]
    tok = pl.BlockSpec((TM_F, D_MODEL), lambda i: (i, 0))
    row = _full((1, D_MODEL))
    return pl.pallas_call(
        _ffn_kernel,
        grid=(T // TM_F,),
        in_specs=[tok, row, _full((D_MODEL, 2 * D_FF)), _full((D_FF, D_MODEL)), row],
        out_specs=tok,
        out_shape=jax.ShapeDtypeStruct((T, D_MODEL), jnp.float32),
        scratch_shapes=[pltpu.VMEM((TM_F, D_MODEL), jnp.float32)],
        compiler_params=_params(1),
        name="ffn",
    )(x2d, g_pre, w_gate_up.astype(jnp.bfloat16), w_down.astype(jnp.bfloat16), g_post)


def _layer(x, mem, rel_bias, mix_norm_pre, w_in, cmp_pos_k, cmp_pos_v, cmp_w1_k, cmp_w2_k, cmp_w1_v,
           cmp_w2_v, conv_w, conv_b, mlstm_gate_bias, mlstm_norm, w_out, mix_norm_post, xattn_norm_pre,
           mem_norm, w_xq, w_xkv, w_xo, xattn_norm_post, ffn_norm_pre, w_gate_up, w_down, ffn_norm_post):
    B, S, _ = x.shape
    row = lambda g: g.reshape(1, -1).astype(jnp.float32)
    x2d = x.reshape(B * S, D_MODEL)
    w_tok, w_feat = _in_proj_weights(w_in)
    (kc, vc, ksl, kwn, mqk, qT, vslT, vwnT, mvT, moT, smallT) = _in_proj(
        x2d, row(mix_norm_pre), w_tok, w_feat)
    kcmp, vcmpT = _compress(kc, vc, cmp_pos_k, cmp_pos_v, cmp_w1_k, cmp_w2_k, cmp_w1_v, cmp_w2_v, B, S)
    tables = _bias_tables(rel_bias.astype(jnp.float32))
    ynsa = _nsa(qT, kcmp, vcmpT, ksl, vslT, kwn, vwnT, smallT, tables, B, S)
    yml = _mlstm(mqk, mvT, moT, smallT, conv_w, conv_b, mlstm_gate_bias, mlstm_norm, B, S)
    kx, vx = _mem_kv(mem, row(mem_norm), w_xkv)
    x2 = _mix_xattn(ynsa, yml, x2d, w_out, row(mix_norm_post), row(xattn_norm_pre), w_xq, kx, vx, w_xo,
                    row(xattn_norm_post), B, S)
    x3 = _ffn(x2, row(ffn_norm_pre), w_gate_up, w_down, row(ffn_norm_post))
    return x3.reshape(B, S, D_MODEL)


def kernel(x, mem, rel_bias, mix_norm_pre, w_in, cmp_pos_k, cmp_pos_v, cmp_w1_k, cmp_w2_k, cmp_w1_v, cmp_w2_v,
           conv_w, conv_b, mlstm_gate_bias, mlstm_norm, w_out, mix_norm_post, xattn_norm_pre, mem_norm, w_xq,
           w_xkv, w_xo, xattn_norm_post, ffn_norm_pre, w_gate_up, w_down, ffn_norm_post):
    depth = w_in.shape[0]
    for l in range(depth):
        x = _layer(x, mem, rel_bias, mix_norm_pre[l], w_in[l], cmp_pos_k[l], cmp_pos_v[l], cmp_w1_k[l],
                   cmp_w2_k[l], cmp_w1_v[l], cmp_w2_v[l], conv_w[l], conv_b[l], mlstm_gate_bias[l],
                   mlstm_norm[l], w_out[l], mix_norm_post[l], xattn_norm_pre[l], mem_norm[l], w_xq[l],
                   w_xkv[l], w_xo[l], xattn_norm_post[l], ffn_norm_pre[l], w_gate_up[l], w_down[l],
                   ffn_norm_post[l])
    return x
```

```python
import functools
import math

import numpy as np
import jax
import jax.numpy as jnp
from jax import lax
from jax.experimental import pallas as pl
from jax.experimental.pallas import tpu as pltpu

D_MODEL = 1024
NSA_WIDTH = 512
NSA_HEAD_DIM = 64
NSA_HEADS = 8
NSA_KV_HEADS = 2
NSA_GROUP = 4
NSA_KV_WIDTH = 128
CMP_STRIDE = 16
CMP_BLOCK = 32
CMP_HIDDEN = 256
SEL_BLOCK = 64
N_SELECT = 16
WINDOW = 512
Q_BLOCK = 128
FORCED_SCORE = 1.0e4
MLSTM_WIDTH = 512
MLSTM_HEADS = 4
MLSTM_V_DIM = 128
MLSTM_QK_DIM = 64
MLSTM_QK_WIDTH = 256
MLSTM_CHUNK = 64
CONV_WIDTH = 4
REL_BUCKETS = 32
REL_MAX_DISTANCE = 128
XATTN_HEADS = 4
XATTN_HEAD_DIM = 256
D_FF = 2816
NORM_EPS = 1e-6
NEG_INF = -1.0e30
LOG2E = math.log2(math.e)

IN_SIZES = (NSA_WIDTH,) + (NSA_KV_WIDTH,) * 6 + (NSA_HEADS * 3, MLSTM_QK_WIDTH, MLSTM_QK_WIDTH,
                                                 MLSTM_WIDTH, MLSTM_HEADS, MLSTM_HEADS, MLSTM_WIDTH)
IN_OFFSETS = tuple(int(o) for o in np.cumsum((0,) + IN_SIZES)[:-1])

LANES = 128
SUBLANES = 8
VMEM_LIMIT_BYTES = 56 * 1024 * 1024

N_GATE_ROWS = 32
F_TILE = 256


def _rms(x, gain):
    return x * lax.rsqrt(jnp.mean(x * x, axis=-1, keepdims=True) + NORM_EPS) * gain


def _nt(a, b):
    return lax.dot_general(a, b, (((1,), (1,)), ((), ())), preferred_element_type=jnp.float32)


def _mm(a, b):
    return jnp.dot(a, b, preferred_element_type=jnp.float32)


def _params(n_axes, flags=None):
    return pltpu.CompilerParams(dimension_semantics=("arbitrary",) * n_axes,
                                vmem_limit_bytes=VMEM_LIMIT_BYTES, flags=flags)


def _full(shape):
    nd = len(shape)
    return pl.BlockSpec(shape, lambda *_: (0,) * nd)


TM_IN = 512
_TOK_GROUPS = (("kc", 128, jnp.float32), ("vc", 128, jnp.float32), ("ksl", 128, jnp.bfloat16),
               ("kwn", 128, jnp.bfloat16), ("mqk", 512, jnp.bfloat16))
_FEAT_GROUPS = (("qT", 512, jnp.bfloat16), ("vslT", 128, jnp.bfloat16), ("vwnT", 128, jnp.bfloat16),
                ("mvT", 512, jnp.bfloat16), ("moT", 512, jnp.bfloat16), ("smallT", N_GATE_ROWS, jnp.float32))


def _in_proj_kernel(x_ref, g_ref, wtok_ref, wfeat_ref, *out_refs):
    h = _rms(x_ref[...], g_ref[...]).astype(jnp.bfloat16)
    n_tok = len(_TOK_GROUPS)
    off = 0
    for (name, width, dt), o_ref in zip(_TOK_GROUPS, out_refs[:n_tok]):
        o_ref[...] = _mm(h, wtok_ref[:, off:off + width]).astype(dt)
        off += width
    off = 0
    for (name, rows, dt), o_ref in zip(_FEAT_GROUPS, out_refs[n_tok:]):
        r = _nt(wfeat_ref[off:off + rows, :], h)
        if name == "qT":
            r = r * (NSA_HEAD_DIM ** -0.5 * LOG2E)
        for j in range(TM_IN // LANES):
            o_ref[j] = r[:, j * LANES:(j + 1) * LANES].astype(dt)
        off += rows


def _in_proj(x2d, gain, w_tok, w_feat):
    T = x2d.shape[0]
    n_tok_cols = w_tok.shape[1]
    n_feat_rows = w_feat.shape[0]
    out_shape, out_specs = [], []
    for name, width, dt in _TOK_GROUPS:
        out_shape.append(jax.ShapeDtypeStruct((T, width), dt))
        out_specs.append(pl.BlockSpec((TM_IN, width), lambda i: (i, 0)))
    for name, rows, dt in _FEAT_GROUPS:
        out_shape.append(jax.ShapeDtypeStruct((T // LANES, rows, LANES), dt))
        out_specs.append(pl.BlockSpec((TM_IN // LANES, rows, LANES), lambda i: (i, 0, 0)))
    return pl.pallas_call(
        _in_proj_kernel,
        grid=(T // TM_IN,),
        in_specs=[pl.BlockSpec((TM_IN, D_MODEL), lambda i: (i, 0)),
                  _full((1, D_MODEL)),
                  _full((D_MODEL, n_tok_cols)),
                  _full((n_feat_rows, D_MODEL))],
        out_specs=out_specs,
        out_shape=out_shape,
        compiler_params=_params(1),
        name="in_proj",
    )(x2d, gain, w_tok, w_feat)


def _in_proj_weights(w_in):
    (nq, kc, vc, ksl, vsl, kwn, vwn, gt, mq, mk, mv, mi, mf, mo) = [
        w_in[:, o:o + s] for o, s in zip(IN_OFFSETS, IN_SIZES)]
    gt_r = gt.reshape(D_MODEL, NSA_KV_HEADS, NSA_GROUP, 3).transpose(0, 3, 1, 2).reshape(D_MODEL, 24)
    small = jnp.concatenate([gt_r, mi, mf], axis=1)
    w_tok = jnp.concatenate([kc, vc, ksl, kwn, mq, mk], axis=1)
    w_feat = jnp.concatenate([nq, vsl, vwn, mv, mo, small], axis=1).T
    return w_tok.astype(jnp.bfloat16), w_feat.astype(jnp.bfloat16)


N_CHUNK_COLS = CMP_STRIDE * NSA_KV_WIDTH
N_HID2 = NSA_KV_HEADS * CMP_HIDDEN


def _compress_one(c, pos_ref, w1_ref, n_chunks):
    lo = _mm((c + pos_ref[0:1, :]).astype(jnp.bfloat16), w1_ref[0])
    hi = _mm((c + pos_ref[1:2, :]).astype(jnp.bfloat16), w1_ref[1])
    pre = lo + pltpu.roll(hi, n_chunks - 1, 0)
    return (pre * jax.nn.sigmoid(pre)).astype(jnp.bfloat16)


def _chunk_rows(ref, n_chunks):
    return jnp.concatenate([ref[pl.ds(t, n_chunks, stride=CMP_STRIDE), :] for t in range(CMP_STRIDE)], axis=1)


def _compress_kernel(kc_ref, vc_ref, posk_ref, posv_ref, w1k_ref, w1v_ref, w2k_ref, w2vT_ref,
                     kcmp_ref, vcmpT_ref):
    n_chunks = kc_ref.shape[0] // CMP_STRIDE
    hid_k = _compress_one(_chunk_rows(kc_ref, n_chunks), posk_ref, w1k_ref, n_chunks)
    kcmp = _mm(hid_k, w2k_ref[...])
    row = lax.broadcasted_iota(jnp.int32, kcmp.shape, 0)
    kcmp_ref[...] = jnp.where(row < n_chunks - 1, kcmp, 0.0).astype(kcmp_ref.dtype)
    hid_v = _compress_one(_chunk_rows(vc_ref, n_chunks), posv_ref, w1v_ref, n_chunks)
    vcmpT = _nt(w2vT_ref[...], hid_v)
    col = lax.broadcasted_iota(jnp.int32, vcmpT.shape, 1)
    vcmpT_ref[...] = jnp.where(col < n_chunks - 1, vcmpT, 0.0).astype(vcmpT_ref.dtype)


def _compress_weights(pos, w1, w2):
    eye = jnp.eye(NSA_KV_HEADS, dtype=w1.dtype)
    w1r = w1.reshape(2, CMP_STRIDE, NSA_HEAD_DIM, CMP_HIDDEN)
    w1e = jnp.einsum('atdj,hg->athdgj', w1r, eye).reshape(2, N_CHUNK_COLS, N_HID2)
    pos_e = jnp.broadcast_to(pos.reshape(2, CMP_STRIDE, 1, NSA_HEAD_DIM),
                             (2, CMP_STRIDE, NSA_KV_HEADS, NSA_HEAD_DIM)).reshape(2, N_CHUNK_COLS)
    w2e = jnp.einsum('jd,hg->hjgd', w2, eye).reshape(N_HID2, NSA_KV_WIDTH)
    return pos_e, w1e.astype(jnp.bfloat16), w2e.astype(jnp.bfloat16)


def _compress(kc, vc, cmp_pos_k, cmp_pos_v, cmp_w1_k, cmp_w2_k, cmp_w1_v, cmp_w2_v, B, S):
    n_chunks = S // CMP_STRIDE
    posk, w1k, w2k = _compress_weights(cmp_pos_k, cmp_w1_k, cmp_w2_k)
    posv, w1v, w2v = _compress_weights(cmp_pos_v, cmp_w1_v, cmp_w2_v)
    chunk_spec = pl.BlockSpec((S, NSA_KV_WIDTH), lambda b: (b, 0))
    return pl.pallas_call(
        _compress_kernel,
        grid=(B,),
        in_specs=[chunk_spec, chunk_spec,
                  _full((2, N_CHUNK_COLS)), _full((2, N_CHUNK_COLS)),
                  _full((2, N_CHUNK_COLS, N_HID2)), _full((2, N_CHUNK_COLS, N_HID2)),
                  _full((N_HID2, NSA_KV_WIDTH)), _full((NSA_KV_WIDTH, N_HID2))],
        out_specs=[pl.BlockSpec((None, n_chunks, NSA_KV_WIDTH), lambda b: (b, 0, 0)),
                   pl.BlockSpec((None, NSA_KV_WIDTH, n_chunks), lambda b: (b, 0, 0))],
        out_shape=[jax.ShapeDtypeStruct((B, n_chunks, NSA_KV_WIDTH), jnp.bfloat16),
                   jax.ShapeDtypeStruct((B, NSA_KV_WIDTH, n_chunks), jnp.bfloat16)],
        compiler_params=_params(1),
        name="compress",
    )(kc, vc, posk, posv, w1k, w1v, w2k, w2v.T)


GQ = NSA_GROUP * Q_BLOCK
TINY = 1e-30
CMP_TAB_ROWS = 512
CMP_TAB_ZERO = 248
CMP_TAB_LOOKUP = (232, 256)
SEL_STEP_SHIFT = 2
KEY_SUPER = Q_BLOCK << SEL_STEP_SHIFT
SEL_TAB_ZERO = KEY_SUPER + Q_BLOCK
SEL_TAB_ROWS = SEL_TAB_ZERO + KEY_SUPER
N_WIN_TILES = WINDOW // Q_BLOCK + 1
SEL_SUB_CHUNKS = 2
V_ROWS = NSA_HEAD_DIM + 16


def _bucket_np(dist):
    n = np.maximum(dist, 0)
    max_exact = REL_BUCKETS // 2
    nf = np.maximum(n, 1).astype(np.float64)
    large = max_exact + (np.log(nf / max_exact) / math.log(REL_MAX_DISTANCE / max_exact)
                         * (REL_BUCKETS - max_exact)).astype(np.int64)
    large = np.minimum(large, REL_BUCKETS - 1)
    return np.where(n < max_exact, n, large).astype(np.int32)


def _bias_index_tables():
    m = np.arange(Q_BLOCK)[:, None]
    r = np.arange(Q_BLOCK)[None, :]
    diag = np.where(r - m >= 0, _bucket_np(r - m), -1).astype(np.int32)
    off = _bucket_np(Q_BLOCK + r - m)
    jp = np.arange(*CMP_TAB_LOOKUP)[:, None] - CMP_TAB_ZERO
    d_c = r - CMP_STRIDE * jp - (CMP_BLOCK - 1)
    cmp_idx = np.where(d_c >= 0, _bucket_np(d_c), -1).astype(np.int32)
    return diag, off, cmp_idx


def _bias_tables_kernel(rb_ref, diag_idx_ref, off_idx_ref, cmp_idx_ref, sel_ref, win_ref, cmp_ref):
    f32 = jnp.float32

    def lookup(idx, head):
        far = rb_ref[head, REL_BUCKETS - 1]
        acc = jnp.full(idx.shape, NEG_INF, f32)
        for k in range(REL_BUCKETS):
            acc = jnp.where(idx == k, (rb_ref[head, k] - far) * LOG2E, acc)
        return acc

    m_io = lax.broadcasted_iota(jnp.int32, (Q_BLOCK, Q_BLOCK), 0)
    r_io = lax.broadcasted_iota(jnp.int32, (Q_BLOCK, Q_BLOCK), 1)
    neg_tile = jnp.full((Q_BLOCK, Q_BLOCK), NEG_INF, f32)
    lo, hi = CMP_TAB_LOOKUP
    for h in range(NSA_KV_HEADS):
        for g in range(NSA_GROUP):
            head = h * NSA_GROUP + g
            sl = slice(g * Q_BLOCK, (g + 1) * Q_BLOCK)
            far = 0.0
            far_tile = jnp.full((Q_BLOCK, Q_BLOCK), far, f32)
            diag_v = lookup(diag_idx_ref[...], head)
            off_v = lookup(off_idx_ref[...], head)
            n_far = (SEL_TAB_ZERO - Q_BLOCK) // Q_BLOCK
            for t in range(SEL_TAB_ROWS // Q_BLOCK):
                rows = slice(t * Q_BLOCK, (t + 1) * Q_BLOCK)
                tile = far_tile if t < n_far else off_v if t == n_far else diag_v if t == n_far + 1 else neg_tile
                sel_ref[h, rows, sl] = tile
            win_ref[h, 0, :, sl] = diag_v
            win_ref[h, 1, :, sl] = off_v
            for back in range(2, N_WIN_TILES - 1):
                win_ref[h, back, :, sl] = far_tile
            win_ref[h, N_WIN_TILES - 1, :, sl] = jnp.where(r_io < m_io, far, NEG_INF)
            win_ref[h, N_WIN_TILES, :, sl] = neg_tile
            cmp_ref[h, 0:lo, sl] = jnp.full((lo, Q_BLOCK), far, f32)
            cmp_ref[h, lo:hi, sl] = lookup(cmp_idx_ref[...], head)
            cmp_ref[h, hi:CMP_TAB_ROWS, sl] = jnp.full((CMP_TAB_ROWS - hi, Q_BLOCK), NEG_INF, f32)


def _bias_tables(rel_bias):
    diag_idx, off_idx, cmp_idx = _bias_index_tables()
    shapes = [(NSA_KV_HEADS, SEL_TAB_ROWS, GQ), (NSA_KV_HEADS, N_WIN_TILES + 1, Q_BLOCK, GQ),
              (NSA_KV_HEADS, CMP_TAB_ROWS, GQ)]
    return pl.pallas_call(
        _bias_tables_kernel,
        in_specs=[pl.BlockSpec(memory_space=pltpu.SMEM),
                  _full(diag_idx.shape), _full(off_idx.shape), _full(cmp_idx.shape)],
        out_specs=[_full(s) for s in shapes],
        out_shape=[jax.ShapeDtypeStruct(s, jnp.float32) for s in shapes],
        grid=(1,),
        compiler_params=_params(1),
        name="bias_tables",
    )(rel_bias, jnp.asarray(diag_idx), jnp.asarray(off_idx), jnp.asarray(cmp_idx))


def _overlap_np(n_cmp_rows, n_sel):
    cmp_start = np.arange(n_cmp_rows) * CMP_STRIDE
    cmp_end = cmp_start + CMP_BLOCK - 1
    sel_start = np.arange(n_sel) * SEL_BLOCK
    ov = ((cmp_start[None, :] <= sel_start[:, None] + SEL_BLOCK - 1)
          & (cmp_end[None, :] >= sel_start[:, None])).astype(np.float32)
    ov[:, n_cmp_rows - 1] = 0.0
    return ov


def _tile4(a):
    return jnp.concatenate([a] * NSA_GROUP, axis=1)


def _select_blocks(score, score_ref, n_top, hooks=()):
    n_sel = score.shape[0]
    score_ref[...] = score
    n_grp = n_sel // SUBLANES
    grp = [score[SUBLANES * v:SUBLANES * (v + 1), :] for v in range(n_grp)]
    cnt = [jnp.zeros((SUBLANES, Q_BLOCK), jnp.int32) for _ in range(n_grp)]
    sub_io = lax.broadcasted_iota(jnp.int32, (SUBLANES, Q_BLOCK), 0)
    hook_at = {(i * n_sel) // len(hooks): hk for i, hk in enumerate(hooks)} if hooks else {}
    for jp in range(n_sel):
        if jp in hook_at:
            hook_at[jp]()
        row = score_ref[jp:jp + 1, :]
        for v in range(n_grp):
            if SUBLANES * v > jp:
                inc = (row >= grp[v]).astype(jnp.int32)
            elif SUBLANES * (v + 1) - 1 < jp:
                inc = (row > grp[v]).astype(jnp.int32)
            else:
                tie = (sub_io > jp - SUBLANES * v).astype(jnp.int32)
                inc = jnp.where(row > grp[v], 1, jnp.where(row == grp[v], tie, 0))
            cnt[v] = cnt[v] + inc
    return [jnp.where(cnt[v] < n_top, 0.0, NEG_INF) for v in range(n_grp)]


def _nsa_kernel(q_ref, kcmp_ref, vcmpT_ref, ksl_ref, vslT_ref, kwn_ref, vwnT_ref, gate_ref,
                seltab_ref, wintab_ref, cmptab_ref, ovl_ref, blkind_ref, out_ref, score_ref, selb_ref, sbuf_ref,
                sbuf2_ref, swin_ref):
    c = pl.program_id(1)
    n_cmp = kcmp_ref.shape[0]
    n_sel = ovl_ref.shape[0]
    n_top = min(N_SELECT, n_sel)
    f32 = jnp.float32
    bf16 = jnp.bfloat16
    DH = NSA_HEAD_DIM
    heads = range(NSA_KV_HEADS)

    q = q_ref[...]
    zq = jnp.zeros((DH, GQ), bf16)
    qcat, qpad = [], []
    for h in heads:
        qcat.append(jnp.concatenate([q[(h * NSA_GROUP + g) * DH:(h * NSA_GROUP + g + 1) * DH, :]
                                     for g in range(NSA_GROUP)], axis=1))
        qpad.append(jnp.concatenate([qcat[h], zq] if h == 0 else [zq, qcat[h]], axis=0))

    backs = list(range(N_WIN_TILES))
    kts = [jnp.maximum(c - back, 0) for back in backs]
    slots = [jnp.where(c >= back, back, N_WIN_TILES) for back in backs]
    m_win = [jnp.full((1, GQ), NEG_INF, f32) for _ in heads]

    def win_score(back, h):
        key0 = pl.multiple_of(kts[back] * Q_BLOCK, Q_BLOCK)
        s = _mm(kwn_ref[pl.ds(key0, Q_BLOCK), :], qpad[h]) + wintab_ref[h, slots[back]]
        swin_ref[h, back * Q_BLOCK:(back + 1) * Q_BLOCK, :] = s
        m_win[h] = jnp.maximum(m_win[h], jnp.max(s, axis=0, keepdims=True))

    cmp_off = pl.multiple_of(CMP_TAB_ZERO - (Q_BLOCK // CMP_STRIDE) * c, SUBLANES)
    kcmp = kcmp_ref[...]
    j_io = lax.broadcasted_iota(jnp.int32, (n_sel, Q_BLOCK), 0)
    r_io = lax.broadcasted_iota(jnp.int32, (n_sel, Q_BLOCK), 1)
    cur = (Q_BLOCK // SEL_BLOCK) * c + (r_io >= SEL_BLOCK).astype(jnp.int32)
    forced = (j_io == 0) | (j_io == cur) | (j_io == cur - 1)
    visible = j_io <= cur
    o_c = []
    for h in heads:
        tab = cmptab_ref[h, pl.ds(cmp_off, n_cmp), :]
        s = _mm(kcmp, qpad[h]) + tab
        m = jnp.max(s, axis=0, keepdims=True)
        p = jnp.where(tab > 0.5 * NEG_INF, jnp.exp2(s - m), 0.0)
        l = jnp.sum(p, axis=0, keepdims=True)
        pn = p * (1.0 / jnp.maximum(l, TINY))
        o_c.append(_mm(vcmpT_ref[h * DH:(h + 1) * DH, :], pn.astype(bf16)))
        psum = pn[:, 0:Q_BLOCK]
        for g in range(1, NSA_GROUP):
            psum = psum + pn[:, g * Q_BLOCK:(g + 1) * Q_BLOCK]
        imp = jnp.dot(ovl_ref[...], psum, precision=lax.Precision.HIGHEST,
                      preferred_element_type=f32)
        score = jnp.where(forced, FORCED_SCORE, jnp.where(visible, imp, -1.0))
        rows = _select_blocks(score, score_ref.at[h], n_top,
                              hooks=[functools.partial(win_score, back, h) for back in backs])
        for v, blk in enumerate(rows):
            selb_ref[h, SUBLANES * v:SUBLANES * (v + 1), :] = blk

    def ones_rows(n_keys):
        return jnp.ones((V_ROWS - DH, n_keys), bf16)

    acc_w = [jnp.zeros((V_ROWS, GQ), f32) for _ in heads]

    def win_value(back):
        for h in heads:
            p = jnp.exp2(swin_ref[h, back * Q_BLOCK:(back + 1) * Q_BLOCK, :] - m_win[h]).astype(bf16)
            vT = jnp.concatenate([vwnT_ref[kts[back], h * DH:(h + 1) * DH, :], ones_rows(Q_BLOCK)], axis=0)
            acc_w[h] = acc_w[h] + _mm(vT, p)

    blocks_per_step = KEY_SUPER // SEL_BLOCK
    tiles_per_step = KEY_SUPER // Q_BLOCK

    sub = KEY_SUPER // SEL_SUB_CHUNKS
    blocks_per_sub = sub // SEL_BLOCK
    tiles_per_sub = sub // Q_BLOCK

    own_lanes = [(lax.broadcasted_iota(jnp.int32, (sub, NSA_KV_WIDTH), 1) >= DH) == bool(h) for h in heads]

    def q_with_mask_rows(j, h):
        blk0 = pl.multiple_of(j * blocks_per_step, blocks_per_step)
        rows = jnp.concatenate([_tile4(selb_ref[h, pl.ds(blk0, blocks_per_step), :]),
                                jnp.zeros((DH - blocks_per_step, GQ), f32)], axis=0).astype(bf16)
        return jnp.concatenate([qcat[h], rows] if h == 0 else [rows, qcat[h]], axis=0)

    def score_chunk(j, ci, h, q_aug, buf_ref):
        key0 = pl.multiple_of(j * KEY_SUPER + ci * sub, sub)
        tab_off = pl.multiple_of(jnp.maximum(j * KEY_SUPER - c * Q_BLOCK + SEL_TAB_ZERO, 0) + ci * sub, Q_BLOCK)
        k_aug = jnp.where(own_lanes[h], ksl_ref[pl.ds(key0, sub), :], blkind_ref[h, ci * sub:(ci + 1) * sub, :])
        s = _mm(k_aug, q_aug) + seltab_ref[h, pl.ds(tab_off, sub), :]
        buf_ref[h, ci * sub:(ci + 1) * sub, :] = s
        return jnp.max(s, axis=0, keepdims=True)

    def value_chunk(j, ci, h, m_h, buf_ref):
        p = jnp.exp2(buf_ref[h, ci * sub:(ci + 1) * sub, :] - m_h).astype(bf16)
        vT = jnp.concatenate([vslT_ref[j * tiles_per_step + ci * tiles_per_sub + i, h * DH:(h + 1) * DH, :]
                              for i in range(tiles_per_sub)], axis=1)
        return _mm(jnp.concatenate([vT, ones_rows(sub)], axis=0), p)

    def values(j, m_old, m_cur, acc, src_ref, before_chunk=None):
        acc = [jnp.exp2(m_old[h] - m_cur[h]) * acc[h] for h in heads]
        for ci in range(SEL_SUB_CHUNKS):
            if before_chunk is not None:
                before_chunk(ci)
            for h in heads:
                acc[h] = acc[h] + value_chunk(j, ci, h, m_cur[h], src_ref)
        return tuple(acc)

    def sel_step(j, carry, src_ref, dst_ref):
        m_old, m_cur, acc = carry
        m_run = list(m_cur)
        q_aug = [q_with_mask_rows(j + 1, h) for h in heads]

        def scores(ci):
            for h in heads:
                m_run[h] = jnp.maximum(m_run[h], score_chunk(j + 1, ci, h, q_aug[h], dst_ref))

        acc = values(j, m_old, m_cur, acc, src_ref, before_chunk=scores)
        return m_cur, tuple(m_run), acc

    def sel_pair(i, carry):
        carry = sel_step(2 * i, carry, sbuf_ref, sbuf2_ref)
        return sel_step(2 * i + 1, carry, sbuf2_ref, sbuf_ref)

    n_steps = lax.shift_right_logical(c, SEL_STEP_SHIFT) + 1
    m_init = tuple(jnp.full((1, GQ), NEG_INF, f32) for _ in heads)
    acc_init = tuple(jnp.zeros((V_ROWS, GQ), f32) for _ in heads)
    m_first = list(m_init)
    win_order = list(backs)
    q_aug0 = [q_with_mask_rows(0, h) for h in heads]
    for ci in range(SEL_SUB_CHUNKS):
        for _ in range(-(-N_WIN_TILES // SEL_SUB_CHUNKS)):
            if win_order:
                win_value(win_order.pop(0))
        for h in heads:
            m_first[h] = jnp.maximum(m_first[h], score_chunk(0, ci, h, q_aug0[h], sbuf_ref))
    while win_order:
        win_value(win_order.pop(0))
    o_w = [a[0:DH, :] * (1.0 / a[DH:DH + 1, :]) for a in acc_w]
    n_piped = n_steps - 1
    odd = n_piped & 1
    carry = lax.fori_loop(0, lax.shift_right_logical(n_piped, 1), sel_pair, (m_init, tuple(m_first), acc_init))
    carry = lax.fori_loop(0, odd, lambda _, cr: sel_step(n_piped - 1, cr, sbuf_ref, sbuf2_ref), carry)
    m_old, m_cur, acc = carry
    acc = lax.fori_loop(0, 1 - odd, lambda _, a: values(n_piped, m_old, m_cur, a, sbuf_ref), acc)
    acc = lax.fori_loop(0, odd, lambda _, a: values(n_piped, m_old, m_cur, a, sbuf2_ref), acc)
    o_s = [a[0:DH, :] * (1.0 / jnp.maximum(a[DH:DH + 1, :], TINY)) for a in acc]

    for h in heads:
        ys = []
        for g in range(NSA_GROUP):
            sl = slice(g * Q_BLOCK, (g + 1) * Q_BLOCK)
            row0 = h * NSA_GROUP + g
            gates = [jax.nn.sigmoid(gate_ref[kind * NSA_HEADS + row0:kind * NSA_HEADS + row0 + 1, :])
                     for kind in range(3)]
            ys.append(gates[0] * o_c[h][:, sl] + gates[1] * o_s[h][:, sl] + gates[2] * o_w[h][:, sl])
        yT = jnp.concatenate(ys, axis=0)
        for half in range(2):
            col = (2 * h + half) * LANES
            out_ref[:, col:col + LANES] = yT[half * LANES:(half + 1) * LANES, :].T.astype(out_ref.dtype)


def _nsa(qT, kcmp, vcmpT, ksl, vslT, kwn, vwnT, smallT, tables, B, S):
    assert S % KEY_SUPER == 0
    nq = S // Q_BLOCK
    n_cmp = S // CMP_STRIDE
    n_sel = S // SEL_BLOCK
    seltab, wintab, cmptab = tables
    ovl = jnp.asarray(_overlap_np(n_cmp, n_sel))
    key_blk = np.arange(KEY_SUPER)[:, None] // SEL_BLOCK
    lane = np.arange(NSA_KV_WIDTH)[None, :]
    blkind = jnp.asarray(np.stack([lane - NSA_HEAD_DIM == key_blk, lane == key_blk]), jnp.bfloat16)
    ksl3 = ksl.reshape(B, S, NSA_KV_WIDTH)
    kwn3 = kwn.reshape(B, S, NSA_KV_WIDTH)
    vslT4 = vslT.reshape(B, nq, NSA_KV_WIDTH, Q_BLOCK)
    vwnT4 = vwnT.reshape(B, nq, NSA_KV_WIDTH, Q_BLOCK)
    k_spec = pl.BlockSpec((None, S, NSA_KV_WIDTH), lambda b, c: (b, 0, 0))
    vT_spec = pl.BlockSpec((None, nq, NSA_KV_WIDTH, Q_BLOCK), lambda b, c: (b, 0, 0, 0))
    const = lambda a: pl.BlockSpec(a.shape, lambda b, c: (0,) * a.ndim)
    return pl.pallas_call(
        _nsa_kernel,
        grid=(B, nq),
        in_specs=[pl.BlockSpec((None, NSA_WIDTH, Q_BLOCK), lambda b, c: (b * nq + c, 0, 0)),
                  pl.BlockSpec((None, n_cmp, NSA_KV_WIDTH), lambda b, c: (b, 0, 0)),
                  pl.BlockSpec((None, NSA_KV_WIDTH, n_cmp), lambda b, c: (b, 0, 0)),
                  k_spec, vT_spec, k_spec, vT_spec,
                  pl.BlockSpec((None, N_GATE_ROWS, Q_BLOCK), lambda b, c: (b * nq + c, 0, 0)),
                  const(seltab), const(wintab), const(cmptab), const(ovl), const(blkind)],
        out_specs=pl.BlockSpec((Q_BLOCK, NSA_WIDTH), lambda b, c: (b * nq + c, 0)),
        out_shape=jax.ShapeDtypeStruct((B * S, NSA_WIDTH), jnp.bfloat16),
        scratch_shapes=[pltpu.VMEM((NSA_KV_HEADS, n_sel, Q_BLOCK), jnp.float32),
                        pltpu.VMEM((NSA_KV_HEADS, n_sel, Q_BLOCK), jnp.float32),
                        pltpu.VMEM((NSA_KV_HEADS, KEY_SUPER, GQ), jnp.float32),
                        pltpu.VMEM((NSA_KV_HEADS, KEY_SUPER, GQ), jnp.float32),
                        pltpu.VMEM((NSA_KV_HEADS, N_WIN_TILES * Q_BLOCK, GQ), jnp.float32)],
        compiler_params=_params(2),
        name="nsa",
    )(qT, kcmp, vcmpT, ksl3, vslT4, kwn3, vwnT4, smallT, seltab, wintab, cmptab, ovl, blkind)


CONV_TILE = 256
I_ROW = 24
C_ROWS = MLSTM_V_DIM + 16


def _log_sigmoid(x):
    return jnp.minimum(x, 0.0) - jnp.log(1.0 + jnp.exp(-jnp.abs(x)))


def _mlstm_kernel(mqk_ref, mvT_ref, moT_ref, gates_ref, gb_ref, tri_ref, last_ref, convw_ref, convb_ref,
                  norm_ref, out_ref, qk_ref, rows_ref):
    S = mqk_ref.shape[0]
    nt = S // LANES
    L = MLSTM_CHUNK
    f32 = jnp.float32
    bf16 = jnp.bfloat16
    kscale_row = jnp.where(lax.broadcasted_iota(jnp.int32, (1, 2 * MLSTM_QK_WIDTH), 1) < MLSTM_QK_WIDTH,
                           1.0, MLSTM_QK_DIM ** -0.5)

    def conv_body(i, _):
        t0 = pl.multiple_of(i * CONV_TILE, CONV_TILE)
        cur = mqk_ref[pl.ds(t0, CONV_TILE), :].astype(f32)
        prev_start = pl.multiple_of(jnp.maximum(t0 - 16, 0), 16)
        prev = mqk_ref[pl.ds(prev_start, 16), :].astype(f32)[8:16, :] * jnp.where(i > 0, 1.0, 0.0)
        ext = jnp.concatenate([prev, cur], axis=0)
        y = convb_ref[...]
        for j in range(CONV_WIDTH):
            lo = SUBLANES - (CONV_WIDTH - 1) + j
            y = y + convw_ref[j:j + 1, :] * ext[lo:lo + CONV_TILE, :]
        y = y * jax.nn.sigmoid(y) * kscale_row
        qk_ref[pl.ds(t0, CONV_TILE), :] = y.astype(bf16)
        return 0

    lax.fori_loop(0, S // CONV_TILE, conv_body, 0)

    H = MLSTM_HEADS
    G8 = 2 * H
    n_rows = nt * G8
    a3 = gates_ref[:, I_ROW:I_ROW + G8, :] + gb_ref[...][None]
    is_f = lax.broadcasted_iota(jnp.int32, a3.shape, 1) >= H
    x = jnp.where(is_f, _log_sigmoid(a3), a3).reshape(n_rows, LANES)
    bcum = jnp.dot(x, tri_ref[...], precision=lax.Precision.HIGHEST, preferred_element_type=f32)
    b_rows = pltpu.roll(bcum, n_rows - H, 0)
    g_rows = x - b_rows
    pos = lax.broadcasted_iota(jnp.int32, (n_rows, LANES), 1) & (L - 1)

    def chunk_cummax(a):
        shift = 1
        while shift < L:
            a = jnp.where(pos >= shift, jnp.maximum(a, pltpu.roll(a, shift, 1)), a)
            shift *= 2
        return a

    def chunk_last(a):
        return jnp.dot(a, last_ref[...], precision=lax.Precision.HIGHEST, preferred_element_type=f32)

    bl_rows = chunk_last(b_rows)
    rows_ref[0] = g_rows
    rows_ref[1] = chunk_cummax(g_rows)
    rows_ref[2] = b_rows
    rows_ref[3] = bl_rows
    rows_ref[4] = chunk_last(chunk_cummax(bl_rows + g_rows))

    s_io = lax.broadcasted_iota(jnp.int32, (LANES, LANES), 0)
    t_io = lax.broadcasted_iota(jnp.int32, (LANES, LANES), 1)
    causal = (s_io <= t_io) & ((s_io >= L) == (t_io >= L))
    lane = lax.broadcasted_iota(jnp.int32, (1, LANES), 1)
    in_chunk = [lane < L, lane >= L]
    QD, VD = MLSTM_QK_DIM, MLSTM_V_DIM
    ones_aug = jnp.ones((C_ROWS - VD, LANES), bf16)
    head_lanes = [(lax.broadcasted_iota(jnp.int32, (LANES, LANES), 1) >= QD) == bool(par) for par in range(2)]

    def swap_halves(row):
        return pltpu.roll(row, L, 1)

    def tile_body(i, state):
        t0 = pl.multiple_of(i * LANES, LANES)
        r0 = pl.multiple_of(i * G8, G8)
        qk = qk_ref[pl.ds(t0, LANES), :]
        qk32 = qk.astype(f32)
        pairsT = [qk32[:, j * LANES:(j + 1) * LANES].T for j in range(4)]
        g8, cm8, b8, bl8, wm8 = [rows_ref[j, pl.ds(r0, G8), :] for j in range(5)]
        new_state = []
        for hh in range(H):
            pair, par = hh // 2, hh % 2
            g_r, cm_r, b_r, bl_r, wm_r = [a[hh:hh + 1, :] for a in (g8, cm8, b8, bl8, wm8)]
            caug, m_in = state[hh]
            qT = pairsT[pair][par * QD:(par + 1) * QD, :]
            kT = pairsT[2 + pair][par * QD:(par + 1) * QD, :]
            kpair = qk[:, (2 + pair) * LANES:(3 + pair) * LANES]
            qmask = jnp.where(head_lanes[par], qk[:, pair * LANES:(pair + 1) * LANES], jnp.zeros((), bf16))
            vaug = jnp.concatenate([mvT_ref[i, hh * VD:(hh + 1) * VD, :], ones_aug], axis=0)

            m_mid = swap_halves(jnp.maximum(bl_r + m_in, wm_r))
            m_prev = jnp.where(in_chunk[0], m_in, m_mid)
            m_next = jnp.maximum(bl_r + m_prev, wm_r)
            m_intra = b_r + cm_r
            m_inter = b_r + m_prev
            m_t = jnp.maximum(m_inter, m_intra)
            e_intra = jnp.exp(m_intra - m_t)
            e_inter = jnp.exp(m_inter - m_t)
            decay = jnp.exp(bl_r + m_prev - m_next)
            inject = jnp.exp(wm_r - m_next)

            g_mat = jnp.broadcast_to(g_r, (LANES, LANES)).T
            w = jnp.exp(jnp.where(causal, g_mat - cm_r, NEG_INF))
            st = _nt(kpair, qmask) * (w * e_intra)
            y = _mm(vaug, st.astype(bf16))
            kw = kT * jnp.exp(bl_r + g_r - wm_r)
            cs = caug
            for p in range(LANES // L):
                qs = jnp.where(in_chunk[p], qT * e_inter, 0.0).astype(bf16)
                y = y + _mm(cs.astype(bf16), qs)
                u = _nt(vaug, jnp.where(in_chunk[p], kw, 0.0).astype(bf16))
                dec = decay if p == 0 else swap_halves(decay)
                inj = inject if p == 0 else swap_halves(inject)
                cs = dec[:, 0:QD] * cs + inj[:, 0:QD] * u
            m_out = jnp.where(in_chunk[1], m_next, swap_halves(m_next))
            new_state.append((cs, m_out))

            den = y[VD:VD + 1, :]
            hT = y[0:VD, :] * (1.0 / jnp.maximum(jnp.abs(den), jnp.exp(-m_t)))
            sl = slice(hh * VD, (hh + 1) * VD)
            hn = hT * lax.rsqrt(jnp.mean(hT * hT, axis=0, keepdims=True) + NORM_EPS) * norm_ref[sl, :]
            yT = jax.nn.sigmoid(moT_ref[i, sl, :].astype(f32)) * hn
            out_ref[pl.ds(t0, LANES), sl] = yT.T.astype(out_ref.dtype)
        return tuple(new_state)

    init = tuple((jnp.zeros((C_ROWS, QD), f32), jnp.zeros((1, LANES), f32)) for _ in range(H))
    lax.fori_loop(0, nt, tile_body, init, unroll=4)


def _mlstm(mqk, mvT, moT, smallT, conv_w, conv_b, gate_bias, mlstm_norm, B, S):
    nt = S // LANES
    gb = jnp.broadcast_to(gate_bias.astype(jnp.float32).reshape(2 * MLSTM_HEADS, 1), (2 * MLSTM_HEADS, LANES))
    norm_cols = jnp.broadcast_to(mlstm_norm.astype(jnp.float32).reshape(MLSTM_WIDTH, 1), (MLSTM_WIDTH, LANES))
    lane = np.arange(LANES)
    same_chunk = lane[:, None] // MLSTM_CHUNK == lane[None, :] // MLSTM_CHUNK
    tri = (same_chunk & (lane[:, None] <= lane[None, :])).astype(np.float32)
    last = (same_chunk & (lane[:, None] % MLSTM_CHUNK == MLSTM_CHUNK - 1)).astype(np.float32)
    seq = lambda w: pl.BlockSpec((S, w), lambda b: (b, 0))
    tiles = lambda rows: pl.BlockSpec((nt, rows, LANES), lambda b: (b, 0, 0))
    return pl.pallas_call(
        _mlstm_kernel,
        grid=(B,),
        in_specs=[seq(2 * MLSTM_QK_WIDTH), tiles(MLSTM_WIDTH), tiles(MLSTM_WIDTH), tiles(N_GATE_ROWS),
                  _full((2 * MLSTM_HEADS, LANES)), _full((LANES, LANES)), _full((LANES, LANES)),
                  _full((CONV_WIDTH, 2 * MLSTM_QK_WIDTH)), _full((1, 2 * MLSTM_QK_WIDTH)),
                  _full((MLSTM_WIDTH, LANES))],
        out_specs=seq(MLSTM_WIDTH),
        out_shape=jax.ShapeDtypeStruct((B * S, MLSTM_WIDTH), jnp.bfloat16),
        scratch_shapes=[pltpu.VMEM((S, 2 * MLSTM_QK_WIDTH), jnp.bfloat16),
                        pltpu.VMEM((5, nt * 2 * MLSTM_HEADS, LANES), jnp.float32)],
        compiler_params=_params(1),
        name="mlstm",
    )(mqk, mvT, moT, smallT, gb, jnp.asarray(tri), jnp.asarray(last), conv_w, conv_b.reshape(1, -1), norm_cols)


def _mem_kv_kernel(mem_ref, g_ref, w_ref, k_ref, v_ref):
    mn = _rms(mem_ref[...], g_ref[...]).astype(jnp.bfloat16)
    k_ref[...] = _mm(mn, w_ref[:, :D_MODEL]).astype(k_ref.dtype)
    v_ref[...] = _mm(mn, w_ref[:, D_MODEL:]).astype(v_ref.dtype)


def _mem_kv(mem, gain, w_xkv):
    B, M, _ = mem.shape
    spec = pl.BlockSpec((None, M, D_MODEL), lambda b: (b, 0, 0))
    return pl.pallas_call(
        _mem_kv_kernel,
        grid=(B,),
        in_specs=[spec, _full((1, D_MODEL)), _full((D_MODEL, 2 * D_MODEL))],
        out_specs=[spec, spec],
        out_shape=[jax.ShapeDtypeStruct((B, M, D_MODEL), jnp.bfloat16)] * 2,
        compiler_params=_params(1),
        name="mem_kv",
    )(mem, gain, w_xkv.astype(jnp.bfloat16))


TM_X = 512


def _mix_xattn_kernel(ynsa_ref, yml_ref, x_ref, wout_ref, gpost_ref, gpre_ref, wq_ref, k_ref, v_ref,
                      wo_ref, gpost2_ref, out_ref):
    bf16 = jnp.bfloat16
    y = _mm(ynsa_ref[...], wout_ref[:NSA_WIDTH, :]) + _mm(yml_ref[...], wout_ref[NSA_WIDTH:, :])
    x1 = x_ref[...] + _rms(y, gpost_ref[...])
    h2 = _rms(x1, gpre_ref[...]).astype(bf16)
    q = (_mm(h2, wq_ref[...]) * (XATTN_HEAD_DIM ** -0.5)).astype(bf16)
    outs = []
    for hh in range(XATTN_HEADS):
        sl = slice(hh * XATTN_HEAD_DIM, (hh + 1) * XATTN_HEAD_DIM)
        s = _nt(q[:, sl], k_ref[:, sl])
        p = jnp.exp(s - jnp.max(s, axis=1, keepdims=True))
        l = jnp.sum(p, axis=1, keepdims=True)
        outs.append((_mm(p.astype(bf16), v_ref[:, sl]) * (1.0 / l)).astype(bf16))
    o = jnp.concatenate(outs, axis=1)
    y2 = _mm(o, wo_ref[...])
    out_ref[...] = x1 + _rms(y2, gpost2_ref[...])


def _mix_xattn(ynsa, yml, x2d, w_out, g_post, g_pre, w_xq, kx, vx, w_xo, g_post2, B, S):
    nt = S // TM_X
    M = kx.shape[1]
    tok = lambda w: pl.BlockSpec((TM_X, w), lambda b, i: (b * nt + i, 0))
    mem_spec = pl.BlockSpec((None, M, D_MODEL), lambda b, i: (b, 0, 0))
    sq = _full((D_MODEL, D_MODEL))
    row = _full((1, D_MODEL))
    bf = lambda w: w.astype(jnp.bfloat16)
    return pl.pallas_call(
        _mix_xattn_kernel,
        grid=(B, nt),
        in_specs=[tok(NSA_WIDTH), tok(MLSTM_WIDTH), tok(D_MODEL), sq, row, row, sq, mem_spec, mem_spec,
                  sq, row],
        out_specs=tok(D_MODEL),
        out_shape=jax.ShapeDtypeStruct((B * S, D_MODEL), jnp.float32),
        compiler_params=_params(2),
        name="mix_xattn",
    )(ynsa, yml, x2d, bf(w_out), g_post, g_pre, bf(w_xq), kx, vx, bf(w_xo), g_post2)


TM_F = 512


def _ffn_kernel(x_ref, gpre_ref, wgu_ref, wd_ref, gpost_ref, out_ref, acc_ref):
    bf16 = jnp.bfloat16
    x = x_ref[...]
    h = _rms(x, gpre_ref[...]).astype(bf16)
    acc_ref[...] = jnp.zeros_like(acc_ref)
    for j in range(D_FF // F_TILE):
        cols = slice(j * F_TILE, (j + 1) * F_TILE)
        g = _mm(h, wgu_ref[:, cols])
        u = _mm(h, wgu_ref[:, D_FF + j * F_TILE:D_FF + (j + 1) * F_TILE])
        act = (g * jax.nn.sigmoid(g) * u).astype(bf16)
        acc_ref[...] += _mm(act, wd_ref[cols, :])
    out_ref[...] = x + _rms(acc_ref[...], gpost_ref[...])


def _ffn(x2d, g_pre, w_gate_up, w_down, g_post):
    T = x2d.shape[0]
    tok = pl.BlockSpec((TM_F, D_MODEL), lambda i: (i, 0))
    row = _full((1, D_MODEL))
    return pl.pallas_call(
        _ffn_kernel,
        grid=(T // TM_F,),
        in_specs=[tok, row, _full((D_MODEL, 2 * D_FF)), _full((D_FF, D_MODEL)), row],
        out_specs=tok,
        out_shape=jax.ShapeDtypeStruct((T, D_MODEL), jnp.float32),
        scratch_shapes=[pltpu.VMEM((TM_F, D_MODEL), jnp.float32)],
        compiler_params=_params(1),
        name="ffn",
    )(x2d, g_pre, w_gate_up.astype(jnp.bfloat16), w_down.astype(jnp.bfloat16), g_post)


def _layer(x, mem, rel_bias, mix_norm_pre, w_in, cmp_pos_k, cmp_pos_v, cmp_w1_k, cmp_w2_k, cmp_w1_v,
           cmp_w2_v, conv_w, conv_b, mlstm_gate_bias, mlstm_norm, w_out, mix_norm_post, xattn_norm_pre,
           mem_norm, w_xq, w_xkv, w_xo, xattn_norm_post, ffn_norm_pre, w_gate_up, w_down, ffn_norm_post):
    B, S, _ = x.shape
    row = lambda g: g.reshape(1, -1).astype(jnp.float32)
    x2d = x.reshape(B * S, D_MODEL)
    w_tok, w_feat = _in_proj_weights(w_in)
    (kc, vc, ksl, kwn, mqk, qT, vslT, vwnT, mvT, moT, smallT) = _in_proj(
        x2d, row(mix_norm_pre), w_tok, w_feat)
    kcmp, vcmpT = _compress(kc, vc, cmp_pos_k, cmp_pos_v, cmp_w1_k, cmp_w2_k, cmp_w1_v, cmp_w2_v, B, S)
    tables = _bias_tables(rel_bias.astype(jnp.float32))
    ynsa = _nsa(qT, kcmp, vcmpT, ksl, vslT, kwn, vwnT, smallT, tables, B, S)
    yml = _mlstm(mqk, mvT, moT, smallT, conv_w, conv_b, mlstm_gate_bias, mlstm_norm, B, S)
    kx, vx = _mem_kv(mem, row(mem_norm), w_xkv)
    x2 = _mix_xattn(ynsa, yml, x2d, w_out, row(mix_norm_post), row(xattn_norm_pre), w_xq, kx, vx, w_xo,
                    row(xattn_norm_post), B, S)
    x3 = _ffn(x2, row(ffn_norm_pre), w_gate_up, w_down, row(ffn_norm_post))
    return x3.reshape(B, S, D_MODEL)


def kernel(x, mem, rel_bias, mix_norm_pre, w_in, cmp_pos_k, cmp_pos_v, cmp_w1_k, cmp_w2_k, cmp_w1_v, cmp_w2_v,
           conv_w, conv_b, mlstm_gate_bias, mlstm_norm, w_out, mix_norm_post, xattn_norm_pre, mem_norm, w_xq,
           w_xkv, w_xo, xattn_norm_post, ffn_norm_pre, w_gate_up, w_down, ffn_norm_post):
    depth = w_in.shape[0]
    for l in range(depth):
        x = _layer(x, mem, rel_bias, mix_norm_pre[l], w_in[l], cmp_pos_k[l], cmp_pos_v[l], cmp_w1_k[l],
                   cmp_w2_k[l], cmp_w1_v[l], cmp_w2_v[l], conv_w[l], conv_b[l], mlstm_gate_bias[l],
                   mlstm_norm[l], w_out[l], mix_norm_post[l], xattn_norm_pre[l], mem_norm[l], w_xq[l],
                   w_xkv[l], w_xo[l], xattn_norm_post[l], ffn_norm_pre[l], w_gate_up[l], w_down[l],
                   ffn_norm_post[l])
    return x
```

```python
import functools
import math

import numpy as np
import jax
import jax.numpy as jnp
from jax import lax
from jax.experimental import pallas as pl
from jax.experimental.pallas import tpu as pltpu

D_MODEL = 1024
NSA_WIDTH = 512
NSA_HEAD_DIM = 64
NSA_HEADS = 8
NSA_KV_HEADS = 2
NSA_GROUP = 4
NSA_KV_WIDTH = 128
CMP_STRIDE = 16
CMP_BLOCK = 32
CMP_HIDDEN = 256
SEL_BLOCK = 64
N_SELECT = 16
WINDOW = 512
Q_BLOCK = 128
FORCED_SCORE = 1.0e4
MLSTM_WIDTH = 512
MLSTM_HEADS = 4
MLSTM_V_DIM = 128
MLSTM_QK_DIM = 64
MLSTM_QK_WIDTH = 256
MLSTM_CHUNK = 64
CONV_WIDTH = 4
REL_BUCKETS = 32
REL_MAX_DISTANCE = 128
XATTN_HEADS = 4
XATTN_HEAD_DIM = 256
D_FF = 2816
NORM_EPS = 1e-6
NEG_INF = -1.0e30
LOG2E = math.log2(math.e)

IN_SIZES = (NSA_WIDTH,) + (NSA_KV_WIDTH,) * 6 + (NSA_HEADS * 3, MLSTM_QK_WIDTH, MLSTM_QK_WIDTH,
                                                 MLSTM_WIDTH, MLSTM_HEADS, MLSTM_HEADS, MLSTM_WIDTH)
IN_OFFSETS = tuple(int(o) for o in np.cumsum((0,) + IN_SIZES)[:-1])

LANES = 128
SUBLANES = 8
VMEM_LIMIT_BYTES = 56 * 1024 * 1024

N_GATE_ROWS = 32
F_TILE = 256


def _rms(x, gain):
    return x * lax.rsqrt(jnp.mean(x * x, axis=-1, keepdims=True) + NORM_EPS) * gain


def _nt(a, b):
    return lax.dot_general(a, b, (((1,), (1,)), ((), ())), preferred_element_type=jnp.float32)


def _mm(a, b):
    return jnp.dot(a, b, preferred_element_type=jnp.float32)


def _params(n_axes, flags=None):
    return pltpu.CompilerParams(dimension_semantics=("arbitrary",) * n_axes,
                                vmem_limit_bytes=VMEM_LIMIT_BYTES, flags=flags)


def _full(shape):
    nd = len(shape)
    return pl.BlockSpec(shape, lambda *_: (0,) * nd)


TM_IN = 1024
IN_HALVES = 2
_TOK_GROUPS = (("kc", 128, jnp.float32), ("vc", 128, jnp.float32), ("ksl", 128, jnp.bfloat16),
               ("kwn", 128, jnp.bfloat16), ("mqk", 512, jnp.bfloat16))
_FEAT_GROUPS = (("qT", 512, jnp.bfloat16), ("vslT", 128, jnp.bfloat16), ("vwnT", 128, jnp.bfloat16),
                ("mvT", 512, jnp.bfloat16), ("moT", 512, jnp.bfloat16), ("smallT", N_GATE_ROWS, jnp.float32))


def _in_proj_kernel(x_ref, g_ref, wtok_ref, wfeat_ref, *out_refs):
    n_tok = len(_TOK_GROUPS)
    half = TM_IN // IN_HALVES

    def norm(i):
        return _rms(x_ref[i * half:(i + 1) * half, :], g_ref[...]).astype(jnp.bfloat16)

    def token_major(i, h):
        off = 0
        for (name, width, dt), o_ref in zip(_TOK_GROUPS, out_refs[:n_tok]):
            o_ref[i * half:(i + 1) * half, :] = _mm(h, wtok_ref[:, off:off + width]).astype(dt)
            off += width

    def feature_major(i, h):
        off = 0
        for (name, rows, dt), o_ref in zip(_FEAT_GROUPS, out_refs[n_tok:]):
            r = _nt(wfeat_ref[off:off + rows, :], h)
            if name == "qT":
                r = r * (NSA_HEAD_DIM ** -0.5 * LOG2E)
            for j in range(half // LANES):
                o_ref[i * (half // LANES) + j] = r[:, j * LANES:(j + 1) * LANES].astype(dt)
            off += rows

    h = norm(0)
    for i in range(IN_HALVES):
        token_major(i, h)
        h_next = norm(i + 1) if i + 1 < IN_HALVES else None
        feature_major(i, h)
        h = h_next


def _in_proj(x2d, gain, w_tok, w_feat):
    T = x2d.shape[0]
    n_tok_cols = w_tok.shape[1]
    n_feat_rows = w_feat.shape[0]
    out_shape, out_specs = [], []
    for name, width, dt in _TOK_GROUPS:
        out_shape.append(jax.ShapeDtypeStruct((T, width), dt))
        out_specs.append(pl.BlockSpec((TM_IN, width), lambda i: (i, 0)))
    for name, rows, dt in _FEAT_GROUPS:
        out_shape.append(jax.ShapeDtypeStruct((T // LANES, rows, LANES), dt))
        out_specs.append(pl.BlockSpec((TM_IN // LANES, rows, LANES), lambda i: (i, 0, 0)))
    return pl.pallas_call(
        _in_proj_kernel,
        grid=(T // TM_IN,),
        in_specs=[pl.BlockSpec((TM_IN, D_MODEL), lambda i: (i, 0)),
                  _full((1, D_MODEL)),
                  _full((D_MODEL, n_tok_cols)),
                  _full((n_feat_rows, D_MODEL))],
        out_specs=out_specs,
        out_shape=out_shape,
        compiler_params=_params(1),
        name="in_proj",
    )(x2d, gain, w_tok, w_feat)


def _in_proj_weights(w_in):
    (nq, kc, vc, ksl, vsl, kwn, vwn, gt, mq, mk, mv, mi, mf, mo) = [
        w_in[:, o:o + s] for o, s in zip(IN_OFFSETS, IN_SIZES)]
    gt_r = gt.reshape(D_MODEL, NSA_KV_HEADS, NSA_GROUP, 3).transpose(0, 3, 1, 2).reshape(D_MODEL, 24)
    small = jnp.concatenate([gt_r, mi, mf], axis=1)
    w_tok = jnp.concatenate([kc, vc, ksl, kwn, mq, mk], axis=1)
    w_feat = jnp.concatenate([nq, vsl, vwn, mv, mo, small], axis=1).T
    return w_tok.astype(jnp.bfloat16), w_feat.astype(jnp.bfloat16)


N_CHUNK_COLS = CMP_STRIDE * NSA_KV_WIDTH
N_HID2 = NSA_KV_HEADS * CMP_HIDDEN


def _compress_one(c, pos_ref, w1_ref, n_chunks):
    lo = _mm((c + pos_ref[0:1, :]).astype(jnp.bfloat16), w1_ref[0])
    hi = _mm((c + pos_ref[1:2, :]).astype(jnp.bfloat16), w1_ref[1])
    pre = lo + pltpu.roll(hi, n_chunks - 1, 0)
    return (pre * jax.nn.sigmoid(pre)).astype(jnp.bfloat16)


def _chunk_rows(ref, n_chunks):
    return jnp.concatenate([ref[pl.ds(t, n_chunks, stride=CMP_STRIDE), :] for t in range(CMP_STRIDE)], axis=1)


def _compress_kernel(kc_ref, vc_ref, posk_ref, posv_ref, w1k_ref, w1v_ref, w2k_ref, w2vT_ref,
                     kcmp_ref, vcmpT_ref):
    n_chunks = kc_ref.shape[0] // CMP_STRIDE
    hid_k = _compress_one(_chunk_rows(kc_ref, n_chunks), posk_ref, w1k_ref, n_chunks)
    kcmp = _mm(hid_k, w2k_ref[...])
    row = lax.broadcasted_iota(jnp.int32, kcmp.shape, 0)
    kcmp_ref[...] = jnp.where(row < n_chunks - 1, kcmp, 0.0).astype(kcmp_ref.dtype)
    hid_v = _compress_one(_chunk_rows(vc_ref, n_chunks), posv_ref, w1v_ref, n_chunks)
    vcmpT = _nt(w2vT_ref[...], hid_v)
    col = lax.broadcasted_iota(jnp.int32, vcmpT.shape, 1)
    vcmpT_ref[...] = jnp.where(col < n_chunks - 1, vcmpT, 0.0).astype(vcmpT_ref.dtype)


def _compress_weights(pos, w1, w2):
    eye = jnp.eye(NSA_KV_HEADS, dtype=w1.dtype)
    w1r = w1.reshape(2, CMP_STRIDE, NSA_HEAD_DIM, CMP_HIDDEN)
    w1e = jnp.einsum('atdj,hg->athdgj', w1r, eye).reshape(2, N_CHUNK_COLS, N_HID2)
    pos_e = jnp.broadcast_to(pos.reshape(2, CMP_STRIDE, 1, NSA_HEAD_DIM),
                             (2, CMP_STRIDE, NSA_KV_HEADS, NSA_HEAD_DIM)).reshape(2, N_CHUNK_COLS)
    w2e = jnp.einsum('jd,hg->hjgd', w2, eye).reshape(N_HID2, NSA_KV_WIDTH)
    return pos_e, w1e.astype(jnp.bfloat16), w2e.astype(jnp.bfloat16)


def _compress(kc, vc, cmp_pos_k, cmp_pos_v, cmp_w1_k, cmp_w2_k, cmp_w1_v, cmp_w2_v, B, S):
    n_chunks = S // CMP_STRIDE
    posk, w1k, w2k = _compress_weights(cmp_pos_k, cmp_w1_k, cmp_w2_k)
    posv, w1v, w2v = _compress_weights(cmp_pos_v, cmp_w1_v, cmp_w2_v)
    chunk_spec = pl.BlockSpec((S, NSA_KV_WIDTH), lambda b: (b, 0))
    return pl.pallas_call(
        _compress_kernel,
        grid=(B,),
        in_specs=[chunk_spec, chunk_spec,
                  _full((2, N_CHUNK_COLS)), _full((2, N_CHUNK_COLS)),
                  _full((2, N_CHUNK_COLS, N_HID2)), _full((2, N_CHUNK_COLS, N_HID2)),
                  _full((N_HID2, NSA_KV_WIDTH)), _full((NSA_KV_WIDTH, N_HID2))],
        out_specs=[pl.BlockSpec((None, n_chunks, NSA_KV_WIDTH), lambda b: (b, 0, 0)),
                   pl.BlockSpec((None, NSA_KV_WIDTH, n_chunks), lambda b: (b, 0, 0))],
        out_shape=[jax.ShapeDtypeStruct((B, n_chunks, NSA_KV_WIDTH), jnp.bfloat16),
                   jax.ShapeDtypeStruct((B, NSA_KV_WIDTH, n_chunks), jnp.bfloat16)],
        compiler_params=_params(1),
        name="compress",
    )(kc, vc, posk, posv, w1k, w1v, w2k, w2v.T)


GQ = NSA_GROUP * Q_BLOCK
TINY = 1e-30
CMP_TAB_ROWS = 512
CMP_TAB_ZERO = 248
CMP_TAB_LOOKUP = (232, 256)
SEL_STEP_SHIFT = 2
KEY_SUPER = Q_BLOCK << SEL_STEP_SHIFT
SEL_TAB_ZERO = KEY_SUPER + Q_BLOCK
SEL_TAB_ROWS = SEL_TAB_ZERO + KEY_SUPER
N_WIN_TILES = WINDOW // Q_BLOCK + 1
SEL_SUB_CHUNKS = 2
V_ROWS = NSA_HEAD_DIM + 16


def _bucket_np(dist):
    n = np.maximum(dist, 0)
    max_exact = REL_BUCKETS // 2
    nf = np.maximum(n, 1).astype(np.float64)
    large = max_exact + (np.log(nf / max_exact) / math.log(REL_MAX_DISTANCE / max_exact)
                         * (REL_BUCKETS - max_exact)).astype(np.int64)
    large = np.minimum(large, REL_BUCKETS - 1)
    return np.where(n < max_exact, n, large).astype(np.int32)


def _bias_index_tables():
    m = np.arange(Q_BLOCK)[:, None]
    r = np.arange(Q_BLOCK)[None, :]
    diag = np.where(r - m >= 0, _bucket_np(r - m), -1).astype(np.int32)
    off = _bucket_np(Q_BLOCK + r - m)
    jp = np.arange(*CMP_TAB_LOOKUP)[:, None] - CMP_TAB_ZERO
    d_c = r - CMP_STRIDE * jp - (CMP_BLOCK - 1)
    cmp_idx = np.where(d_c >= 0, _bucket_np(d_c), -1).astype(np.int32)
    return diag, off, cmp_idx


def _bias_tables_kernel(rb_ref, diag_idx_ref, off_idx_ref, cmp_idx_ref, sel_ref, win_ref, cmp_ref):
    f32 = jnp.float32

    def lookup(idx, head):
        far = rb_ref[head, REL_BUCKETS - 1]
        acc = jnp.full(idx.shape, NEG_INF, f32)
        for k in range(REL_BUCKETS):
            acc = jnp.where(idx == k, (rb_ref[head, k] - far) * LOG2E, acc)
        return acc

    m_io = lax.broadcasted_iota(jnp.int32, (Q_BLOCK, Q_BLOCK), 0)
    r_io = lax.broadcasted_iota(jnp.int32, (Q_BLOCK, Q_BLOCK), 1)
    neg_tile = jnp.full((Q_BLOCK, Q_BLOCK), NEG_INF, f32)
    lo, hi = CMP_TAB_LOOKUP
    for h in range(NSA_KV_HEADS):
        for g in range(NSA_GROUP):
            head = h * NSA_GROUP + g
            sl = slice(g * Q_BLOCK, (g + 1) * Q_BLOCK)
            far = 0.0
            far_tile = jnp.full((Q_BLOCK, Q_BLOCK), far, f32)
            diag_v = lookup(diag_idx_ref[...], head)
            off_v = lookup(off_idx_ref[...], head)
            n_far = (SEL_TAB_ZERO - Q_BLOCK) // Q_BLOCK
            for t in range(SEL_TAB_ROWS // Q_BLOCK):
                rows = slice(t * Q_BLOCK, (t + 1) * Q_BLOCK)
                tile = far_tile if t < n_far else off_v if t == n_far else diag_v if t == n_far + 1 else neg_tile
                sel_ref[h, rows, sl] = tile
            win_ref[h, 0, :, sl] = diag_v
            win_ref[h, 1, :, sl] = off_v
            for back in range(2, N_WIN_TILES - 1):
                win_ref[h, back, :, sl] = far_tile
            win_ref[h, N_WIN_TILES - 1, :, sl] = jnp.where(r_io < m_io, far, NEG_INF)
            win_ref[h, N_WIN_TILES, :, sl] = neg_tile
            cmp_ref[h, 0:lo, sl] = jnp.full((lo, Q_BLOCK), far, f32)
            cmp_ref[h, lo:hi, sl] = lookup(cmp_idx_ref[...], head)
            cmp_ref[h, hi:CMP_TAB_ROWS, sl] = jnp.full((CMP_TAB_ROWS - hi, Q_BLOCK), NEG_INF, f32)


def _bias_tables(rel_bias):
    diag_idx, off_idx, cmp_idx = _bias_index_tables()
    shapes = [(NSA_KV_HEADS, SEL_TAB_ROWS, GQ), (NSA_KV_HEADS, N_WIN_TILES + 1, Q_BLOCK, GQ),
              (NSA_KV_HEADS, CMP_TAB_ROWS, GQ)]
    return pl.pallas_call(
        _bias_tables_kernel,
        in_specs=[pl.BlockSpec(memory_space=pltpu.SMEM),
                  _full(diag_idx.shape), _full(off_idx.shape), _full(cmp_idx.shape)],
        out_specs=[_full(s) for s in shapes],
        out_shape=[jax.ShapeDtypeStruct(s, jnp.float32) for s in shapes],
        grid=(1,),
        compiler_params=_params(1),
        name="bias_tables",
    )(rel_bias, jnp.asarray(diag_idx), jnp.asarray(off_idx), jnp.asarray(cmp_idx))


def _overlap_np(n_cmp_rows, n_sel):
    cmp_start = np.arange(n_cmp_rows) * CMP_STRIDE
    cmp_end = cmp_start + CMP_BLOCK - 1
    sel_start = np.arange(n_sel) * SEL_BLOCK
    ov = ((cmp_start[None, :] <= sel_start[:, None] + SEL_BLOCK - 1)
          & (cmp_end[None, :] >= sel_start[:, None])).astype(np.float32)
    ov[:, n_cmp_rows - 1] = 0.0
    return ov


def _tile4(a):
    return jnp.concatenate([a] * NSA_GROUP, axis=1)


def _select_blocks(score, score_ref, n_top, hooks=()):
    n_sel = score.shape[0]
    score_ref[...] = score
    n_grp = n_sel // SUBLANES
    grp = [score[SUBLANES * v:SUBLANES * (v + 1), :] for v in range(n_grp)]
    cnt = [jnp.zeros((SUBLANES, Q_BLOCK), jnp.int32) for _ in range(n_grp)]
    sub_io = lax.broadcasted_iota(jnp.int32, (SUBLANES, Q_BLOCK), 0)
    hook_at = {(i * n_sel) // len(hooks): hk for i, hk in enumerate(hooks)} if hooks else {}
    for jp in range(n_sel):
        if jp in hook_at:
            hook_at[jp]()
        row = score_ref[jp:jp + 1, :]
        for v in range(n_grp):
            if SUBLANES * v > jp:
                inc = (row >= grp[v]).astype(jnp.int32)
            elif SUBLANES * (v + 1) - 1 < jp:
                inc = (row > grp[v]).astype(jnp.int32)
            else:
                tie = (sub_io > jp - SUBLANES * v).astype(jnp.int32)
                inc = jnp.where(row > grp[v], 1, jnp.where(row == grp[v], tie, 0))
            cnt[v] = cnt[v] + inc
    return [jnp.where(cnt[v] < n_top, 0.0, NEG_INF) for v in range(n_grp)]


def _nsa_kernel(q_ref, kcmp_ref, vcmpT_ref, ksl_ref, vslT_ref, kwn_ref, vwnT_ref, gate_ref,
                seltab_ref, wintab_ref, cmptab_ref, ovl_ref, blkind_ref, out_ref, score_ref, selb_ref, sbuf_ref,
                sbuf2_ref, swin_ref):
    c = pl.program_id(1)
    n_cmp = kcmp_ref.shape[0]
    n_sel = ovl_ref.shape[0]
    n_top = min(N_SELECT, n_sel)
    f32 = jnp.float32
    bf16 = jnp.bfloat16
    DH = NSA_HEAD_DIM
    heads = range(NSA_KV_HEADS)

    q = q_ref[...]
    zq = jnp.zeros((DH, GQ), bf16)
    qcat, qpad = [], []
    for h in heads:
        qcat.append(jnp.concatenate([q[(h * NSA_GROUP + g) * DH:(h * NSA_GROUP + g + 1) * DH, :]
                                     for g in range(NSA_GROUP)], axis=1))
        qpad.append(jnp.concatenate([qcat[h], zq] if h == 0 else [zq, qcat[h]], axis=0))

    backs = list(range(N_WIN_TILES))
    kts = [jnp.maximum(c - back, 0) for back in backs]
    slots = [jnp.where(c >= back, back, N_WIN_TILES) for back in backs]
    m_win = [jnp.full((1, GQ), NEG_INF, f32) for _ in heads]

    def win_score(back, h):
        key0 = pl.multiple_of(kts[back] * Q_BLOCK, Q_BLOCK)
        s = _mm(kwn_ref[pl.ds(key0, Q_BLOCK), :], qpad[h]) + wintab_ref[h, slots[back]]
        swin_ref[h, back * Q_BLOCK:(back + 1) * Q_BLOCK, :] = s
        m_win[h] = jnp.maximum(m_win[h], jnp.max(s, axis=0, keepdims=True))

    cmp_off = pl.multiple_of(CMP_TAB_ZERO - (Q_BLOCK // CMP_STRIDE) * c, SUBLANES)
    kcmp = kcmp_ref[...]
    j_io = lax.broadcasted_iota(jnp.int32, (n_sel, Q_BLOCK), 0)
    r_io = lax.broadcasted_iota(jnp.int32, (n_sel, Q_BLOCK), 1)
    cur = (Q_BLOCK // SEL_BLOCK) * c + (r_io >= SEL_BLOCK).astype(jnp.int32)
    forced = (j_io == 0) | (j_io == cur) | (j_io == cur - 1)
    visible = j_io <= cur
    o_c = []
    for h in heads:
        tab = cmptab_ref[h, pl.ds(cmp_off, n_cmp), :]
        s = _mm(kcmp, qpad[h]) + tab
        m = jnp.max(s, axis=0, keepdims=True)
        p = jnp.where(tab > 0.5 * NEG_INF, jnp.exp2(s - m), 0.0)
        l = jnp.sum(p, axis=0, keepdims=True)
        pn = p * (1.0 / jnp.maximum(l, TINY))
        o_c.append(_mm(vcmpT_ref[h * DH:(h + 1) * DH, :], pn.astype(bf16)))
        psum = pn[:, 0:Q_BLOCK]
        for g in range(1, NSA_GROUP):
            psum = psum + pn[:, g * Q_BLOCK:(g + 1) * Q_BLOCK]
        imp = jnp.dot(ovl_ref[...], psum, precision=lax.Precision.HIGHEST,
                      preferred_element_type=f32)
        score = jnp.where(forced, FORCED_SCORE, jnp.where(visible, imp, -1.0))
        rows = _select_blocks(score, score_ref.at[h], n_top,
                              hooks=[functools.partial(win_score, back, h) for back in backs])
        for v, blk in enumerate(rows):
            selb_ref[h, SUBLANES * v:SUBLANES * (v + 1), :] = blk

    def ones_rows(n_keys):
        return jnp.ones((V_ROWS - DH, n_keys), bf16)

    acc_w = [jnp.zeros((V_ROWS, GQ), f32) for _ in heads]

    def win_value(back):
        for h in heads:
            p = jnp.exp2(swin_ref[h, back * Q_BLOCK:(back + 1) * Q_BLOCK, :] - m_win[h]).astype(bf16)
            vT = jnp.concatenate([vwnT_ref[kts[back], h * DH:(h + 1) * DH, :], ones_rows(Q_BLOCK)], axis=0)
            acc_w[h] = acc_w[h] + _mm(vT, p)

    blocks_per_step = KEY_SUPER // SEL_BLOCK
    tiles_per_step = KEY_SUPER // Q_BLOCK

    sub = KEY_SUPER // SEL_SUB_CHUNKS
    blocks_per_sub = sub // SEL_BLOCK
    tiles_per_sub = sub // Q_BLOCK

    own_lanes = [(lax.broadcasted_iota(jnp.int32, (sub, NSA_KV_WIDTH), 1) >= DH) == bool(h) for h in heads]

    def q_with_mask_rows(j, h):
        blk0 = pl.multiple_of(j * blocks_per_step, blocks_per_step)
        rows = jnp.concatenate([_tile4(selb_ref[h, pl.ds(blk0, blocks_per_step), :]),
                                jnp.zeros((DH - blocks_per_step, GQ), f32)], axis=0).astype(bf16)
        return jnp.concatenate([qcat[h], rows] if h == 0 else [rows, qcat[h]], axis=0)

    def score_chunk(j, ci, h, q_aug, buf_ref):
        key0 = pl.multiple_of(j * KEY_SUPER + ci * sub, sub)
        tab_off = pl.multiple_of(jnp.maximum(j * KEY_SUPER - c * Q_BLOCK + SEL_TAB_ZERO, 0) + ci * sub, Q_BLOCK)
        k_aug = jnp.where(own_lanes[h], ksl_ref[pl.ds(key0, sub), :], blkind_ref[h, ci * sub:(ci + 1) * sub, :])
        s = _mm(k_aug, q_aug) + seltab_ref[h, pl.ds(tab_off, sub), :]
        buf_ref[h, ci * sub:(ci + 1) * sub, :] = s
        return jnp.max(s, axis=0, keepdims=True)

    def value_chunk(j, ci, h, m_h, buf_ref):
        p = jnp.exp2(buf_ref[h, ci * sub:(ci + 1) * sub, :] - m_h).astype(bf16)
        vT = jnp.concatenate([vslT_ref[j * tiles_per_step + ci * tiles_per_sub + i, h * DH:(h + 1) * DH, :]
                              for i in range(tiles_per_sub)], axis=1)
        return _mm(jnp.concatenate([vT, ones_rows(sub)], axis=0), p)

    def values(j, m_old, m_cur, acc, src_ref, before_chunk=None):
        acc = [jnp.exp2(m_old[h] - m_cur[h]) * acc[h] for h in heads]
        for ci in range(SEL_SUB_CHUNKS):
            if before_chunk is not None:
                before_chunk(ci)
            for h in heads:
                acc[h] = acc[h] + value_chunk(j, ci, h, m_cur[h], src_ref)
        return tuple(acc)

    def sel_step(j, carry, src_ref, dst_ref):
        m_old, m_cur, acc = carry
        m_run = list(m_cur)
        q_aug = [q_with_mask_rows(j + 1, h) for h in heads]

        def scores(ci):
            for h in heads:
                m_run[h] = jnp.maximum(m_run[h], score_chunk(j + 1, ci, h, q_aug[h], dst_ref))

        acc = values(j, m_old, m_cur, acc, src_ref, before_chunk=scores)
        return m_cur, tuple(m_run), acc

    def sel_pair(i, carry):
        carry = sel_step(2 * i, carry, sbuf_ref, sbuf2_ref)
        return sel_step(2 * i + 1, carry, sbuf2_ref, sbuf_ref)

    n_steps = lax.shift_right_logical(c, SEL_STEP_SHIFT) + 1
    m_init = tuple(jnp.full((1, GQ), NEG_INF, f32) for _ in heads)
    acc_init = tuple(jnp.zeros((V_ROWS, GQ), f32) for _ in heads)
    m_first = list(m_init)
    win_order = list(backs)
    q_aug0 = [q_with_mask_rows(0, h) for h in heads]
    for ci in range(SEL_SUB_CHUNKS):
        for _ in range(-(-N_WIN_TILES // SEL_SUB_CHUNKS)):
            if win_order:
                win_value(win_order.pop(0))
        for h in heads:
            m_first[h] = jnp.maximum(m_first[h], score_chunk(0, ci, h, q_aug0[h], sbuf_ref))
    while win_order:
        win_value(win_order.pop(0))
    o_w = [a[0:DH, :] * (1.0 / a[DH:DH + 1, :]) for a in acc_w]
    n_piped = n_steps - 1
    odd = n_piped & 1
    carry = lax.fori_loop(0, lax.shift_right_logical(n_piped, 1), sel_pair, (m_init, tuple(m_first), acc_init))
    carry = lax.fori_loop(0, odd, lambda _, cr: sel_step(n_piped - 1, cr, sbuf_ref, sbuf2_ref), carry)
    m_old, m_cur, acc = carry
    acc = lax.fori_loop(0, 1 - odd, lambda _, a: values(n_piped, m_old, m_cur, a, sbuf_ref), acc)
    acc = lax.fori_loop(0, odd, lambda _, a: values(n_piped, m_old, m_cur, a, sbuf2_ref), acc)
    o_s = [a[0:DH, :] * (1.0 / jnp.maximum(a[DH:DH + 1, :], TINY)) for a in acc]

    for h in heads:
        ys = []
        for g in range(NSA_GROUP):
            sl = slice(g * Q_BLOCK, (g + 1) * Q_BLOCK)
            row0 = h * NSA_GROUP + g
            gates = [jax.nn.sigmoid(gate_ref[kind * NSA_HEADS + row0:kind * NSA_HEADS + row0 + 1, :])
                     for kind in range(3)]
            ys.append(gates[0] * o_c[h][:, sl] + gates[1] * o_s[h][:, sl] + gates[2] * o_w[h][:, sl])
        yT = jnp.concatenate(ys, axis=0)
        for half in range(2):
            col = (2 * h + half) * LANES
            out_ref[:, col:col + LANES] = yT[half * LANES:(half + 1) * LANES, :].T.astype(out_ref.dtype)


def _nsa(qT, kcmp, vcmpT, ksl, vslT, kwn, vwnT, smallT, tables, B, S):
    assert S % KEY_SUPER == 0
    nq = S // Q_BLOCK
    n_cmp = S // CMP_STRIDE
    n_sel = S // SEL_BLOCK
    seltab, wintab, cmptab = tables
    ovl = jnp.asarray(_overlap_np(n_cmp, n_sel))
    key_blk = np.arange(KEY_SUPER)[:, None] // SEL_BLOCK
    lane = np.arange(NSA_KV_WIDTH)[None, :]
    blkind = jnp.asarray(np.stack([lane - NSA_HEAD_DIM == key_blk, lane == key_blk]), jnp.bfloat16)
    ksl3 = ksl.reshape(B, S, NSA_KV_WIDTH)
    kwn3 = kwn.reshape(B, S, NSA_KV_WIDTH)
    vslT4 = vslT.reshape(B, nq, NSA_KV_WIDTH, Q_BLOCK)
    vwnT4 = vwnT.reshape(B, nq, NSA_KV_WIDTH, Q_BLOCK)
    k_spec = pl.BlockSpec((None, S, NSA_KV_WIDTH), lambda b, c: (b, 0, 0))
    vT_spec = pl.BlockSpec((None, nq, NSA_KV_WIDTH, Q_BLOCK), lambda b, c: (b, 0, 0, 0))
    const = lambda a: pl.BlockSpec(a.shape, lambda b, c: (0,) * a.ndim)
    return pl.pallas_call(
        _nsa_kernel,
        grid=(B, nq),
        in_specs=[pl.BlockSpec((None, NSA_WIDTH, Q_BLOCK), lambda b, c: (b * nq + c, 0, 0)),
                  pl.BlockSpec((None, n_cmp, NSA_KV_WIDTH), lambda b, c: (b, 0, 0)),
                  pl.BlockSpec((None, NSA_KV_WIDTH, n_cmp), lambda b, c: (b, 0, 0)),
                  k_spec, vT_spec, k_spec, vT_spec,
                  pl.BlockSpec((None, N_GATE_ROWS, Q_BLOCK), lambda b, c: (b * nq + c, 0, 0)),
                  const(seltab), const(wintab), const(cmptab), const(ovl), const(blkind)],
        out_specs=pl.BlockSpec((Q_BLOCK, NSA_WIDTH), lambda b, c: (b * nq + c, 0)),
        out_shape=jax.ShapeDtypeStruct((B * S, NSA_WIDTH), jnp.bfloat16),
        scratch_shapes=[pltpu.VMEM((NSA_KV_HEADS, n_sel, Q_BLOCK), jnp.float32),
                        pltpu.VMEM((NSA_KV_HEADS, n_sel, Q_BLOCK), jnp.float32),
                        pltpu.VMEM((NSA_KV_HEADS, KEY_SUPER, GQ), jnp.float32),
                        pltpu.VMEM((NSA_KV_HEADS, KEY_SUPER, GQ), jnp.float32),
                        pltpu.VMEM((NSA_KV_HEADS, N_WIN_TILES * Q_BLOCK, GQ), jnp.float32)],
        compiler_params=_params(2),
        name="nsa",
    )(qT, kcmp, vcmpT, ksl3, vslT4, kwn3, vwnT4, smallT, seltab, wintab, cmptab, ovl, blkind)


CONV_TILE = 256
I_ROW = 24
C_ROWS = MLSTM_V_DIM + 16


def _log_sigmoid(x):
    return jnp.minimum(x, 0.0) - jnp.log(1.0 + jnp.exp(-jnp.abs(x)))


def _mlstm_kernel(mqk_ref, mvT_ref, moT_ref, gates_ref, gb_ref, tri_ref, last_ref, convw_ref, convb_ref,
                  norm_ref, out_ref, qk_ref, rows_ref):
    S = mqk_ref.shape[0]
    nt = S // LANES
    L = MLSTM_CHUNK
    f32 = jnp.float32
    bf16 = jnp.bfloat16
    kscale_row = jnp.where(lax.broadcasted_iota(jnp.int32, (1, 2 * MLSTM_QK_WIDTH), 1) < MLSTM_QK_WIDTH,
                           1.0, MLSTM_QK_DIM ** -0.5)

    def conv_body(i, _):
        t0 = pl.multiple_of(i * CONV_TILE, CONV_TILE)
        cur = mqk_ref[pl.ds(t0, CONV_TILE), :].astype(f32)
        prev_start = pl.multiple_of(jnp.maximum(t0 - 16, 0), 16)
        prev = mqk_ref[pl.ds(prev_start, 16), :].astype(f32)[8:16, :] * jnp.where(i > 0, 1.0, 0.0)
        ext = jnp.concatenate([prev, cur], axis=0)
        y = convb_ref[...]
        for j in range(CONV_WIDTH):
            lo = SUBLANES - (CONV_WIDTH - 1) + j
            y = y + convw_ref[j:j + 1, :] * ext[lo:lo + CONV_TILE, :]
        y = y * jax.nn.sigmoid(y) * kscale_row
        qk_ref[pl.ds(t0, CONV_TILE), :] = y.astype(bf16)
        return 0

    lax.fori_loop(0, S // CONV_TILE, conv_body, 0)

    H = MLSTM_HEADS
    G8 = 2 * H
    n_rows = nt * G8
    a3 = gates_ref[:, I_ROW:I_ROW + G8, :] + gb_ref[...][None]
    is_f = lax.broadcasted_iota(jnp.int32, a3.shape, 1) >= H
    x = jnp.where(is_f, _log_sigmoid(a3), a3).reshape(n_rows, LANES)
    bcum = jnp.dot(x, tri_ref[...], precision=lax.Precision.HIGHEST, preferred_element_type=f32)
    b_rows = pltpu.roll(bcum, n_rows - H, 0)
    g_rows = x - b_rows
    pos = lax.broadcasted_iota(jnp.int32, (n_rows, LANES), 1) & (L - 1)

    def chunk_cummax(a):
        shift = 1
        while shift < L:
            a = jnp.where(pos >= shift, jnp.maximum(a, pltpu.roll(a, shift, 1)), a)
            shift *= 2
        return a

    def chunk_last(a):
        return jnp.dot(a, last_ref[...], precision=lax.Precision.HIGHEST, preferred_element_type=f32)

    bl_rows = chunk_last(b_rows)
    rows_ref[0] = g_rows
    rows_ref[1] = chunk_cummax(g_rows)
    rows_ref[2] = b_rows
    rows_ref[3] = bl_rows
    rows_ref[4] = chunk_last(chunk_cummax(bl_rows + g_rows))

    s_io = lax.broadcasted_iota(jnp.int32, (LANES, LANES), 0)
    t_io = lax.broadcasted_iota(jnp.int32, (LANES, LANES), 1)
    causal = (s_io <= t_io) & ((s_io >= L) == (t_io >= L))
    lane = lax.broadcasted_iota(jnp.int32, (1, LANES), 1)
    in_chunk = [lane < L, lane >= L]
    QD, VD = MLSTM_QK_DIM, MLSTM_V_DIM
    ones_aug = jnp.ones((C_ROWS - VD, LANES), bf16)
    head_lanes = [(lax.broadcasted_iota(jnp.int32, (LANES, LANES), 1) >= QD) == bool(par) for par in range(2)]

    def swap_halves(row):
        return pltpu.roll(row, L, 1)

    def tile_body(i, state):
        t0 = pl.multiple_of(i * LANES, LANES)
        r0 = pl.multiple_of(i * G8, G8)
        qk = qk_ref[pl.ds(t0, LANES), :]
        qk32 = qk.astype(f32)
        pairsT = [qk32[:, j * LANES:(j + 1) * LANES].T for j in range(4)]
        g8, cm8, b8, bl8, wm8 = [rows_ref[j, pl.ds(r0, G8), :] for j in range(5)]
        new_state = []
        for hh in range(H):
            pair, par = hh // 2, hh % 2
            g_r, cm_r, b_r, bl_r, wm_r = [a[hh:hh + 1, :] for a in (g8, cm8, b8, bl8, wm8)]
            caug, m_in = state[hh]
            qT = pairsT[pair][par * QD:(par + 1) * QD, :]
            kT = pairsT[2 + pair][par * QD:(par + 1) * QD, :]
            kpair = qk[:, (2 + pair) * LANES:(3 + pair) * LANES]
            qmask = jnp.where(head_lanes[par], qk[:, pair * LANES:(pair + 1) * LANES], jnp.zeros((), bf16))
            vaug = jnp.concatenate([mvT_ref[i, hh * VD:(hh + 1) * VD, :], ones_aug], axis=0)

            m_mid = swap_halves(jnp.maximum(bl_r + m_in, wm_r))
            m_prev = jnp.where(in_chunk[0], m_in, m_mid)
            m_next = jnp.maximum(bl_r + m_prev, wm_r)
            m_intra = b_r + cm_r
            m_inter = b_r + m_prev
            m_t = jnp.maximum(m_inter, m_intra)
            e_intra = jnp.exp(m_intra - m_t)
            e_inter = jnp.exp(m_inter - m_t)
            decay = jnp.exp(bl_r + m_prev - m_next)
            inject = jnp.exp(wm_r - m_next)

            g_mat = jnp.broadcast_to(g_r, (LANES, LANES)).T
            w = jnp.exp(jnp.where(causal, g_mat - cm_r, NEG_INF))
            st = _nt(kpair, qmask) * (w * e_intra)
            y = _mm(vaug, st.astype(bf16))
            kw = kT * jnp.exp(bl_r + g_r - wm_r)
            cs = caug
            for p in range(LANES // L):
                qs = jnp.where(in_chunk[p], qT * e_inter, 0.0).astype(bf16)
                y = y + _mm(cs.astype(bf16), qs)
                u = _nt(vaug, jnp.where(in_chunk[p], kw, 0.0).astype(bf16))
                dec = decay if p == 0 else swap_halves(decay)
                inj = inject if p == 0 else swap_halves(inject)
                cs = dec[:, 0:QD] * cs + inj[:, 0:QD] * u
            m_out = jnp.where(in_chunk[1], m_next, swap_halves(m_next))
            new_state.append((cs, m_out))

            den = y[VD:VD + 1, :]
            hT = y[0:VD, :] * (1.0 / jnp.maximum(jnp.abs(den), jnp.exp(-m_t)))
            sl = slice(hh * VD, (hh + 1) * VD)
            hn = hT * lax.rsqrt(jnp.mean(hT * hT, axis=0, keepdims=True) + NORM_EPS) * norm_ref[sl, :]
            yT = jax.nn.sigmoid(moT_ref[i, sl, :].astype(f32)) * hn
            out_ref[pl.ds(t0, LANES), sl] = yT.T.astype(out_ref.dtype)
        return tuple(new_state)

    init = tuple((jnp.zeros((C_ROWS, QD), f32), jnp.zeros((1, LANES), f32)) for _ in range(H))
    lax.fori_loop(0, nt, tile_body, init, unroll=4)


def _mlstm(mqk, mvT, moT, smallT, conv_w, conv_b, gate_bias, mlstm_norm, B, S):
    nt = S // LANES
    gb = jnp.broadcast_to(gate_bias.astype(jnp.float32).reshape(2 * MLSTM_HEADS, 1), (2 * MLSTM_HEADS, LANES))
    norm_cols = jnp.broadcast_to(mlstm_norm.astype(jnp.float32).reshape(MLSTM_WIDTH, 1), (MLSTM_WIDTH, LANES))
    lane = np.arange(LANES)
    same_chunk = lane[:, None] // MLSTM_CHUNK == lane[None, :] // MLSTM_CHUNK
    tri = (same_chunk & (lane[:, None] <= lane[None, :])).astype(np.float32)
    last = (same_chunk & (lane[:, None] % MLSTM_CHUNK == MLSTM_CHUNK - 1)).astype(np.float32)
    seq = lambda w: pl.BlockSpec((S, w), lambda b: (b, 0))
    tiles = lambda rows: pl.BlockSpec((nt, rows, LANES), lambda b: (b, 0, 0))
    return pl.pallas_call(
        _mlstm_kernel,
        grid=(B,),
        in_specs=[seq(2 * MLSTM_QK_WIDTH), tiles(MLSTM_WIDTH), tiles(MLSTM_WIDTH), tiles(N_GATE_ROWS),
                  _full((2 * MLSTM_HEADS, LANES)), _full((LANES, LANES)), _full((LANES, LANES)),
                  _full((CONV_WIDTH, 2 * MLSTM_QK_WIDTH)), _full((1, 2 * MLSTM_QK_WIDTH)),
                  _full((MLSTM_WIDTH, LANES))],
        out_specs=seq(MLSTM_WIDTH),
        out_shape=jax.ShapeDtypeStruct((B * S, MLSTM_WIDTH), jnp.bfloat16),
        scratch_shapes=[pltpu.VMEM((S, 2 * MLSTM_QK_WIDTH), jnp.bfloat16),
                        pltpu.VMEM((5, nt * 2 * MLSTM_HEADS, LANES), jnp.float32)],
        compiler_params=_params(1),
        name="mlstm",
    )(mqk, mvT, moT, smallT, gb, jnp.asarray(tri), jnp.asarray(last), conv_w, conv_b.reshape(1, -1), norm_cols)


def _mem_kv_kernel(mem_ref, g_ref, w_ref, k_ref, v_ref):
    mn = _rms(mem_ref[...], g_ref[...]).astype(jnp.bfloat16)
    k_ref[...] = _mm(mn, w_ref[:, :D_MODEL]).astype(k_ref.dtype)
    v_ref[...] = _mm(mn, w_ref[:, D_MODEL:]).astype(v_ref.dtype)


def _mem_kv(mem, gain, w_xkv):
    B, M, _ = mem.shape
    spec = pl.BlockSpec((None, M, D_MODEL), lambda b: (b, 0, 0))
    return pl.pallas_call(
        _mem_kv_kernel,
        grid=(B,),
        in_specs=[spec, _full((1, D_MODEL)), _full((D_MODEL, 2 * D_MODEL))],
        out_specs=[spec, spec],
        out_shape=[jax.ShapeDtypeStruct((B, M, D_MODEL), jnp.bfloat16)] * 2,
        compiler_params=_params(1),
        name="mem_kv",
    )(mem, gain, w_xkv.astype(jnp.bfloat16))


TM_X = 1024
X_HALVES = 2


def _mix_xattn_kernel(ynsa_ref, yml_ref, x_ref, wout_ref, gpost_ref, gpre_ref, wq_ref, k_ref, v_ref,
                      wo_ref, gpost2_ref, out_ref):
    bf16 = jnp.bfloat16
    halves = [slice(i * (TM_X // X_HALVES), (i + 1) * (TM_X // X_HALVES)) for i in range(X_HALVES)]
    y = [_mm(ynsa_ref[r, :], wout_ref[:NSA_WIDTH, :]) + _mm(yml_ref[r, :], wout_ref[NSA_WIDTH:, :]) for r in halves]
    x1 = [x_ref[r, :] + _rms(y[i], gpost_ref[...]) for i, r in enumerate(halves)]
    h2 = [_rms(x1[i], gpre_ref[...]).astype(bf16) for i in range(X_HALVES)]
    q = [(_mm(h2[i], wq_ref[...]) * (XATTN_HEAD_DIM ** -0.5)).astype(bf16) for i in range(X_HALVES)]
    outs = [[] for _ in range(X_HALVES)]
    for hh in range(XATTN_HEADS):
        sl = slice(hh * XATTN_HEAD_DIM, (hh + 1) * XATTN_HEAD_DIM)
        s = [_nt(q[i][:, sl], k_ref[:, sl]) for i in range(X_HALVES)]
        for i in range(X_HALVES):
            p = jnp.exp(s[i] - jnp.max(s[i], axis=1, keepdims=True))
            l = jnp.sum(p, axis=1, keepdims=True)
            outs[i].append((_mm(p.astype(bf16), v_ref[:, sl]) * (1.0 / l)).astype(bf16))
    y2 = [_mm(jnp.concatenate(outs[i], axis=1), wo_ref[...]) for i in range(X_HALVES)]
    for i, r in enumerate(halves):
        out_ref[r, :] = x1[i] + _rms(y2[i], gpost2_ref[...])


def _mix_xattn(ynsa, yml, x2d, w_out, g_post, g_pre, w_xq, kx, vx, w_xo, g_post2, B, S):
    nt = S // TM_X
    M = kx.shape[1]
    tok = lambda w: pl.BlockSpec((TM_X, w), lambda b, i: (b * nt + i, 0))
    mem_spec = pl.BlockSpec((None, M, D_MODEL), lambda b, i: (b, 0, 0))
    sq = _full((D_MODEL, D_MODEL))
    row = _full((1, D_MODEL))
    bf = lambda w: w.astype(jnp.bfloat16)
    return pl.pallas_call(
        _mix_xattn_kernel,
        grid=(B, nt),
        in_specs=[tok(NSA_WIDTH), tok(MLSTM_WIDTH), tok(D_MODEL), sq, row, row, sq, mem_spec, mem_spec,
                  sq, row],
        out_specs=tok(D_MODEL),
        out_shape=jax.ShapeDtypeStruct((B * S, D_MODEL), jnp.float32),
        compiler_params=_params(2),
        name="mix_xattn",
    )(ynsa, yml, x2d, bf(w_out), g_post, g_pre, bf(w_xq), kx, vx, bf(w_xo), g_post2)


TM_F = 512


def _ffn_kernel(x_ref, gpre_ref, wgu_ref, wd_ref, gpost_ref, out_ref, acc_ref):
    bf16 = jnp.bfloat16
    x = x_ref[...]
    h = _rms(x, gpre_ref[...]).astype(bf16)
    acc_ref[...] = jnp.zeros_like(acc_ref)
    for j in range(D_FF // F_TILE):
        cols = slice(j * F_TILE, (j + 1) * F_TILE)
        g = _mm(h, wgu_ref[:, cols])
        u = _mm(h, wgu_ref[:, D_FF + j * F_TILE:D_FF + (j + 1) * F_TILE])
        act = (g * jax.nn.sigmoid(g) * u).astype(bf16)
        acc_ref[...] += _mm(act, wd_ref[cols, :])
    out_ref[...] = x + _rms(acc_ref[...], gpost_ref[...])


def _ffn(x2d, g_pre, w_gate_up, w_down, g_post):
    T = x2d.shape[0]
    tok = pl.BlockSpec((TM_F, D_MODEL), lambda i: (i, 0))
    row = _full((1, D_MODEL))
    return pl.pallas_call(
        _ffn_kernel,
        grid=(T // TM_F,),
        in_specs=[tok, row, _full((D_MODEL, 2 * D_FF)), _full((D_FF, D_MODEL)), row],
        out_specs=tok,
        out_shape=jax.ShapeDtypeStruct((T, D_MODEL), jnp.float32),
        scratch_shapes=[pltpu.VMEM((TM_F, D_MODEL), jnp.float32)],
        compiler_params=_params(1),
        name="ffn",
    )(x2d, g_pre, w_gate_up.astype(jnp.bfloat16), w_down.astype(jnp.bfloat16), g_post)


def _layer(x, mem, rel_bias, mix_norm_pre, w_in, cmp_pos_k, cmp_pos_v, cmp_w1_k, cmp_w2_k, cmp_w1_v,
           cmp_w2_v, conv_w, conv_b, mlstm_gate_bias, mlstm_norm, w_out, mix_norm_post, xattn_norm_pre,
           mem_norm, w_xq, w_xkv, w_xo, xattn_norm_post, ffn_norm_pre, w_gate_up, w_down, ffn_norm_post):
    B, S, _ = x.shape
    row = lambda g: g.reshape(1, -1).astype(jnp.float32)
    x2d = x.reshape(B * S, D_MODEL)
    w_tok, w_feat = _in_proj_weights(w_in)
    (kc, vc, ksl, kwn, mqk, qT, vslT, vwnT, mvT, moT, smallT) = _in_proj(
        x2d, row(mix_norm_pre), w_tok, w_feat)
    kcmp, vcmpT = _compress(kc, vc, cmp_pos_k, cmp_pos_v, cmp_w1_k, cmp_w2_k, cmp_w1_v, cmp_w2_v, B, S)
    tables = _bias_tables(rel_bias.astype(jnp.float32))
    ynsa = _nsa(qT, kcmp, vcmpT, ksl, vslT, kwn, vwnT, smallT, tables, B, S)
    yml = _mlstm(mqk, mvT, moT, smallT, conv_w, conv_b, mlstm_gate_bias, mlstm_norm, B, S)
    kx, vx = _mem_kv(mem, row(mem_norm), w_xkv)
    x2 = _mix_xattn(ynsa, yml, x2d, w_out, row(mix_norm_post), row(xattn_norm_pre), w_xq, kx, vx, w_xo,
                    row(xattn_norm_post), B, S)
    x3 = _ffn(x2, row(ffn_norm_pre), w_gate_up, w_down, row(ffn_norm_post))
    return x3.reshape(B, S, D_MODEL)


def kernel(x, mem, rel_bias, mix_norm_pre, w_in, cmp_pos_k, cmp_pos_v, cmp_w1_k, cmp_w2_k, cmp_w1_v, cmp_w2_v,
           conv_w, conv_b, mlstm_gate_bias, mlstm_norm, w_out, mix_norm_post, xattn_norm_pre, mem_norm, w_xq,
           w_xkv, w_xo, xattn_norm_post, ffn_norm_pre, w_gate_up, w_down, ffn_norm_post):
    depth = w_in.shape[0]
    for l in range(depth):
        x = _layer(x, mem, rel_bias, mix_norm_pre[l], w_in[l], cmp_pos_k[l], cmp_pos_v[l], cmp_w1_k[l],
                   cmp_w2_k[l], cmp_w1_v[l], cmp_w2_v[l], conv_w[l], conv_b[l], mlstm_gate_bias[l],
                   mlstm_norm[l], w_out[l], mix_norm_post[l], xattn_norm_pre[l], mem_norm[l], w_xq[l],
                   w_xkv[l], w_xo[l], xattn_norm_post[l], ffn_norm_pre[l], w_gate_up[l], w_down[l],
                   ffn_norm_post[l])
    return x
```

```python
import functools
import math

import numpy as np
import jax
import jax.numpy as jnp
from jax import lax
from jax.experimental import pallas as pl
from jax.experimental.pallas import tpu as pltpu

D_MODEL = 1024
NSA_WIDTH = 512
NSA_HEAD_DIM = 64
NSA_HEADS = 8
NSA_KV_HEADS = 2
NSA_GROUP = 4
NSA_KV_WIDTH = 128
CMP_STRIDE = 16
CMP_BLOCK = 32
CMP_HIDDEN = 256
SEL_BLOCK = 64
N_SELECT = 16
WINDOW = 512
Q_BLOCK = 128
FORCED_SCORE = 1.0e4
MLSTM_WIDTH = 512
MLSTM_HEADS = 4
MLSTM_V_DIM = 128
MLSTM_QK_DIM = 64
MLSTM_QK_WIDTH = 256
MLSTM_CHUNK = 64
CONV_WIDTH = 4
REL_BUCKETS = 32
REL_MAX_DISTANCE = 128
XATTN_HEADS = 4
XATTN_HEAD_DIM = 256
D_FF = 2816
NORM_EPS = 1e-6
NEG_INF = -1.0e30
LOG2E = math.log2(math.e)

IN_SIZES = (NSA_WIDTH,) + (NSA_KV_WIDTH,) * 6 + (NSA_HEADS * 3, MLSTM_QK_WIDTH, MLSTM_QK_WIDTH,
                                                 MLSTM_WIDTH, MLSTM_HEADS, MLSTM_HEADS, MLSTM_WIDTH)
IN_OFFSETS = tuple(int(o) for o in np.cumsum((0,) + IN_SIZES)[:-1])

LANES = 128
SUBLANES = 8
BF16_ROWS = 16
VMEM_LIMIT_BYTES = 56 * 1024 * 1024

N_GATE_ROWS = 32
F_TILE = 256


def _rms(x, gain):
    return x * lax.rsqrt(jnp.mean(x * x, axis=-1, keepdims=True) + NORM_EPS) * gain


def _nt(a, b):
    return lax.dot_general(a, b, (((1,), (1,)), ((), ())), preferred_element_type=jnp.float32)


def _mm(a, b):
    return jnp.dot(a, b, preferred_element_type=jnp.float32)


def _params(n_axes, flags=None):
    return pltpu.CompilerParams(dimension_semantics=("arbitrary",) * n_axes,
                                vmem_limit_bytes=VMEM_LIMIT_BYTES, flags=flags)


def _full(shape, buffers=None):
    nd = len(shape)
    mode = None if buffers is None else pl.Buffered(buffers)
    return pl.BlockSpec(shape, lambda *_: (0,) * nd, pipeline_mode=mode)


TM_IN = 1024
IN_HALVES = 2
_TOK_GROUPS = (("kc", 128, jnp.float32), ("vc", 128, jnp.float32), ("ksl", 128, jnp.bfloat16),
               ("kwn", 128, jnp.bfloat16), ("mqk", 512, jnp.bfloat16))
_FEAT_GROUPS = (("qT", 512, jnp.bfloat16), ("vslT", 128, jnp.bfloat16), ("vwnT", 128, jnp.bfloat16),
                ("mvT", 512, jnp.bfloat16), ("moT", 512, jnp.bfloat16), ("smallT", N_GATE_ROWS, jnp.float32))


def _in_proj_kernel(x_ref, g_ref, wtok_ref, wfeat_ref, *out_refs):
    n_tok = len(_TOK_GROUPS)
    half = TM_IN // IN_HALVES

    def norm(i):
        return _rms(x_ref[i * half:(i + 1) * half, :], g_ref[...]).astype(jnp.bfloat16)

    def token_major(i, h):
        off = 0
        for (name, width, dt), o_ref in zip(_TOK_GROUPS, out_refs[:n_tok]):
            o_ref[i * half:(i + 1) * half, :] = _mm(h, wtok_ref[:, off:off + width]).astype(dt)
            off += width

    def feature_major(i, h):
        off = 0
        for (name, rows, dt), o_ref in zip(_FEAT_GROUPS, out_refs[n_tok:]):
            r = _nt(wfeat_ref[off:off + rows, :], h)
            if name == "qT":
                r = r * (NSA_HEAD_DIM ** -0.5 * LOG2E)
            for j in range(half // LANES):
                o_ref[i * (half // LANES) + j] = r[:, j * LANES:(j + 1) * LANES].astype(dt)
            off += rows

    h = norm(0)
    for i in range(IN_HALVES):
        token_major(i, h)
        h_next = norm(i + 1) if i + 1 < IN_HALVES else None
        feature_major(i, h)
        h = h_next


def _in_proj(x2d, gain, w_tok, w_feat):
    T = x2d.shape[0]
    n_tok_cols = w_tok.shape[1]
    n_feat_rows = w_feat.shape[0]
    out_shape, out_specs = [], []
    for name, width, dt in _TOK_GROUPS:
        out_shape.append(jax.ShapeDtypeStruct((T, width), dt))
        out_specs.append(pl.BlockSpec((TM_IN, width), lambda i: (i, 0)))
    for name, rows, dt in _FEAT_GROUPS:
        out_shape.append(jax.ShapeDtypeStruct((T // LANES, rows, LANES), dt))
        out_specs.append(pl.BlockSpec((TM_IN // LANES, rows, LANES), lambda i: (i, 0, 0)))
    return pl.pallas_call(
        _in_proj_kernel,
        grid=(T // TM_IN,),
        in_specs=[pl.BlockSpec((TM_IN, D_MODEL), lambda i: (i, 0)),
                  _full((1, D_MODEL)),
                  _full((D_MODEL, n_tok_cols)),
                  _full((n_feat_rows, D_MODEL))],
        out_specs=out_specs,
        out_shape=out_shape,
        compiler_params=_params(1),
        name="in_proj",
    )(x2d, gain, w_tok, w_feat)


def _in_proj_weights(w_in):
    (nq, kc, vc, ksl, vsl, kwn, vwn, gt, mq, mk, mv, mi, mf, mo) = [
        w_in[:, o:o + s] for o, s in zip(IN_OFFSETS, IN_SIZES)]
    gt_r = gt.reshape(D_MODEL, NSA_KV_HEADS, NSA_GROUP, 3).transpose(0, 3, 1, 2).reshape(D_MODEL, 24)
    small = jnp.concatenate([gt_r, mi, mf], axis=1)
    w_tok = jnp.concatenate([kc, vc, ksl, kwn, mq, mk], axis=1)
    w_feat = jnp.concatenate([nq, vsl, vwn, mv, mo, small], axis=1).T
    return w_tok.astype(jnp.bfloat16), w_feat.astype(jnp.bfloat16)


N_CHUNK_COLS = CMP_STRIDE * NSA_KV_WIDTH
N_HID2 = NSA_KV_HEADS * CMP_HIDDEN


def _compress_one(c, pos_ref, w1_ref, n_chunks):
    lo = _mm((c + pos_ref[0:1, :]).astype(jnp.bfloat16), w1_ref[0])
    hi = _mm((c + pos_ref[1:2, :]).astype(jnp.bfloat16), w1_ref[1])
    pre = lo + pltpu.roll(hi, n_chunks - 1, 0)
    return (pre * jax.nn.sigmoid(pre)).astype(jnp.bfloat16)


def _chunk_rows(ref, n_chunks):
    return jnp.concatenate([ref[pl.ds(t, n_chunks, stride=CMP_STRIDE), :] for t in range(CMP_STRIDE)], axis=1)


def _compress_kernel(kc_ref, vc_ref, posk_ref, posv_ref, w1k_ref, w1v_ref, w2k_ref, w2vT_ref,
                     kcmp_ref, vcmpT_ref):
    n_chunks = kc_ref.shape[0] // CMP_STRIDE
    hid_k = _compress_one(_chunk_rows(kc_ref, n_chunks), posk_ref, w1k_ref, n_chunks)
    kcmp = _mm(hid_k, w2k_ref[...])
    row = lax.broadcasted_iota(jnp.int32, kcmp.shape, 0)
    kcmp_ref[...] = jnp.where(row < n_chunks - 1, kcmp, 0.0).astype(kcmp_ref.dtype)
    hid_v = _compress_one(_chunk_rows(vc_ref, n_chunks), posv_ref, w1v_ref, n_chunks)
    vcmpT = _nt(w2vT_ref[...], hid_v)
    col = lax.broadcasted_iota(jnp.int32, vcmpT.shape, 1)
    vcmpT_ref[...] = jnp.where(col < n_chunks - 1, vcmpT, 0.0).astype(vcmpT_ref.dtype)


def _compress_weights(pos, w1, w2):
    eye = jnp.eye(NSA_KV_HEADS, dtype=w1.dtype)
    w1r = w1.reshape(2, CMP_STRIDE, NSA_HEAD_DIM, CMP_HIDDEN)
    w1e = jnp.einsum('atdj,hg->athdgj', w1r, eye).reshape(2, N_CHUNK_COLS, N_HID2)
    pos_e = jnp.broadcast_to(pos.reshape(2, CMP_STRIDE, 1, NSA_HEAD_DIM),
                             (2, CMP_STRIDE, NSA_KV_HEADS, NSA_HEAD_DIM)).reshape(2, N_CHUNK_COLS)
    w2e = jnp.einsum('jd,hg->hjgd', w2, eye).reshape(N_HID2, NSA_KV_WIDTH)
    return pos_e, w1e.astype(jnp.bfloat16), w2e.astype(jnp.bfloat16)


def _compress(kc, vc, cmp_pos_k, cmp_pos_v, cmp_w1_k, cmp_w2_k, cmp_w1_v, cmp_w2_v, B, S):
    n_chunks = S // CMP_STRIDE
    posk, w1k, w2k = _compress_weights(cmp_pos_k, cmp_w1_k, cmp_w2_k)
    posv, w1v, w2v = _compress_weights(cmp_pos_v, cmp_w1_v, cmp_w2_v)
    chunk_spec = pl.BlockSpec((S, NSA_KV_WIDTH), lambda b: (b, 0))
    return pl.pallas_call(
        _compress_kernel,
        grid=(B,),
        in_specs=[chunk_spec, chunk_spec,
                  _full((2, N_CHUNK_COLS)), _full((2, N_CHUNK_COLS)),
                  _full((2, N_CHUNK_COLS, N_HID2)), _full((2, N_CHUNK_COLS, N_HID2)),
                  _full((N_HID2, NSA_KV_WIDTH)), _full((NSA_KV_WIDTH, N_HID2))],
        out_specs=[pl.BlockSpec((None, n_chunks, NSA_KV_WIDTH), lambda b: (b, 0, 0)),
                   pl.BlockSpec((None, NSA_KV_WIDTH, n_chunks), lambda b: (b, 0, 0))],
        out_shape=[jax.ShapeDtypeStruct((B, n_chunks, NSA_KV_WIDTH), jnp.bfloat16),
                   jax.ShapeDtypeStruct((B, NSA_KV_WIDTH, n_chunks), jnp.bfloat16)],
        compiler_params=_params(1),
        name="compress",
    )(kc, vc, posk, posv, w1k, w1v, w2k, w2v.T)


GQ = NSA_GROUP * Q_BLOCK
TINY = 1e-30
CMP_TAB_ROWS = 512
CMP_TAB_ZERO = 248
CMP_TAB_LOOKUP = (232, 256)
SEL_STEP_SHIFT = 2
KEY_SUPER = Q_BLOCK << SEL_STEP_SHIFT
SEL_TAB_ZERO = KEY_SUPER + Q_BLOCK
SEL_TAB_ROWS = SEL_TAB_ZERO + KEY_SUPER
N_WIN_TILES = WINDOW // Q_BLOCK + 1
SEL_SUB_CHUNKS = 2
V_ROWS = NSA_HEAD_DIM + BF16_ROWS


def _bucket_np(dist):
    n = np.maximum(dist, 0)
    max_exact = REL_BUCKETS // 2
    nf = np.maximum(n, 1).astype(np.float64)
    large = max_exact + (np.log(nf / max_exact) / math.log(REL_MAX_DISTANCE / max_exact)
                         * (REL_BUCKETS - max_exact)).astype(np.int64)
    large = np.minimum(large, REL_BUCKETS - 1)
    return np.where(n < max_exact, n, large).astype(np.int32)


def _bias_index_tables():
    m = np.arange(Q_BLOCK)[:, None]
    r = np.arange(Q_BLOCK)[None, :]
    diag = np.where(r - m >= 0, _bucket_np(r - m), -1).astype(np.int32)
    off = _bucket_np(Q_BLOCK + r - m)
    jp = np.arange(*CMP_TAB_LOOKUP)[:, None] - CMP_TAB_ZERO
    d_c = r - CMP_STRIDE * jp - (CMP_BLOCK - 1)
    cmp_idx = np.where(d_c >= 0, _bucket_np(d_c), -1).astype(np.int32)
    return diag, off, cmp_idx


def _bias_tables_kernel(rb_ref, diag_idx_ref, off_idx_ref, cmp_idx_ref, sel_ref, win_ref, cmp_ref):
    f32 = jnp.float32

    def lookup(idx, head):
        far = rb_ref[head, REL_BUCKETS - 1]
        acc = jnp.full(idx.shape, NEG_INF, f32)
        for k in range(REL_BUCKETS):
            acc = jnp.where(idx == k, (rb_ref[head, k] - far) * LOG2E, acc)
        return acc

    m_io = lax.broadcasted_iota(jnp.int32, (Q_BLOCK, Q_BLOCK), 0)
    r_io = lax.broadcasted_iota(jnp.int32, (Q_BLOCK, Q_BLOCK), 1)
    neg_tile = jnp.full((Q_BLOCK, Q_BLOCK), NEG_INF, f32)
    lo, hi = CMP_TAB_LOOKUP
    for h in range(NSA_KV_HEADS):
        for g in range(NSA_GROUP):
            head = h * NSA_GROUP + g
            sl = slice(g * Q_BLOCK, (g + 1) * Q_BLOCK)
            far = 0.0
            far_tile = jnp.full((Q_BLOCK, Q_BLOCK), far, f32)
            diag_v = lookup(diag_idx_ref[...], head)
            off_v = lookup(off_idx_ref[...], head)
            n_far = (SEL_TAB_ZERO - Q_BLOCK) // Q_BLOCK
            for t in range(SEL_TAB_ROWS // Q_BLOCK):
                rows = slice(t * Q_BLOCK, (t + 1) * Q_BLOCK)
                tile = far_tile if t < n_far else off_v if t == n_far else diag_v if t == n_far + 1 else neg_tile
                sel_ref[h, rows, sl] = tile
            win_ref[h, 0, :, sl] = diag_v
            win_ref[h, 1, :, sl] = off_v
            for back in range(2, N_WIN_TILES - 1):
                win_ref[h, back, :, sl] = far_tile
            win_ref[h, N_WIN_TILES - 1, :, sl] = jnp.where(r_io < m_io, far, NEG_INF)
            win_ref[h, N_WIN_TILES, :, sl] = neg_tile
            cmp_ref[h, 0:lo, sl] = jnp.full((lo, Q_BLOCK), far, f32)
            cmp_ref[h, lo:hi, sl] = lookup(cmp_idx_ref[...], head)
            cmp_ref[h, hi:CMP_TAB_ROWS, sl] = jnp.full((CMP_TAB_ROWS - hi, Q_BLOCK), NEG_INF, f32)


def _bias_tables(rel_bias):
    diag_idx, off_idx, cmp_idx = _bias_index_tables()
    shapes = [(NSA_KV_HEADS, SEL_TAB_ROWS, GQ), (NSA_KV_HEADS, N_WIN_TILES + 1, Q_BLOCK, GQ),
              (NSA_KV_HEADS, CMP_TAB_ROWS, GQ)]
    return pl.pallas_call(
        _bias_tables_kernel,
        in_specs=[pl.BlockSpec(memory_space=pltpu.SMEM),
                  _full(diag_idx.shape), _full(off_idx.shape), _full(cmp_idx.shape)],
        out_specs=[_full(s) for s in shapes],
        out_shape=[jax.ShapeDtypeStruct(s, jnp.float32) for s in shapes],
        grid=(1,),
        compiler_params=_params(1),
        name="bias_tables",
    )(rel_bias, jnp.asarray(diag_idx), jnp.asarray(off_idx), jnp.asarray(cmp_idx))


def _overlap_np(n_cmp_rows, n_sel):
    cmp_start = np.arange(n_cmp_rows) * CMP_STRIDE
    cmp_end = cmp_start + CMP_BLOCK - 1
    sel_start = np.arange(n_sel) * SEL_BLOCK
    ov = ((cmp_start[None, :] <= sel_start[:, None] + SEL_BLOCK - 1)
          & (cmp_end[None, :] >= sel_start[:, None])).astype(np.float32)
    ov[:, n_cmp_rows - 1] = 0.0
    return ov


def _tile4(a):
    return jnp.concatenate([a] * NSA_GROUP, axis=1)


def _select_blocks(score, score_ref, n_top, hooks=()):
    n_sel = score.shape[0]
    score_ref[...] = score
    n_grp = n_sel // SUBLANES
    grp = [score[SUBLANES * v:SUBLANES * (v + 1), :] for v in range(n_grp)]
    cnt = [jnp.zeros((SUBLANES, Q_BLOCK), jnp.int32) for _ in range(n_grp)]
    sub_io = lax.broadcasted_iota(jnp.int32, (SUBLANES, Q_BLOCK), 0)
    hook_at = {(i * n_sel) // len(hooks): hk for i, hk in enumerate(hooks)} if hooks else {}
    for jp in range(n_sel):
        if jp in hook_at:
            hook_at[jp]()
        row = score_ref[jp:jp + 1, :]
        for v in range(n_grp):
            if SUBLANES * v > jp:
                inc = (row >= grp[v]).astype(jnp.int32)
            elif SUBLANES * (v + 1) - 1 < jp:
                inc = (row > grp[v]).astype(jnp.int32)
            else:
                tie = (sub_io > jp - SUBLANES * v).astype(jnp.int32)
                inc = jnp.where(row > grp[v], 1, jnp.where(row == grp[v], tie, 0))
            cnt[v] = cnt[v] + inc
    return [jnp.where(cnt[v] < n_top, 0.0, NEG_INF) for v in range(n_grp)]


def _nsa_kernel(q_ref, kcmp_ref, vcmpT_ref, ksl_ref, vslT_ref, kwn_ref, vwnT_ref, gate_ref,
                seltab_ref, wintab_ref, cmptab_ref, ovl_ref, blkind_ref, out_ref, score_ref, selb_ref, sbuf_ref,
                sbuf2_ref, swin_ref):
    c = pl.program_id(1)
    n_cmp = kcmp_ref.shape[0]
    n_sel = ovl_ref.shape[0]
    n_top = min(N_SELECT, n_sel)
    f32 = jnp.float32
    bf16 = jnp.bfloat16
    DH = NSA_HEAD_DIM
    heads = range(NSA_KV_HEADS)

    q = q_ref[...]
    zq = jnp.zeros((DH, GQ), bf16)
    qcat, qpad = [], []
    for h in heads:
        qcat.append(jnp.concatenate([q[(h * NSA_GROUP + g) * DH:(h * NSA_GROUP + g + 1) * DH, :]
                                     for g in range(NSA_GROUP)], axis=1))
        qpad.append(jnp.concatenate([qcat[h], zq] if h == 0 else [zq, qcat[h]], axis=0))

    backs = list(range(N_WIN_TILES))
    kts = [jnp.maximum(c - back, 0) for back in backs]
    slots = [jnp.where(c >= back, back, N_WIN_TILES) for back in backs]
    m_win = [jnp.full((1, GQ), NEG_INF, f32) for _ in heads]

    def win_score(back, h):
        key0 = pl.multiple_of(kts[back] * Q_BLOCK, Q_BLOCK)
        s = _mm(kwn_ref[pl.ds(key0, Q_BLOCK), :], qpad[h]) + wintab_ref[h, slots[back]]
        swin_ref[h, back * Q_BLOCK:(back + 1) * Q_BLOCK, :] = s
        return jnp.max(s, axis=0, keepdims=True)

    cmp_off = pl.multiple_of(CMP_TAB_ZERO - (Q_BLOCK // CMP_STRIDE) * c, SUBLANES)
    kcmp = kcmp_ref[...]
    j_io = lax.broadcasted_iota(jnp.int32, (n_sel, Q_BLOCK), 0)
    r_io = lax.broadcasted_iota(jnp.int32, (n_sel, Q_BLOCK), 1)
    cur = (Q_BLOCK // SEL_BLOCK) * c + (r_io >= SEL_BLOCK).astype(jnp.int32)
    forced = (j_io == 0) | (j_io == cur) | (j_io == cur - 1)
    visible = j_io <= cur
    o_c = []
    for h in heads:
        tab = cmptab_ref[h, pl.ds(cmp_off, n_cmp), :]
        s = _mm(kcmp, qpad[h]) + tab
        m = jnp.maximum(jnp.max(s, axis=0, keepdims=True), 0.1 * NEG_INF)
        p = jnp.exp2(s - m)
        l = jnp.sum(p, axis=0, keepdims=True)
        pn = p * (1.0 / jnp.maximum(l, TINY))
        o_c.append(_mm(vcmpT_ref[h * DH:(h + 1) * DH, :], pn.astype(bf16)))
        psum = pn[:, 0:Q_BLOCK]
        for g in range(1, NSA_GROUP):
            psum = psum + pn[:, g * Q_BLOCK:(g + 1) * Q_BLOCK]
        imp = jnp.dot(ovl_ref[...], psum, precision=lax.Precision.HIGHEST,
                      preferred_element_type=f32)
        score = jnp.where(forced, FORCED_SCORE, jnp.where(visible, imp, -1.0))

        def hook(back, h=h):
            m_win[h] = jnp.maximum(m_win[h], win_score(back, h))

        rows = _select_blocks(score, score_ref.at[h], n_top, hooks=[functools.partial(hook, back) for back in backs])
        for v, blk in enumerate(rows):
            selb_ref[h, SUBLANES * v:SUBLANES * (v + 1), :] = blk

    def ones_rows(n_keys):
        return jnp.ones((V_ROWS - DH, n_keys), bf16)

    acc_w = [jnp.zeros((V_ROWS, GQ), f32) for _ in heads]

    def win_value(back):
        for h in heads:
            p = jnp.exp2(swin_ref[h, back * Q_BLOCK:(back + 1) * Q_BLOCK, :] - m_win[h]).astype(bf16)
            vT = jnp.concatenate([vwnT_ref[kts[back], h * DH:(h + 1) * DH, :], ones_rows(Q_BLOCK)], axis=0)
            acc_w[h] = acc_w[h] + _mm(vT, p)

    blocks_per_step = KEY_SUPER // SEL_BLOCK
    tiles_per_step = KEY_SUPER // Q_BLOCK

    sub = KEY_SUPER // SEL_SUB_CHUNKS
    blocks_per_sub = sub // SEL_BLOCK
    tiles_per_sub = sub // Q_BLOCK

    own_lanes = [(lax.broadcasted_iota(jnp.int32, (sub, NSA_KV_WIDTH), 1) >= DH) == bool(h) for h in heads]

    def q_with_mask_rows(j, h):
        blk0 = pl.multiple_of(j * blocks_per_step, blocks_per_step)
        rows = jnp.concatenate([_tile4(selb_ref[h, pl.ds(blk0, blocks_per_step), :]),
                                jnp.zeros((DH - blocks_per_step, GQ), f32)], axis=0).astype(bf16)
        return jnp.concatenate([qcat[h], rows] if h == 0 else [rows, qcat[h]], axis=0)

    def score_chunk(j, ci, h, q_aug, buf_ref):
        key0 = pl.multiple_of(j * KEY_SUPER + ci * sub, sub)
        tab_off = pl.multiple_of(jnp.maximum(j * KEY_SUPER - c * Q_BLOCK + SEL_TAB_ZERO, 0) + ci * sub, Q_BLOCK)
        k_aug = jnp.where(own_lanes[h], ksl_ref[pl.ds(key0, sub), :], blkind_ref[h, ci * sub:(ci + 1) * sub, :])
        s = _mm(k_aug, q_aug) + seltab_ref[h, pl.ds(tab_off, sub), :]
        buf_ref[h, ci * sub:(ci + 1) * sub, :] = s
        return jnp.max(s, axis=0, keepdims=True)

    def value_chunk(j, ci, h, m_h, buf_ref):
        p = jnp.exp2(buf_ref[h, ci * sub:(ci + 1) * sub, :] - m_h).astype(bf16)
        vT = jnp.concatenate([vslT_ref[j * tiles_per_step + ci * tiles_per_sub + i, h * DH:(h + 1) * DH, :]
                              for i in range(tiles_per_sub)], axis=1)
        return _mm(jnp.concatenate([vT, ones_rows(sub)], axis=0), p)

    def values(j, m_old, m_cur, acc, src_ref, before_chunk=None):
        acc = [jnp.exp2(m_old[h] - m_cur[h]) * acc[h] for h in heads]
        for ci in range(SEL_SUB_CHUNKS):
            if before_chunk is not None:
                before_chunk(ci)
            for h in heads:
                acc[h] = acc[h] + value_chunk(j, ci, h, m_cur[h], src_ref)
        return tuple(acc)

    def sel_step(j, carry, src_ref, dst_ref):
        m_old, m_cur, acc = carry
        m_run = list(m_cur)
        q_aug = [q_with_mask_rows(j + 1, h) for h in heads]

        def scores(ci):
            for h in heads:
                m_run[h] = jnp.maximum(m_run[h], score_chunk(j + 1, ci, h, q_aug[h], dst_ref))

        acc = values(j, m_old, m_cur, acc, src_ref, before_chunk=scores)
        return m_cur, tuple(m_run), acc

    def sel_pair(i, carry):
        carry = sel_step(2 * i, carry, sbuf_ref, sbuf2_ref)
        return sel_step(2 * i + 1, carry, sbuf2_ref, sbuf_ref)

    n_steps = lax.shift_right_logical(c, SEL_STEP_SHIFT) + 1
    m_init = tuple(jnp.full((1, GQ), NEG_INF, f32) for _ in heads)
    acc_init = tuple(jnp.zeros((V_ROWS, GQ), f32) for _ in heads)
    m_first = list(m_init)
    win_order = list(backs)
    q_aug0 = [q_with_mask_rows(0, h) for h in heads]
    for ci in range(SEL_SUB_CHUNKS):
        for _ in range(-(-N_WIN_TILES // SEL_SUB_CHUNKS)):
            if win_order:
                win_value(win_order.pop(0))
        for h in heads:
            m_first[h] = jnp.maximum(m_first[h], score_chunk(0, ci, h, q_aug0[h], sbuf_ref))
    while win_order:
        win_value(win_order.pop(0))
    o_w = [a[0:DH, :] * (1.0 / a[DH:DH + 1, :]) for a in acc_w]
    n_piped = n_steps - 1
    odd = n_piped & 1
    carry = lax.fori_loop(0, lax.shift_right_logical(n_piped, 1), sel_pair, (m_init, tuple(m_first), acc_init))
    carry = lax.fori_loop(0, odd, lambda _, cr: sel_step(n_piped - 1, cr, sbuf_ref, sbuf2_ref), carry)
    m_old, m_cur, acc = carry
    acc = lax.fori_loop(0, 1 - odd, lambda _, a: values(n_piped, m_old, m_cur, a, sbuf_ref), acc)
    acc = lax.fori_loop(0, odd, lambda _, a: values(n_piped, m_old, m_cur, a, sbuf2_ref), acc)
    o_s = [a[0:DH, :] * (1.0 / jnp.maximum(a[DH:DH + 1, :], TINY)) for a in acc]

    for h in heads:
        ys = []
        for g in range(NSA_GROUP):
            sl = slice(g * Q_BLOCK, (g + 1) * Q_BLOCK)
            row0 = h * NSA_GROUP + g
            gates = [jax.nn.sigmoid(gate_ref[kind * NSA_HEADS + row0:kind * NSA_HEADS + row0 + 1, :])
                     for kind in range(3)]
            ys.append(gates[0] * o_c[h][:, sl] + gates[1] * o_s[h][:, sl] + gates[2] * o_w[h][:, sl])
        yT = jnp.concatenate(ys, axis=0)
        for half in range(2):
            col = (2 * h + half) * LANES
            out_ref[:, col:col + LANES] = yT[half * LANES:(half + 1) * LANES, :].T.astype(out_ref.dtype)


def _nsa(qT, kcmp, vcmpT, ksl, vslT, kwn, vwnT, smallT, tables, B, S):
    assert S % KEY_SUPER == 0
    nq = S // Q_BLOCK
    n_cmp = S // CMP_STRIDE
    n_sel = S // SEL_BLOCK
    seltab, wintab, cmptab = tables
    ovl = jnp.asarray(_overlap_np(n_cmp, n_sel))
    key_blk = np.arange(KEY_SUPER)[:, None] // SEL_BLOCK
    lane = np.arange(NSA_KV_WIDTH)[None, :]
    blkind = jnp.asarray(np.stack([lane - NSA_HEAD_DIM == key_blk, lane == key_blk]), jnp.bfloat16)
    ksl3 = ksl.reshape(B, S, NSA_KV_WIDTH)
    kwn3 = kwn.reshape(B, S, NSA_KV_WIDTH)
    vslT4 = vslT.reshape(B, nq, NSA_KV_WIDTH, Q_BLOCK)
    vwnT4 = vwnT.reshape(B, nq, NSA_KV_WIDTH, Q_BLOCK)
    k_spec = pl.BlockSpec((None, S, NSA_KV_WIDTH), lambda b, c: (b, 0, 0))
    vT_spec = pl.BlockSpec((None, nq, NSA_KV_WIDTH, Q_BLOCK), lambda b, c: (b, 0, 0, 0))
    const = lambda a: pl.BlockSpec(a.shape, lambda b, c: (0,) * a.ndim)
    return pl.pallas_call(
        _nsa_kernel,
        grid=(B, nq),
        in_specs=[pl.BlockSpec((None, NSA_WIDTH, Q_BLOCK), lambda b, c: (b * nq + c, 0, 0)),
                  pl.BlockSpec((None, n_cmp, NSA_KV_WIDTH), lambda b, c: (b, 0, 0)),
                  pl.BlockSpec((None, NSA_KV_WIDTH, n_cmp), lambda b, c: (b, 0, 0)),
                  k_spec, vT_spec, k_spec, vT_spec,
                  pl.BlockSpec((None, N_GATE_ROWS, Q_BLOCK), lambda b, c: (b * nq + c, 0, 0)),
                  const(seltab), const(wintab), const(cmptab), const(ovl), const(blkind)],
        out_specs=pl.BlockSpec((Q_BLOCK, NSA_WIDTH), lambda b, c: (b * nq + c, 0)),
        out_shape=jax.ShapeDtypeStruct((B * S, NSA_WIDTH), jnp.bfloat16),
        scratch_shapes=[pltpu.VMEM((NSA_KV_HEADS, n_sel, Q_BLOCK), jnp.float32),
                        pltpu.VMEM((NSA_KV_HEADS, n_sel, Q_BLOCK), jnp.float32),
                        pltpu.VMEM((NSA_KV_HEADS, KEY_SUPER, GQ), jnp.float32),
                        pltpu.VMEM((NSA_KV_HEADS, KEY_SUPER, GQ), jnp.float32),
                        pltpu.VMEM((NSA_KV_HEADS, N_WIN_TILES * Q_BLOCK, GQ), jnp.float32)],
        compiler_params=_params(2),
        name="nsa",
    )(qT, kcmp, vcmpT, ksl3, vslT4, kwn3, vwnT4, smallT, seltab, wintab, cmptab, ovl, blkind)


CONV_TILE = 256
I_ROW = 24
C_ROWS = MLSTM_V_DIM + BF16_ROWS


def _log_sigmoid(x):
    return jnp.minimum(x, 0.0) - jnp.log(1.0 + jnp.exp(-jnp.abs(x)))


def _mlstm_kernel(mqk_ref, mvT_ref, moT_ref, gates_ref, gb_ref, tri_ref, last_ref, convw_ref, convb_ref,
                  norm_ref, out_ref, qk_ref, rows_ref):
    S = mqk_ref.shape[0]
    nt = S // LANES
    L = MLSTM_CHUNK
    f32 = jnp.float32
    bf16 = jnp.bfloat16
    kscale_row = jnp.where(lax.broadcasted_iota(jnp.int32, (1, 2 * MLSTM_QK_WIDTH), 1) < MLSTM_QK_WIDTH,
                           1.0, MLSTM_QK_DIM ** -0.5)

    def conv_body(i, _):
        t0 = pl.multiple_of(i * CONV_TILE, CONV_TILE)
        cur = mqk_ref[pl.ds(t0, CONV_TILE), :].astype(f32)
        prev_start = pl.multiple_of(jnp.maximum(t0 - BF16_ROWS, 0), BF16_ROWS)
        prev = (mqk_ref[pl.ds(prev_start, BF16_ROWS), :].astype(f32)[BF16_ROWS - SUBLANES:, :]
                * jnp.where(i > 0, 1.0, 0.0))
        ext = jnp.concatenate([prev, cur], axis=0)
        y = convb_ref[...]
        for j in range(CONV_WIDTH):
            lo = SUBLANES - (CONV_WIDTH - 1) + j
            y = y + convw_ref[j:j + 1, :] * ext[lo:lo + CONV_TILE, :]
        y = y * jax.nn.sigmoid(y) * kscale_row
        qk_ref[pl.ds(t0, CONV_TILE), :] = y.astype(bf16)
        return 0

    lax.fori_loop(0, S // CONV_TILE, conv_body, 0)

    H = MLSTM_HEADS
    G8 = 2 * H
    n_rows = nt * G8
    a3 = gates_ref[:, I_ROW:I_ROW + G8, :] + gb_ref[...][None]
    is_f = lax.broadcasted_iota(jnp.int32, a3.shape, 1) >= H
    x = jnp.where(is_f, _log_sigmoid(a3), a3).reshape(n_rows, LANES)
    bcum = jnp.dot(x, tri_ref[...], precision=lax.Precision.HIGHEST, preferred_element_type=f32)
    b_rows = pltpu.roll(bcum, n_rows - H, 0)
    g_rows = x - b_rows
    pos = lax.broadcasted_iota(jnp.int32, (n_rows, LANES), 1) & (L - 1)

    def chunk_cummax(a):
        shift = 1
        while shift < L:
            a = jnp.where(pos >= shift, jnp.maximum(a, pltpu.roll(a, shift, 1)), a)
            shift *= 2
        return a

    def chunk_last(a):
        return jnp.dot(a, last_ref[...], precision=lax.Precision.HIGHEST, preferred_element_type=f32)

    bl_rows = chunk_last(b_rows)
    rows_ref[0] = g_rows
    rows_ref[1] = chunk_cummax(g_rows)
    rows_ref[2] = b_rows
    rows_ref[3] = bl_rows
    rows_ref[4] = chunk_last(chunk_cummax(bl_rows + g_rows))

    s_io = lax.broadcasted_iota(jnp.int32, (LANES, LANES), 0)
    t_io = lax.broadcasted_iota(jnp.int32, (LANES, LANES), 1)
    causal = (s_io <= t_io) & ((s_io >= L) == (t_io >= L))
    lane = lax.broadcasted_iota(jnp.int32, (1, LANES), 1)
    in_chunk = [lane < L, lane >= L]
    QD, VD = MLSTM_QK_DIM, MLSTM_V_DIM
    ones_aug = jnp.ones((C_ROWS - VD, LANES), bf16)
    head_lanes = [(lax.broadcasted_iota(jnp.int32, (LANES, LANES), 1) >= QD) == bool(par) for par in range(2)]

    def swap_halves(row):
        return pltpu.roll(row, L, 1)

    def tile_body(i, state):
        t0 = pl.multiple_of(i * LANES, LANES)
        r0 = pl.multiple_of(i * G8, G8)
        qk = qk_ref[pl.ds(t0, LANES), :]
        qk32 = qk.astype(f32)
        pairsT = [qk32[:, j * LANES:(j + 1) * LANES].T for j in range(4)]
        g8, cm8, b8, bl8, wm8 = [rows_ref[j, pl.ds(r0, G8), :] for j in range(5)]
        new_state = []
        for hh in range(H):
            pair, par = hh // 2, hh % 2
            g_r, cm_r, b_r, bl_r, wm_r = [a[hh:hh + 1, :] for a in (g8, cm8, b8, bl8, wm8)]
            caug, m_in = state[hh]
            qT = pairsT[pair][par * QD:(par + 1) * QD, :]
            kT = pairsT[2 + pair][par * QD:(par + 1) * QD, :]
            kpair = qk[:, (2 + pair) * LANES:(3 + pair) * LANES]
            qmask = jnp.where(head_lanes[par], qk[:, pair * LANES:(pair + 1) * LANES], jnp.zeros((), bf16))
            vaug = jnp.concatenate([mvT_ref[i, hh * VD:(hh + 1) * VD, :], ones_aug], axis=0)

            m_mid = swap_halves(jnp.maximum(bl_r + m_in, wm_r))
            m_prev = jnp.where(in_chunk[0], m_in, m_mid)
            m_next = jnp.maximum(bl_r + m_prev, wm_r)
            m_intra = b_r + cm_r
            m_inter = b_r + m_prev
            m_t = jnp.maximum(m_inter, m_intra)
            e_intra = jnp.exp(m_intra - m_t)
            e_inter = jnp.exp(m_inter - m_t)
            decay = jnp.exp(bl_r + m_prev - m_next)
            inject = jnp.exp(wm_r - m_next)

            g_mat = jnp.broadcast_to(g_r, (LANES, LANES)).T
            w = jnp.exp(jnp.where(causal, g_mat - cm_r, NEG_INF))
            st = _nt(kpair, qmask) * (w * e_intra)
            y = _mm(vaug, st.astype(bf16))
            kw = kT * jnp.exp(bl_r + g_r - wm_r)
            cs = caug
            for p in range(LANES // L):
                qs = jnp.where(in_chunk[p], qT * e_inter, 0.0).astype(bf16)
                y = y + _mm(cs.astype(bf16), qs)
                u = _nt(vaug, jnp.where(in_chunk[p], kw, 0.0).astype(bf16))
                dec = decay if p == 0 else swap_halves(decay)
                inj = inject if p == 0 else swap_halves(inject)
                cs = dec[:, 0:QD] * cs + inj[:, 0:QD] * u
            m_out = jnp.where(in_chunk[1], m_next, swap_halves(m_next))
            new_state.append((cs, m_out))

            den = y[VD:VD + 1, :]
            hT = y[0:VD, :] * (1.0 / jnp.maximum(jnp.abs(den), jnp.exp(-m_t)))
            sl = slice(hh * VD, (hh + 1) * VD)
            hn = hT * lax.rsqrt(jnp.mean(hT * hT, axis=0, keepdims=True) + NORM_EPS) * norm_ref[sl, :]
            yT = jax.nn.sigmoid(moT_ref[i, sl, :].astype(f32)) * hn
            out_ref[pl.ds(t0, LANES), sl] = yT.T.astype(out_ref.dtype)
        return tuple(new_state)

    init = tuple((jnp.zeros((C_ROWS, QD), f32), jnp.zeros((1, LANES), f32)) for _ in range(H))
    lax.fori_loop(0, nt, tile_body, init, unroll=4)


def _mlstm(mqk, mvT, moT, smallT, conv_w, conv_b, gate_bias, mlstm_norm, B, S):
    nt = S // LANES
    gb = jnp.broadcast_to(gate_bias.astype(jnp.float32).reshape(2 * MLSTM_HEADS, 1), (2 * MLSTM_HEADS, LANES))
    norm_cols = jnp.broadcast_to(mlstm_norm.astype(jnp.float32).reshape(MLSTM_WIDTH, 1), (MLSTM_WIDTH, LANES))
    lane = np.arange(LANES)
    same_chunk = lane[:, None] // MLSTM_CHUNK == lane[None, :] // MLSTM_CHUNK
    tri = (same_chunk & (lane[:, None] <= lane[None, :])).astype(np.float32)
    last = (same_chunk & (lane[:, None] % MLSTM_CHUNK == MLSTM_CHUNK - 1)).astype(np.float32)
    seq = lambda w: pl.BlockSpec((S, w), lambda b: (b, 0))
    tiles = lambda rows: pl.BlockSpec((nt, rows, LANES), lambda b: (b, 0, 0))
    return pl.pallas_call(
        _mlstm_kernel,
        grid=(B,),
        in_specs=[seq(2 * MLSTM_QK_WIDTH), tiles(MLSTM_WIDTH), tiles(MLSTM_WIDTH), tiles(N_GATE_ROWS),
                  _full((2 * MLSTM_HEADS, LANES)), _full((LANES, LANES)), _full((LANES, LANES)),
                  _full((CONV_WIDTH, 2 * MLSTM_QK_WIDTH)), _full((1, 2 * MLSTM_QK_WIDTH)),
                  _full((MLSTM_WIDTH, LANES))],
        out_specs=seq(MLSTM_WIDTH),
        out_shape=jax.ShapeDtypeStruct((B * S, MLSTM_WIDTH), jnp.bfloat16),
        scratch_shapes=[pltpu.VMEM((S, 2 * MLSTM_QK_WIDTH), jnp.bfloat16),
                        pltpu.VMEM((5, nt * 2 * MLSTM_HEADS, LANES), jnp.float32)],
        compiler_params=_params(1),
        name="mlstm",
    )(mqk, mvT, moT, smallT, gb, jnp.asarray(tri), jnp.asarray(last), conv_w, conv_b.reshape(1, -1), norm_cols)


def _mem_kv_kernel(mem_ref, g_ref, w_ref, k_ref, v_ref):
    mn = _rms(mem_ref[...], g_ref[...]).astype(jnp.bfloat16)
    k_ref[...] = _mm(mn, w_ref[:, :D_MODEL]).astype(k_ref.dtype)
    v_ref[...] = _mm(mn, w_ref[:, D_MODEL:]).astype(v_ref.dtype)


def _mem_kv(mem, gain, w_xkv):
    B, M, _ = mem.shape
    spec = pl.BlockSpec((None, M, D_MODEL), lambda b: (b, 0, 0))
    return pl.pallas_call(
        _mem_kv_kernel,
        grid=(B,),
        in_specs=[spec, _full((1, D_MODEL)), _full((D_MODEL, 2 * D_MODEL))],
        out_specs=[spec, spec],
        out_shape=[jax.ShapeDtypeStruct((B, M, D_MODEL), jnp.bfloat16)] * 2,
        compiler_params=_params(1),
        name="mem_kv",
    )(mem, gain, w_xkv.astype(jnp.bfloat16))


TM_X = 1024
X_HALVES = 2


def _mix_xattn_kernel(ynsa_ref, yml_ref, x_ref, wout_ref, gpost_ref, gpre_ref, wq_ref, k_ref, v_ref,
                      wo_ref, gpost2_ref, out_ref):
    bf16 = jnp.bfloat16
    halves = [slice(i * (TM_X // X_HALVES), (i + 1) * (TM_X // X_HALVES)) for i in range(X_HALVES)]
    y = [_mm(ynsa_ref[r, :], wout_ref[:NSA_WIDTH, :]) + _mm(yml_ref[r, :], wout_ref[NSA_WIDTH:, :]) for r in halves]
    x1 = [x_ref[r, :] + _rms(y[i], gpost_ref[...]) for i, r in enumerate(halves)]
    h2 = [_rms(x1[i], gpre_ref[...]).astype(bf16) for i in range(X_HALVES)]
    q = [(_mm(h2[i], wq_ref[...]) * (XATTN_HEAD_DIM ** -0.5)).astype(bf16) for i in range(X_HALVES)]
    outs = [[] for _ in range(X_HALVES)]
    for hh in range(XATTN_HEADS):
        sl = slice(hh * XATTN_HEAD_DIM, (hh + 1) * XATTN_HEAD_DIM)
        s = [_nt(q[i][:, sl], k_ref[:, sl]) for i in range(X_HALVES)]
        for i in range(X_HALVES):
            p = jnp.exp(s[i] - jnp.max(s[i], axis=1, keepdims=True))
            l = jnp.sum(p, axis=1, keepdims=True)
            outs[i].append((_mm(p.astype(bf16), v_ref[:, sl]) * (1.0 / l)).astype(bf16))
    y2 = [_mm(jnp.concatenate(outs[i], axis=1), wo_ref[...]) for i in range(X_HALVES)]
    for i, r in enumerate(halves):
        out_ref[r, :] = x1[i] + _rms(y2[i], gpost2_ref[...])


def _mix_xattn(ynsa, yml, x2d, w_out, g_post, g_pre, w_xq, kx, vx, w_xo, g_post2, B, S):
    nt = S // TM_X
    M = kx.shape[1]
    tok = lambda w: pl.BlockSpec((TM_X, w), lambda b, i: (b * nt + i, 0))
    mem_spec = pl.BlockSpec((None, M, D_MODEL), lambda b, i: (b, 0, 0))
    sq = _full((D_MODEL, D_MODEL))
    row = _full((1, D_MODEL))
    bf = lambda w: w.astype(jnp.bfloat16)
    return pl.pallas_call(
        _mix_xattn_kernel,
        grid=(B, nt),
        in_specs=[tok(NSA_WIDTH), tok(MLSTM_WIDTH), tok(D_MODEL), sq, row, row, sq, mem_spec, mem_spec,
                  sq, row],
        out_specs=tok(D_MODEL),
        out_shape=jax.ShapeDtypeStruct((B * S, D_MODEL), jnp.float32),
        compiler_params=_params(2),
        name="mix_xattn",
    )(ynsa, yml, x2d, bf(w_out), g_post, g_pre, bf(w_xq), kx, vx, bf(w_xo), g_post2)


TM_F = 1024
F_HALVES = 2


def _ffn_kernel(x_ref, gpre_ref, wgu_ref, wd_ref, gpost_ref, out_ref, acc_ref):
    bf16 = jnp.bfloat16
    n_chunks = D_FF // F_TILE
    half = TM_F // F_HALVES

    def pre(i):
        return _rms(x_ref[i * half:(i + 1) * half, :], gpre_ref[...]).astype(bf16)

    def chunk(i, h, j):
        rows = slice(i * half, (i + 1) * half)
        cols = slice(j * F_TILE, (j + 1) * F_TILE)
        g = _mm(h, wgu_ref[:, cols])
        u = _mm(h, wgu_ref[:, D_FF + j * F_TILE:D_FF + (j + 1) * F_TILE])
        act = (g * jax.nn.sigmoid(g) * u).astype(bf16)
        down = _mm(act, wd_ref[cols, :])
        if j == 0:
            acc_ref[rows, :] = down
        else:
            acc_ref[rows, :] += down

    def post(i):
        rows = slice(i * half, (i + 1) * half)
        out_ref[rows, :] = x_ref[rows, :] + _rms(acc_ref[rows, :], gpost_ref[...])

    h = pre(0)
    for i in range(F_HALVES):
        chunk(i, h, 0)
        if i > 0:
            post(i - 1)
        h_next = pre(i + 1) if i + 1 < F_HALVES else None
        for j in range(1, n_chunks):
            chunk(i, h, j)
        h = h_next
    post(F_HALVES - 1)


def _ffn(x2d, g_pre, w_gate_up, w_down, g_post):
    T = x2d.shape[0]
    tok = pl.BlockSpec((TM_F, D_MODEL), lambda i: (i, 0))
    row = _full((1, D_MODEL))
    return pl.pallas_call(
        _ffn_kernel,
        grid=(T // TM_F,),
        in_specs=[tok, row, _full((D_MODEL, 2 * D_FF), buffers=1), _full((D_FF, D_MODEL), buffers=1), row],
        out_specs=tok,
        out_shape=jax.ShapeDtypeStruct((T, D_MODEL), jnp.float32),
        scratch_shapes=[pltpu.VMEM((TM_F, D_MODEL), jnp.float32)],
        compiler_params=_params(1),
        name="ffn",
    )(x2d, g_pre, w_gate_up.astype(jnp.bfloat16), w_down.astype(jnp.bfloat16), g_post)


def _layer(x, mem, rel_bias, mix_norm_pre, w_in, cmp_pos_k, cmp_pos_v, cmp_w1_k, cmp_w2_k, cmp_w1_v,
           cmp_w2_v, conv_w, conv_b, mlstm_gate_bias, mlstm_norm, w_out, mix_norm_post, xattn_norm_pre,
           mem_norm, w_xq, w_xkv, w_xo, xattn_norm_post, ffn_norm_pre, w_gate_up, w_down, ffn_norm_post):
    B, S, _ = x.shape
    row = lambda g: g.reshape(1, -1).astype(jnp.float32)
    x2d = x.reshape(B * S, D_MODEL)
    w_tok, w_feat = _in_proj_weights(w_in)
    (kc, vc, ksl, kwn, mqk, qT, vslT, vwnT, mvT, moT, smallT) = _in_proj(
        x2d, row(mix_norm_pre), w_tok, w_feat)
    kcmp, vcmpT = _compress(kc, vc, cmp_pos_k, cmp_pos_v, cmp_w1_k, cmp_w2_k, cmp_w1_v, cmp_w2_v, B, S)
    tables = _bias_tables(rel_bias.astype(jnp.float32))
    ynsa = _nsa(qT, kcmp, vcmpT, ksl, vslT, kwn, vwnT, smallT, tables, B, S)
    yml = _mlstm(mqk, mvT, moT, smallT, conv_w, conv_b, mlstm_gate_bias, mlstm_norm, B, S)
    kx, vx = _mem_kv(mem, row(mem_norm), w_xkv)
    x2 = _mix_xattn(ynsa, yml, x2d, w_out, row(mix_norm_post), row(xattn_norm_pre), w_xq, kx, vx, w_xo,
                    row(xattn_norm_post), B, S)
    x3 = _ffn(x2, row(ffn_norm_pre), w_gate_up, w_down, row(ffn_norm_post))
    return x3.reshape(B, S, D_MODEL)


def kernel(x, mem, rel_bias, mix_norm_pre, w_in, cmp_pos_k, cmp_pos_v, cmp_w1_k, cmp_w2_k, cmp_w1_v, cmp_w2_v,
           conv_w, conv_b, mlstm_gate_bias, mlstm_norm, w_out, mix_norm_post, xattn_norm_pre, mem_norm, w_xq,
           w_xkv, w_xo, xattn_norm_post, ffn_norm_pre, w_gate_up, w_down, ffn_norm_post):
    depth = w_in.shape[0]
    for l in range(depth):
        x = _layer(x, mem, rel_bias, mix_norm_pre[l], w_in[l], cmp_pos_k[l], cmp_pos_v[l], cmp_w1_k[l],
                   cmp_w2_k[l], cmp_w1_v[l], cmp_w2_v[l], conv_w[l], conv_b[l], mlstm_gate_bias[l],
                   mlstm_norm[l], w_out[l], mix_norm_post[l], xattn_norm_pre[l], mem_norm[l], w_xq[l],
                   w_xkv[l], w_xo[l], xattn_norm_post[l], ffn_norm_pre[l], w_gate_up[l], w_down[l],
                   ffn_norm_post[l])
    return x
```

```python
import functools
import math

import numpy as np
import jax
import jax.numpy as jnp
from jax import lax
from jax.experimental import pallas as pl
from jax.experimental.pallas import tpu as pltpu

D_MODEL = 1024
NSA_WIDTH = 512
NSA_HEAD_DIM = 64
NSA_HEADS = 8
NSA_KV_HEADS = 2
NSA_GROUP = 4
NSA_KV_WIDTH = 128
CMP_STRIDE = 16
CMP_BLOCK = 32
CMP_HIDDEN = 256
SEL_BLOCK = 64
N_SELECT = 16
WINDOW = 512
Q_BLOCK = 128
FORCED_SCORE = 1.0e4
MLSTM_WIDTH = 512
MLSTM_HEADS = 4
MLSTM_V_DIM = 128
MLSTM_QK_DIM = 64
MLSTM_QK_WIDTH = 256
MLSTM_CHUNK = 64
CONV_WIDTH = 4
REL_BUCKETS = 32
REL_MAX_DISTANCE = 128
XATTN_HEADS = 4
XATTN_HEAD_DIM = 256
D_FF = 2816
NORM_EPS = 1e-6
NEG_INF = -1.0e30
LOG2E = math.log2(math.e)

IN_SIZES = (NSA_WIDTH,) + (NSA_KV_WIDTH,) * 6 + (NSA_HEADS * 3, MLSTM_QK_WIDTH, MLSTM_QK_WIDTH,
                                                 MLSTM_WIDTH, MLSTM_HEADS, MLSTM_HEADS, MLSTM_WIDTH)
IN_OFFSETS = tuple(int(o) for o in np.cumsum((0,) + IN_SIZES)[:-1])

LANES = 128
SUBLANES = 8
BF16_ROWS = 16
VMEM_LIMIT_BYTES = 56 * 1024 * 1024

N_GATE_ROWS = 32
F_TILE = 256


def _rms(x, gain):
    return x * lax.rsqrt(jnp.mean(x * x, axis=-1, keepdims=True) + NORM_EPS) * gain


def _nt(a, b):
    return lax.dot_general(a, b, (((1,), (1,)), ((), ())), preferred_element_type=jnp.float32)


def _mm(a, b):
    return jnp.dot(a, b, preferred_element_type=jnp.float32)


def _params(n_axes, flags=None):
    return pltpu.CompilerParams(dimension_semantics=("arbitrary",) * n_axes,
                                vmem_limit_bytes=VMEM_LIMIT_BYTES, flags=flags)


def _full(shape, buffers=None):
    nd = len(shape)
    mode = None if buffers is None else pl.Buffered(buffers)
    return pl.BlockSpec(shape, lambda *_: (0,) * nd, pipeline_mode=mode)


TM_IN = 1024
IN_HALVES = 2
_TOK_GROUPS = (("kc", 128, jnp.float32), ("vc", 128, jnp.float32), ("ksl", 128, jnp.bfloat16),
               ("kwn", 128, jnp.bfloat16), ("mqk", 512, jnp.bfloat16))
_FEAT_GROUPS = (("qT", 512, jnp.bfloat16), ("vslT", 128, jnp.bfloat16), ("vwnT", 128, jnp.bfloat16),
                ("mvT", 512, jnp.bfloat16), ("moT", 512, jnp.bfloat16), ("smallT", N_GATE_ROWS, jnp.float32))


def _in_proj_kernel(x_ref, g_ref, wtok_ref, wfeat_ref, *out_refs):
    n_tok = len(_TOK_GROUPS)
    half = TM_IN // IN_HALVES

    def norm(i):
        return _rms(x_ref[i * half:(i + 1) * half, :], g_ref[...]).astype(jnp.bfloat16)

    def token_major(i, h):
        off = 0
        for (name, width, dt), o_ref in zip(_TOK_GROUPS, out_refs[:n_tok]):
            o_ref[i * half:(i + 1) * half, :] = _mm(h, wtok_ref[:, off:off + width]).astype(dt)
            off += width

    def feature_major(i, h):
        off = 0
        for (name, rows, dt), o_ref in zip(_FEAT_GROUPS, out_refs[n_tok:]):
            r = _nt(wfeat_ref[off:off + rows, :], h)
            if name == "qT":
                r = r * (NSA_HEAD_DIM ** -0.5 * LOG2E)
            for j in range(half // LANES):
                o_ref[i * (half // LANES) + j] = r[:, j * LANES:(j + 1) * LANES].astype(dt)
            off += rows

    h = norm(0)
    for i in range(IN_HALVES):
        token_major(i, h)
        h_next = norm(i + 1) if i + 1 < IN_HALVES else None
        feature_major(i, h)
        h = h_next


def _in_proj(x2d, gain, w_tok, w_feat):
    T = x2d.shape[0]
    n_tok_cols = w_tok.shape[1]
    n_feat_rows = w_feat.shape[0]
    out_shape, out_specs = [], []
    for name, width, dt in _TOK_GROUPS:
        out_shape.append(jax.ShapeDtypeStruct((T, width), dt))
        out_specs.append(pl.BlockSpec((TM_IN, width), lambda i: (i, 0)))
    for name, rows, dt in _FEAT_GROUPS:
        out_shape.append(jax.ShapeDtypeStruct((T // LANES, rows, LANES), dt))
        out_specs.append(pl.BlockSpec((TM_IN // LANES, rows, LANES), lambda i: (i, 0, 0)))
    return pl.pallas_call(
        _in_proj_kernel,
        grid=(T // TM_IN,),
        in_specs=[pl.BlockSpec((TM_IN, D_MODEL), lambda i: (i, 0)),
                  _full((1, D_MODEL)),
                  _full((D_MODEL, n_tok_cols)),
                  _full((n_feat_rows, D_MODEL))],
        out_specs=out_specs,
        out_shape=out_shape,
        compiler_params=_params(1),
        name="in_proj",
    )(x2d, gain, w_tok, w_feat)


def _in_proj_weights(w_in):
    (nq, kc, vc, ksl, vsl, kwn, vwn, gt, mq, mk, mv, mi, mf, mo) = [
        w_in[:, o:o + s] for o, s in zip(IN_OFFSETS, IN_SIZES)]
    gt_r = gt.reshape(D_MODEL, NSA_KV_HEADS, NSA_GROUP, 3).transpose(0, 3, 1, 2).reshape(D_MODEL, 24)
    small = jnp.concatenate([gt_r, mi, mf], axis=1)
    w_tok = jnp.concatenate([kc, vc, ksl, kwn, mq, mk], axis=1)
    w_feat = jnp.concatenate([nq, vsl, vwn, mv, mo, small], axis=1).T
    return w_tok.astype(jnp.bfloat16), w_feat.astype(jnp.bfloat16)


N_CHUNK_COLS = CMP_STRIDE * NSA_KV_WIDTH
N_HID2 = NSA_KV_HEADS * CMP_HIDDEN


def _compress_one(c, pos_ref, w1_ref, n_chunks):
    lo = _mm((c + pos_ref[0:1, :]).astype(jnp.bfloat16), w1_ref[0])
    hi = _mm((c + pos_ref[1:2, :]).astype(jnp.bfloat16), w1_ref[1])
    pre = lo + pltpu.roll(hi, n_chunks - 1, 0)
    return (pre * jax.nn.sigmoid(pre)).astype(jnp.bfloat16)


def _chunk_rows(ref, n_chunks):
    return jnp.concatenate([ref[pl.ds(t, n_chunks, stride=CMP_STRIDE), :] for t in range(CMP_STRIDE)], axis=1)


def _compress_kernel(kc_ref, vc_ref, posk_ref, posv_ref, w1k_ref, w1v_ref, w2k_ref, w2vT_ref,
                     kcmp_ref, vcmpT_ref):
    n_chunks = kc_ref.shape[0] // CMP_STRIDE
    hid_k = _compress_one(_chunk_rows(kc_ref, n_chunks), posk_ref, w1k_ref, n_chunks)
    kcmp = _mm(hid_k, w2k_ref[...])
    row = lax.broadcasted_iota(jnp.int32, kcmp.shape, 0)
    kcmp_ref[...] = jnp.where(row < n_chunks - 1, kcmp, 0.0).astype(kcmp_ref.dtype)
    hid_v = _compress_one(_chunk_rows(vc_ref, n_chunks), posv_ref, w1v_ref, n_chunks)
    vcmpT = _nt(w2vT_ref[...], hid_v)
    col = lax.broadcasted_iota(jnp.int32, vcmpT.shape, 1)
    vcmpT_ref[...] = jnp.where(col < n_chunks - 1, vcmpT, 0.0).astype(vcmpT_ref.dtype)


def _compress_weights(pos, w1, w2):
    eye = jnp.eye(NSA_KV_HEADS, dtype=w1.dtype)
    w1r = w1.reshape(2, CMP_STRIDE, NSA_HEAD_DIM, CMP_HIDDEN)
    w1e = jnp.einsum('atdj,hg->athdgj', w1r, eye).reshape(2, N_CHUNK_COLS, N_HID2)
    pos_e = jnp.broadcast_to(pos.reshape(2, CMP_STRIDE, 1, NSA_HEAD_DIM),
                             (2, CMP_STRIDE, NSA_KV_HEADS, NSA_HEAD_DIM)).reshape(2, N_CHUNK_COLS)
    w2e = jnp.einsum('jd,hg->hjgd', w2, eye).reshape(N_HID2, NSA_KV_WIDTH)
    return pos_e, w1e.astype(jnp.bfloat16), w2e.astype(jnp.bfloat16)


def _compress(kc, vc, cmp_pos_k, cmp_pos_v, cmp_w1_k, cmp_w2_k, cmp_w1_v, cmp_w2_v, B, S):
    n_chunks = S // CMP_STRIDE
    posk, w1k, w2k = _compress_weights(cmp_pos_k, cmp_w1_k, cmp_w2_k)
    posv, w1v, w2v = _compress_weights(cmp_pos_v, cmp_w1_v, cmp_w2_v)
    chunk_spec = pl.BlockSpec((S, NSA_KV_WIDTH), lambda b: (b, 0))
    return pl.pallas_call(
        _compress_kernel,
        grid=(B,),
        in_specs=[chunk_spec, chunk_spec,
                  _full((2, N_CHUNK_COLS)), _full((2, N_CHUNK_COLS)),
                  _full((2, N_CHUNK_COLS, N_HID2)), _full((2, N_CHUNK_COLS, N_HID2)),
                  _full((N_HID2, NSA_KV_WIDTH)), _full((NSA_KV_WIDTH, N_HID2))],
        out_specs=[pl.BlockSpec((None, n_chunks, NSA_KV_WIDTH), lambda b: (b, 0, 0)),
                   pl.BlockSpec((None, NSA_KV_WIDTH, n_chunks), lambda b: (b, 0, 0))],
        out_shape=[jax.ShapeDtypeStruct((B, n_chunks, NSA_KV_WIDTH), jnp.bfloat16),
                   jax.ShapeDtypeStruct((B, NSA_KV_WIDTH, n_chunks), jnp.bfloat16)],
        compiler_params=_params(1),
        name="compress",
    )(kc, vc, posk, posv, w1k, w1v, w2k, w2v.T)


GQ = NSA_GROUP * Q_BLOCK
TINY = 1e-30
CMP_TAB_ROWS = 512
CMP_TAB_ZERO = 248
CMP_TAB_LOOKUP = (232, 256)
SEL_STEP_SHIFT = 2
KEY_SUPER = Q_BLOCK << SEL_STEP_SHIFT
SEL_TAB_ZERO = KEY_SUPER + Q_BLOCK
SEL_TAB_ROWS = SEL_TAB_ZERO + KEY_SUPER
N_WIN_TILES = WINDOW // Q_BLOCK + 1
SEL_SUB_CHUNKS = 2
V_ROWS = NSA_HEAD_DIM + BF16_ROWS


def _bucket_np(dist):
    n = np.maximum(dist, 0)
    max_exact = REL_BUCKETS // 2
    nf = np.maximum(n, 1).astype(np.float64)
    large = max_exact + (np.log(nf / max_exact) / math.log(REL_MAX_DISTANCE / max_exact)
                         * (REL_BUCKETS - max_exact)).astype(np.int64)
    large = np.minimum(large, REL_BUCKETS - 1)
    return np.where(n < max_exact, n, large).astype(np.int32)


def _bias_index_tables():
    m = np.arange(Q_BLOCK)[:, None]
    r = np.arange(Q_BLOCK)[None, :]
    diag = np.where(r - m >= 0, _bucket_np(r - m), -1).astype(np.int32)
    off = _bucket_np(Q_BLOCK + r - m)
    jp = np.arange(*CMP_TAB_LOOKUP)[:, None] - CMP_TAB_ZERO
    d_c = r - CMP_STRIDE * jp - (CMP_BLOCK - 1)
    cmp_idx = np.where(d_c >= 0, _bucket_np(d_c), -1).astype(np.int32)
    return diag, off, cmp_idx


def _bias_tables_kernel(rb_ref, diag_idx_ref, off_idx_ref, cmp_idx_ref, sel_ref, win_ref, cmp_ref):
    f32 = jnp.float32

    def lookup(idx, head):
        far = rb_ref[head, REL_BUCKETS - 1]
        acc = jnp.full(idx.shape, NEG_INF, f32)
        for k in range(REL_BUCKETS):
            acc = jnp.where(idx == k, (rb_ref[head, k] - far) * LOG2E, acc)
        return acc

    m_io = lax.broadcasted_iota(jnp.int32, (Q_BLOCK, Q_BLOCK), 0)
    r_io = lax.broadcasted_iota(jnp.int32, (Q_BLOCK, Q_BLOCK), 1)
    neg_tile = jnp.full((Q_BLOCK, Q_BLOCK), NEG_INF, f32)
    lo, hi = CMP_TAB_LOOKUP
    for h in range(NSA_KV_HEADS):
        for g in range(NSA_GROUP):
            head = h * NSA_GROUP + g
            sl = slice(g * Q_BLOCK, (g + 1) * Q_BLOCK)
            far = 0.0
            far_tile = jnp.full((Q_BLOCK, Q_BLOCK), far, f32)
            diag_v = lookup(diag_idx_ref[...], head)
            off_v = lookup(off_idx_ref[...], head)
            n_far = (SEL_TAB_ZERO - Q_BLOCK) // Q_BLOCK
            for t in range(SEL_TAB_ROWS // Q_BLOCK):
                rows = slice(t * Q_BLOCK, (t + 1) * Q_BLOCK)
                tile = far_tile if t < n_far else off_v if t == n_far else diag_v if t == n_far + 1 else neg_tile
                sel_ref[h, rows, sl] = tile
            win_ref[h, 0, :, sl] = diag_v
            win_ref[h, 1, :, sl] = off_v
            for back in range(2, N_WIN_TILES - 1):
                win_ref[h, back, :, sl] = far_tile
            win_ref[h, N_WIN_TILES - 1, :, sl] = jnp.where(r_io < m_io, far, NEG_INF)
            win_ref[h, N_WIN_TILES, :, sl] = neg_tile
            cmp_ref[h, 0:lo, sl] = jnp.full((lo, Q_BLOCK), far, f32)
            cmp_ref[h, lo:hi, sl] = lookup(cmp_idx_ref[...], head)
            cmp_ref[h, hi:CMP_TAB_ROWS, sl] = jnp.full((CMP_TAB_ROWS - hi, Q_BLOCK), NEG_INF, f32)


def _bias_tables(rel_bias):
    diag_idx, off_idx, cmp_idx = _bias_index_tables()
    shapes = [(NSA_KV_HEADS, SEL_TAB_ROWS, GQ), (NSA_KV_HEADS, N_WIN_TILES + 1, Q_BLOCK, GQ),
              (NSA_KV_HEADS, CMP_TAB_ROWS, GQ)]
    return pl.pallas_call(
        _bias_tables_kernel,
        in_specs=[pl.BlockSpec(memory_space=pltpu.SMEM),
                  _full(diag_idx.shape), _full(off_idx.shape), _full(cmp_idx.shape)],
        out_specs=[_full(s) for s in shapes],
        out_shape=[jax.ShapeDtypeStruct(s, jnp.float32) for s in shapes],
        grid=(1,),
        compiler_params=_params(1),
        name="bias_tables",
    )(rel_bias, jnp.asarray(diag_idx), jnp.asarray(off_idx), jnp.asarray(cmp_idx))


def _overlap_np(n_cmp_rows, n_sel):
    cmp_start = np.arange(n_cmp_rows) * CMP_STRIDE
    cmp_end = cmp_start + CMP_BLOCK - 1
    sel_start = np.arange(n_sel) * SEL_BLOCK
    ov = ((cmp_start[None, :] <= sel_start[:, None] + SEL_BLOCK - 1)
          & (cmp_end[None, :] >= sel_start[:, None])).astype(np.float32)
    ov[:, n_cmp_rows - 1] = 0.0
    return ov


def _tile4(a):
    return jnp.concatenate([a] * NSA_GROUP, axis=1)


def _select_blocks(score, score_ref, n_top, hooks=()):
    n_sel = score.shape[0]
    score_ref[...] = score
    n_grp = n_sel // SUBLANES
    grp = [score[SUBLANES * v:SUBLANES * (v + 1), :] for v in range(n_grp)]
    cnt = [jnp.zeros((SUBLANES, Q_BLOCK), jnp.int32) for _ in range(n_grp)]
    sub_io = lax.broadcasted_iota(jnp.int32, (SUBLANES, Q_BLOCK), 0)
    hook_at = {(i * n_sel) // len(hooks): hk for i, hk in enumerate(hooks)} if hooks else {}
    for jp in range(n_sel):
        if jp in hook_at:
            hook_at[jp]()
        row = score_ref[jp:jp + 1, :]
        for v in range(n_grp):
            if SUBLANES * v > jp:
                inc = (row >= grp[v]).astype(jnp.int32)
            elif SUBLANES * (v + 1) - 1 < jp:
                inc = (row > grp[v]).astype(jnp.int32)
            else:
                tie = (sub_io > jp - SUBLANES * v).astype(jnp.int32)
                inc = jnp.where(row > grp[v], 1, jnp.where(row == grp[v], tie, 0))
            cnt[v] = cnt[v] + inc
    return [jnp.where(cnt[v] < n_top, 0.0, NEG_INF) for v in range(n_grp)]


def _nsa_kernel(q_ref, kcmp_ref, vcmpT_ref, ksl_ref, vslT_ref, kwn_ref, vwnT_ref, gate_ref,
                seltab_ref, wintab_ref, cmptab_ref, ovl_ref, blkind_ref, out_ref, score_ref, selb_ref, sbuf_ref,
                sbuf2_ref, swin_ref):
    c = pl.program_id(1)
    n_cmp = kcmp_ref.shape[0]
    n_sel = ovl_ref.shape[0]
    n_top = min(N_SELECT, n_sel)
    f32 = jnp.float32
    bf16 = jnp.bfloat16
    DH = NSA_HEAD_DIM
    heads = range(NSA_KV_HEADS)

    q = q_ref[...]
    zq = jnp.zeros((DH, GQ), bf16)
    qcat, qpad = [], []
    for h in heads:
        qcat.append(jnp.concatenate([q[(h * NSA_GROUP + g) * DH:(h * NSA_GROUP + g + 1) * DH, :]
                                     for g in range(NSA_GROUP)], axis=1))
        qpad.append(jnp.concatenate([qcat[h], zq] if h == 0 else [zq, qcat[h]], axis=0))

    backs = list(range(N_WIN_TILES))
    kts = [jnp.maximum(c - back, 0) for back in backs]
    slots = [jnp.where(c >= back, back, N_WIN_TILES) for back in backs]
    m_win = [jnp.full((1, GQ), NEG_INF, f32) for _ in heads]

    def win_score(back, h):
        key0 = pl.multiple_of(kts[back] * Q_BLOCK, Q_BLOCK)
        s = _mm(kwn_ref[pl.ds(key0, Q_BLOCK), :], qpad[h]) + wintab_ref[h, slots[back]]
        swin_ref[h, back * Q_BLOCK:(back + 1) * Q_BLOCK, :] = s
        return jnp.max(s, axis=0, keepdims=True)

    cmp_off = pl.multiple_of(CMP_TAB_ZERO - (Q_BLOCK // CMP_STRIDE) * c, SUBLANES)
    kcmp = kcmp_ref[...]
    j_io = lax.broadcasted_iota(jnp.int32, (n_sel, Q_BLOCK), 0)
    r_io = lax.broadcasted_iota(jnp.int32, (n_sel, Q_BLOCK), 1)
    cur = (Q_BLOCK // SEL_BLOCK) * c + (r_io >= SEL_BLOCK).astype(jnp.int32)
    forced = (j_io == 0) | (j_io == cur) | (j_io == cur - 1)
    visible = j_io <= cur
    o_c = []
    for h in heads:
        tab = cmptab_ref[h, pl.ds(cmp_off, n_cmp), :]
        s = _mm(kcmp, qpad[h]) + tab
        m = jnp.maximum(jnp.max(s, axis=0, keepdims=True), 0.1 * NEG_INF)
        p = jnp.exp2(s - m)
        l = jnp.sum(p, axis=0, keepdims=True)
        pn = p * (1.0 / jnp.maximum(l, TINY))
        o_c.append(_mm(vcmpT_ref[h * DH:(h + 1) * DH, :], pn.astype(bf16)))
        psum = pn[:, 0:Q_BLOCK]
        for g in range(1, NSA_GROUP):
            psum = psum + pn[:, g * Q_BLOCK:(g + 1) * Q_BLOCK]
        imp = jnp.dot(ovl_ref[...], psum, precision=lax.Precision.HIGHEST,
                      preferred_element_type=f32)
        score = jnp.where(forced, FORCED_SCORE, jnp.where(visible, imp, -1.0))

        def hook(back, h=h):
            m_win[h] = jnp.maximum(m_win[h], win_score(back, h))

        rows = _select_blocks(score, score_ref.at[h], n_top, hooks=[functools.partial(hook, back) for back in backs])
        for v, blk in enumerate(rows):
            selb_ref[h, SUBLANES * v:SUBLANES * (v + 1), :] = blk

    def ones_rows(n_keys):
        return jnp.ones((V_ROWS - DH, n_keys), bf16)

    acc_w = [jnp.zeros((V_ROWS, GQ), f32) for _ in heads]

    def win_value(back):
        for h in heads:
            p = jnp.exp2(swin_ref[h, back * Q_BLOCK:(back + 1) * Q_BLOCK, :] - m_win[h]).astype(bf16)
            vT = jnp.concatenate([vwnT_ref[kts[back], h * DH:(h + 1) * DH, :], ones_rows(Q_BLOCK)], axis=0)
            acc_w[h] = acc_w[h] + _mm(vT, p)

    blocks_per_step = KEY_SUPER // SEL_BLOCK
    tiles_per_step = KEY_SUPER // Q_BLOCK

    sub = KEY_SUPER // SEL_SUB_CHUNKS
    blocks_per_sub = sub // SEL_BLOCK
    tiles_per_sub = sub // Q_BLOCK

    own_lanes = [(lax.broadcasted_iota(jnp.int32, (sub, NSA_KV_WIDTH), 1) >= DH) == bool(h) for h in heads]

    def q_with_mask_rows(j, h):
        blk0 = pl.multiple_of(j * blocks_per_step, blocks_per_step)
        rows = jnp.concatenate([_tile4(selb_ref[h, pl.ds(blk0, blocks_per_step), :]),
                                jnp.zeros((DH - blocks_per_step, GQ), f32)], axis=0).astype(bf16)
        return jnp.concatenate([qcat[h], rows] if h == 0 else [rows, qcat[h]], axis=0)

    def score_chunk(j, ci, h, q_aug, buf_ref, near=True):
        key0 = pl.multiple_of(j * KEY_SUPER + ci * sub, sub)
        k_aug = jnp.where(own_lanes[h], ksl_ref[pl.ds(key0, sub), :], blkind_ref[h, ci * sub:(ci + 1) * sub, :])
        s = _mm(k_aug, q_aug)
        if near:
            tab_off = pl.multiple_of(
                jnp.maximum(j * KEY_SUPER - c * Q_BLOCK + SEL_TAB_ZERO, 0) + ci * sub, Q_BLOCK)
            s = s + seltab_ref[h, pl.ds(tab_off, sub), :]
        buf_ref[h, ci * sub:(ci + 1) * sub, :] = s
        return jnp.max(s, axis=0, keepdims=True)

    def value_chunk(j, ci, h, m_h, buf_ref):
        p = jnp.exp2(buf_ref[h, ci * sub:(ci + 1) * sub, :] - m_h).astype(bf16)
        vT = jnp.concatenate([vslT_ref[j * tiles_per_step + ci * tiles_per_sub + i, h * DH:(h + 1) * DH, :]
                              for i in range(tiles_per_sub)], axis=1)
        return _mm(jnp.concatenate([vT, ones_rows(sub)], axis=0), p)

    def values(j, m_old, m_cur, acc, src_ref, before_chunk=None):
        acc = [jnp.exp2(m_old[h] - m_cur[h]) * acc[h] for h in heads]
        for ci in range(SEL_SUB_CHUNKS):
            if before_chunk is not None:
                before_chunk(ci)
            for h in heads:
                acc[h] = acc[h] + value_chunk(j, ci, h, m_cur[h], src_ref)
        return tuple(acc)

    def sel_step(j, carry, src_ref, dst_ref, near=True):
        m_old, m_cur, acc = carry
        m_run = list(m_cur)
        q_aug = [q_with_mask_rows(j + 1, h) for h in heads]

        def scores(ci):
            for h in heads:
                m_run[h] = jnp.maximum(m_run[h], score_chunk(j + 1, ci, h, q_aug[h], dst_ref, near))

        acc = values(j, m_old, m_cur, acc, src_ref, before_chunk=scores)
        return m_cur, tuple(m_run), acc

    def sel_pair(i, carry, near):
        carry = sel_step(2 * i, carry, sbuf_ref, sbuf2_ref, near)
        return sel_step(2 * i + 1, carry, sbuf2_ref, sbuf_ref, near)

    n_steps = lax.shift_right_logical(c, SEL_STEP_SHIFT) + 1
    m_init = tuple(jnp.full((1, GQ), NEG_INF, f32) for _ in heads)
    acc_init = tuple(jnp.zeros((V_ROWS, GQ), f32) for _ in heads)
    m_first = list(m_init)
    win_order = list(backs)
    q_aug0 = [q_with_mask_rows(0, h) for h in heads]
    for ci in range(SEL_SUB_CHUNKS):
        for _ in range(-(-N_WIN_TILES // SEL_SUB_CHUNKS)):
            if win_order:
                win_value(win_order.pop(0))
        for h in heads:
            m_first[h] = jnp.maximum(m_first[h], score_chunk(0, ci, h, q_aug0[h], sbuf_ref))
    while win_order:
        win_value(win_order.pop(0))
    o_w = [a[0:DH, :] * (1.0 / a[DH:DH + 1, :]) for a in acc_w]
    n_piped = n_steps - 1
    odd = n_piped & 1
    n_far = lax.shift_right_logical(jnp.maximum(c - 1, 0), SEL_STEP_SHIFT)
    far_pairs = lax.shift_right_logical(jnp.maximum(n_far - 1, 0), 1)
    carry = lax.fori_loop(0, far_pairs, functools.partial(sel_pair, near=False),
                          (m_init, tuple(m_first), acc_init))
    carry = lax.fori_loop(far_pairs, lax.shift_right_logical(n_piped, 1), functools.partial(sel_pair, near=True),
                          carry)
    carry = lax.fori_loop(0, odd, lambda _, cr: sel_step(n_piped - 1, cr, sbuf_ref, sbuf2_ref), carry)
    m_old, m_cur, acc = carry
    acc = lax.fori_loop(0, 1 - odd, lambda _, a: values(n_piped, m_old, m_cur, a, sbuf_ref), acc)
    acc = lax.fori_loop(0, odd, lambda _, a: values(n_piped, m_old, m_cur, a, sbuf2_ref), acc)
    o_s = [a[0:DH, :] * (1.0 / jnp.maximum(a[DH:DH + 1, :], TINY)) for a in acc]

    for h in heads:
        ys = []
        for g in range(NSA_GROUP):
            sl = slice(g * Q_BLOCK, (g + 1) * Q_BLOCK)
            row0 = h * NSA_GROUP + g
            gates = [jax.nn.sigmoid(gate_ref[kind * NSA_HEADS + row0:kind * NSA_HEADS + row0 + 1, :])
                     for kind in range(3)]
            ys.append(gates[0] * o_c[h][:, sl] + gates[1] * o_s[h][:, sl] + gates[2] * o_w[h][:, sl])
        yT = jnp.concatenate(ys, axis=0)
        for half in range(2):
            col = (2 * h + half) * LANES
            out_ref[:, col:col + LANES] = yT[half * LANES:(half + 1) * LANES, :].T.astype(out_ref.dtype)


def _nsa(qT, kcmp, vcmpT, ksl, vslT, kwn, vwnT, smallT, tables, B, S):
    assert S % KEY_SUPER == 0
    nq = S // Q_BLOCK
    n_cmp = S // CMP_STRIDE
    n_sel = S // SEL_BLOCK
    seltab, wintab, cmptab = tables
    ovl = jnp.asarray(_overlap_np(n_cmp, n_sel))
    key_blk = np.arange(KEY_SUPER)[:, None] // SEL_BLOCK
    lane = np.arange(NSA_KV_WIDTH)[None, :]
    blkind = jnp.asarray(np.stack([lane - NSA_HEAD_DIM == key_blk, lane == key_blk]), jnp.bfloat16)
    ksl3 = ksl.reshape(B, S, NSA_KV_WIDTH)
    kwn3 = kwn.reshape(B, S, NSA_KV_WIDTH)
    vslT4 = vslT.reshape(B, nq, NSA_KV_WIDTH, Q_BLOCK)
    vwnT4 = vwnT.reshape(B, nq, NSA_KV_WIDTH, Q_BLOCK)
    k_spec = pl.BlockSpec((None, S, NSA_KV_WIDTH), lambda b, c: (b, 0, 0))
    vT_spec = pl.BlockSpec((None, nq, NSA_KV_WIDTH, Q_BLOCK), lambda b, c: (b, 0, 0, 0))
    const = lambda a: pl.BlockSpec(a.shape, lambda b, c: (0,) * a.ndim)
    return pl.pallas_call(
        _nsa_kernel,
        grid=(B, nq),
        in_specs=[pl.BlockSpec((None, NSA_WIDTH, Q_BLOCK), lambda b, c: (b * nq + c, 0, 0)),
                  pl.BlockSpec((None, n_cmp, NSA_KV_WIDTH), lambda b, c: (b, 0, 0)),
                  pl.BlockSpec((None, NSA_KV_WIDTH, n_cmp), lambda b, c: (b, 0, 0)),
                  k_spec, vT_spec, k_spec, vT_spec,
                  pl.BlockSpec((None, N_GATE_ROWS, Q_BLOCK), lambda b, c: (b * nq + c, 0, 0)),
                  const(seltab), const(wintab), const(cmptab), const(ovl), const(blkind)],
        out_specs=pl.BlockSpec((Q_BLOCK, NSA_WIDTH), lambda b, c: (b * nq + c, 0)),
        out_shape=jax.ShapeDtypeStruct((B * S, NSA_WIDTH), jnp.bfloat16),
        scratch_shapes=[pltpu.VMEM((NSA_KV_HEADS, n_sel, Q_BLOCK), jnp.float32),
                        pltpu.VMEM((NSA_KV_HEADS, n_sel, Q_BLOCK), jnp.float32),
                        pltpu.VMEM((NSA_KV_HEADS, KEY_SUPER, GQ), jnp.float32),
                        pltpu.VMEM((NSA_KV_HEADS, KEY_SUPER, GQ), jnp.float32),
                        pltpu.VMEM((NSA_KV_HEADS, N_WIN_TILES * Q_BLOCK, GQ), jnp.float32)],
        compiler_params=_params(2),
        name="nsa",
    )(qT, kcmp, vcmpT, ksl3, vslT4, kwn3, vwnT4, smallT, seltab, wintab, cmptab, ovl, blkind)


CONV_TILE = 256
I_ROW = 24
C_ROWS = MLSTM_V_DIM + BF16_ROWS


def _log_sigmoid(x):
    return jnp.minimum(x, 0.0) - jnp.log(1.0 + jnp.exp(-jnp.abs(x)))


def _mlstm_kernel(mqk_ref, mvT_ref, moT_ref, gates_ref, gb_ref, tri_ref, last_ref, convw_ref, convb_ref,
                  norm_ref, out_ref, qk_ref, rows_ref):
    S = mqk_ref.shape[0]
    nt = S // LANES
    L = MLSTM_CHUNK
    f32 = jnp.float32
    bf16 = jnp.bfloat16
    kscale_row = jnp.where(lax.broadcasted_iota(jnp.int32, (1, 2 * MLSTM_QK_WIDTH), 1) < MLSTM_QK_WIDTH,
                           1.0, MLSTM_QK_DIM ** -0.5)

    def conv_body(i, _):
        t0 = pl.multiple_of(i * CONV_TILE, CONV_TILE)
        cur = mqk_ref[pl.ds(t0, CONV_TILE), :].astype(f32)
        prev_start = pl.multiple_of(jnp.maximum(t0 - BF16_ROWS, 0), BF16_ROWS)
        prev = (mqk_ref[pl.ds(prev_start, BF16_ROWS), :].astype(f32)[BF16_ROWS - SUBLANES:, :]
                * jnp.where(i > 0, 1.0, 0.0))
        ext = jnp.concatenate([prev, cur], axis=0)
        y = convb_ref[...]
        for j in range(CONV_WIDTH):
            lo = SUBLANES - (CONV_WIDTH - 1) + j
            y = y + convw_ref[j:j + 1, :] * ext[lo:lo + CONV_TILE, :]
        y = y * jax.nn.sigmoid(y) * kscale_row
        qk_ref[pl.ds(t0, CONV_TILE), :] = y.astype(bf16)
        return 0

    lax.fori_loop(0, S // CONV_TILE, conv_body, 0)

    H = MLSTM_HEADS
    G8 = 2 * H
    n_rows = nt * G8
    a3 = gates_ref[:, I_ROW:I_ROW + G8, :] + gb_ref[...][None]
    is_f = lax.broadcasted_iota(jnp.int32, a3.shape, 1) >= H
    x = jnp.where(is_f, _log_sigmoid(a3), a3).reshape(n_rows, LANES)
    bcum = jnp.dot(x, tri_ref[...], precision=lax.Precision.HIGHEST, preferred_element_type=f32)
    b_rows = pltpu.roll(bcum, n_rows - H, 0)
    g_rows = x - b_rows
    pos = lax.broadcasted_iota(jnp.int32, (n_rows, LANES), 1) & (L - 1)

    def chunk_cummax(a):
        shift = 1
        while shift < L:
            a = jnp.where(pos >= shift, jnp.maximum(a, pltpu.roll(a, shift, 1)), a)
            shift *= 2
        return a

    def chunk_last(a):
        return jnp.dot(a, last_ref[...], precision=lax.Precision.HIGHEST, preferred_element_type=f32)

    bl_rows = chunk_last(b_rows)
    rows_ref[0] = g_rows
    rows_ref[1] = chunk_cummax(g_rows)
    rows_ref[2] = b_rows
    rows_ref[3] = bl_rows
    rows_ref[4] = chunk_last(chunk_cummax(bl_rows + g_rows))

    s_io = lax.broadcasted_iota(jnp.int32, (LANES, LANES), 0)
    t_io = lax.broadcasted_iota(jnp.int32, (LANES, LANES), 1)
    causal = (s_io <= t_io) & ((s_io >= L) == (t_io >= L))
    lane = lax.broadcasted_iota(jnp.int32, (1, LANES), 1)
    in_chunk = [lane < L, lane >= L]
    QD, VD = MLSTM_QK_DIM, MLSTM_V_DIM
    ones_aug = jnp.ones((C_ROWS - VD, LANES), bf16)
    head_lanes = [(lax.broadcasted_iota(jnp.int32, (LANES, LANES), 1) >= QD) == bool(par) for par in range(2)]

    def swap_halves(row):
        return pltpu.roll(row, L, 1)

    def tile_body(i, state):
        t0 = pl.multiple_of(i * LANES, LANES)
        r0 = pl.multiple_of(i * G8, G8)
        qk = qk_ref[pl.ds(t0, LANES), :]
        qk32 = qk.astype(f32)
        pairsT = [qk32[:, j * LANES:(j + 1) * LANES].T for j in range(4)]
        g8, cm8, b8, bl8, wm8 = [rows_ref[j, pl.ds(r0, G8), :] for j in range(5)]
        new_state = []
        for hh in range(H):
            pair, par = hh // 2, hh % 2
            g_r, cm_r, b_r, bl_r, wm_r = [a[hh:hh + 1, :] for a in (g8, cm8, b8, bl8, wm8)]
            caug, m_in = state[hh]
            qT = pairsT[pair][par * QD:(par + 1) * QD, :]
            kT = pairsT[2 + pair][par * QD:(par + 1) * QD, :]
            kpair = qk[:, (2 + pair) * LANES:(3 + pair) * LANES]
            qmask = jnp.where(head_lanes[par], qk[:, pair * LANES:(pair + 1) * LANES], jnp.zeros((), bf16))
            vaug = jnp.concatenate([mvT_ref[i, hh * VD:(hh + 1) * VD, :], ones_aug], axis=0)

            m_mid = swap_halves(jnp.maximum(bl_r + m_in, wm_r))
            m_prev = jnp.where(in_chunk[0], m_in, m_mid)
            m_next = jnp.maximum(bl_r + m_prev, wm_r)
            m_intra = b_r + cm_r
            m_inter = b_r + m_prev
            m_t = jnp.maximum(m_inter, m_intra)
            e_intra = jnp.exp(m_intra - m_t)
            e_inter = jnp.exp(m_inter - m_t)
            decay = jnp.exp(bl_r + m_prev - m_next)
            inject = jnp.exp(wm_r - m_next)

            g_mat = jnp.broadcast_to(g_r, (LANES, LANES)).T
            w = jnp.exp(jnp.where(causal, g_mat - cm_r, NEG_INF))
            st = _nt(kpair, qmask) * (w * e_intra)
            y = _mm(vaug, st.astype(bf16))
            kw = kT * jnp.exp(bl_r + g_r - wm_r)
            cs = caug
            for p in range(LANES // L):
                qs = jnp.where(in_chunk[p], qT * e_inter, 0.0).astype(bf16)
                y = y + _mm(cs.astype(bf16), qs)
                u = _nt(vaug, jnp.where(in_chunk[p], kw, 0.0).astype(bf16))
                dec = decay if p == 0 else swap_halves(decay)
                inj = inject if p == 0 else swap_halves(inject)
                cs = dec[:, 0:QD] * cs + inj[:, 0:QD] * u
            m_out = jnp.where(in_chunk[1], m_next, swap_halves(m_next))
            new_state.append((cs, m_out))

            den = y[VD:VD + 1, :]
            hT = y[0:VD, :] * (1.0 / jnp.maximum(jnp.abs(den), jnp.exp(-m_t)))
            sl = slice(hh * VD, (hh + 1) * VD)
            hn = hT * lax.rsqrt(jnp.mean(hT * hT, axis=0, keepdims=True) + NORM_EPS) * norm_ref[sl, :]
            yT = jax.nn.sigmoid(moT_ref[i, sl, :].astype(f32)) * hn
            out_ref[pl.ds(t0, LANES), sl] = yT.T.astype(out_ref.dtype)
        return tuple(new_state)

    init = tuple((jnp.zeros((C_ROWS, QD), f32), jnp.zeros((1, LANES), f32)) for _ in range(H))
    lax.fori_loop(0, nt, tile_body, init, unroll=4)


def _mlstm(mqk, mvT, moT, smallT, conv_w, conv_b, gate_bias, mlstm_norm, B, S):
    nt = S // LANES
    gb = jnp.broadcast_to(gate_bias.astype(jnp.float32).reshape(2 * MLSTM_HEADS, 1), (2 * MLSTM_HEADS, LANES))
    norm_cols = jnp.broadcast_to(mlstm_norm.astype(jnp.float32).reshape(MLSTM_WIDTH, 1), (MLSTM_WIDTH, LANES))
    lane = np.arange(LANES)
    same_chunk = lane[:, None] // MLSTM_CHUNK == lane[None, :] // MLSTM_CHUNK
    tri = (same_chunk & (lane[:, None] <= lane[None, :])).astype(np.float32)
    last = (same_chunk & (lane[:, None] % MLSTM_CHUNK == MLSTM_CHUNK - 1)).astype(np.float32)
    seq = lambda w: pl.BlockSpec((S, w), lambda b: (b, 0))
    tiles = lambda rows: pl.BlockSpec((nt, rows, LANES), lambda b: (b, 0, 0))
    return pl.pallas_call(
        _mlstm_kernel,
        grid=(B,),
        in_specs=[seq(2 * MLSTM_QK_WIDTH), tiles(MLSTM_WIDTH), tiles(MLSTM_WIDTH), tiles(N_GATE_ROWS),
                  _full((2 * MLSTM_HEADS, LANES)), _full((LANES, LANES)), _full((LANES, LANES)),
                  _full((CONV_WIDTH, 2 * MLSTM_QK_WIDTH)), _full((1, 2 * MLSTM_QK_WIDTH)),
                  _full((MLSTM_WIDTH, LANES))],
        out_specs=seq(MLSTM_WIDTH),
        out_shape=jax.ShapeDtypeStruct((B * S, MLSTM_WIDTH), jnp.bfloat16),
        scratch_shapes=[pltpu.VMEM((S, 2 * MLSTM_QK_WIDTH), jnp.bfloat16),
                        pltpu.VMEM((5, nt * 2 * MLSTM_HEADS, LANES), jnp.float32)],
        compiler_params=_params(1),
        name="mlstm",
    )(mqk, mvT, moT, smallT, gb, jnp.asarray(tri), jnp.asarray(last), conv_w, conv_b.reshape(1, -1), norm_cols)


def _mem_kv_kernel(mem_ref, g_ref, w_ref, k_ref, v_ref):
    mn = _rms(mem_ref[...], g_ref[...]).astype(jnp.bfloat16)
    k_ref[...] = _mm(mn, w_ref[:, :D_MODEL]).astype(k_ref.dtype)
    v_ref[...] = _mm(mn, w_ref[:, D_MODEL:]).astype(v_ref.dtype)


def _mem_kv(mem, gain, w_xkv):
    B, M, _ = mem.shape
    spec = pl.BlockSpec((None, M, D_MODEL), lambda b: (b, 0, 0))
    return pl.pallas_call(
        _mem_kv_kernel,
        grid=(B,),
        in_specs=[spec, _full((1, D_MODEL)), _full((D_MODEL, 2 * D_MODEL))],
        out_specs=[spec, spec],
        out_shape=[jax.ShapeDtypeStruct((B, M, D_MODEL), jnp.bfloat16)] * 2,
        compiler_params=_params(1),
        name="mem_kv",
    )(mem, gain, w_xkv.astype(jnp.bfloat16))


TM_X = 1024
X_HALVES = 2


def _mix_xattn_kernel(ynsa_ref, yml_ref, x_ref, wout_ref, gpost_ref, gpre_ref, wq_ref, k_ref, v_ref,
                      wo_ref, gpost2_ref, out_ref):
    bf16 = jnp.bfloat16
    halves = [slice(i * (TM_X // X_HALVES), (i + 1) * (TM_X // X_HALVES)) for i in range(X_HALVES)]
    y = [_mm(ynsa_ref[r, :], wout_ref[:NSA_WIDTH, :]) + _mm(yml_ref[r, :], wout_ref[NSA_WIDTH:, :]) for r in halves]
    x1 = [x_ref[r, :] + _rms(y[i], gpost_ref[...]) for i, r in enumerate(halves)]
    h2 = [_rms(x1[i], gpre_ref[...]).astype(bf16) for i in range(X_HALVES)]
    q = [(_mm(h2[i], wq_ref[...]) * (XATTN_HEAD_DIM ** -0.5)).astype(bf16) for i in range(X_HALVES)]
    outs = [[] for _ in range(X_HALVES)]
    for hh in range(XATTN_HEADS):
        sl = slice(hh * XATTN_HEAD_DIM, (hh + 1) * XATTN_HEAD_DIM)
        s = [_nt(q[i][:, sl], k_ref[:, sl]) for i in range(X_HALVES)]
        for i in range(X_HALVES):
            p = jnp.exp(s[i] - jnp.max(s[i], axis=1, keepdims=True))
            l = jnp.sum(p, axis=1, keepdims=True)
            outs[i].append((_mm(p.astype(bf16), v_ref[:, sl]) * (1.0 / l)).astype(bf16))
    y2 = [_mm(jnp.concatenate(outs[i], axis=1), wo_ref[...]) for i in range(X_HALVES)]
    for i, r in enumerate(halves):
        out_ref[r, :] = x1[i] + _rms(y2[i], gpost2_ref[...])


def _mix_xattn(ynsa, yml, x2d, w_out, g_post, g_pre, w_xq, kx, vx, w_xo, g_post2, B, S):
    nt = S // TM_X
    M = kx.shape[1]
    tok = lambda w: pl.BlockSpec((TM_X, w), lambda b, i: (b * nt + i, 0))
    mem_spec = pl.BlockSpec((None, M, D_MODEL), lambda b, i: (b, 0, 0))
    sq = _full((D_MODEL, D_MODEL))
    row = _full((1, D_MODEL))
    bf = lambda w: w.astype(jnp.bfloat16)
    return pl.pallas_call(
        _mix_xattn_kernel,
        grid=(B, nt),
        in_specs=[tok(NSA_WIDTH), tok(MLSTM_WIDTH), tok(D_MODEL), sq, row, row, sq, mem_spec, mem_spec,
                  sq, row],
        out_specs=tok(D_MODEL),
        out_shape=jax.ShapeDtypeStruct((B * S, D_MODEL), jnp.float32),
        compiler_params=_params(2),
        name="mix_xattn",
    )(ynsa, yml, x2d, bf(w_out), g_post, g_pre, bf(w_xq), kx, vx, bf(w_xo), g_post2)


TM_F = 1024
F_HALVES = 2


def _ffn_kernel(x_ref, gpre_ref, wgu_ref, wd_ref, gpost_ref, out_ref, acc_ref):
    bf16 = jnp.bfloat16
    n_chunks = D_FF // F_TILE
    half = TM_F // F_HALVES

    def pre(i):
        return _rms(x_ref[i * half:(i + 1) * half, :], gpre_ref[...]).astype(bf16)

    def chunk(i, h, j):
        rows = slice(i * half, (i + 1) * half)
        cols = slice(j * F_TILE, (j + 1) * F_TILE)
        g = _mm(h, wgu_ref[:, cols])
        u = _mm(h, wgu_ref[:, D_FF + j * F_TILE:D_FF + (j + 1) * F_TILE])
        act = (g * jax.nn.sigmoid(g) * u).astype(bf16)
        down = _mm(act, wd_ref[cols, :])
        if j == 0:
            acc_ref[rows, :] = down
        else:
            acc_ref[rows, :] += down

    def post(i):
        rows = slice(i * half, (i + 1) * half)
        out_ref[rows, :] = x_ref[rows, :] + _rms(acc_ref[rows, :], gpost_ref[...])

    h = pre(0)
    for i in range(F_HALVES):
        chunk(i, h, 0)
        if i > 0:
            post(i - 1)
        h_next = pre(i + 1) if i + 1 < F_HALVES else None
        for j in range(1, n_chunks):
            chunk(i, h, j)
        h = h_next
    post(F_HALVES - 1)


def _ffn(x2d, g_pre, w_gate_up, w_down, g_post):
    T = x2d.shape[0]
    tok = pl.BlockSpec((TM_F, D_MODEL), lambda i: (i, 0))
    row = _full((1, D_MODEL))
    return pl.pallas_call(
        _ffn_kernel,
        grid=(T // TM_F,),
        in_specs=[tok, row, _full((D_MODEL, 2 * D_FF), buffers=1), _full((D_FF, D_MODEL), buffers=1), row],
        out_specs=tok,
        out_shape=jax.ShapeDtypeStruct((T, D_MODEL), jnp.float32),
        scratch_shapes=[pltpu.VMEM((TM_F, D_MODEL), jnp.float32)],
        compiler_params=_params(1),
        name="ffn",
    )(x2d, g_pre, w_gate_up.astype(jnp.bfloat16), w_down.astype(jnp.bfloat16), g_post)


def _layer(x, mem, rel_bias, mix_norm_pre, w_in, cmp_pos_k, cmp_pos_v, cmp_w1_k, cmp_w2_k, cmp_w1_v,
           cmp_w2_v, conv_w, conv_b, mlstm_gate_bias, mlstm_norm, w_out, mix_norm_post, xattn_norm_pre,
           mem_norm, w_xq, w_xkv, w_xo, xattn_norm_post, ffn_norm_pre, w_gate_up, w_down, ffn_norm_post):
    B, S, _ = x.shape
    row = lambda g: g.reshape(1, -1).astype(jnp.float32)
    x2d = x.reshape(B * S, D_MODEL)
    w_tok, w_feat = _in_proj_weights(w_in)
    (kc, vc, ksl, kwn, mqk, qT, vslT, vwnT, mvT, moT, smallT) = _in_proj(
        x2d, row(mix_norm_pre), w_tok, w_feat)
    kcmp, vcmpT = _compress(kc, vc, cmp_pos_k, cmp_pos_v, cmp_w1_k, cmp_w2_k, cmp_w1_v, cmp_w2_v, B, S)
    tables = _bias_tables(rel_bias.astype(jnp.float32))
    ynsa = _nsa(qT, kcmp, vcmpT, ksl, vslT, kwn, vwnT, smallT, tables, B, S)
    yml = _mlstm(mqk, mvT, moT, smallT, conv_w, conv_b, mlstm_gate_bias, mlstm_norm, B, S)
    kx, vx = _mem_kv(mem, row(mem_norm), w_xkv)
    x2 = _mix_xattn(ynsa, yml, x2d, w_out, row(mix_norm_post), row(xattn_norm_pre), w_xq, kx, vx, w_xo,
                    row(xattn_norm_post), B, S)
    x3 = _ffn(x2, row(ffn_norm_pre), w_gate_up, w_down, row(ffn_norm_post))
    return x3.reshape(B, S, D_MODEL)


def kernel(x, mem, rel_bias, mix_norm_pre, w_in, cmp_pos_k, cmp_pos_v, cmp_w1_k, cmp_w2_k, cmp_w1_v, cmp_w2_v,
           conv_w, conv_b, mlstm_gate_bias, mlstm_norm, w_out, mix_norm_post, xattn_norm_pre, mem_norm, w_xq,
           w_xkv, w_xo, xattn_norm_post, ffn_norm_pre, w_gate_up, w_down, ffn_norm_post):
    depth = w_in.shape[0]
    for l in range(depth):
        x = _layer(x, mem, rel_bias, mix_norm_pre[l], w_in[l], cmp_pos_k[l], cmp_pos_v[l], cmp_w1_k[l],
                   cmp_w2_k[l], cmp_w1_v[l], cmp_w2_v[l], conv_w[l], conv_b[l], mlstm_gate_bias[l],
                   mlstm_norm[l], w_out[l], mix_norm_post[l], xattn_norm_pre[l], mem_norm[l], w_xq[l],
                   w_xkv[l], w_xo[l], xattn_norm_post[l], ffn_norm_pre[l], w_gate_up[l], w_down[l],
                   ffn_norm_post[l])
    return x
```

```python
import functools
import math

import numpy as np
import jax
import jax.numpy as jnp
from jax import lax
from jax.experimental import pallas as pl
from jax.experimental.pallas import tpu as pltpu

D_MODEL = 1024
NSA_WIDTH = 512
NSA_HEAD_DIM = 64
NSA_HEADS = 8
NSA_KV_HEADS = 2
NSA_GROUP = 4
NSA_KV_WIDTH = 128
CMP_STRIDE = 16
CMP_BLOCK = 32
CMP_HIDDEN = 256
SEL_BLOCK = 64
N_SELECT = 16
WINDOW = 512
Q_BLOCK = 128
FORCED_SCORE = 1.0e4
MLSTM_WIDTH = 512
MLSTM_HEADS = 4
MLSTM_V_DIM = 128
MLSTM_QK_DIM = 64
MLSTM_QK_WIDTH = 256
MLSTM_CHUNK = 64
CONV_WIDTH = 4
REL_BUCKETS = 32
REL_MAX_DISTANCE = 128
XATTN_HEADS = 4
XATTN_HEAD_DIM = 256
D_FF = 2816
NORM_EPS = 1e-6
NEG_INF = -1.0e30
LOG2E = math.log2(math.e)

IN_SIZES = (NSA_WIDTH,) + (NSA_KV_WIDTH,) * 6 + (NSA_HEADS * 3, MLSTM_QK_WIDTH, MLSTM_QK_WIDTH,
                                                 MLSTM_WIDTH, MLSTM_HEADS, MLSTM_HEADS, MLSTM_WIDTH)
IN_OFFSETS = tuple(int(o) for o in np.cumsum((0,) + IN_SIZES)[:-1])

LANES = 128
SUBLANES = 8
BF16_ROWS = 16
VMEM_LIMIT_BYTES = 56 * 1024 * 1024

N_GATE_ROWS = 32
F_TILE = 256


def _rms(x, gain):
    return x * lax.rsqrt(jnp.mean(x * x, axis=-1, keepdims=True) + NORM_EPS) * gain


def _nt(a, b):
    return lax.dot_general(a, b, (((1,), (1,)), ((), ())), preferred_element_type=jnp.float32)


def _mm(a, b):
    return jnp.dot(a, b, preferred_element_type=jnp.float32)


def _params(n_axes, flags=None):
    return pltpu.CompilerParams(dimension_semantics=("arbitrary",) * n_axes,
                                vmem_limit_bytes=VMEM_LIMIT_BYTES, flags=flags)


def _full(shape, buffers=None):
    nd = len(shape)
    mode = None if buffers is None else pl.Buffered(buffers)
    return pl.BlockSpec(shape, lambda *_: (0,) * nd, pipeline_mode=mode)


TM_IN = 1024
IN_HALVES = 2
TOK_DOT_WIDTH = 512
_TOK_GROUPS = (("kc", 128, jnp.float32), ("vc", 128, jnp.float32), ("ksl", 128, jnp.bfloat16),
               ("kwn", 128, jnp.bfloat16), ("mqk", 512, jnp.bfloat16))
_FEAT_GROUPS = (("qT", 512, jnp.bfloat16), ("vslT", 128, jnp.bfloat16), ("vwnT", 128, jnp.bfloat16),
                ("mvT", 512, jnp.bfloat16), ("moT", 512, jnp.bfloat16), ("smallT", N_GATE_ROWS, jnp.float32))


def _in_proj_kernel(x_ref, g_ref, wtok_ref, wfeat_ref, *out_refs):
    n_tok = len(_TOK_GROUPS)
    half = TM_IN // IN_HALVES

    def norm(i):
        return _rms(x_ref[i * half:(i + 1) * half, :], g_ref[...]).astype(jnp.bfloat16)

    def token_major(i, h):
        n_cols = sum(width for _, width, _ in _TOK_GROUPS)
        res = [_mm(h, wtok_ref[:, c0:c0 + TOK_DOT_WIDTH]) for c0 in range(0, n_cols, TOK_DOT_WIDTH)]
        off = 0
        for (name, width, dt), o_ref in zip(_TOK_GROUPS, out_refs[:n_tok]):
            r = res[off // TOK_DOT_WIDTH]
            lo = off % TOK_DOT_WIDTH
            o_ref[i * half:(i + 1) * half, :] = r[:, lo:lo + width].astype(dt)
            off += width

    def feature_major(i, h):
        off = 0
        for (name, rows, dt), o_ref in zip(_FEAT_GROUPS, out_refs[n_tok:]):
            r = _nt(wfeat_ref[off:off + rows, :], h)
            if name == "qT":
                r = r * (NSA_HEAD_DIM ** -0.5 * LOG2E)
            for j in range(half // LANES):
                o_ref[i * (half // LANES) + j] = r[:, j * LANES:(j + 1) * LANES].astype(dt)
            off += rows

    h = norm(0)
    for i in range(IN_HALVES):
        token_major(i, h)
        h_next = norm(i + 1) if i + 1 < IN_HALVES else None
        feature_major(i, h)
        h = h_next


def _in_proj(x2d, gain, w_tok, w_feat):
    T = x2d.shape[0]
    n_tok_cols = w_tok.shape[1]
    n_feat_rows = w_feat.shape[0]
    out_shape, out_specs = [], []
    for name, width, dt in _TOK_GROUPS:
        out_shape.append(jax.ShapeDtypeStruct((T, width), dt))
        out_specs.append(pl.BlockSpec((TM_IN, width), lambda i: (i, 0)))
    for name, rows, dt in _FEAT_GROUPS:
        out_shape.append(jax.ShapeDtypeStruct((T // LANES, rows, LANES), dt))
        out_specs.append(pl.BlockSpec((TM_IN // LANES, rows, LANES), lambda i: (i, 0, 0)))
    return pl.pallas_call(
        _in_proj_kernel,
        grid=(T // TM_IN,),
        in_specs=[pl.BlockSpec((TM_IN, D_MODEL), lambda i: (i, 0)),
                  _full((1, D_MODEL)),
                  _full((D_MODEL, n_tok_cols)),
                  _full((n_feat_rows, D_MODEL))],
        out_specs=out_specs,
        out_shape=out_shape,
        compiler_params=_params(1),
        name="in_proj",
    )(x2d, gain, w_tok, w_feat)


def _in_proj_weights(w_in):
    (nq, kc, vc, ksl, vsl, kwn, vwn, gt, mq, mk, mv, mi, mf, mo) = [
        w_in[:, o:o + s] for o, s in zip(IN_OFFSETS, IN_SIZES)]
    gt_r = gt.reshape(D_MODEL, NSA_KV_HEADS, NSA_GROUP, 3).transpose(0, 3, 1, 2).reshape(D_MODEL, 24)
    small = jnp.concatenate([gt_r, mi, mf], axis=1)
    w_tok = jnp.concatenate([kc, vc, ksl, kwn, mq, mk], axis=1)
    w_feat = jnp.concatenate([nq, vsl, vwn, mv, mo, small], axis=1).T
    return w_tok.astype(jnp.bfloat16), w_feat.astype(jnp.bfloat16)


N_CHUNK_COLS = CMP_STRIDE * NSA_KV_WIDTH
N_HID2 = NSA_KV_HEADS * CMP_HIDDEN


def _compress_one(c, pos_ref, w1_ref, n_chunks):
    lo = _mm((c + pos_ref[0:1, :]).astype(jnp.bfloat16), w1_ref[0])
    hi = _mm((c + pos_ref[1:2, :]).astype(jnp.bfloat16), w1_ref[1])
    pre = lo + pltpu.roll(hi, n_chunks - 1, 0)
    return (pre * jax.nn.sigmoid(pre)).astype(jnp.bfloat16)


def _chunk_rows(ref, n_chunks):
    return jnp.concatenate([ref[pl.ds(t, n_chunks, stride=CMP_STRIDE), :] for t in range(CMP_STRIDE)], axis=1)


def _compress_kernel(kc_ref, vc_ref, posk_ref, posv_ref, w1k_ref, w1v_ref, w2k_ref, w2vT_ref,
                     kcmp_ref, vcmpT_ref):
    n_chunks = kc_ref.shape[0] // CMP_STRIDE
    hid_k = _compress_one(_chunk_rows(kc_ref, n_chunks), posk_ref, w1k_ref, n_chunks)
    kcmp = _mm(hid_k, w2k_ref[...])
    row = lax.broadcasted_iota(jnp.int32, kcmp.shape, 0)
    kcmp_ref[...] = jnp.where(row < n_chunks - 1, kcmp, 0.0).astype(kcmp_ref.dtype)
    hid_v = _compress_one(_chunk_rows(vc_ref, n_chunks), posv_ref, w1v_ref, n_chunks)
    vcmpT = _nt(w2vT_ref[...], hid_v)
    col = lax.broadcasted_iota(jnp.int32, vcmpT.shape, 1)
    vcmpT_ref[...] = jnp.where(col < n_chunks - 1, vcmpT, 0.0).astype(vcmpT_ref.dtype)


def _compress_weights(pos, w1, w2):
    eye = jnp.eye(NSA_KV_HEADS, dtype=w1.dtype)
    w1r = w1.reshape(2, CMP_STRIDE, NSA_HEAD_DIM, CMP_HIDDEN)
    w1e = jnp.einsum('atdj,hg->athdgj', w1r, eye).reshape(2, N_CHUNK_COLS, N_HID2)
    pos_e = jnp.broadcast_to(pos.reshape(2, CMP_STRIDE, 1, NSA_HEAD_DIM),
                             (2, CMP_STRIDE, NSA_KV_HEADS, NSA_HEAD_DIM)).reshape(2, N_CHUNK_COLS)
    w2e = jnp.einsum('jd,hg->hjgd', w2, eye).reshape(N_HID2, NSA_KV_WIDTH)
    return pos_e, w1e.astype(jnp.bfloat16), w2e.astype(jnp.bfloat16)


def _compress(kc, vc, cmp_pos_k, cmp_pos_v, cmp_w1_k, cmp_w2_k, cmp_w1_v, cmp_w2_v, B, S):
    n_chunks = S // CMP_STRIDE
    posk, w1k, w2k = _compress_weights(cmp_pos_k, cmp_w1_k, cmp_w2_k)
    posv, w1v, w2v = _compress_weights(cmp_pos_v, cmp_w1_v, cmp_w2_v)
    chunk_spec = pl.BlockSpec((S, NSA_KV_WIDTH), lambda b: (b, 0))
    return pl.pallas_call(
        _compress_kernel,
        grid=(B,),
        in_specs=[chunk_spec, chunk_spec,
                  _full((2, N_CHUNK_COLS)), _full((2, N_CHUNK_COLS)),
                  _full((2, N_CHUNK_COLS, N_HID2)), _full((2, N_CHUNK_COLS, N_HID2)),
                  _full((N_HID2, NSA_KV_WIDTH)), _full((NSA_KV_WIDTH, N_HID2))],
        out_specs=[pl.BlockSpec((None, n_chunks, NSA_KV_WIDTH), lambda b: (b, 0, 0)),
                   pl.BlockSpec((None, NSA_KV_WIDTH, n_chunks), lambda b: (b, 0, 0))],
        out_shape=[jax.ShapeDtypeStruct((B, n_chunks, NSA_KV_WIDTH), jnp.bfloat16),
                   jax.ShapeDtypeStruct((B, NSA_KV_WIDTH, n_chunks), jnp.bfloat16)],
        compiler_params=_params(1),
        name="compress",
    )(kc, vc, posk, posv, w1k, w1v, w2k, w2v.T)


GQ = NSA_GROUP * Q_BLOCK
TINY = 1e-30
CMP_TAB_ROWS = 512
CMP_TAB_ZERO = 248
CMP_TAB_LOOKUP = (232, 256)
SEL_STEP_SHIFT = 2
KEY_SUPER = Q_BLOCK << SEL_STEP_SHIFT
SEL_TAB_ZERO = KEY_SUPER + Q_BLOCK
SEL_TAB_ROWS = SEL_TAB_ZERO + KEY_SUPER
N_WIN_TILES = WINDOW // Q_BLOCK + 1
SEL_SUB_CHUNKS = 2
V_ROWS = NSA_HEAD_DIM + BF16_ROWS


def _bucket_np(dist):
    n = np.maximum(dist, 0)
    max_exact = REL_BUCKETS // 2
    nf = np.maximum(n, 1).astype(np.float64)
    large = max_exact + (np.log(nf / max_exact) / math.log(REL_MAX_DISTANCE / max_exact)
                         * (REL_BUCKETS - max_exact)).astype(np.int64)
    large = np.minimum(large, REL_BUCKETS - 1)
    return np.where(n < max_exact, n, large).astype(np.int32)


def _bias_index_tables():
    m = np.arange(Q_BLOCK)[:, None]
    r = np.arange(Q_BLOCK)[None, :]
    diag = np.where(r - m >= 0, _bucket_np(r - m), -1).astype(np.int32)
    off = _bucket_np(Q_BLOCK + r - m)
    jp = np.arange(*CMP_TAB_LOOKUP)[:, None] - CMP_TAB_ZERO
    d_c = r - CMP_STRIDE * jp - (CMP_BLOCK - 1)
    cmp_idx = np.where(d_c >= 0, _bucket_np(d_c), -1).astype(np.int32)
    return diag, off, cmp_idx


def _bias_tables_kernel(rb_ref, diag_idx_ref, off_idx_ref, cmp_idx_ref, sel_ref, win_ref, cmp_ref):
    f32 = jnp.float32

    def lookup(idx, head):
        far = rb_ref[head, REL_BUCKETS - 1]
        acc = jnp.full(idx.shape, NEG_INF, f32)
        for k in range(REL_BUCKETS):
            acc = jnp.where(idx == k, (rb_ref[head, k] - far) * LOG2E, acc)
        return acc

    m_io = lax.broadcasted_iota(jnp.int32, (Q_BLOCK, Q_BLOCK), 0)
    r_io = lax.broadcasted_iota(jnp.int32, (Q_BLOCK, Q_BLOCK), 1)
    neg_tile = jnp.full((Q_BLOCK, Q_BLOCK), NEG_INF, f32)
    lo, hi = CMP_TAB_LOOKUP
    for h in range(NSA_KV_HEADS):
        for g in range(NSA_GROUP):
            head = h * NSA_GROUP + g
            sl = slice(g * Q_BLOCK, (g + 1) * Q_BLOCK)
            far = 0.0
            far_tile = jnp.full((Q_BLOCK, Q_BLOCK), far, f32)
            diag_v = lookup(diag_idx_ref[...], head)
            off_v = lookup(off_idx_ref[...], head)
            n_far = (SEL_TAB_ZERO - Q_BLOCK) // Q_BLOCK
            for t in range(SEL_TAB_ROWS // Q_BLOCK):
                rows = slice(t * Q_BLOCK, (t + 1) * Q_BLOCK)
                tile = far_tile if t < n_far else off_v if t == n_far else diag_v if t == n_far + 1 else neg_tile
                sel_ref[h, rows, sl] = tile
            win_ref[h, 0, :, sl] = diag_v
            win_ref[h, 1, :, sl] = off_v
            for back in range(2, N_WIN_TILES - 1):
                win_ref[h, back, :, sl] = far_tile
            win_ref[h, N_WIN_TILES - 1, :, sl] = jnp.where(r_io < m_io, far, NEG_INF)
            win_ref[h, N_WIN_TILES, :, sl] = neg_tile
            cmp_ref[h, 0:lo, sl] = jnp.full((lo, Q_BLOCK), far, f32)
            cmp_ref[h, lo:hi, sl] = lookup(cmp_idx_ref[...], head)
            cmp_ref[h, hi:CMP_TAB_ROWS, sl] = jnp.full((CMP_TAB_ROWS - hi, Q_BLOCK), NEG_INF, f32)


def _bias_tables(rel_bias):
    diag_idx, off_idx, cmp_idx = _bias_index_tables()
    shapes = [(NSA_KV_HEADS, SEL_TAB_ROWS, GQ), (NSA_KV_HEADS, N_WIN_TILES + 1, Q_BLOCK, GQ),
              (NSA_KV_HEADS, CMP_TAB_ROWS, GQ)]
    return pl.pallas_call(
        _bias_tables_kernel,
        in_specs=[pl.BlockSpec(memory_space=pltpu.SMEM),
                  _full(diag_idx.shape), _full(off_idx.shape), _full(cmp_idx.shape)],
        out_specs=[_full(s) for s in shapes],
        out_shape=[jax.ShapeDtypeStruct(s, jnp.float32) for s in shapes],
        grid=(1,),
        compiler_params=_params(1),
        name="bias_tables",
    )(rel_bias, jnp.asarray(diag_idx), jnp.asarray(off_idx), jnp.asarray(cmp_idx))


def _overlap_np(n_cmp_rows, n_sel):
    cmp_start = np.arange(n_cmp_rows) * CMP_STRIDE
    cmp_end = cmp_start + CMP_BLOCK - 1
    sel_start = np.arange(n_sel) * SEL_BLOCK
    ov = ((cmp_start[None, :] <= sel_start[:, None] + SEL_BLOCK - 1)
          & (cmp_end[None, :] >= sel_start[:, None])).astype(np.float32)
    ov[:, n_cmp_rows - 1] = 0.0
    return ov


def _tile4(a):
    return jnp.concatenate([a] * NSA_GROUP, axis=1)


def _select_blocks(score, score_ref, n_top, hooks=()):
    n_sel = score.shape[0]
    score_ref[...] = score
    n_grp = n_sel // SUBLANES
    grp = [score[SUBLANES * v:SUBLANES * (v + 1), :] for v in range(n_grp)]
    cnt = [jnp.zeros((SUBLANES, Q_BLOCK), jnp.int32) for _ in range(n_grp)]
    sub_io = lax.broadcasted_iota(jnp.int32, (SUBLANES, Q_BLOCK), 0)
    hook_at = {(i * n_sel) // len(hooks): hk for i, hk in enumerate(hooks)} if hooks else {}
    for jp in range(n_sel):
        if jp in hook_at:
            hook_at[jp]()
        row = score_ref[jp:jp + 1, :]
        for v in range(n_grp):
            if SUBLANES * v > jp:
                inc = (row >= grp[v]).astype(jnp.int32)
            elif SUBLANES * (v + 1) - 1 < jp:
                inc = (row > grp[v]).astype(jnp.int32)
            else:
                tie = (sub_io > jp - SUBLANES * v).astype(jnp.int32)
                inc = jnp.where(row > grp[v], 1, jnp.where(row == grp[v], tie, 0))
            cnt[v] = cnt[v] + inc
    return [jnp.where(cnt[v] < n_top, 0.0, NEG_INF) for v in range(n_grp)]


def _nsa_kernel(q_ref, kcmp_ref, vcmpT_ref, ksl_ref, vslT_ref, kwn_ref, vwnT_ref, gate_ref,
                seltab_ref, wintab_ref, cmptab_ref, ovl_ref, blkind_ref, out_ref, score_ref, selb_ref, sbuf_ref,
                sbuf2_ref, swin_ref):
    c = pl.program_id(1)
    n_cmp = kcmp_ref.shape[0]
    n_sel = ovl_ref.shape[0]
    n_top = min(N_SELECT, n_sel)
    f32 = jnp.float32
    bf16 = jnp.bfloat16
    DH = NSA_HEAD_DIM
    heads = range(NSA_KV_HEADS)

    q = q_ref[...]
    zq = jnp.zeros((DH, GQ), bf16)
    qcat, qpad = [], []
    for h in heads:
        qcat.append(jnp.concatenate([q[(h * NSA_GROUP + g) * DH:(h * NSA_GROUP + g + 1) * DH, :]
                                     for g in range(NSA_GROUP)], axis=1))
        qpad.append(jnp.concatenate([qcat[h], zq] if h == 0 else [zq, qcat[h]], axis=0))

    backs = list(range(N_WIN_TILES))
    kts = [jnp.maximum(c - back, 0) for back in backs]
    slots = [jnp.where(c >= back, back, N_WIN_TILES) for back in backs]
    m_win = [jnp.full((1, GQ), NEG_INF, f32) for _ in heads]

    def win_score(back, h):
        key0 = pl.multiple_of(kts[back] * Q_BLOCK, Q_BLOCK)
        s = _mm(kwn_ref[pl.ds(key0, Q_BLOCK), :], qpad[h]) + wintab_ref[h, slots[back]]
        swin_ref[h, back * Q_BLOCK:(back + 1) * Q_BLOCK, :] = s
        return jnp.max(s, axis=0, keepdims=True)

    cmp_off = pl.multiple_of(CMP_TAB_ZERO - (Q_BLOCK // CMP_STRIDE) * c, SUBLANES)
    kcmp = kcmp_ref[...]
    j_io = lax.broadcasted_iota(jnp.int32, (n_sel, Q_BLOCK), 0)
    r_io = lax.broadcasted_iota(jnp.int32, (n_sel, Q_BLOCK), 1)
    cur = (Q_BLOCK // SEL_BLOCK) * c + (r_io >= SEL_BLOCK).astype(jnp.int32)
    forced = (j_io == 0) | (j_io == cur) | (j_io == cur - 1)
    visible = j_io <= cur
    o_c = []
    for h in heads:
        tab = cmptab_ref[h, pl.ds(cmp_off, n_cmp), :]
        s = _mm(kcmp, qpad[h]) + tab
        m = jnp.maximum(jnp.max(s, axis=0, keepdims=True), 0.1 * NEG_INF)
        p = jnp.exp2(s - m)
        l = jnp.sum(p, axis=0, keepdims=True)
        pn = p * (1.0 / jnp.maximum(l, TINY))
        o_c.append(_mm(vcmpT_ref[h * DH:(h + 1) * DH, :], pn.astype(bf16)))
        psum = pn[:, 0:Q_BLOCK]
        for g in range(1, NSA_GROUP):
            psum = psum + pn[:, g * Q_BLOCK:(g + 1) * Q_BLOCK]
        imp = jnp.dot(ovl_ref[...], psum, precision=lax.Precision.HIGHEST,
                      preferred_element_type=f32)
        score = jnp.where(forced, FORCED_SCORE, jnp.where(visible, imp, -1.0))

        def hook(back, h=h):
            m_win[h] = jnp.maximum(m_win[h], win_score(back, h))

        rows = _select_blocks(score, score_ref.at[h], n_top, hooks=[functools.partial(hook, back) for back in backs])
        for v, blk in enumerate(rows):
            selb_ref[h, SUBLANES * v:SUBLANES * (v + 1), :] = blk

    def ones_rows(n_keys):
        return jnp.ones((V_ROWS - DH, n_keys), bf16)

    acc_w = [jnp.zeros((V_ROWS, GQ), f32) for _ in heads]

    def win_value(back):
        for h in heads:
            p = jnp.exp2(swin_ref[h, back * Q_BLOCK:(back + 1) * Q_BLOCK, :] - m_win[h]).astype(bf16)
            vT = jnp.concatenate([vwnT_ref[kts[back], h * DH:(h + 1) * DH, :], ones_rows(Q_BLOCK)], axis=0)
            acc_w[h] = acc_w[h] + _mm(vT, p)

    blocks_per_step = KEY_SUPER // SEL_BLOCK
    tiles_per_step = KEY_SUPER // Q_BLOCK

    sub = KEY_SUPER // SEL_SUB_CHUNKS
    blocks_per_sub = sub // SEL_BLOCK
    tiles_per_sub = sub // Q_BLOCK

    own_lanes = [(lax.broadcasted_iota(jnp.int32, (sub, NSA_KV_WIDTH), 1) >= DH) == bool(h) for h in heads]

    def q_with_mask_rows(j, h):
        blk0 = pl.multiple_of(j * blocks_per_step, blocks_per_step)
        rows = jnp.concatenate([_tile4(selb_ref[h, pl.ds(blk0, blocks_per_step), :]),
                                jnp.zeros((DH - blocks_per_step, GQ), f32)], axis=0).astype(bf16)
        return jnp.concatenate([qcat[h], rows] if h == 0 else [rows, qcat[h]], axis=0)

    def score_chunk(j, ci, h, q_aug, buf_ref, near=True):
        key0 = pl.multiple_of(j * KEY_SUPER + ci * sub, sub)
        k_aug = jnp.where(own_lanes[h], ksl_ref[pl.ds(key0, sub), :], blkind_ref[h, ci * sub:(ci + 1) * sub, :])
        s = _mm(k_aug, q_aug)
        if near:
            tab_off = pl.multiple_of(
                jnp.maximum(j * KEY_SUPER - c * Q_BLOCK + SEL_TAB_ZERO, 0) + ci * sub, Q_BLOCK)
            s = s + seltab_ref[h, pl.ds(tab_off, sub), :]
        buf_ref[h, ci * sub:(ci + 1) * sub, :] = s
        return jnp.max(s, axis=0, keepdims=True)

    def value_chunk(j, ci, h, m_h, buf_ref):
        p = jnp.exp2(buf_ref[h, ci * sub:(ci + 1) * sub, :] - m_h).astype(bf16)
        vT = jnp.concatenate([vslT_ref[j * tiles_per_step + ci * tiles_per_sub + i, h * DH:(h + 1) * DH, :]
                              for i in range(tiles_per_sub)], axis=1)
        return _mm(jnp.concatenate([vT, ones_rows(sub)], axis=0), p)

    def values(j, m_old, m_cur, acc, src_ref, before_chunk=None):
        acc = [jnp.exp2(m_old[h] - m_cur[h]) * acc[h] for h in heads]
        for ci in range(SEL_SUB_CHUNKS):
            if before_chunk is not None:
                before_chunk(ci)
            for h in heads:
                acc[h] = acc[h] + value_chunk(j, ci, h, m_cur[h], src_ref)
        return tuple(acc)

    def sel_step(j, carry, src_ref, dst_ref, near=True):
        m_old, m_cur, acc = carry
        m_run = list(m_cur)
        q_aug = [q_with_mask_rows(j + 1, h) for h in heads]

        def scores(ci):
            for h in heads:
                m_run[h] = jnp.maximum(m_run[h], score_chunk(j + 1, ci, h, q_aug[h], dst_ref, near))

        acc = values(j, m_old, m_cur, acc, src_ref, before_chunk=scores)
        return m_cur, tuple(m_run), acc

    def sel_pair(i, carry, near):
        carry = sel_step(2 * i, carry, sbuf_ref, sbuf2_ref, near)
        return sel_step(2 * i + 1, carry, sbuf2_ref, sbuf_ref, near)

    n_steps = lax.shift_right_logical(c, SEL_STEP_SHIFT) + 1
    m_init = tuple(jnp.full((1, GQ), NEG_INF, f32) for _ in heads)
    acc_init = tuple(jnp.zeros((V_ROWS, GQ), f32) for _ in heads)
    m_first = list(m_init)
    win_order = list(backs)
    q_aug0 = [q_with_mask_rows(0, h) for h in heads]
    for ci in range(SEL_SUB_CHUNKS):
        for _ in range(-(-N_WIN_TILES // SEL_SUB_CHUNKS)):
            if win_order:
                win_value(win_order.pop(0))
        for h in heads:
            m_first[h] = jnp.maximum(m_first[h], score_chunk(0, ci, h, q_aug0[h], sbuf_ref))
    while win_order:
        win_value(win_order.pop(0))
    o_w = [a[0:DH, :] * (1.0 / a[DH:DH + 1, :]) for a in acc_w]
    n_piped = n_steps - 1
    odd = n_piped & 1
    n_far = lax.shift_right_logical(jnp.maximum(c - 1, 0), SEL_STEP_SHIFT)
    far_pairs = lax.shift_right_logical(jnp.maximum(n_far - 1, 0), 1)
    carry = lax.fori_loop(0, far_pairs, functools.partial(sel_pair, near=False),
                          (m_init, tuple(m_first), acc_init))
    carry = lax.fori_loop(far_pairs, lax.shift_right_logical(n_piped, 1), functools.partial(sel_pair, near=True),
                          carry)
    carry = lax.fori_loop(0, odd, lambda _, cr: sel_step(n_piped - 1, cr, sbuf_ref, sbuf2_ref), carry)
    m_old, m_cur, acc = carry
    acc = lax.fori_loop(0, 1 - odd, lambda _, a: values(n_piped, m_old, m_cur, a, sbuf_ref), acc)
    acc = lax.fori_loop(0, odd, lambda _, a: values(n_piped, m_old, m_cur, a, sbuf2_ref), acc)
    o_s = [a[0:DH, :] * (1.0 / jnp.maximum(a[DH:DH + 1, :], TINY)) for a in acc]

    for h in heads:
        ys = []
        for g in range(NSA_GROUP):
            sl = slice(g * Q_BLOCK, (g + 1) * Q_BLOCK)
            row0 = h * NSA_GROUP + g
            gates = [jax.nn.sigmoid(gate_ref[kind * NSA_HEADS + row0:kind * NSA_HEADS + row0 + 1, :])
                     for kind in range(3)]
            ys.append(gates[0] * o_c[h][:, sl] + gates[1] * o_s[h][:, sl] + gates[2] * o_w[h][:, sl])
        yT = jnp.concatenate(ys, axis=0)
        for half in range(2):
            col = (2 * h + half) * LANES
            out_ref[:, col:col + LANES] = yT[half * LANES:(half + 1) * LANES, :].T.astype(out_ref.dtype)


def _nsa(qT, kcmp, vcmpT, ksl, vslT, kwn, vwnT, smallT, tables, B, S):
    assert S % KEY_SUPER == 0
    nq = S // Q_BLOCK
    n_cmp = S // CMP_STRIDE
    n_sel = S // SEL_BLOCK
    seltab, wintab, cmptab = tables
    ovl = jnp.asarray(_overlap_np(n_cmp, n_sel))
    key_blk = np.arange(KEY_SUPER)[:, None] // SEL_BLOCK
    lane = np.arange(NSA_KV_WIDTH)[None, :]
    blkind = jnp.asarray(np.stack([lane - NSA_HEAD_DIM == key_blk, lane == key_blk]), jnp.bfloat16)
    ksl3 = ksl.reshape(B, S, NSA_KV_WIDTH)
    kwn3 = kwn.reshape(B, S, NSA_KV_WIDTH)
    vslT4 = vslT.reshape(B, nq, NSA_KV_WIDTH, Q_BLOCK)
    vwnT4 = vwnT.reshape(B, nq, NSA_KV_WIDTH, Q_BLOCK)
    k_spec = pl.BlockSpec((None, S, NSA_KV_WIDTH), lambda b, c: (b, 0, 0))
    vT_spec = pl.BlockSpec((None, nq, NSA_KV_WIDTH, Q_BLOCK), lambda b, c: (b, 0, 0, 0))
    const = lambda a: pl.BlockSpec(a.shape, lambda b, c: (0,) * a.ndim)
    return pl.pallas_call(
        _nsa_kernel,
        grid=(B, nq),
        in_specs=[pl.BlockSpec((None, NSA_WIDTH, Q_BLOCK), lambda b, c: (b * nq + c, 0, 0)),
                  pl.BlockSpec((None, n_cmp, NSA_KV_WIDTH), lambda b, c: (b, 0, 0)),
                  pl.BlockSpec((None, NSA_KV_WIDTH, n_cmp), lambda b, c: (b, 0, 0)),
                  k_spec, vT_spec, k_spec, vT_spec,
                  pl.BlockSpec((None, N_GATE_ROWS, Q_BLOCK), lambda b, c: (b * nq + c, 0, 0)),
                  const(seltab), const(wintab), const(cmptab), const(ovl), const(blkind)],
        out_specs=pl.BlockSpec((Q_BLOCK, NSA_WIDTH), lambda b, c: (b * nq + c, 0)),
        out_shape=jax.ShapeDtypeStruct((B * S, NSA_WIDTH), jnp.bfloat16),
        scratch_shapes=[pltpu.VMEM((NSA_KV_HEADS, n_sel, Q_BLOCK), jnp.float32),
                        pltpu.VMEM((NSA_KV_HEADS, n_sel, Q_BLOCK), jnp.float32),
                        pltpu.VMEM((NSA_KV_HEADS, KEY_SUPER, GQ), jnp.float32),
                        pltpu.VMEM((NSA_KV_HEADS, KEY_SUPER, GQ), jnp.float32),
                        pltpu.VMEM((NSA_KV_HEADS, N_WIN_TILES * Q_BLOCK, GQ), jnp.float32)],
        compiler_params=_params(2),
        name="nsa",
    )(qT, kcmp, vcmpT, ksl3, vslT4, kwn3, vwnT4, smallT, seltab, wintab, cmptab, ovl, blkind)


CONV_TILE = 256
I_ROW = 24
C_ROWS = MLSTM_V_DIM + BF16_ROWS


def _log_sigmoid(x):
    return jnp.minimum(x, 0.0) - jnp.log(1.0 + jnp.exp(-jnp.abs(x)))


def _mlstm_kernel(mqk_ref, mvT_ref, moT_ref, gates_ref, gb_ref, tri_ref, last_ref, convw_ref, convb_ref,
                  norm_ref, out_ref, qk_ref, rows_ref):
    S = mqk_ref.shape[0]
    nt = S // LANES
    L = MLSTM_CHUNK
    f32 = jnp.float32
    bf16 = jnp.bfloat16
    kscale_row = jnp.where(lax.broadcasted_iota(jnp.int32, (1, 2 * MLSTM_QK_WIDTH), 1) < MLSTM_QK_WIDTH,
                           1.0, MLSTM_QK_DIM ** -0.5)

    def conv_body(i, _):
        t0 = pl.multiple_of(i * CONV_TILE, CONV_TILE)
        cur = mqk_ref[pl.ds(t0, CONV_TILE), :].astype(f32)
        prev_start = pl.multiple_of(jnp.maximum(t0 - BF16_ROWS, 0), BF16_ROWS)
        prev = (mqk_ref[pl.ds(prev_start, BF16_ROWS), :].astype(f32)[BF16_ROWS - SUBLANES:, :]
                * jnp.where(i > 0, 1.0, 0.0))
        ext = jnp.concatenate([prev, cur], axis=0)
        y = convb_ref[...]
        for j in range(CONV_WIDTH):
            lo = SUBLANES - (CONV_WIDTH - 1) + j
            y = y + convw_ref[j:j + 1, :] * ext[lo:lo + CONV_TILE, :]
        y = y * jax.nn.sigmoid(y) * kscale_row
        qk_ref[pl.ds(t0, CONV_TILE), :] = y.astype(bf16)
        return 0

    lax.fori_loop(0, S // CONV_TILE, conv_body, 0)

    H = MLSTM_HEADS
    G8 = 2 * H
    n_rows = nt * G8
    a3 = gates_ref[:, I_ROW:I_ROW + G8, :] + gb_ref[...][None]
    is_f = lax.broadcasted_iota(jnp.int32, a3.shape, 1) >= H
    x = jnp.where(is_f, _log_sigmoid(a3), a3).reshape(n_rows, LANES)
    bcum = jnp.dot(x, tri_ref[...], precision=lax.Precision.HIGHEST, preferred_element_type=f32)
    b_rows = pltpu.roll(bcum, n_rows - H, 0)
    g_rows = x - b_rows
    pos = lax.broadcasted_iota(jnp.int32, (n_rows, LANES), 1) & (L - 1)

    def chunk_cummax(a):
        shift = 1
        while shift < L:
            a = jnp.where(pos >= shift, jnp.maximum(a, pltpu.roll(a, shift, 1)), a)
            shift *= 2
        return a

    def chunk_last(a):
        return jnp.dot(a, last_ref[...], precision=lax.Precision.HIGHEST, preferred_element_type=f32)

    bl_rows = chunk_last(b_rows)
    rows_ref[0] = g_rows
    rows_ref[1] = chunk_cummax(g_rows)
    rows_ref[2] = b_rows
    rows_ref[3] = bl_rows
    rows_ref[4] = chunk_last(chunk_cummax(bl_rows + g_rows))

    s_io = lax.broadcasted_iota(jnp.int32, (LANES, LANES), 0)
    t_io = lax.broadcasted_iota(jnp.int32, (LANES, LANES), 1)
    causal = (s_io <= t_io) & ((s_io >= L) == (t_io >= L))
    lane = lax.broadcasted_iota(jnp.int32, (1, LANES), 1)
    in_chunk = [lane < L, lane >= L]
    QD, VD = MLSTM_QK_DIM, MLSTM_V_DIM
    ones_aug = jnp.ones((C_ROWS - VD, LANES), bf16)
    head_lanes = [(lax.broadcasted_iota(jnp.int32, (LANES, LANES), 1) >= QD) == bool(par) for par in range(2)]

    def swap_halves(row):
        return pltpu.roll(row, L, 1)

    def tile_body(i, state):
        t0 = pl.multiple_of(i * LANES, LANES)
        r0 = pl.multiple_of(i * G8, G8)
        qk = qk_ref[pl.ds(t0, LANES), :]
        qk32 = qk.astype(f32)
        pairsT = [qk32[:, j * LANES:(j + 1) * LANES].T for j in range(4)]
        g8, cm8, b8, bl8, wm8 = [rows_ref[j, pl.ds(r0, G8), :] for j in range(5)]
        new_state = []
        for hh in range(H):
            pair, par = hh // 2, hh % 2
            g_r, cm_r, b_r, bl_r, wm_r = [a[hh:hh + 1, :] for a in (g8, cm8, b8, bl8, wm8)]
            caug, m_in = state[hh]
            qT = pairsT[pair][par * QD:(par + 1) * QD, :]
            kT = pairsT[2 + pair][par * QD:(par + 1) * QD, :]
            kpair = qk[:, (2 + pair) * LANES:(3 + pair) * LANES]
            qmask = jnp.where(head_lanes[par], qk[:, pair * LANES:(pair + 1) * LANES], jnp.zeros((), bf16))
            vaug = jnp.concatenate([mvT_ref[i, hh * VD:(hh + 1) * VD, :], ones_aug], axis=0)

            m_mid = swap_halves(jnp.maximum(bl_r + m_in, wm_r))
            m_prev = jnp.where(in_chunk[0], m_in, m_mid)
            m_next = jnp.maximum(bl_r + m_prev, wm_r)
            m_intra = b_r + cm_r
            m_inter = b_r + m_prev
            m_t = jnp.maximum(m_inter, m_intra)
            e_intra = jnp.exp(m_intra - m_t)
            e_inter = jnp.exp(m_inter - m_t)
            decay = jnp.exp(bl_r + m_prev - m_next)
            inject = jnp.exp(wm_r - m_next)

            g_mat = jnp.broadcast_to(g_r, (LANES, LANES)).T
            w = jnp.exp(jnp.where(causal, g_mat - cm_r, NEG_INF))
            st = _nt(kpair, qmask) * (w * e_intra)
            y = _mm(vaug, st.astype(bf16))
            kw = kT * jnp.exp(bl_r + g_r - wm_r)
            cs = caug
            for p in range(LANES // L):
                qs = jnp.where(in_chunk[p], qT * e_inter, 0.0).astype(bf16)
                y = y + _mm(cs.astype(bf16), qs)
                u = _nt(vaug, jnp.where(in_chunk[p], kw, 0.0).astype(bf16))
                dec = decay if p == 0 else swap_halves(decay)
                inj = inject if p == 0 else swap_halves(inject)
                cs = dec[:, 0:QD] * cs + inj[:, 0:QD] * u
            m_out = jnp.where(in_chunk[1], m_next, swap_halves(m_next))
            new_state.append((cs, m_out))

            den = y[VD:VD + 1, :]
            hT = y[0:VD, :] * (1.0 / jnp.maximum(jnp.abs(den), jnp.exp(-m_t)))
            sl = slice(hh * VD, (hh + 1) * VD)
            hn = hT * lax.rsqrt(jnp.mean(hT * hT, axis=0, keepdims=True) + NORM_EPS) * norm_ref[sl, :]
            yT = jax.nn.sigmoid(moT_ref[i, sl, :].astype(f32)) * hn
            out_ref[pl.ds(t0, LANES), sl] = yT.T.astype(out_ref.dtype)
        return tuple(new_state)

    init = tuple((jnp.zeros((C_ROWS, QD), f32), jnp.zeros((1, LANES), f32)) for _ in range(H))
    lax.fori_loop(0, nt, tile_body, init, unroll=4)


def _mlstm(mqk, mvT, moT, smallT, conv_w, conv_b, gate_bias, mlstm_norm, B, S):
    nt = S // LANES
    gb = jnp.broadcast_to(gate_bias.astype(jnp.float32).reshape(2 * MLSTM_HEADS, 1), (2 * MLSTM_HEADS, LANES))
    norm_cols = jnp.broadcast_to(mlstm_norm.astype(jnp.float32).reshape(MLSTM_WIDTH, 1), (MLSTM_WIDTH, LANES))
    lane = np.arange(LANES)
    same_chunk = lane[:, None] // MLSTM_CHUNK == lane[None, :] // MLSTM_CHUNK
    tri = (same_chunk & (lane[:, None] <= lane[None, :])).astype(np.float32)
    last = (same_chunk & (lane[:, None] % MLSTM_CHUNK == MLSTM_CHUNK - 1)).astype(np.float32)
    seq = lambda w: pl.BlockSpec((S, w), lambda b: (b, 0))
    tiles = lambda rows: pl.BlockSpec((nt, rows, LANES), lambda b: (b, 0, 0))
    return pl.pallas_call(
        _mlstm_kernel,
        grid=(B,),
        in_specs=[seq(2 * MLSTM_QK_WIDTH), tiles(MLSTM_WIDTH), tiles(MLSTM_WIDTH), tiles(N_GATE_ROWS),
                  _full((2 * MLSTM_HEADS, LANES)), _full((LANES, LANES)), _full((LANES, LANES)),
                  _full((CONV_WIDTH, 2 * MLSTM_QK_WIDTH)), _full((1, 2 * MLSTM_QK_WIDTH)),
                  _full((MLSTM_WIDTH, LANES))],
        out_specs=seq(MLSTM_WIDTH),
        out_shape=jax.ShapeDtypeStruct((B * S, MLSTM_WIDTH), jnp.bfloat16),
        scratch_shapes=[pltpu.VMEM((S, 2 * MLSTM_QK_WIDTH), jnp.bfloat16),
                        pltpu.VMEM((5, nt * 2 * MLSTM_HEADS, LANES), jnp.float32)],
        compiler_params=_params(1),
        name="mlstm",
    )(mqk, mvT, moT, smallT, gb, jnp.asarray(tri), jnp.asarray(last), conv_w, conv_b.reshape(1, -1), norm_cols)


def _mem_kv_kernel(mem_ref, g_ref, w_ref, k_ref, v_ref):
    mn = _rms(mem_ref[...], g_ref[...]).astype(jnp.bfloat16)
    k_ref[...] = _mm(mn, w_ref[:, :D_MODEL]).astype(k_ref.dtype)
    v_ref[...] = _mm(mn, w_ref[:, D_MODEL:]).astype(v_ref.dtype)


def _mem_kv(mem, gain, w_xkv):
    B, M, _ = mem.shape
    spec = pl.BlockSpec((None, M, D_MODEL), lambda b: (b, 0, 0))
    return pl.pallas_call(
        _mem_kv_kernel,
        grid=(B,),
        in_specs=[spec, _full((1, D_MODEL)), _full((D_MODEL, 2 * D_MODEL))],
        out_specs=[spec, spec],
        out_shape=[jax.ShapeDtypeStruct((B, M, D_MODEL), jnp.bfloat16)] * 2,
        compiler_params=_params(1),
        name="mem_kv",
    )(mem, gain, w_xkv.astype(jnp.bfloat16))


TM_X = 1024
X_HALVES = 2


def _mix_xattn_kernel(ynsa_ref, yml_ref, x_ref, wout_ref, gpost_ref, gpre_ref, wq_ref, k_ref, v_ref,
                      wo_ref, gpost2_ref, out_ref):
    bf16 = jnp.bfloat16
    halves = [slice(i * (TM_X // X_HALVES), (i + 1) * (TM_X // X_HALVES)) for i in range(X_HALVES)]
    y = [_mm(ynsa_ref[r, :], wout_ref[:NSA_WIDTH, :]) + _mm(yml_ref[r, :], wout_ref[NSA_WIDTH:, :]) for r in halves]
    x1 = [x_ref[r, :] + _rms(y[i], gpost_ref[...]) for i, r in enumerate(halves)]
    h2 = [_rms(x1[i], gpre_ref[...]).astype(bf16) for i in range(X_HALVES)]
    q = [(_mm(h2[i], wq_ref[...]) * (XATTN_HEAD_DIM ** -0.5)).astype(bf16) for i in range(X_HALVES)]
    outs = [[] for _ in range(X_HALVES)]
    for hh in range(XATTN_HEADS):
        sl = slice(hh * XATTN_HEAD_DIM, (hh + 1) * XATTN_HEAD_DIM)
        s = [_nt(q[i][:, sl], k_ref[:, sl]) for i in range(X_HALVES)]
        for i in range(X_HALVES):
            p = jnp.exp(s[i] - jnp.max(s[i], axis=1, keepdims=True))
            l = jnp.sum(p, axis=1, keepdims=True)
            outs[i].append((_mm(p.astype(bf16), v_ref[:, sl]) * (1.0 / l)).astype(bf16))
    y2 = [_mm(jnp.concatenate(outs[i], axis=1), wo_ref[...]) for i in range(X_HALVES)]
    for i, r in enumerate(halves):
        out_ref[r, :] = x1[i] + _rms(y2[i], gpost2_ref[...])


def _mix_xattn(ynsa, yml, x2d, w_out, g_post, g_pre, w_xq, kx, vx, w_xo, g_post2, B, S):
    nt = S // TM_X
    M = kx.shape[1]
    tok = lambda w: pl.BlockSpec((TM_X, w), lambda b, i: (b * nt + i, 0))
    mem_spec = pl.BlockSpec((None, M, D_MODEL), lambda b, i: (b, 0, 0))
    sq = _full((D_MODEL, D_MODEL))
    row = _full((1, D_MODEL))
    bf = lambda w: w.astype(jnp.bfloat16)
    return pl.pallas_call(
        _mix_xattn_kernel,
        grid=(B, nt),
        in_specs=[tok(NSA_WIDTH), tok(MLSTM_WIDTH), tok(D_MODEL), sq, row, row, sq, mem_spec, mem_spec,
                  sq, row],
        out_specs=tok(D_MODEL),
        out_shape=jax.ShapeDtypeStruct((B * S, D_MODEL), jnp.float32),
        compiler_params=_params(2),
        name="mix_xattn",
    )(ynsa, yml, x2d, bf(w_out), g_post, g_pre, bf(w_xq), kx, vx, bf(w_xo), g_post2)


TM_F = 1024
F_HALVES = 2


def _ffn_kernel(x_ref, gpre_ref, wgu_ref, wd_ref, gpost_ref, out_ref, acc_ref):
    bf16 = jnp.bfloat16
    n_chunks = D_FF // F_TILE
    half = TM_F // F_HALVES

    def pre(i):
        return _rms(x_ref[i * half:(i + 1) * half, :], gpre_ref[...]).astype(bf16)

    def chunk(i, h, j):
        rows = slice(i * half, (i + 1) * half)
        cols = slice(j * F_TILE, (j + 1) * F_TILE)
        g = _mm(h, wgu_ref[:, cols])
        u = _mm(h, wgu_ref[:, D_FF + j * F_TILE:D_FF + (j + 1) * F_TILE])
        act = (g * jax.nn.sigmoid(g) * u).astype(bf16)
        down = _mm(act, wd_ref[cols, :])
        if j == 0:
            acc_ref[rows, :] = down
        else:
            acc_ref[rows, :] += down

    def post(i):
        rows = slice(i * half, (i + 1) * half)
        out_ref[rows, :] = x_ref[rows, :] + _rms(acc_ref[rows, :], gpost_ref[...])

    h = pre(0)
    for i in range(F_HALVES):
        chunk(i, h, 0)
        if i > 0:
            post(i - 1)
        h_next = pre(i + 1) if i + 1 < F_HALVES else None
        for j in range(1, n_chunks):
            chunk(i, h, j)
        h = h_next
    post(F_HALVES - 1)


def _ffn(x2d, g_pre, w_gate_up, w_down, g_post):
    T = x2d.shape[0]
    tok = pl.BlockSpec((TM_F, D_MODEL), lambda i: (i, 0))
    row = _full((1, D_MODEL))
    return pl.pallas_call(
        _ffn_kernel,
        grid=(T // TM_F,),
        in_specs=[tok, row, _full((D_MODEL, 2 * D_FF), buffers=1), _full((D_FF, D_MODEL), buffers=1), row],
        out_specs=tok,
        out_shape=jax.ShapeDtypeStruct((T, D_MODEL), jnp.float32),
        scratch_shapes=[pltpu.VMEM((TM_F, D_MODEL), jnp.float32)],
        compiler_params=_params(1),
        name="ffn",
    )(x2d, g_pre, w_gate_up.astype(jnp.bfloat16), w_down.astype(jnp.bfloat16), g_post)


def _layer(x, mem, rel_bias, mix_norm_pre, w_in, cmp_pos_k, cmp_pos_v, cmp_w1_k, cmp_w2_k, cmp_w1_v,
           cmp_w2_v, conv_w, conv_b, mlstm_gate_bias, mlstm_norm, w_out, mix_norm_post, xattn_norm_pre,
           mem_norm, w_xq, w_xkv, w_xo, xattn_norm_post, ffn_norm_pre, w_gate_up, w_down, ffn_norm_post):
    B, S, _ = x.shape
    row = lambda g: g.reshape(1, -1).astype(jnp.float32)
    x2d = x.reshape(B * S, D_MODEL)
    w_tok, w_feat = _in_proj_weights(w_in)
    (kc, vc, ksl, kwn, mqk, qT, vslT, vwnT, mvT, moT, smallT) = _in_proj(
        x2d, row(mix_norm_pre), w_tok, w_feat)
    kcmp, vcmpT = _compress(kc, vc, cmp_pos_k, cmp_pos_v, cmp_w1_k, cmp_w2_k, cmp_w1_v, cmp_w2_v, B, S)
    tables = _bias_tables(rel_bias.astype(jnp.float32))
    ynsa = _nsa(qT, kcmp, vcmpT, ksl, vslT, kwn, vwnT, smallT, tables, B, S)
    yml = _mlstm(mqk, mvT, moT, smallT, conv_w, conv_b, mlstm_gate_bias, mlstm_norm, B, S)
    kx, vx = _mem_kv(mem, row(mem_norm), w_xkv)
    x2 = _mix_xattn(ynsa, yml, x2d, w_out, row(mix_norm_post), row(xattn_norm_pre), w_xq, kx, vx, w_xo,
                    row(xattn_norm_post), B, S)
    x3 = _ffn(x2, row(ffn_norm_pre), w_gate_up, w_down, row(ffn_norm_post))
    return x3.reshape(B, S, D_MODEL)


def kernel(x, mem, rel_bias, mix_norm_pre, w_in, cmp_pos_k, cmp_pos_v, cmp_w1_k, cmp_w2_k, cmp_w1_v, cmp_w2_v,
           conv_w, conv_b, mlstm_gate_bias, mlstm_norm, w_out, mix_norm_post, xattn_norm_pre, mem_norm, w_xq,
           w_xkv, w_xo, xattn_norm_post, ffn_norm_pre, w_gate_up, w_down, ffn_norm_post):
    depth = w_in.shape[0]
    for l in range(depth):
        x = _layer(x, mem, rel_bias, mix_norm_pre[l], w_in[l], cmp_pos_k[l], cmp_pos_v[l], cmp_w1_k[l],
                   cmp_w2_k[l], cmp_w1_v[l], cmp_w2_v[l], conv_w[l], conv_b[l], mlstm_gate_bias[l],
                   mlstm_norm[l], w_out[l], mix_norm_post[l], xattn_norm_pre[l], mem_norm[l], w_xq[l],
                   w_xkv[l], w_xo[l], xattn_norm_post[l], ffn_norm_pre[l], w_gate_up[l], w_down[l],
                   ffn_norm_post[l])
    return x
```

```python
import functools
import math

import numpy as np
import jax
import jax.numpy as jnp
from jax import lax
from jax.experimental import pallas as pl
from jax.experimental.pallas import tpu as pltpu

D_MODEL = 1024
NSA_WIDTH = 512
NSA_HEAD_DIM = 64
NSA_HEADS = 8
NSA_KV_HEADS = 2
NSA_GROUP = 4
NSA_KV_WIDTH = 128
CMP_STRIDE = 16
CMP_BLOCK = 32
CMP_HIDDEN = 256
SEL_BLOCK = 64
N_SELECT = 16
WINDOW = 512
Q_BLOCK = 128
FORCED_SCORE = 1.0e4
MLSTM_WIDTH = 512
MLSTM_HEADS = 4
MLSTM_V_DIM = 128
MLSTM_QK_DIM = 64
MLSTM_QK_WIDTH = 256
MLSTM_CHUNK = 64
CONV_WIDTH = 4
REL_BUCKETS = 32
REL_MAX_DISTANCE = 128
XATTN_HEADS = 4
XATTN_HEAD_DIM = 256
D_FF = 2816
NORM_EPS = 1e-6
NEG_INF = -1.0e30
LOG2E = math.log2(math.e)

IN_SIZES = (NSA_WIDTH,) + (NSA_KV_WIDTH,) * 6 + (NSA_HEADS * 3, MLSTM_QK_WIDTH, MLSTM_QK_WIDTH,
                                                 MLSTM_WIDTH, MLSTM_HEADS, MLSTM_HEADS, MLSTM_WIDTH)
IN_OFFSETS = tuple(int(o) for o in np.cumsum((0,) + IN_SIZES)[:-1])

LANES = 128
SUBLANES = 8
BF16_ROWS = 16
VMEM_LIMIT_BYTES = 56 * 1024 * 1024

N_GATE_ROWS = 32
F_TILE = 256


def _rms(x, gain):
    return x * lax.rsqrt(jnp.mean(x * x, axis=-1, keepdims=True) + NORM_EPS) * gain


def _nt(a, b):
    return lax.dot_general(a, b, (((1,), (1,)), ((), ())), preferred_element_type=jnp.float32)


def _mm(a, b):
    return jnp.dot(a, b, preferred_element_type=jnp.float32)


def _params(n_axes, flags=None):
    return pltpu.CompilerParams(dimension_semantics=("arbitrary",) * n_axes,
                                vmem_limit_bytes=VMEM_LIMIT_BYTES, flags=flags)


def _full(shape, buffers=None):
    nd = len(shape)
    mode = None if buffers is None else pl.Buffered(buffers)
    return pl.BlockSpec(shape, lambda *_: (0,) * nd, pipeline_mode=mode)


TM_IN = 1024
IN_HALVES = 2
TOK_DOT_WIDTH = 512
_TOK_GROUPS = (("kc", 128, jnp.float32), ("vc", 128, jnp.float32), ("ksl", 128, jnp.bfloat16),
               ("kwn", 128, jnp.bfloat16), ("mqk", 512, jnp.bfloat16))
_FEAT_GROUPS = (("qT", 512, jnp.bfloat16), ("vslT", 128, jnp.bfloat16), ("vwnT", 128, jnp.bfloat16),
                ("mvT", 512, jnp.bfloat16), ("moT", 512, jnp.bfloat16), ("smallT", N_GATE_ROWS, jnp.float32))


def _in_proj_kernel(x_ref, g_ref, wtok_ref, wfeat_ref, *out_refs):
    n_tok = len(_TOK_GROUPS)
    half = TM_IN // IN_HALVES

    def norm(i):
        return _rms(x_ref[i * half:(i + 1) * half, :], g_ref[...]).astype(jnp.bfloat16)

    def token_major(i, h):
        n_cols = sum(width for _, width, _ in _TOK_GROUPS)
        res = [_mm(h, wtok_ref[:, c0:c0 + TOK_DOT_WIDTH]) for c0 in range(0, n_cols, TOK_DOT_WIDTH)]
        off = 0
        for (name, width, dt), o_ref in zip(_TOK_GROUPS, out_refs[:n_tok]):
            r = res[off // TOK_DOT_WIDTH]
            lo = off % TOK_DOT_WIDTH
            o_ref[i * half:(i + 1) * half, :] = r[:, lo:lo + width].astype(dt)
            off += width

    def feature_major(i, h):
        off = 0
        for (name, rows, dt), o_ref in zip(_FEAT_GROUPS, out_refs[n_tok:]):
            r = _nt(wfeat_ref[off:off + rows, :], h)
            if name == "qT":
                r = r * (NSA_HEAD_DIM ** -0.5 * LOG2E)
            for j in range(half // LANES):
                o_ref[i * (half // LANES) + j] = r[:, j * LANES:(j + 1) * LANES].astype(dt)
            off += rows

    h = norm(0)
    for i in range(IN_HALVES):
        token_major(i, h)
        h_next = norm(i + 1) if i + 1 < IN_HALVES else None
        feature_major(i, h)
        h = h_next


def _in_proj(x2d, gain, w_tok, w_feat):
    T = x2d.shape[0]
    n_tok_cols = w_tok.shape[1]
    n_feat_rows = w_feat.shape[0]
    out_shape, out_specs = [], []
    for name, width, dt in _TOK_GROUPS:
        out_shape.append(jax.ShapeDtypeStruct((T, width), dt))
        out_specs.append(pl.BlockSpec((TM_IN, width), lambda i: (i, 0)))
    for name, rows, dt in _FEAT_GROUPS:
        out_shape.append(jax.ShapeDtypeStruct((T // LANES, rows, LANES), dt))
        out_specs.append(pl.BlockSpec((TM_IN // LANES, rows, LANES), lambda i: (i, 0, 0)))
    return pl.pallas_call(
        _in_proj_kernel,
        grid=(T // TM_IN,),
        in_specs=[pl.BlockSpec((TM_IN, D_MODEL), lambda i: (i, 0)),
                  _full((1, D_MODEL)),
                  _full((D_MODEL, n_tok_cols)),
                  _full((n_feat_rows, D_MODEL))],
        out_specs=out_specs,
        out_shape=out_shape,
        compiler_params=_params(1),
        name="in_proj",
    )(x2d, gain, w_tok, w_feat)


def _in_proj_weights(w_in):
    (nq, kc, vc, ksl, vsl, kwn, vwn, gt, mq, mk, mv, mi, mf, mo) = [
        w_in[:, o:o + s] for o, s in zip(IN_OFFSETS, IN_SIZES)]
    gt_r = gt.reshape(D_MODEL, NSA_KV_HEADS, NSA_GROUP, 3).transpose(0, 3, 1, 2).reshape(D_MODEL, 24)
    small = jnp.concatenate([gt_r, mi, mf], axis=1)
    w_tok = jnp.concatenate([kc, vc, ksl, kwn, mq, mk], axis=1)
    w_feat = jnp.concatenate([nq, vsl, vwn, mv, mo, small], axis=1).T
    return w_tok.astype(jnp.bfloat16), w_feat.astype(jnp.bfloat16)


N_CHUNK_COLS = CMP_STRIDE * NSA_KV_WIDTH
N_HID2 = NSA_KV_HEADS * CMP_HIDDEN


def _compress_one(c, pos_ref, w1_ref, n_chunks):
    lo = _mm((c + pos_ref[0:1, :]).astype(jnp.bfloat16), w1_ref[0])
    hi = _mm((c + pos_ref[1:2, :]).astype(jnp.bfloat16), w1_ref[1])
    pre = lo + pltpu.roll(hi, n_chunks - 1, 0)
    return (pre * jax.nn.sigmoid(pre)).astype(jnp.bfloat16)


def _chunk_rows(ref, n_chunks):
    return jnp.concatenate([ref[pl.ds(t, n_chunks, stride=CMP_STRIDE), :] for t in range(CMP_STRIDE)], axis=1)


def _compress_kernel(kc_ref, vc_ref, posk_ref, posv_ref, w1k_ref, w1v_ref, w2k_ref, w2vT_ref,
                     kcmp_ref, vcmpT_ref):
    n_chunks = kc_ref.shape[0] // CMP_STRIDE
    hid_k = _compress_one(_chunk_rows(kc_ref, n_chunks), posk_ref, w1k_ref, n_chunks)
    kcmp = _mm(hid_k, w2k_ref[...])
    row = lax.broadcasted_iota(jnp.int32, kcmp.shape, 0)
    kcmp_ref[...] = jnp.where(row < n_chunks - 1, kcmp, 0.0).astype(kcmp_ref.dtype)
    hid_v = _compress_one(_chunk_rows(vc_ref, n_chunks), posv_ref, w1v_ref, n_chunks)
    vcmpT = _nt(w2vT_ref[...], hid_v)
    col = lax.broadcasted_iota(jnp.int32, vcmpT.shape, 1)
    vcmpT_ref[...] = jnp.where(col < n_chunks - 1, vcmpT, 0.0).astype(vcmpT_ref.dtype)


def _compress_weights(pos, w1, w2):
    eye = jnp.eye(NSA_KV_HEADS, dtype=w1.dtype)
    w1r = w1.reshape(2, CMP_STRIDE, NSA_HEAD_DIM, CMP_HIDDEN)
    w1e = jnp.einsum('atdj,hg->athdgj', w1r, eye).reshape(2, N_CHUNK_COLS, N_HID2)
    pos_e = jnp.broadcast_to(pos.reshape(2, CMP_STRIDE, 1, NSA_HEAD_DIM),
                             (2, CMP_STRIDE, NSA_KV_HEADS, NSA_HEAD_DIM)).reshape(2, N_CHUNK_COLS)
    w2e = jnp.einsum('jd,hg->hjgd', w2, eye).reshape(N_HID2, NSA_KV_WIDTH)
    return pos_e, w1e.astype(jnp.bfloat16), w2e.astype(jnp.bfloat16)


def _compress(kc, vc, cmp_pos_k, cmp_pos_v, cmp_w1_k, cmp_w2_k, cmp_w1_v, cmp_w2_v, B, S):
    n_chunks = S // CMP_STRIDE
    posk, w1k, w2k = _compress_weights(cmp_pos_k, cmp_w1_k, cmp_w2_k)
    posv, w1v, w2v = _compress_weights(cmp_pos_v, cmp_w1_v, cmp_w2_v)
    chunk_spec = pl.BlockSpec((S, NSA_KV_WIDTH), lambda b: (b, 0))
    return pl.pallas_call(
        _compress_kernel,
        grid=(B,),
        in_specs=[chunk_spec, chunk_spec,
                  _full((2, N_CHUNK_COLS)), _full((2, N_CHUNK_COLS)),
                  _full((2, N_CHUNK_COLS, N_HID2)), _full((2, N_CHUNK_COLS, N_HID2)),
                  _full((N_HID2, NSA_KV_WIDTH)), _full((NSA_KV_WIDTH, N_HID2))],
        out_specs=[pl.BlockSpec((None, n_chunks, NSA_KV_WIDTH), lambda b: (b, 0, 0)),
                   pl.BlockSpec((None, NSA_KV_WIDTH, n_chunks), lambda b: (b, 0, 0))],
        out_shape=[jax.ShapeDtypeStruct((B, n_chunks, NSA_KV_WIDTH), jnp.bfloat16),
                   jax.ShapeDtypeStruct((B, NSA_KV_WIDTH, n_chunks), jnp.bfloat16)],
        compiler_params=_params(1),
        name="compress",
    )(kc, vc, posk, posv, w1k, w1v, w2k, w2v.T)


GQ = NSA_GROUP * Q_BLOCK
TINY = 1e-30
CMP_TAB_ROWS = 512
CMP_TAB_ZERO = 248
CMP_TAB_LOOKUP = (232, 256)
SEL_STEP_SHIFT = 2
KEY_SUPER = Q_BLOCK << SEL_STEP_SHIFT
SEL_TAB_ZERO = KEY_SUPER + Q_BLOCK
SEL_TAB_ROWS = SEL_TAB_ZERO + KEY_SUPER
N_WIN_TILES = WINDOW // Q_BLOCK + 1
Q_PER_STEP = 4
SEL_SUB_CHUNKS = 2
V_ROWS = NSA_HEAD_DIM + BF16_ROWS


def _bucket_np(dist):
    n = np.maximum(dist, 0)
    max_exact = REL_BUCKETS // 2
    nf = np.maximum(n, 1).astype(np.float64)
    large = max_exact + (np.log(nf / max_exact) / math.log(REL_MAX_DISTANCE / max_exact)
                         * (REL_BUCKETS - max_exact)).astype(np.int64)
    large = np.minimum(large, REL_BUCKETS - 1)
    return np.where(n < max_exact, n, large).astype(np.int32)


def _bias_index_tables():
    m = np.arange(Q_BLOCK)[:, None]
    r = np.arange(Q_BLOCK)[None, :]
    diag = np.where(r - m >= 0, _bucket_np(r - m), -1).astype(np.int32)
    off = _bucket_np(Q_BLOCK + r - m)
    jp = np.arange(*CMP_TAB_LOOKUP)[:, None] - CMP_TAB_ZERO
    d_c = r - CMP_STRIDE * jp - (CMP_BLOCK - 1)
    cmp_idx = np.where(d_c >= 0, _bucket_np(d_c), -1).astype(np.int32)
    return diag, off, cmp_idx


def _bias_tables_kernel(rb_ref, diag_idx_ref, off_idx_ref, cmp_idx_ref, sel_ref, win_ref, cmp_ref):
    f32 = jnp.float32

    def lookup(idx, head):
        far = rb_ref[head, REL_BUCKETS - 1]
        acc = jnp.full(idx.shape, NEG_INF, f32)
        for k in range(REL_BUCKETS):
            acc = jnp.where(idx == k, (rb_ref[head, k] - far) * LOG2E, acc)
        return acc

    m_io = lax.broadcasted_iota(jnp.int32, (Q_BLOCK, Q_BLOCK), 0)
    r_io = lax.broadcasted_iota(jnp.int32, (Q_BLOCK, Q_BLOCK), 1)
    neg_tile = jnp.full((Q_BLOCK, Q_BLOCK), NEG_INF, f32)
    lo, hi = CMP_TAB_LOOKUP
    for h in range(NSA_KV_HEADS):
        for g in range(NSA_GROUP):
            head = h * NSA_GROUP + g
            sl = slice(g * Q_BLOCK, (g + 1) * Q_BLOCK)
            far = 0.0
            far_tile = jnp.full((Q_BLOCK, Q_BLOCK), far, f32)
            diag_v = lookup(diag_idx_ref[...], head)
            off_v = lookup(off_idx_ref[...], head)
            n_far = (SEL_TAB_ZERO - Q_BLOCK) // Q_BLOCK
            for t in range(SEL_TAB_ROWS // Q_BLOCK):
                rows = slice(t * Q_BLOCK, (t + 1) * Q_BLOCK)
                tile = far_tile if t < n_far else off_v if t == n_far else diag_v if t == n_far + 1 else neg_tile
                sel_ref[h, rows, sl] = tile
            win_ref[h, 0, :, sl] = diag_v
            win_ref[h, 1, :, sl] = off_v
            for back in range(2, N_WIN_TILES - 1):
                win_ref[h, back, :, sl] = far_tile
            win_ref[h, N_WIN_TILES - 1, :, sl] = jnp.where(r_io < m_io, far, NEG_INF)
            win_ref[h, N_WIN_TILES, :, sl] = neg_tile
            cmp_ref[h, 0:lo, sl] = jnp.full((lo, Q_BLOCK), far, f32)
            cmp_ref[h, lo:hi, sl] = lookup(cmp_idx_ref[...], head)
            cmp_ref[h, hi:CMP_TAB_ROWS, sl] = jnp.full((CMP_TAB_ROWS - hi, Q_BLOCK), NEG_INF, f32)


def _bias_tables(rel_bias):
    diag_idx, off_idx, cmp_idx = _bias_index_tables()
    shapes = [(NSA_KV_HEADS, SEL_TAB_ROWS, GQ), (NSA_KV_HEADS, N_WIN_TILES + 1, Q_BLOCK, GQ),
              (NSA_KV_HEADS, CMP_TAB_ROWS, GQ)]
    return pl.pallas_call(
        _bias_tables_kernel,
        in_specs=[pl.BlockSpec(memory_space=pltpu.SMEM),
                  _full(diag_idx.shape), _full(off_idx.shape), _full(cmp_idx.shape)],
        out_specs=[_full(s) for s in shapes],
        out_shape=[jax.ShapeDtypeStruct(s, jnp.float32) for s in shapes],
        grid=(1,),
        compiler_params=_params(1),
        name="bias_tables",
    )(rel_bias, jnp.asarray(diag_idx), jnp.asarray(off_idx), jnp.asarray(cmp_idx))


def _overlap_np(n_cmp_rows, n_sel):
    cmp_start = np.arange(n_cmp_rows) * CMP_STRIDE
    cmp_end = cmp_start + CMP_BLOCK - 1
    sel_start = np.arange(n_sel) * SEL_BLOCK
    ov = ((cmp_start[None, :] <= sel_start[:, None] + SEL_BLOCK - 1)
          & (cmp_end[None, :] >= sel_start[:, None])).astype(np.float32)
    ov[:, n_cmp_rows - 1] = 0.0
    return ov


def _tile4(a):
    return jnp.concatenate([a] * NSA_GROUP, axis=1)


def _select_blocks(score, score_ref, n_top, hooks=()):
    n_sel = score.shape[0]
    score_ref[...] = score
    n_grp = n_sel // SUBLANES
    grp = [score[SUBLANES * v:SUBLANES * (v + 1), :] for v in range(n_grp)]
    cnt = [jnp.zeros((SUBLANES, Q_BLOCK), jnp.int32) for _ in range(n_grp)]
    sub_io = lax.broadcasted_iota(jnp.int32, (SUBLANES, Q_BLOCK), 0)
    hook_at = {(i * n_sel) // len(hooks): hk for i, hk in enumerate(hooks)} if hooks else {}
    for jp in range(n_sel):
        if jp in hook_at:
            hook_at[jp]()
        row = score_ref[jp:jp + 1, :]
        for v in range(n_grp):
            if SUBLANES * v > jp:
                inc = (row >= grp[v]).astype(jnp.int32)
            elif SUBLANES * (v + 1) - 1 < jp:
                inc = (row > grp[v]).astype(jnp.int32)
            else:
                tie = (sub_io > jp - SUBLANES * v).astype(jnp.int32)
                inc = jnp.where(row > grp[v], 1, jnp.where(row == grp[v], tie, 0))
            cnt[v] = cnt[v] + inc
    return [jnp.where(cnt[v] < n_top, 0.0, NEG_INF) for v in range(n_grp)]


def _nsa_kernel(*refs):
    step = pl.program_id(1)
    tail = []
    for blk in range(Q_PER_STEP):
        tail = _nsa_block(step * Q_PER_STEP + blk, blk, tail, *refs)
    for part in tail:
        part()


def _nsa_block(c, blk, deferred, q_ref, kcmp_ref, vcmpT_ref, ksl_ref, vslT_ref, kwn_ref, vwnT_ref, gate_ref,
               seltab_ref, wintab_ref, cmptab_ref, ovl_ref, blkind_ref, out_ref, score_ref, selb_ref, sbufs_ref,
               swin_ref):
    sbuf_ref, sbuf2_ref = sbufs_ref.at[0], sbufs_ref.at[1]
    n_cmp = kcmp_ref.shape[0]
    n_sel = ovl_ref.shape[0]
    n_top = min(N_SELECT, n_sel)
    f32 = jnp.float32
    bf16 = jnp.bfloat16
    DH = NSA_HEAD_DIM
    heads = range(NSA_KV_HEADS)

    q = q_ref[blk]
    zq = jnp.zeros((DH, GQ), bf16)
    qcat, qpad = [], []
    for h in heads:
        qcat.append(jnp.concatenate([q[(h * NSA_GROUP + g) * DH:(h * NSA_GROUP + g + 1) * DH, :]
                                     for g in range(NSA_GROUP)], axis=1))
        qpad.append(jnp.concatenate([qcat[h], zq] if h == 0 else [zq, qcat[h]], axis=0))

    backs = list(range(N_WIN_TILES))
    kts = [jnp.maximum(c - back, 0) for back in backs]
    slots = [jnp.where(c >= back, back, N_WIN_TILES) for back in backs]
    m_win = [jnp.full((1, GQ), NEG_INF, f32) for _ in heads]

    def win_score(back, h):
        key0 = pl.multiple_of(kts[back] * Q_BLOCK, Q_BLOCK)
        s = _mm(kwn_ref[pl.ds(key0, Q_BLOCK), :], qpad[h]) + wintab_ref[h, slots[back]]
        swin_ref[h, back * Q_BLOCK:(back + 1) * Q_BLOCK, :] = s
        return jnp.max(s, axis=0, keepdims=True)

    cmp_off = pl.multiple_of(CMP_TAB_ZERO - (Q_BLOCK // CMP_STRIDE) * c, SUBLANES)
    kcmp = kcmp_ref[...]
    j_io = lax.broadcasted_iota(jnp.int32, (n_sel, Q_BLOCK), 0)
    r_io = lax.broadcasted_iota(jnp.int32, (n_sel, Q_BLOCK), 1)
    cur = (Q_BLOCK // SEL_BLOCK) * c + (r_io >= SEL_BLOCK).astype(jnp.int32)
    forced = (j_io == 0) | (j_io == cur) | (j_io == cur - 1)
    visible = j_io <= cur
    o_c = []
    for h in heads:
        tab = cmptab_ref[h, pl.ds(cmp_off, n_cmp), :]
        s = _mm(kcmp, qpad[h]) + tab
        m = jnp.maximum(jnp.max(s, axis=0, keepdims=True), 0.1 * NEG_INF)
        p = jnp.exp2(s - m)
        l = jnp.sum(p, axis=0, keepdims=True)
        pn = p * (1.0 / jnp.maximum(l, TINY))
        o_c.append(_mm(vcmpT_ref[h * DH:(h + 1) * DH, :], pn.astype(bf16)))
        psum = pn[:, 0:Q_BLOCK]
        for g in range(1, NSA_GROUP):
            psum = psum + pn[:, g * Q_BLOCK:(g + 1) * Q_BLOCK]
        imp = jnp.dot(ovl_ref[...], psum, precision=lax.Precision.HIGHEST,
                      preferred_element_type=f32)
        score = jnp.where(forced, FORCED_SCORE, jnp.where(visible, imp, -1.0))

        def hook(back, h=h):
            m_win[h] = jnp.maximum(m_win[h], win_score(back, h))

        hooks = [functools.partial(hook, back) for back in backs]
        if h == 0:
            for i, part in enumerate(deferred):
                hooks.insert(2 * i + 1, part)
        rows = _select_blocks(score, score_ref.at[h], n_top, hooks=hooks)
        for v, mask_rows in enumerate(rows):
            selb_ref[h, SUBLANES * v:SUBLANES * (v + 1), :] = mask_rows

    def ones_rows(n_keys):
        return jnp.ones((V_ROWS - DH, n_keys), bf16)

    acc_w = [jnp.zeros((V_ROWS, GQ), f32) for _ in heads]

    def win_value(back):
        for h in heads:
            p = jnp.exp2(swin_ref[h, back * Q_BLOCK:(back + 1) * Q_BLOCK, :] - m_win[h]).astype(bf16)
            vT = jnp.concatenate([vwnT_ref[kts[back], h * DH:(h + 1) * DH, :], ones_rows(Q_BLOCK)], axis=0)
            acc_w[h] = acc_w[h] + _mm(vT, p)

    blocks_per_step = KEY_SUPER // SEL_BLOCK
    tiles_per_step = KEY_SUPER // Q_BLOCK

    sub = KEY_SUPER // SEL_SUB_CHUNKS
    blocks_per_sub = sub // SEL_BLOCK
    tiles_per_sub = sub // Q_BLOCK

    own_lanes = [(lax.broadcasted_iota(jnp.int32, (sub, NSA_KV_WIDTH), 1) >= DH) == bool(h) for h in heads]

    def q_with_mask_rows(j, h):
        blk0 = pl.multiple_of(j * blocks_per_step, blocks_per_step)
        rows = jnp.concatenate([_tile4(selb_ref[h, pl.ds(blk0, blocks_per_step), :]),
                                jnp.zeros((DH - blocks_per_step, GQ), f32)], axis=0).astype(bf16)
        return jnp.concatenate([qcat[h], rows] if h == 0 else [rows, qcat[h]], axis=0)

    def score_chunk(j, ci, h, q_aug, buf_ref, near=True):
        key0 = pl.multiple_of(j * KEY_SUPER + ci * sub, sub)
        k_aug = jnp.where(own_lanes[h], ksl_ref[pl.ds(key0, sub), :], blkind_ref[h, ci * sub:(ci + 1) * sub, :])
        s = _mm(k_aug, q_aug)
        if near:
            tab_off = pl.multiple_of(
                jnp.maximum(j * KEY_SUPER - c * Q_BLOCK + SEL_TAB_ZERO, 0) + ci * sub, Q_BLOCK)
            s = s + seltab_ref[h, pl.ds(tab_off, sub), :]
        buf_ref[h, ci * sub:(ci + 1) * sub, :] = s
        return jnp.max(s, axis=0, keepdims=True)

    def value_chunk(j, ci, h, m_h, buf_ref):
        p = jnp.exp2(buf_ref[h, ci * sub:(ci + 1) * sub, :] - m_h).astype(bf16)
        vT = jnp.concatenate([vslT_ref[j * tiles_per_step + ci * tiles_per_sub + i, h * DH:(h + 1) * DH, :]
                              for i in range(tiles_per_sub)], axis=1)
        return _mm(jnp.concatenate([vT, ones_rows(sub)], axis=0), p)

    def values(j, m_old, m_cur, acc, src_ref, before_chunk=None):
        acc = [jnp.exp2(m_old[h] - m_cur[h]) * acc[h] for h in heads]
        for ci in range(SEL_SUB_CHUNKS):
            if before_chunk is not None:
                before_chunk(ci)
            for h in heads:
                acc[h] = acc[h] + value_chunk(j, ci, h, m_cur[h], src_ref)
        return tuple(acc)

    def sel_step(j, carry, src_ref, dst_ref, near=True):
        m_old, m_cur, acc = carry
        m_run = list(m_cur)
        q_aug = [q_with_mask_rows(j + 1, h) for h in heads]

        def scores(ci):
            for h in heads:
                m_run[h] = jnp.maximum(m_run[h], score_chunk(j + 1, ci, h, q_aug[h], dst_ref, near))

        acc = values(j, m_old, m_cur, acc, src_ref, before_chunk=scores)
        return m_cur, tuple(m_run), acc

    def sel_pair(i, carry, near):
        carry = sel_step(2 * i, carry, sbuf_ref, sbuf2_ref, near)
        return sel_step(2 * i + 1, carry, sbuf2_ref, sbuf_ref, near)

    n_steps = lax.shift_right_logical(c, SEL_STEP_SHIFT) + 1
    m_init = tuple(jnp.full((1, GQ), NEG_INF, f32) for _ in heads)
    acc_init = tuple(jnp.zeros((V_ROWS, GQ), f32) for _ in heads)
    m_first = list(m_init)
    win_order = list(backs)
    q_aug0 = [q_with_mask_rows(0, h) for h in heads]
    for ci in range(SEL_SUB_CHUNKS):
        for _ in range(-(-N_WIN_TILES // SEL_SUB_CHUNKS)):
            if win_order:
                win_value(win_order.pop(0))
        for h in heads:
            m_first[h] = jnp.maximum(m_first[h], score_chunk(0, ci, h, q_aug0[h], sbuf_ref))
    while win_order:
        win_value(win_order.pop(0))
    o_w = [a[0:DH, :] * (1.0 / a[DH:DH + 1, :]) for a in acc_w]
    n_piped = n_steps - 1
    odd = n_piped & 1
    n_far = lax.shift_right_logical(jnp.maximum(c - 1, 0), SEL_STEP_SHIFT)
    far_pairs = lax.shift_right_logical(jnp.maximum(n_far - 1, 0), 1)
    carry = lax.fori_loop(0, far_pairs, functools.partial(sel_pair, near=False),
                          (m_init, tuple(m_first), acc_init))
    carry = lax.fori_loop(far_pairs, lax.shift_right_logical(n_piped, 1), functools.partial(sel_pair, near=True),
                          carry)
    carry = lax.fori_loop(0, odd, lambda _, cr: sel_step(n_piped - 1, cr, sbuf_ref, sbuf2_ref), carry)
    m_old, m_cur, acc = carry
    acc = [jnp.exp2(m_old[h] - m_cur[h]) * acc[h] for h in heads]
    last_ref = sbufs_ref.at[odd]

    def last_values(ci):
        for h in heads:
            acc[h] = acc[h] + value_chunk(n_piped, ci, h, m_cur[h], last_ref)

    def combine():
        o_s = [a[0:DH, :] * (1.0 / jnp.maximum(a[DH:DH + 1, :], TINY)) for a in acc]
        for h in heads:
            ys = []
            for g in range(NSA_GROUP):
                sl = slice(g * Q_BLOCK, (g + 1) * Q_BLOCK)
                row0 = h * NSA_GROUP + g
                gates = [jax.nn.sigmoid(gate_ref[blk, kind * NSA_HEADS + row0:kind * NSA_HEADS + row0 + 1, :])
                         for kind in range(3)]
                ys.append(gates[0] * o_c[h][:, sl] + gates[1] * o_s[h][:, sl] + gates[2] * o_w[h][:, sl])
            yT = jnp.concatenate(ys, axis=0)
            for half in range(2):
                col = (2 * h + half) * LANES
                out_ref[blk * Q_BLOCK:(blk + 1) * Q_BLOCK, col:col + LANES] = (
                    yT[half * LANES:(half + 1) * LANES, :].T.astype(out_ref.dtype))

    return [functools.partial(last_values, ci) for ci in range(SEL_SUB_CHUNKS)] + [combine]


def _nsa(qT, kcmp, vcmpT, ksl, vslT, kwn, vwnT, smallT, tables, B, S):
    assert S % KEY_SUPER == 0
    nq = S // Q_BLOCK
    n_steps = nq // Q_PER_STEP
    n_cmp = S // CMP_STRIDE
    n_sel = S // SEL_BLOCK
    seltab, wintab, cmptab = tables
    ovl = jnp.asarray(_overlap_np(n_cmp, n_sel))
    key_blk = np.arange(KEY_SUPER)[:, None] // SEL_BLOCK
    lane = np.arange(NSA_KV_WIDTH)[None, :]
    blkind = jnp.asarray(np.stack([lane - NSA_HEAD_DIM == key_blk, lane == key_blk]), jnp.bfloat16)
    ksl3 = ksl.reshape(B, S, NSA_KV_WIDTH)
    kwn3 = kwn.reshape(B, S, NSA_KV_WIDTH)
    vslT4 = vslT.reshape(B, nq, NSA_KV_WIDTH, Q_BLOCK)
    vwnT4 = vwnT.reshape(B, nq, NSA_KV_WIDTH, Q_BLOCK)
    k_spec = pl.BlockSpec((None, S, NSA_KV_WIDTH), lambda b, c: (b, 0, 0))
    vT_spec = pl.BlockSpec((None, nq, NSA_KV_WIDTH, Q_BLOCK), lambda b, c: (b, 0, 0, 0))
    const = lambda a: pl.BlockSpec(a.shape, lambda b, c: (0,) * a.ndim)
    return pl.pallas_call(
        _nsa_kernel,
        grid=(B, n_steps),
        in_specs=[pl.BlockSpec((Q_PER_STEP, NSA_WIDTH, Q_BLOCK), lambda b, c: (b * n_steps + c, 0, 0)),
                  pl.BlockSpec((None, n_cmp, NSA_KV_WIDTH), lambda b, c: (b, 0, 0)),
                  pl.BlockSpec((None, NSA_KV_WIDTH, n_cmp), lambda b, c: (b, 0, 0)),
                  k_spec, vT_spec, k_spec, vT_spec,
                  pl.BlockSpec((Q_PER_STEP, N_GATE_ROWS, Q_BLOCK), lambda b, c: (b * n_steps + c, 0, 0)),
                  const(seltab), const(wintab), const(cmptab), const(ovl), const(blkind)],
        out_specs=pl.BlockSpec((Q_PER_STEP * Q_BLOCK, NSA_WIDTH), lambda b, c: (b * n_steps + c, 0)),
        out_shape=jax.ShapeDtypeStruct((B * S, NSA_WIDTH), jnp.bfloat16),
        scratch_shapes=[pltpu.VMEM((NSA_KV_HEADS, n_sel, Q_BLOCK), jnp.float32),
                        pltpu.VMEM((NSA_KV_HEADS, n_sel, Q_BLOCK), jnp.float32),
                        pltpu.VMEM((2, NSA_KV_HEADS, KEY_SUPER, GQ), jnp.float32),
                        pltpu.VMEM((NSA_KV_HEADS, N_WIN_TILES * Q_BLOCK, GQ), jnp.float32)],
        compiler_params=_params(2),
        name="nsa",
    )(qT, kcmp, vcmpT, ksl3, vslT4, kwn3, vwnT4, smallT, seltab, wintab, cmptab, ovl, blkind)


CONV_TILE = 256
I_ROW = 24
C_ROWS = MLSTM_V_DIM + BF16_ROWS


def _log_sigmoid(x):
    return jnp.minimum(x, 0.0) - jnp.log(1.0 + jnp.exp(-jnp.abs(x)))


def _mlstm_kernel(mqk_ref, mvT_ref, moT_ref, gates_ref, gb_ref, tri_ref, last_ref, convw_ref, convb_ref,
                  norm_ref, out_ref, qk_ref, rows_ref):
    S = mqk_ref.shape[0]
    nt = S // LANES
    L = MLSTM_CHUNK
    f32 = jnp.float32
    bf16 = jnp.bfloat16
    kscale_row = jnp.where(lax.broadcasted_iota(jnp.int32, (1, 2 * MLSTM_QK_WIDTH), 1) < MLSTM_QK_WIDTH,
                           1.0, MLSTM_QK_DIM ** -0.5)

    def conv_body(i, _):
        t0 = pl.multiple_of(i * CONV_TILE, CONV_TILE)
        cur = mqk_ref[pl.ds(t0, CONV_TILE), :].astype(f32)
        prev_start = pl.multiple_of(jnp.maximum(t0 - BF16_ROWS, 0), BF16_ROWS)
        prev = (mqk_ref[pl.ds(prev_start, BF16_ROWS), :].astype(f32)[BF16_ROWS - SUBLANES:, :]
                * jnp.where(i > 0, 1.0, 0.0))
        ext = jnp.concatenate([prev, cur], axis=0)
        y = convb_ref[...]
        for j in range(CONV_WIDTH):
            lo = SUBLANES - (CONV_WIDTH - 1) + j
            y = y + convw_ref[j:j + 1, :] * ext[lo:lo + CONV_TILE, :]
        y = y * jax.nn.sigmoid(y) * kscale_row
        qk_ref[pl.ds(t0, CONV_TILE), :] = y.astype(bf16)
        return 0

    lax.fori_loop(0, S // CONV_TILE, conv_body, 0)

    H = MLSTM_HEADS
    G8 = 2 * H
    n_rows = nt * G8
    a3 = gates_ref[:, I_ROW:I_ROW + G8, :] + gb_ref[...][None]
    is_f = lax.broadcasted_iota(jnp.int32, a3.shape, 1) >= H
    x = jnp.where(is_f, _log_sigmoid(a3), a3).reshape(n_rows, LANES)
    bcum = jnp.dot(x, tri_ref[...], precision=lax.Precision.HIGHEST, preferred_element_type=f32)
    b_rows = pltpu.roll(bcum, n_rows - H, 0)
    g_rows = x - b_rows
    pos = lax.broadcasted_iota(jnp.int32, (n_rows, LANES), 1) & (L - 1)

    def chunk_cummax(a):
        shift = 1
        while shift < L:
            a = jnp.where(pos >= shift, jnp.maximum(a, pltpu.roll(a, shift, 1)), a)
            shift *= 2
        return a

    def chunk_last(a):
        return jnp.dot(a, last_ref[...], precision=lax.Precision.HIGHEST, preferred_element_type=f32)

    bl_rows = chunk_last(b_rows)
    rows_ref[0] = g_rows
    rows_ref[1] = chunk_cummax(g_rows)
    rows_ref[2] = b_rows
    rows_ref[3] = bl_rows
    rows_ref[4] = chunk_last(chunk_cummax(bl_rows + g_rows))

    s_io = lax.broadcasted_iota(jnp.int32, (LANES, LANES), 0)
    t_io = lax.broadcasted_iota(jnp.int32, (LANES, LANES), 1)
    causal = (s_io <= t_io) & ((s_io >= L) == (t_io >= L))
    lane = lax.broadcasted_iota(jnp.int32, (1, LANES), 1)
    in_chunk = [lane < L, lane >= L]
    QD, VD = MLSTM_QK_DIM, MLSTM_V_DIM
    ones_aug = jnp.ones((C_ROWS - VD, LANES), bf16)
    head_lanes = [(lax.broadcasted_iota(jnp.int32, (LANES, LANES), 1) >= QD) == bool(par) for par in range(2)]

    def swap_halves(row):
        return pltpu.roll(row, L, 1)

    def tile_body(i, state):
        t0 = pl.multiple_of(i * LANES, LANES)
        r0 = pl.multiple_of(i * G8, G8)
        qk = qk_ref[pl.ds(t0, LANES), :]
        qk32 = qk.astype(f32)
        pairsT = [qk32[:, j * LANES:(j + 1) * LANES].T for j in range(4)]
        g8, cm8, b8, bl8, wm8 = [rows_ref[j, pl.ds(r0, G8), :] for j in range(5)]
        new_state = []
        for hh in range(H):
            pair, par = hh // 2, hh % 2
            g_r, cm_r, b_r, bl_r, wm_r = [a[hh:hh + 1, :] for a in (g8, cm8, b8, bl8, wm8)]
            caug, m_in = state[hh]
            qT = pairsT[pair][par * QD:(par + 1) * QD, :]
            kT = pairsT[2 + pair][par * QD:(par + 1) * QD, :]
            kpair = qk[:, (2 + pair) * LANES:(3 + pair) * LANES]
            qmask = jnp.where(head_lanes[par], qk[:, pair * LANES:(pair + 1) * LANES], jnp.zeros((), bf16))
            vaug = jnp.concatenate([mvT_ref[i, hh * VD:(hh + 1) * VD, :], ones_aug], axis=0)

            m_mid = swap_halves(jnp.maximum(bl_r + m_in, wm_r))
            m_prev = jnp.where(in_chunk[0], m_in, m_mid)
            m_next = jnp.maximum(bl_r + m_prev, wm_r)
            m_intra = b_r + cm_r
            m_inter = b_r + m_prev
            m_t = jnp.maximum(m_inter, m_intra)
            e_intra = jnp.exp(m_intra - m_t)
            e_inter = jnp.exp(m_inter - m_t)
            decay = jnp.exp(bl_r + m_prev - m_next)
            inject = jnp.exp(wm_r - m_next)

            g_mat = jnp.broadcast_to(g_r, (LANES, LANES)).T
            w = jnp.exp(jnp.where(causal, g_mat - cm_r, NEG_INF))
            st = _nt(kpair, qmask) * (w * e_intra)
            y = _mm(vaug, st.astype(bf16))
            kw = kT * jnp.exp(bl_r + g_r - wm_r)
            cs = caug
            for p in range(LANES // L):
                qs = jnp.where(in_chunk[p], qT * e_inter, 0.0).astype(bf16)
                y = y + _mm(cs.astype(bf16), qs)
                u = _nt(vaug, jnp.where(in_chunk[p], kw, 0.0).astype(bf16))
                dec = decay if p == 0 else swap_halves(decay)
                inj = inject if p == 0 else swap_halves(inject)
                cs = dec[:, 0:QD] * cs + inj[:, 0:QD] * u
            m_out = jnp.where(in_chunk[1], m_next, swap_halves(m_next))
            new_state.append((cs, m_out))

            den = y[VD:VD + 1, :]
            hT = y[0:VD, :] * (1.0 / jnp.maximum(jnp.abs(den), jnp.exp(-m_t)))
            sl = slice(hh * VD, (hh + 1) * VD)
            hn = hT * lax.rsqrt(jnp.mean(hT * hT, axis=0, keepdims=True) + NORM_EPS) * norm_ref[sl, :]
            yT = jax.nn.sigmoid(moT_ref[i, sl, :].astype(f32)) * hn
            out_ref[pl.ds(t0, LANES), sl] = yT.T.astype(out_ref.dtype)
        return tuple(new_state)

    init = tuple((jnp.zeros((C_ROWS, QD), f32), jnp.zeros((1, LANES), f32)) for _ in range(H))
    lax.fori_loop(0, nt, tile_body, init, unroll=4)


def _mlstm(mqk, mvT, moT, smallT, conv_w, conv_b, gate_bias, mlstm_norm, B, S):
    nt = S // LANES
    gb = jnp.broadcast_to(gate_bias.astype(jnp.float32).reshape(2 * MLSTM_HEADS, 1), (2 * MLSTM_HEADS, LANES))
    norm_cols = jnp.broadcast_to(mlstm_norm.astype(jnp.float32).reshape(MLSTM_WIDTH, 1), (MLSTM_WIDTH, LANES))
    lane = np.arange(LANES)
    same_chunk = lane[:, None] // MLSTM_CHUNK == lane[None, :] // MLSTM_CHUNK
    tri = (same_chunk & (lane[:, None] <= lane[None, :])).astype(np.float32)
    last = (same_chunk & (lane[:, None] % MLSTM_CHUNK == MLSTM_CHUNK - 1)).astype(np.float32)
    seq = lambda w: pl.BlockSpec((S, w), lambda b: (b, 0))
    tiles = lambda rows: pl.BlockSpec((nt, rows, LANES), lambda b: (b, 0, 0))
    return pl.pallas_call(
        _mlstm_kernel,
        grid=(B,),
        in_specs=[seq(2 * MLSTM_QK_WIDTH), tiles(MLSTM_WIDTH), tiles(MLSTM_WIDTH), tiles(N_GATE_ROWS),
                  _full((2 * MLSTM_HEADS, LANES)), _full((LANES, LANES)), _full((LANES, LANES)),
                  _full((CONV_WIDTH, 2 * MLSTM_QK_WIDTH)), _full((1, 2 * MLSTM_QK_WIDTH)),
                  _full((MLSTM_WIDTH, LANES))],
        out_specs=seq(MLSTM_WIDTH),
        out_shape=jax.ShapeDtypeStruct((B * S, MLSTM_WIDTH), jnp.bfloat16),
        scratch_shapes=[pltpu.VMEM((S, 2 * MLSTM_QK_WIDTH), jnp.bfloat16),
                        pltpu.VMEM((5, nt * 2 * MLSTM_HEADS, LANES), jnp.float32)],
        compiler_params=_params(1),
        name="mlstm",
    )(mqk, mvT, moT, smallT, gb, jnp.asarray(tri), jnp.asarray(last), conv_w, conv_b.reshape(1, -1), norm_cols)


def _mem_kv_kernel(mem_ref, g_ref, w_ref, k_ref, v_ref):
    mn = _rms(mem_ref[...], g_ref[...]).astype(jnp.bfloat16)
    k_ref[...] = _mm(mn, w_ref[:, :D_MODEL]).astype(k_ref.dtype)
    v_ref[...] = _mm(mn, w_ref[:, D_MODEL:]).astype(v_ref.dtype)


def _mem_kv(mem, gain, w_xkv):
    B, M, _ = mem.shape
    spec = pl.BlockSpec((None, M, D_MODEL), lambda b: (b, 0, 0))
    return pl.pallas_call(
        _mem_kv_kernel,
        grid=(B,),
        in_specs=[spec, _full((1, D_MODEL)), _full((D_MODEL, 2 * D_MODEL))],
        out_specs=[spec, spec],
        out_shape=[jax.ShapeDtypeStruct((B, M, D_MODEL), jnp.bfloat16)] * 2,
        compiler_params=_params(1),
        name="mem_kv",
    )(mem, gain, w_xkv.astype(jnp.bfloat16))


TM_X = 1024
X_HALVES = 2


def _mix_xattn_kernel(ynsa_ref, yml_ref, x_ref, wout_ref, gpost_ref, gpre_ref, wq_ref, k_ref, v_ref,
                      wo_ref, gpost2_ref, out_ref):
    bf16 = jnp.bfloat16
    halves = [slice(i * (TM_X // X_HALVES), (i + 1) * (TM_X // X_HALVES)) for i in range(X_HALVES)]
    y = [_mm(ynsa_ref[r, :], wout_ref[:NSA_WIDTH, :]) + _mm(yml_ref[r, :], wout_ref[NSA_WIDTH:, :]) for r in halves]
    x1 = [x_ref[r, :] + _rms(y[i], gpost_ref[...]) for i, r in enumerate(halves)]
    h2 = [_rms(x1[i], gpre_ref[...]).astype(bf16) for i in range(X_HALVES)]
    q = [(_mm(h2[i], wq_ref[...]) * (XATTN_HEAD_DIM ** -0.5)).astype(bf16) for i in range(X_HALVES)]
    outs = [[] for _ in range(X_HALVES)]
    for hh in range(XATTN_HEADS):
        sl = slice(hh * XATTN_HEAD_DIM, (hh + 1) * XATTN_HEAD_DIM)
        s = [_nt(q[i][:, sl], k_ref[:, sl]) for i in range(X_HALVES)]
        for i in range(X_HALVES):
            p = jnp.exp(s[i] - jnp.max(s[i], axis=1, keepdims=True))
            l = jnp.sum(p, axis=1, keepdims=True)
            outs[i].append((_mm(p.astype(bf16), v_ref[:, sl]) * (1.0 / l)).astype(bf16))
    y2 = [_mm(jnp.concatenate(outs[i], axis=1), wo_ref[...]) for i in range(X_HALVES)]
    for i, r in enumerate(halves):
        out_ref[r, :] = x1[i] + _rms(y2[i], gpost2_ref[...])


def _mix_xattn(ynsa, yml, x2d, w_out, g_post, g_pre, w_xq, kx, vx, w_xo, g_post2, B, S):
    nt = S // TM_X
    M = kx.shape[1]
    tok = lambda w: pl.BlockSpec((TM_X, w), lambda b, i: (b * nt + i, 0))
    mem_spec = pl.BlockSpec((None, M, D_MODEL), lambda b, i: (b, 0, 0))
    sq = _full((D_MODEL, D_MODEL))
    row = _full((1, D_MODEL))
    bf = lambda w: w.astype(jnp.bfloat16)
    return pl.pallas_call(
        _mix_xattn_kernel,
        grid=(B, nt),
        in_specs=[tok(NSA_WIDTH), tok(MLSTM_WIDTH), tok(D_MODEL), sq, row, row, sq, mem_spec, mem_spec,
                  sq, row],
        out_specs=tok(D_MODEL),
        out_shape=jax.ShapeDtypeStruct((B * S, D_MODEL), jnp.float32),
        compiler_params=_params(2),
        name="mix_xattn",
    )(ynsa, yml, x2d, bf(w_out), g_post, g_pre, bf(w_xq), kx, vx, bf(w_xo), g_post2)


TM_F = 1024
F_HALVES = 2


def _ffn_kernel(x_ref, gpre_ref, wgu_ref, wd_ref, gpost_ref, out_ref, acc_ref):
    bf16 = jnp.bfloat16
    n_chunks = D_FF // F_TILE
    half = TM_F // F_HALVES

    def pre(i):
        return _rms(x_ref[i * half:(i + 1) * half, :], gpre_ref[...]).astype(bf16)

    def chunk(i, h, j):
        rows = slice(i * half, (i + 1) * half)
        cols = slice(j * F_TILE, (j + 1) * F_TILE)
        g = _mm(h, wgu_ref[:, cols])
        u = _mm(h, wgu_ref[:, D_FF + j * F_TILE:D_FF + (j + 1) * F_TILE])
        act = (g * jax.nn.sigmoid(g) * u).astype(bf16)
        down = _mm(act, wd_ref[cols, :])
        if j == 0:
            acc_ref[rows, :] = down
        else:
            acc_ref[rows, :] += down

    def post(i):
        rows = slice(i * half, (i + 1) * half)
        out_ref[rows, :] = x_ref[rows, :] + _rms(acc_ref[rows, :], gpost_ref[...])

    h = pre(0)
    for i in range(F_HALVES):
        chunk(i, h, 0)
        if i > 0:
            post(i - 1)
        h_next = pre(i + 1) if i + 1 < F_HALVES else None
        for j in range(1, n_chunks):
            chunk(i, h, j)
        h = h_next
    post(F_HALVES - 1)


def _ffn(x2d, g_pre, w_gate_up, w_down, g_post):
    T = x2d.shape[0]
    tok = pl.BlockSpec((TM_F, D_MODEL), lambda i: (i, 0))
    row = _full((1, D_MODEL))
    return pl.pallas_call(
        _ffn_kernel,
        grid=(T // TM_F,),
        in_specs=[tok, row, _full((D_MODEL, 2 * D_FF), buffers=1), _full((D_FF, D_MODEL), buffers=1), row],
        out_specs=tok,
        out_shape=jax.ShapeDtypeStruct((T, D_MODEL), jnp.float32),
        scratch_shapes=[pltpu.VMEM((TM_F, D_MODEL), jnp.float32)],
        compiler_params=_params(1),
        name="ffn",
    )(x2d, g_pre, w_gate_up.astype(jnp.bfloat16), w_down.astype(jnp.bfloat16), g_post)


def _layer(x, mem, rel_bias, mix_norm_pre, w_in, cmp_pos_k, cmp_pos_v, cmp_w1_k, cmp_w2_k, cmp_w1_v,
           cmp_w2_v, conv_w, conv_b, mlstm_gate_bias, mlstm_norm, w_out, mix_norm_post, xattn_norm_pre,
           mem_norm, w_xq, w_xkv, w_xo, xattn_norm_post, ffn_norm_pre, w_gate_up, w_down, ffn_norm_post):
    B, S, _ = x.shape
    row = lambda g: g.reshape(1, -1).astype(jnp.float32)
    x2d = x.reshape(B * S, D_MODEL)
    w_tok, w_feat = _in_proj_weights(w_in)
    (kc, vc, ksl, kwn, mqk, qT, vslT, vwnT, mvT, moT, smallT) = _in_proj(
        x2d, row(mix_norm_pre), w_tok, w_feat)
    kcmp, vcmpT = _compress(kc, vc, cmp_pos_k, cmp_pos_v, cmp_w1_k, cmp_w2_k, cmp_w1_v, cmp_w2_v, B, S)
    tables = _bias_tables(rel_bias.astype(jnp.float32))
    ynsa = _nsa(qT, kcmp, vcmpT, ksl, vslT, kwn, vwnT, smallT, tables, B, S)
    yml = _mlstm(mqk, mvT, moT, smallT, conv_w, conv_b, mlstm_gate_bias, mlstm_norm, B, S)
    kx, vx = _mem_kv(mem, row(mem_norm), w_xkv)
    x2 = _mix_xattn(ynsa, yml, x2d, w_out, row(mix_norm_post), row(xattn_norm_pre), w_xq, kx, vx, w_xo,
                    row(xattn_norm_post), B, S)
    x3 = _ffn(x2, row(ffn_norm_pre), w_gate_up, w_down, row(ffn_norm_post))
    return x3.reshape(B, S, D_MODEL)


def kernel(x, mem, rel_bias, mix_norm_pre, w_in, cmp_pos_k, cmp_pos_v, cmp_w1_k, cmp_w2_k, cmp_w1_v, cmp_w2_v,
           conv_w, conv_b, mlstm_gate_bias, mlstm_norm, w_out, mix_norm_post, xattn_norm_pre, mem_norm, w_xq,
           w_xkv, w_xo, xattn_norm_post, ffn_norm_pre, w_gate_up, w_down, ffn_norm_post):
    depth = w_in.shape[0]
    for l in range(depth):
        x = _layer(x, mem, rel_bias, mix_norm_pre[l], w_in[l], cmp_pos_k[l], cmp_pos_v[l], cmp_w1_k[l],
                   cmp_w2_k[l], cmp_w1_v[l], cmp_w2_v[l], conv_w[l], conv_b[l], mlstm_gate_bias[l],
                   mlstm_norm[l], w_out[l], mix_norm_post[l], xattn_norm_pre[l], mem_norm[l], w_xq[l],
                   w_xkv[l], w_xo[l], xattn_norm_post[l], ffn_norm_pre[l], w_gate_up[l], w_down[l],
                   ffn_norm_post[l])
    return x
```

```python
import functools
import math

import numpy as np
import jax
import jax.numpy as jnp
from jax import lax
from jax.experimental import pallas as pl
from jax.experimental.pallas import tpu as pltpu

D_MODEL = 1024
NSA_WIDTH = 512
NSA_HEAD_DIM = 64
NSA_HEADS = 8
NSA_KV_HEADS = 2
NSA_GROUP = 4
NSA_KV_WIDTH = 128
CMP_STRIDE = 16
CMP_BLOCK = 32
CMP_HIDDEN = 256
SEL_BLOCK = 64
N_SELECT = 16
WINDOW = 512
Q_BLOCK = 128
FORCED_SCORE = 1.0e4
MLSTM_WIDTH = 512
MLSTM_HEADS = 4
MLSTM_V_DIM = 128
MLSTM_QK_DIM = 64
MLSTM_QK_WIDTH = 256
MLSTM_CHUNK = 64
CONV_WIDTH = 4
REL_BUCKETS = 32
REL_MAX_DISTANCE = 128
XATTN_HEADS = 4
XATTN_HEAD_DIM = 256
D_FF = 2816
NORM_EPS = 1e-6
NEG_INF = -1.0e30
LOG2E = math.log2(math.e)

IN_SIZES = (NSA_WIDTH,) + (NSA_KV_WIDTH,) * 6 + (NSA_HEADS * 3, MLSTM_QK_WIDTH, MLSTM_QK_WIDTH,
                                                 MLSTM_WIDTH, MLSTM_HEADS, MLSTM_HEADS, MLSTM_WIDTH)
IN_OFFSETS = tuple(int(o) for o in np.cumsum((0,) + IN_SIZES)[:-1])

LANES = 128
SUBLANES = 8
BF16_ROWS = 16
VMEM_LIMIT_BYTES = 56 * 1024 * 1024

N_GATE_ROWS = 32
F_TILE = 256


def _rms(x, gain):
    return x * lax.rsqrt(jnp.mean(x * x, axis=-1, keepdims=True) + NORM_EPS) * gain


def _nt(a, b):
    return lax.dot_general(a, b, (((1,), (1,)), ((), ())), preferred_element_type=jnp.float32)


def _mm(a, b):
    return jnp.dot(a, b, preferred_element_type=jnp.float32)


def _params(n_axes, flags=None):
    return pltpu.CompilerParams(dimension_semantics=("arbitrary",) * n_axes,
                                vmem_limit_bytes=VMEM_LIMIT_BYTES, flags=flags)


def _full(shape, buffers=None):
    nd = len(shape)
    mode = None if buffers is None else pl.Buffered(buffers)
    return pl.BlockSpec(shape, lambda *_: (0,) * nd, pipeline_mode=mode)


TM_IN = 1024
IN_HALVES = 2
TOK_DOT_WIDTH = 512
_TOK_GROUPS = (("kc", 128, jnp.float32), ("vc", 128, jnp.float32), ("ksl", 128, jnp.bfloat16),
               ("kwn", 128, jnp.bfloat16), ("mqk", 512, jnp.bfloat16))
_FEAT_GROUPS = (("qT", 512, jnp.bfloat16), ("vslT", 128, jnp.bfloat16), ("vwnT", 128, jnp.bfloat16),
                ("mvT", 512, jnp.bfloat16), ("moT", 512, jnp.bfloat16), ("smallT", N_GATE_ROWS, jnp.float32))


def _in_proj_kernel(x_ref, g_ref, wtok_ref, wfeat_ref, *out_refs):
    n_tok = len(_TOK_GROUPS)
    half = TM_IN // IN_HALVES

    def norm(i):
        return _rms(x_ref[i * half:(i + 1) * half, :], g_ref[...]).astype(jnp.bfloat16)

    def token_major(i, h):
        n_cols = sum(width for _, width, _ in _TOK_GROUPS)
        res = [_mm(h, wtok_ref[:, c0:c0 + TOK_DOT_WIDTH]) for c0 in range(0, n_cols, TOK_DOT_WIDTH)]
        off = 0
        for (name, width, dt), o_ref in zip(_TOK_GROUPS, out_refs[:n_tok]):
            r = res[off // TOK_DOT_WIDTH]
            lo = off % TOK_DOT_WIDTH
            o_ref[i * half:(i + 1) * half, :] = r[:, lo:lo + width].astype(dt)
            off += width

    def feature_major(i, h):
        off = 0
        for (name, rows, dt), o_ref in zip(_FEAT_GROUPS, out_refs[n_tok:]):
            r = _nt(wfeat_ref[off:off + rows, :], h)
            if name == "qT":
                r = r * (NSA_HEAD_DIM ** -0.5 * LOG2E)
            for j in range(half // LANES):
                o_ref[i * (half // LANES) + j] = r[:, j * LANES:(j + 1) * LANES].astype(dt)
            off += rows

    h = norm(0)
    for i in range(IN_HALVES):
        token_major(i, h)
        h_next = norm(i + 1) if i + 1 < IN_HALVES else None
        feature_major(i, h)
        h = h_next


def _in_proj(x2d, gain, w_tok, w_feat):
    T = x2d.shape[0]
    n_tok_cols = w_tok.shape[1]
    n_feat_rows = w_feat.shape[0]
    out_shape, out_specs = [], []
    for name, width, dt in _TOK_GROUPS:
        out_shape.append(jax.ShapeDtypeStruct((T, width), dt))
        out_specs.append(pl.BlockSpec((TM_IN, width), lambda i: (i, 0)))
    for name, rows, dt in _FEAT_GROUPS:
        out_shape.append(jax.ShapeDtypeStruct((T // LANES, rows, LANES), dt))
        out_specs.append(pl.BlockSpec((TM_IN // LANES, rows, LANES), lambda i: (i, 0, 0)))
    return pl.pallas_call(
        _in_proj_kernel,
        grid=(T // TM_IN,),
        in_specs=[pl.BlockSpec((TM_IN, D_MODEL), lambda i: (i, 0)),
                  _full((1, D_MODEL)),
                  _full((D_MODEL, n_tok_cols)),
                  _full((n_feat_rows, D_MODEL))],
        out_specs=out_specs,
        out_shape=out_shape,
        compiler_params=_params(1),
        name="in_proj",
    )(x2d, gain, w_tok, w_feat)


def _in_proj_weights(w_in):
    (nq, kc, vc, ksl, vsl, kwn, vwn, gt, mq, mk, mv, mi, mf, mo) = [
        w_in[:, o:o + s] for o, s in zip(IN_OFFSETS, IN_SIZES)]
    gt_r = gt.reshape(D_MODEL, NSA_KV_HEADS, NSA_GROUP, 3).transpose(0, 3, 1, 2).reshape(D_MODEL, 24)
    small = jnp.concatenate([gt_r, mi, mf], axis=1)
    w_tok = jnp.concatenate([kc, vc, ksl, kwn, mq, mk], axis=1)
    w_feat = jnp.concatenate([nq, vsl, vwn, mv, mo, small], axis=1).T
    return w_tok.astype(jnp.bfloat16), w_feat.astype(jnp.bfloat16)


N_CHUNK_COLS = CMP_STRIDE * NSA_KV_WIDTH
N_HID2 = NSA_KV_HEADS * CMP_HIDDEN


def _compress_one(tok_ref, pos_ref, w1_ref, n_chunks):
    bf16 = jnp.bfloat16
    zero = jnp.zeros((NSA_HEAD_DIM, CMP_HIDDEN), bf16)
    rows = [tok_ref[pl.ds(t, n_chunks, stride=CMP_STRIDE), :] for t in range(CMP_STRIDE)]
    halves = []
    for a in range(2):
        acc = None
        for t0 in range(0, CMP_STRIDE, 2):
            xs, ws = [], []
            for t in (t0, t0 + 1):
                xs.append((rows[t] + pos_ref[a:a + 1, t * NSA_KV_WIDTH:(t + 1) * NSA_KV_WIDTH]).astype(bf16))
                w = w1_ref[a, t]
                ws.append(jnp.concatenate([jnp.concatenate([w, zero], axis=1),
                                           jnp.concatenate([zero, w], axis=1)], axis=0))
            d = _mm(jnp.concatenate(xs, axis=1), jnp.concatenate(ws, axis=0))
            acc = d if acc is None else acc + d
        halves.append(acc)
    pre = halves[0] + pltpu.roll(halves[1], n_chunks - 1, 0)
    return (pre * jax.nn.sigmoid(pre)).astype(bf16)


def _compress_kernel(kc_ref, vc_ref, posk_ref, posv_ref, w1k_ref, w1v_ref, w2k_ref, w2vT_ref,
                     kcmp_ref, vcmpT_ref):
    n_chunks = kc_ref.shape[0] // CMP_STRIDE
    hid_k = _compress_one(kc_ref, posk_ref, w1k_ref, n_chunks)
    kcmp = _mm(hid_k, w2k_ref[...])
    row = lax.broadcasted_iota(jnp.int32, kcmp.shape, 0)
    kcmp_ref[...] = jnp.where(row < n_chunks - 1, kcmp, 0.0).astype(kcmp_ref.dtype)
    hid_v = _compress_one(vc_ref, posv_ref, w1v_ref, n_chunks)
    vcmpT = _nt(w2vT_ref[...], hid_v)
    col = lax.broadcasted_iota(jnp.int32, vcmpT.shape, 1)
    vcmpT_ref[...] = jnp.where(col < n_chunks - 1, vcmpT, 0.0).astype(vcmpT_ref.dtype)


def _compress_weights(pos, w1, w2):
    eye = jnp.eye(NSA_KV_HEADS, dtype=w1.dtype)
    w1r = w1.reshape(2, CMP_STRIDE, NSA_HEAD_DIM, CMP_HIDDEN)
    pos_e = jnp.broadcast_to(pos.reshape(2, CMP_STRIDE, 1, NSA_HEAD_DIM),
                             (2, CMP_STRIDE, NSA_KV_HEADS, NSA_HEAD_DIM)).reshape(2, N_CHUNK_COLS)
    w2e = jnp.einsum('jd,hg->hjgd', w2, eye).reshape(N_HID2, NSA_KV_WIDTH)
    return pos_e, w1r.astype(jnp.bfloat16), w2e.astype(jnp.bfloat16)


def _compress(kc, vc, cmp_pos_k, cmp_pos_v, cmp_w1_k, cmp_w2_k, cmp_w1_v, cmp_w2_v, B, S):
    n_chunks = S // CMP_STRIDE
    posk, w1k, w2k = _compress_weights(cmp_pos_k, cmp_w1_k, cmp_w2_k)
    posv, w1v, w2v = _compress_weights(cmp_pos_v, cmp_w1_v, cmp_w2_v)
    chunk_spec = pl.BlockSpec((S, NSA_KV_WIDTH), lambda b: (b, 0))
    return pl.pallas_call(
        _compress_kernel,
        grid=(B,),
        in_specs=[chunk_spec, chunk_spec,
                  _full((2, N_CHUNK_COLS)), _full((2, N_CHUNK_COLS)),
                  _full((2, CMP_STRIDE, NSA_HEAD_DIM, CMP_HIDDEN)), _full((2, CMP_STRIDE, NSA_HEAD_DIM, CMP_HIDDEN)),
                  _full((N_HID2, NSA_KV_WIDTH)), _full((NSA_KV_WIDTH, N_HID2))],
        out_specs=[pl.BlockSpec((None, n_chunks, NSA_KV_WIDTH), lambda b: (b, 0, 0)),
                   pl.BlockSpec((None, NSA_KV_WIDTH, n_chunks), lambda b: (b, 0, 0))],
        out_shape=[jax.ShapeDtypeStruct((B, n_chunks, NSA_KV_WIDTH), jnp.bfloat16),
                   jax.ShapeDtypeStruct((B, NSA_KV_WIDTH, n_chunks), jnp.bfloat16)],
        compiler_params=_params(1),
        name="compress",
    )(kc, vc, posk, posv, w1k, w1v, w2k, w2v.T)


GQ = NSA_GROUP * Q_BLOCK
TINY = 1e-30
CMP_TAB_ROWS = 512
CMP_TAB_ZERO = 248
CMP_TAB_LOOKUP = (232, 256)
SEL_STEP_SHIFT = 2
KEY_SUPER = Q_BLOCK << SEL_STEP_SHIFT
SEL_TAB_ZERO = KEY_SUPER + Q_BLOCK
SEL_TAB_ROWS = SEL_TAB_ZERO + KEY_SUPER
N_WIN_TILES = WINDOW // Q_BLOCK + 1
Q_PER_STEP = 4
SEL_SUB_CHUNKS = 2
V_ROWS = NSA_HEAD_DIM + BF16_ROWS


def _bucket_np(dist):
    n = np.maximum(dist, 0)
    max_exact = REL_BUCKETS // 2
    nf = np.maximum(n, 1).astype(np.float64)
    large = max_exact + (np.log(nf / max_exact) / math.log(REL_MAX_DISTANCE / max_exact)
                         * (REL_BUCKETS - max_exact)).astype(np.int64)
    large = np.minimum(large, REL_BUCKETS - 1)
    return np.where(n < max_exact, n, large).astype(np.int32)


def _bias_index_tables():
    m = np.arange(Q_BLOCK)[:, None]
    r = np.arange(Q_BLOCK)[None, :]
    diag = np.where(r - m >= 0, _bucket_np(r - m), -1).astype(np.int32)
    off = _bucket_np(Q_BLOCK + r - m)
    jp = np.arange(*CMP_TAB_LOOKUP)[:, None] - CMP_TAB_ZERO
    d_c = r - CMP_STRIDE * jp - (CMP_BLOCK - 1)
    cmp_idx = np.where(d_c >= 0, _bucket_np(d_c), -1).astype(np.int32)
    return diag, off, cmp_idx


def _bias_tables_kernel(rb_ref, diag_idx_ref, off_idx_ref, cmp_idx_ref, sel_ref, win_ref, cmp_ref):
    f32 = jnp.float32

    def lookup(idx, head):
        far = rb_ref[head, REL_BUCKETS - 1]
        acc = jnp.full(idx.shape, NEG_INF, f32)
        for k in range(REL_BUCKETS):
            acc = jnp.where(idx == k, (rb_ref[head, k] - far) * LOG2E, acc)
        return acc

    m_io = lax.broadcasted_iota(jnp.int32, (Q_BLOCK, Q_BLOCK), 0)
    r_io = lax.broadcasted_iota(jnp.int32, (Q_BLOCK, Q_BLOCK), 1)
    neg_tile = jnp.full((Q_BLOCK, Q_BLOCK), NEG_INF, f32)
    lo, hi = CMP_TAB_LOOKUP
    for h in range(NSA_KV_HEADS):
        for g in range(NSA_GROUP):
            head = h * NSA_GROUP + g
            sl = slice(g * Q_BLOCK, (g + 1) * Q_BLOCK)
            far = 0.0
            far_tile = jnp.full((Q_BLOCK, Q_BLOCK), far, f32)
            diag_v = lookup(diag_idx_ref[...], head)
            off_v = lookup(off_idx_ref[...], head)
            n_far = (SEL_TAB_ZERO - Q_BLOCK) // Q_BLOCK
            for t in range(SEL_TAB_ROWS // Q_BLOCK):
                rows = slice(t * Q_BLOCK, (t + 1) * Q_BLOCK)
                tile = far_tile if t < n_far else off_v if t == n_far else diag_v if t == n_far + 1 else neg_tile
                sel_ref[h, rows, sl] = tile
            win_ref[h, 0, :, sl] = diag_v
            win_ref[h, 1, :, sl] = off_v
            for back in range(2, N_WIN_TILES - 1):
                win_ref[h, back, :, sl] = far_tile
            win_ref[h, N_WIN_TILES - 1, :, sl] = jnp.where(r_io < m_io, far, NEG_INF)
            win_ref[h, N_WIN_TILES, :, sl] = neg_tile
            cmp_ref[h, 0:lo, sl] = jnp.full((lo, Q_BLOCK), far, f32)
            cmp_ref[h, lo:hi, sl] = lookup(cmp_idx_ref[...], head)
            cmp_ref[h, hi:CMP_TAB_ROWS, sl] = jnp.full((CMP_TAB_ROWS - hi, Q_BLOCK), NEG_INF, f32)


def _bias_tables(rel_bias):
    diag_idx, off_idx, cmp_idx = _bias_index_tables()
    shapes = [(NSA_KV_HEADS, SEL_TAB_ROWS, GQ), (NSA_KV_HEADS, N_WIN_TILES + 1, Q_BLOCK, GQ),
              (NSA_KV_HEADS, CMP_TAB_ROWS, GQ)]
    return pl.pallas_call(
        _bias_tables_kernel,
        in_specs=[pl.BlockSpec(memory_space=pltpu.SMEM),
                  _full(diag_idx.shape), _full(off_idx.shape), _full(cmp_idx.shape)],
        out_specs=[_full(s) for s in shapes],
        out_shape=[jax.ShapeDtypeStruct(s, jnp.float32) for s in shapes],
        grid=(1,),
        compiler_params=_params(1),
        name="bias_tables",
    )(rel_bias, jnp.asarray(diag_idx), jnp.asarray(off_idx), jnp.asarray(cmp_idx))


def _overlap_np(n_cmp_rows, n_sel):
    cmp_start = np.arange(n_cmp_rows) * CMP_STRIDE
    cmp_end = cmp_start + CMP_BLOCK - 1
    sel_start = np.arange(n_sel) * SEL_BLOCK
    ov = ((cmp_start[None, :] <= sel_start[:, None] + SEL_BLOCK - 1)
          & (cmp_end[None, :] >= sel_start[:, None])).astype(np.float32)
    ov[:, n_cmp_rows - 1] = 0.0
    return ov


def _tile4(a):
    return jnp.concatenate([a] * NSA_GROUP, axis=1)


def _select_blocks(score, score_ref, n_top, hooks=()):
    n_sel = score.shape[0]
    score_ref[...] = score
    n_grp = n_sel // SUBLANES
    grp = [score[SUBLANES * v:SUBLANES * (v + 1), :] for v in range(n_grp)]
    cnt = [jnp.zeros((SUBLANES, Q_BLOCK), jnp.int32) for _ in range(n_grp)]
    sub_io = lax.broadcasted_iota(jnp.int32, (SUBLANES, Q_BLOCK), 0)
    hook_at = {(i * n_sel) // len(hooks): hk for i, hk in enumerate(hooks)} if hooks else {}
    for jp in range(n_sel):
        if jp in hook_at:
            hook_at[jp]()
        row = score_ref[jp:jp + 1, :]
        for v in range(n_grp):
            if SUBLANES * v > jp:
                inc = (row >= grp[v]).astype(jnp.int32)
            elif SUBLANES * (v + 1) - 1 < jp:
                inc = (row > grp[v]).astype(jnp.int32)
            else:
                tie = (sub_io > jp - SUBLANES * v).astype(jnp.int32)
                inc = jnp.where(row > grp[v], 1, jnp.where(row == grp[v], tie, 0))
            cnt[v] = cnt[v] + inc
    return [jnp.where(cnt[v] < n_top, 0.0, NEG_INF) for v in range(n_grp)]


def _nsa_kernel(*refs):
    step = pl.program_id(1)
    tail = []
    for blk in range(Q_PER_STEP):
        tail = _nsa_block(step * Q_PER_STEP + blk, blk, tail, *refs)
    for part in tail:
        part()


def _nsa_block(c, blk, deferred, q_ref, kcmp_ref, vcmpT_ref, ksl_ref, vslT_ref, kwn_ref, vwnT_ref, gate_ref,
               seltab_ref, wintab_ref, cmptab_ref, ovl_ref, blkind_ref, out_ref, score_ref, selb_ref, sbufs_ref,
               swin_ref):
    sbuf_ref, sbuf2_ref = sbufs_ref.at[0], sbufs_ref.at[1]
    n_cmp = kcmp_ref.shape[0]
    n_sel = ovl_ref.shape[0]
    n_top = min(N_SELECT, n_sel)
    f32 = jnp.float32
    bf16 = jnp.bfloat16
    DH = NSA_HEAD_DIM
    heads = range(NSA_KV_HEADS)

    q = q_ref[blk]
    zq = jnp.zeros((DH, GQ), bf16)
    qcat, qpad = [], []
    for h in heads:
        qcat.append(jnp.concatenate([q[(h * NSA_GROUP + g) * DH:(h * NSA_GROUP + g + 1) * DH, :]
                                     for g in range(NSA_GROUP)], axis=1))
        qpad.append(jnp.concatenate([qcat[h], zq] if h == 0 else [zq, qcat[h]], axis=0))

    backs = list(range(N_WIN_TILES))
    kts = [jnp.maximum(c - back, 0) for back in backs]
    slots = [jnp.where(c >= back, back, N_WIN_TILES) for back in backs]
    m_win = [jnp.full((1, GQ), NEG_INF, f32) for _ in heads]

    def win_score(back, h):
        key0 = pl.multiple_of(kts[back] * Q_BLOCK, Q_BLOCK)
        s = _mm(kwn_ref[pl.ds(key0, Q_BLOCK), :], qpad[h]) + wintab_ref[h, slots[back]]
        swin_ref[h, back * Q_BLOCK:(back + 1) * Q_BLOCK, :] = s
        return jnp.max(s, axis=0, keepdims=True)

    cmp_off = pl.multiple_of(CMP_TAB_ZERO - (Q_BLOCK // CMP_STRIDE) * c, SUBLANES)
    kcmp = kcmp_ref[...]
    j_io = lax.broadcasted_iota(jnp.int32, (n_sel, Q_BLOCK), 0)
    r_io = lax.broadcasted_iota(jnp.int32, (n_sel, Q_BLOCK), 1)
    cur = (Q_BLOCK // SEL_BLOCK) * c + (r_io >= SEL_BLOCK).astype(jnp.int32)
    forced = (j_io == 0) | (j_io == cur) | (j_io == cur - 1)
    visible = j_io <= cur
    o_c = []
    for h in heads:
        tab = cmptab_ref[h, pl.ds(cmp_off, n_cmp), :]
        s = _mm(kcmp, qpad[h]) + tab
        m = jnp.maximum(jnp.max(s, axis=0, keepdims=True), 0.1 * NEG_INF)
        p = jnp.exp2(s - m)
        l = jnp.sum(p, axis=0, keepdims=True)
        pn = p * (1.0 / jnp.maximum(l, TINY))
        o_c.append(_mm(vcmpT_ref[h * DH:(h + 1) * DH, :], pn.astype(bf16)))
        psum = pn[:, 0:Q_BLOCK]
        for g in range(1, NSA_GROUP):
            psum = psum + pn[:, g * Q_BLOCK:(g + 1) * Q_BLOCK]
        imp = jnp.dot(ovl_ref[...], psum, precision=lax.Precision.HIGHEST,
                      preferred_element_type=f32)
        score = jnp.where(forced, FORCED_SCORE, jnp.where(visible, imp, -1.0))

        def hook(back, h=h):
            m_win[h] = jnp.maximum(m_win[h], win_score(back, h))

        hooks = [functools.partial(hook, back) for back in backs]
        if h == 0:
            for i, part in enumerate(deferred):
                hooks.insert(2 * i + 1, part)
        rows = _select_blocks(score, score_ref.at[h], n_top, hooks=hooks)
        for v, mask_rows in enumerate(rows):
            selb_ref[h, SUBLANES * v:SUBLANES * (v + 1), :] = mask_rows

    def ones_rows(n_keys):
        return jnp.ones((V_ROWS - DH, n_keys), bf16)

    acc_w = [jnp.zeros((V_ROWS, GQ), f32) for _ in heads]

    def win_value(back):
        for h in heads:
            p = jnp.exp2(swin_ref[h, back * Q_BLOCK:(back + 1) * Q_BLOCK, :] - m_win[h]).astype(bf16)
            vT = jnp.concatenate([vwnT_ref[kts[back], h * DH:(h + 1) * DH, :], ones_rows(Q_BLOCK)], axis=0)
            acc_w[h] = acc_w[h] + _mm(vT, p)

    blocks_per_step = KEY_SUPER // SEL_BLOCK
    tiles_per_step = KEY_SUPER // Q_BLOCK

    sub = KEY_SUPER // SEL_SUB_CHUNKS
    blocks_per_sub = sub // SEL_BLOCK
    tiles_per_sub = sub // Q_BLOCK

    own_lanes = [(lax.broadcasted_iota(jnp.int32, (sub, NSA_KV_WIDTH), 1) >= DH) == bool(h) for h in heads]

    def q_with_mask_rows(j, h):
        blk0 = pl.multiple_of(j * blocks_per_step, blocks_per_step)
        rows = jnp.concatenate([_tile4(selb_ref[h, pl.ds(blk0, blocks_per_step), :]),
                                jnp.zeros((DH - blocks_per_step, GQ), f32)], axis=0).astype(bf16)
        return jnp.concatenate([qcat[h], rows] if h == 0 else [rows, qcat[h]], axis=0)

    def score_chunk(j, ci, h, q_aug, buf_ref, near=True):
        key0 = pl.multiple_of(j * KEY_SUPER + ci * sub, sub)
        k_aug = jnp.where(own_lanes[h], ksl_ref[pl.ds(key0, sub), :], blkind_ref[h, ci * sub:(ci + 1) * sub, :])
        s = _mm(k_aug, q_aug)
        if near:
            tab_off = pl.multiple_of(
                jnp.maximum(j * KEY_SUPER - c * Q_BLOCK + SEL_TAB_ZERO, 0) + ci * sub, Q_BLOCK)
            s = s + seltab_ref[h, pl.ds(tab_off, sub), :]
        buf_ref[h, ci * sub:(ci + 1) * sub, :] = s
        return jnp.max(s, axis=0, keepdims=True)

    def value_chunk(j, ci, h, m_h, buf_ref):
        p = jnp.exp2(buf_ref[h, ci * sub:(ci + 1) * sub, :] - m_h).astype(bf16)
        vT = jnp.concatenate([vslT_ref[j * tiles_per_step + ci * tiles_per_sub + i, h * DH:(h + 1) * DH, :]
                              for i in range(tiles_per_sub)], axis=1)
        return _mm(jnp.concatenate([vT, ones_rows(sub)], axis=0), p)

    def values(j, m_old, m_cur, acc, src_ref, before_chunk=None):
        acc = [jnp.exp2(m_old[h] - m_cur[h]) * acc[h] for h in heads]
        for ci in range(SEL_SUB_CHUNKS):
            if before_chunk is not None:
                before_chunk(ci)
            for h in heads:
                acc[h] = acc[h] + value_chunk(j, ci, h, m_cur[h], src_ref)
        return tuple(acc)

    def sel_step(j, carry, src_ref, dst_ref, near=True):
        m_old, m_cur, acc = carry
        m_run = list(m_cur)
        q_aug = [q_with_mask_rows(j + 1, h) for h in heads]

        def scores(ci):
            for h in heads:
                m_run[h] = jnp.maximum(m_run[h], score_chunk(j + 1, ci, h, q_aug[h], dst_ref, near))

        acc = values(j, m_old, m_cur, acc, src_ref, before_chunk=scores)
        return m_cur, tuple(m_run), acc

    def sel_pair(i, carry, near):
        carry = sel_step(2 * i, carry, sbuf_ref, sbuf2_ref, near)
        return sel_step(2 * i + 1, carry, sbuf2_ref, sbuf_ref, near)

    n_steps = lax.shift_right_logical(c, SEL_STEP_SHIFT) + 1
    m_init = tuple(jnp.full((1, GQ), NEG_INF, f32) for _ in heads)
    acc_init = tuple(jnp.zeros((V_ROWS, GQ), f32) for _ in heads)
    m_first = list(m_init)
    win_order = list(backs)
    q_aug0 = [q_with_mask_rows(0, h) for h in heads]
    for ci in range(SEL_SUB_CHUNKS):
        for _ in range(-(-N_WIN_TILES // SEL_SUB_CHUNKS)):
            if win_order:
                win_value(win_order.pop(0))
        for h in heads:
            m_first[h] = jnp.maximum(m_first[h], score_chunk(0, ci, h, q_aug0[h], sbuf_ref))
    while win_order:
        win_value(win_order.pop(0))
    o_w = [a[0:DH, :] * (1.0 / a[DH:DH + 1, :]) for a in acc_w]
    n_piped = n_steps - 1
    odd = n_piped & 1
    n_far = lax.shift_right_logical(jnp.maximum(c - 1, 0), SEL_STEP_SHIFT)
    far_pairs = lax.shift_right_logical(jnp.maximum(n_far - 1, 0), 1)
    carry = lax.fori_loop(0, far_pairs, functools.partial(sel_pair, near=False),
                          (m_init, tuple(m_first), acc_init))
    carry = lax.fori_loop(far_pairs, lax.shift_right_logical(n_piped, 1), functools.partial(sel_pair, near=True),
                          carry)
    carry = lax.fori_loop(0, odd, lambda _, cr: sel_step(n_piped - 1, cr, sbuf_ref, sbuf2_ref), carry)
    m_old, m_cur, acc = carry
    acc = [jnp.exp2(m_old[h] - m_cur[h]) * acc[h] for h in heads]
    last_ref = sbufs_ref.at[odd]

    def last_values(ci):
        for h in heads:
            acc[h] = acc[h] + value_chunk(n_piped, ci, h, m_cur[h], last_ref)

    def combine():
        o_s = [a[0:DH, :] * (1.0 / jnp.maximum(a[DH:DH + 1, :], TINY)) for a in acc]
        for h in heads:
            ys = []
            for g in range(NSA_GROUP):
                sl = slice(g * Q_BLOCK, (g + 1) * Q_BLOCK)
                row0 = h * NSA_GROUP + g
                gates = [jax.nn.sigmoid(gate_ref[blk, kind * NSA_HEADS + row0:kind * NSA_HEADS + row0 + 1, :])
                         for kind in range(3)]
                ys.append(gates[0] * o_c[h][:, sl] + gates[1] * o_s[h][:, sl] + gates[2] * o_w[h][:, sl])
            yT = jnp.concatenate(ys, axis=0)
            for half in range(2):
                col = (2 * h + half) * LANES
                out_ref[blk * Q_BLOCK:(blk + 1) * Q_BLOCK, col:col + LANES] = (
                    yT[half * LANES:(half + 1) * LANES, :].T.astype(out_ref.dtype))

    return [functools.partial(last_values, ci) for ci in range(SEL_SUB_CHUNKS)] + [combine]


def _nsa(qT, kcmp, vcmpT, ksl, vslT, kwn, vwnT, smallT, tables, B, S):
    assert S % KEY_SUPER == 0
    nq = S // Q_BLOCK
    n_steps = nq // Q_PER_STEP
    n_cmp = S // CMP_STRIDE
    n_sel = S // SEL_BLOCK
    seltab, wintab, cmptab = tables
    ovl = jnp.asarray(_overlap_np(n_cmp, n_sel))
    key_blk = np.arange(KEY_SUPER)[:, None] // SEL_BLOCK
    lane = np.arange(NSA_KV_WIDTH)[None, :]
    blkind = jnp.asarray(np.stack([lane - NSA_HEAD_DIM == key_blk, lane == key_blk]), jnp.bfloat16)
    ksl3 = ksl.reshape(B, S, NSA_KV_WIDTH)
    kwn3 = kwn.reshape(B, S, NSA_KV_WIDTH)
    vslT4 = vslT.reshape(B, nq, NSA_KV_WIDTH, Q_BLOCK)
    vwnT4 = vwnT.reshape(B, nq, NSA_KV_WIDTH, Q_BLOCK)
    k_spec = pl.BlockSpec((None, S, NSA_KV_WIDTH), lambda b, c: (b, 0, 0))
    vT_spec = pl.BlockSpec((None, nq, NSA_KV_WIDTH, Q_BLOCK), lambda b, c: (b, 0, 0, 0))
    const = lambda a: pl.BlockSpec(a.shape, lambda b, c: (0,) * a.ndim)
    return pl.pallas_call(
        _nsa_kernel,
        grid=(B, n_steps),
        in_specs=[pl.BlockSpec((Q_PER_STEP, NSA_WIDTH, Q_BLOCK), lambda b, c: (b * n_steps + c, 0, 0)),
                  pl.BlockSpec((None, n_cmp, NSA_KV_WIDTH), lambda b, c: (b, 0, 0)),
                  pl.BlockSpec((None, NSA_KV_WIDTH, n_cmp), lambda b, c: (b, 0, 0)),
                  k_spec, vT_spec, k_spec, vT_spec,
                  pl.BlockSpec((Q_PER_STEP, N_GATE_ROWS, Q_BLOCK), lambda b, c: (b * n_steps + c, 0, 0)),
                  const(seltab), const(wintab), const(cmptab), const(ovl), const(blkind)],
        out_specs=pl.BlockSpec((Q_PER_STEP * Q_BLOCK, NSA_WIDTH), lambda b, c: (b * n_steps + c, 0)),
        out_shape=jax.ShapeDtypeStruct((B * S, NSA_WIDTH), jnp.bfloat16),
        scratch_shapes=[pltpu.VMEM((NSA_KV_HEADS, n_sel, Q_BLOCK), jnp.float32),
                        pltpu.VMEM((NSA_KV_HEADS, n_sel, Q_BLOCK), jnp.float32),
                        pltpu.VMEM((2, NSA_KV_HEADS, KEY_SUPER, GQ), jnp.float32),
                        pltpu.VMEM((NSA_KV_HEADS, N_WIN_TILES * Q_BLOCK, GQ), jnp.float32)],
        compiler_params=_params(2),
        name="nsa",
    )(qT, kcmp, vcmpT, ksl3, vslT4, kwn3, vwnT4, smallT, seltab, wintab, cmptab, ovl, blkind)


CONV_TILE = 256
I_ROW = 24
C_ROWS = MLSTM_V_DIM + BF16_ROWS


def _log_sigmoid(x):
    return jnp.minimum(x, 0.0) - jnp.log(1.0 + jnp.exp(-jnp.abs(x)))


def _mlstm_kernel(mqk_ref, mvT_ref, moT_ref, gates_ref, gb_ref, tri_ref, last_ref, convw_ref, convb_ref,
                  norm_ref, out_ref, qk_ref, rows_ref):
    S = mqk_ref.shape[0]
    nt = S // LANES
    L = MLSTM_CHUNK
    f32 = jnp.float32
    bf16 = jnp.bfloat16
    kscale_row = jnp.where(lax.broadcasted_iota(jnp.int32, (1, 2 * MLSTM_QK_WIDTH), 1) < MLSTM_QK_WIDTH,
                           1.0, MLSTM_QK_DIM ** -0.5)

    def conv_body(i, _):
        t0 = pl.multiple_of(i * CONV_TILE, CONV_TILE)
        cur = mqk_ref[pl.ds(t0, CONV_TILE), :].astype(f32)
        prev_start = pl.multiple_of(jnp.maximum(t0 - BF16_ROWS, 0), BF16_ROWS)
        prev = (mqk_ref[pl.ds(prev_start, BF16_ROWS), :].astype(f32)[BF16_ROWS - SUBLANES:, :]
                * jnp.where(i > 0, 1.0, 0.0))
        ext = jnp.concatenate([prev, cur], axis=0)
        y = convb_ref[...]
        for j in range(CONV_WIDTH):
            lo = SUBLANES - (CONV_WIDTH - 1) + j
            y = y + convw_ref[j:j + 1, :] * ext[lo:lo + CONV_TILE, :]
        y = y * jax.nn.sigmoid(y) * kscale_row
        qk_ref[pl.ds(t0, CONV_TILE), :] = y.astype(bf16)
        return 0

    lax.fori_loop(0, S // CONV_TILE, conv_body, 0)

    H = MLSTM_HEADS
    G8 = 2 * H
    n_rows = nt * G8
    a3 = gates_ref[:, I_ROW:I_ROW + G8, :] + gb_ref[...][None]
    is_f = lax.broadcasted_iota(jnp.int32, a3.shape, 1) >= H
    x = jnp.where(is_f, _log_sigmoid(a3), a3).reshape(n_rows, LANES)
    bcum = jnp.dot(x, tri_ref[...], precision=lax.Precision.HIGHEST, preferred_element_type=f32)
    b_rows = pltpu.roll(bcum, n_rows - H, 0)
    g_rows = x - b_rows
    pos = lax.broadcasted_iota(jnp.int32, (n_rows, LANES), 1) & (L - 1)

    def chunk_cummax(a):
        shift = 1
        while shift < L:
            a = jnp.where(pos >= shift, jnp.maximum(a, pltpu.roll(a, shift, 1)), a)
            shift *= 2
        return a

    def chunk_last(a):
        return jnp.dot(a, last_ref[...], precision=lax.Precision.HIGHEST, preferred_element_type=f32)

    bl_rows = chunk_last(b_rows)
    rows_ref[0] = g_rows
    rows_ref[1] = chunk_cummax(g_rows)
    rows_ref[2] = b_rows
    rows_ref[3] = bl_rows
    rows_ref[4] = chunk_last(chunk_cummax(bl_rows + g_rows))

    s_io = lax.broadcasted_iota(jnp.int32, (LANES, LANES), 0)
    t_io = lax.broadcasted_iota(jnp.int32, (LANES, LANES), 1)
    causal = (s_io <= t_io) & ((s_io >= L) == (t_io >= L))
    lane = lax.broadcasted_iota(jnp.int32, (1, LANES), 1)
    in_chunk = [lane < L, lane >= L]
    QD, VD = MLSTM_QK_DIM, MLSTM_V_DIM
    ones_aug = jnp.ones((C_ROWS - VD, LANES), bf16)
    head_lanes = [(lax.broadcasted_iota(jnp.int32, (LANES, LANES), 1) >= QD) == bool(par) for par in range(2)]

    def swap_halves(row):
        return pltpu.roll(row, L, 1)

    def tile_body(i, state):
        t0 = pl.multiple_of(i * LANES, LANES)
        r0 = pl.multiple_of(i * G8, G8)
        qk = qk_ref[pl.ds(t0, LANES), :]
        qk32 = qk.astype(f32)
        pairsT = [qk32[:, j * LANES:(j + 1) * LANES].T for j in range(4)]
        g8, cm8, b8, bl8, wm8 = [rows_ref[j, pl.ds(r0, G8), :] for j in range(5)]
        new_state = []
        for hh in range(H):
            pair, par = hh // 2, hh % 2
            g_r, cm_r, b_r, bl_r, wm_r = [a[hh:hh + 1, :] for a in (g8, cm8, b8, bl8, wm8)]
            caug, m_in = state[hh]
            qT = pairsT[pair][par * QD:(par + 1) * QD, :]
            kT = pairsT[2 + pair][par * QD:(par + 1) * QD, :]
            kpair = qk[:, (2 + pair) * LANES:(3 + pair) * LANES]
            qmask = jnp.where(head_lanes[par], qk[:, pair * LANES:(pair + 1) * LANES], jnp.zeros((), bf16))
            vaug = jnp.concatenate([mvT_ref[i, hh * VD:(hh + 1) * VD, :], ones_aug], axis=0)

            m_mid = swap_halves(jnp.maximum(bl_r + m_in, wm_r))
            m_prev = jnp.where(in_chunk[0], m_in, m_mid)
            m_next = jnp.maximum(bl_r + m_prev, wm_r)
            m_intra = b_r + cm_r
            m_inter = b_r + m_prev
            m_t = jnp.maximum(m_inter, m_intra)
            e_intra = jnp.exp(m_intra - m_t)
            e_inter = jnp.exp(m_inter - m_t)
            decay = jnp.exp(bl_r + m_prev - m_next)
            inject = jnp.exp(wm_r - m_next)

            g_mat = jnp.broadcast_to(g_r, (LANES, LANES)).T
            w = jnp.exp(jnp.where(causal, g_mat - cm_r, NEG_INF))
            st = _nt(kpair, qmask) * (w * e_intra)
            y = _mm(vaug, st.astype(bf16))
            kw = kT * jnp.exp(bl_r + g_r - wm_r)
            cs = caug
            for p in range(LANES // L):
                qs = jnp.where(in_chunk[p], qT * e_inter, 0.0).astype(bf16)
                y = y + _mm(cs.astype(bf16), qs)
                u = _nt(vaug, jnp.where(in_chunk[p], kw, 0.0).astype(bf16))
                dec = decay if p == 0 else swap_halves(decay)
                inj = inject if p == 0 else swap_halves(inject)
                cs = dec[:, 0:QD] * cs + inj[:, 0:QD] * u
            m_out = jnp.where(in_chunk[1], m_next, swap_halves(m_next))
            new_state.append((cs, m_out))

            den = y[VD:VD + 1, :]
            hT = y[0:VD, :] * (1.0 / jnp.maximum(jnp.abs(den), jnp.exp(-m_t)))
            sl = slice(hh * VD, (hh + 1) * VD)
            hn = hT * lax.rsqrt(jnp.mean(hT * hT, axis=0, keepdims=True) + NORM_EPS) * norm_ref[sl, :]
            yT = jax.nn.sigmoid(moT_ref[i, sl, :].astype(f32)) * hn
            out_ref[pl.ds(t0, LANES), sl] = yT.T.astype(out_ref.dtype)
        return tuple(new_state)

    init = tuple((jnp.zeros((C_ROWS, QD), f32), jnp.zeros((1, LANES), f32)) for _ in range(H))
    lax.fori_loop(0, nt, tile_body, init, unroll=4)


def _mlstm(mqk, mvT, moT, smallT, conv_w, conv_b, gate_bias, mlstm_norm, B, S):
    nt = S // LANES
    gb = jnp.broadcast_to(gate_bias.astype(jnp.float32).reshape(2 * MLSTM_HEADS, 1), (2 * MLSTM_HEADS, LANES))
    norm_cols = jnp.broadcast_to(mlstm_norm.astype(jnp.float32).reshape(MLSTM_WIDTH, 1), (MLSTM_WIDTH, LANES))
    lane = np.arange(LANES)
    same_chunk = lane[:, None] // MLSTM_CHUNK == lane[None, :] // MLSTM_CHUNK
    tri = (same_chunk & (lane[:, None] <= lane[None, :])).astype(np.float32)
    last = (same_chunk & (lane[:, None] % MLSTM_CHUNK == MLSTM_CHUNK - 1)).astype(np.float32)
    seq = lambda w: pl.BlockSpec((S, w), lambda b: (b, 0))
    tiles = lambda rows: pl.BlockSpec((nt, rows, LANES), lambda b: (b, 0, 0))
    return pl.pallas_call(
        _mlstm_kernel,
        grid=(B,),
        in_specs=[seq(2 * MLSTM_QK_WIDTH), tiles(MLSTM_WIDTH), tiles(MLSTM_WIDTH), tiles(N_GATE_ROWS),
                  _full((2 * MLSTM_HEADS, LANES)), _full((LANES, LANES)), _full((LANES, LANES)),
                  _full((CONV_WIDTH, 2 * MLSTM_QK_WIDTH)), _full((1, 2 * MLSTM_QK_WIDTH)),
                  _full((MLSTM_WIDTH, LANES))],
        out_specs=seq(MLSTM_WIDTH),
        out_shape=jax.ShapeDtypeStruct((B * S, MLSTM_WIDTH), jnp.bfloat16),
        scratch_shapes=[pltpu.VMEM((S, 2 * MLSTM_QK_WIDTH), jnp.bfloat16),
                        pltpu.VMEM((5, nt * 2 * MLSTM_HEADS, LANES), jnp.float32)],
        compiler_params=_params(1),
        name="mlstm",
    )(mqk, mvT, moT, smallT, gb, jnp.asarray(tri), jnp.asarray(last), conv_w, conv_b.reshape(1, -1), norm_cols)


def _mem_kv_kernel(mem_ref, g_ref, w_ref, k_ref, v_ref):
    mn = _rms(mem_ref[...], g_ref[...]).astype(jnp.bfloat16)
    k_ref[...] = _mm(mn, w_ref[:, :D_MODEL]).astype(k_ref.dtype)
    v_ref[...] = _mm(mn, w_ref[:, D_MODEL:]).astype(v_ref.dtype)


def _mem_kv(mem, gain, w_xkv):
    B, M, _ = mem.shape
    spec = pl.BlockSpec((None, M, D_MODEL), lambda b: (b, 0, 0))
    return pl.pallas_call(
        _mem_kv_kernel,
        grid=(B,),
        in_specs=[spec, _full((1, D_MODEL)), _full((D_MODEL, 2 * D_MODEL))],
        out_specs=[spec, spec],
        out_shape=[jax.ShapeDtypeStruct((B, M, D_MODEL), jnp.bfloat16)] * 2,
        compiler_params=_params(1),
        name="mem_kv",
    )(mem, gain, w_xkv.astype(jnp.bfloat16))


TM_X = 1024
X_HALVES = 2


def _mix_xattn_kernel(ynsa_ref, yml_ref, x_ref, wout_ref, gpost_ref, gpre_ref, wq_ref, k_ref, v_ref,
                      wo_ref, gpost2_ref, out_ref):
    bf16 = jnp.bfloat16
    halves = [slice(i * (TM_X // X_HALVES), (i + 1) * (TM_X // X_HALVES)) for i in range(X_HALVES)]
    y = [_mm(ynsa_ref[r, :], wout_ref[:NSA_WIDTH, :]) + _mm(yml_ref[r, :], wout_ref[NSA_WIDTH:, :]) for r in halves]
    x1 = [x_ref[r, :] + _rms(y[i], gpost_ref[...]) for i, r in enumerate(halves)]
    h2 = [_rms(x1[i], gpre_ref[...]).astype(bf16) for i in range(X_HALVES)]
    q = [(_mm(h2[i], wq_ref[...]) * (XATTN_HEAD_DIM ** -0.5)).astype(bf16) for i in range(X_HALVES)]
    outs = [[] for _ in range(X_HALVES)]
    for hh in range(XATTN_HEADS):
        sl = slice(hh * XATTN_HEAD_DIM, (hh + 1) * XATTN_HEAD_DIM)
        s = [_nt(q[i][:, sl], k_ref[:, sl]) for i in range(X_HALVES)]
        for i in range(X_HALVES):
            p = jnp.exp(s[i] - jnp.max(s[i], axis=1, keepdims=True))
            l = jnp.sum(p, axis=1, keepdims=True)
            outs[i].append((_mm(p.astype(bf16), v_ref[:, sl]) * (1.0 / l)).astype(bf16))
    y2 = [_mm(jnp.concatenate(outs[i], axis=1), wo_ref[...]) for i in range(X_HALVES)]
    for i, r in enumerate(halves):
        out_ref[r, :] = x1[i] + _rms(y2[i], gpost2_ref[...])


def _mix_xattn(ynsa, yml, x2d, w_out, g_post, g_pre, w_xq, kx, vx, w_xo, g_post2, B, S):
    nt = S // TM_X
    M = kx.shape[1]
    tok = lambda w: pl.BlockSpec((TM_X, w), lambda b, i: (b * nt + i, 0))
    mem_spec = pl.BlockSpec((None, M, D_MODEL), lambda b, i: (b, 0, 0))
    sq = _full((D_MODEL, D_MODEL))
    row = _full((1, D_MODEL))
    bf = lambda w: w.astype(jnp.bfloat16)
    return pl.pallas_call(
        _mix_xattn_kernel,
        grid=(B, nt),
        in_specs=[tok(NSA_WIDTH), tok(MLSTM_WIDTH), tok(D_MODEL), sq, row, row, sq, mem_spec, mem_spec,
                  sq, row],
        out_specs=tok(D_MODEL),
        out_shape=jax.ShapeDtypeStruct((B * S, D_MODEL), jnp.float32),
        compiler_params=_params(2),
        name="mix_xattn",
    )(ynsa, yml, x2d, bf(w_out), g_post, g_pre, bf(w_xq), kx, vx, bf(w_xo), g_post2)


TM_F = 1024
F_HALVES = 2


def _ffn_kernel(x_ref, gpre_ref, wgu_ref, wd_ref, gpost_ref, out_ref, acc_ref):
    bf16 = jnp.bfloat16
    n_chunks = D_FF // F_TILE
    half = TM_F // F_HALVES

    def pre(i):
        return _rms(x_ref[i * half:(i + 1) * half, :], gpre_ref[...]).astype(bf16)

    def chunk(i, h, j):
        rows = slice(i * half, (i + 1) * half)
        cols = slice(j * F_TILE, (j + 1) * F_TILE)
        g = _mm(h, wgu_ref[:, cols])
        u = _mm(h, wgu_ref[:, D_FF + j * F_TILE:D_FF + (j + 1) * F_TILE])
        act = (g * jax.nn.sigmoid(g) * u).astype(bf16)
        down = _mm(act, wd_ref[cols, :])
        if j == 0:
            acc_ref[rows, :] = down
        else:
            acc_ref[rows, :] += down

    def post(i):
        rows = slice(i * half, (i + 1) * half)
        out_ref[rows, :] = x_ref[rows, :] + _rms(acc_ref[rows, :], gpost_ref[...])

    h = pre(0)
    for i in range(F_HALVES):
        chunk(i, h, 0)
        if i > 0:
            post(i - 1)
        h_next = pre(i + 1) if i + 1 < F_HALVES else None
        for j in range(1, n_chunks):
            chunk(i, h, j)
        h = h_next
    post(F_HALVES - 1)


def _ffn(x2d, g_pre, w_gate_up, w_down, g_post):
    T = x2d.shape[0]
    tok = pl.BlockSpec((TM_F, D_MODEL), lambda i: (i, 0))
    row = _full((1, D_MODEL))
    return pl.pallas_call(
        _ffn_kernel,
        grid=(T // TM_F,),
        in_specs=[tok, row, _full((D_MODEL, 2 * D_FF), buffers=1), _full((D_FF, D_MODEL), buffers=1), row],
        out_specs=tok,
        out_shape=jax.ShapeDtypeStruct((T, D_MODEL), jnp.float32),
        scratch_shapes=[pltpu.VMEM((TM_F, D_MODEL), jnp.float32)],
        compiler_params=_params(1),
        name="ffn",
    )(x2d, g_pre, w_gate_up.astype(jnp.bfloat16), w_down.astype(jnp.bfloat16), g_post)


def _layer(x, mem, rel_bias, mix_norm_pre, w_in, cmp_pos_k, cmp_pos_v, cmp_w1_k, cmp_w2_k, cmp_w1_v,
           cmp_w2_v, conv_w, conv_b, mlstm_gate_bias, mlstm_norm, w_out, mix_norm_post, xattn_norm_pre,
           mem_norm, w_xq, w_xkv, w_xo, xattn_norm_post, ffn_norm_pre, w_gate_up, w_down, ffn_norm_post):
    B, S, _ = x.shape
    row = lambda g: g.reshape(1, -1).astype(jnp.float32)
    x2d = x.reshape(B * S, D_MODEL)
    w_tok, w_feat = _in_proj_weights(w_in)
    (kc, vc, ksl, kwn, mqk, qT, vslT, vwnT, mvT, moT, smallT) = _in_proj(
        x2d, row(mix_norm_pre), w_tok, w_feat)
    kcmp, vcmpT = _compress(kc, vc, cmp_pos_k, cmp_pos_v, cmp_w1_k, cmp_w2_k, cmp_w1_v, cmp_w2_v, B, S)
    tables = _bias_tables(rel_bias.astype(jnp.float32))
    ynsa = _nsa(qT, kcmp, vcmpT, ksl, vslT, kwn, vwnT, smallT, tables, B, S)
    yml = _mlstm(mqk, mvT, moT, smallT, conv_w, conv_b, mlstm_gate_bias, mlstm_norm, B, S)
    kx, vx = _mem_kv(mem, row(mem_norm), w_xkv)
    x2 = _mix_xattn(ynsa, yml, x2d, w_out, row(mix_norm_post), row(xattn_norm_pre), w_xq, kx, vx, w_xo,
                    row(xattn_norm_post), B, S)
    x3 = _ffn(x2, row(ffn_norm_pre), w_gate_up, w_down, row(ffn_norm_post))
    return x3.reshape(B, S, D_MODEL)


def kernel(x, mem, rel_bias, mix_norm_pre, w_in, cmp_pos_k, cmp_pos_v, cmp_w1_k, cmp_w2_k, cmp_w1_v, cmp_w2_v,
           conv_w, conv_b, mlstm_gate_bias, mlstm_norm, w_out, mix_norm_post, xattn_norm_pre, mem_norm, w_xq,
           w_xkv, w_xo, xattn_norm_post, ffn_norm_pre, w_gate_up, w_down, ffn_norm_post):
    depth = w_in.shape[0]
    for l in range(depth):
        x = _layer(x, mem, rel_bias, mix_norm_pre[l], w_in[l], cmp_pos_k[l], cmp_pos_v[l], cmp_w1_k[l],
                   cmp_w2_k[l], cmp_w1_v[l], cmp_w2_v[l], conv_w[l], conv_b[l], mlstm_gate_bias[l],
                   mlstm_norm[l], w_out[l], mix_norm_post[l], xattn_norm_pre[l], mem_norm[l], w_xq[l],
                   w_xkv[l], w_xo[l], xattn_norm_post[l], ffn_norm_pre[l], w_gate_up[l], w_down[l],
                   ffn_norm_post[l])
    return x
```

```python
import functools
import math

import numpy as np
import jax
import jax.numpy as jnp
from jax import lax
from jax.experimental import pallas as pl
from jax.experimental.pallas import tpu as pltpu

D_MODEL = 1024
NSA_WIDTH = 512
NSA_HEAD_DIM = 64
NSA_HEADS = 8
NSA_KV_HEADS = 2
NSA_GROUP = 4
NSA_KV_WIDTH = 128
CMP_STRIDE = 16
CMP_BLOCK = 32
CMP_HIDDEN = 256
SEL_BLOCK = 64
N_SELECT = 16
WINDOW = 512
Q_BLOCK = 128
FORCED_SCORE = 1.0e4
MLSTM_WIDTH = 512
MLSTM_HEADS = 4
MLSTM_V_DIM = 128
MLSTM_QK_DIM = 64
MLSTM_QK_WIDTH = 256
MLSTM_CHUNK = 64
CONV_WIDTH = 4
REL_BUCKETS = 32
REL_MAX_DISTANCE = 128
XATTN_HEADS = 4
XATTN_HEAD_DIM = 256
D_FF = 2816
NORM_EPS = 1e-6
NEG_INF = -1.0e30
LOG2E = math.log2(math.e)

IN_SIZES = (NSA_WIDTH,) + (NSA_KV_WIDTH,) * 6 + (NSA_HEADS * 3, MLSTM_QK_WIDTH, MLSTM_QK_WIDTH,
                                                 MLSTM_WIDTH, MLSTM_HEADS, MLSTM_HEADS, MLSTM_WIDTH)
IN_OFFSETS = tuple(int(o) for o in np.cumsum((0,) + IN_SIZES)[:-1])

LANES = 128
SUBLANES = 8
BF16_ROWS = 16
VMEM_LIMIT_BYTES = 56 * 1024 * 1024

N_GATE_ROWS = 32
F_TILE = 256


def _rms(x, gain):
    return x * lax.rsqrt(jnp.mean(x * x, axis=-1, keepdims=True) + NORM_EPS) * gain


def _nt(a, b):
    return lax.dot_general(a, b, (((1,), (1,)), ((), ())), preferred_element_type=jnp.float32)


def _mm(a, b):
    return jnp.dot(a, b, preferred_element_type=jnp.float32)


def _params(n_axes, flags=None):
    return pltpu.CompilerParams(dimension_semantics=("arbitrary",) * n_axes,
                                vmem_limit_bytes=VMEM_LIMIT_BYTES, flags=flags)


def _full(shape, buffers=None):
    nd = len(shape)
    mode = None if buffers is None else pl.Buffered(buffers)
    return pl.BlockSpec(shape, lambda *_: (0,) * nd, pipeline_mode=mode)


TM_IN = 1024
IN_HALVES = 2
TOK_DOT_WIDTH = 512
_TOK_GROUPS = (("kc", 128, jnp.float32), ("vc", 128, jnp.float32), ("ksl", 128, jnp.bfloat16),
               ("kwn", 128, jnp.bfloat16), ("mqk", 512, jnp.bfloat16))
_FEAT_GROUPS = (("qT", 512, jnp.bfloat16), ("vslT", 128, jnp.bfloat16), ("vwnT", 128, jnp.bfloat16),
                ("mvT", 512, jnp.bfloat16), ("moT", 512, jnp.bfloat16), ("smallT", N_GATE_ROWS, jnp.float32))


def _in_proj_kernel(x_ref, g_ref, wtok_ref, wfeat_ref, *out_refs):
    n_tok = len(_TOK_GROUPS)
    half = TM_IN // IN_HALVES

    def norm(i):
        return _rms(x_ref[i * half:(i + 1) * half, :], g_ref[...]).astype(jnp.bfloat16)

    def token_major(i, h):
        n_cols = sum(width for _, width, _ in _TOK_GROUPS)
        res = [_mm(h, wtok_ref[:, c0:c0 + TOK_DOT_WIDTH]) for c0 in range(0, n_cols, TOK_DOT_WIDTH)]
        off = 0
        for (name, width, dt), o_ref in zip(_TOK_GROUPS, out_refs[:n_tok]):
            r = res[off // TOK_DOT_WIDTH]
            lo = off % TOK_DOT_WIDTH
            o_ref[i * half:(i + 1) * half, :] = r[:, lo:lo + width].astype(dt)
            off += width

    def feature_major(i, h):
        off = 0
        for (name, rows, dt), o_ref in zip(_FEAT_GROUPS, out_refs[n_tok:]):
            r = _nt(wfeat_ref[off:off + rows, :], h)
            if name == "qT":
                r = r * (NSA_HEAD_DIM ** -0.5 * LOG2E)
            for j in range(half // LANES):
                o_ref[i * (half // LANES) + j] = r[:, j * LANES:(j + 1) * LANES].astype(dt)
            off += rows

    h = norm(0)
    for i in range(IN_HALVES):
        token_major(i, h)
        h_next = norm(i + 1) if i + 1 < IN_HALVES else None
        feature_major(i, h)
        h = h_next


def _in_proj(x2d, gain, w_tok, w_feat):
    T = x2d.shape[0]
    n_tok_cols = w_tok.shape[1]
    n_feat_rows = w_feat.shape[0]
    out_shape, out_specs = [], []
    for name, width, dt in _TOK_GROUPS:
        out_shape.append(jax.ShapeDtypeStruct((T, width), dt))
        out_specs.append(pl.BlockSpec((TM_IN, width), lambda i: (i, 0)))
    for name, rows, dt in _FEAT_GROUPS:
        out_shape.append(jax.ShapeDtypeStruct((T // LANES, rows, LANES), dt))
        out_specs.append(pl.BlockSpec((TM_IN // LANES, rows, LANES), lambda i: (i, 0, 0)))
    return pl.pallas_call(
        _in_proj_kernel,
        grid=(T // TM_IN,),
        in_specs=[pl.BlockSpec((TM_IN, D_MODEL), lambda i: (i, 0)),
                  _full((1, D_MODEL)),
                  _full((D_MODEL, n_tok_cols)),
                  _full((n_feat_rows, D_MODEL))],
        out_specs=out_specs,
        out_shape=out_shape,
        compiler_params=_params(1),
        name="in_proj",
    )(x2d, gain, w_tok, w_feat)


def _in_proj_weights(w_in):
    (nq, kc, vc, ksl, vsl, kwn, vwn, gt, mq, mk, mv, mi, mf, mo) = [
        w_in[:, o:o + s] for o, s in zip(IN_OFFSETS, IN_SIZES)]
    gt_r = gt.reshape(D_MODEL, NSA_KV_HEADS, NSA_GROUP, 3).transpose(0, 3, 1, 2).reshape(D_MODEL, 24)
    small = jnp.concatenate([gt_r, mi, mf], axis=1)
    w_tok = jnp.concatenate([kc, vc, ksl, kwn, mq, mk], axis=1)
    w_feat = jnp.concatenate([nq, vsl, vwn, mv, mo, small], axis=1).T
    return w_tok.astype(jnp.bfloat16), w_feat.astype(jnp.bfloat16)


N_CHUNK_COLS = CMP_STRIDE * NSA_KV_WIDTH
N_HID2 = NSA_KV_HEADS * CMP_HIDDEN


def _compress_one(tok_ref, pos_ref, w1_ref, n_chunks):
    bf16 = jnp.bfloat16
    zero = jnp.zeros((NSA_HEAD_DIM, CMP_HIDDEN), bf16)
    rows = [tok_ref[pl.ds(t, n_chunks, stride=CMP_STRIDE), :] for t in range(CMP_STRIDE)]
    halves = []
    for a in range(2):
        acc = None
        for t0 in range(0, CMP_STRIDE, 2):
            xs, ws = [], []
            for t in (t0, t0 + 1):
                xs.append((rows[t] + pos_ref[a:a + 1, t * NSA_KV_WIDTH:(t + 1) * NSA_KV_WIDTH]).astype(bf16))
                w = w1_ref[a, t]
                ws.append(jnp.concatenate([jnp.concatenate([w, zero], axis=1),
                                           jnp.concatenate([zero, w], axis=1)], axis=0))
            d = _mm(jnp.concatenate(xs, axis=1), jnp.concatenate(ws, axis=0))
            acc = d if acc is None else acc + d
        halves.append(acc)
    pre = halves[0] + pltpu.roll(halves[1], n_chunks - 1, 0)
    return (pre * jax.nn.sigmoid(pre)).astype(bf16)


def _compress_kernel(kc_ref, vc_ref, posk_ref, posv_ref, w1k_ref, w1v_ref, w2k_ref, w2vT_ref,
                     kcmp_ref, vcmpT_ref):
    n_chunks = kc_ref.shape[0] // CMP_STRIDE
    hid_k = _compress_one(kc_ref, posk_ref, w1k_ref, n_chunks)
    kcmp = _mm(hid_k, w2k_ref[...])
    row = lax.broadcasted_iota(jnp.int32, kcmp.shape, 0)
    kcmp_ref[...] = jnp.where(row < n_chunks - 1, kcmp, 0.0).astype(kcmp_ref.dtype)
    hid_v = _compress_one(vc_ref, posv_ref, w1v_ref, n_chunks)
    vcmpT = _nt(w2vT_ref[...], hid_v)
    col = lax.broadcasted_iota(jnp.int32, vcmpT.shape, 1)
    vcmpT_ref[...] = jnp.where(col < n_chunks - 1, vcmpT, 0.0).astype(vcmpT_ref.dtype)


def _compress_weights(pos, w1, w2):
    eye = jnp.eye(NSA_KV_HEADS, dtype=w1.dtype)
    w1r = w1.reshape(2, CMP_STRIDE, NSA_HEAD_DIM, CMP_HIDDEN)
    pos_e = jnp.broadcast_to(pos.reshape(2, CMP_STRIDE, 1, NSA_HEAD_DIM),
                             (2, CMP_STRIDE, NSA_KV_HEADS, NSA_HEAD_DIM)).reshape(2, N_CHUNK_COLS)
    w2e = jnp.einsum('jd,hg->hjgd', w2, eye).reshape(N_HID2, NSA_KV_WIDTH)
    return pos_e, w1r.astype(jnp.bfloat16), w2e.astype(jnp.bfloat16)


def _compress(kc, vc, cmp_pos_k, cmp_pos_v, cmp_w1_k, cmp_w2_k, cmp_w1_v, cmp_w2_v, B, S):
    n_chunks = S // CMP_STRIDE
    posk, w1k, w2k = _compress_weights(cmp_pos_k, cmp_w1_k, cmp_w2_k)
    posv, w1v, w2v = _compress_weights(cmp_pos_v, cmp_w1_v, cmp_w2_v)
    chunk_spec = pl.BlockSpec((S, NSA_KV_WIDTH), lambda b: (b, 0))
    return pl.pallas_call(
        _compress_kernel,
        grid=(B,),
        in_specs=[chunk_spec, chunk_spec,
                  _full((2, N_CHUNK_COLS)), _full((2, N_CHUNK_COLS)),
                  _full((2, CMP_STRIDE, NSA_HEAD_DIM, CMP_HIDDEN)), _full((2, CMP_STRIDE, NSA_HEAD_DIM, CMP_HIDDEN)),
                  _full((N_HID2, NSA_KV_WIDTH)), _full((NSA_KV_WIDTH, N_HID2))],
        out_specs=[pl.BlockSpec((None, n_chunks, NSA_KV_WIDTH), lambda b: (b, 0, 0)),
                   pl.BlockSpec((None, NSA_KV_WIDTH, n_chunks), lambda b: (b, 0, 0))],
        out_shape=[jax.ShapeDtypeStruct((B, n_chunks, NSA_KV_WIDTH), jnp.bfloat16),
                   jax.ShapeDtypeStruct((B, NSA_KV_WIDTH, n_chunks), jnp.bfloat16)],
        compiler_params=_params(1),
        name="compress",
    )(kc, vc, posk, posv, w1k, w1v, w2k, w2v.T)


GQ = NSA_GROUP * Q_BLOCK
TINY = 1e-30
CMP_TAB_ROWS = 512
CMP_TAB_ZERO = 248
CMP_TAB_LOOKUP = (232, 256)
SEL_STEP_SHIFT = 2
KEY_SUPER = Q_BLOCK << SEL_STEP_SHIFT
SEL_TAB_ZERO = KEY_SUPER + Q_BLOCK
SEL_TAB_ROWS = SEL_TAB_ZERO + KEY_SUPER
N_WIN_TILES = WINDOW // Q_BLOCK + 1
Q_PER_STEP = 4
SEL_SUB_CHUNKS = 2
V_ROWS = NSA_HEAD_DIM + BF16_ROWS


def _bucket_np(dist):
    n = np.maximum(dist, 0)
    max_exact = REL_BUCKETS // 2
    nf = np.maximum(n, 1).astype(np.float64)
    large = max_exact + (np.log(nf / max_exact) / math.log(REL_MAX_DISTANCE / max_exact)
                         * (REL_BUCKETS - max_exact)).astype(np.int64)
    large = np.minimum(large, REL_BUCKETS - 1)
    return np.where(n < max_exact, n, large).astype(np.int32)


def _bias_index_tables():
    m = np.arange(Q_BLOCK)[:, None]
    r = np.arange(Q_BLOCK)[None, :]
    diag = np.where(r - m >= 0, _bucket_np(r - m), -1).astype(np.int32)
    off = _bucket_np(Q_BLOCK + r - m)
    jp = np.arange(*CMP_TAB_LOOKUP)[:, None] - CMP_TAB_ZERO
    d_c = r - CMP_STRIDE * jp - (CMP_BLOCK - 1)
    cmp_idx = np.where(d_c >= 0, _bucket_np(d_c), -1).astype(np.int32)
    return diag, off, cmp_idx


def _bias_tables_kernel(rb_ref, diag_idx_ref, off_idx_ref, cmp_idx_ref, sel_ref, win_ref, cmp_ref):
    f32 = jnp.float32

    def lookup(idx, head):
        far = rb_ref[head, REL_BUCKETS - 1]
        acc = jnp.full(idx.shape, NEG_INF, f32)
        for k in range(REL_BUCKETS):
            acc = jnp.where(idx == k, (rb_ref[head, k] - far) * LOG2E, acc)
        return acc

    m_io = lax.broadcasted_iota(jnp.int32, (Q_BLOCK, Q_BLOCK), 0)
    r_io = lax.broadcasted_iota(jnp.int32, (Q_BLOCK, Q_BLOCK), 1)
    neg_tile = jnp.full((Q_BLOCK, Q_BLOCK), NEG_INF, f32)
    lo, hi = CMP_TAB_LOOKUP
    for h in range(NSA_KV_HEADS):
        for g in range(NSA_GROUP):
            head = h * NSA_GROUP + g
            sl = slice(g * Q_BLOCK, (g + 1) * Q_BLOCK)
            far = 0.0
            far_tile = jnp.full((Q_BLOCK, Q_BLOCK), far, f32)
            diag_v = lookup(diag_idx_ref[...], head)
            off_v = lookup(off_idx_ref[...], head)
            n_far = (SEL_TAB_ZERO - Q_BLOCK) // Q_BLOCK
            for t in range(SEL_TAB_ROWS // Q_BLOCK):
                rows = slice(t * Q_BLOCK, (t + 1) * Q_BLOCK)
                tile = far_tile if t < n_far else off_v if t == n_far else diag_v if t == n_far + 1 else neg_tile
                sel_ref[h, rows, sl] = tile
            win_ref[h, 0, :, sl] = diag_v
            win_ref[h, 1, :, sl] = off_v
            for back in range(2, N_WIN_TILES - 1):
                win_ref[h, back, :, sl] = far_tile
            win_ref[h, N_WIN_TILES - 1, :, sl] = jnp.where(r_io < m_io, far, NEG_INF)
            win_ref[h, N_WIN_TILES, :, sl] = neg_tile
            cmp_ref[h, 0:lo, sl] = jnp.full((lo, Q_BLOCK), far, f32)
            cmp_ref[h, lo:hi, sl] = lookup(cmp_idx_ref[...], head)
            cmp_ref[h, hi:CMP_TAB_ROWS, sl] = jnp.full((CMP_TAB_ROWS - hi, Q_BLOCK), NEG_INF, f32)


def _bias_tables(rel_bias):
    diag_idx, off_idx, cmp_idx = _bias_index_tables()
    shapes = [(NSA_KV_HEADS, SEL_TAB_ROWS, GQ), (NSA_KV_HEADS, N_WIN_TILES + 1, Q_BLOCK, GQ),
              (NSA_KV_HEADS, CMP_TAB_ROWS, GQ)]
    return pl.pallas_call(
        _bias_tables_kernel,
        in_specs=[pl.BlockSpec(memory_space=pltpu.SMEM),
                  _full(diag_idx.shape), _full(off_idx.shape), _full(cmp_idx.shape)],
        out_specs=[_full(s) for s in shapes],
        out_shape=[jax.ShapeDtypeStruct(s, jnp.float32) for s in shapes],
        grid=(1,),
        compiler_params=_params(1),
        name="bias_tables",
    )(rel_bias, jnp.asarray(diag_idx), jnp.asarray(off_idx), jnp.asarray(cmp_idx))


def _overlap_np(n_cmp_rows, n_sel):
    cmp_start = np.arange(n_cmp_rows) * CMP_STRIDE
    cmp_end = cmp_start + CMP_BLOCK - 1
    sel_start = np.arange(n_sel) * SEL_BLOCK
    ov = ((cmp_start[None, :] <= sel_start[:, None] + SEL_BLOCK - 1)
          & (cmp_end[None, :] >= sel_start[:, None])).astype(np.float32)
    ov[:, n_cmp_rows - 1] = 0.0
    return ov


def _tile4(a):
    return jnp.concatenate([a] * NSA_GROUP, axis=1)


def _select_blocks(score, score_ref, n_top, hooks=()):
    n_sel = score.shape[0]
    score_ref[...] = score
    n_grp = n_sel // SUBLANES
    grp = [score[SUBLANES * v:SUBLANES * (v + 1), :] for v in range(n_grp)]
    cnt = [jnp.zeros((SUBLANES, Q_BLOCK), jnp.int32) for _ in range(n_grp)]
    sub_io = lax.broadcasted_iota(jnp.int32, (SUBLANES, Q_BLOCK), 0)
    hook_at = {(i * n_sel) // len(hooks): hk for i, hk in enumerate(hooks)} if hooks else {}
    for jp in range(n_sel):
        if jp in hook_at:
            hook_at[jp]()
        row = score_ref[jp:jp + 1, :]
        for v in range(n_grp):
            if SUBLANES * v > jp:
                inc = (row >= grp[v]).astype(jnp.int32)
            elif SUBLANES * (v + 1) - 1 < jp:
                inc = (row > grp[v]).astype(jnp.int32)
            else:
                tie = (sub_io > jp - SUBLANES * v).astype(jnp.int32)
                inc = jnp.where(row > grp[v], 1, jnp.where(row == grp[v], tie, 0))
            cnt[v] = cnt[v] + inc
    return [jnp.where(cnt[v] < n_top, 0.0, NEG_INF) for v in range(n_grp)]


def _nsa_kernel(*refs):
    step = pl.program_id(1)
    tail = []
    for blk in range(Q_PER_STEP):
        tail = _nsa_block(step * Q_PER_STEP + blk, blk, tail, *refs)
    for part in tail:
        part()


def _nsa_block(c, blk, deferred, q_ref, kcmp_ref, vcmpT_ref, ksl_ref, vslT_ref, kwn_ref, vwnT_ref, gate_ref,
               seltab_ref, wintab_ref, cmptab_ref, ovl_ref, blkind_ref, out_ref, score_ref, selb_ref, sbufs_ref,
               swin_ref):
    sbuf_ref, sbuf2_ref = sbufs_ref.at[0], sbufs_ref.at[1]
    n_cmp = kcmp_ref.shape[0]
    n_sel = ovl_ref.shape[0]
    n_top = min(N_SELECT, n_sel)
    f32 = jnp.float32
    bf16 = jnp.bfloat16
    DH = NSA_HEAD_DIM
    heads = range(NSA_KV_HEADS)

    q = q_ref[blk]
    zq = jnp.zeros((DH, GQ), bf16)
    qcat, qpad = [], []
    for h in heads:
        qcat.append(jnp.concatenate([q[(h * NSA_GROUP + g) * DH:(h * NSA_GROUP + g + 1) * DH, :]
                                     for g in range(NSA_GROUP)], axis=1))
        qpad.append(jnp.concatenate([qcat[h], zq] if h == 0 else [zq, qcat[h]], axis=0))

    backs = list(range(N_WIN_TILES))
    kts = [jnp.maximum(c - back, 0) for back in backs]
    slots = [jnp.where(c >= back, back, N_WIN_TILES) for back in backs]
    m_win = [jnp.full((1, GQ), NEG_INF, f32) for _ in heads]

    def win_score(back, h):
        key0 = pl.multiple_of(kts[back] * Q_BLOCK, Q_BLOCK)
        s = _mm(kwn_ref[pl.ds(key0, Q_BLOCK), :], qpad[h]) + wintab_ref[h, slots[back]]
        swin_ref[h, back * Q_BLOCK:(back + 1) * Q_BLOCK, :] = s
        return jnp.max(s, axis=0, keepdims=True)

    cmp_off = pl.multiple_of(CMP_TAB_ZERO - (Q_BLOCK // CMP_STRIDE) * c, SUBLANES)
    kcmp = kcmp_ref[...]
    j_io = lax.broadcasted_iota(jnp.int32, (n_sel, Q_BLOCK), 0)
    r_io = lax.broadcasted_iota(jnp.int32, (n_sel, Q_BLOCK), 1)
    cur = (Q_BLOCK // SEL_BLOCK) * c + (r_io >= SEL_BLOCK).astype(jnp.int32)
    forced = (j_io == 0) | (j_io == cur) | (j_io == cur - 1)
    visible = j_io <= cur
    o_c = []
    for h in heads:
        tab = cmptab_ref[h, pl.ds(cmp_off, n_cmp), :]
        s = _mm(kcmp, qpad[h]) + tab
        m = jnp.maximum(jnp.max(s, axis=0, keepdims=True), 0.1 * NEG_INF)
        p = jnp.exp2(s - m)
        l = jnp.sum(p, axis=0, keepdims=True)
        pn = p * (1.0 / jnp.maximum(l, TINY))
        o_c.append(_mm(vcmpT_ref[h * DH:(h + 1) * DH, :], pn.astype(bf16)))
        psum = pn[:, 0:Q_BLOCK]
        for g in range(1, NSA_GROUP):
            psum = psum + pn[:, g * Q_BLOCK:(g + 1) * Q_BLOCK]
        imp = jnp.dot(ovl_ref[...], psum, precision=lax.Precision.HIGHEST,
                      preferred_element_type=f32)
        score = jnp.where(forced, FORCED_SCORE, jnp.where(visible, imp, -1.0))

        def hook(back, h=h):
            m_win[h] = jnp.maximum(m_win[h], win_score(back, h))

        hooks = [functools.partial(hook, back) for back in backs]
        if h == 0:
            for i, part in enumerate(deferred):
                hooks.insert(2 * i + 1, part)
        rows = _select_blocks(score, score_ref.at[h], n_top, hooks=hooks)
        for v, mask_rows in enumerate(rows):
            selb_ref[h, SUBLANES * v:SUBLANES * (v + 1), :] = mask_rows

    def ones_rows(n_keys):
        return jnp.ones((V_ROWS - DH, n_keys), bf16)

    acc_w = [jnp.zeros((V_ROWS, GQ), f32) for _ in heads]

    def win_value(back):
        for h in heads:
            p = jnp.exp2(swin_ref[h, back * Q_BLOCK:(back + 1) * Q_BLOCK, :] - m_win[h]).astype(bf16)
            vT = jnp.concatenate([vwnT_ref[kts[back], h * DH:(h + 1) * DH, :], ones_rows(Q_BLOCK)], axis=0)
            acc_w[h] = acc_w[h] + _mm(vT, p)

    blocks_per_step = KEY_SUPER // SEL_BLOCK
    tiles_per_step = KEY_SUPER // Q_BLOCK

    sub = KEY_SUPER // SEL_SUB_CHUNKS
    blocks_per_sub = sub // SEL_BLOCK
    tiles_per_sub = sub // Q_BLOCK

    own_lanes = [(lax.broadcasted_iota(jnp.int32, (sub, NSA_KV_WIDTH), 1) >= DH) == bool(h) for h in heads]

    def q_with_mask_rows(j, h):
        blk0 = pl.multiple_of(j * blocks_per_step, blocks_per_step)
        rows = jnp.concatenate([_tile4(selb_ref[h, pl.ds(blk0, blocks_per_step), :]),
                                jnp.zeros((DH - blocks_per_step, GQ), f32)], axis=0).astype(bf16)
        return jnp.concatenate([qcat[h], rows] if h == 0 else [rows, qcat[h]], axis=0)

    def score_chunk(j, ci, h, q_aug, buf_ref, near=True):
        key0 = pl.multiple_of(j * KEY_SUPER + ci * sub, sub)
        k_aug = jnp.where(own_lanes[h], ksl_ref[pl.ds(key0, sub), :], blkind_ref[h, ci * sub:(ci + 1) * sub, :])
        s = _mm(k_aug, q_aug)
        if near:
            tab_off = pl.multiple_of(
                jnp.maximum(j * KEY_SUPER - c * Q_BLOCK + SEL_TAB_ZERO, 0) + ci * sub, Q_BLOCK)
            s = s + seltab_ref[h, pl.ds(tab_off, sub), :]
        buf_ref[h, ci * sub:(ci + 1) * sub, :] = s
        return jnp.max(s, axis=0, keepdims=True)

    def value_chunk(j, ci, h, m_h, buf_ref):
        p = jnp.exp2(buf_ref[h, ci * sub:(ci + 1) * sub, :] - m_h).astype(bf16)
        vT = jnp.concatenate([vslT_ref[j * tiles_per_step + ci * tiles_per_sub + i, h * DH:(h + 1) * DH, :]
                              for i in range(tiles_per_sub)], axis=1)
        return _mm(jnp.concatenate([vT, ones_rows(sub)], axis=0), p)

    def values(j, m_old, m_cur, acc, src_ref, before_chunk=None):
        acc = [jnp.exp2(m_old[h] - m_cur[h]) * acc[h] for h in heads]
        for ci in range(SEL_SUB_CHUNKS):
            if before_chunk is not None:
                before_chunk(ci)
            for h in heads:
                acc[h] = acc[h] + value_chunk(j, ci, h, m_cur[h], src_ref)
        return tuple(acc)

    def sel_step(j, carry, src_ref, dst_ref, near=True):
        m_old, m_cur, acc = carry
        m_run = list(m_cur)
        q_aug = [q_with_mask_rows(j + 1, h) for h in heads]

        def scores(ci):
            for h in heads:
                m_run[h] = jnp.maximum(m_run[h], score_chunk(j + 1, ci, h, q_aug[h], dst_ref, near))

        acc = values(j, m_old, m_cur, acc, src_ref, before_chunk=scores)
        return m_cur, tuple(m_run), acc

    def sel_pair(i, carry, near):
        carry = sel_step(2 * i, carry, sbuf_ref, sbuf2_ref, near)
        return sel_step(2 * i + 1, carry, sbuf2_ref, sbuf_ref, near)

    n_steps = lax.shift_right_logical(c, SEL_STEP_SHIFT) + 1
    m_init = tuple(jnp.full((1, GQ), NEG_INF, f32) for _ in heads)
    acc_init = tuple(jnp.zeros((V_ROWS, GQ), f32) for _ in heads)
    m_first = list(m_init)
    win_order = list(backs)
    q_aug0 = [q_with_mask_rows(0, h) for h in heads]
    for ci in range(SEL_SUB_CHUNKS):
        for _ in range(-(-N_WIN_TILES // SEL_SUB_CHUNKS)):
            if win_order:
                win_value(win_order.pop(0))
        for h in heads:
            m_first[h] = jnp.maximum(m_first[h], score_chunk(0, ci, h, q_aug0[h], sbuf_ref))
    while win_order:
        win_value(win_order.pop(0))
    o_w = [a[0:DH, :] * (1.0 / a[DH:DH + 1, :]) for a in acc_w]
    n_piped = n_steps - 1
    odd = n_piped & 1
    n_far = lax.shift_right_logical(jnp.maximum(c - 1, 0), SEL_STEP_SHIFT)
    far_pairs = lax.shift_right_logical(jnp.maximum(n_far - 1, 0), 1)
    carry = lax.fori_loop(0, far_pairs, functools.partial(sel_pair, near=False),
                          (m_init, tuple(m_first), acc_init))
    carry = lax.fori_loop(far_pairs, lax.shift_right_logical(n_piped, 1), functools.partial(sel_pair, near=True),
                          carry)
    carry = lax.fori_loop(0, odd, lambda _, cr: sel_step(n_piped - 1, cr, sbuf_ref, sbuf2_ref), carry)
    m_old, m_cur, acc = carry
    acc = [jnp.exp2(m_old[h] - m_cur[h]) * acc[h] for h in heads]
    last_ref = sbufs_ref.at[odd]

    def last_values(ci):
        for h in heads:
            acc[h] = acc[h] + value_chunk(n_piped, ci, h, m_cur[h], last_ref)

    def combine():
        o_s = [a[0:DH, :] * (1.0 / jnp.maximum(a[DH:DH + 1, :], TINY)) for a in acc]
        for h in heads:
            ys = []
            for g in range(NSA_GROUP):
                sl = slice(g * Q_BLOCK, (g + 1) * Q_BLOCK)
                row0 = h * NSA_GROUP + g
                gates = [jax.nn.sigmoid(gate_ref[blk, kind * NSA_HEADS + row0:kind * NSA_HEADS + row0 + 1, :])
                         for kind in range(3)]
                ys.append(gates[0] * o_c[h][:, sl] + gates[1] * o_s[h][:, sl] + gates[2] * o_w[h][:, sl])
            yT = jnp.concatenate(ys, axis=0)
            for half in range(2):
                col = (2 * h + half) * LANES
                out_ref[blk * Q_BLOCK:(blk + 1) * Q_BLOCK, col:col + LANES] = (
                    yT[half * LANES:(half + 1) * LANES, :].T.astype(out_ref.dtype))

    return [functools.partial(last_values, ci) for ci in range(SEL_SUB_CHUNKS)] + [combine]


def _nsa(qT, kcmp, vcmpT, ksl, vslT, kwn, vwnT, smallT, tables, B, S):
    assert S % KEY_SUPER == 0
    nq = S // Q_BLOCK
    n_steps = nq // Q_PER_STEP
    n_cmp = S // CMP_STRIDE
    n_sel = S // SEL_BLOCK
    seltab, wintab, cmptab = tables
    ovl = jnp.asarray(_overlap_np(n_cmp, n_sel))
    key_blk = np.arange(KEY_SUPER)[:, None] // SEL_BLOCK
    lane = np.arange(NSA_KV_WIDTH)[None, :]
    blkind = jnp.asarray(np.stack([lane - NSA_HEAD_DIM == key_blk, lane == key_blk]), jnp.bfloat16)
    ksl3 = ksl.reshape(B, S, NSA_KV_WIDTH)
    kwn3 = kwn.reshape(B, S, NSA_KV_WIDTH)
    vslT4 = vslT.reshape(B, nq, NSA_KV_WIDTH, Q_BLOCK)
    vwnT4 = vwnT.reshape(B, nq, NSA_KV_WIDTH, Q_BLOCK)
    k_spec = pl.BlockSpec((None, S, NSA_KV_WIDTH), lambda b, c: (b, 0, 0))
    vT_spec = pl.BlockSpec((None, nq, NSA_KV_WIDTH, Q_BLOCK), lambda b, c: (b, 0, 0, 0))
    const = lambda a: pl.BlockSpec(a.shape, lambda b, c: (0,) * a.ndim)
    return pl.pallas_call(
        _nsa_kernel,
        grid=(B, n_steps),
        in_specs=[pl.BlockSpec((Q_PER_STEP, NSA_WIDTH, Q_BLOCK), lambda b, c: (b * n_steps + c, 0, 0)),
                  pl.BlockSpec((None, n_cmp, NSA_KV_WIDTH), lambda b, c: (b, 0, 0)),
                  pl.BlockSpec((None, NSA_KV_WIDTH, n_cmp), lambda b, c: (b, 0, 0)),
                  k_spec, vT_spec, k_spec, vT_spec,
                  pl.BlockSpec((Q_PER_STEP, N_GATE_ROWS, Q_BLOCK), lambda b, c: (b * n_steps + c, 0, 0)),
                  const(seltab), const(wintab), const(cmptab), const(ovl), const(blkind)],
        out_specs=pl.BlockSpec((Q_PER_STEP * Q_BLOCK, NSA_WIDTH), lambda b, c: (b * n_steps + c, 0)),
        out_shape=jax.ShapeDtypeStruct((B * S, NSA_WIDTH), jnp.bfloat16),
        scratch_shapes=[pltpu.VMEM((NSA_KV_HEADS, n_sel, Q_BLOCK), jnp.float32),
                        pltpu.VMEM((NSA_KV_HEADS, n_sel, Q_BLOCK), jnp.float32),
                        pltpu.VMEM((2, NSA_KV_HEADS, KEY_SUPER, GQ), jnp.float32),
                        pltpu.VMEM((NSA_KV_HEADS, N_WIN_TILES * Q_BLOCK, GQ), jnp.float32)],
        compiler_params=_params(2),
        name="nsa",
    )(qT, kcmp, vcmpT, ksl3, vslT4, kwn3, vwnT4, smallT, seltab, wintab, cmptab, ovl, blkind)


CONV_TILE = 256
I_ROW = 24
C_ROWS = MLSTM_V_DIM + BF16_ROWS


def _log_sigmoid(x):
    return jnp.minimum(x, 0.0) - jnp.log(1.0 + jnp.exp(-jnp.abs(x)))


def _mlstm_kernel(mqk_ref, mvT_ref, moT_ref, gates_ref, gb_ref, tri_ref, last_ref, convw_ref, convb_ref,
                  norm_ref, out_ref, qk_ref, rows_ref):
    S = mqk_ref.shape[0]
    nt = S // LANES
    L = MLSTM_CHUNK
    f32 = jnp.float32
    bf16 = jnp.bfloat16
    kscale_row = jnp.where(lax.broadcasted_iota(jnp.int32, (1, 2 * MLSTM_QK_WIDTH), 1) < MLSTM_QK_WIDTH,
                           1.0, MLSTM_QK_DIM ** -0.5)

    def conv_body(i, _):
        t0 = pl.multiple_of(i * CONV_TILE, CONV_TILE)
        cur = mqk_ref[pl.ds(t0, CONV_TILE), :].astype(f32)
        prev_start = pl.multiple_of(jnp.maximum(t0 - BF16_ROWS, 0), BF16_ROWS)
        prev = (mqk_ref[pl.ds(prev_start, BF16_ROWS), :].astype(f32)[BF16_ROWS - SUBLANES:, :]
                * jnp.where(i > 0, 1.0, 0.0))
        ext = jnp.concatenate([prev, cur], axis=0)
        y = convb_ref[...]
        for j in range(CONV_WIDTH):
            lo = SUBLANES - (CONV_WIDTH - 1) + j
            y = y + convw_ref[j:j + 1, :] * ext[lo:lo + CONV_TILE, :]
        y = y * jax.nn.sigmoid(y) * kscale_row
        qk_ref[pl.ds(t0, CONV_TILE), :] = y.astype(bf16)
        return 0

    lax.fori_loop(0, S // CONV_TILE, conv_body, 0)

    H = MLSTM_HEADS
    G8 = 2 * H
    n_rows = nt * G8
    a3 = gates_ref[:, I_ROW:I_ROW + G8, :] + gb_ref[...][None]
    is_f = lax.broadcasted_iota(jnp.int32, a3.shape, 1) >= H
    x = jnp.where(is_f, _log_sigmoid(a3), a3).reshape(n_rows, LANES)
    bcum = jnp.dot(x, tri_ref[...], precision=lax.Precision.HIGHEST, preferred_element_type=f32)
    b_rows = pltpu.roll(bcum, n_rows - H, 0)
    g_rows = x - b_rows
    pos = lax.broadcasted_iota(jnp.int32, (n_rows, LANES), 1) & (L - 1)

    def chunk_cummax(a):
        shift = 1
        while shift < L:
            a = jnp.where(pos >= shift, jnp.maximum(a, pltpu.roll(a, shift, 1)), a)
            shift *= 2
        return a

    def chunk_last(a):
        return jnp.dot(a, last_ref[...], precision=lax.Precision.HIGHEST, preferred_element_type=f32)

    bl_rows = chunk_last(b_rows)
    rows_ref[0] = g_rows
    rows_ref[1] = chunk_cummax(g_rows)
    rows_ref[2] = b_rows
    rows_ref[3] = bl_rows
    rows_ref[4] = chunk_last(chunk_cummax(bl_rows + g_rows))

    s_io = lax.broadcasted_iota(jnp.int32, (LANES, LANES), 0)
    t_io = lax.broadcasted_iota(jnp.int32, (LANES, LANES), 1)
    causal = (s_io <= t_io) & ((s_io >= L) == (t_io >= L))
    lane = lax.broadcasted_iota(jnp.int32, (1, LANES), 1)
    in_chunk = [lane < L, lane >= L]
    QD, VD = MLSTM_QK_DIM, MLSTM_V_DIM
    ones_aug = jnp.ones((C_ROWS - VD, LANES), bf16)
    head_lanes = [(lax.broadcasted_iota(jnp.int32, (LANES, LANES), 1) >= QD) == bool(par) for par in range(2)]

    def swap_halves(row):
        return pltpu.roll(row, L, 1)

    def tile_body(i, state):
        t0 = pl.multiple_of(i * LANES, LANES)
        r0 = pl.multiple_of(i * G8, G8)
        qk = qk_ref[pl.ds(t0, LANES), :]
        qk32 = qk.astype(f32)
        pairsT = [qk32[:, j * LANES:(j + 1) * LANES].T for j in range(4)]
        g8, cm8, b8, bl8, wm8 = [rows_ref[j, pl.ds(r0, G8), :] for j in range(5)]
        new_state = []
        for hh in range(H):
            pair, par = hh // 2, hh % 2
            g_r, cm_r, b_r, bl_r, wm_r = [a[hh:hh + 1, :] for a in (g8, cm8, b8, bl8, wm8)]
            caug, m_in = state[hh]
            qT = pairsT[pair][par * QD:(par + 1) * QD, :]
            kT = pairsT[2 + pair][par * QD:(par + 1) * QD, :]
            kpair = qk[:, (2 + pair) * LANES:(3 + pair) * LANES]
            qmask = jnp.where(head_lanes[par], qk[:, pair * LANES:(pair + 1) * LANES], jnp.zeros((), bf16))
            vaug = jnp.concatenate([mvT_ref[i, hh * VD:(hh + 1) * VD, :], ones_aug], axis=0)

            m_mid = swap_halves(jnp.maximum(bl_r + m_in, wm_r))
            m_prev = jnp.where(in_chunk[0], m_in, m_mid)
            m_next = jnp.maximum(bl_r + m_prev, wm_r)
            m_intra = b_r + cm_r
            m_inter = b_r + m_prev
            m_t = jnp.maximum(m_inter, m_intra)
            e_intra = jnp.exp(m_intra - m_t)
            e_inter = jnp.exp(m_inter - m_t)
            decay = jnp.exp(bl_r + m_prev - m_next)
            inject = jnp.exp(wm_r - m_next)

            g_mat = jnp.broadcast_to(g_r, (LANES, LANES)).T
            w = jnp.exp(jnp.where(causal, g_mat - cm_r, NEG_INF))
            st = _nt(kpair, qmask) * (w * e_intra)
            y = _mm(vaug, st.astype(bf16))
            kw = kT * jnp.exp(bl_r + g_r - wm_r)
            cs = caug
            for p in range(LANES // L):
                qs = jnp.where(in_chunk[p], qT * e_inter, 0.0).astype(bf16)
                y = y + _mm(cs.astype(bf16), qs)
                u = _nt(vaug, jnp.where(in_chunk[p], kw, 0.0).astype(bf16))
                dec = decay if p == 0 else swap_halves(decay)
                inj = inject if p == 0 else swap_halves(inject)
                cs = dec[:, 0:QD] * cs + inj[:, 0:QD] * u
            m_out = jnp.where(in_chunk[1], m_next, swap_halves(m_next))
            new_state.append((cs, m_out))

            den = y[VD:VD + 1, :]
            hT = y[0:VD, :] * (1.0 / jnp.maximum(jnp.abs(den), jnp.exp(-m_t)))
            sl = slice(hh * VD, (hh + 1) * VD)
            hn = hT * lax.rsqrt(jnp.mean(hT * hT, axis=0, keepdims=True) + NORM_EPS) * norm_ref[sl, :]
            yT = jax.nn.sigmoid(moT_ref[i, sl, :].astype(f32)) * hn
            out_ref[pl.ds(t0, LANES), sl] = yT.T.astype(out_ref.dtype)
        return tuple(new_state)

    init = tuple((jnp.zeros((C_ROWS, QD), f32), jnp.zeros((1, LANES), f32)) for _ in range(H))
    lax.fori_loop(0, nt, tile_body, init, unroll=8)


def _mlstm(mqk, mvT, moT, smallT, conv_w, conv_b, gate_bias, mlstm_norm, B, S):
    nt = S // LANES
    gb = jnp.broadcast_to(gate_bias.astype(jnp.float32).reshape(2 * MLSTM_HEADS, 1), (2 * MLSTM_HEADS, LANES))
    norm_cols = jnp.broadcast_to(mlstm_norm.astype(jnp.float32).reshape(MLSTM_WIDTH, 1), (MLSTM_WIDTH, LANES))
    lane = np.arange(LANES)
    same_chunk = lane[:, None] // MLSTM_CHUNK == lane[None, :] // MLSTM_CHUNK
    tri = (same_chunk & (lane[:, None] <= lane[None, :])).astype(np.float32)
    last = (same_chunk & (lane[:, None] % MLSTM_CHUNK == MLSTM_CHUNK - 1)).astype(np.float32)
    seq = lambda w: pl.BlockSpec((S, w), lambda b: (b, 0))
    tiles = lambda rows: pl.BlockSpec((nt, rows, LANES), lambda b: (b, 0, 0))
    return pl.pallas_call(
        _mlstm_kernel,
        grid=(B,),
        in_specs=[seq(2 * MLSTM_QK_WIDTH), tiles(MLSTM_WIDTH), tiles(MLSTM_WIDTH), tiles(N_GATE_ROWS),
                  _full((2 * MLSTM_HEADS, LANES)), _full((LANES, LANES)), _full((LANES, LANES)),
                  _full((CONV_WIDTH, 2 * MLSTM_QK_WIDTH)), _full((1, 2 * MLSTM_QK_WIDTH)),
                  _full((MLSTM_WIDTH, LANES))],
        out_specs=seq(MLSTM_WIDTH),
        out_shape=jax.ShapeDtypeStruct((B * S, MLSTM_WIDTH), jnp.bfloat16),
        scratch_shapes=[pltpu.VMEM((S, 2 * MLSTM_QK_WIDTH), jnp.bfloat16),
                        pltpu.VMEM((5, nt * 2 * MLSTM_HEADS, LANES), jnp.float32)],
        compiler_params=_params(1),
        name="mlstm",
    )(mqk, mvT, moT, smallT, gb, jnp.asarray(tri), jnp.asarray(last), conv_w, conv_b.reshape(1, -1), norm_cols)


def _mem_kv_kernel(mem_ref, g_ref, w_ref, k_ref, v_ref):
    mn = _rms(mem_ref[...], g_ref[...]).astype(jnp.bfloat16)
    k_ref[...] = _mm(mn, w_ref[:, :D_MODEL]).astype(k_ref.dtype)
    v_ref[...] = _mm(mn, w_ref[:, D_MODEL:]).astype(v_ref.dtype)


def _mem_kv(mem, gain, w_xkv):
    B, M, _ = mem.shape
    spec = pl.BlockSpec((None, M, D_MODEL), lambda b: (b, 0, 0))
    return pl.pallas_call(
        _mem_kv_kernel,
        grid=(B,),
        in_specs=[spec, _full((1, D_MODEL)), _full((D_MODEL, 2 * D_MODEL))],
        out_specs=[spec, spec],
        out_shape=[jax.ShapeDtypeStruct((B, M, D_MODEL), jnp.bfloat16)] * 2,
        compiler_params=_params(1),
        name="mem_kv",
    )(mem, gain, w_xkv.astype(jnp.bfloat16))


TM_X = 1024
X_HALVES = 2


def _mix_xattn_kernel(ynsa_ref, yml_ref, x_ref, wout_ref, gpost_ref, gpre_ref, wq_ref, k_ref, v_ref,
                      wo_ref, gpost2_ref, out_ref):
    bf16 = jnp.bfloat16
    halves = [slice(i * (TM_X // X_HALVES), (i + 1) * (TM_X // X_HALVES)) for i in range(X_HALVES)]
    y = [_mm(ynsa_ref[r, :], wout_ref[:NSA_WIDTH, :]) + _mm(yml_ref[r, :], wout_ref[NSA_WIDTH:, :]) for r in halves]
    x1 = [x_ref[r, :] + _rms(y[i], gpost_ref[...]) for i, r in enumerate(halves)]
    h2 = [_rms(x1[i], gpre_ref[...]).astype(bf16) for i in range(X_HALVES)]
    q = [(_mm(h2[i], wq_ref[...]) * (XATTN_HEAD_DIM ** -0.5)).astype(bf16) for i in range(X_HALVES)]
    outs = [[] for _ in range(X_HALVES)]
    for hh in range(XATTN_HEADS):
        sl = slice(hh * XATTN_HEAD_DIM, (hh + 1) * XATTN_HEAD_DIM)
        s = [_nt(q[i][:, sl], k_ref[:, sl]) for i in range(X_HALVES)]
        for i in range(X_HALVES):
            p = jnp.exp(s[i] - jnp.max(s[i], axis=1, keepdims=True))
            l = jnp.sum(p, axis=1, keepdims=True)
            outs[i].append((_mm(p.astype(bf16), v_ref[:, sl]) * (1.0 / l)).astype(bf16))
    y2 = [_mm(jnp.concatenate(outs[i], axis=1), wo_ref[...]) for i in range(X_HALVES)]
    for i, r in enumerate(halves):
        out_ref[r, :] = x1[i] + _rms(y2[i], gpost2_ref[...])


def _mix_xattn(ynsa, yml, x2d, w_out, g_post, g_pre, w_xq, kx, vx, w_xo, g_post2, B, S):
    nt = S // TM_X
    M = kx.shape[1]
    tok = lambda w: pl.BlockSpec((TM_X, w), lambda b, i: (b * nt + i, 0))
    mem_spec = pl.BlockSpec((None, M, D_MODEL), lambda b, i: (b, 0, 0))
    sq = _full((D_MODEL, D_MODEL))
    row = _full((1, D_MODEL))
    bf = lambda w: w.astype(jnp.bfloat16)
    return pl.pallas_call(
        _mix_xattn_kernel,
        grid=(B, nt),
        in_specs=[tok(NSA_WIDTH), tok(MLSTM_WIDTH), tok(D_MODEL), sq, row, row, sq, mem_spec, mem_spec,
                  sq, row],
        out_specs=tok(D_MODEL),
        out_shape=jax.ShapeDtypeStruct((B * S, D_MODEL), jnp.float32),
        compiler_params=_params(2),
        name="mix_xattn",
    )(ynsa, yml, x2d, bf(w_out), g_post, g_pre, bf(w_xq), kx, vx, bf(w_xo), g_post2)


TM_F = 1024
F_HALVES = 2


def _ffn_kernel(x_ref, gpre_ref, wgu_ref, wd_ref, gpost_ref, out_ref, acc_ref):
    bf16 = jnp.bfloat16
    n_chunks = D_FF // F_TILE
    half = TM_F // F_HALVES

    def pre(i):
        return _rms(x_ref[i * half:(i + 1) * half, :], gpre_ref[...]).astype(bf16)

    def chunk(i, h, j):
        rows = slice(i * half, (i + 1) * half)
        cols = slice(j * F_TILE, (j + 1) * F_TILE)
        g = _mm(h, wgu_ref[:, cols])
        u = _mm(h, wgu_ref[:, D_FF + j * F_TILE:D_FF + (j + 1) * F_TILE])
        act = (g * jax.nn.sigmoid(g) * u).astype(bf16)
        down = _mm(act, wd_ref[cols, :])
        if j == 0:
            acc_ref[rows, :] = down
        else:
            acc_ref[rows, :] += down

    def post(i):
        rows = slice(i * half, (i + 1) * half)
        out_ref[rows, :] = x_ref[rows, :] + _rms(acc_ref[rows, :], gpost_ref[...])

    h = pre(0)
    for i in range(F_HALVES):
        chunk(i, h, 0)
        if i > 0:
            post(i - 1)
        h_next = pre(i + 1) if i + 1 < F_HALVES else None
        for j in range(1, n_chunks):
            chunk(i, h, j)
        h = h_next
    post(F_HALVES - 1)


def _ffn(x2d, g_pre, w_gate_up, w_down, g_post):
    T = x2d.shape[0]
    tok = pl.BlockSpec((TM_F, D_MODEL), lambda i: (i, 0))
    row = _full((1, D_MODEL))
    return pl.pallas_call(
        _ffn_kernel,
        grid=(T // TM_F,),
        in_specs=[tok, row, _full((D_MODEL, 2 * D_FF), buffers=1), _full((D_FF, D_MODEL), buffers=1), row],
        out_specs=tok,
        out_shape=jax.ShapeDtypeStruct((T, D_MODEL), jnp.float32),
        scratch_shapes=[pltpu.VMEM((TM_F, D_MODEL), jnp.float32)],
        compiler_params=_params(1),
        name="ffn",
    )(x2d, g_pre, w_gate_up.astype(jnp.bfloat16), w_down.astype(jnp.bfloat16), g_post)


def _layer(x, mem, rel_bias, mix_norm_pre, w_in, cmp_pos_k, cmp_pos_v, cmp_w1_k, cmp_w2_k, cmp_w1_v,
           cmp_w2_v, conv_w, conv_b, mlstm_gate_bias, mlstm_norm, w_out, mix_norm_post, xattn_norm_pre,
           mem_norm, w_xq, w_xkv, w_xo, xattn_norm_post, ffn_norm_pre, w_gate_up, w_down, ffn_norm_post):
    B, S, _ = x.shape
    row = lambda g: g.reshape(1, -1).astype(jnp.float32)
    x2d = x.reshape(B * S, D_MODEL)
    w_tok, w_feat = _in_proj_weights(w_in)
    (kc, vc, ksl, kwn, mqk, qT, vslT, vwnT, mvT, moT, smallT) = _in_proj(
        x2d, row(mix_norm_pre), w_tok, w_feat)
    kcmp, vcmpT = _compress(kc, vc, cmp_pos_k, cmp_pos_v, cmp_w1_k, cmp_w2_k, cmp_w1_v, cmp_w2_v, B, S)
    tables = _bias_tables(rel_bias.astype(jnp.float32))
    ynsa = _nsa(qT, kcmp, vcmpT, ksl, vslT, kwn, vwnT, smallT, tables, B, S)
    yml = _mlstm(mqk, mvT, moT, smallT, conv_w, conv_b, mlstm_gate_bias, mlstm_norm, B, S)
    kx, vx = _mem_kv(mem, row(mem_norm), w_xkv)
    x2 = _mix_xattn(ynsa, yml, x2d, w_out, row(mix_norm_post), row(xattn_norm_pre), w_xq, kx, vx, w_xo,
                    row(xattn_norm_post), B, S)
    x3 = _ffn(x2, row(ffn_norm_pre), w_gate_up, w_down, row(ffn_norm_post))
    return x3.reshape(B, S, D_MODEL)


def kernel(x, mem, rel_bias, mix_norm_pre, w_in, cmp_pos_k, cmp_pos_v, cmp_w1_k, cmp_w2_k, cmp_w1_v, cmp_w2_v,
           conv_w, conv_b, mlstm_gate_bias, mlstm_norm, w_out, mix_norm_post, xattn_norm_pre, mem_norm, w_xq,
           w_xkv, w_xo, xattn_norm_post, ffn_norm_pre, w_gate_up, w_down, ffn_norm_post):
    depth = w_in.shape[0]
    for l in range(depth):
        x = _layer(x, mem, rel_bias, mix_norm_pre[l], w_in[l], cmp_pos_k[l], cmp_pos_v[l], cmp_w1_k[l],
                   cmp_w2_k[l], cmp_w1_v[l], cmp_w2_v[l], conv_w[l], conv_b[l], mlstm_gate_bias[l],
                   mlstm_norm[l], w_out[l], mix_norm_post[l], xattn_norm_pre[l], mem_norm[l], w_xq[l],
                   w_xkv[l], w_xo[l], xattn_norm_post[l], ffn_norm_pre[l], w_gate_up[l], w_down[l],
                   ffn_norm_post[l])
    return x
```

```python
import functools
import math

import numpy as np
import jax
import jax.numpy as jnp
from jax import lax
from jax.experimental import pallas as pl
from jax.experimental.pallas import tpu as pltpu

D_MODEL = 1024
NSA_WIDTH = 512
NSA_HEAD_DIM = 64
NSA_HEADS = 8
NSA_KV_HEADS = 2
NSA_GROUP = 4
NSA_KV_WIDTH = 128
CMP_STRIDE = 16
CMP_BLOCK = 32
CMP_HIDDEN = 256
SEL_BLOCK = 64
N_SELECT = 16
WINDOW = 512
Q_BLOCK = 128
FORCED_SCORE = 1.0e4
MLSTM_WIDTH = 512
MLSTM_HEADS = 4
MLSTM_V_DIM = 128
MLSTM_QK_DIM = 64
MLSTM_QK_WIDTH = 256
MLSTM_CHUNK = 64
CONV_WIDTH = 4
REL_BUCKETS = 32
REL_MAX_DISTANCE = 128
XATTN_HEADS = 4
XATTN_HEAD_DIM = 256
D_FF = 2816
NORM_EPS = 1e-6
NEG_INF = -1.0e30
LOG2E = math.log2(math.e)

IN_SIZES = (NSA_WIDTH,) + (NSA_KV_WIDTH,) * 6 + (NSA_HEADS * 3, MLSTM_QK_WIDTH, MLSTM_QK_WIDTH,
                                                 MLSTM_WIDTH, MLSTM_HEADS, MLSTM_HEADS, MLSTM_WIDTH)
IN_OFFSETS = tuple(int(o) for o in np.cumsum((0,) + IN_SIZES)[:-1])

LANES = 128
SUBLANES = 8
BF16_ROWS = 16
VMEM_LIMIT_BYTES = 56 * 1024 * 1024

N_GATE_ROWS = 32
F_TILE = 256


def _rms(x, gain):
    return x * lax.rsqrt(jnp.mean(x * x, axis=-1, keepdims=True) + NORM_EPS) * gain


def _nt(a, b):
    return lax.dot_general(a, b, (((1,), (1,)), ((), ())), preferred_element_type=jnp.float32)


def _mm(a, b):
    return jnp.dot(a, b, preferred_element_type=jnp.float32)


def _params(n_axes, flags=None):
    return pltpu.CompilerParams(dimension_semantics=("arbitrary",) * n_axes,
                                vmem_limit_bytes=VMEM_LIMIT_BYTES, flags=flags)


def _full(shape, buffers=None):
    nd = len(shape)
    mode = None if buffers is None else pl.Buffered(buffers)
    return pl.BlockSpec(shape, lambda *_: (0,) * nd, pipeline_mode=mode)


TM_IN = 1024
IN_HALVES = 2
TOK_DOT_WIDTH = 512
_TOK_GROUPS = (("kc", 128, jnp.float32), ("vc", 128, jnp.float32), ("ksl", 128, jnp.bfloat16),
               ("kwn", 128, jnp.bfloat16), ("mqk", 512, jnp.bfloat16))
_FEAT_GROUPS = (("qT", 512, jnp.bfloat16), ("vslT", 128, jnp.bfloat16), ("vwnT", 128, jnp.bfloat16),
                ("mvT", 512, jnp.bfloat16), ("moT", 512, jnp.bfloat16), ("smallT", N_GATE_ROWS, jnp.float32))


def _in_proj_kernel(x_ref, g_ref, wtok_ref, wfeat_ref, *out_refs):
    n_tok = len(_TOK_GROUPS)
    half = TM_IN // IN_HALVES

    def norm(i):
        return _rms(x_ref[i * half:(i + 1) * half, :], g_ref[...]).astype(jnp.bfloat16)

    def token_major(i, h):
        n_cols = sum(width for _, width, _ in _TOK_GROUPS)
        res = [_mm(h, wtok_ref[:, c0:c0 + TOK_DOT_WIDTH]) for c0 in range(0, n_cols, TOK_DOT_WIDTH)]
        off = 0
        for (name, width, dt), o_ref in zip(_TOK_GROUPS, out_refs[:n_tok]):
            r = res[off // TOK_DOT_WIDTH]
            lo = off % TOK_DOT_WIDTH
            o_ref[i * half:(i + 1) * half, :] = r[:, lo:lo + width].astype(dt)
            off += width

    def feature_major(i, h):
        off = 0
        for (name, rows, dt), o_ref in zip(_FEAT_GROUPS, out_refs[n_tok:]):
            r = _nt(wfeat_ref[off:off + rows, :], h)
            if name == "qT":
                r = r * (NSA_HEAD_DIM ** -0.5 * LOG2E)
            for j in range(half // LANES):
                o_ref[i * (half // LANES) + j] = r[:, j * LANES:(j + 1) * LANES].astype(dt)
            off += rows

    h = norm(0)
    for i in range(IN_HALVES):
        token_major(i, h)
        h_next = norm(i + 1) if i + 1 < IN_HALVES else None
        feature_major(i, h)
        h = h_next


def _in_proj(x2d, gain, w_tok, w_feat):
    T = x2d.shape[0]
    n_tok_cols = w_tok.shape[1]
    n_feat_rows = w_feat.shape[0]
    out_shape, out_specs = [], []
    for name, width, dt in _TOK_GROUPS:
        out_shape.append(jax.ShapeDtypeStruct((T, width), dt))
        out_specs.append(pl.BlockSpec((TM_IN, width), lambda i: (i, 0)))
    for name, rows, dt in _FEAT_GROUPS:
        out_shape.append(jax.ShapeDtypeStruct((T // LANES, rows, LANES), dt))
        out_specs.append(pl.BlockSpec((TM_IN // LANES, rows, LANES), lambda i: (i, 0, 0)))
    return pl.pallas_call(
        _in_proj_kernel,
        grid=(T // TM_IN,),
        in_specs=[pl.BlockSpec((TM_IN, D_MODEL), lambda i: (i, 0)),
                  _full((1, D_MODEL)),
                  _full((D_MODEL, n_tok_cols)),
                  _full((n_feat_rows, D_MODEL))],
        out_specs=out_specs,
        out_shape=out_shape,
        compiler_params=_params(1),
        name="in_proj",
    )(x2d, gain, w_tok, w_feat)


def _in_proj_weights(w_in):
    (nq, kc, vc, ksl, vsl, kwn, vwn, gt, mq, mk, mv, mi, mf, mo) = [
        w_in[:, o:o + s] for o, s in zip(IN_OFFSETS, IN_SIZES)]
    gt_r = gt.reshape(D_MODEL, NSA_KV_HEADS, NSA_GROUP, 3).transpose(0, 3, 1, 2).reshape(D_MODEL, 24)
    small = jnp.concatenate([gt_r, mi, mf], axis=1)
    w_tok = jnp.concatenate([kc, vc, ksl, kwn, mq, mk], axis=1)
    w_feat = jnp.concatenate([nq, vsl, vwn, mv, mo, small], axis=1).T
    return w_tok.astype(jnp.bfloat16), w_feat.astype(jnp.bfloat16)


N_CHUNK_COLS = CMP_STRIDE * NSA_KV_WIDTH
N_HID2 = NSA_KV_HEADS * CMP_HIDDEN


def _compress_one(tok_ref, pos_ref, w1_ref, n_chunks):
    bf16 = jnp.bfloat16
    zero = jnp.zeros((NSA_HEAD_DIM, CMP_HIDDEN), bf16)
    rows = [tok_ref[pl.ds(t, n_chunks, stride=CMP_STRIDE), :] for t in range(CMP_STRIDE)]
    halves = []
    for a in range(2):
        acc = None
        for t0 in range(0, CMP_STRIDE, 2):
            xs, ws = [], []
            for t in (t0, t0 + 1):
                xs.append((rows[t] + pos_ref[a:a + 1, t * NSA_KV_WIDTH:(t + 1) * NSA_KV_WIDTH]).astype(bf16))
                w = w1_ref[a, t]
                ws.append(jnp.concatenate([jnp.concatenate([w, zero], axis=1),
                                           jnp.concatenate([zero, w], axis=1)], axis=0))
            d = _mm(jnp.concatenate(xs, axis=1), jnp.concatenate(ws, axis=0))
            acc = d if acc is None else acc + d
        halves.append(acc)
    pre = halves[0] + pltpu.roll(halves[1], n_chunks - 1, 0)
    return (pre * jax.nn.sigmoid(pre)).astype(bf16)


def _compress_kernel(kc_ref, vc_ref, posk_ref, posv_ref, w1k_ref, w1v_ref, w2k_ref, w2vT_ref,
                     kcmp_ref, vcmpT_ref):
    n_chunks = kc_ref.shape[0] // CMP_STRIDE
    hid_k = _compress_one(kc_ref, posk_ref, w1k_ref, n_chunks)
    kcmp = _mm(hid_k, w2k_ref[...])
    row = lax.broadcasted_iota(jnp.int32, kcmp.shape, 0)
    kcmp_ref[...] = jnp.where(row < n_chunks - 1, kcmp, 0.0).astype(kcmp_ref.dtype)
    hid_v = _compress_one(vc_ref, posv_ref, w1v_ref, n_chunks)
    vcmpT = _nt(w2vT_ref[...], hid_v)
    col = lax.broadcasted_iota(jnp.int32, vcmpT.shape, 1)
    vcmpT_ref[...] = jnp.where(col < n_chunks - 1, vcmpT, 0.0).astype(vcmpT_ref.dtype)


def _compress_weights(pos, w1, w2):
    eye = jnp.eye(NSA_KV_HEADS, dtype=w1.dtype)
    w1r = w1.reshape(2, CMP_STRIDE, NSA_HEAD_DIM, CMP_HIDDEN)
    pos_e = jnp.broadcast_to(pos.reshape(2, CMP_STRIDE, 1, NSA_HEAD_DIM),
                             (2, CMP_STRIDE, NSA_KV_HEADS, NSA_HEAD_DIM)).reshape(2, N_CHUNK_COLS)
    w2e = jnp.einsum('jd,hg->hjgd', w2, eye).reshape(N_HID2, NSA_KV_WIDTH)
    return pos_e, w1r.astype(jnp.bfloat16), w2e.astype(jnp.bfloat16)


def _compress(kc, vc, cmp_pos_k, cmp_pos_v, cmp_w1_k, cmp_w2_k, cmp_w1_v, cmp_w2_v, B, S):
    n_chunks = S // CMP_STRIDE
    posk, w1k, w2k = _compress_weights(cmp_pos_k, cmp_w1_k, cmp_w2_k)
    posv, w1v, w2v = _compress_weights(cmp_pos_v, cmp_w1_v, cmp_w2_v)
    chunk_spec = pl.BlockSpec((S, NSA_KV_WIDTH), lambda b: (b, 0))
    return pl.pallas_call(
        _compress_kernel,
        grid=(B,),
        in_specs=[chunk_spec, chunk_spec,
                  _full((2, N_CHUNK_COLS)), _full((2, N_CHUNK_COLS)),
                  _full((2, CMP_STRIDE, NSA_HEAD_DIM, CMP_HIDDEN)), _full((2, CMP_STRIDE, NSA_HEAD_DIM, CMP_HIDDEN)),
                  _full((N_HID2, NSA_KV_WIDTH)), _full((NSA_KV_WIDTH, N_HID2))],
        out_specs=[pl.BlockSpec((None, n_chunks, NSA_KV_WIDTH), lambda b: (b, 0, 0)),
                   pl.BlockSpec((None, NSA_KV_WIDTH, n_chunks), lambda b: (b, 0, 0))],
        out_shape=[jax.ShapeDtypeStruct((B, n_chunks, NSA_KV_WIDTH), jnp.bfloat16),
                   jax.ShapeDtypeStruct((B, NSA_KV_WIDTH, n_chunks), jnp.bfloat16)],
        compiler_params=_params(1),
        name="compress",
    )(kc, vc, posk, posv, w1k, w1v, w2k, w2v.T)


GQ = NSA_GROUP * Q_BLOCK
TINY = 1e-30
CMP_TAB_ROWS = 512
CMP_TAB_ZERO = 248
CMP_TAB_LOOKUP = (232, 256)
SEL_STEP_SHIFT = 2
KEY_SUPER = Q_BLOCK << SEL_STEP_SHIFT
SEL_TAB_ZERO = KEY_SUPER + Q_BLOCK
SEL_TAB_ROWS = SEL_TAB_ZERO + KEY_SUPER
N_WIN_TILES = WINDOW // Q_BLOCK + 1
Q_PER_STEP = 8
SEL_SUB_CHUNKS = 2
V_ROWS = NSA_HEAD_DIM + BF16_ROWS


def _bucket_np(dist):
    n = np.maximum(dist, 0)
    max_exact = REL_BUCKETS // 2
    nf = np.maximum(n, 1).astype(np.float64)
    large = max_exact + (np.log(nf / max_exact) / math.log(REL_MAX_DISTANCE / max_exact)
                         * (REL_BUCKETS - max_exact)).astype(np.int64)
    large = np.minimum(large, REL_BUCKETS - 1)
    return np.where(n < max_exact, n, large).astype(np.int32)


def _bias_index_tables():
    m = np.arange(Q_BLOCK)[:, None]
    r = np.arange(Q_BLOCK)[None, :]
    diag = np.where(r - m >= 0, _bucket_np(r - m), -1).astype(np.int32)
    off = _bucket_np(Q_BLOCK + r - m)
    jp = np.arange(*CMP_TAB_LOOKUP)[:, None] - CMP_TAB_ZERO
    d_c = r - CMP_STRIDE * jp - (CMP_BLOCK - 1)
    cmp_idx = np.where(d_c >= 0, _bucket_np(d_c), -1).astype(np.int32)
    return diag, off, cmp_idx


def _bias_tables_kernel(rb_ref, diag_idx_ref, off_idx_ref, cmp_idx_ref, sel_ref, win_ref, cmp_ref):
    f32 = jnp.float32

    def lookup(idx, head):
        far = rb_ref[head, REL_BUCKETS - 1]
        acc = jnp.full(idx.shape, NEG_INF, f32)
        for k in range(REL_BUCKETS):
            acc = jnp.where(idx == k, (rb_ref[head, k] - far) * LOG2E, acc)
        return acc

    m_io = lax.broadcasted_iota(jnp.int32, (Q_BLOCK, Q_BLOCK), 0)
    r_io = lax.broadcasted_iota(jnp.int32, (Q_BLOCK, Q_BLOCK), 1)
    neg_tile = jnp.full((Q_BLOCK, Q_BLOCK), NEG_INF, f32)
    lo, hi = CMP_TAB_LOOKUP
    for h in range(NSA_KV_HEADS):
        for g in range(NSA_GROUP):
            head = h * NSA_GROUP + g
            sl = slice(g * Q_BLOCK, (g + 1) * Q_BLOCK)
            far = 0.0
            far_tile = jnp.full((Q_BLOCK, Q_BLOCK), far, f32)
            diag_v = lookup(diag_idx_ref[...], head)
            off_v = lookup(off_idx_ref[...], head)
            n_far = (SEL_TAB_ZERO - Q_BLOCK) // Q_BLOCK
            for t in range(SEL_TAB_ROWS // Q_BLOCK):
                rows = slice(t * Q_BLOCK, (t + 1) * Q_BLOCK)
                tile = far_tile if t < n_far else off_v if t == n_far else diag_v if t == n_far + 1 else neg_tile
                sel_ref[h, rows, sl] = tile
            win_ref[h, 0, :, sl] = diag_v
            win_ref[h, 1, :, sl] = off_v
            for back in range(2, N_WIN_TILES - 1):
                win_ref[h, back, :, sl] = far_tile
            win_ref[h, N_WIN_TILES - 1, :, sl] = jnp.where(r_io < m_io, far, NEG_INF)
            win_ref[h, N_WIN_TILES, :, sl] = neg_tile
            cmp_ref[h, 0:lo, sl] = jnp.full((lo, Q_BLOCK), far, f32)
            cmp_ref[h, lo:hi, sl] = lookup(cmp_idx_ref[...], head)
            cmp_ref[h, hi:CMP_TAB_ROWS, sl] = jnp.full((CMP_TAB_ROWS - hi, Q_BLOCK), NEG_INF, f32)


def _bias_tables(rel_bias):
    diag_idx, off_idx, cmp_idx = _bias_index_tables()
    shapes = [(NSA_KV_HEADS, SEL_TAB_ROWS, GQ), (NSA_KV_HEADS, N_WIN_TILES + 1, Q_BLOCK, GQ),
              (NSA_KV_HEADS, CMP_TAB_ROWS, GQ)]
    return pl.pallas_call(
        _bias_tables_kernel,
        in_specs=[pl.BlockSpec(memory_space=pltpu.SMEM),
                  _full(diag_idx.shape), _full(off_idx.shape), _full(cmp_idx.shape)],
        out_specs=[_full(s) for s in shapes],
        out_shape=[jax.ShapeDtypeStruct(s, jnp.float32) for s in shapes],
        grid=(1,),
        compiler_params=_params(1),
        name="bias_tables",
    )(rel_bias, jnp.asarray(diag_idx), jnp.asarray(off_idx), jnp.asarray(cmp_idx))


def _overlap_np(n_cmp_rows, n_sel):
    cmp_start = np.arange(n_cmp_rows) * CMP_STRIDE
    cmp_end = cmp_start + CMP_BLOCK - 1
    sel_start = np.arange(n_sel) * SEL_BLOCK
    ov = ((cmp_start[None, :] <= sel_start[:, None] + SEL_BLOCK - 1)
          & (cmp_end[None, :] >= sel_start[:, None])).astype(np.float32)
    ov[:, n_cmp_rows - 1] = 0.0
    return ov


def _tile4(a):
    return jnp.concatenate([a] * NSA_GROUP, axis=1)


def _select_blocks(score, score_ref, n_top, hooks=()):
    n_sel = score.shape[0]
    score_ref[...] = score
    n_grp = n_sel // SUBLANES
    grp = [score[SUBLANES * v:SUBLANES * (v + 1), :] for v in range(n_grp)]
    cnt = [jnp.zeros((SUBLANES, Q_BLOCK), jnp.int32) for _ in range(n_grp)]
    sub_io = lax.broadcasted_iota(jnp.int32, (SUBLANES, Q_BLOCK), 0)
    hook_at = {(i * n_sel) // len(hooks): hk for i, hk in enumerate(hooks)} if hooks else {}
    for jp in range(n_sel):
        if jp in hook_at:
            hook_at[jp]()
        row = score_ref[jp:jp + 1, :]
        for v in range(n_grp):
            if SUBLANES * v > jp:
                inc = (row >= grp[v]).astype(jnp.int32)
            elif SUBLANES * (v + 1) - 1 < jp:
                inc = (row > grp[v]).astype(jnp.int32)
            else:
                tie = (sub_io > jp - SUBLANES * v).astype(jnp.int32)
                inc = jnp.where(row > grp[v], 1, jnp.where(row == grp[v], tie, 0))
            cnt[v] = cnt[v] + inc
    return [jnp.where(cnt[v] < n_top, 0.0, NEG_INF) for v in range(n_grp)]


def _nsa_kernel(*refs):
    step = pl.program_id(1)
    tail = []
    for blk in range(Q_PER_STEP):
        tail = _nsa_block(step * Q_PER_STEP + blk, blk, tail, *refs)
    for part in tail:
        part()


def _nsa_block(c, blk, deferred, q_ref, kcmp_ref, vcmpT_ref, ksl_ref, vslT_ref, kwn_ref, vwnT_ref, gate_ref,
               seltab_ref, wintab_ref, cmptab_ref, ovl_ref, blkind_ref, out_ref, score_ref, selb_ref, sbufs_ref,
               swin_ref):
    sbuf_ref, sbuf2_ref = sbufs_ref.at[0], sbufs_ref.at[1]
    n_cmp = kcmp_ref.shape[0]
    n_sel = ovl_ref.shape[0]
    n_top = min(N_SELECT, n_sel)
    f32 = jnp.float32
    bf16 = jnp.bfloat16
    DH = NSA_HEAD_DIM
    heads = range(NSA_KV_HEADS)

    q = q_ref[blk]
    zq = jnp.zeros((DH, GQ), bf16)
    qcat, qpad = [], []
    for h in heads:
        qcat.append(jnp.concatenate([q[(h * NSA_GROUP + g) * DH:(h * NSA_GROUP + g + 1) * DH, :]
                                     for g in range(NSA_GROUP)], axis=1))
        qpad.append(jnp.concatenate([qcat[h], zq] if h == 0 else [zq, qcat[h]], axis=0))

    backs = list(range(N_WIN_TILES))
    kts = [jnp.maximum(c - back, 0) for back in backs]
    slots = [jnp.where(c >= back, back, N_WIN_TILES) for back in backs]
    m_win = [jnp.full((1, GQ), NEG_INF, f32) for _ in heads]

    def win_score(back, h):
        key0 = pl.multiple_of(kts[back] * Q_BLOCK, Q_BLOCK)
        s = _mm(kwn_ref[pl.ds(key0, Q_BLOCK), :], qpad[h]) + wintab_ref[h, slots[back]]
        swin_ref[h, back * Q_BLOCK:(back + 1) * Q_BLOCK, :] = s
        return jnp.max(s, axis=0, keepdims=True)

    cmp_off = pl.multiple_of(CMP_TAB_ZERO - (Q_BLOCK // CMP_STRIDE) * c, SUBLANES)
    kcmp = kcmp_ref[...]
    j_io = lax.broadcasted_iota(jnp.int32, (n_sel, Q_BLOCK), 0)
    r_io = lax.broadcasted_iota(jnp.int32, (n_sel, Q_BLOCK), 1)
    cur = (Q_BLOCK // SEL_BLOCK) * c + (r_io >= SEL_BLOCK).astype(jnp.int32)
    forced = (j_io == 0) | (j_io == cur) | (j_io == cur - 1)
    visible = j_io <= cur
    o_c = []
    for h in heads:
        tab = cmptab_ref[h, pl.ds(cmp_off, n_cmp), :]
        s = _mm(kcmp, qpad[h]) + tab
        m = jnp.maximum(jnp.max(s, axis=0, keepdims=True), 0.1 * NEG_INF)
        p = jnp.exp2(s - m)
        l = jnp.sum(p, axis=0, keepdims=True)
        pn = p * (1.0 / jnp.maximum(l, TINY))
        o_c.append(_mm(vcmpT_ref[h * DH:(h + 1) * DH, :], pn.astype(bf16)))
        psum = pn[:, 0:Q_BLOCK]
        for g in range(1, NSA_GROUP):
            psum = psum + pn[:, g * Q_BLOCK:(g + 1) * Q_BLOCK]
        imp = jnp.dot(ovl_ref[...], psum, precision=lax.Precision.HIGHEST,
                      preferred_element_type=f32)
        score = jnp.where(forced, FORCED_SCORE, jnp.where(visible, imp, -1.0))

        def hook(back, h=h):
            m_win[h] = jnp.maximum(m_win[h], win_score(back, h))

        hooks = [functools.partial(hook, back) for back in backs]
        if h == 0:
            for i, part in enumerate(deferred):
                hooks.insert(2 * i + 1, part)
        rows = _select_blocks(score, score_ref.at[h], n_top, hooks=hooks)
        for v, mask_rows in enumerate(rows):
            selb_ref[h, SUBLANES * v:SUBLANES * (v + 1), :] = mask_rows

    def ones_rows(n_keys):
        return jnp.ones((V_ROWS - DH, n_keys), bf16)

    acc_w = [jnp.zeros((V_ROWS, GQ), f32) for _ in heads]

    def win_value(back):
        for h in heads:
            p = jnp.exp2(swin_ref[h, back * Q_BLOCK:(back + 1) * Q_BLOCK, :] - m_win[h]).astype(bf16)
            vT = jnp.concatenate([vwnT_ref[kts[back], h * DH:(h + 1) * DH, :], ones_rows(Q_BLOCK)], axis=0)
            acc_w[h] = acc_w[h] + _mm(vT, p)

    blocks_per_step = KEY_SUPER // SEL_BLOCK
    tiles_per_step = KEY_SUPER // Q_BLOCK

    sub = KEY_SUPER // SEL_SUB_CHUNKS
    blocks_per_sub = sub // SEL_BLOCK
    tiles_per_sub = sub // Q_BLOCK

    own_lanes = [(lax.broadcasted_iota(jnp.int32, (sub, NSA_KV_WIDTH), 1) >= DH) == bool(h) for h in heads]

    def q_with_mask_rows(j, h):
        blk0 = pl.multiple_of(j * blocks_per_step, blocks_per_step)
        rows = jnp.concatenate([_tile4(selb_ref[h, pl.ds(blk0, blocks_per_step), :]),
                                jnp.zeros((DH - blocks_per_step, GQ), f32)], axis=0).astype(bf16)
        return jnp.concatenate([qcat[h], rows] if h == 0 else [rows, qcat[h]], axis=0)

    def score_chunk(j, ci, h, q_aug, buf_ref, near=True):
        key0 = pl.multiple_of(j * KEY_SUPER + ci * sub, sub)
        k_aug = jnp.where(own_lanes[h], ksl_ref[pl.ds(key0, sub), :], blkind_ref[h, ci * sub:(ci + 1) * sub, :])
        s = _mm(k_aug, q_aug)
        if near:
            tab_off = pl.multiple_of(
                jnp.maximum(j * KEY_SUPER - c * Q_BLOCK + SEL_TAB_ZERO, 0) + ci * sub, Q_BLOCK)
            s = s + seltab_ref[h, pl.ds(tab_off, sub), :]
        buf_ref[h, ci * sub:(ci + 1) * sub, :] = s
        return jnp.max(s, axis=0, keepdims=True)

    def value_chunk(j, ci, h, m_h, buf_ref):
        p = jnp.exp2(buf_ref[h, ci * sub:(ci + 1) * sub, :] - m_h).astype(bf16)
        vT = jnp.concatenate([vslT_ref[j * tiles_per_step + ci * tiles_per_sub + i, h * DH:(h + 1) * DH, :]
                              for i in range(tiles_per_sub)], axis=1)
        return _mm(jnp.concatenate([vT, ones_rows(sub)], axis=0), p)

    def values(j, m_old, m_cur, acc, src_ref, before_chunk=None):
        acc = [jnp.exp2(m_old[h] - m_cur[h]) * acc[h] for h in heads]
        for ci in range(SEL_SUB_CHUNKS):
            if before_chunk is not None:
                before_chunk(ci)
            for h in heads:
                acc[h] = acc[h] + value_chunk(j, ci, h, m_cur[h], src_ref)
        return tuple(acc)

    def sel_step(j, carry, src_ref, dst_ref, near=True):
        m_old, m_cur, acc = carry
        m_run = list(m_cur)
        q_aug = [q_with_mask_rows(j + 1, h) for h in heads]

        def scores(ci):
            for h in heads:
                m_run[h] = jnp.maximum(m_run[h], score_chunk(j + 1, ci, h, q_aug[h], dst_ref, near))

        acc = values(j, m_old, m_cur, acc, src_ref, before_chunk=scores)
        return m_cur, tuple(m_run), acc

    def sel_pair(i, carry, near):
        carry = sel_step(2 * i, carry, sbuf_ref, sbuf2_ref, near)
        return sel_step(2 * i + 1, carry, sbuf2_ref, sbuf_ref, near)

    n_steps = lax.shift_right_logical(c, SEL_STEP_SHIFT) + 1
    m_init = tuple(jnp.full((1, GQ), NEG_INF, f32) for _ in heads)
    acc_init = tuple(jnp.zeros((V_ROWS, GQ), f32) for _ in heads)
    m_first = list(m_init)
    win_order = list(backs)
    q_aug0 = [q_with_mask_rows(0, h) for h in heads]
    for ci in range(SEL_SUB_CHUNKS):
        for _ in range(-(-N_WIN_TILES // SEL_SUB_CHUNKS)):
            if win_order:
                win_value(win_order.pop(0))
        for h in heads:
            m_first[h] = jnp.maximum(m_first[h], score_chunk(0, ci, h, q_aug0[h], sbuf_ref))
    while win_order:
        win_value(win_order.pop(0))
    o_w = [a[0:DH, :] * (1.0 / a[DH:DH + 1, :]) for a in acc_w]
    n_piped = n_steps - 1
    odd = n_piped & 1
    n_far = lax.shift_right_logical(jnp.maximum(c - 1, 0), SEL_STEP_SHIFT)
    far_pairs = lax.shift_right_logical(jnp.maximum(n_far - 1, 0), 1)
    carry = lax.fori_loop(0, far_pairs, functools.partial(sel_pair, near=False),
                          (m_init, tuple(m_first), acc_init))
    carry = lax.fori_loop(far_pairs, lax.shift_right_logical(n_piped, 1), functools.partial(sel_pair, near=True),
                          carry)
    carry = lax.fori_loop(0, odd, lambda _, cr: sel_step(n_piped - 1, cr, sbuf_ref, sbuf2_ref), carry)
    m_old, m_cur, acc = carry
    acc = [jnp.exp2(m_old[h] - m_cur[h]) * acc[h] for h in heads]
    last_ref = sbufs_ref.at[odd]

    def last_values(ci):
        for h in heads:
            acc[h] = acc[h] + value_chunk(n_piped, ci, h, m_cur[h], last_ref)

    def combine():
        o_s = [a[0:DH, :] * (1.0 / jnp.maximum(a[DH:DH + 1, :], TINY)) for a in acc]
        for h in heads:
            ys = []
            for g in range(NSA_GROUP):
                sl = slice(g * Q_BLOCK, (g + 1) * Q_BLOCK)
                row0 = h * NSA_GROUP + g
                gates = [jax.nn.sigmoid(gate_ref[blk, kind * NSA_HEADS + row0:kind * NSA_HEADS + row0 + 1, :])
                         for kind in range(3)]
                ys.append(gates[0] * o_c[h][:, sl] + gates[1] * o_s[h][:, sl] + gates[2] * o_w[h][:, sl])
            yT = jnp.concatenate(ys, axis=0)
            for half in range(2):
                col = (2 * h + half) * LANES
                out_ref[blk * Q_BLOCK:(blk + 1) * Q_BLOCK, col:col + LANES] = (
                    yT[half * LANES:(half + 1) * LANES, :].T.astype(out_ref.dtype))

    return [functools.partial(last_values, ci) for ci in range(SEL_SUB_CHUNKS)] + [combine]


def _nsa(qT, kcmp, vcmpT, ksl, vslT, kwn, vwnT, smallT, tables, B, S):
    assert S % KEY_SUPER == 0
    nq = S // Q_BLOCK
    n_steps = nq // Q_PER_STEP
    n_cmp = S // CMP_STRIDE
    n_sel = S // SEL_BLOCK
    seltab, wintab, cmptab = tables
    ovl = jnp.asarray(_overlap_np(n_cmp, n_sel))
    key_blk = np.arange(KEY_SUPER)[:, None] // SEL_BLOCK
    lane = np.arange(NSA_KV_WIDTH)[None, :]
    blkind = jnp.asarray(np.stack([lane - NSA_HEAD_DIM == key_blk, lane == key_blk]), jnp.bfloat16)
    ksl3 = ksl.reshape(B, S, NSA_KV_WIDTH)
    kwn3 = kwn.reshape(B, S, NSA_KV_WIDTH)
    vslT4 = vslT.reshape(B, nq, NSA_KV_WIDTH, Q_BLOCK)
    vwnT4 = vwnT.reshape(B, nq, NSA_KV_WIDTH, Q_BLOCK)
    k_spec = pl.BlockSpec((None, S, NSA_KV_WIDTH), lambda b, c: (b, 0, 0))
    vT_spec = pl.BlockSpec((None, nq, NSA_KV_WIDTH, Q_BLOCK), lambda b, c: (b, 0, 0, 0))
    const = lambda a: pl.BlockSpec(a.shape, lambda b, c: (0,) * a.ndim)
    return pl.pallas_call(
        _nsa_kernel,
        grid=(B, n_steps),
        in_specs=[pl.BlockSpec((Q_PER_STEP, NSA_WIDTH, Q_BLOCK), lambda b, c: (b * n_steps + c, 0, 0)),
                  pl.BlockSpec((None, n_cmp, NSA_KV_WIDTH), lambda b, c: (b, 0, 0)),
                  pl.BlockSpec((None, NSA_KV_WIDTH, n_cmp), lambda b, c: (b, 0, 0)),
                  k_spec, vT_spec, k_spec, vT_spec,
                  pl.BlockSpec((Q_PER_STEP, N_GATE_ROWS, Q_BLOCK), lambda b, c: (b * n_steps + c, 0, 0)),
                  const(seltab), const(wintab), const(cmptab), const(ovl), const(blkind)],
        out_specs=pl.BlockSpec((Q_PER_STEP * Q_BLOCK, NSA_WIDTH), lambda b, c: (b * n_steps + c, 0)),
        out_shape=jax.ShapeDtypeStruct((B * S, NSA_WIDTH), jnp.bfloat16),
        scratch_shapes=[pltpu.VMEM((NSA_KV_HEADS, n_sel, Q_BLOCK), jnp.float32),
                        pltpu.VMEM((NSA_KV_HEADS, n_sel, Q_BLOCK), jnp.float32),
                        pltpu.VMEM((2, NSA_KV_HEADS, KEY_SUPER, GQ), jnp.float32),
                        pltpu.VMEM((NSA_KV_HEADS, N_WIN_TILES * Q_BLOCK, GQ), jnp.float32)],
        compiler_params=_params(2),
        name="nsa",
    )(qT, kcmp, vcmpT, ksl3, vslT4, kwn3, vwnT4, smallT, seltab, wintab, cmptab, ovl, blkind)


CONV_TILE = 256
I_ROW = 24
C_ROWS = MLSTM_V_DIM + BF16_ROWS


def _log_sigmoid(x):
    return jnp.minimum(x, 0.0) - jnp.log(1.0 + jnp.exp(-jnp.abs(x)))


def _mlstm_kernel(mqk_ref, mvT_ref, moT_ref, gates_ref, gb_ref, tri_ref, last_ref, convw_ref, convb_ref,
                  norm_ref, out_ref, qk_ref, rows_ref):
    S = mqk_ref.shape[0]
    nt = S // LANES
    L = MLSTM_CHUNK
    f32 = jnp.float32
    bf16 = jnp.bfloat16
    kscale_row = jnp.where(lax.broadcasted_iota(jnp.int32, (1, 2 * MLSTM_QK_WIDTH), 1) < MLSTM_QK_WIDTH,
                           1.0, MLSTM_QK_DIM ** -0.5)

    def conv_body(i, _):
        t0 = pl.multiple_of(i * CONV_TILE, CONV_TILE)
        cur = mqk_ref[pl.ds(t0, CONV_TILE), :].astype(f32)
        prev_start = pl.multiple_of(jnp.maximum(t0 - BF16_ROWS, 0), BF16_ROWS)
        prev = (mqk_ref[pl.ds(prev_start, BF16_ROWS), :].astype(f32)[BF16_ROWS - SUBLANES:, :]
                * jnp.where(i > 0, 1.0, 0.0))
        ext = jnp.concatenate([prev, cur], axis=0)
        y = convb_ref[...]
        for j in range(CONV_WIDTH):
            lo = SUBLANES - (CONV_WIDTH - 1) + j
            y = y + convw_ref[j:j + 1, :] * ext[lo:lo + CONV_TILE, :]
        y = y * jax.nn.sigmoid(y) * kscale_row
        qk_ref[pl.ds(t0, CONV_TILE), :] = y.astype(bf16)
        return 0

    lax.fori_loop(0, S // CONV_TILE, conv_body, 0)

    H = MLSTM_HEADS
    G8 = 2 * H
    n_rows = nt * G8
    a3 = gates_ref[:, I_ROW:I_ROW + G8, :] + gb_ref[...][None]
    is_f = lax.broadcasted_iota(jnp.int32, a3.shape, 1) >= H
    x = jnp.where(is_f, _log_sigmoid(a3), a3).reshape(n_rows, LANES)
    bcum = jnp.dot(x, tri_ref[...], precision=lax.Precision.HIGHEST, preferred_element_type=f32)
    b_rows = pltpu.roll(bcum, n_rows - H, 0)
    g_rows = x - b_rows
    pos = lax.broadcasted_iota(jnp.int32, (n_rows, LANES), 1) & (L - 1)

    def chunk_cummax(a):
        shift = 1
        while shift < L:
            a = jnp.where(pos >= shift, jnp.maximum(a, pltpu.roll(a, shift, 1)), a)
            shift *= 2
        return a

    def chunk_last(a):
        return jnp.dot(a, last_ref[...], precision=lax.Precision.HIGHEST, preferred_element_type=f32)

    bl_rows = chunk_last(b_rows)
    rows_ref[0] = g_rows
    rows_ref[1] = chunk_cummax(g_rows)
    rows_ref[2] = b_rows
    rows_ref[3] = bl_rows
    rows_ref[4] = chunk_last(chunk_cummax(bl_rows + g_rows))

    s_io = lax.broadcasted_iota(jnp.int32, (LANES, LANES), 0)
    t_io = lax.broadcasted_iota(jnp.int32, (LANES, LANES), 1)
    causal = (s_io <= t_io) & ((s_io >= L) == (t_io >= L))
    lane = lax.broadcasted_iota(jnp.int32, (1, LANES), 1)
    in_chunk = [lane < L, lane >= L]
    QD, VD = MLSTM_QK_DIM, MLSTM_V_DIM
    ones_aug = jnp.ones((C_ROWS - VD, LANES), bf16)
    head_lanes = [(lax.broadcasted_iota(jnp.int32, (LANES, LANES), 1) >= QD) == bool(par) for par in range(2)]

    def swap_halves(row):
        return pltpu.roll(row, L, 1)

    def tile_body(i, state):
        t0 = pl.multiple_of(i * LANES, LANES)
        r0 = pl.multiple_of(i * G8, G8)
        qk = qk_ref[pl.ds(t0, LANES), :]
        qk32 = qk.astype(f32)
        pairsT = [qk32[:, j * LANES:(j + 1) * LANES].T for j in range(4)]
        g8, cm8, b8, bl8, wm8 = [rows_ref[j, pl.ds(r0, G8), :] for j in range(5)]
        new_state = []
        for hh in range(H):
            pair, par = hh // 2, hh % 2
            g_r, cm_r, b_r, bl_r, wm_r = [a[hh:hh + 1, :] for a in (g8, cm8, b8, bl8, wm8)]
            caug, m_in = state[hh]
            qT = pairsT[pair][par * QD:(par + 1) * QD, :]
            kT = pairsT[2 + pair][par * QD:(par + 1) * QD, :]
            kpair = qk[:, (2 + pair) * LANES:(3 + pair) * LANES]
            qmask = jnp.where(head_lanes[par], qk[:, pair * LANES:(pair + 1) * LANES], jnp.zeros((), bf16))
            vaug = jnp.concatenate([mvT_ref[i, hh * VD:(hh + 1) * VD, :], ones_aug], axis=0)

            m_mid = swap_halves(jnp.maximum(bl_r + m_in, wm_r))
            m_prev = jnp.where(in_chunk[0], m_in, m_mid)
            m_next = jnp.maximum(bl_r + m_prev, wm_r)
            m_intra = b_r + cm_r
            m_inter = b_r + m_prev
            m_t = jnp.maximum(m_inter, m_intra)
            e_intra = jnp.exp(m_intra - m_t)
            e_inter = jnp.exp(m_inter - m_t)
            decay = jnp.exp(bl_r + m_prev - m_next)
            inject = jnp.exp(wm_r - m_next)

            g_mat = jnp.broadcast_to(g_r, (LANES, LANES)).T
            w = jnp.exp(jnp.where(causal, g_mat - cm_r, NEG_INF))
            st = _nt(kpair, qmask) * (w * e_intra)
            y = _mm(vaug, st.astype(bf16))
            kw = kT * jnp.exp(bl_r + g_r - wm_r)
            cs = caug
            for p in range(LANES // L):
                qs = jnp.where(in_chunk[p], qT * e_inter, 0.0).astype(bf16)
                y = y + _mm(cs.astype(bf16), qs)
                u = _nt(vaug, jnp.where(in_chunk[p], kw, 0.0).astype(bf16))
                dec = decay if p == 0 else swap_halves(decay)
                inj = inject if p == 0 else swap_halves(inject)
                cs = dec[:, 0:QD] * cs + inj[:, 0:QD] * u
            m_out = jnp.where(in_chunk[1], m_next, swap_halves(m_next))
            new_state.append((cs, m_out))

            den = y[VD:VD + 1, :]
            hT = y[0:VD, :] * (1.0 / jnp.maximum(jnp.abs(den), jnp.exp(-m_t)))
            sl = slice(hh * VD, (hh + 1) * VD)
            hn = hT * lax.rsqrt(jnp.mean(hT * hT, axis=0, keepdims=True) + NORM_EPS) * norm_ref[sl, :]
            yT = jax.nn.sigmoid(moT_ref[i, sl, :].astype(f32)) * hn
            out_ref[pl.ds(t0, LANES), sl] = yT.T.astype(out_ref.dtype)
        return tuple(new_state)

    init = tuple((jnp.zeros((C_ROWS, QD), f32), jnp.zeros((1, LANES), f32)) for _ in range(H))
    lax.fori_loop(0, nt, tile_body, init, unroll=8)


def _mlstm(mqk, mvT, moT, smallT, conv_w, conv_b, gate_bias, mlstm_norm, B, S):
    nt = S // LANES
    gb = jnp.broadcast_to(gate_bias.astype(jnp.float32).reshape(2 * MLSTM_HEADS, 1), (2 * MLSTM_HEADS, LANES))
    norm_cols = jnp.broadcast_to(mlstm_norm.astype(jnp.float32).reshape(MLSTM_WIDTH, 1), (MLSTM_WIDTH, LANES))
    lane = np.arange(LANES)
    same_chunk = lane[:, None] // MLSTM_CHUNK == lane[None, :] // MLSTM_CHUNK
    tri = (same_chunk & (lane[:, None] <= lane[None, :])).astype(np.float32)
    last = (same_chunk & (lane[:, None] % MLSTM_CHUNK == MLSTM_CHUNK - 1)).astype(np.float32)
    seq = lambda w: pl.BlockSpec((S, w), lambda b: (b, 0))
    tiles = lambda rows: pl.BlockSpec((nt, rows, LANES), lambda b: (b, 0, 0))
    return pl.pallas_call(
        _mlstm_kernel,
        grid=(B,),
        in_specs=[seq(2 * MLSTM_QK_WIDTH), tiles(MLSTM_WIDTH), tiles(MLSTM_WIDTH), tiles(N_GATE_ROWS),
                  _full((2 * MLSTM_HEADS, LANES)), _full((LANES, LANES)), _full((LANES, LANES)),
                  _full((CONV_WIDTH, 2 * MLSTM_QK_WIDTH)), _full((1, 2 * MLSTM_QK_WIDTH)),
                  _full((MLSTM_WIDTH, LANES))],
        out_specs=seq(MLSTM_WIDTH),
        out_shape=jax.ShapeDtypeStruct((B * S, MLSTM_WIDTH), jnp.bfloat16),
        scratch_shapes=[pltpu.VMEM((S, 2 * MLSTM_QK_WIDTH), jnp.bfloat16),
                        pltpu.VMEM((5, nt * 2 * MLSTM_HEADS, LANES), jnp.float32)],
        compiler_params=_params(1),
        name="mlstm",
    )(mqk, mvT, moT, smallT, gb, jnp.asarray(tri), jnp.asarray(last), conv_w, conv_b.reshape(1, -1), norm_cols)


def _mem_kv_kernel(mem_ref, g_ref, w_ref, k_ref, v_ref):
    mn = _rms(mem_ref[...], g_ref[...]).astype(jnp.bfloat16)
    k_ref[...] = _mm(mn, w_ref[:, :D_MODEL]).astype(k_ref.dtype)
    v_ref[...] = _mm(mn, w_ref[:, D_MODEL:]).astype(v_ref.dtype)


def _mem_kv(mem, gain, w_xkv):
    B, M, _ = mem.shape
    spec = pl.BlockSpec((None, M, D_MODEL), lambda b: (b, 0, 0))
    return pl.pallas_call(
        _mem_kv_kernel,
        grid=(B,),
        in_specs=[spec, _full((1, D_MODEL)), _full((D_MODEL, 2 * D_MODEL))],
        out_specs=[spec, spec],
        out_shape=[jax.ShapeDtypeStruct((B, M, D_MODEL), jnp.bfloat16)] * 2,
        compiler_params=_params(1),
        name="mem_kv",
    )(mem, gain, w_xkv.astype(jnp.bfloat16))


TM_X = 1024
X_HALVES = 2


def _mix_xattn_kernel(ynsa_ref, yml_ref, x_ref, wout_ref, gpost_ref, gpre_ref, wq_ref, k_ref, v_ref,
                      wo_ref, gpost2_ref, out_ref):
    bf16 = jnp.bfloat16
    halves = [slice(i * (TM_X // X_HALVES), (i + 1) * (TM_X // X_HALVES)) for i in range(X_HALVES)]
    y = [_mm(ynsa_ref[r, :], wout_ref[:NSA_WIDTH, :]) + _mm(yml_ref[r, :], wout_ref[NSA_WIDTH:, :]) for r in halves]
    x1 = [x_ref[r, :] + _rms(y[i], gpost_ref[...]) for i, r in enumerate(halves)]
    h2 = [_rms(x1[i], gpre_ref[...]).astype(bf16) for i in range(X_HALVES)]
    q = [(_mm(h2[i], wq_ref[...]) * (XATTN_HEAD_DIM ** -0.5)).astype(bf16) for i in range(X_HALVES)]
    outs = [[] for _ in range(X_HALVES)]
    for hh in range(XATTN_HEADS):
        sl = slice(hh * XATTN_HEAD_DIM, (hh + 1) * XATTN_HEAD_DIM)
        s = [_nt(q[i][:, sl], k_ref[:, sl]) for i in range(X_HALVES)]
        for i in range(X_HALVES):
            p = jnp.exp(s[i] - jnp.max(s[i], axis=1, keepdims=True))
            l = jnp.sum(p, axis=1, keepdims=True)
            outs[i].append((_mm(p.astype(bf16), v_ref[:, sl]) * (1.0 / l)).astype(bf16))
    y2 = [_mm(jnp.concatenate(outs[i], axis=1), wo_ref[...]) for i in range(X_HALVES)]
    for i, r in enumerate(halves):
        out_ref[r, :] = x1[i] + _rms(y2[i], gpost2_ref[...])


def _mix_xattn(ynsa, yml, x2d, w_out, g_post, g_pre, w_xq, kx, vx, w_xo, g_post2, B, S):
    nt = S // TM_X
    M = kx.shape[1]
    tok = lambda w: pl.BlockSpec((TM_X, w), lambda b, i: (b * nt + i, 0))
    mem_spec = pl.BlockSpec((None, M, D_MODEL), lambda b, i: (b, 0, 0))
    sq = _full((D_MODEL, D_MODEL))
    row = _full((1, D_MODEL))
    bf = lambda w: w.astype(jnp.bfloat16)
    return pl.pallas_call(
        _mix_xattn_kernel,
        grid=(B, nt),
        in_specs=[tok(NSA_WIDTH), tok(MLSTM_WIDTH), tok(D_MODEL), sq, row, row, sq, mem_spec, mem_spec,
                  sq, row],
        out_specs=tok(D_MODEL),
        out_shape=jax.ShapeDtypeStruct((B * S, D_MODEL), jnp.float32),
        compiler_params=_params(2),
        name="mix_xattn",
    )(ynsa, yml, x2d, bf(w_out), g_post, g_pre, bf(w_xq), kx, vx, bf(w_xo), g_post2)


TM_F = 1024
F_HALVES = 2


def _ffn_kernel(x_ref, gpre_ref, wgu_ref, wd_ref, gpost_ref, out_ref, acc_ref):
    bf16 = jnp.bfloat16
    n_chunks = D_FF // F_TILE
    half = TM_F // F_HALVES

    def pre(i):
        return _rms(x_ref[i * half:(i + 1) * half, :], gpre_ref[...]).astype(bf16)

    def chunk(i, h, j):
        rows = slice(i * half, (i + 1) * half)
        cols = slice(j * F_TILE, (j + 1) * F_TILE)
        g = _mm(h, wgu_ref[:, cols])
        u = _mm(h, wgu_ref[:, D_FF + j * F_TILE:D_FF + (j + 1) * F_TILE])
        act = (g * jax.nn.sigmoid(g) * u).astype(bf16)
        down = _mm(act, wd_ref[cols, :])
        if j == 0:
            acc_ref[rows, :] = down
        else:
            acc_ref[rows, :] += down

    def post(i):
        rows = slice(i * half, (i + 1) * half)
        out_ref[rows, :] = x_ref[rows, :] + _rms(acc_ref[rows, :], gpost_ref[...])

    h = pre(0)
    for i in range(F_HALVES):
        chunk(i, h, 0)
        if i > 0:
            post(i - 1)
        h_next = pre(i + 1) if i + 1 < F_HALVES else None
        for j in range(1, n_chunks):
            chunk(i, h, j)
        h = h_next
    post(F_HALVES - 1)


def _ffn(x2d, g_pre, w_gate_up, w_down, g_post):
    T = x2d.shape[0]
    tok = pl.BlockSpec((TM_F, D_MODEL), lambda i: (i, 0))
    row = _full((1, D_MODEL))
    return pl.pallas_call(
        _ffn_kernel,
        grid=(T // TM_F,),
        in_specs=[tok, row, _full((D_MODEL, 2 * D_FF), buffers=1), _full((D_FF, D_MODEL), buffers=1), row],
        out_specs=tok,
        out_shape=jax.ShapeDtypeStruct((T, D_MODEL), jnp.float32),
        scratch_shapes=[pltpu.VMEM((TM_F, D_MODEL), jnp.float32)],
        compiler_params=_params(1),
        name="ffn",
    )(x2d, g_pre, w_gate_up.astype(jnp.bfloat16), w_down.astype(jnp.bfloat16), g_post)


def _layer(x, mem, rel_bias, mix_norm_pre, w_in, cmp_pos_k, cmp_pos_v, cmp_w1_k, cmp_w2_k, cmp_w1_v,
           cmp_w2_v, conv_w, conv_b, mlstm_gate_bias, mlstm_norm, w_out, mix_norm_post, xattn_norm_pre,
           mem_norm, w_xq, w_xkv, w_xo, xattn_norm_post, ffn_norm_pre, w_gate_up, w_down, ffn_norm_post):
    B, S, _ = x.shape
    row = lambda g: g.reshape(1, -1).astype(jnp.float32)
    x2d = x.reshape(B * S, D_MODEL)
    w_tok, w_feat = _in_proj_weights(w_in)
    (kc, vc, ksl, kwn, mqk, qT, vslT, vwnT, mvT, moT, smallT) = _in_proj(
        x2d, row(mix_norm_pre), w_tok, w_feat)
    kcmp, vcmpT = _compress(kc, vc, cmp_pos_k, cmp_pos_v, cmp_w1_k, cmp_w2_k, cmp_w1_v, cmp_w2_v, B, S)
    tables = _bias_tables(rel_bias.astype(jnp.float32))
    ynsa = _nsa(qT, kcmp, vcmpT, ksl, vslT, kwn, vwnT, smallT, tables, B, S)
    yml = _mlstm(mqk, mvT, moT, smallT, conv_w, conv_b, mlstm_gate_bias, mlstm_norm, B, S)
    kx, vx = _mem_kv(mem, row(mem_norm), w_xkv)
    x2 = _mix_xattn(ynsa, yml, x2d, w_out, row(mix_norm_post), row(xattn_norm_pre), w_xq, kx, vx, w_xo,
                    row(xattn_norm_post), B, S)
    x3 = _ffn(x2, row(ffn_norm_pre), w_gate_up, w_down, row(ffn_norm_post))
    return x3.reshape(B, S, D_MODEL)


def kernel(x, mem, rel_bias, mix_norm_pre, w_in, cmp_pos_k, cmp_pos_v, cmp_w1_k, cmp_w2_k, cmp_w1_v, cmp_w2_v,
           conv_w, conv_b, mlstm_gate_bias, mlstm_norm, w_out, mix_norm_post, xattn_norm_pre, mem_norm, w_xq,
           w_xkv, w_xo, xattn_norm_post, ffn_norm_pre, w_gate_up, w_down, ffn_norm_post):
    depth = w_in.shape[0]
    for l in range(depth):
        x = _layer(x, mem, rel_bias, mix_norm_pre[l], w_in[l], cmp_pos_k[l], cmp_pos_v[l], cmp_w1_k[l],
                   cmp_w2_k[l], cmp_w1_v[l], cmp_w2_v[l], conv_w[l], conv_b[l], mlstm_gate_bias[l],
                   mlstm_norm[l], w_out[l], mix_norm_post[l], xattn_norm_pre[l], mem_norm[l], w_xq[l],
                   w_xkv[l], w_xo[l], xattn_norm_post[l], ffn_norm_pre[l], w_gate_up[l], w_down[l],
                   ffn_norm_post[l])
    return x
```

```python
import functools
import math

import numpy as np
import jax
import jax.numpy as jnp
from jax import lax
from jax.experimental import pallas as pl
from jax.experimental.pallas import tpu as pltpu

D_MODEL = 1024
NSA_WIDTH = 512
NSA_HEAD_DIM = 64
NSA_HEADS = 8
NSA_KV_HEADS = 2
NSA_GROUP = 4
NSA_KV_WIDTH = 128
CMP_STRIDE = 16
CMP_BLOCK = 32
CMP_HIDDEN = 256
SEL_BLOCK = 64
N_SELECT = 16
WINDOW = 512
Q_BLOCK = 128
FORCED_SCORE = 1.0e4
MLSTM_WIDTH = 512
MLSTM_HEADS = 4
MLSTM_V_DIM = 128
MLSTM_QK_DIM = 64
MLSTM_QK_WIDTH = 256
MLSTM_CHUNK = 64
CONV_WIDTH = 4
REL_BUCKETS = 32
REL_MAX_DISTANCE = 128
XATTN_HEADS = 4
XATTN_HEAD_DIM = 256
D_FF = 2816
NORM_EPS = 1e-6
NEG_INF = -1.0e30
LOG2E = math.log2(math.e)

IN_SIZES = (NSA_WIDTH,) + (NSA_KV_WIDTH,) * 6 + (NSA_HEADS * 3, MLSTM_QK_WIDTH, MLSTM_QK_WIDTH,
                                                 MLSTM_WIDTH, MLSTM_HEADS, MLSTM_HEADS, MLSTM_WIDTH)
IN_OFFSETS = tuple(int(o) for o in np.cumsum((0,) + IN_SIZES)[:-1])

LANES = 128
SUBLANES = 8
BF16_ROWS = 16
VMEM_LIMIT_BYTES = 56 * 1024 * 1024

N_GATE_ROWS = 32
F_TILE = 256


def _rms(x, gain):
    return x * lax.rsqrt(jnp.mean(x * x, axis=-1, keepdims=True) + NORM_EPS) * gain


def _nt(a, b):
    return lax.dot_general(a, b, (((1,), (1,)), ((), ())), preferred_element_type=jnp.float32)


def _mm(a, b):
    return jnp.dot(a, b, preferred_element_type=jnp.float32)


def _params(n_axes, flags=None):
    return pltpu.CompilerParams(dimension_semantics=("arbitrary",) * n_axes,
                                vmem_limit_bytes=VMEM_LIMIT_BYTES, flags=flags)


def _full(shape, buffers=None):
    nd = len(shape)
    mode = None if buffers is None else pl.Buffered(buffers)
    return pl.BlockSpec(shape, lambda *_: (0,) * nd, pipeline_mode=mode)


TM_IN = 1024
IN_HALVES = 2
TOK_DOT_WIDTH = 512
_TOK_GROUPS = (("kc", 128, jnp.float32), ("vc", 128, jnp.float32), ("ksl", 128, jnp.bfloat16),
               ("kwn", 128, jnp.bfloat16), ("mqk", 512, jnp.bfloat16))
_FEAT_GROUPS = (("qT", 512, jnp.bfloat16), ("vslT", 128, jnp.bfloat16), ("vwnT", 128, jnp.bfloat16),
                ("mvT", 512, jnp.bfloat16), ("moT", 512, jnp.bfloat16), ("smallT", N_GATE_ROWS, jnp.float32))


def _in_proj_kernel(x_ref, g_ref, wtok_ref, wfeat_ref, *out_refs):
    n_tok = len(_TOK_GROUPS)
    half = TM_IN // IN_HALVES

    def norm(i):
        return _rms(x_ref[i * half:(i + 1) * half, :], g_ref[...]).astype(jnp.bfloat16)

    def token_major(i, h):
        n_cols = sum(width for _, width, _ in _TOK_GROUPS)
        res = [_mm(h, wtok_ref[:, c0:c0 + TOK_DOT_WIDTH]) for c0 in range(0, n_cols, TOK_DOT_WIDTH)]
        off = 0
        for (name, width, dt), o_ref in zip(_TOK_GROUPS, out_refs[:n_tok]):
            r = res[off // TOK_DOT_WIDTH]
            lo = off % TOK_DOT_WIDTH
            o_ref[i * half:(i + 1) * half, :] = r[:, lo:lo + width].astype(dt)
            off += width

    def feature_major(i, h):
        off = 0
        for (name, rows, dt), o_ref in zip(_FEAT_GROUPS, out_refs[n_tok:]):
            r = _nt(wfeat_ref[off:off + rows, :], h)
            if name == "qT":
                r = r * (NSA_HEAD_DIM ** -0.5 * LOG2E)
            for j in range(half // LANES):
                o_ref[i * (half // LANES) + j] = r[:, j * LANES:(j + 1) * LANES].astype(dt)
            off += rows

    h = norm(0)
    for i in range(IN_HALVES):
        token_major(i, h)
        h_next = norm(i + 1) if i + 1 < IN_HALVES else None
        feature_major(i, h)
        h = h_next


def _in_proj(x2d, gain, w_tok, w_feat):
    T = x2d.shape[0]
    n_tok_cols = w_tok.shape[1]
    n_feat_rows = w_feat.shape[0]
    out_shape, out_specs = [], []
    for name, width, dt in _TOK_GROUPS:
        out_shape.append(jax.ShapeDtypeStruct((T, width), dt))
        out_specs.append(pl.BlockSpec((TM_IN, width), lambda i: (i, 0)))
    for name, rows, dt in _FEAT_GROUPS:
        out_shape.append(jax.ShapeDtypeStruct((T // LANES, rows, LANES), dt))
        out_specs.append(pl.BlockSpec((TM_IN // LANES, rows, LANES), lambda i: (i, 0, 0)))
    return pl.pallas_call(
        _in_proj_kernel,
        grid=(T // TM_IN,),
        in_specs=[pl.BlockSpec((TM_IN, D_MODEL), lambda i: (i, 0)),
                  _full((1, D_MODEL)),
                  _full((D_MODEL, n_tok_cols)),
                  _full((n_feat_rows, D_MODEL))],
        out_specs=out_specs,
        out_shape=out_shape,
        compiler_params=_params(1),
        name="in_proj",
    )(x2d, gain, w_tok, w_feat)


def _in_proj_weights(w_in):
    (nq, kc, vc, ksl, vsl, kwn, vwn, gt, mq, mk, mv, mi, mf, mo) = [
        w_in[:, o:o + s] for o, s in zip(IN_OFFSETS, IN_SIZES)]
    gt_r = gt.reshape(D_MODEL, NSA_KV_HEADS, NSA_GROUP, 3).transpose(0, 3, 1, 2).reshape(D_MODEL, 24)
    small = jnp.concatenate([gt_r, mi, mf], axis=1)
    w_tok = jnp.concatenate([kc, vc, ksl, kwn, mq, mk], axis=1)
    w_feat = jnp.concatenate([nq, vsl, vwn, mv, mo, small], axis=1).T
    return w_tok.astype(jnp.bfloat16), w_feat.astype(jnp.bfloat16)


N_CHUNK_COLS = CMP_STRIDE * NSA_KV_WIDTH
N_HID2 = NSA_KV_HEADS * CMP_HIDDEN


def _compress_one(tok_ref, pos_ref, w1_ref, n_chunks):
    bf16 = jnp.bfloat16
    zero = jnp.zeros((NSA_HEAD_DIM, CMP_HIDDEN), bf16)
    rows = [tok_ref[pl.ds(t, n_chunks, stride=CMP_STRIDE), :] for t in range(CMP_STRIDE)]
    halves = []
    for a in range(2):
        acc = None
        for t0 in range(0, CMP_STRIDE, 2):
            xs, ws = [], []
            for t in (t0, t0 + 1):
                xs.append((rows[t] + pos_ref[a:a + 1, t * NSA_KV_WIDTH:(t + 1) * NSA_KV_WIDTH]).astype(bf16))
                w = w1_ref[a, t]
                ws.append(jnp.concatenate([jnp.concatenate([w, zero], axis=1),
                                           jnp.concatenate([zero, w], axis=1)], axis=0))
            d = _mm(jnp.concatenate(xs, axis=1), jnp.concatenate(ws, axis=0))
            acc = d if acc is None else acc + d
        halves.append(acc)
    pre = halves[0] + pltpu.roll(halves[1], n_chunks - 1, 0)
    return (pre * jax.nn.sigmoid(pre)).astype(bf16)


def _compress_kernel(kc_ref, vc_ref, posk_ref, posv_ref, w1k_ref, w1v_ref, w2k_ref, w2vT_ref,
                     kcmp_ref, vcmpT_ref):
    n_chunks = kc_ref.shape[0] // CMP_STRIDE
    hid_k = _compress_one(kc_ref, posk_ref, w1k_ref, n_chunks)
    kcmp = _mm(hid_k, w2k_ref[...])
    row = lax.broadcasted_iota(jnp.int32, kcmp.shape, 0)
    kcmp_ref[...] = jnp.where(row < n_chunks - 1, kcmp, 0.0).astype(kcmp_ref.dtype)
    hid_v = _compress_one(vc_ref, posv_ref, w1v_ref, n_chunks)
    vcmpT = _nt(w2vT_ref[...], hid_v)
    col = lax.broadcasted_iota(jnp.int32, vcmpT.shape, 1)
    vcmpT_ref[...] = jnp.where(col < n_chunks - 1, vcmpT, 0.0).astype(vcmpT_ref.dtype)


def _compress_weights(pos, w1, w2):
    eye = jnp.eye(NSA_KV_HEADS, dtype=w1.dtype)
    w1r = w1.reshape(2, CMP_STRIDE, NSA_HEAD_DIM, CMP_HIDDEN)
    pos_e = jnp.broadcast_to(pos.reshape(2, CMP_STRIDE, 1, NSA_HEAD_DIM),
                             (2, CMP_STRIDE, NSA_KV_HEADS, NSA_HEAD_DIM)).reshape(2, N_CHUNK_COLS)
    w2e = jnp.einsum('jd,hg->hjgd', w2, eye).reshape(N_HID2, NSA_KV_WIDTH)
    return pos_e, w1r.astype(jnp.bfloat16), w2e.astype(jnp.bfloat16)


def _compress(kc, vc, cmp_pos_k, cmp_pos_v, cmp_w1_k, cmp_w2_k, cmp_w1_v, cmp_w2_v, B, S):
    n_chunks = S // CMP_STRIDE
    posk, w1k, w2k = _compress_weights(cmp_pos_k, cmp_w1_k, cmp_w2_k)
    posv, w1v, w2v = _compress_weights(cmp_pos_v, cmp_w1_v, cmp_w2_v)
    chunk_spec = pl.BlockSpec((S, NSA_KV_WIDTH), lambda b: (b, 0))
    return pl.pallas_call(
        _compress_kernel,
        grid=(B,),
        in_specs=[chunk_spec, chunk_spec,
                  _full((2, N_CHUNK_COLS)), _full((2, N_CHUNK_COLS)),
                  _full((2, CMP_STRIDE, NSA_HEAD_DIM, CMP_HIDDEN)), _full((2, CMP_STRIDE, NSA_HEAD_DIM, CMP_HIDDEN)),
                  _full((N_HID2, NSA_KV_WIDTH)), _full((NSA_KV_WIDTH, N_HID2))],
        out_specs=[pl.BlockSpec((None, n_chunks, NSA_KV_WIDTH), lambda b: (b, 0, 0)),
                   pl.BlockSpec((None, NSA_KV_WIDTH, n_chunks), lambda b: (b, 0, 0))],
        out_shape=[jax.ShapeDtypeStruct((B, n_chunks, NSA_KV_WIDTH), jnp.bfloat16),
                   jax.ShapeDtypeStruct((B, NSA_KV_WIDTH, n_chunks), jnp.bfloat16)],
        compiler_params=_params(1),
        name="compress",
    )(kc, vc, posk, posv, w1k, w1v, w2k, w2v.T)


GQ = NSA_GROUP * Q_BLOCK
TINY = 1e-30
CMP_TAB_ROWS = 512
CMP_TAB_ZERO = 248
CMP_TAB_LOOKUP = (232, 256)
SEL_STEP_SHIFT = 2
KEY_SUPER = Q_BLOCK << SEL_STEP_SHIFT
SEL_TAB_ZERO = KEY_SUPER + Q_BLOCK
SEL_TAB_ROWS = SEL_TAB_ZERO + KEY_SUPER
N_WIN_TILES = WINDOW // Q_BLOCK + 1
Q_PER_STEP = 4
SEL_SUB_CHUNKS = 2
V_ROWS = NSA_HEAD_DIM + BF16_ROWS


def _bucket_np(dist):
    n = np.maximum(dist, 0)
    max_exact = REL_BUCKETS // 2
    nf = np.maximum(n, 1).astype(np.float64)
    large = max_exact + (np.log(nf / max_exact) / math.log(REL_MAX_DISTANCE / max_exact)
                         * (REL_BUCKETS - max_exact)).astype(np.int64)
    large = np.minimum(large, REL_BUCKETS - 1)
    return np.where(n < max_exact, n, large).astype(np.int32)


def _bias_index_tables():
    m = np.arange(Q_BLOCK)[:, None]
    r = np.arange(Q_BLOCK)[None, :]
    diag = np.where(r - m >= 0, _bucket_np(r - m), -1).astype(np.int32)
    off = _bucket_np(Q_BLOCK + r - m)
    jp = np.arange(*CMP_TAB_LOOKUP)[:, None] - CMP_TAB_ZERO
    d_c = r - CMP_STRIDE * jp - (CMP_BLOCK - 1)
    cmp_idx = np.where(d_c >= 0, _bucket_np(d_c), -1).astype(np.int32)
    return diag, off, cmp_idx


def _bias_tables_kernel(rb_ref, diag_idx_ref, off_idx_ref, cmp_idx_ref, sel_ref, win_ref, cmp_ref):
    f32 = jnp.float32

    def lookup(idx, head):
        far = rb_ref[head, REL_BUCKETS - 1]
        acc = jnp.full(idx.shape, NEG_INF, f32)
        for k in range(REL_BUCKETS):
            acc = jnp.where(idx == k, (rb_ref[head, k] - far) * LOG2E, acc)
        return acc

    m_io = lax.broadcasted_iota(jnp.int32, (Q_BLOCK, Q_BLOCK), 0)
    r_io = lax.broadcasted_iota(jnp.int32, (Q_BLOCK, Q_BLOCK), 1)
    neg_tile = jnp.full((Q_BLOCK, Q_BLOCK), NEG_INF, f32)
    lo, hi = CMP_TAB_LOOKUP
    for h in range(NSA_KV_HEADS):
        for g in range(NSA_GROUP):
            head = h * NSA_GROUP + g
            sl = slice(g * Q_BLOCK, (g + 1) * Q_BLOCK)
            far = 0.0
            far_tile = jnp.full((Q_BLOCK, Q_BLOCK), far, f32)
            diag_v = lookup(diag_idx_ref[...], head)
            off_v = lookup(off_idx_ref[...], head)
            n_far = (SEL_TAB_ZERO - Q_BLOCK) // Q_BLOCK
            for t in range(SEL_TAB_ROWS // Q_BLOCK):
                rows = slice(t * Q_BLOCK, (t + 1) * Q_BLOCK)
                tile = far_tile if t < n_far else off_v if t == n_far else diag_v if t == n_far + 1 else neg_tile
                sel_ref[h, rows, sl] = tile
            win_ref[h, 0, :, sl] = diag_v
            win_ref[h, 1, :, sl] = off_v
            for back in range(2, N_WIN_TILES - 1):
                win_ref[h, back, :, sl] = far_tile
            win_ref[h, N_WIN_TILES - 1, :, sl] = jnp.where(r_io < m_io, far, NEG_INF)
            win_ref[h, N_WIN_TILES, :, sl] = neg_tile
            cmp_ref[h, 0:lo, sl] = jnp.full((lo, Q_BLOCK), far, f32)
            cmp_ref[h, lo:hi, sl] = lookup(cmp_idx_ref[...], head)
            cmp_ref[h, hi:CMP_TAB_ROWS, sl] = jnp.full((CMP_TAB_ROWS - hi, Q_BLOCK), NEG_INF, f32)


def _bias_tables(rel_bias):
    diag_idx, off_idx, cmp_idx = _bias_index_tables()
    shapes = [(NSA_KV_HEADS, SEL_TAB_ROWS, GQ), (NSA_KV_HEADS, N_WIN_TILES + 1, Q_BLOCK, GQ),
              (NSA_KV_HEADS, CMP_TAB_ROWS, GQ)]
    return pl.pallas_call(
        _bias_tables_kernel,
        in_specs=[pl.BlockSpec(memory_space=pltpu.SMEM),
                  _full(diag_idx.shape), _full(off_idx.shape), _full(cmp_idx.shape)],
        out_specs=[_full(s) for s in shapes],
        out_shape=[jax.ShapeDtypeStruct(s, jnp.float32) for s in shapes],
        grid=(1,),
        compiler_params=_params(1),
        name="bias_tables",
    )(rel_bias, jnp.asarray(diag_idx), jnp.asarray(off_idx), jnp.asarray(cmp_idx))


def _overlap_np(n_cmp_rows, n_sel):
    cmp_start = np.arange(n_cmp_rows) * CMP_STRIDE
    cmp_end = cmp_start + CMP_BLOCK - 1
    sel_start = np.arange(n_sel) * SEL_BLOCK
    ov = ((cmp_start[None, :] <= sel_start[:, None] + SEL_BLOCK - 1)
          & (cmp_end[None, :] >= sel_start[:, None])).astype(np.float32)
    ov[:, n_cmp_rows - 1] = 0.0
    return ov


def _tile4(a):
    return jnp.concatenate([a] * NSA_GROUP, axis=1)


def _select_blocks(score, score_ref, n_top, hooks=()):
    n_sel = score.shape[0]
    score_ref[...] = score
    n_grp = n_sel // SUBLANES
    grp = [score[SUBLANES * v:SUBLANES * (v + 1), :] for v in range(n_grp)]
    cnt = [jnp.zeros((SUBLANES, Q_BLOCK), jnp.int32) for _ in range(n_grp)]
    sub_io = lax.broadcasted_iota(jnp.int32, (SUBLANES, Q_BLOCK), 0)
    hook_at = {(i * n_sel) // len(hooks): hk for i, hk in enumerate(hooks)} if hooks else {}
    for jp in range(n_sel):
        if jp in hook_at:
            hook_at[jp]()
        row = score_ref[jp:jp + 1, :]
        for v in range(n_grp):
            if SUBLANES * v > jp:
                inc = (row >= grp[v]).astype(jnp.int32)
            elif SUBLANES * (v + 1) - 1 < jp:
                inc = (row > grp[v]).astype(jnp.int32)
            else:
                tie = (sub_io > jp - SUBLANES * v).astype(jnp.int32)
                inc = jnp.where(row > grp[v], 1, jnp.where(row == grp[v], tie, 0))
            cnt[v] = cnt[v] + inc
    return [jnp.where(cnt[v] < n_top, 0.0, NEG_INF) for v in range(n_grp)]


def _nsa_kernel(*refs):
    step = pl.program_id(1)
    tail = []
    for blk in range(Q_PER_STEP):
        tail = _nsa_block(step * Q_PER_STEP + blk, blk, tail, *refs)
    for part in tail:
        part()


def _nsa_block(c, blk, deferred, q_ref, kcmp_ref, vcmpT_ref, ksl_ref, vslT_ref, kwn_ref, vwnT_ref, gate_ref,
               seltab_ref, wintab_ref, cmptab_ref, ovl_ref, blkind_ref, out_ref, score_ref, selb_ref, sbufs_ref,
               swin_ref):
    sbuf_ref, sbuf2_ref = sbufs_ref.at[0], sbufs_ref.at[1]
    n_cmp = kcmp_ref.shape[0]
    n_sel = ovl_ref.shape[0]
    n_top = min(N_SELECT, n_sel)
    f32 = jnp.float32
    bf16 = jnp.bfloat16
    DH = NSA_HEAD_DIM
    heads = range(NSA_KV_HEADS)

    q = q_ref[blk]
    zq = jnp.zeros((DH, GQ), bf16)
    qcat, qpad = [], []
    for h in heads:
        qcat.append(jnp.concatenate([q[(h * NSA_GROUP + g) * DH:(h * NSA_GROUP + g + 1) * DH, :]
                                     for g in range(NSA_GROUP)], axis=1))
        qpad.append(jnp.concatenate([qcat[h], zq] if h == 0 else [zq, qcat[h]], axis=0))

    backs = list(range(N_WIN_TILES))
    kts = [jnp.maximum(c - back, 0) for back in backs]
    slots = [jnp.where(c >= back, back, N_WIN_TILES) for back in backs]
    m_win = [jnp.full((1, GQ), NEG_INF, f32) for _ in heads]

    def win_score(back, h):
        key0 = pl.multiple_of(kts[back] * Q_BLOCK, Q_BLOCK)
        s = _mm(kwn_ref[pl.ds(key0, Q_BLOCK), :], qpad[h]) + wintab_ref[h, slots[back]]
        swin_ref[h, back * Q_BLOCK:(back + 1) * Q_BLOCK, :] = s
        return jnp.max(s, axis=0, keepdims=True)

    cmp_off = pl.multiple_of(CMP_TAB_ZERO - (Q_BLOCK // CMP_STRIDE) * c, SUBLANES)
    kcmp = kcmp_ref[...]
    j_io = lax.broadcasted_iota(jnp.int32, (n_sel, Q_BLOCK), 0)
    r_io = lax.broadcasted_iota(jnp.int32, (n_sel, Q_BLOCK), 1)
    cur = (Q_BLOCK // SEL_BLOCK) * c + (r_io >= SEL_BLOCK).astype(jnp.int32)
    forced = (j_io == 0) | (j_io == cur) | (j_io == cur - 1)
    visible = j_io <= cur
    o_c = []
    for h in heads:
        tab = cmptab_ref[h, pl.ds(cmp_off, n_cmp), :]
        s = _mm(kcmp, qpad[h]) + tab
        m = jnp.maximum(jnp.max(s, axis=0, keepdims=True), 0.1 * NEG_INF)
        p = jnp.exp2(s - m)
        l = jnp.sum(p, axis=0, keepdims=True)
        pn = p * (1.0 / jnp.maximum(l, TINY))
        o_c.append(_mm(vcmpT_ref[h * DH:(h + 1) * DH, :], pn.astype(bf16)))
        psum = pn[:, 0:Q_BLOCK]
        for g in range(1, NSA_GROUP):
            psum = psum + pn[:, g * Q_BLOCK:(g + 1) * Q_BLOCK]
        imp = jnp.dot(ovl_ref[...], psum, precision=lax.Precision.HIGHEST,
                      preferred_element_type=f32)
        score = jnp.where(forced, FORCED_SCORE, jnp.where(visible, imp, -1.0))

        def hook(back, h=h):
            m_win[h] = jnp.maximum(m_win[h], win_score(back, h))

        hooks = [functools.partial(hook, back) for back in backs]
        if h == 0:
            for i, part in enumerate(deferred):
                hooks.insert(2 * i + 1, part)
        rows = _select_blocks(score, score_ref.at[h], n_top, hooks=hooks)
        for v, mask_rows in enumerate(rows):
            selb_ref[h, SUBLANES * v:SUBLANES * (v + 1), :] = mask_rows

    def ones_rows(n_keys):
        return jnp.ones((V_ROWS - DH, n_keys), bf16)

    acc_w = [jnp.zeros((V_ROWS, GQ), f32) for _ in heads]

    def win_value(back):
        for h in heads:
            p = jnp.exp2(swin_ref[h, back * Q_BLOCK:(back + 1) * Q_BLOCK, :] - m_win[h]).astype(bf16)
            vT = jnp.concatenate([vwnT_ref[kts[back], h * DH:(h + 1) * DH, :], ones_rows(Q_BLOCK)], axis=0)
            acc_w[h] = acc_w[h] + _mm(vT, p)

    blocks_per_step = KEY_SUPER // SEL_BLOCK
    tiles_per_step = KEY_SUPER // Q_BLOCK

    sub = KEY_SUPER // SEL_SUB_CHUNKS
    blocks_per_sub = sub // SEL_BLOCK
    tiles_per_sub = sub // Q_BLOCK

    own_lanes = [(lax.broadcasted_iota(jnp.int32, (sub, NSA_KV_WIDTH), 1) >= DH) == bool(h) for h in heads]

    def q_with_mask_rows(j, h):
        blk0 = pl.multiple_of(j * blocks_per_step, blocks_per_step)
        rows = jnp.concatenate([_tile4(selb_ref[h, pl.ds(blk0, blocks_per_step), :]),
                                jnp.zeros((DH - blocks_per_step, GQ), f32)], axis=0).astype(bf16)
        return jnp.concatenate([qcat[h], rows] if h == 0 else [rows, qcat[h]], axis=0)

    def score_chunk(j, ci, h, q_aug, buf_ref, near=True):
        key0 = pl.multiple_of(j * KEY_SUPER + ci * sub, sub)
        k_aug = jnp.where(own_lanes[h], ksl_ref[pl.ds(key0, sub), :], blkind_ref[h, ci * sub:(ci + 1) * sub, :])
        s = _mm(k_aug, q_aug)
        if near:
            tab_off = pl.multiple_of(
                jnp.maximum(j * KEY_SUPER - c * Q_BLOCK + SEL_TAB_ZERO, 0) + ci * sub, Q_BLOCK)
            s = s + seltab_ref[h, pl.ds(tab_off, sub), :]
        buf_ref[h, ci * sub:(ci + 1) * sub, :] = s
        return jnp.max(s, axis=0, keepdims=True)

    def value_chunk(j, ci, h, m_h, buf_ref):
        p = jnp.exp2(buf_ref[h, ci * sub:(ci + 1) * sub, :] - m_h).astype(bf16)
        vT = jnp.concatenate([vslT_ref[j * tiles_per_step + ci * tiles_per_sub + i, h * DH:(h + 1) * DH, :]
                              for i in range(tiles_per_sub)], axis=1)
        return _mm(jnp.concatenate([vT, ones_rows(sub)], axis=0), p)

    def values(j, m_old, m_cur, acc, src_ref, before_chunk=None):
        acc = [jnp.exp2(m_old[h] - m_cur[h]) * acc[h] for h in heads]
        for ci in range(SEL_SUB_CHUNKS):
            if before_chunk is not None:
                before_chunk(ci)
            for h in heads:
                acc[h] = acc[h] + value_chunk(j, ci, h, m_cur[h], src_ref)
        return tuple(acc)

    def sel_step(j, carry, src_ref, dst_ref, near=True):
        m_old, m_cur, acc = carry
        m_run = list(m_cur)
        q_aug = [q_with_mask_rows(j + 1, h) for h in heads]

        def scores(ci):
            for h in heads:
                m_run[h] = jnp.maximum(m_run[h], score_chunk(j + 1, ci, h, q_aug[h], dst_ref, near))

        acc = values(j, m_old, m_cur, acc, src_ref, before_chunk=scores)
        return m_cur, tuple(m_run), acc

    def sel_pair(i, carry, near):
        carry = sel_step(2 * i, carry, sbuf_ref, sbuf2_ref, near)
        return sel_step(2 * i + 1, carry, sbuf2_ref, sbuf_ref, near)

    n_steps = lax.shift_right_logical(c, SEL_STEP_SHIFT) + 1
    m_init = tuple(jnp.full((1, GQ), NEG_INF, f32) for _ in heads)
    acc_init = tuple(jnp.zeros((V_ROWS, GQ), f32) for _ in heads)
    m_first = list(m_init)
    win_order = list(backs)
    q_aug0 = [q_with_mask_rows(0, h) for h in heads]
    for ci in range(SEL_SUB_CHUNKS):
        for _ in range(-(-N_WIN_TILES // SEL_SUB_CHUNKS)):
            if win_order:
                win_value(win_order.pop(0))
        for h in heads:
            m_first[h] = jnp.maximum(m_first[h], score_chunk(0, ci, h, q_aug0[h], sbuf_ref))
    while win_order:
        win_value(win_order.pop(0))
    o_w = [a[0:DH, :] * (1.0 / a[DH:DH + 1, :]) for a in acc_w]
    n_piped = n_steps - 1
    odd = n_piped & 1
    n_far = lax.shift_right_logical(jnp.maximum(c - 1, 0), SEL_STEP_SHIFT)
    far_pairs = lax.shift_right_logical(jnp.maximum(n_far - 1, 0), 1)
    carry = lax.fori_loop(0, far_pairs, functools.partial(sel_pair, near=False),
                          (m_init, tuple(m_first), acc_init))
    carry = lax.fori_loop(far_pairs, lax.shift_right_logical(n_piped, 1), functools.partial(sel_pair, near=True),
                          carry)
    carry = lax.fori_loop(0, odd, lambda _, cr: sel_step(n_piped - 1, cr, sbuf_ref, sbuf2_ref), carry)
    m_old, m_cur, acc = carry
    acc = [jnp.exp2(m_old[h] - m_cur[h]) * acc[h] for h in heads]
    last_ref = sbufs_ref.at[odd]

    def last_values(ci):
        for h in heads:
            acc[h] = acc[h] + value_chunk(n_piped, ci, h, m_cur[h], last_ref)

    def combine():
        o_s = [a[0:DH, :] * (1.0 / jnp.maximum(a[DH:DH + 1, :], TINY)) for a in acc]
        for h in heads:
            ys = []
            for g in range(NSA_GROUP):
                sl = slice(g * Q_BLOCK, (g + 1) * Q_BLOCK)
                row0 = h * NSA_GROUP + g
                gates = [jax.nn.sigmoid(gate_ref[blk, kind * NSA_HEADS + row0:kind * NSA_HEADS + row0 + 1, :])
                         for kind in range(3)]
                ys.append(gates[0] * o_c[h][:, sl] + gates[1] * o_s[h][:, sl] + gates[2] * o_w[h][:, sl])
            yT = jnp.concatenate(ys, axis=0)
            for half in range(2):
                col = (2 * h + half) * LANES
                out_ref[blk * Q_BLOCK:(blk + 1) * Q_BLOCK, col:col + LANES] = (
                    yT[half * LANES:(half + 1) * LANES, :].T.astype(out_ref.dtype))

    return [functools.partial(last_values, ci) for ci in range(SEL_SUB_CHUNKS)] + [combine]


def _nsa(qT, kcmp, vcmpT, ksl, vslT, kwn, vwnT, smallT, tables, B, S):
    assert S % KEY_SUPER == 0
    nq = S // Q_BLOCK
    n_steps = nq // Q_PER_STEP
    n_cmp = S // CMP_STRIDE
    n_sel = S // SEL_BLOCK
    seltab, wintab, cmptab = tables
    ovl = jnp.asarray(_overlap_np(n_cmp, n_sel))
    key_blk = np.arange(KEY_SUPER)[:, None] // SEL_BLOCK
    lane = np.arange(NSA_KV_WIDTH)[None, :]
    blkind = jnp.asarray(np.stack([lane - NSA_HEAD_DIM == key_blk, lane == key_blk]), jnp.bfloat16)
    ksl3 = ksl.reshape(B, S, NSA_KV_WIDTH)
    kwn3 = kwn.reshape(B, S, NSA_KV_WIDTH)
    vslT4 = vslT.reshape(B, nq, NSA_KV_WIDTH, Q_BLOCK)
    vwnT4 = vwnT.reshape(B, nq, NSA_KV_WIDTH, Q_BLOCK)
    k_spec = pl.BlockSpec((None, S, NSA_KV_WIDTH), lambda b, c: (b, 0, 0))
    vT_spec = pl.BlockSpec((None, nq, NSA_KV_WIDTH, Q_BLOCK), lambda b, c: (b, 0, 0, 0))
    const = lambda a: pl.BlockSpec(a.shape, lambda b, c: (0,) * a.ndim)
    return pl.pallas_call(
        _nsa_kernel,
        grid=(B, n_steps),
        in_specs=[pl.BlockSpec((Q_PER_STEP, NSA_WIDTH, Q_BLOCK), lambda b, c: (b * n_steps + c, 0, 0)),
                  pl.BlockSpec((None, n_cmp, NSA_KV_WIDTH), lambda b, c: (b, 0, 0)),
                  pl.BlockSpec((None, NSA_KV_WIDTH, n_cmp), lambda b, c: (b, 0, 0)),
                  k_spec, vT_spec, k_spec, vT_spec,
                  pl.BlockSpec((Q_PER_STEP, N_GATE_ROWS, Q_BLOCK), lambda b, c: (b * n_steps + c, 0, 0)),
                  const(seltab), const(wintab), const(cmptab), const(ovl), const(blkind)],
        out_specs=pl.BlockSpec((Q_PER_STEP * Q_BLOCK, NSA_WIDTH), lambda b, c: (b * n_steps + c, 0)),
        out_shape=jax.ShapeDtypeStruct((B * S, NSA_WIDTH), jnp.bfloat16),
        scratch_shapes=[pltpu.VMEM((NSA_KV_HEADS, n_sel, Q_BLOCK), jnp.float32),
                        pltpu.VMEM((NSA_KV_HEADS, n_sel, Q_BLOCK), jnp.float32),
                        pltpu.VMEM((2, NSA_KV_HEADS, KEY_SUPER, GQ), jnp.float32),
                        pltpu.VMEM((NSA_KV_HEADS, N_WIN_TILES * Q_BLOCK, GQ), jnp.float32)],
        compiler_params=_params(2),
        name="nsa",
    )(qT, kcmp, vcmpT, ksl3, vslT4, kwn3, vwnT4, smallT, seltab, wintab, cmptab, ovl, blkind)


CONV_TILE = 256
I_ROW = 24
C_ROWS = MLSTM_V_DIM + BF16_ROWS


def _log_sigmoid(x):
    return jnp.minimum(x, 0.0) - jnp.log(1.0 + jnp.exp(-jnp.abs(x)))


def _mlstm_kernel(mqk_ref, mvT_ref, moT_ref, gates_ref, gb_ref, tri_ref, last_ref, convw_ref, convb_ref,
                  norm_ref, out_ref, qk_ref, rows_ref):
    S = mqk_ref.shape[0]
    nt = S // LANES
    L = MLSTM_CHUNK
    f32 = jnp.float32
    bf16 = jnp.bfloat16
    kscale_row = jnp.where(lax.broadcasted_iota(jnp.int32, (1, 2 * MLSTM_QK_WIDTH), 1) < MLSTM_QK_WIDTH,
                           1.0, MLSTM_QK_DIM ** -0.5)

    def conv_body(i, _):
        t0 = pl.multiple_of(i * CONV_TILE, CONV_TILE)
        cur = mqk_ref[pl.ds(t0, CONV_TILE), :].astype(f32)
        prev_start = pl.multiple_of(jnp.maximum(t0 - BF16_ROWS, 0), BF16_ROWS)
        prev = (mqk_ref[pl.ds(prev_start, BF16_ROWS), :].astype(f32)[BF16_ROWS - SUBLANES:, :]
                * jnp.where(i > 0, 1.0, 0.0))
        ext = jnp.concatenate([prev, cur], axis=0)
        y = convb_ref[...]
        for j in range(CONV_WIDTH):
            lo = SUBLANES - (CONV_WIDTH - 1) + j
            y = y + convw_ref[j:j + 1, :] * ext[lo:lo + CONV_TILE, :]
        y = y * jax.nn.sigmoid(y) * kscale_row
        qk_ref[pl.ds(t0, CONV_TILE), :] = y.astype(bf16)
        return 0

    lax.fori_loop(0, S // CONV_TILE, conv_body, 0)

    H = MLSTM_HEADS
    G8 = 2 * H
    n_rows = nt * G8
    a3 = gates_ref[:, I_ROW:I_ROW + G8, :] + gb_ref[...][None]
    is_f = lax.broadcasted_iota(jnp.int32, a3.shape, 1) >= H
    x = jnp.where(is_f, _log_sigmoid(a3), a3).reshape(n_rows, LANES)
    bcum = jnp.dot(x, tri_ref[...], precision=lax.Precision.HIGHEST, preferred_element_type=f32)
    b_rows = pltpu.roll(bcum, n_rows - H, 0)
    g_rows = x - b_rows
    pos = lax.broadcasted_iota(jnp.int32, (n_rows, LANES), 1) & (L - 1)

    def chunk_cummax(a):
        shift = 1
        while shift < L:
            a = jnp.where(pos >= shift, jnp.maximum(a, pltpu.roll(a, shift, 1)), a)
            shift *= 2
        return a

    def chunk_last(a):
        return jnp.dot(a, last_ref[...], precision=lax.Precision.HIGHEST, preferred_element_type=f32)

    bl_rows = chunk_last(b_rows)
    rows_ref[0] = g_rows
    rows_ref[1] = chunk_cummax(g_rows)
    rows_ref[2] = b_rows
    rows_ref[3] = bl_rows
    rows_ref[4] = chunk_last(chunk_cummax(bl_rows + g_rows))

    s_io = lax.broadcasted_iota(jnp.int32, (LANES, LANES), 0)
    t_io = lax.broadcasted_iota(jnp.int32, (LANES, LANES), 1)
    causal = (s_io <= t_io) & ((s_io >= L) == (t_io >= L))
    lane = lax.broadcasted_iota(jnp.int32, (1, LANES), 1)
    in_chunk = [lane < L, lane >= L]
    QD, VD = MLSTM_QK_DIM, MLSTM_V_DIM
    ones_aug = jnp.ones((C_ROWS - VD, LANES), bf16)
    head_lanes = [(lax.broadcasted_iota(jnp.int32, (LANES, LANES), 1) >= QD) == bool(par) for par in range(2)]

    def swap_halves(row):
        return pltpu.roll(row, L, 1)

    def tile_body(i, state):
        t0 = pl.multiple_of(i * LANES, LANES)
        r0 = pl.multiple_of(i * G8, G8)
        qk = qk_ref[pl.ds(t0, LANES), :]
        qk32 = qk.astype(f32)
        pairsT = [qk32[:, j * LANES:(j + 1) * LANES].T for j in range(4)]
        g8, cm8, b8, bl8, wm8 = [rows_ref[j, pl.ds(r0, G8), :] for j in range(5)]
        new_state = []
        for hh in range(H):
            pair, par = hh // 2, hh % 2
            g_r, cm_r, b_r, bl_r, wm_r = [a[hh:hh + 1, :] for a in (g8, cm8, b8, bl8, wm8)]
            caug, m_in = state[hh]
            qT = pairsT[pair][par * QD:(par + 1) * QD, :]
            kT = pairsT[2 + pair][par * QD:(par + 1) * QD, :]
            kpair = qk[:, (2 + pair) * LANES:(3 + pair) * LANES]
            qmask = jnp.where(head_lanes[par], qk[:, pair * LANES:(pair + 1) * LANES], jnp.zeros((), bf16))
            vaug = jnp.concatenate([mvT_ref[i, hh * VD:(hh + 1) * VD, :], ones_aug], axis=0)

            m_mid = swap_halves(jnp.maximum(bl_r + m_in, wm_r))
            m_prev = jnp.where(in_chunk[0], m_in, m_mid)
            m_next = jnp.maximum(bl_r + m_prev, wm_r)
            m_intra = b_r + cm_r
            m_inter = b_r + m_prev
            m_t = jnp.maximum(m_inter, m_intra)
            e_intra = jnp.exp(m_intra - m_t)
            e_inter = jnp.exp(m_inter - m_t)
            decay = jnp.exp(bl_r + m_prev - m_next)
            inject = jnp.exp(wm_r - m_next)

            g_mat = jnp.broadcast_to(g_r, (LANES, LANES)).T
            w = jnp.exp(jnp.where(causal, g_mat - cm_r, NEG_INF))
            st = _nt(kpair, qmask) * (w * e_intra)
            y = _mm(vaug, st.astype(bf16))
            kw = kT * jnp.exp(bl_r + g_r - wm_r)
            cs = caug
            for p in range(LANES // L):
                qs = jnp.where(in_chunk[p], qT * e_inter, 0.0).astype(bf16)
                y = y + _mm(cs.astype(bf16), qs)
                u = _nt(vaug, jnp.where(in_chunk[p], kw, 0.0).astype(bf16))
                dec = decay if p == 0 else swap_halves(decay)
                inj = inject if p == 0 else swap_halves(inject)
                cs = dec[:, 0:QD] * cs + inj[:, 0:QD] * u
            m_out = jnp.where(in_chunk[1], m_next, swap_halves(m_next))
            new_state.append((cs, m_out))

            den = y[VD:VD + 1, :]
            hT = y[0:VD, :] * (1.0 / jnp.maximum(jnp.abs(den), jnp.exp(-m_t)))
            sl = slice(hh * VD, (hh + 1) * VD)
            hn = hT * lax.rsqrt(jnp.mean(hT * hT, axis=0, keepdims=True) + NORM_EPS) * norm_ref[sl, :]
            yT = jax.nn.sigmoid(moT_ref[i, sl, :].astype(f32)) * hn
            out_ref[pl.ds(t0, LANES), sl] = yT.T.astype(out_ref.dtype)
        return tuple(new_state)

    init = tuple((jnp.zeros((C_ROWS, QD), f32), jnp.zeros((1, LANES), f32)) for _ in range(H))
    lax.fori_loop(0, nt, tile_body, init, unroll=8)


def _mlstm(mqk, mvT, moT, smallT, conv_w, conv_b, gate_bias, mlstm_norm, B, S):
    nt = S // LANES
    gb = jnp.broadcast_to(gate_bias.astype(jnp.float32).reshape(2 * MLSTM_HEADS, 1), (2 * MLSTM_HEADS, LANES))
    norm_cols = jnp.broadcast_to(mlstm_norm.astype(jnp.float32).reshape(MLSTM_WIDTH, 1), (MLSTM_WIDTH, LANES))
    lane = np.arange(LANES)
    same_chunk = lane[:, None] // MLSTM_CHUNK == lane[None, :] // MLSTM_CHUNK
    tri = (same_chunk & (lane[:, None] <= lane[None, :])).astype(np.float32)
    last = (same_chunk & (lane[:, None] % MLSTM_CHUNK == MLSTM_CHUNK - 1)).astype(np.float32)
    seq = lambda w: pl.BlockSpec((S, w), lambda b: (b, 0))
    tiles = lambda rows: pl.BlockSpec((nt, rows, LANES), lambda b: (b, 0, 0))
    return pl.pallas_call(
        _mlstm_kernel,
        grid=(B,),
        in_specs=[seq(2 * MLSTM_QK_WIDTH), tiles(MLSTM_WIDTH), tiles(MLSTM_WIDTH), tiles(N_GATE_ROWS),
                  _full((2 * MLSTM_HEADS, LANES)), _full((LANES, LANES)), _full((LANES, LANES)),
                  _full((CONV_WIDTH, 2 * MLSTM_QK_WIDTH)), _full((1, 2 * MLSTM_QK_WIDTH)),
                  _full((MLSTM_WIDTH, LANES))],
        out_specs=seq(MLSTM_WIDTH),
        out_shape=jax.ShapeDtypeStruct((B * S, MLSTM_WIDTH), jnp.bfloat16),
        scratch_shapes=[pltpu.VMEM((S, 2 * MLSTM_QK_WIDTH), jnp.bfloat16),
                        pltpu.VMEM((5, nt * 2 * MLSTM_HEADS, LANES), jnp.float32)],
        compiler_params=_params(1),
        name="mlstm",
    )(mqk, mvT, moT, smallT, gb, jnp.asarray(tri), jnp.asarray(last), conv_w, conv_b.reshape(1, -1), norm_cols)


TM_X = 1024
X_HALVES = 2


def _mix_xattn_kernel(ynsa_ref, yml_ref, x_ref, wout_ref, gpost_ref, gpre_ref, wq_ref, mem_ref, gmem_ref, wkv_ref,
                      wo_ref, gpost2_ref, out_ref, k_ref, v_ref):
    bf16 = jnp.bfloat16

    @pl.when(pl.program_id(1) == 0)
    def _():
        mn = _rms(mem_ref[...], gmem_ref[...]).astype(bf16)
        k_ref[...] = _mm(mn, wkv_ref[:, :D_MODEL]).astype(bf16)
        v_ref[...] = _mm(mn, wkv_ref[:, D_MODEL:]).astype(bf16)

    halves = [slice(i * (TM_X // X_HALVES), (i + 1) * (TM_X // X_HALVES)) for i in range(X_HALVES)]
    y = [_mm(ynsa_ref[r, :], wout_ref[:NSA_WIDTH, :]) + _mm(yml_ref[r, :], wout_ref[NSA_WIDTH:, :]) for r in halves]
    x1 = [x_ref[r, :] + _rms(y[i], gpost_ref[...]) for i, r in enumerate(halves)]
    h2 = [_rms(x1[i], gpre_ref[...]).astype(bf16) for i in range(X_HALVES)]
    q = [(_mm(h2[i], wq_ref[...]) * (XATTN_HEAD_DIM ** -0.5)).astype(bf16) for i in range(X_HALVES)]
    outs = [[] for _ in range(X_HALVES)]
    for hh in range(XATTN_HEADS):
        sl = slice(hh * XATTN_HEAD_DIM, (hh + 1) * XATTN_HEAD_DIM)
        s = [_nt(q[i][:, sl], k_ref[:, sl]) for i in range(X_HALVES)]
        for i in range(X_HALVES):
            p = jnp.exp(s[i] - jnp.max(s[i], axis=1, keepdims=True))
            l = jnp.sum(p, axis=1, keepdims=True)
            outs[i].append((_mm(p.astype(bf16), v_ref[:, sl]) * (1.0 / l)).astype(bf16))
    y2 = [_mm(jnp.concatenate(outs[i], axis=1), wo_ref[...]) for i in range(X_HALVES)]
    for i, r in enumerate(halves):
        out_ref[r, :] = x1[i] + _rms(y2[i], gpost2_ref[...])


def _mix_xattn(ynsa, yml, x2d, w_out, g_post, g_pre, w_xq, mem, g_mem, w_xkv, w_xo, g_post2, B, S):
    nt = S // TM_X
    M = mem.shape[1]
    tok = lambda w: pl.BlockSpec((TM_X, w), lambda b, i: (b * nt + i, 0))
    mem_spec = pl.BlockSpec((None, M, D_MODEL), lambda b, i: (b, 0, 0))
    sq = _full((D_MODEL, D_MODEL))
    row = _full((1, D_MODEL))
    bf = lambda w: w.astype(jnp.bfloat16)
    return pl.pallas_call(
        _mix_xattn_kernel,
        grid=(B, nt),
        in_specs=[tok(NSA_WIDTH), tok(MLSTM_WIDTH), tok(D_MODEL), sq, row, row, sq, mem_spec, row,
                  _full((D_MODEL, 2 * D_MODEL), buffers=1), sq, row],
        out_specs=tok(D_MODEL),
        out_shape=jax.ShapeDtypeStruct((B * S, D_MODEL), jnp.float32),
        scratch_shapes=[pltpu.VMEM((M, D_MODEL), jnp.bfloat16), pltpu.VMEM((M, D_MODEL), jnp.bfloat16)],
        compiler_params=_params(2),
        name="mix_xattn",
    )(ynsa, yml, x2d, bf(w_out), g_post, g_pre, bf(w_xq), mem, g_mem, bf(w_xkv), bf(w_xo), g_post2)


TM_F = 1024
F_HALVES = 2


def _ffn_kernel(x_ref, gpre_ref, wgu_ref, wd_ref, gpost_ref, out_ref, acc_ref):
    bf16 = jnp.bfloat16
    n_chunks = D_FF // F_TILE
    half = TM_F // F_HALVES

    def pre(i):
        return _rms(x_ref[i * half:(i + 1) * half, :], gpre_ref[...]).astype(bf16)

    def chunk(i, h, j):
        rows = slice(i * half, (i + 1) * half)
        cols = slice(j * F_TILE, (j + 1) * F_TILE)
        g = _mm(h, wgu_ref[:, cols])
        u = _mm(h, wgu_ref[:, D_FF + j * F_TILE:D_FF + (j + 1) * F_TILE])
        act = (g * jax.nn.sigmoid(g) * u).astype(bf16)
        down = _mm(act, wd_ref[cols, :])
        if j == 0:
            acc_ref[rows, :] = down
        else:
            acc_ref[rows, :] += down

    def post(i):
        rows = slice(i * half, (i + 1) * half)
        out_ref[rows, :] = x_ref[rows, :] + _rms(acc_ref[rows, :], gpost_ref[...])

    h = pre(0)
    for i in range(F_HALVES):
        chunk(i, h, 0)
        if i > 0:
            post(i - 1)
        h_next = pre(i + 1) if i + 1 < F_HALVES else None
        for j in range(1, n_chunks):
            chunk(i, h, j)
        h = h_next
    post(F_HALVES - 1)


def _ffn(x2d, g_pre, w_gate_up, w_down, g_post):
    T = x2d.shape[0]
    tok = pl.BlockSpec((TM_F, D_MODEL), lambda i: (i, 0))
    row = _full((1, D_MODEL))
    return pl.pallas_call(
        _ffn_kernel,
        grid=(T // TM_F,),
        in_specs=[tok, row, _full((D_MODEL, 2 * D_FF), buffers=1), _full((D_FF, D_MODEL), buffers=1), row],
        out_specs=tok,
        out_shape=jax.ShapeDtypeStruct((T, D_MODEL), jnp.float32),
        scratch_shapes=[pltpu.VMEM((TM_F, D_MODEL), jnp.float32)],
        compiler_params=_params(1),
        name="ffn",
    )(x2d, g_pre, w_gate_up.astype(jnp.bfloat16), w_down.astype(jnp.bfloat16), g_post)


def _layer(x, mem, rel_bias, mix_norm_pre, w_in, cmp_pos_k, cmp_pos_v, cmp_w1_k, cmp_w2_k, cmp_w1_v,
           cmp_w2_v, conv_w, conv_b, mlstm_gate_bias, mlstm_norm, w_out, mix_norm_post, xattn_norm_pre,
           mem_norm, w_xq, w_xkv, w_xo, xattn_norm_post, ffn_norm_pre, w_gate_up, w_down, ffn_norm_post):
    B, S, _ = x.shape
    row = lambda g: g.reshape(1, -1).astype(jnp.float32)
    x2d = x.reshape(B * S, D_MODEL)
    w_tok, w_feat = _in_proj_weights(w_in)
    (kc, vc, ksl, kwn, mqk, qT, vslT, vwnT, mvT, moT, smallT) = _in_proj(
        x2d, row(mix_norm_pre), w_tok, w_feat)
    kcmp, vcmpT = _compress(kc, vc, cmp_pos_k, cmp_pos_v, cmp_w1_k, cmp_w2_k, cmp_w1_v, cmp_w2_v, B, S)
    tables = _bias_tables(rel_bias.astype(jnp.float32))
    ynsa = _nsa(qT, kcmp, vcmpT, ksl, vslT, kwn, vwnT, smallT, tables, B, S)
    yml = _mlstm(mqk, mvT, moT, smallT, conv_w, conv_b, mlstm_gate_bias, mlstm_norm, B, S)
    x2 = _mix_xattn(ynsa, yml, x2d, w_out, row(mix_norm_post), row(xattn_norm_pre), w_xq, mem, row(mem_norm),
                    w_xkv, w_xo, row(xattn_norm_post), B, S)
    x3 = _ffn(x2, row(ffn_norm_pre), w_gate_up, w_down, row(ffn_norm_post))
    return x3.reshape(B, S, D_MODEL)


def kernel(x, mem, rel_bias, mix_norm_pre, w_in, cmp_pos_k, cmp_pos_v, cmp_w1_k, cmp_w2_k, cmp_w1_v, cmp_w2_v,
           conv_w, conv_b, mlstm_gate_bias, mlstm_norm, w_out, mix_norm_post, xattn_norm_pre, mem_norm, w_xq,
           w_xkv, w_xo, xattn_norm_post, ffn_norm_pre, w_gate_up, w_down, ffn_norm_post):
    depth = w_in.shape[0]
    for l in range(depth):
        x = _layer(x, mem, rel_bias, mix_norm_pre[l], w_in[l], cmp_pos_k[l], cmp_pos_v[l], cmp_w1_k[l],
                   cmp_w2_k[l], cmp_w1_v[l], cmp_w2_v[l], conv_w[l], conv_b[l], mlstm_gate_bias[l],
                   mlstm_norm[l], w_out[l], mix_norm_post[l], xattn_norm_pre[l], mem_norm[l], w_xq[l],
                   w_xkv[l], w_xo[l], xattn_norm_post[l], ffn_norm_pre[l], w_gate_up[l], w_down[l],
                   ffn_norm_post[l])
    return x
```

```python
import functools
import math

import numpy as np
import jax
import jax.numpy as jnp
from jax import lax
from jax.experimental import pallas as pl
from jax.experimental.pallas import tpu as pltpu

D_MODEL = 1024
NSA_WIDTH = 512
NSA_HEAD_DIM = 64
NSA_HEADS = 8
NSA_KV_HEADS = 2
NSA_GROUP = 4
NSA_KV_WIDTH = 128
CMP_STRIDE = 16
CMP_BLOCK = 32
CMP_HIDDEN = 256
SEL_BLOCK = 64
N_SELECT = 16
WINDOW = 512
Q_BLOCK = 128
FORCED_SCORE = 1.0e4
MLSTM_WIDTH = 512
MLSTM_HEADS = 4
MLSTM_V_DIM = 128
MLSTM_QK_DIM = 64
MLSTM_QK_WIDTH = 256
MLSTM_CHUNK = 64
CONV_WIDTH = 4
REL_BUCKETS = 32
REL_MAX_DISTANCE = 128
XATTN_HEADS = 4
XATTN_HEAD_DIM = 256
D_FF = 2816
NORM_EPS = 1e-6
NEG_INF = -1.0e30
LOG2E = math.log2(math.e)

IN_SIZES = (NSA_WIDTH,) + (NSA_KV_WIDTH,) * 6 + (NSA_HEADS * 3, MLSTM_QK_WIDTH, MLSTM_QK_WIDTH,
                                                 MLSTM_WIDTH, MLSTM_HEADS, MLSTM_HEADS, MLSTM_WIDTH)
IN_OFFSETS = tuple(int(o) for o in np.cumsum((0,) + IN_SIZES)[:-1])

LANES = 128
SUBLANES = 8
BF16_ROWS = 16
VMEM_LIMIT_BYTES = 56 * 1024 * 1024

N_GATE_ROWS = 32
F_TILE = 256


def _rms(x, gain):
    return x * lax.rsqrt(jnp.mean(x * x, axis=-1, keepdims=True) + NORM_EPS) * gain


def _nt(a, b):
    return lax.dot_general(a, b, (((1,), (1,)), ((), ())), preferred_element_type=jnp.float32)


def _mm(a, b):
    return jnp.dot(a, b, preferred_element_type=jnp.float32)


def _params(n_axes, flags=None):
    return pltpu.CompilerParams(dimension_semantics=("arbitrary",) * n_axes,
                                vmem_limit_bytes=VMEM_LIMIT_BYTES, flags=flags)


def _full(shape, buffers=None):
    nd = len(shape)
    mode = None if buffers is None else pl.Buffered(buffers)
    return pl.BlockSpec(shape, lambda *_: (0,) * nd, pipeline_mode=mode)


TM_IN = 1024
IN_HALVES = 2
TOK_DOT_WIDTH = 512
_TOK_GROUPS = (("kc", 128, jnp.float32), ("vc", 128, jnp.float32), ("ksl", 128, jnp.bfloat16),
               ("kwn", 128, jnp.bfloat16), ("mqk", 512, jnp.bfloat16))
_FEAT_GROUPS = (("qT", 512, jnp.bfloat16), ("vslT", 128, jnp.bfloat16), ("vwnT", 128, jnp.bfloat16),
                ("mvT", 512, jnp.bfloat16), ("moT", 512, jnp.bfloat16), ("smallT", N_GATE_ROWS, jnp.float32))


def _in_proj_kernel(x_ref, g_ref, wtok_ref, wfeat_ref, *out_refs):
    n_tok = len(_TOK_GROUPS)
    half = TM_IN // IN_HALVES

    def norm(i):
        return _rms(x_ref[i * half:(i + 1) * half, :], g_ref[...]).astype(jnp.bfloat16)

    def token_major(i, h):
        n_cols = sum(width for _, width, _ in _TOK_GROUPS)
        res = [_mm(h, wtok_ref[:, c0:c0 + TOK_DOT_WIDTH]) for c0 in range(0, n_cols, TOK_DOT_WIDTH)]
        off = 0
        for (name, width, dt), o_ref in zip(_TOK_GROUPS, out_refs[:n_tok]):
            r = res[off // TOK_DOT_WIDTH]
            lo = off % TOK_DOT_WIDTH
            o_ref[i * half:(i + 1) * half, :] = r[:, lo:lo + width].astype(dt)
            off += width

    def feature_major(i, h):
        off = 0
        for (name, rows, dt), o_ref in zip(_FEAT_GROUPS, out_refs[n_tok:]):
            r = _nt(wfeat_ref[off:off + rows, :], h)
            if name == "qT":
                r = r * (NSA_HEAD_DIM ** -0.5 * LOG2E)
            for j in range(half // LANES):
                o_ref[i * (half // LANES) + j] = r[:, j * LANES:(j + 1) * LANES].astype(dt)
            off += rows

    h = norm(0)
    for i in range(IN_HALVES):
        token_major(i, h)
        h_next = norm(i + 1) if i + 1 < IN_HALVES else None
        feature_major(i, h)
        h = h_next


def _in_proj(x2d, gain, w_tok, w_feat):
    T = x2d.shape[0]
    n_tok_cols = w_tok.shape[1]
    n_feat_rows = w_feat.shape[0]
    out_shape, out_specs = [], []
    for name, width, dt in _TOK_GROUPS:
        out_shape.append(jax.ShapeDtypeStruct((T, width), dt))
        out_specs.append(pl.BlockSpec((TM_IN, width), lambda i: (i, 0)))
    for name, rows, dt in _FEAT_GROUPS:
        out_shape.append(jax.ShapeDtypeStruct((T // LANES, rows, LANES), dt))
        out_specs.append(pl.BlockSpec((TM_IN // LANES, rows, LANES), lambda i: (i, 0, 0)))
    return pl.pallas_call(
        _in_proj_kernel,
        grid=(T // TM_IN,),
        in_specs=[pl.BlockSpec((TM_IN, D_MODEL), lambda i: (i, 0)),
                  _full((1, D_MODEL)),
                  _full((D_MODEL, n_tok_cols)),
                  _full((n_feat_rows, D_MODEL))],
        out_specs=out_specs,
        out_shape=out_shape,
        compiler_params=_params(1),
        name="in_proj",
    )(x2d, gain, w_tok, w_feat)


def _in_proj_weights(w_in):
    (nq, kc, vc, ksl, vsl, kwn, vwn, gt, mq, mk, mv, mi, mf, mo) = [
        w_in[:, o:o + s] for o, s in zip(IN_OFFSETS, IN_SIZES)]
    gt_r = gt.reshape(D_MODEL, NSA_KV_HEADS, NSA_GROUP, 3).transpose(0, 3, 1, 2).reshape(D_MODEL, 24)
    small = jnp.concatenate([gt_r, mi, mf], axis=1)
    w_tok = jnp.concatenate([kc, vc, ksl, kwn, mq, mk], axis=1)
    w_feat = jnp.concatenate([nq, vsl, vwn, mv, mo, small], axis=1).T
    return w_tok.astype(jnp.bfloat16), w_feat.astype(jnp.bfloat16)


N_CHUNK_COLS = CMP_STRIDE * NSA_KV_WIDTH
N_HID2 = NSA_KV_HEADS * CMP_HIDDEN


def _compress_one(tok_ref, pos_ref, w1_ref, n_chunks):
    bf16 = jnp.bfloat16
    zero = jnp.zeros((NSA_HEAD_DIM, CMP_HIDDEN), bf16)
    rows = [tok_ref[pl.ds(t, n_chunks, stride=CMP_STRIDE), :] for t in range(CMP_STRIDE)]
    halves = []
    for a in range(2):
        acc = None
        for t0 in range(0, CMP_STRIDE, 2):
            xs, ws = [], []
            for t in (t0, t0 + 1):
                xs.append((rows[t] + pos_ref[a:a + 1, t * NSA_KV_WIDTH:(t + 1) * NSA_KV_WIDTH]).astype(bf16))
                w = w1_ref[a, t]
                ws.append(jnp.concatenate([jnp.concatenate([w, zero], axis=1),
                                           jnp.concatenate([zero, w], axis=1)], axis=0))
            d = _mm(jnp.concatenate(xs, axis=1), jnp.concatenate(ws, axis=0))
            acc = d if acc is None else acc + d
        halves.append(acc)
    pre = halves[0] + pltpu.roll(halves[1], n_chunks - 1, 0)
    return (pre * jax.nn.sigmoid(pre)).astype(bf16)


def _compress_kernel(kc_ref, vc_ref, posk_ref, posv_ref, w1k_ref, w1v_ref, w2k_ref, w2vT_ref,
                     kcmp_ref, vcmpT_ref):
    n_chunks = kc_ref.shape[0] // CMP_STRIDE
    hid_k = _compress_one(kc_ref, posk_ref, w1k_ref, n_chunks)
    kcmp = _mm(hid_k, w2k_ref[...])
    row = lax.broadcasted_iota(jnp.int32, kcmp.shape, 0)
    kcmp_ref[...] = jnp.where(row < n_chunks - 1, kcmp, 0.0).astype(kcmp_ref.dtype)
    hid_v = _compress_one(vc_ref, posv_ref, w1v_ref, n_chunks)
    vcmpT = _nt(w2vT_ref[...], hid_v)
    col = lax.broadcasted_iota(jnp.int32, vcmpT.shape, 1)
    vcmpT_ref[...] = jnp.where(col < n_chunks - 1, vcmpT, 0.0).astype(vcmpT_ref.dtype)


def _compress_weights(pos, w1, w2):
    eye = jnp.eye(NSA_KV_HEADS, dtype=w1.dtype)
    w1r = w1.reshape(2, CMP_STRIDE, NSA_HEAD_DIM, CMP_HIDDEN)
    pos_e = jnp.broadcast_to(pos.reshape(2, CMP_STRIDE, 1, NSA_HEAD_DIM),
                             (2, CMP_STRIDE, NSA_KV_HEADS, NSA_HEAD_DIM)).reshape(2, N_CHUNK_COLS)
    w2e = jnp.einsum('jd,hg->hjgd', w2, eye).reshape(N_HID2, NSA_KV_WIDTH)
    return pos_e, w1r.astype(jnp.bfloat16), w2e.astype(jnp.bfloat16)


def _compress(kc, vc, cmp_pos_k, cmp_pos_v, cmp_w1_k, cmp_w2_k, cmp_w1_v, cmp_w2_v, B, S):
    n_chunks = S // CMP_STRIDE
    posk, w1k, w2k = _compress_weights(cmp_pos_k, cmp_w1_k, cmp_w2_k)
    posv, w1v, w2v = _compress_weights(cmp_pos_v, cmp_w1_v, cmp_w2_v)
    chunk_spec = pl.BlockSpec((S, NSA_KV_WIDTH), lambda b: (b, 0))
    return pl.pallas_call(
        _compress_kernel,
        grid=(B,),
        in_specs=[chunk_spec, chunk_spec,
                  _full((2, N_CHUNK_COLS)), _full((2, N_CHUNK_COLS)),
                  _full((2, CMP_STRIDE, NSA_HEAD_DIM, CMP_HIDDEN)), _full((2, CMP_STRIDE, NSA_HEAD_DIM, CMP_HIDDEN)),
                  _full((N_HID2, NSA_KV_WIDTH)), _full((NSA_KV_WIDTH, N_HID2))],
        out_specs=[pl.BlockSpec((None, n_chunks, NSA_KV_WIDTH), lambda b: (b, 0, 0)),
                   pl.BlockSpec((None, NSA_KV_WIDTH, n_chunks), lambda b: (b, 0, 0))],
        out_shape=[jax.ShapeDtypeStruct((B, n_chunks, NSA_KV_WIDTH), jnp.bfloat16),
                   jax.ShapeDtypeStruct((B, NSA_KV_WIDTH, n_chunks), jnp.bfloat16)],
        compiler_params=_params(1),
        name="compress",
    )(kc, vc, posk, posv, w1k, w1v, w2k, w2v.T)


GQ = NSA_GROUP * Q_BLOCK
TINY = 1e-30
CMP_TAB_ROWS = 512
CMP_TAB_ZERO = 248
CMP_TAB_LOOKUP = (232, 256)
SEL_STEP_SHIFT = 2
KEY_SUPER = Q_BLOCK << SEL_STEP_SHIFT
SEL_TAB_ZERO = KEY_SUPER + Q_BLOCK
SEL_TAB_ROWS = SEL_TAB_ZERO + KEY_SUPER
N_WIN_TILES = WINDOW // Q_BLOCK + 1
Q_PER_STEP = 4
SEL_SUB_CHUNKS = 2
V_ROWS = NSA_HEAD_DIM + BF16_ROWS


def _bucket_np(dist):
    n = np.maximum(dist, 0)
    max_exact = REL_BUCKETS // 2
    nf = np.maximum(n, 1).astype(np.float64)
    large = max_exact + (np.log(nf / max_exact) / math.log(REL_MAX_DISTANCE / max_exact)
                         * (REL_BUCKETS - max_exact)).astype(np.int64)
    large = np.minimum(large, REL_BUCKETS - 1)
    return np.where(n < max_exact, n, large).astype(np.int32)


def _bias_index_tables():
    m = np.arange(Q_BLOCK)[:, None]
    r = np.arange(Q_BLOCK)[None, :]
    diag = np.where(r - m >= 0, _bucket_np(r - m), -1).astype(np.int32)
    off = _bucket_np(Q_BLOCK + r - m)
    jp = np.arange(*CMP_TAB_LOOKUP)[:, None] - CMP_TAB_ZERO
    d_c = r - CMP_STRIDE * jp - (CMP_BLOCK - 1)
    cmp_idx = np.where(d_c >= 0, _bucket_np(d_c), -1).astype(np.int32)
    return diag, off, cmp_idx


def _bias_tables_kernel(rb_ref, diag_idx_ref, off_idx_ref, cmp_idx_ref, sel_ref, win_ref, cmp_ref):
    f32 = jnp.float32

    def lookup(idx, head):
        far = rb_ref[head, REL_BUCKETS - 1]
        acc = jnp.full(idx.shape, NEG_INF, f32)
        for k in range(REL_BUCKETS):
            acc = jnp.where(idx == k, (rb_ref[head, k] - far) * LOG2E, acc)
        return acc

    m_io = lax.broadcasted_iota(jnp.int32, (Q_BLOCK, Q_BLOCK), 0)
    r_io = lax.broadcasted_iota(jnp.int32, (Q_BLOCK, Q_BLOCK), 1)
    neg_tile = jnp.full((Q_BLOCK, Q_BLOCK), NEG_INF, f32)
    lo, hi = CMP_TAB_LOOKUP
    for h in range(NSA_KV_HEADS):
        for g in range(NSA_GROUP):
            head = h * NSA_GROUP + g
            sl = slice(g * Q_BLOCK, (g + 1) * Q_BLOCK)
            far = 0.0
            far_tile = jnp.full((Q_BLOCK, Q_BLOCK), far, f32)
            diag_v = lookup(diag_idx_ref[...], head)
            off_v = lookup(off_idx_ref[...], head)
            n_far = (SEL_TAB_ZERO - Q_BLOCK) // Q_BLOCK
            for t in range(SEL_TAB_ROWS // Q_BLOCK):
                rows = slice(t * Q_BLOCK, (t + 1) * Q_BLOCK)
                tile = far_tile if t < n_far else off_v if t == n_far else diag_v if t == n_far + 1 else neg_tile
                sel_ref[h, rows, sl] = tile
            win_ref[h, 0, :, sl] = diag_v
            win_ref[h, 1, :, sl] = off_v
            for back in range(2, N_WIN_TILES - 1):
                win_ref[h, back, :, sl] = far_tile
            win_ref[h, N_WIN_TILES - 1, :, sl] = jnp.where(r_io < m_io, far, NEG_INF)
            win_ref[h, N_WIN_TILES, :, sl] = neg_tile
            cmp_ref[h, 0:lo, sl] = jnp.full((lo, Q_BLOCK), far, f32)
            cmp_ref[h, lo:hi, sl] = lookup(cmp_idx_ref[...], head)
            cmp_ref[h, hi:CMP_TAB_ROWS, sl] = jnp.full((CMP_TAB_ROWS - hi, Q_BLOCK), NEG_INF, f32)


def _bias_tables(rel_bias):
    diag_idx, off_idx, cmp_idx = _bias_index_tables()
    shapes = [(NSA_KV_HEADS, SEL_TAB_ROWS, GQ), (NSA_KV_HEADS, N_WIN_TILES + 1, Q_BLOCK, GQ),
              (NSA_KV_HEADS, CMP_TAB_ROWS, GQ)]
    return pl.pallas_call(
        _bias_tables_kernel,
        in_specs=[pl.BlockSpec(memory_space=pltpu.SMEM),
                  _full(diag_idx.shape), _full(off_idx.shape), _full(cmp_idx.shape)],
        out_specs=[_full(s) for s in shapes],
        out_shape=[jax.ShapeDtypeStruct(s, jnp.float32) for s in shapes],
        grid=(1,),
        compiler_params=_params(1),
        name="bias_tables",
    )(rel_bias, jnp.asarray(diag_idx), jnp.asarray(off_idx), jnp.asarray(cmp_idx))


def _overlap_np(n_cmp_rows, n_sel):
    cmp_start = np.arange(n_cmp_rows) * CMP_STRIDE
    cmp_end = cmp_start + CMP_BLOCK - 1
    sel_start = np.arange(n_sel) * SEL_BLOCK
    ov = ((cmp_start[None, :] <= sel_start[:, None] + SEL_BLOCK - 1)
          & (cmp_end[None, :] >= sel_start[:, None])).astype(np.float32)
    ov[:, n_cmp_rows - 1] = 0.0
    return ov


def _tile4(a):
    return jnp.concatenate([a] * NSA_GROUP, axis=1)


def _select_blocks(score, score_ref, n_top, hooks=()):
    n_sel = score.shape[0]
    score_ref[...] = score
    n_grp = n_sel // SUBLANES
    grp = [score[SUBLANES * v:SUBLANES * (v + 1), :] for v in range(n_grp)]
    cnt = [jnp.zeros((SUBLANES, Q_BLOCK), jnp.int32) for _ in range(n_grp)]
    sub_io = lax.broadcasted_iota(jnp.int32, (SUBLANES, Q_BLOCK), 0)
    hook_at = {(i * n_sel) // len(hooks): hk for i, hk in enumerate(hooks)} if hooks else {}
    for jp in range(n_sel):
        if jp in hook_at:
            hook_at[jp]()
        row = score_ref[jp:jp + 1, :]
        for v in range(n_grp):
            if SUBLANES * v > jp:
                inc = (row >= grp[v]).astype(jnp.int32)
            elif SUBLANES * (v + 1) - 1 < jp:
                inc = (row > grp[v]).astype(jnp.int32)
            else:
                tie = (sub_io > jp - SUBLANES * v).astype(jnp.int32)
                inc = jnp.where(row > grp[v], 1, jnp.where(row == grp[v], tie, 0))
            cnt[v] = cnt[v] + inc
    return [jnp.where(cnt[v] < n_top, 0.0, NEG_INF) for v in range(n_grp)]


def _nsa_kernel(*refs):
    step = pl.program_id(1)
    tail = []
    for blk in range(Q_PER_STEP):
        tail = _nsa_block(step * Q_PER_STEP + blk, blk, tail, *refs)
    for part in tail:
        part()


def _nsa_block(c, blk, deferred, q_ref, kcmp_ref, vcmpT_ref, ksl_ref, vslT_ref, kwn_ref, vwnT_ref, gate_ref,
               seltab_ref, wintab_ref, cmptab_ref, ovl_ref, blkind_ref, out_ref, score_ref, selb_ref, sbufs_ref,
               swin_ref):
    sbuf_ref, sbuf2_ref = sbufs_ref.at[0], sbufs_ref.at[1]
    n_cmp = kcmp_ref.shape[0]
    n_sel = ovl_ref.shape[0]
    n_top = min(N_SELECT, n_sel)
    f32 = jnp.float32
    bf16 = jnp.bfloat16
    DH = NSA_HEAD_DIM
    heads = range(NSA_KV_HEADS)

    q = q_ref[blk]
    zq = jnp.zeros((DH, GQ), bf16)
    qcat, qpad = [], []
    for h in heads:
        qcat.append(jnp.concatenate([q[(h * NSA_GROUP + g) * DH:(h * NSA_GROUP + g + 1) * DH, :]
                                     for g in range(NSA_GROUP)], axis=1))
        qpad.append(jnp.concatenate([qcat[h], zq] if h == 0 else [zq, qcat[h]], axis=0))

    backs = list(range(N_WIN_TILES))
    kts = [jnp.maximum(c - back, 0) for back in backs]
    slots = [jnp.where(c >= back, back, N_WIN_TILES) for back in backs]
    m_win = [jnp.full((1, GQ), NEG_INF, f32) for _ in heads]

    def win_score(back, h):
        key0 = pl.multiple_of(kts[back] * Q_BLOCK, Q_BLOCK)
        s = _mm(kwn_ref[pl.ds(key0, Q_BLOCK), :], qpad[h]) + wintab_ref[h, slots[back]]
        swin_ref[h, back * Q_BLOCK:(back + 1) * Q_BLOCK, :] = s
        return jnp.max(s, axis=0, keepdims=True)

    cmp_off = pl.multiple_of(CMP_TAB_ZERO - (Q_BLOCK // CMP_STRIDE) * c, SUBLANES)
    kcmp = kcmp_ref[...]
    j_io = lax.broadcasted_iota(jnp.int32, (n_sel, Q_BLOCK), 0)
    r_io = lax.broadcasted_iota(jnp.int32, (n_sel, Q_BLOCK), 1)
    cur = (Q_BLOCK // SEL_BLOCK) * c + (r_io >= SEL_BLOCK).astype(jnp.int32)
    forced = (j_io == 0) | (j_io == cur) | (j_io == cur - 1)
    visible = j_io <= cur
    o_c = []
    for h in heads:
        tab = cmptab_ref[h, pl.ds(cmp_off, n_cmp), :]
        s = _mm(kcmp, qpad[h]) + tab
        m = jnp.maximum(jnp.max(s, axis=0, keepdims=True), 0.1 * NEG_INF)
        p = jnp.exp2(s - m)
        l = jnp.sum(p, axis=0, keepdims=True)
        pn = p * (1.0 / jnp.maximum(l, TINY))
        o_c.append(_mm(vcmpT_ref[h * DH:(h + 1) * DH, :], pn.astype(bf16)))
        psum = pn[:, 0:Q_BLOCK]
        for g in range(1, NSA_GROUP):
            psum = psum + pn[:, g * Q_BLOCK:(g + 1) * Q_BLOCK]
        imp = jnp.dot(ovl_ref[...], psum, precision=lax.Precision.HIGHEST,
                      preferred_element_type=f32)
        score = jnp.where(forced, FORCED_SCORE, jnp.where(visible, imp, -1.0))

        def hook(back, h=h):
            m_win[h] = jnp.maximum(m_win[h], win_score(back, h))

        hooks = [functools.partial(hook, back) for back in backs]
        if h == 0:
            for i, part in enumerate(deferred):
                hooks.insert(2 * i + 1, part)
        rows = _select_blocks(score, score_ref.at[h], n_top, hooks=hooks)
        for v, mask_rows in enumerate(rows):
            selb_ref[h, SUBLANES * v:SUBLANES * (v + 1), :] = mask_rows

    def ones_rows(n_keys):
        return jnp.ones((V_ROWS - DH, n_keys), bf16)

    acc_w = [jnp.zeros((V_ROWS, GQ), f32) for _ in heads]

    def win_value(back):
        for h in heads:
            p = jnp.exp2(swin_ref[h, back * Q_BLOCK:(back + 1) * Q_BLOCK, :] - m_win[h]).astype(bf16)
            vT = jnp.concatenate([vwnT_ref[kts[back], h * DH:(h + 1) * DH, :], ones_rows(Q_BLOCK)], axis=0)
            acc_w[h] = acc_w[h] + _mm(vT, p)

    blocks_per_step = KEY_SUPER // SEL_BLOCK
    tiles_per_step = KEY_SUPER // Q_BLOCK

    sub = KEY_SUPER // SEL_SUB_CHUNKS
    blocks_per_sub = sub // SEL_BLOCK
    tiles_per_sub = sub // Q_BLOCK

    own_lanes = [(lax.broadcasted_iota(jnp.int32, (sub, NSA_KV_WIDTH), 1) >= DH) == bool(h) for h in heads]

    def q_with_mask_rows(j, h):
        blk0 = pl.multiple_of(j * blocks_per_step, blocks_per_step)
        rows = jnp.concatenate([_tile4(selb_ref[h, pl.ds(blk0, blocks_per_step), :]),
                                jnp.zeros((DH - blocks_per_step, GQ), f32)], axis=0).astype(bf16)
        return jnp.concatenate([qcat[h], rows] if h == 0 else [rows, qcat[h]], axis=0)

    def score_chunk(j, ci, h, q_aug, buf_ref, near=True):
        key0 = pl.multiple_of(j * KEY_SUPER + ci * sub, sub)
        k_aug = jnp.where(own_lanes[h], ksl_ref[pl.ds(key0, sub), :], blkind_ref[h, ci * sub:(ci + 1) * sub, :])
        s = _mm(k_aug, q_aug)
        if near:
            tab_off = pl.multiple_of(
                jnp.maximum(j * KEY_SUPER - c * Q_BLOCK + SEL_TAB_ZERO, 0) + ci * sub, Q_BLOCK)
            s = s + seltab_ref[h, pl.ds(tab_off, sub), :]
        buf_ref[h, ci * sub:(ci + 1) * sub, :] = s
        return jnp.max(s, axis=0, keepdims=True)

    def value_chunk(j, ci, h, m_h, buf_ref):
        p = jnp.exp2(buf_ref[h, ci * sub:(ci + 1) * sub, :] - m_h).astype(bf16)
        vT = jnp.concatenate([vslT_ref[j * tiles_per_step + ci * tiles_per_sub + i, h * DH:(h + 1) * DH, :]
                              for i in range(tiles_per_sub)], axis=1)
        return _mm(jnp.concatenate([vT, ones_rows(sub)], axis=0), p)

    def values(j, m_old, m_cur, acc, src_ref, before_chunk=None):
        acc = [jnp.exp2(m_old[h] - m_cur[h]) * acc[h] for h in heads]
        for ci in range(SEL_SUB_CHUNKS):
            if before_chunk is not None:
                before_chunk(ci)
            for h in heads:
                acc[h] = acc[h] + value_chunk(j, ci, h, m_cur[h], src_ref)
        return tuple(acc)

    def sel_step(j, carry, src_ref, dst_ref, near=True):
        m_old, m_cur, acc = carry
        m_run = list(m_cur)
        q_aug = [q_with_mask_rows(j + 1, h) for h in heads]

        def scores(ci):
            for h in heads:
                m_run[h] = jnp.maximum(m_run[h], score_chunk(j + 1, ci, h, q_aug[h], dst_ref, near))

        acc = values(j, m_old, m_cur, acc, src_ref, before_chunk=scores)
        return m_cur, tuple(m_run), acc

    def sel_pair(i, carry, near):
        carry = sel_step(2 * i, carry, sbuf_ref, sbuf2_ref, near)
        return sel_step(2 * i + 1, carry, sbuf2_ref, sbuf_ref, near)

    n_steps = lax.shift_right_logical(c, SEL_STEP_SHIFT) + 1
    m_init = tuple(jnp.full((1, GQ), NEG_INF, f32) for _ in heads)
    acc_init = tuple(jnp.zeros((V_ROWS, GQ), f32) for _ in heads)
    m_first = list(m_init)
    win_order = list(backs)
    q_aug0 = [q_with_mask_rows(0, h) for h in heads]
    for ci in range(SEL_SUB_CHUNKS):
        for _ in range(-(-N_WIN_TILES // SEL_SUB_CHUNKS)):
            if win_order:
                win_value(win_order.pop(0))
        for h in heads:
            m_first[h] = jnp.maximum(m_first[h], score_chunk(0, ci, h, q_aug0[h], sbuf_ref))
    while win_order:
        win_value(win_order.pop(0))
    o_w = [a[0:DH, :] * (1.0 / a[DH:DH + 1, :]) for a in acc_w]
    n_piped = n_steps - 1
    odd = n_piped & 1
    n_far = lax.shift_right_logical(jnp.maximum(c - 1, 0), SEL_STEP_SHIFT)
    far_pairs = lax.shift_right_logical(jnp.maximum(n_far - 1, 0), 1)
    carry = lax.fori_loop(0, far_pairs, functools.partial(sel_pair, near=False),
                          (m_init, tuple(m_first), acc_init))
    carry = lax.fori_loop(far_pairs, lax.shift_right_logical(n_piped, 1), functools.partial(sel_pair, near=True),
                          carry)
    carry = lax.fori_loop(0, odd, lambda _, cr: sel_step(n_piped - 1, cr, sbuf_ref, sbuf2_ref), carry)
    m_old, m_cur, acc = carry
    acc = [jnp.exp2(m_old[h] - m_cur[h]) * acc[h] for h in heads]
    last_ref = sbufs_ref.at[odd]

    def last_values(ci):
        for h in heads:
            acc[h] = acc[h] + value_chunk(n_piped, ci, h, m_cur[h], last_ref)

    def combine():
        o_s = [a[0:DH, :] * (1.0 / jnp.maximum(a[DH:DH + 1, :], TINY)) for a in acc]
        for h in heads:
            ys = []
            for g in range(NSA_GROUP):
                sl = slice(g * Q_BLOCK, (g + 1) * Q_BLOCK)
                row0 = h * NSA_GROUP + g
                gates = [jax.nn.sigmoid(gate_ref[blk, kind * NSA_HEADS + row0:kind * NSA_HEADS + row0 + 1, :])
                         for kind in range(3)]
                ys.append(gates[0] * o_c[h][:, sl] + gates[1] * o_s[h][:, sl] + gates[2] * o_w[h][:, sl])
            yT = jnp.concatenate(ys, axis=0)
            for half in range(2):
                col = (2 * h + half) * LANES
                out_ref[blk * Q_BLOCK:(blk + 1) * Q_BLOCK, col:col + LANES] = (
                    yT[half * LANES:(half + 1) * LANES, :].T.astype(out_ref.dtype))

    return [functools.partial(last_values, ci) for ci in range(SEL_SUB_CHUNKS)] + [combine]


def _nsa(qT, kcmp, vcmpT, ksl, vslT, kwn, vwnT, smallT, tables, B, S):
    assert S % KEY_SUPER == 0
    nq = S // Q_BLOCK
    n_steps = nq // Q_PER_STEP
    n_cmp = S // CMP_STRIDE
    n_sel = S // SEL_BLOCK
    seltab, wintab, cmptab = tables
    ovl = jnp.asarray(_overlap_np(n_cmp, n_sel))
    key_blk = np.arange(KEY_SUPER)[:, None] // SEL_BLOCK
    lane = np.arange(NSA_KV_WIDTH)[None, :]
    blkind = jnp.asarray(np.stack([lane - NSA_HEAD_DIM == key_blk, lane == key_blk]), jnp.bfloat16)
    ksl3 = ksl.reshape(B, S, NSA_KV_WIDTH)
    kwn3 = kwn.reshape(B, S, NSA_KV_WIDTH)
    vslT4 = vslT.reshape(B, nq, NSA_KV_WIDTH, Q_BLOCK)
    vwnT4 = vwnT.reshape(B, nq, NSA_KV_WIDTH, Q_BLOCK)
    k_spec = pl.BlockSpec((None, S, NSA_KV_WIDTH), lambda b, c: (b, 0, 0))
    vT_spec = pl.BlockSpec((None, nq, NSA_KV_WIDTH, Q_BLOCK), lambda b, c: (b, 0, 0, 0))
    const = lambda a: pl.BlockSpec(a.shape, lambda b, c: (0,) * a.ndim)
    return pl.pallas_call(
        _nsa_kernel,
        grid=(B, n_steps),
        in_specs=[pl.BlockSpec((Q_PER_STEP, NSA_WIDTH, Q_BLOCK), lambda b, c: (b * n_steps + c, 0, 0)),
                  pl.BlockSpec((None, n_cmp, NSA_KV_WIDTH), lambda b, c: (b, 0, 0)),
                  pl.BlockSpec((None, NSA_KV_WIDTH, n_cmp), lambda b, c: (b, 0, 0)),
                  k_spec, vT_spec, k_spec, vT_spec,
                  pl.BlockSpec((Q_PER_STEP, N_GATE_ROWS, Q_BLOCK), lambda b, c: (b * n_steps + c, 0, 0)),
                  const(seltab), const(wintab), const(cmptab), const(ovl), const(blkind)],
        out_specs=pl.BlockSpec((Q_PER_STEP * Q_BLOCK, NSA_WIDTH), lambda b, c: (b * n_steps + c, 0)),
        out_shape=jax.ShapeDtypeStruct((B * S, NSA_WIDTH), jnp.bfloat16),
        scratch_shapes=[pltpu.VMEM((NSA_KV_HEADS, n_sel, Q_BLOCK), jnp.float32),
                        pltpu.VMEM((NSA_KV_HEADS, n_sel, Q_BLOCK), jnp.float32),
                        pltpu.VMEM((2, NSA_KV_HEADS, KEY_SUPER, GQ), jnp.float32),
                        pltpu.VMEM((NSA_KV_HEADS, N_WIN_TILES * Q_BLOCK, GQ), jnp.float32)],
        compiler_params=_params(2),
        name="nsa",
    )(qT, kcmp, vcmpT, ksl3, vslT4, kwn3, vwnT4, smallT, seltab, wintab, cmptab, ovl, blkind)


CONV_TILE = 256
I_ROW = 24
C_ROWS = MLSTM_V_DIM + BF16_ROWS


def _log_sigmoid(x):
    return jnp.minimum(x, 0.0) - jnp.log(1.0 + jnp.exp(-jnp.abs(x)))


def _mlstm_kernel(mqk_ref, mvT_ref, moT_ref, gates_ref, gb_ref, tri_ref, last_ref, convw_ref, convb_ref,
                  norm_ref, out_ref, qk_ref, rows_ref):
    S = mqk_ref.shape[0]
    nt = S // LANES
    L = MLSTM_CHUNK
    f32 = jnp.float32
    bf16 = jnp.bfloat16
    kscale_row = jnp.where(lax.broadcasted_iota(jnp.int32, (1, 2 * MLSTM_QK_WIDTH), 1) < MLSTM_QK_WIDTH,
                           1.0, MLSTM_QK_DIM ** -0.5)

    def conv_body(i, _):
        t0 = pl.multiple_of(i * CONV_TILE, CONV_TILE)
        cur = mqk_ref[pl.ds(t0, CONV_TILE), :].astype(f32)
        prev_start = pl.multiple_of(jnp.maximum(t0 - BF16_ROWS, 0), BF16_ROWS)
        prev = (mqk_ref[pl.ds(prev_start, BF16_ROWS), :].astype(f32)[BF16_ROWS - SUBLANES:, :]
                * jnp.where(i > 0, 1.0, 0.0))
        ext = jnp.concatenate([prev, cur], axis=0)
        y = convb_ref[...]
        for j in range(CONV_WIDTH):
            lo = SUBLANES - (CONV_WIDTH - 1) + j
            y = y + convw_ref[j:j + 1, :] * ext[lo:lo + CONV_TILE, :]
        y = y * jax.nn.sigmoid(y) * kscale_row
        qk_ref[pl.ds(t0, CONV_TILE), :] = y.astype(bf16)
        return 0

    lax.fori_loop(0, S // CONV_TILE, conv_body, 0)

    H = MLSTM_HEADS
    G8 = 2 * H
    n_rows = nt * G8
    a3 = gates_ref[:, I_ROW:I_ROW + G8, :] + gb_ref[...][None]
    is_f = lax.broadcasted_iota(jnp.int32, a3.shape, 1) >= H
    x = jnp.where(is_f, _log_sigmoid(a3), a3).reshape(n_rows, LANES)
    bcum = jnp.dot(x, tri_ref[...], precision=lax.Precision.HIGHEST, preferred_element_type=f32)
    b_rows = pltpu.roll(bcum, n_rows - H, 0)
    g_rows = x - b_rows
    pos = lax.broadcasted_iota(jnp.int32, (n_rows, LANES), 1) & (L - 1)

    def chunk_cummax(a):
        shift = 1
        while shift < L:
            a = jnp.where(pos >= shift, jnp.maximum(a, pltpu.roll(a, shift, 1)), a)
            shift *= 2
        return a

    def chunk_last(a):
        return jnp.dot(a, last_ref[...], precision=lax.Precision.HIGHEST, preferred_element_type=f32)

    bl_rows = chunk_last(b_rows)
    rows_ref[0] = g_rows
    rows_ref[1] = chunk_cummax(g_rows)
    rows_ref[2] = b_rows
    rows_ref[3] = bl_rows
    rows_ref[4] = chunk_last(chunk_cummax(bl_rows + g_rows))

    s_io = lax.broadcasted_iota(jnp.int32, (LANES, LANES), 0)
    t_io = lax.broadcasted_iota(jnp.int32, (LANES, LANES), 1)
    causal = (s_io <= t_io) & ((s_io >= L) == (t_io >= L))
    lane = lax.broadcasted_iota(jnp.int32, (1, LANES), 1)
    in_chunk = [lane < L, lane >= L]
    QD, VD = MLSTM_QK_DIM, MLSTM_V_DIM
    ones_aug = jnp.ones((C_ROWS - VD, LANES), bf16)
    head_lanes = [(lax.broadcasted_iota(jnp.int32, (LANES, LANES), 1) >= QD) == bool(par) for par in range(2)]

    def swap_halves(row):
        return pltpu.roll(row, L, 1)

    def tile_body(i, state):
        t0 = pl.multiple_of(i * LANES, LANES)
        r0 = pl.multiple_of(i * G8, G8)
        qk = qk_ref[pl.ds(t0, LANES), :]
        qk32 = qk.astype(f32)
        pairsT = [qk32[:, j * LANES:(j + 1) * LANES].T for j in range(4)]
        g8, cm8, b8, bl8, wm8 = [rows_ref[j, pl.ds(r0, G8), :] for j in range(5)]
        new_state = []
        for hh in range(H):
            pair, par = hh // 2, hh % 2
            g_r, cm_r, b_r, bl_r, wm_r = [a[hh:hh + 1, :] for a in (g8, cm8, b8, bl8, wm8)]
            caug, m_in = state[hh]
            qT = pairsT[pair][par * QD:(par + 1) * QD, :]
            kT = pairsT[2 + pair][par * QD:(par + 1) * QD, :]
            kpair = qk[:, (2 + pair) * LANES:(3 + pair) * LANES]
            qmask = jnp.where(head_lanes[par], qk[:, pair * LANES:(pair + 1) * LANES], jnp.zeros((), bf16))
            vaug = jnp.concatenate([mvT_ref[i, hh * VD:(hh + 1) * VD, :], ones_aug], axis=0)

            m_mid = swap_halves(jnp.maximum(bl_r + m_in, wm_r))
            m_prev = jnp.where(in_chunk[0], m_in, m_mid)
            m_next = jnp.maximum(bl_r + m_prev, wm_r)
            m_intra = b_r + cm_r
            m_inter = b_r + m_prev
            m_t = jnp.maximum(m_inter, m_intra)
            e_intra = jnp.exp(m_intra - m_t)
            e_inter = jnp.exp(m_inter - m_t)
            decay = jnp.exp(bl_r + m_prev - m_next)
            inject = jnp.exp(wm_r - m_next)

            g_mat = jnp.broadcast_to(g_r, (LANES, LANES)).T
            w = jnp.exp(jnp.where(causal, g_mat - cm_r, NEG_INF))
            st = _nt(kpair, qmask) * (w * e_intra)
            y = _mm(vaug, st.astype(bf16))
            kw = kT * jnp.exp(bl_r + g_r - wm_r)
            cs = caug
            for p in range(LANES // L):
                qs = jnp.where(in_chunk[p], qT * e_inter, 0.0).astype(bf16)
                y = y + _mm(cs.astype(bf16), qs)
                u = _nt(vaug, jnp.where(in_chunk[p], kw, 0.0).astype(bf16))
                dec = decay if p == 0 else swap_halves(decay)
                inj = inject if p == 0 else swap_halves(inject)
                cs = dec[:, 0:QD] * cs + inj[:, 0:QD] * u
            m_out = jnp.where(in_chunk[1], m_next, swap_halves(m_next))
            new_state.append((cs, m_out))

            den = y[VD:VD + 1, :]
            hT = y[0:VD, :] * (1.0 / jnp.maximum(jnp.abs(den), jnp.exp(-m_t)))
            sl = slice(hh * VD, (hh + 1) * VD)
            hn = hT * lax.rsqrt(jnp.mean(hT * hT, axis=0, keepdims=True) + NORM_EPS) * norm_ref[sl, :]
            yT = jax.nn.sigmoid(moT_ref[i, sl, :].astype(f32)) * hn
            out_ref[pl.ds(t0, LANES), sl] = yT.T.astype(out_ref.dtype)
        return tuple(new_state)

    init = tuple((jnp.zeros((C_ROWS, QD), f32), jnp.zeros((1, LANES), f32)) for _ in range(H))
    lax.fori_loop(0, nt, tile_body, init, unroll=8)


def _mlstm(mqk, mvT, moT, smallT, conv_w, conv_b, gate_bias, mlstm_norm, B, S):
    nt = S // LANES
    gb = jnp.broadcast_to(gate_bias.astype(jnp.float32).reshape(2 * MLSTM_HEADS, 1), (2 * MLSTM_HEADS, LANES))
    norm_cols = jnp.broadcast_to(mlstm_norm.astype(jnp.float32).reshape(MLSTM_WIDTH, 1), (MLSTM_WIDTH, LANES))
    lane = np.arange(LANES)
    same_chunk = lane[:, None] // MLSTM_CHUNK == lane[None, :] // MLSTM_CHUNK
    tri = (same_chunk & (lane[:, None] <= lane[None, :])).astype(np.float32)
    last = (same_chunk & (lane[:, None] % MLSTM_CHUNK == MLSTM_CHUNK - 1)).astype(np.float32)
    seq = lambda w: pl.BlockSpec((S, w), lambda b: (b, 0))
    tiles = lambda rows: pl.BlockSpec((nt, rows, LANES), lambda b: (b, 0, 0))
    return pl.pallas_call(
        _mlstm_kernel,
        grid=(B,),
        in_specs=[seq(2 * MLSTM_QK_WIDTH), tiles(MLSTM_WIDTH), tiles(MLSTM_WIDTH), tiles(N_GATE_ROWS),
                  _full((2 * MLSTM_HEADS, LANES)), _full((LANES, LANES)), _full((LANES, LANES)),
                  _full((CONV_WIDTH, 2 * MLSTM_QK_WIDTH)), _full((1, 2 * MLSTM_QK_WIDTH)),
                  _full((MLSTM_WIDTH, LANES))],
        out_specs=seq(MLSTM_WIDTH),
        out_shape=jax.ShapeDtypeStruct((B * S, MLSTM_WIDTH), jnp.bfloat16),
        scratch_shapes=[pltpu.VMEM((S, 2 * MLSTM_QK_WIDTH), jnp.bfloat16),
                        pltpu.VMEM((5, nt * 2 * MLSTM_HEADS, LANES), jnp.float32)],
        compiler_params=_params(1),
        name="mlstm",
    )(mqk, mvT, moT, smallT, gb, jnp.asarray(tri), jnp.asarray(last), conv_w, conv_b.reshape(1, -1), norm_cols)


def _mem_kv_kernel(mem_ref, g_ref, w_ref, k_ref, v_ref):
    mn = _rms(mem_ref[...], g_ref[...]).astype(jnp.bfloat16)
    k_ref[...] = _mm(mn, w_ref[:, :D_MODEL]).astype(k_ref.dtype)
    v_ref[...] = _mm(mn, w_ref[:, D_MODEL:]).astype(v_ref.dtype)


def _mem_kv(mem, gain, w_xkv):
    B, M, _ = mem.shape
    spec = pl.BlockSpec((None, M, D_MODEL), lambda b: (b, 0, 0))
    return pl.pallas_call(
        _mem_kv_kernel,
        grid=(B,),
        in_specs=[spec, _full((1, D_MODEL)), _full((D_MODEL, 2 * D_MODEL))],
        out_specs=[spec, spec],
        out_shape=[jax.ShapeDtypeStruct((B, M, D_MODEL), jnp.bfloat16)] * 2,
        compiler_params=_params(1),
        name="mem_kv",
    )(mem, gain, w_xkv.astype(jnp.bfloat16))


TM_X = 1024
X_HALVES = 2


def _mix_xattn_kernel(ynsa_ref, yml_ref, x_ref, wout_ref, gpost_ref, gpre_ref, wq_ref, k_ref, v_ref,
                      wo_ref, gpost2_ref, out_ref):
    bf16 = jnp.bfloat16
    halves = [slice(i * (TM_X // X_HALVES), (i + 1) * (TM_X // X_HALVES)) for i in range(X_HALVES)]
    y = [_mm(ynsa_ref[r, :], wout_ref[:NSA_WIDTH, :]) + _mm(yml_ref[r, :], wout_ref[NSA_WIDTH:, :]) for r in halves]
    x1 = [x_ref[r, :] + _rms(y[i], gpost_ref[...]) for i, r in enumerate(halves)]
    h2 = [_rms(x1[i], gpre_ref[...]).astype(bf16) for i in range(X_HALVES)]
    q = [(_mm(h2[i], wq_ref[...]) * (XATTN_HEAD_DIM ** -0.5 * LOG2E)).astype(bf16) for i in range(X_HALVES)]
    outs = [[] for _ in range(X_HALVES)]
    for hh in range(XATTN_HEADS):
        sl = slice(hh * XATTN_HEAD_DIM, (hh + 1) * XATTN_HEAD_DIM)
        s = [_nt(q[i][:, sl], k_ref[:, sl]) for i in range(X_HALVES)]
        for i in range(X_HALVES):
            p = jnp.exp2(s[i] - jnp.max(s[i], axis=1, keepdims=True))
            l = jnp.sum(p, axis=1, keepdims=True)
            outs[i].append((_mm(p.astype(bf16), v_ref[:, sl]) * (1.0 / l)).astype(bf16))
    y2 = [_mm(jnp.concatenate(outs[i], axis=1), wo_ref[...]) for i in range(X_HALVES)]
    for i, r in enumerate(halves):
        out_ref[r, :] = x1[i] + _rms(y2[i], gpost2_ref[...])


def _mix_xattn(ynsa, yml, x2d, w_out, g_post, g_pre, w_xq, kx, vx, w_xo, g_post2, B, S):
    nt = S // TM_X
    M = kx.shape[1]
    tok = lambda w: pl.BlockSpec((TM_X, w), lambda b, i: (b * nt + i, 0))
    mem_spec = pl.BlockSpec((None, M, D_MODEL), lambda b, i: (b, 0, 0))
    sq = _full((D_MODEL, D_MODEL))
    row = _full((1, D_MODEL))
    bf = lambda w: w.astype(jnp.bfloat16)
    return pl.pallas_call(
        _mix_xattn_kernel,
        grid=(B, nt),
        in_specs=[tok(NSA_WIDTH), tok(MLSTM_WIDTH), tok(D_MODEL), sq, row, row, sq, mem_spec, mem_spec,
                  sq, row],
        out_specs=tok(D_MODEL),
        out_shape=jax.ShapeDtypeStruct((B * S, D_MODEL), jnp.float32),
        compiler_params=_params(2),
        name="mix_xattn",
    )(ynsa, yml, x2d, bf(w_out), g_post, g_pre, bf(w_xq), kx, vx, bf(w_xo), g_post2)


TM_F = 1024
F_HALVES = 2


def _ffn_kernel(x_ref, gpre_ref, wgu_ref, wd_ref, gpost_ref, out_ref, acc_ref):
    bf16 = jnp.bfloat16
    n_chunks = D_FF // F_TILE
    half = TM_F // F_HALVES

    def pre(i):
        return _rms(x_ref[i * half:(i + 1) * half, :], gpre_ref[...]).astype(bf16)

    def chunk(i, h, j):
        rows = slice(i * half, (i + 1) * half)
        cols = slice(j * F_TILE, (j + 1) * F_TILE)
        g = _mm(h, wgu_ref[:, cols])
        u = _mm(h, wgu_ref[:, D_FF + j * F_TILE:D_FF + (j + 1) * F_TILE])
        act = (g * jax.nn.sigmoid(g) * u).astype(bf16)
        down = _mm(act, wd_ref[cols, :])
        if j == 0:
            acc_ref[rows, :] = down
        else:
            acc_ref[rows, :] += down

    def post(i):
        rows = slice(i * half, (i + 1) * half)
        out_ref[rows, :] = x_ref[rows, :] + _rms(acc_ref[rows, :], gpost_ref[...])

    h = pre(0)
    for i in range(F_HALVES):
        chunk(i, h, 0)
        if i > 0:
            post(i - 1)
        h_next = pre(i + 1) if i + 1 < F_HALVES else None
        for j in range(1, n_chunks):
            chunk(i, h, j)
        h = h_next
    post(F_HALVES - 1)


def _ffn(x2d, g_pre, w_gate_up, w_down, g_post):
    T = x2d.shape[0]
    tok = pl.BlockSpec((TM_F, D_MODEL), lambda i: (i, 0))
    row = _full((1, D_MODEL))
    return pl.pallas_call(
        _ffn_kernel,
        grid=(T // TM_F,),
        in_specs=[tok, row, _full((D_MODEL, 2 * D_FF), buffers=1), _full((D_FF, D_MODEL), buffers=1), row],
        out_specs=tok,
        out_shape=jax.ShapeDtypeStruct((T, D_MODEL), jnp.float32),
        scratch_shapes=[pltpu.VMEM((TM_F, D_MODEL), jnp.float32)],
        compiler_params=_params(1),
        name="ffn",
    )(x2d, g_pre, w_gate_up.astype(jnp.bfloat16), w_down.astype(jnp.bfloat16), g_post)


def _layer(x, mem, rel_bias, mix_norm_pre, w_in, cmp_pos_k, cmp_pos_v, cmp_w1_k, cmp_w2_k, cmp_w1_v,
           cmp_w2_v, conv_w, conv_b, mlstm_gate_bias, mlstm_norm, w_out, mix_norm_post, xattn_norm_pre,
           mem_norm, w_xq, w_xkv, w_xo, xattn_norm_post, ffn_norm_pre, w_gate_up, w_down, ffn_norm_post):
    B, S, _ = x.shape
    row = lambda g: g.reshape(1, -1).astype(jnp.float32)
    x2d = x.reshape(B * S, D_MODEL)
    w_tok, w_feat = _in_proj_weights(w_in)
    (kc, vc, ksl, kwn, mqk, qT, vslT, vwnT, mvT, moT, smallT) = _in_proj(
        x2d, row(mix_norm_pre), w_tok, w_feat)
    kcmp, vcmpT = _compress(kc, vc, cmp_pos_k, cmp_pos_v, cmp_w1_k, cmp_w2_k, cmp_w1_v, cmp_w2_v, B, S)
    tables = _bias_tables(rel_bias.astype(jnp.float32))
    ynsa = _nsa(qT, kcmp, vcmpT, ksl, vslT, kwn, vwnT, smallT, tables, B, S)
    yml = _mlstm(mqk, mvT, moT, smallT, conv_w, conv_b, mlstm_gate_bias, mlstm_norm, B, S)
    kx, vx = _mem_kv(mem, row(mem_norm), w_xkv)
    x2 = _mix_xattn(ynsa, yml, x2d, w_out, row(mix_norm_post), row(xattn_norm_pre), w_xq, kx, vx, w_xo,
                    row(xattn_norm_post), B, S)
    x3 = _ffn(x2, row(ffn_norm_pre), w_gate_up, w_down, row(ffn_norm_post))
    return x3.reshape(B, S, D_MODEL)


def kernel(x, mem, rel_bias, mix_norm_pre, w_in, cmp_pos_k, cmp_pos_v, cmp_w1_k, cmp_w2_k, cmp_w1_v, cmp_w2_v,
           conv_w, conv_b, mlstm_gate_bias, mlstm_norm, w_out, mix_norm_post, xattn_norm_pre, mem_norm, w_xq,
           w_xkv, w_xo, xattn_norm_post, ffn_norm_pre, w_gate_up, w_down, ffn_norm_post):
    depth = w_in.shape[0]
    for l in range(depth):
        x = _layer(x, mem, rel_bias, mix_norm_pre[l], w_in[l], cmp_pos_k[l], cmp_pos_v[l], cmp_w1_k[l],
                   cmp_w2_k[l], cmp_w1_v[l], cmp_w2_v[l], conv_w[l], conv_b[l], mlstm_gate_bias[l],
                   mlstm_norm[l], w_out[l], mix_norm_post[l], xattn_norm_pre[l], mem_norm[l], w_xq[l],
                   w_xkv[l], w_xo[l], xattn_norm_post[l], ffn_norm_pre[l], w_gate_up[l], w_down[l],
                   ffn_norm_post[l])
    return x
```
